```python
import math
import jax, jax.numpy as jnp
from jax import lax
import numpy as np

D_MODEL = 1024
BATCH = 8
SEQ = 4096
DEPTH = 4

N_MEM = 256
HEAD_DIM = 64
MIX_WIDTH = D_MODEL
MEM_HEADS = 4
MEM_WIDTH = MEM_HEADS * HEAD_DIM
MAIN_WIDTH = MIX_WIDTH - MEM_WIDTH
SB_HEADS = MAIN_WIDTH // HEAD_DIM
CONV_WIDTH = 3
D_FF = -(-8 * D_MODEL // (3 * 256)) * 256
N_A_LAYERS = DEPTH // 2
N_B_LAYERS = DEPTH - N_A_LAYERS
BLOCK_Q = 128
EPS = 1e-6

kernel_name = "shortconv_stickbreaking_yoco_hybrid"


def rmsnorm(x, g):
    xf = x.astype(jnp.float32)
    y = xf * lax.rsqrt(jnp.mean(xf * xf, axis=-1, keepdims=True) + EPS)
    return (y * g.astype(jnp.float32)).astype(x.dtype)


def causal_short_conv(u, w):
    c = u.shape[-1]
    return lax.conv_general_dilated(
        u, w[:, None, :].astype(u.dtype), window_strides=(1,),
        padding=[(CONV_WIDTH - 1, 0)],
        dimension_numbers=("NWC", "WIO", "NWC"),
        feature_group_count=c)


def memory_cross_attention(q, mem_k, mem_v):
    scale = 1.0 / math.sqrt(HEAD_DIM)
    s = jnp.einsum("bshd,bmhd->bhsm", q.astype(jnp.float32), mem_k.astype(jnp.float32)) * scale
    p = jax.nn.softmax(s, axis=-1)
    o = jnp.einsum("bhsm,bmhd->bshd", p, mem_v.astype(jnp.float32))
    return o.astype(q.dtype)


def stick_breaking_attention(q, k, v):
    b, s_len, h, d = q.shape
    scale = 1.0 / math.sqrt(d)
    qh = jnp.transpose(q, (0, 2, 1, 3)).astype(jnp.float32)
    kh = jnp.transpose(k, (0, 2, 1, 3)).astype(jnp.float32)
    vh = jnp.transpose(v, (0, 2, 1, 3)).astype(jnp.float32)
    outs = []
    for blk in range(s_len // BLOCK_Q):
        start = blk * BLOCK_Q
        end = start + BLOCK_Q
        qb = qh[:, :, start:end]
        kb = kh[:, :, :end]
        vb = vh[:, :, :end]
        z = jnp.einsum("bhtd,bhsd->bhts", qb, kb) * scale
        t_idx = start + jnp.arange(BLOCK_Q)[:, None]
        s_idx = jnp.arange(end)[None, :]
        causal = s_idx < t_idx
        log_not = jnp.where(causal, jax.nn.log_sigmoid(-z), 0.0)
        tail = lax.cumsum(log_not, axis=3, reverse=True) - log_not
        log_a = jax.nn.log_sigmoid(z) + tail
        a = jnp.where(causal, jnp.exp(log_a), 0.0)
        outs.append(jnp.einsum("bhts,bhsd->bhtd", a, vb))
    o = jnp.concatenate(outs, axis=2)
    return jnp.transpose(o, (0, 2, 1, 3)).astype(q.dtype)


def swiglu(h, w_gate, w_up, w_down):
    return (jax.nn.silu(h @ w_gate) * (h @ w_up)) @ w_down


def _fwd_setup_inputs(seed: int = 0) -> dict:
    key = jax.random.key(seed)
    ks = jax.random.split(key, 16)
    f32 = jnp.float32

    def nrm(k, shape, fan_in):
        return jax.random.normal(k, shape, f32) * (fan_in ** -0.5)

    def gain(k, shape):
        return jnp.ones(shape, f32) + 0.02 * jax.random.normal(k, shape, f32)

    x = jax.random.normal(ks[0], (BATCH, SEQ, D_MODEL), f32)
    mem = jax.random.normal(ks[1], (BATCH, N_MEM, D_MODEL), f32)
    return {
        "x": x,
        "mem": mem,
        "mix_norm": gain(ks[2], (DEPTH, D_MODEL)),
        "a_in": nrm(ks[3], (N_A_LAYERS, D_MODEL, 3 * MAIN_WIDTH + MEM_WIDTH), D_MODEL),
        "conv_w": nrm(ks[4], (N_A_LAYERS, CONV_WIDTH, MAIN_WIDTH), CONV_WIDTH),
        "b_in": nrm(ks[5], (N_B_LAYERS, D_MODEL, MAIN_WIDTH + MEM_WIDTH), D_MODEL),
        "kv_norm": gain(ks[6], (D_MODEL,)),
        "w_kv_shared": nrm(ks[7], (D_MODEL, 2 * MAIN_WIDTH), D_MODEL),
        "w_mem_kv": nrm(ks[8], (DEPTH, D_MODEL, 2 * MEM_WIDTH), D_MODEL),
        "w_o": nrm(ks[9], (DEPTH, MIX_WIDTH, D_MODEL), MIX_WIDTH),
        "ffn_norm": gain(ks[10], (DEPTH, D_MODEL)),
        "w_gate": nrm(ks[11], (DEPTH, D_MODEL, D_FF), D_MODEL),
        "w_up": nrm(ks[12], (DEPTH, D_MODEL, D_FF), D_MODEL),
        "w_down": nrm(ks[13], (DEPTH, D_FF, D_MODEL), D_FF),
        "mem_norm": gain(ks[14], (D_MODEL,)),
        "final_norm": gain(ks[15], (D_MODEL,)),
    }


def _fwd_reference(x, mem, mix_norm, a_in, conv_w, b_in, kv_norm, w_kv_shared, w_mem_kv,
              w_o, ffn_norm, w_gate, w_up, w_down, mem_norm, final_norm):
    b, s_len, _ = x.shape
    m_len = mem.shape[1]
    mem_n = rmsnorm(mem, mem_norm)
    k_sh = None
    v_sh = None
    for i in range(DEPTH):
        h = rmsnorm(x, mix_norm[i])
        mkv = (mem_n @ w_mem_kv[i]).reshape(b, m_len, 2, MEM_HEADS, HEAD_DIM)
        mem_k, mem_v = mkv[:, :, 0], mkv[:, :, 1]
        if i < N_A_LAYERS:
            p = h @ a_in[i]
            b_gate = p[..., :MAIN_WIDTH]
            c_gate = p[..., MAIN_WIDTH:2 * MAIN_WIDTH]
            u = p[..., 2 * MAIN_WIDTH:3 * MAIN_WIDTH]
            q_mem = p[..., 3 * MAIN_WIDTH:]
            y_main = b_gate * causal_short_conv(c_gate * u, conv_w[i])
        else:
            j = i - N_A_LAYERS
            p = h @ b_in[j]
            q_sb = p[..., :MAIN_WIDTH].reshape(b, s_len, SB_HEADS, HEAD_DIM)
            q_mem = p[..., MAIN_WIDTH:]
            y_main = stick_breaking_attention(q_sb, k_sh, v_sh).reshape(b, s_len, MAIN_WIDTH)
        y_mem = memory_cross_attention(
            q_mem.reshape(b, s_len, MEM_HEADS, HEAD_DIM), mem_k, mem_v
        ).reshape(b, s_len, MEM_WIDTH)
        x = x + jnp.concatenate([y_main, y_mem], axis=-1) @ w_o[i]
        x = x + swiglu(rmsnorm(x, ffn_norm[i]), w_gate[i], w_up[i], w_down[i])
        if i == N_A_LAYERS - 1:
            kv = (rmsnorm(x, kv_norm) @ w_kv_shared).reshape(b, s_len, 2, SB_HEADS, HEAD_DIM)
            k_sh, v_sh = kv[:, :, 0], kv[:, :, 1]
    return rmsnorm(x, final_norm)


import jax as _jax
import jax.numpy as _jnp

TWIN_FORMAT = 'train_step'
FWD_PARAMS = ['x', 'mem', 'mix_norm', 'a_in', 'conv_w', 'b_in', 'kv_norm', 'w_kv_shared', 'w_mem_kv', 'w_o', 'ffn_norm', 'w_gate', 'w_up', 'w_down', 'mem_norm', 'final_norm']
TWIN_WEIGHTS = ['mix_norm', 'a_in', 'conv_w', 'b_in', 'kv_norm', 'w_kv_shared', 'w_mem_kv', 'w_o', 'ffn_norm', 'w_gate', 'w_up', 'w_down', 'mem_norm', 'final_norm']
TWIN_DIFF_INPUT = 'x'
TWIN_INPUTS = ['x', 'mem', 'mix_norm', 'a_in', 'conv_w', 'b_in', 'kv_norm', 'w_kv_shared', 'w_mem_kv', 'w_o', 'ffn_norm', 'w_gate', 'w_up', 'w_down', 'mem_norm', 'final_norm', 'loss_target', 'm_mix_norm', 'm_a_in', 'm_conv_w', 'm_b_in', 'm_kv_norm', 'm_w_kv_shared', 'm_w_mem_kv', 'm_w_o', 'm_ffn_norm', 'm_w_gate', 'm_w_up', 'm_w_down', 'm_mem_norm', 'm_final_norm', 'v_mix_norm', 'v_a_in', 'v_conv_w', 'v_b_in', 'v_kv_norm', 'v_w_kv_shared', 'v_w_mem_kv', 'v_w_o', 'v_ffn_norm', 'v_w_gate', 'v_w_up', 'v_w_down', 'v_mem_norm', 'v_final_norm']
TWIN_OUTPUTS = ['loss', 'grad_x', 'grad_mix_norm', 'grad_a_in', 'grad_conv_w', 'grad_b_in', 'grad_kv_norm', 'grad_w_kv_shared', 'grad_w_mem_kv', 'grad_w_o', 'grad_ffn_norm', 'grad_w_gate', 'grad_w_up', 'grad_w_down', 'grad_mem_norm', 'grad_final_norm', 'delta_mix_norm', 'delta_a_in', 'delta_conv_w', 'delta_b_in', 'delta_kv_norm', 'delta_w_kv_shared', 'delta_w_mem_kv', 'delta_w_o', 'delta_ffn_norm', 'delta_w_gate', 'delta_w_up', 'delta_w_down', 'delta_mem_norm', 'delta_final_norm', 'new_m_mix_norm', 'new_m_a_in', 'new_m_conv_w', 'new_m_b_in', 'new_m_kv_norm', 'new_m_w_kv_shared', 'new_m_w_mem_kv', 'new_m_w_o', 'new_m_ffn_norm', 'new_m_w_gate', 'new_m_w_up', 'new_m_w_down', 'new_m_mem_norm', 'new_m_final_norm', 'new_v_mix_norm', 'new_v_a_in', 'new_v_conv_w', 'new_v_b_in', 'new_v_kv_norm', 'new_v_w_kv_shared', 'new_v_w_mem_kv', 'new_v_w_o', 'new_v_ffn_norm', 'new_v_w_gate', 'new_v_w_up', 'new_v_w_down', 'new_v_mem_norm', 'new_v_final_norm']
TWIN_LEAF_KINDS = {'loss': 'loss', 'grad_x': 'grad_x', 'grad_mix_norm': 'grad_w', 'grad_a_in': 'grad_w', 'grad_conv_w': 'grad_w', 'grad_b_in': 'grad_w', 'grad_kv_norm': 'grad_w', 'grad_w_kv_shared': 'grad_w', 'grad_w_mem_kv': 'grad_w', 'grad_w_o': 'grad_w', 'grad_ffn_norm': 'grad_w', 'grad_w_gate': 'grad_w', 'grad_w_up': 'grad_w', 'grad_w_down': 'grad_w', 'grad_mem_norm': 'grad_w', 'grad_final_norm': 'grad_w', 'delta_mix_norm': 'delta_w', 'delta_a_in': 'delta_w', 'delta_conv_w': 'delta_w', 'delta_b_in': 'delta_w', 'delta_kv_norm': 'delta_w', 'delta_w_kv_shared': 'delta_w', 'delta_w_mem_kv': 'delta_w', 'delta_w_o': 'delta_w', 'delta_ffn_norm': 'delta_w', 'delta_w_gate': 'delta_w', 'delta_w_up': 'delta_w', 'delta_w_down': 'delta_w', 'delta_mem_norm': 'delta_w', 'delta_final_norm': 'delta_w', 'new_m_mix_norm': 'new_m', 'new_m_a_in': 'new_m', 'new_m_conv_w': 'new_m', 'new_m_b_in': 'new_m', 'new_m_kv_norm': 'new_m', 'new_m_w_kv_shared': 'new_m', 'new_m_w_mem_kv': 'new_m', 'new_m_w_o': 'new_m', 'new_m_ffn_norm': 'new_m', 'new_m_w_gate': 'new_m', 'new_m_w_up': 'new_m', 'new_m_w_down': 'new_m', 'new_m_mem_norm': 'new_m', 'new_m_final_norm': 'new_m', 'new_v_mix_norm': 'new_v', 'new_v_a_in': 'new_v', 'new_v_conv_w': 'new_v', 'new_v_b_in': 'new_v', 'new_v_kv_norm': 'new_v', 'new_v_w_kv_shared': 'new_v', 'new_v_w_mem_kv': 'new_v', 'new_v_w_o': 'new_v', 'new_v_ffn_norm': 'new_v', 'new_v_w_gate': 'new_v', 'new_v_w_up': 'new_v', 'new_v_w_down': 'new_v', 'new_v_mem_norm': 'new_v', 'new_v_final_norm': 'new_v'}


def _forward(args):
    return _fwd_reference(*[args[k] for k in FWD_PARAMS])


def _output_shape():
    def fwd():
        inp = _fwd_setup_inputs(0)
        return _fwd_reference(*[inp[k] for k in FWD_PARAMS])
    out = _jax.eval_shape(fwd)
    return out.shape, out.dtype

N_MICROBATCH = 1
ADAM_LR = 0.001
ADAM_B1 = 0.9
ADAM_B2 = 0.999
ADAM_EPS = 1e-08
ADAM_WD = 0.01
ADAM_STEP = 10
PER_EXAMPLE_BATCH_AXIS = {'x': 0, 'mem': 0, 'loss_target': 0}
SHARED_INPUTS = []
_WEIGHT_DTYPES = {'mix_norm': _jnp.float32, 'a_in': _jnp.float32, 'conv_w': _jnp.float32, 'b_in': _jnp.float32, 'kv_norm': _jnp.float32, 'w_kv_shared': _jnp.float32, 'w_mem_kv': _jnp.float32, 'w_o': _jnp.float32, 'ffn_norm': _jnp.float32, 'w_gate': _jnp.float32, 'w_up': _jnp.float32, 'w_down': _jnp.float32, 'mem_norm': _jnp.float32, 'final_norm': _jnp.float32}
MOMENT_SCALE = {'mix_norm': 1.868960e-01, 'a_in': 1.660632e-01, 'conv_w': 1.778560e-01, 'b_in': 2.517671e-02, 'kv_norm': 8.561779e-02, 'w_kv_shared': 6.748052e-02, 'w_mem_kv': 1.519302e-02, 'w_o': 1.133347e-01, 'ffn_norm': 1.100314e-01, 'w_gate': 4.564622e-02, 'w_up': 4.421302e-02, 'w_down': 7.326268e-02, 'mem_norm': 2.280073e-02, 'final_norm': 3.199327e+01}


def _to_microbatches(a, axis):
    t = _jnp.moveaxis(a, axis, 0)
    t = t.reshape((N_MICROBATCH, t.shape[0] // N_MICROBATCH) + t.shape[1:])
    return _jnp.moveaxis(t, 1, axis + 1)


def setup_inputs(seed: int = 0) -> dict:
    inp = _fwd_setup_inputs(seed)
    key = _jax.random.fold_in(_jax.random.key(seed), 7919)
    shape, _ = _output_shape()
    out = dict(inp)
    out["loss_target"] = _jax.random.normal(_jax.random.fold_in(key, 0), shape, _jnp.float32)
    for i, name in enumerate(TWIN_WEIGHTS):
        w = inp[name].astype(_jnp.float32)
        if MOMENT_SCALE is None:
            s = _jnp.sqrt(_jnp.mean(_jnp.square(w)) + 1e-30)
        else:
            s = MOMENT_SCALE[name]
        km, kv = _jax.random.split(_jax.random.fold_in(key, i + 1))
        out[name] = w
        out["m_" + name] = s * _jax.random.normal(km, w.shape, _jnp.float32)
        out["v_" + name] = (s * s) * _jax.random.uniform(kv, w.shape, _jnp.float32, 0.5, 1.5)
    if N_MICROBATCH > 1:
        for name, axis in PER_EXAMPLE_BATCH_AXIS.items():
            out[name] = _to_microbatches(out[name], axis)
    return {'x': out['x'], 'mem': out['mem'], 'mix_norm': out['mix_norm'], 'a_in': out['a_in'], 'conv_w': out['conv_w'], 'b_in': out['b_in'], 'kv_norm': out['kv_norm'], 'w_kv_shared': out['w_kv_shared'], 'w_mem_kv': out['w_mem_kv'], 'w_o': out['w_o'], 'ffn_norm': out['ffn_norm'], 'w_gate': out['w_gate'], 'w_up': out['w_up'], 'w_down': out['w_down'], 'mem_norm': out['mem_norm'], 'final_norm': out['final_norm'], 'loss_target': out['loss_target'], 'm_mix_norm': out['m_mix_norm'], 'm_a_in': out['m_a_in'], 'm_conv_w': out['m_conv_w'], 'm_b_in': out['m_b_in'], 'm_kv_norm': out['m_kv_norm'], 'm_w_kv_shared': out['m_w_kv_shared'], 'm_w_mem_kv': out['m_w_mem_kv'], 'm_w_o': out['m_w_o'], 'm_ffn_norm': out['m_ffn_norm'], 'm_w_gate': out['m_w_gate'], 'm_w_up': out['m_w_up'], 'm_w_down': out['m_w_down'], 'm_mem_norm': out['m_mem_norm'], 'm_final_norm': out['m_final_norm'], 'v_mix_norm': out['v_mix_norm'], 'v_a_in': out['v_a_in'], 'v_conv_w': out['v_conv_w'], 'v_b_in': out['v_b_in'], 'v_kv_norm': out['v_kv_norm'], 'v_w_kv_shared': out['v_w_kv_shared'], 'v_w_mem_kv': out['v_w_mem_kv'], 'v_w_o': out['v_w_o'], 'v_ffn_norm': out['v_ffn_norm'], 'v_w_gate': out['v_w_gate'], 'v_w_up': out['v_w_up'], 'v_w_down': out['v_w_down'], 'v_mem_norm': out['v_mem_norm'], 'v_final_norm': out['v_final_norm']}


def _loss(weights, diff, rest, loss_target):
    with _jax.named_scope("forward"):
        args = {**rest, TWIN_DIFF_INPUT: diff, **{k: w.astype(_WEIGHT_DTYPES[k]) for k, w in weights.items()}}
        y = _forward(args)
    with _jax.named_scope("loss_head"):
        err = _jnp.square(y.astype(_jnp.float32) - loss_target)
        return 0.5 * _jnp.sum(_jnp.mean(err, axis=-1)) if err.ndim else 0.5 * err


def _adamw(w, g, m, v):
    m = ADAM_B1 * m + (1.0 - ADAM_B1) * g
    v = ADAM_B2 * v + (1.0 - ADAM_B2) * _jnp.square(g)
    m_hat = m / (1.0 - ADAM_B1 ** ADAM_STEP)
    v_hat = v / (1.0 - ADAM_B2 ** ADAM_STEP)
    delta = -ADAM_LR * (m_hat / (_jnp.sqrt(v_hat) + ADAM_EPS) + ADAM_WD * w)
    return delta, m, v


def reference(x, mem, mix_norm, a_in, conv_w, b_in, kv_norm, w_kv_shared, w_mem_kv, w_o, ffn_norm, w_gate, w_up, w_down, mem_norm, final_norm, loss_target, m_mix_norm, m_a_in, m_conv_w, m_b_in, m_kv_norm, m_w_kv_shared, m_w_mem_kv, m_w_o, m_ffn_norm, m_w_gate, m_w_up, m_w_down, m_mem_norm, m_final_norm, v_mix_norm, v_a_in, v_conv_w, v_b_in, v_kv_norm, v_w_kv_shared, v_w_mem_kv, v_w_o, v_ffn_norm, v_w_gate, v_w_up, v_w_down, v_mem_norm, v_final_norm):
    given = dict(x=x, mem=mem, mix_norm=mix_norm, a_in=a_in, conv_w=conv_w, b_in=b_in, kv_norm=kv_norm, w_kv_shared=w_kv_shared, w_mem_kv=w_mem_kv, w_o=w_o, ffn_norm=ffn_norm, w_gate=w_gate, w_up=w_up, w_down=w_down, mem_norm=mem_norm, final_norm=final_norm, loss_target=loss_target, m_mix_norm=m_mix_norm, m_a_in=m_a_in, m_conv_w=m_conv_w, m_b_in=m_b_in, m_kv_norm=m_kv_norm, m_w_kv_shared=m_w_kv_shared, m_w_mem_kv=m_w_mem_kv, m_w_o=m_w_o, m_ffn_norm=m_ffn_norm, m_w_gate=m_w_gate, m_w_up=m_w_up, m_w_down=m_w_down, m_mem_norm=m_mem_norm, m_final_norm=m_final_norm, v_mix_norm=v_mix_norm, v_a_in=v_a_in, v_conv_w=v_conv_w, v_b_in=v_b_in, v_kv_norm=v_kv_norm, v_w_kv_shared=v_w_kv_shared, v_w_mem_kv=v_w_mem_kv, v_w_o=v_w_o, v_ffn_norm=v_ffn_norm, v_w_gate=v_w_gate, v_w_up=v_w_up, v_w_down=v_w_down, v_mem_norm=v_mem_norm, v_final_norm=v_final_norm)
    weights = {n: given[n] for n in TWIN_WEIGHTS}
    shared = {n: given[n] for n in SHARED_INPUTS}
    per_example = {n: given[n] for n in ['x', 'mem']}
    grad_fn = _jax.value_and_grad(_loss, argnums=(0, 1))

    def one_microbatch(ex, loss_target):
        ex = dict(ex)
        diff = ex.pop(TWIN_DIFF_INPUT)
        return grad_fn(weights, diff, {**shared, **ex}, loss_target)

    if N_MICROBATCH == 1:
        loss, (grad_w, grad_x) = one_microbatch(per_example, given["loss_target"])
    else:
        def body(carry, xs):
            loss_sum, grad_sum = carry
            l_k, (gw_k, gx_k) = one_microbatch(xs[0], xs[1])
            with _jax.named_scope("update"):
                return (loss_sum + l_k, _jax.tree.map(_jnp.add, grad_sum, gw_k)), gx_k

        init = (_jnp.zeros((), _jnp.float32), _jax.tree.map(_jnp.zeros_like, weights))
        (loss, grad_w), grad_x = _jax.lax.scan(body, init, (per_example, given["loss_target"]))
    with _jax.named_scope("update"):
        delta_w, new_m, new_v = {}, {}, {}
        for n in TWIN_WEIGHTS:
            delta_w[n], new_m[n], new_v[n] = _adamw(weights[n], grad_w[n], given["m_" + n], given["v_" + n])
    return (loss, grad_x, *[grad_w[n] for n in TWIN_WEIGHTS], *[delta_w[n] for n in TWIN_WEIGHTS],
            *[new_m[n] for n in TWIN_WEIGHTS], *[new_v[n] for n in TWIN_WEIGHTS])
```

```python
import functools
import math

import jax
import jax.numpy as jnp
from jax import lax
from jax.experimental import pallas as pl
from jax.experimental.pallas import tpu as pltpu

F32 = jnp.float32
BF16 = jnp.bfloat16
MESH = pl.DeviceIdType.MESH

HEAD_DIM = 64
MEM_HEADS = 4
MEM_WIDTH = MEM_HEADS * HEAD_DIM
EPS = 1e-6
LANES = 128
BF16_ROWS = 16
VMEM_LIMIT = 56 * 1024 * 1024
N_DEV = 8

ADAM_LR = 0.001
ADAM_B1 = 0.9
ADAM_B2 = 0.999
ADAM_EPS = 1e-08
ADAM_WD = 0.01
ADAM_STEP = 10

ANY = pl.BlockSpec(memory_space=pl.ANY)


def _cparams(sem=None):
    return pltpu.CompilerParams(dimension_semantics=sem, vmem_limit_bytes=VMEM_LIMIT)


def _pick(n, cands):
    for c in cands:
        if n % c == 0:
            return c
    raise ValueError(f"no tile for {n} in {cands}")


def _mm(a, b, form, out_dtype, *, name, b_idx=None, residual=None, into=None, into_idx=None):
    if form == "tn":
        K, M = a.shape
    else:
        M, K = a.shape
    if form == "nt":
        N, K2 = b.shape[-2:]
    else:
        K2, N = b.shape[-2:]
    assert K == K2, (name, a.shape, b.shape)
    tm = _pick(M, (512, 256, 128))
    tn = _pick(N, (1408, 1280, 1024, 768, 512, 256, 128))
    tk = _pick(K, (1024, 1408, 1280, 768, 512, 256))
    nk = K // tk
    dims = {"nn": (((1,), (0,)), ((), ())), "nt": (((1,), (1,)), ((), ())), "tn": (((0,), (0,)), ((), ()))}[form]

    a_spec = pl.BlockSpec((tk, tm), lambda i, j, k: (k, i)) if form == "tn" else pl.BlockSpec((tm, tk), lambda i, j, k: (i, k))
    b_blk = (tn, tk) if form == "nt" else (tk, tn)
    b_map2 = (lambda i, j, k: (j, k)) if form == "nt" else (lambda i, j, k: (k, j))
    if b.ndim == 3:
        b_spec = pl.BlockSpec((None,) + b_blk, lambda i, j, k: (b_idx,) + b_map2(i, j, k))
    else:
        b_spec = pl.BlockSpec(b_blk, b_map2)
    operands, in_specs = [a, b], [a_spec, b_spec]
    if residual is not None:
        operands.append(residual)
        in_specs.append(pl.BlockSpec((tm, tn), lambda i, j, k: (i, j)))
    aliases = {}
    if into is not None:
        aliases = {len(operands): 0}
        operands.append(into)
        in_specs.append(ANY)
        out_shape = jax.ShapeDtypeStruct(into.shape, into.dtype)
        out_spec = pl.BlockSpec((None, tm, tn), lambda i, j, k: (into_idx, i, j))
        out_dtype = into.dtype
    else:
        out_shape = jax.ShapeDtypeStruct((M, N), out_dtype)
        out_spec = pl.BlockSpec((tm, tn), lambda i, j, k: (i, j))
    has_res, has_into = residual is not None, into is not None

    def body(*refs):
        a_ref, b_ref = refs[0], refs[1]
        r_ref = refs[2] if has_res else None
        o_ref = refs[2 + int(has_res) + int(has_into)]
        acc_ref = refs[-1]
        part = lax.dot_general(a_ref[...].astype(BF16), b_ref[...].astype(BF16), dims, preferred_element_type=F32)

        def finish(total):
            if has_res:
                total = total + r_ref[...].astype(F32)
            o_ref[...] = total.astype(out_dtype)

        if nk == 1:
            finish(part)
        else:
            k = pl.program_id(2)

            @pl.when(k == 0)
            def _():
                acc_ref[...] = part

            @pl.when(jnp.logical_and(k > 0, k < nk - 1))
            def _():
                acc_ref[...] += part

            @pl.when(k == nk - 1)
            def _():
                finish(acc_ref[...] + part)

    return pl.pallas_call(
        body, name=name, out_shape=out_shape, grid=(M // tm, N // tn, nk), in_specs=in_specs, out_specs=out_spec,
        scratch_shapes=[pltpu.VMEM((tm, tn), F32)], input_output_aliases=aliases,
        compiler_params=_cparams(("parallel", "parallel", "arbitrary")),
    )(*operands)


def _rmsnorm(x, g, *, name):
    R, D = x.shape
    tr = _pick(R, (512, 256))

    def body(x_ref, g_ref, o_ref):
        xv = x_ref[...]
        r = lax.rsqrt(jnp.mean(xv * xv, axis=-1, keepdims=True) + EPS)
        o_ref[...] = (xv * r * g_ref[...]).astype(BF16)

    return pl.pallas_call(
        body, name=name, out_shape=jax.ShapeDtypeStruct((R, D), BF16), grid=(R // tr,),
        in_specs=[pl.BlockSpec((tr, D), lambda i: (i, 0)), pl.BlockSpec((1, D), lambda i: (0, 0))],
        out_specs=pl.BlockSpec((tr, D), lambda i: (i, 0)), compiler_params=_cparams(("parallel",)),
    )(x, g)


def _rmsnorm_bwd(x, g, dh, dx_in, *, name):
    R, D = x.shape
    tr = _pick(R, (512, 256))
    has_in = dx_in is not None

    def body(*refs):
        x_ref, g_ref, dh_ref = refs[:3]
        dxi_ref = refs[3] if has_in else None
        dx_ref, dg_ref = refs[3 + int(has_in):]
        xv = x_ref[...]
        r = lax.rsqrt(jnp.mean(xv * xv, axis=-1, keepdims=True) + EPS)
        xhat = xv * r
        dhv = dh_ref[...].astype(F32)
        dxh = dhv * g_ref[...]
        dx = r * (dxh - xhat * jnp.mean(dxh * xhat, axis=-1, keepdims=True))
        if has_in:
            dx = dx + dxi_ref[...]
        dx_ref[...] = dx
        part = jnp.sum(dhv * xhat, axis=0, keepdims=True)

        @pl.when(pl.program_id(0) == 0)
        def _():
            dg_ref[...] = part

        @pl.when(pl.program_id(0) > 0)
        def _():
            dg_ref[...] += part

    row = pl.BlockSpec((tr, D), lambda i: (i, 0))
    vec = pl.BlockSpec((1, D), lambda i: (0, 0))
    ops = [x, g, dh] + ([dx_in] if has_in else [])
    return pl.pallas_call(
        body, name=name, out_shape=(jax.ShapeDtypeStruct((R, D), F32), jax.ShapeDtypeStruct((1, D), F32)),
        grid=(R // tr,), in_specs=[row, vec, row] + ([row] if has_in else []), out_specs=(row, vec),
        compiler_params=_cparams(("arbitrary",)),
    )(*ops)


def _loss_head(x, g, target, *, name):
    R, D = x.shape
    tr = _pick(R, (512, 256))

    def body(x_ref, g_ref, t_ref, dx_ref, dg_ref, loss_ref):
        xv = x_ref[...]
        gv = g_ref[...]
        r = lax.rsqrt(jnp.mean(xv * xv, axis=-1, keepdims=True) + EPS)
        xhat = xv * r
        err = xhat * gv - t_ref[...]
        loss = 0.5 * jnp.sum(jnp.mean(err * err, axis=-1, keepdims=True), axis=0, keepdims=True)
        dy = err * (1.0 / D)
        dxh = dy * gv
        dx_ref[...] = r * (dxh - xhat * jnp.mean(dxh * xhat, axis=-1, keepdims=True))
        dg = jnp.sum(dy * xhat, axis=0, keepdims=True)
        lossv = jnp.broadcast_to(loss, (1, LANES))

        @pl.when(pl.program_id(0) == 0)
        def _():
            dg_ref[...] = dg
            loss_ref[...] = lossv

        @pl.when(pl.program_id(0) > 0)
        def _():
            dg_ref[...] += dg
            loss_ref[...] += lossv

    row = pl.BlockSpec((tr, D), lambda i: (i, 0))
    vec = pl.BlockSpec((1, D), lambda i: (0, 0))
    return pl.pallas_call(
        body, name=name,
        out_shape=(jax.ShapeDtypeStruct((R, D), F32), jax.ShapeDtypeStruct((1, D), F32), jax.ShapeDtypeStruct((1, LANES), F32)),
        grid=(R // tr,), in_specs=[row, vec, row], out_specs=(row, vec, pl.BlockSpec((1, LANES), lambda i: (0, 0))),
        compiler_params=_cparams(("arbitrary",)),
    )(x, g, target)


def _conv_taps(gv, S):
    t = lax.broadcasted_iota(jnp.int32, gv.shape, 0)
    g1 = jnp.where(t >= 1, pltpu.roll(gv, 1, 0), 0.0)
    g2 = jnp.where(t >= 2, pltpu.roll(gv, 2, 0), 0.0)
    return g1, g2


def _conv_fwd(p, w, main, *, name):
    S = p.shape[0]
    tc = LANES
    nb = main // tc

    def body(b_ref, c_ref, u_ref, w_ref, y_ref):
        gv = c_ref[...] * u_ref[...]
        g1, g2 = _conv_taps(gv, S)
        cv = w_ref[0:1, :] * g2 + w_ref[1:2, :] * g1 + w_ref[2:3, :] * gv
        y_ref[...] = (b_ref[...] * cv).astype(BF16)

    col = lambda off: pl.BlockSpec((S, tc), lambda j: (0, off + j))
    return pl.pallas_call(
        body, name=name, out_shape=jax.ShapeDtypeStruct((S, main), BF16), grid=(nb,),
        in_specs=[col(0), col(nb), col(2 * nb), pl.BlockSpec((3, tc), lambda j: (0, j))],
        out_specs=pl.BlockSpec((S, tc), lambda j: (0, j)), compiler_params=_cparams(("parallel",)),
    )(p, p, p, w)


def _conv_bwd(p, w, dy, main, *, name):
    S = p.shape[0]
    tc = LANES
    nb = main // tc

    def body(b_ref, c_ref, u_ref, w_ref, dy_ref, db_ref, dc_ref, du_ref, dw_ref):
        cvv, uv = c_ref[...], u_ref[...]
        gv = cvv * uv
        g1, g2 = _conv_taps(gv, S)
        w0, w1, w2 = w_ref[0:1, :], w_ref[1:2, :], w_ref[2:3, :]
        dyv = dy_ref[...].astype(F32)
        db_ref[...] = (dyv * (w0 * g2 + w1 * g1 + w2 * gv)).astype(BF16)
        dcv = dyv * b_ref[...]
        t = lax.broadcasted_iota(jnp.int32, dcv.shape, 0)
        n1 = jnp.where(t <= S - 2, pltpu.roll(dcv, S - 1, 0), 0.0)
        n2 = jnp.where(t <= S - 3, pltpu.roll(dcv, S - 2, 0), 0.0)
        dg = w2 * dcv + w1 * n1 + w0 * n2
        dc_ref[...] = (dg * uv).astype(BF16)
        du_ref[...] = (dg * cvv).astype(BF16)
        dw_ref[0:1, :] = jnp.sum(dcv * g2, axis=0, keepdims=True)
        dw_ref[1:2, :] = jnp.sum(dcv * g1, axis=0, keepdims=True)
        dw_ref[2:3, :] = jnp.sum(dcv * gv, axis=0, keepdims=True)

    col = lambda off: pl.BlockSpec((S, tc), lambda j: (0, off + j))
    out = jax.ShapeDtypeStruct((S, main), BF16)
    return pl.pallas_call(
        body, name=name, out_shape=(out, out, out, jax.ShapeDtypeStruct((3, main), F32)), grid=(nb,),
        in_specs=[col(0), col(nb), col(2 * nb), pl.BlockSpec((3, tc), lambda j: (0, j)), col(0)],
        out_specs=(col(0), col(0), col(0), pl.BlockSpec((3, tc), lambda j: (0, j))),
        compiler_params=_cparams(("parallel",)),
    )(p, p, p, w, dy)


def _head_mask(width, h):
    lane = lax.broadcasted_iota(jnp.int32, (1, width), 1)
    return jnp.logical_and(lane >= h * HEAD_DIM, lane < (h + 1) * HEAD_DIM)


_NT = (((1,), (1,)), ((), ()))
_NN = (((1,), (0,)), ((), ()))
_TN = (((0,), (0,)), ((), ()))


def _dot(a, b, dims):
    return lax.dot_general(a, b, dims, preferred_element_type=F32)


def _mem_probs(qh, kv):
    s = _dot(qh, kv, _NT) * (1.0 / math.sqrt(HEAD_DIM))
    s = s - jnp.max(s, axis=-1, keepdims=True)
    e = jnp.exp(s)
    return e / jnp.sum(e, axis=-1, keepdims=True)


def _memattn_fwd(p, qblk, mkv, *, name):
    S = p.shape[0]
    M = mkv.shape[0]
    W = MEM_WIDTH
    tq = _pick(S, (512, 256))

    def body(q_ref, k_ref, v_ref, o_ref):
        q = q_ref[...].astype(BF16)
        kv, vv = k_ref[...], v_ref[...]
        out = jnp.zeros((tq, W), F32)
        for h in range(MEM_HEADS):
            m = _head_mask(W, h)
            pr = _mem_probs(jnp.where(m, q, jnp.zeros_like(q)), kv)
            out = jnp.where(m, _dot(pr.astype(BF16), vv, _NN), out)
        o_ref[...] = out.astype(BF16)

    return pl.pallas_call(
        body, name=name, out_shape=jax.ShapeDtypeStruct((S, W), BF16), grid=(S // tq,),
        in_specs=[pl.BlockSpec((tq, W), lambda i: (i, qblk)), pl.BlockSpec((M, W), lambda i: (0, 0)),
                  pl.BlockSpec((M, W), lambda i: (0, 1))],
        out_specs=pl.BlockSpec((tq, W), lambda i: (i, 0)), compiler_params=_cparams(("parallel",)),
    )(p, mkv, mkv)


def _memattn_bwd(p, qblk, mkv, dy, dyblk, *, name):
    S = p.shape[0]
    M = mkv.shape[0]
    W = MEM_WIDTH
    tq = _pick(S, (512, 256))
    scale = 1.0 / math.sqrt(HEAD_DIM)

    def body(q_ref, k_ref, v_ref, do_ref, dq_ref, dk_ref, dv_ref, dk_acc, dv_acc):
        q = q_ref[...].astype(BF16)
        do = do_ref[...].astype(BF16)
        kv, vv = k_ref[...], v_ref[...]
        dq = jnp.zeros((tq, W), F32)
        dk = jnp.zeros((M, W), F32)
        dv = jnp.zeros((M, W), F32)
        for h in range(MEM_HEADS):
            m = _head_mask(W, h)
            qh = jnp.where(m, q, jnp.zeros_like(q))
            doh = jnp.where(m, do, jnp.zeros_like(do))
            pr = _mem_probs(qh, kv)
            dpr = _dot(doh, vv, _NT)
            ds = (pr * (dpr - jnp.sum(dpr * pr, axis=-1, keepdims=True)) * scale).astype(BF16)
            dq = jnp.where(m, _dot(ds, kv, _NN), dq)
            dk = dk + _dot(ds, qh, _TN)
            dv = dv + _dot(pr.astype(BF16), doh, _TN)
        dq_ref[...] = dq.astype(BF16)
        i = pl.program_id(0)

        @pl.when(i == 0)
        def _():
            dk_acc[...] = dk
            dv_acc[...] = dv

        @pl.when(i > 0)
        def _():
            dk_acc[...] += dk
            dv_acc[...] += dv

        @pl.when(i == S // tq - 1)
        def _():
            dk_ref[...] = dk_acc[...].astype(BF16)
            dv_ref[...] = dv_acc[...].astype(BF16)

    kspec = lambda c: pl.BlockSpec((M, W), lambda i: (0, c))
    return pl.pallas_call(
        body, name=name,
        out_shape=(jax.ShapeDtypeStruct((S, W), BF16), jax.ShapeDtypeStruct((M, W), BF16), jax.ShapeDtypeStruct((M, W), BF16)),
        grid=(S // tq,),
        in_specs=[pl.BlockSpec((tq, W), lambda i: (i, qblk)), kspec(0), kspec(1), pl.BlockSpec((tq, W), lambda i: (i, dyblk))],
        out_specs=(pl.BlockSpec((tq, W), lambda i: (i, 0)), kspec(0), kspec(0)),
        scratch_shapes=[pltpu.VMEM((M, W), F32), pltpu.VMEM((M, W), F32)],
        compiler_params=_cparams(("arbitrary",)),
    )(p, mkv, mkv, dy)


SB_TQ = 256


def _split_dot(v, tri):
    hi = v.astype(BF16)
    lo = (v - hi.astype(F32)).astype(BF16)
    return _dot(hi, tri, _NN) + _dot(lo, tri, _NN)


def _sb_scores(qh, kb, causal):
    z = _dot(qh, kb, _NT) * (1.0 / math.sqrt(HEAD_DIM))
    e = jnp.exp(-jnp.abs(z))
    sp = jnp.maximum(z, 0.0) + jnp.log(1.0 + e)
    ln = -sp
    if causal is not None:
        ln = jnp.where(causal, ln, 0.0)
    return z, ln, sp, e


def _sb_fwd(p, kv, heads, *, name):
    S = p.shape[0]
    tq = SB_TQ
    npair = heads // 2

    def body(q_ref, k_ref, v_ref, o_ref, lt_ref):
        qi = pl.program_id(1)
        r = lax.broadcasted_iota(jnp.int32, (tq, tq), 0)
        c = lax.broadcasted_iota(jnp.int32, (tq, tq), 1)
        tri = (r > c).astype(BF16)
        causal = c < r
        q = q_ref[...]
        out = jnp.zeros((tq, LANES), F32)
        tot = jnp.zeros((tq, LANES), F32)
        for h in range(2):
            m = _head_mask(LANES, h)
            qh = jnp.where(m, q, jnp.zeros_like(q))

            def block(j, carry, acc, mask):
                off = pl.multiple_of(j * tq, tq)
                kb = k_ref[pl.ds(off, tq), :]
                vb = v_ref[pl.ds(off, tq), :]
                z, ln, sp, _ = _sb_scores(qh, kb, mask)
                a = jnp.exp(z - sp + _split_dot(ln, tri) + carry)
                if mask is not None:
                    a = jnp.where(mask, a, 0.0)
                acc = acc + _dot(a.astype(BF16), vb, _NN)
                return carry + jnp.sum(ln, axis=1, keepdims=True), acc

            carry, acc = block(qi, jnp.zeros((tq, 1), F32), jnp.zeros((tq, LANES), F32), causal)
            carry, acc = lax.fori_loop(0, qi, lambda it, ca: block(qi - 1 - it, ca[0], ca[1], None), (carry, acc))
            out = jnp.where(m, acc, out)
            tot = jnp.where(m, carry, tot)
        o_ref[...] = out.astype(BF16)
        lt_ref[...] = tot

    W = heads * HEAD_DIM
    qspec = pl.BlockSpec((tq, LANES), lambda hp, i: (i, hp))
    return pl.pallas_call(
        body, name=name, out_shape=(jax.ShapeDtypeStruct((S, W), BF16), jax.ShapeDtypeStruct((S, W), F32)), grid=(npair, S // tq),
        in_specs=[qspec, pl.BlockSpec((S, LANES), lambda hp, i: (0, hp)), pl.BlockSpec((S, LANES), lambda hp, i: (0, npair + hp))],
        out_specs=(qspec, qspec), compiler_params=_cparams(("parallel", "arbitrary")),
    )(p, kv, kv)


def _sb_bwd(p, kv, lt, dy, heads, dk_in, dv_in, *, name):
    S = p.shape[0]
    tq = SB_TQ
    npair = heads // 2
    has_in = dk_in is not None
    scale = 1.0 / math.sqrt(HEAD_DIM)

    def body(*refs):
        q_ref, k_ref, v_ref, lt_ref, do_ref = refs[:5]
        dq_ref, dk_ref, dv_ref = refs[5 + 2 * int(has_in):]
        qi = pl.program_id(1)

        @pl.when(qi == 0)
        def _():
            if has_in:
                dk_ref[...] = refs[5][...]
                dv_ref[...] = refs[6][...]
            else:
                dk_ref[...] = jnp.zeros_like(dk_ref)
                dv_ref[...] = jnp.zeros_like(dv_ref)

        r = lax.broadcasted_iota(jnp.int32, (tq, tq), 0)
        c = lax.broadcasted_iota(jnp.int32, (tq, tq), 1)
        tri_inc = (r <= c).astype(BF16)
        tri_exc = (r < c).astype(BF16)
        causal = c < r
        q = q_ref[...]
        do = do_ref[...]
        ltv = lt_ref[...]
        dq = jnp.zeros((tq, LANES), F32)
        for h in range(2):
            m = _head_mask(LANES, h)
            qh = jnp.where(m, q, jnp.zeros_like(q))
            doh = jnp.where(m, do, jnp.zeros_like(do))
            total = jnp.max(jnp.where(m, ltv, -jnp.inf), axis=1, keepdims=True)

            def block(j, left, gleft, acc, mask):
                off = pl.multiple_of(j * tq, tq)
                kb = k_ref[pl.ds(off, tq), :]
                vb = v_ref[pl.ds(off, tq), :]
                z, ln, sp, e = _sb_scores(qh, kb, mask)
                a = jnp.exp(z - sp + (total - (left + _split_dot(ln, tri_inc))))
                if mask is not None:
                    a = jnp.where(mask, a, 0.0)
                g = a * _dot(doh, vb, _NT)
                below = gleft + _dot(g.astype(BF16), tri_exc, _NN)
                rinv = 1.0 / (1.0 + e)
                beta = jnp.where(z >= 0.0, rinv, e * rinv)
                dz = g * (1.0 - beta) - below * beta
                if mask is not None:
                    dz = jnp.where(mask, dz, 0.0)
                dzs = (dz * scale).astype(BF16)
                acc = acc + _dot(dzs, kb, _NN)
                dk_ref[pl.ds(off, tq), :] += _dot(dzs, qh, _TN)
                dv_ref[pl.ds(off, tq), :] += _dot(a.astype(BF16), doh, _TN)
                return (left + jnp.sum(ln, axis=1, keepdims=True), gleft + jnp.sum(g, axis=1, keepdims=True), acc)

            zero = jnp.zeros((tq, 1), F32)
            st = lax.fori_loop(0, qi, lambda j, s: block(j, s[0], s[1], s[2], None), (zero, zero, jnp.zeros((tq, LANES), F32)))
            st = block(qi, st[0], st[1], st[2], causal)
            dq = jnp.where(m, st[2], dq)
        dq_ref[...] = dq.astype(BF16)

    W = heads * HEAD_DIM
    qspec = pl.BlockSpec((tq, LANES), lambda hp, i: (i, hp))
    seq = lambda off: pl.BlockSpec((S, LANES), lambda hp, i: (0, off + hp))
    ops = [p, kv, kv, lt, dy] + ([dk_in, dv_in] if has_in else [])
    return pl.pallas_call(
        body, name=name,
        out_shape=(jax.ShapeDtypeStruct((S, W), BF16), jax.ShapeDtypeStruct((S, W), F32), jax.ShapeDtypeStruct((S, W), F32)),
        grid=(npair, S // tq),
        in_specs=[qspec, seq(0), seq(npair), qspec, qspec] + ([seq(0), seq(0)] if has_in else []),
        out_specs=(qspec, seq(0), seq(0)),
        compiler_params=_cparams(("parallel", "arbitrary")),
    )(*ops)


def _swiglu_fwd(gate, up, *, name):
    S, F = gate.shape
    tr = _pick(S, (256,))

    def body(g_ref, u_ref, a_ref):
        gv = g_ref[...]
        a_ref[...] = (gv * jax.nn.sigmoid(gv) * u_ref[...]).astype(BF16)

    blk = pl.BlockSpec((tr, F), lambda i: (i, 0))
    return pl.pallas_call(body, name=name, out_shape=jax.ShapeDtypeStruct((S, F), BF16), grid=(S // tr,),
                          in_specs=[blk, blk], out_specs=blk, compiler_params=_cparams(("parallel",)))(gate, up)


def _swiglu_bwd(gate, up, dact, *, name):
    S, F = gate.shape
    tr = _pick(S, (256,))

    def body(g_ref, u_ref, da_ref, dg_ref, du_ref):
        gv, uv, dav = g_ref[...], u_ref[...], da_ref[...]
        s = jax.nn.sigmoid(gv)
        silu = gv * s
        dg_ref[...] = (dav * uv * (s + silu * (1.0 - s))).astype(BF16)
        du_ref[...] = (dav * silu).astype(BF16)

    blk = pl.BlockSpec((tr, F), lambda i: (i, 0))
    out = jax.ShapeDtypeStruct((S, F), BF16)
    return pl.pallas_call(body, name=name, out_shape=(out, out), grid=(S // tr,), in_specs=[blk, blk, blk],
                          out_specs=(blk, blk), compiler_params=_cparams(("parallel",)))(gate, up, dact)


def _adamw(w, g, m, v, *, name):
    R, C = w.shape
    tr = R
    for cand in (1024, 512, 256, 128, 64, 32, 16, 8):
        if R % cand == 0 and cand * C * 4 <= (1 << 20):
            tr = cand
            break
    bc1 = 1.0 - ADAM_B1 ** ADAM_STEP
    bc2 = 1.0 - ADAM_B2 ** ADAM_STEP

    def body(w_ref, g_ref, m_ref, v_ref, d_ref, nm_ref, nv_ref):
        gv = g_ref[...]
        nm = ADAM_B1 * m_ref[...] + (1.0 - ADAM_B1) * gv
        nv = ADAM_B2 * v_ref[...] + (1.0 - ADAM_B2) * (gv * gv)
        nm_ref[...] = nm
        nv_ref[...] = nv
        d_ref[...] = -ADAM_LR * ((nm / bc1) / (jnp.sqrt(nv / bc2) + ADAM_EPS) + ADAM_WD * w_ref[...])

    blk = pl.BlockSpec((tr, C), lambda i: (i, 0))
    out = jax.ShapeDtypeStruct((R, C), F32)
    return pl.pallas_call(body, name=name, out_shape=(out, out, out), grid=(R // tr,), in_specs=[blk] * 4,
                          out_specs=(blk, blk, blk), compiler_params=_cparams(("parallel",)))(w, g, m, v)


def _place():
    x, y, c = lax.axis_index("x"), lax.axis_index("y"), lax.axis_index("c")
    return x, y, c


def _all_gather_weights(shards):
    n = len(shards)

    def body(*refs):
        sh, full = refs[:n], refs[n:2 * n]
        send_sems, recv_sems, local_sems = refs[2 * n:]
        x, y, c = _place()
        me, sibling = (x, y, c), (x, y, 1 - c)
        chips = [(1 - x, y), (x, 1 - y), (1 - x, 1 - y)]

        def rows(t, px, py, pc):
            r = sh[t].shape[1]
            return full[t].at[:, pl.ds(pl.multiple_of((4 * px + 2 * py + pc) * r, BF16_ROWS), r), :]

        def copy(t, k, block, to, src=None):
            return pltpu.make_async_remote_copy(
                src_ref=rows(t, *block) if src is None else src, dst_ref=rows(t, *block),
                send_sem=send_sems.at[7 * t + k], recv_sem=recv_sems.at[7 * t + k], device_id=to, device_id_type=MESH)

        started = []
        for t in range(n):
            mine = pltpu.make_async_copy(sh[t], rows(t, *me), local_sems.at[t])
            mine.start()
            started.append(mine)
        sends = []
        for t in range(n):
            first = [copy(t, 0, me, sibling, src=sh[t])]
            first += [copy(t, 1 + j, me, (*chip, c), src=sh[t]) for j, chip in enumerate(chips)]
            for cp in first:
                cp.start()
            sends += first
        for t in range(n):
            for j, chip in enumerate(chips):
                copy(t, 1 + j, (*chip, c), me).wait_recv()
                fwd = copy(t, 4 + j, (*chip, c), sibling)
                fwd.start()
                sends.append(fwd)
        for t in range(n):
            copy(t, 0, sibling, me).wait_recv()
            for j, chip in enumerate(chips):
                copy(t, 4 + j, (*chip, 1 - c), me).wait_recv()
        for cp in sends:
            cp.wait_send()
        for cp in started:
            cp.wait()

    out_shape = [jax.ShapeDtypeStruct((s.shape[0], N_DEV * s.shape[1], s.shape[2]), s.dtype) for s in shards]
    return pl.pallas_call(
        body, name="all_gather_weights", out_shape=out_shape, in_specs=[ANY] * n, out_specs=[ANY] * n,
        scratch_shapes=[pltpu.SemaphoreType.DMA((7 * n,)), pltpu.SemaphoreType.DMA((7 * n,)), pltpu.SemaphoreType.DMA((n,))],
    )(*shards)


def _whole(ref_a, ref_b, send_sem, recv_sem, me):
    return pltpu.make_async_remote_copy(src_ref=ref_a, dst_ref=ref_b, send_sem=send_sem, recv_sem=recv_sem,
                                        device_id=me, device_id_type=MESH)


def _rs_sibling(grads):
    n = len(grads)

    def body(*refs):
        g, land = refs[:n], refs[n:2 * n]
        send_sems, recv_sems = refs[2 * n:]
        x, y, c = _place()
        for t in range(n):
            for k in range(4):
                pltpu.make_async_remote_copy(
                    src_ref=g[t].at[:, k, 1 - c], dst_ref=land[t].at[k], send_sem=send_sems.at[t], recv_sem=recv_sems.at[t],
                    device_id=(x, y, 1 - c), device_id_type=MESH).start()
        for t in range(n):
            w = _whole(land[t], land[t], send_sems.at[t], recv_sems.at[t], (x, y, c))
            w.wait_send()
            w.wait_recv()

    out_shape = [jax.ShapeDtypeStruct((4, s.shape[0], s.shape[3], s.shape[4]), s.dtype) for s in grads]
    return pl.pallas_call(
        body, name="reduce_scatter_sibling", out_shape=out_shape, in_specs=[ANY] * n, out_specs=[ANY] * n,
        scratch_shapes=[pltpu.SemaphoreType.DMA((n,)), pltpu.SemaphoreType.DMA((n,))],
    )(*grads)


def _rs_chips(sums):
    n = len(sums)

    def body(*refs):
        s, land = refs[:n], refs[n:2 * n]
        send_sems, recv_sems = refs[2 * n:]
        x, y, c = _place()
        chips = [(1 - x, y), (x, 1 - y), (1 - x, 1 - y)]
        for t in range(n):
            for j, (px, py) in enumerate(chips):
                pltpu.make_async_remote_copy(
                    src_ref=s[t].at[2 * px + py], dst_ref=land[t].at[j], send_sem=send_sems.at[t], recv_sem=recv_sems.at[t],
                    device_id=(px, py, c), device_id_type=MESH).start()
        for t in range(n):
            w = _whole(land[t], land[t], send_sems.at[t], recv_sems.at[t], (x, y, c))
            w.wait_send()
            w.wait_recv()

    out_shape = [jax.ShapeDtypeStruct((3,) + s.shape[1:], s.dtype) for s in sums]
    return pl.pallas_call(
        body, name="reduce_scatter_chips", out_shape=out_shape, in_specs=[ANY] * n, out_specs=[ANY] * n,
        scratch_shapes=[pltpu.SemaphoreType.DMA((n,)), pltpu.SemaphoreType.DMA((n,))],
    )(*sums)


def _chip_sum(g, land, core, *, name):
    L, _, _, r, C = g.shape

    def body(core_ref, g_ref, l_ref, o_ref):
        o_ref[...] = (g_ref[...].astype(F32) + l_ref[...].astype(F32)).astype(BF16)

    grid_spec = pltpu.PrefetchScalarGridSpec(
        num_scalar_prefetch=1, grid=(4, L),
        in_specs=[pl.BlockSpec((None, None, None, r, C), lambda k, l, core_ref: (l, k, core_ref[0], 0, 0)),
                  pl.BlockSpec((None, None, r, C), lambda k, l, core_ref: (k, l, 0, 0))],
        out_specs=pl.BlockSpec((None, None, r, C), lambda k, l, core_ref: (k, l, 0, 0)))
    return pl.pallas_call(body, name=name, out_shape=jax.ShapeDtypeStruct((4, L, r, C), BF16), grid_spec=grid_spec,
                          compiler_params=_cparams(("parallel", "parallel")))(core, g, land)


def _final_sum(sums, land, chip, *, name):
    _, L, r, C = sums.shape

    def body(chip_ref, s_ref, a_ref, b_ref, c_ref, o_ref):
        o_ref[...] = ((s_ref[...].astype(F32) + a_ref[...].astype(F32)) + b_ref[...].astype(F32)) + c_ref[...].astype(F32)

    slot = lambda j: pl.BlockSpec((None, None, r, C), lambda l, chip_ref: (j, l, 0, 0))
    grid_spec = pltpu.PrefetchScalarGridSpec(
        num_scalar_prefetch=1, grid=(L,),
        in_specs=[pl.BlockSpec((None, None, r, C), lambda l, chip_ref: (chip_ref[0], l, 0, 0)), slot(0), slot(1), slot(2)],
        out_specs=pl.BlockSpec((None, r, C), lambda l, chip_ref: (l, 0, 0)))
    return pl.pallas_call(body, name=name, out_shape=jax.ShapeDtypeStruct((L, r, C), F32), grid_spec=grid_spec,
                          compiler_params=_cparams(("parallel",)))(chip, sums, land, land, land)


def _exchange(v, reduce, *, name):
    R, C = v.shape

    def body(v_ref, o_ref, *scratch):
        if reduce:
            buf, send_sems, recv_sems = scratch
        else:
            buf = o_ref
            send_sems, recv_sems = scratch
        x, y, c = _place()
        me = 4 * x + 2 * y + c
        buf[me] = v_ref[...]
        copies = []
        for k in range(1, N_DEV):
            kx, ky, kc = (k >> 2) & 1, (k >> 1) & 1, k & 1
            peer = (1 - x if kx else x, 1 - y if ky else y, 1 - c if kc else c)
            cp = pltpu.make_async_remote_copy(src_ref=v_ref, dst_ref=buf.at[me], send_sem=send_sems.at[k - 1],
                                              recv_sem=recv_sems.at[k - 1], device_id=peer, device_id_type=MESH)
            cp.start()
            copies.append(cp)
        for cp in copies:
            cp.wait_recv()
        for cp in copies:
            cp.wait_send()
        if reduce:
            acc = buf[0]
            for d in range(1, N_DEV):
                acc = acc + buf[d]
            o_ref[...] = acc

    sems = [pltpu.SemaphoreType.DMA((N_DEV - 1,)), pltpu.SemaphoreType.DMA((N_DEV - 1,))]
    vm = pl.BlockSpec(memory_space=pltpu.VMEM)
    if reduce:
        return pl.pallas_call(body, name=name, out_shape=jax.ShapeDtypeStruct((R, C), F32), in_specs=[vm], out_specs=vm,
                              scratch_shapes=[pltpu.VMEM((N_DEV, R, C), F32)] + sems)(v)
    return pl.pallas_call(body, name=name, out_shape=jax.ShapeDtypeStruct((N_DEV, R, C), F32), in_specs=[vm], out_specs=vm,
                          scratch_shapes=sems)(v)


def _local_step(x, mem, target, norms, conv_w, W):
    S, D = x.shape
    depth = W["o"].shape[0]
    n_a = W["a"].shape[0]
    main = D - MEM_WIDTH
    heads = main // HEAD_DIM
    row = lambda v: v.reshape(1, D)

    mem_n = _rmsnorm(mem, row(norms["mem_norm"]), name="mem_norm")
    saved = []
    kv = hk = x_kv = None
    for i in range(depth):
        st = {"x": x}
        h = _rmsnorm(x, row(norms["mix_norm"][i]), name=f"mix_norm{i}")
        mkv = _mm(mem_n, W["mkv"], "nn", BF16, b_idx=i, name=f"mkv{i}")
        if i < n_a:
            p = _mm(h, W["a"], "nt", F32, b_idx=i, name=f"a_in{i}")
            y_main = _conv_fwd(p, conv_w[i], main, name=f"conv{i}")
            qblk = 3 * main // MEM_WIDTH
        else:
            p = _mm(h, W["b"], "nn", BF16, b_idx=i - n_a, name=f"b_in{i}")
            y_main, st["lt"] = _sb_fwd(p, kv, heads, name=f"sb{i}")
            qblk = main // MEM_WIDTH
        y_mem = _memattn_fwd(p, qblk, mkv, name=f"memattn{i}")
        y = jnp.concatenate([y_main, y_mem], axis=1)
        xm = _mm(y, W["o"], "nn", F32, b_idx=i, residual=x, name=f"w_o{i}")
        h2 = _rmsnorm(xm, row(norms["ffn_norm"][i]), name=f"ffn_norm{i}")
        gate = _mm(h2, W["g"], "nt", F32, b_idx=i, name=f"w_gate{i}")
        up = _mm(h2, W["u"], "nt", F32, b_idx=i, name=f"w_up{i}")
        act = _swiglu_fwd(gate, up, name=f"swiglu{i}")
        x = _mm(act, W["d"], "nn", F32, b_idx=i, residual=xm, name=f"w_down{i}")
        st.update(h=h, mkv=mkv, p=p, qblk=qblk, y=y, xm=xm, h2=h2, gate=gate, up=up, act=act)
        saved.append(st)
        if i == n_a - 1:
            x_kv = x
            hk = _rmsnorm(x, row(norms["kv_norm"]), name="kv_norm")
            kv = _mm(hk, W["kv"], "nt", BF16, b_idx=0, name="w_kv")

    dx, dg_final, loss = _loss_head(x, row(norms["final_norm"]), target, name="loss_head")

    gW = {k: lax.empty(v.shape, BF16) for k, v in W.items()}
    dg_mix, dg_ffn, dconv = [None] * depth, [None] * depth, [None] * n_a
    dmem_n = dk = dv = dg_kv = None
    for i in reversed(range(depth)):
        st = saved[i]
        dact = _mm(dx, W["d"], "nt", F32, b_idx=i, name=f"d_act{i}")
        gW["d"] = _mm(st["act"], dx, "tn", BF16, into=gW["d"], into_idx=i, name=f"g_w_down{i}")
        dgate, dup = _swiglu_bwd(st["gate"], st["up"], dact, name=f"swiglu_bwd{i}")
        gW["g"] = _mm(dgate, st["h2"], "tn", BF16, into=gW["g"], into_idx=i, name=f"g_w_gate{i}")
        gW["u"] = _mm(dup, st["h2"], "tn", BF16, into=gW["u"], into_idx=i, name=f"g_w_up{i}")
        dh2 = _mm(dgate, W["g"], "nn", F32, b_idx=i, name=f"d_h2g{i}")
        dh2 = _mm(dup, W["u"], "nn", F32, b_idx=i, residual=dh2, name=f"d_h2u{i}")
        dx, dg_ffn[i] = _rmsnorm_bwd(st["xm"], row(norms["ffn_norm"][i]), dh2, dx, name=f"ffn_norm_bwd{i}")
        dy = _mm(dx, W["o"], "nt", BF16, b_idx=i, name=f"d_y{i}")
        gW["o"] = _mm(st["y"], dx, "tn", BF16, into=gW["o"], into_idx=i, name=f"g_w_o{i}")
        dqmem, dmk, dmv = _memattn_bwd(st["p"], st["qblk"], st["mkv"], dy, main // MEM_WIDTH, name=f"memattn_bwd{i}")
        dmkv = jnp.concatenate([dmk, dmv], axis=1)
        if i < n_a:
            db, dc, du, dconv[i] = _conv_bwd(st["p"], conv_w[i], dy, main, name=f"conv_bwd{i}")
            dp = jnp.concatenate([db, dc, du, dqmem], axis=1)
            gW["a"] = _mm(dp, st["h"], "tn", BF16, into=gW["a"], into_idx=i, name=f"g_a_in{i}")
            dh = _mm(dp, W["a"], "nn", F32, b_idx=i, name=f"d_h{i}")
        else:
            dq, dk, dv = _sb_bwd(st["p"], kv, st["lt"], dy, heads, dk, dv, name=f"sb_bwd{i}")
            dp = jnp.concatenate([dq, dqmem], axis=1)
            gW["b"] = _mm(st["h"], dp, "tn", BF16, into=gW["b"], into_idx=i - n_a, name=f"g_b_in{i}")
            dh = _mm(dp, W["b"], "nt", F32, b_idx=i - n_a, name=f"d_h{i}")
        dx, dg_mix[i] = _rmsnorm_bwd(st["x"], row(norms["mix_norm"][i]), dh, dx, name=f"mix_norm_bwd{i}")
        gW["mkv"] = _mm(mem_n, dmkv, "tn", BF16, into=gW["mkv"], into_idx=i, name=f"g_w_mem_kv{i}")
        dmem_n = _mm(dmkv, W["mkv"], "nt", F32, b_idx=i, residual=dmem_n, name=f"d_mem_n{i}")
        if i == n_a:
            dkv = jnp.concatenate([dk, dv], axis=1).astype(BF16)
            gW["kv"] = _mm(dkv, hk, "tn", BF16, into=gW["kv"], into_idx=0, name="g_w_kv")
            dhk = _mm(dkv, W["kv"], "nn", F32, b_idx=0, name="d_hk")
            dx, dg_kv = _rmsnorm_bwd(x_kv, row(norms["kv_norm"]), dhk, dx, name="kv_norm_bwd")
    _, dg_mem = _rmsnorm_bwd(mem, row(norms["mem_norm"]), dmem_n, None, name="mem_norm_bwd")

    small = {"mix_norm": jnp.concatenate(dg_mix, axis=0), "ffn_norm": jnp.concatenate(dg_ffn, axis=0), "kv_norm": dg_kv[0],
             "mem_norm": dg_mem[0], "final_norm": dg_final[0], "conv_w": jnp.stack(dconv, axis=0)}
    return loss, dx, gW, small


_COL_SHARDED = ("a", "kv", "g", "u")
_NAMES = {"a": "a_in", "kv": "w_kv_shared", "g": "w_gate", "u": "w_up", "b": "b_in", "o": "w_o", "d": "w_down", "mkv": "w_mem_kv"}
_ORDER = ("a", "kv", "g", "u", "d", "b", "o", "mkv")
_WEIGHTS = ("mix_norm", "a_in", "conv_w", "b_in", "kv_norm", "w_kv_shared", "w_mem_kv", "w_o", "ffn_norm", "w_gate", "w_up",
            "w_down", "mem_norm", "final_norm")


def _canonical(key, w):
    w3 = w if w.ndim == 3 else w[None]
    if key in _COL_SHARDED:
        w3 = jnp.transpose(w3, (0, 2, 1))
    return w3


def _uncanonical(key, g3, like):
    if key in _COL_SHARDED:
        g3 = jnp.transpose(g3, (0, 2, 1))
    return g3.reshape(like.shape)


def _pad_rows(flat, C):
    n = flat.shape[0]
    rows = -(-n // C)
    return jnp.pad(flat, (0, rows * C - n)).reshape(rows, C)


def kernel(x, mem, mix_norm, a_in, conv_w, b_in, kv_norm, w_kv_shared, w_mem_kv, w_o, ffn_norm, w_gate, w_up, w_down, mem_norm, final_norm, loss_target, m_mix_norm, m_a_in, m_conv_w, m_b_in, m_kv_norm, m_w_kv_shared, m_w_mem_kv, m_w_o, m_ffn_norm, m_w_gate, m_w_up, m_w_down, m_mem_norm, m_final_norm, v_mix_norm, v_a_in, v_conv_w, v_b_in, v_kv_norm, v_w_kv_shared, v_w_mem_kv, v_w_o, v_ffn_norm, v_w_gate, v_w_up, v_w_down, v_mem_norm, v_final_norm):
    weights = dict(mix_norm=mix_norm, a_in=a_in, conv_w=conv_w, b_in=b_in, kv_norm=kv_norm, w_kv_shared=w_kv_shared,
                   w_mem_kv=w_mem_kv, w_o=w_o, ffn_norm=ffn_norm, w_gate=w_gate, w_up=w_up, w_down=w_down,
                   mem_norm=mem_norm, final_norm=final_norm)
    moments_m = dict(mix_norm=m_mix_norm, a_in=m_a_in, conv_w=m_conv_w, b_in=m_b_in, kv_norm=m_kv_norm,
                     w_kv_shared=m_w_kv_shared, w_mem_kv=m_w_mem_kv, w_o=m_w_o, ffn_norm=m_ffn_norm, w_gate=m_w_gate,
                     w_up=m_w_up, w_down=m_w_down, mem_norm=m_mem_norm, final_norm=m_final_norm)
    moments_v = dict(mix_norm=v_mix_norm, a_in=v_a_in, conv_w=v_conv_w, b_in=v_b_in, kv_norm=v_kv_norm,
                     w_kv_shared=v_w_kv_shared, w_mem_kv=v_w_mem_kv, w_o=v_w_o, ffn_norm=v_ffn_norm, w_gate=v_w_gate,
                     w_up=v_w_up, w_down=v_w_down, mem_norm=v_mem_norm, final_norm=v_final_norm)
    D = x.shape[-1]
    xi, yi, ci = _place()
    me = 4 * xi + 2 * yi + ci

    shards = [_canonical(k, weights[_NAMES[k]]).astype(BF16) for k in _ORDER]
    W = dict(zip(_ORDER, _all_gather_weights(shards)))
    cw_shape = conv_w.shape
    cw_rows = _pad_rows(conv_w.reshape(-1), D)
    cw_rows = jnp.pad(cw_rows, ((0, 8 - cw_rows.shape[0]), (0, 0)))
    cw_all = _exchange(cw_rows, False, name="gather_conv_w")
    n_cw = cw_shape[0] * cw_shape[1] * cw_shape[2]
    cw_all = cw_all.reshape(N_DEV, -1)[:, :n_cw].reshape((N_DEV,) + cw_shape)
    conv_full = jnp.transpose(cw_all, (1, 2, 0, 3)).reshape(cw_shape[0], cw_shape[1], N_DEV * cw_shape[2])

    norms = {k: weights[k] for k in ("mix_norm", "ffn_norm", "kv_norm", "mem_norm", "final_norm")}
    loss, grad_x, gW, small = _local_step(x[0], mem[0], loss_target[0], norms, conv_full, W)

    g5 = [gW[k].reshape(gW[k].shape[0], 4, 2, gW[k].shape[1] // N_DEV, gW[k].shape[2]) for k in _ORDER]
    core = ci.reshape(1).astype(jnp.int32)
    chip = (2 * xi + yi).reshape(1).astype(jnp.int32)
    from_sibling = _rs_sibling(g5)
    chip_sums = [_chip_sum(g, l, core, name=f"chip_sum_{k}") for k, g, l in zip(_ORDER, g5, from_sibling)]
    from_chips = _rs_chips(chip_sums)
    grads = {}
    for k, s, l in zip(_ORDER, chip_sums, from_chips):
        g3 = _final_sum(s, l, chip, name=f"final_sum_{k}")
        grads[_NAMES[k]] = _uncanonical(k, g3, weights[_NAMES[k]])

    order = ("mix_norm", "ffn_norm", "kv_norm", "mem_norm", "final_norm", "conv_w")
    flat = jnp.concatenate([small[k].reshape(-1) for k in order] + [loss[0, :1]])
    n_flat = flat.shape[0]
    rows = _pad_rows(flat, D)
    rows = jnp.pad(rows, ((0, (-rows.shape[0]) % 8), (0, 0)))
    total = _exchange(rows, True, name="all_reduce_small").reshape(-1)[:n_flat]
    off = 0
    for k in order:
        n = small[k].size
        grads[k] = total[off:off + n].reshape(small[k].shape)
        off += n
    loss_total = total[off]
    grads["conv_w"] = lax.dynamic_slice_in_dim(grads["conv_w"], me * cw_shape[2], cw_shape[2], axis=2)

    deltas, new_m, new_v = {}, {}, {}
    for k in _WEIGHTS:
        w = weights[k]
        two = (lambda a: a.reshape(-1, a.shape[-1])) if w.ndim > 1 else (lambda a: a.reshape(1, -1))
        d, nm, nv = _adamw(two(w), two(grads[k]), two(moments_m[k]), two(moments_v[k]), name=f"adamw_{k}")
        deltas[k], new_m[k], new_v[k] = d.reshape(w.shape), nm.reshape(w.shape), nv.reshape(w.shape)

    return (loss_total, grad_x[None], *[grads[k] for k in _WEIGHTS], *[deltas[k] for k in _WEIGHTS],
            *[new_m[k] for k in _WEIGHTS], *[new_v[k] for k in _WEIGHTS])
```

```python
import functools
import math

import jax
import jax.numpy as jnp
from jax import lax
from jax.experimental import pallas as pl
from jax.experimental.pallas import tpu as pltpu

F32 = jnp.float32
BF16 = jnp.bfloat16
MESH = pl.DeviceIdType.MESH

HEAD_DIM = 64
MEM_HEADS = 4
MEM_WIDTH = MEM_HEADS * HEAD_DIM
EPS = 1e-6
LANES = 128
BF16_ROWS = 16
VMEM_LIMIT = 56 * 1024 * 1024
N_DEV = 8

ADAM_LR = 0.001
ADAM_B1 = 0.9
ADAM_B2 = 0.999
ADAM_EPS = 1e-08
ADAM_WD = 0.01
ADAM_STEP = 10

ANY = pl.BlockSpec(memory_space=pl.ANY)


def _cparams(sem=None):
    return pltpu.CompilerParams(dimension_semantics=sem, vmem_limit_bytes=VMEM_LIMIT)


def _pick(n, cands):
    for c in cands:
        if n % c == 0:
            return c
    raise ValueError(f"no tile for {n} in {cands}")


def _mm(a, b, form, out_dtype, *, name, b_idx=None, residual=None, into=None, into_idx=None):
    if form == "tn":
        K, M = a.shape
    else:
        M, K = a.shape
    if form == "nt":
        N, K2 = b.shape[-2:]
    else:
        K2, N = b.shape[-2:]
    assert K == K2, (name, a.shape, b.shape)
    wide = (1408, 1280, 1024, 768, 512, 256, 128)
    tm = _pick(M, wide if form == "tn" else (1024, 512, 256, 128))
    tn = _pick(N, wide)
    tk = _pick(K, (1024, 1408, 1280, 768, 512, 256))
    nk = K // tk
    dims = {"nn": (((1,), (0,)), ((), ())), "nt": (((1,), (1,)), ((), ())), "tn": (((0,), (0,)), ((), ()))}[form]
    a_bytes, b_bytes = M * K * a.dtype.itemsize, N * K * b.dtype.itemsize
    n_outer = nk == 1 and (N // tn) * a_bytes + b_bytes < a_bytes + (M // tm) * b_bytes
    ij = (lambda g0, g1: (g1, g0)) if n_outer else (lambda g0, g1: (g0, g1))

    def spec(block, f):
        return pl.BlockSpec(block, lambda g0, g1, k: f(*ij(g0, g1), k))

    a_spec = spec((tk, tm), lambda i, j, k: (k, i)) if form == "tn" else spec((tm, tk), lambda i, j, k: (i, k))
    b_blk = (tn, tk) if form == "nt" else (tk, tn)
    b_map2 = (lambda i, j, k: (j, k)) if form == "nt" else (lambda i, j, k: (k, j))
    if b.ndim == 3:
        b_spec = spec((None,) + b_blk, lambda i, j, k: (b_idx,) + b_map2(i, j, k))
    else:
        b_spec = spec(b_blk, b_map2)
    operands, in_specs = [a, b], [a_spec, b_spec]
    if residual is not None:
        operands.append(residual)
        in_specs.append(spec((tm, tn), lambda i, j, k: (i, j)))
    aliases = {}
    if into is not None:
        aliases = {len(operands): 0}
        operands.append(into)
        in_specs.append(ANY)
        out_shape = jax.ShapeDtypeStruct(into.shape, into.dtype)
        out_spec = spec((None, tm, tn), lambda i, j, k: (into_idx, i, j))
        out_dtype = into.dtype
    else:
        out_shape = jax.ShapeDtypeStruct((M, N), out_dtype)
        out_spec = spec((tm, tn), lambda i, j, k: (i, j))
    has_res, has_into = residual is not None, into is not None
    grid = (N // tn, M // tm, nk) if n_outer else (M // tm, N // tn, nk)

    def body(*refs):
        a_ref, b_ref = refs[0], refs[1]
        r_ref = refs[2] if has_res else None
        o_ref = refs[2 + int(has_res) + int(has_into)]
        acc_ref = refs[-1]
        part = lax.dot_general(a_ref[...].astype(BF16), b_ref[...].astype(BF16), dims, preferred_element_type=F32)

        def finish(total):
            if has_res:
                total = total + r_ref[...].astype(F32)
            o_ref[...] = total.astype(out_dtype)

        if nk == 1:
            finish(part)
        else:
            k = pl.program_id(2)

            @pl.when(k == 0)
            def _():
                acc_ref[...] = part

            @pl.when(jnp.logical_and(k > 0, k < nk - 1))
            def _():
                acc_ref[...] += part

            @pl.when(k == nk - 1)
            def _():
                finish(acc_ref[...] + part)

    return pl.pallas_call(
        body, name=name, out_shape=out_shape, grid=grid, in_specs=in_specs, out_specs=out_spec,
        scratch_shapes=[pltpu.VMEM((tm, tn), F32)], input_output_aliases=aliases,
        compiler_params=_cparams(("parallel", "parallel", "arbitrary")),
    )(*operands)


def _rmsnorm(x, g, *, name):
    R, D = x.shape
    tr = _pick(R, (512, 256))

    def body(x_ref, g_ref, o_ref):
        xv = x_ref[...]
        r = lax.rsqrt(jnp.mean(xv * xv, axis=-1, keepdims=True) + EPS)
        o_ref[...] = (xv * r * g_ref[...]).astype(BF16)

    return pl.pallas_call(
        body, name=name, out_shape=jax.ShapeDtypeStruct((R, D), BF16), grid=(R // tr,),
        in_specs=[pl.BlockSpec((tr, D), lambda i: (i, 0)), pl.BlockSpec((1, D), lambda i: (0, 0))],
        out_specs=pl.BlockSpec((tr, D), lambda i: (i, 0)), compiler_params=_cparams(("parallel",)),
    )(x, g)


def _rmsnorm_bwd(x, g, dh, dx_in, *, name):
    R, D = x.shape
    tr = _pick(R, (512, 256))
    has_in = dx_in is not None

    def body(*refs):
        x_ref, g_ref, dh_ref = refs[:3]
        dxi_ref = refs[3] if has_in else None
        dx_ref, dxb_ref, dg_ref = refs[3 + int(has_in):]
        xv = x_ref[...]
        r = lax.rsqrt(jnp.mean(xv * xv, axis=-1, keepdims=True) + EPS)
        xhat = xv * r
        dhv = dh_ref[...].astype(F32)
        dxh = dhv * g_ref[...]
        dx = r * (dxh - xhat * jnp.mean(dxh * xhat, axis=-1, keepdims=True))
        if has_in:
            dx = dx + dxi_ref[...]
        dx_ref[...] = dx
        dxb_ref[...] = dx.astype(BF16)
        part = jnp.sum(dhv * xhat, axis=0, keepdims=True)

        @pl.when(pl.program_id(0) == 0)
        def _():
            dg_ref[...] = part

        @pl.when(pl.program_id(0) > 0)
        def _():
            dg_ref[...] += part

    row = pl.BlockSpec((tr, D), lambda i: (i, 0))
    vec = pl.BlockSpec((1, D), lambda i: (0, 0))
    ops = [x, g, dh] + ([dx_in] if has_in else [])
    return pl.pallas_call(
        body, name=name,
        out_shape=(jax.ShapeDtypeStruct((R, D), F32), jax.ShapeDtypeStruct((R, D), BF16), jax.ShapeDtypeStruct((1, D), F32)),
        grid=(R // tr,), in_specs=[row, vec, row] + ([row] if has_in else []), out_specs=(row, row, vec),
        compiler_params=_cparams(("arbitrary",)),
    )(*ops)


def _loss_head(x, g, target, *, name):
    R, D = x.shape
    tr = _pick(R, (512, 256))

    def body(x_ref, g_ref, t_ref, dx_ref, dxb_ref, dg_ref, loss_ref):
        xv = x_ref[...]
        gv = g_ref[...]
        r = lax.rsqrt(jnp.mean(xv * xv, axis=-1, keepdims=True) + EPS)
        xhat = xv * r
        err = xhat * gv - t_ref[...]
        loss = 0.5 * jnp.sum(jnp.mean(err * err, axis=-1, keepdims=True), axis=0, keepdims=True)
        dy = err * (1.0 / D)
        dxh = dy * gv
        dx = r * (dxh - xhat * jnp.mean(dxh * xhat, axis=-1, keepdims=True))
        dx_ref[...] = dx
        dxb_ref[...] = dx.astype(BF16)
        dg = jnp.sum(dy * xhat, axis=0, keepdims=True)
        lossv = jnp.broadcast_to(loss, (1, LANES))

        @pl.when(pl.program_id(0) == 0)
        def _():
            dg_ref[...] = dg
            loss_ref[...] = lossv

        @pl.when(pl.program_id(0) > 0)
        def _():
            dg_ref[...] += dg
            loss_ref[...] += lossv

    row = pl.BlockSpec((tr, D), lambda i: (i, 0))
    vec = pl.BlockSpec((1, D), lambda i: (0, 0))
    return pl.pallas_call(
        body, name=name,
        out_shape=(jax.ShapeDtypeStruct((R, D), F32), jax.ShapeDtypeStruct((R, D), BF16), jax.ShapeDtypeStruct((1, D), F32),
                   jax.ShapeDtypeStruct((1, LANES), F32)),
        grid=(R // tr,), in_specs=[row, vec, row], out_specs=(row, row, vec, pl.BlockSpec((1, LANES), lambda i: (0, 0))),
        compiler_params=_cparams(("arbitrary",)),
    )(x, g, target)


def _conv_taps(gv, S):
    t = lax.broadcasted_iota(jnp.int32, gv.shape, 0)
    g1 = jnp.where(t >= 1, pltpu.roll(gv, 1, 0), 0.0)
    g2 = jnp.where(t >= 2, pltpu.roll(gv, 2, 0), 0.0)
    return g1, g2


def _conv_fwd(p, w, main, *, name):
    S = p.shape[0]
    tc = LANES
    nb = main // tc

    def body(b_ref, c_ref, u_ref, w_ref, y_ref):
        gv = c_ref[...].astype(F32) * u_ref[...].astype(F32)
        g1, g2 = _conv_taps(gv, S)
        cv = w_ref[0:1, :] * g2 + w_ref[1:2, :] * g1 + w_ref[2:3, :] * gv
        y_ref[...] = (b_ref[...].astype(F32) * cv).astype(BF16)

    col = lambda off: pl.BlockSpec((S, tc), lambda j: (0, off + j))
    return pl.pallas_call(
        body, name=name, out_shape=jax.ShapeDtypeStruct((S, main), BF16), grid=(nb,),
        in_specs=[col(0), col(nb), col(2 * nb), pl.BlockSpec((3, tc), lambda j: (0, j))],
        out_specs=pl.BlockSpec((S, tc), lambda j: (0, j)), compiler_params=_cparams(("parallel",)),
    )(p, p, p, w)


def _conv_bwd(p, w, dy, main, *, name):
    S = p.shape[0]
    tc = LANES
    nb = main // tc

    def body(b_ref, c_ref, u_ref, w_ref, dy_ref, db_ref, dc_ref, du_ref, dw_ref):
        cvv, uv = c_ref[...].astype(F32), u_ref[...].astype(F32)
        gv = cvv * uv
        g1, g2 = _conv_taps(gv, S)
        w0, w1, w2 = w_ref[0:1, :], w_ref[1:2, :], w_ref[2:3, :]
        dyv = dy_ref[...].astype(F32)
        db_ref[...] = (dyv * (w0 * g2 + w1 * g1 + w2 * gv)).astype(BF16)
        dcv = dyv * b_ref[...].astype(F32)
        t = lax.broadcasted_iota(jnp.int32, dcv.shape, 0)
        n1 = jnp.where(t <= S - 2, pltpu.roll(dcv, S - 1, 0), 0.0)
        n2 = jnp.where(t <= S - 3, pltpu.roll(dcv, S - 2, 0), 0.0)
        dg = w2 * dcv + w1 * n1 + w0 * n2
        dc_ref[...] = (dg * uv).astype(BF16)
        du_ref[...] = (dg * cvv).astype(BF16)
        dw_ref[0:1, :] = jnp.sum(dcv * g2, axis=0, keepdims=True)
        dw_ref[1:2, :] = jnp.sum(dcv * g1, axis=0, keepdims=True)
        dw_ref[2:3, :] = jnp.sum(dcv * gv, axis=0, keepdims=True)

    col = lambda off: pl.BlockSpec((S, tc), lambda j: (0, off + j))
    out = jax.ShapeDtypeStruct((S, main), BF16)
    return pl.pallas_call(
        body, name=name, out_shape=(out, out, out, jax.ShapeDtypeStruct((3, main), F32)), grid=(nb,),
        in_specs=[col(0), col(nb), col(2 * nb), pl.BlockSpec((3, tc), lambda j: (0, j)), col(0)],
        out_specs=(col(0), col(0), col(0), pl.BlockSpec((3, tc), lambda j: (0, j))),
        compiler_params=_cparams(("parallel",)),
    )(p, p, p, w, dy)


def _head_mask(width, h):
    lane = lax.broadcasted_iota(jnp.int32, (1, width), 1)
    return jnp.logical_and(lane >= h * HEAD_DIM, lane < (h + 1) * HEAD_DIM)


_NT = (((1,), (1,)), ((), ()))
_NN = (((1,), (0,)), ((), ()))
_TN = (((0,), (0,)), ((), ()))


def _dot(a, b, dims):
    return lax.dot_general(a, b, dims, preferred_element_type=F32)


def _mem_probs(qh, kv):
    s = _dot(qh, kv, _NT) * (1.0 / math.sqrt(HEAD_DIM))
    s = s - jnp.max(s, axis=-1, keepdims=True)
    e = jnp.exp(s)
    return e / jnp.sum(e, axis=-1, keepdims=True)


def _memattn_fwd(p, qblk, mkv, *, name):
    S = p.shape[0]
    M = mkv.shape[0]
    W = MEM_WIDTH
    tq = _pick(S, (512, 256))

    def body(q_ref, k_ref, v_ref, o_ref):
        q = q_ref[...].astype(BF16)
        kv, vv = k_ref[...], v_ref[...]
        out = jnp.zeros((tq, W), F32)
        for h in range(MEM_HEADS):
            m = _head_mask(W, h)
            pr = _mem_probs(jnp.where(m, q, jnp.zeros_like(q)), kv)
            out = jnp.where(m, _dot(pr.astype(BF16), vv, _NN), out)
        o_ref[...] = out.astype(BF16)

    return pl.pallas_call(
        body, name=name, out_shape=jax.ShapeDtypeStruct((S, W), BF16), grid=(S // tq,),
        in_specs=[pl.BlockSpec((tq, W), lambda i: (i, qblk)), pl.BlockSpec((M, W), lambda i: (0, 0)),
                  pl.BlockSpec((M, W), lambda i: (0, 1))],
        out_specs=pl.BlockSpec((tq, W), lambda i: (i, 0)), compiler_params=_cparams(("parallel",)),
    )(p, mkv, mkv)


def _memattn_bwd(p, qblk, mkv, dy, dyblk, *, name):
    S = p.shape[0]
    M = mkv.shape[0]
    W = MEM_WIDTH
    tq = _pick(S, (512, 256))
    scale = 1.0 / math.sqrt(HEAD_DIM)

    def body(q_ref, k_ref, v_ref, do_ref, dq_ref, dk_ref, dv_ref, dk_acc, dv_acc):
        q = q_ref[...].astype(BF16)
        do = do_ref[...].astype(BF16)
        kv, vv = k_ref[...], v_ref[...]
        dq = jnp.zeros((tq, W), F32)
        dk = jnp.zeros((M, W), F32)
        dv = jnp.zeros((M, W), F32)
        for h in range(MEM_HEADS):
            m = _head_mask(W, h)
            qh = jnp.where(m, q, jnp.zeros_like(q))
            doh = jnp.where(m, do, jnp.zeros_like(do))
            pr = _mem_probs(qh, kv)
            dpr = _dot(doh, vv, _NT)
            ds = (pr * (dpr - jnp.sum(dpr * pr, axis=-1, keepdims=True)) * scale).astype(BF16)
            dq = jnp.where(m, _dot(ds, kv, _NN), dq)
            dk = dk + _dot(ds, qh, _TN)
            dv = dv + _dot(pr.astype(BF16), doh, _TN)
        dq_ref[...] = dq.astype(BF16)
        i = pl.program_id(0)

        @pl.when(i == 0)
        def _():
            dk_acc[...] = dk
            dv_acc[...] = dv

        @pl.when(i > 0)
        def _():
            dk_acc[...] += dk
            dv_acc[...] += dv

        @pl.when(i == S // tq - 1)
        def _():
            dk_ref[...] = dk_acc[...].astype(BF16)
            dv_ref[...] = dv_acc[...].astype(BF16)

    kspec = lambda c: pl.BlockSpec((M, W), lambda i: (0, c))
    return pl.pallas_call(
        body, name=name,
        out_shape=(jax.ShapeDtypeStruct((S, W), BF16), jax.ShapeDtypeStruct((M, W), BF16), jax.ShapeDtypeStruct((M, W), BF16)),
        grid=(S // tq,),
        in_specs=[pl.BlockSpec((tq, W), lambda i: (i, qblk)), kspec(0), kspec(1), pl.BlockSpec((tq, W), lambda i: (i, dyblk))],
        out_specs=(pl.BlockSpec((tq, W), lambda i: (i, 0)), kspec(0), kspec(0)),
        scratch_shapes=[pltpu.VMEM((M, W), F32), pltpu.VMEM((M, W), F32)],
        compiler_params=_cparams(("arbitrary",)),
    )(p, mkv, mkv, dy)


SB_TQ = 256
SB_CLAMP = 80.0


SB_CHUNK = 64


def _by_rows(fn, *arrays):
    rows = next(a for a in arrays if a is not None).shape[0]
    outs = [fn(*[None if a is None else a[r0:r0 + SB_CHUNK] for a in arrays]) for r0 in range(0, rows, SB_CHUNK)]
    return tuple(jnp.concatenate(col, axis=0) for col in zip(*outs))


def _sb_scores(qh, kb, causal):
    def chain(z, mask):
        z = jnp.clip(z, -SB_CLAMP, SB_CLAMP)
        w = 1.0 + jnp.exp(z)
        sp = jnp.log(w)
        zs = z - sp
        if mask is not None:
            sp = jnp.where(mask, sp, 0.0)
            zs = jnp.where(mask, zs, -1e30)
            w = jnp.where(mask, w, 1.0)
        return zs, sp.astype(BF16), jnp.sum(sp, axis=1, keepdims=True), w

    return _by_rows(chain, _dot(qh, kb, _NT), causal)


def _sb_weights(zs, spb, tri, carry):
    return _by_rows(lambda zs_c, t_c, c_c: (jnp.exp(zs_c - (t_c + c_c)).astype(BF16),), zs, _dot(spb, tri, _NN), carry)[0]


def _stack_heads(v, m0):
    zero = jnp.zeros_like(v)
    return jnp.concatenate([jnp.where(m0, v, zero), jnp.where(m0, zero, v)], axis=0)


def _stacked_causal(tq):
    r = lax.broadcasted_iota(jnp.int32, (2 * tq, tq), 0)
    c = lax.broadcasted_iota(jnp.int32, (2 * tq, tq), 1)
    return c < jnp.where(r >= tq, r - tq, r)


def _sb_fwd(p, kv, heads, *, name):
    S = p.shape[0]
    tq = SB_TQ
    npair = heads // 2

    def body(q_ref, k_ref, v_ref, o_ref, o32_ref):
        qi = pl.program_id(1)
        r = lax.broadcasted_iota(jnp.int32, (tq, tq), 0)
        c = lax.broadcasted_iota(jnp.int32, (tq, tq), 1)
        tri = (r > c).astype(BF16)
        causal = _stacked_causal(tq)
        m0 = _head_mask(LANES, 0)
        qh = _stack_heads(q_ref[...] * jnp.asarray(1.0 / math.sqrt(HEAD_DIM), BF16), m0)

        def block(j, carry, acc, mask):
            off = pl.multiple_of(j * tq, tq)
            kb = k_ref[pl.ds(off, tq), :]
            vb = v_ref[pl.ds(off, tq), :]
            zs, spb, sp_sum, _ = _sb_scores(qh, kb, mask)
            acc = acc + _dot(_sb_weights(zs, spb, tri, carry), vb, _NN)
            return carry + sp_sum, acc

        st = block(qi, jnp.zeros((2 * tq, 1), F32), jnp.zeros((2 * tq, LANES), F32), causal)
        odd = qi % 2
        st = lax.cond(odd == 1, lambda s: block(qi - 1, s[0], s[1], None), lambda s: s, st)

        def pair(it, s):
            j = qi - 1 - odd - 2 * it
            s = block(j, s[0], s[1], None)
            return block(j - 1, s[0], s[1], None)

        carry, acc = lax.fori_loop(0, qi // 2, pair, st)
        out = jnp.where(m0, acc[:tq], acc[tq:])
        o_ref[...] = out.astype(BF16)
        o32_ref[...] = out

    W = heads * HEAD_DIM
    qspec = pl.BlockSpec((tq, LANES), lambda hp, i: (i, hp))
    return pl.pallas_call(
        body, name=name, out_shape=(jax.ShapeDtypeStruct((S, W), BF16), jax.ShapeDtypeStruct((S, W), F32)), grid=(npair, S // tq),
        in_specs=[qspec, pl.BlockSpec((S, LANES), lambda hp, i: (0, hp)), pl.BlockSpec((S, LANES), lambda hp, i: (0, npair + hp))],
        out_specs=(qspec, qspec), compiler_params=_cparams(("parallel", "arbitrary")),
    )(p, kv, kv)


def _sb_bwd(p, kv, o32, dy, heads, dk_in, dv_in, *, name):
    S = p.shape[0]
    tq = SB_TQ
    npair = heads // 2
    has_in = dk_in is not None
    scale = 1.0 / math.sqrt(HEAD_DIM)

    def body(*refs):
        q_ref, k_ref, v_ref, o_ref, do_ref = refs[:5]
        dq_ref, dk_ref, dv_ref = refs[5 + 2 * int(has_in):]
        qi = pl.program_id(1)

        @pl.when(qi == 0)
        def _():
            if has_in:
                dk_ref[...] = refs[5][...]
                dv_ref[...] = refs[6][...]
            else:
                dk_ref[...] = jnp.zeros_like(dk_ref)
                dv_ref[...] = jnp.zeros_like(dv_ref)

        r = lax.broadcasted_iota(jnp.int32, (tq, tq), 0)
        c = lax.broadcasted_iota(jnp.int32, (tq, tq), 1)
        tri = (r > c).astype(BF16)
        tri_low = (r < c).astype(BF16)
        causal = _stacked_causal(tq)
        m0 = _head_mask(LANES, 0)
        qh = _stack_heads(q_ref[...] * jnp.asarray(scale, BF16), m0)
        do = do_ref[...]
        doh = _stack_heads(do, m0)
        dov = do.astype(F32) * o_ref[...]
        dsum = jnp.concatenate([jnp.sum(jnp.where(m0, dov, 0.0), axis=1, keepdims=True),
                                jnp.sum(jnp.where(m0, 0.0, dov), axis=1, keepdims=True)], axis=0)

        def block(j, carry, gcarry, acc, mask):
            off = pl.multiple_of(j * tq, tq)
            kb = k_ref[pl.ds(off, tq), :]
            vb = v_ref[pl.ds(off, tq), :]
            zs, spb, sp_sum, w = _sb_scores(qh, kb, mask)
            ab = _sb_weights(zs, spb, tri, carry)

            def grads(ab_c, da_c):
                g = ab_c.astype(F32) * da_c
                return g, g.astype(BF16), jnp.sum(g, axis=1, keepdims=True)

            g, gb, g_sum = _by_rows(grads, ab, _dot(doh, vb, _NT))
            gcarry = gcarry + g_sum

            def logit_grads(g_c, w_c, low_c, left_c):
                rinv = 1.0 / w_c
                return ((g_c * rinv - (left_c + low_c) * (1.0 - rinv)).astype(BF16),)

            dzs = _by_rows(logit_grads, g, w, _dot(gb, tri_low, _NN), dsum - gcarry)[0]
            acc = acc + _dot(dzs, kb, _NN)
            dk_ref[pl.ds(off, tq), :] += _dot(dzs, qh, _TN)
            dv_ref[pl.ds(off, tq), :] += _dot(ab, doh, _TN)
            return (carry + sp_sum, gcarry, acc)

        zero = jnp.zeros((2 * tq, 1), F32)
        st = block(qi, zero, zero, jnp.zeros((2 * tq, LANES), F32), causal)
        odd = qi % 2
        st = lax.cond(odd == 1, lambda s: block(qi - 1, s[0], s[1], s[2], None), lambda s: s, st)

        def pair(it, s):
            j = qi - 1 - odd - 2 * it
            s = block(j, s[0], s[1], s[2], None)
            return block(j - 1, s[0], s[1], s[2], None)

        st = lax.fori_loop(0, qi // 2, pair, st)
        dq_ref[...] = (jnp.where(m0, st[2][:tq], st[2][tq:]) * scale).astype(BF16)

    W = heads * HEAD_DIM
    qspec = pl.BlockSpec((tq, LANES), lambda hp, i: (i, hp))
    seq = lambda off: pl.BlockSpec((S, LANES), lambda hp, i: (0, off + hp))
    ops = [p, kv, kv, o32, dy] + ([dk_in, dv_in] if has_in else [])
    return pl.pallas_call(
        body, name=name,
        out_shape=(jax.ShapeDtypeStruct((S, W), BF16), jax.ShapeDtypeStruct((S, W), F32), jax.ShapeDtypeStruct((S, W), F32)),
        grid=(npair, S // tq),
        in_specs=[qspec, seq(0), seq(npair), qspec, qspec] + ([seq(0), seq(0)] if has_in else []),
        out_specs=(qspec, seq(0), seq(0)),
        compiler_params=_cparams(("parallel", "arbitrary")),
    )(*ops)


def _ffn_up(h, wg, wu, idx, *, name):
    S, D = h.shape
    F = wg.shape[1]
    tm = _pick(S, (512, 256))
    tn = _pick(F, (1408, 1024, 512, 256, 128))

    def body(h_ref, g_ref, u_ref, gate_ref, up_ref, act_ref):
        hv = h_ref[...]
        g = _dot(hv, g_ref[...], _NT)
        u = _dot(hv, u_ref[...], _NT)
        gate_ref[...] = g.astype(BF16)
        up_ref[...] = u.astype(BF16)
        act_ref[...] = (g * jax.nn.sigmoid(g) * u).astype(BF16)

    wspec = pl.BlockSpec((None, tn, D), lambda j, i: (idx, j, 0))
    ospec = pl.BlockSpec((tm, tn), lambda j, i: (i, j))
    out = jax.ShapeDtypeStruct((S, F), BF16)
    return pl.pallas_call(
        body, name=name, out_shape=(out, out, out), grid=(F // tn, S // tm),
        in_specs=[pl.BlockSpec((tm, D), lambda j, i: (i, 0)), wspec, wspec], out_specs=(ospec, ospec, ospec),
        compiler_params=_cparams(("parallel", "parallel")),
    )(h, wg, wu)


def _ffn_down_bwd(dx, wd, idx, gate, up, *, name):
    S, D = dx.shape
    F = wd.shape[1]
    tm = _pick(S, (512, 256))
    tn = _pick(F, (1408, 1024, 512, 256, 128))

    def body(dx_ref, w_ref, g_ref, u_ref, dg_ref, du_ref):
        da = _dot(dx_ref[...], w_ref[...], _NT)
        gv, uv = g_ref[...].astype(F32), u_ref[...].astype(F32)
        s = jax.nn.sigmoid(gv)
        silu = gv * s
        dg_ref[...] = (da * uv * (s + silu * (1.0 - s))).astype(BF16)
        du_ref[...] = (da * silu).astype(BF16)

    ospec = pl.BlockSpec((tm, tn), lambda j, i: (i, j))
    out = jax.ShapeDtypeStruct((S, F), BF16)
    return pl.pallas_call(
        body, name=name, out_shape=(out, out), grid=(F // tn, S // tm),
        in_specs=[pl.BlockSpec((tm, D), lambda j, i: (i, 0)), pl.BlockSpec((None, tn, D), lambda j, i: (idx, j, 0)), ospec, ospec],
        out_specs=(ospec, ospec), compiler_params=_cparams(("parallel", "parallel")),
    )(dx, wd, gate, up)


def _adamw(w, g, m, v, *, name):
    R, C = w.shape
    tr = R
    for cand in (1024, 512, 256, 128, 64, 32, 16, 8):
        if R % cand == 0 and cand * C * 4 <= (1 << 20):
            tr = cand
            break
    bc1 = 1.0 - ADAM_B1 ** ADAM_STEP
    bc2 = 1.0 - ADAM_B2 ** ADAM_STEP

    def body(w_ref, g_ref, m_ref, v_ref, d_ref, nm_ref, nv_ref):
        gv = g_ref[...]
        nm = ADAM_B1 * m_ref[...] + (1.0 - ADAM_B1) * gv
        nv = ADAM_B2 * v_ref[...] + (1.0 - ADAM_B2) * (gv * gv)
        nm_ref[...] = nm
        nv_ref[...] = nv
        d_ref[...] = -ADAM_LR * ((nm / bc1) / (jnp.sqrt(nv / bc2) + ADAM_EPS) + ADAM_WD * w_ref[...])

    blk = pl.BlockSpec((tr, C), lambda i: (i, 0))
    out = jax.ShapeDtypeStruct((R, C), F32)
    return pl.pallas_call(body, name=name, out_shape=(out, out, out), grid=(R // tr,), in_specs=[blk] * 4,
                          out_specs=(blk, blk, blk), compiler_params=_cparams(("parallel",)))(w, g, m, v)


def _place():
    x, y, c = lax.axis_index("x"), lax.axis_index("y"), lax.axis_index("c")
    return x, y, c


def _all_gather_weights(shards):
    n = len(shards)

    def body(*refs):
        sh, full = refs[:n], refs[n:2 * n]
        send_sems, recv_sems, local_sems = refs[2 * n:]
        x, y, c = _place()
        me, sibling = (x, y, c), (x, y, 1 - c)
        chips = [(1 - x, y), (x, 1 - y), (1 - x, 1 - y)]

        def rows(t, px, py, pc):
            r = sh[t].shape[1]
            return full[t].at[:, pl.ds(pl.multiple_of((4 * px + 2 * py + pc) * r, BF16_ROWS), r), :]

        def copy(t, k, block, to, src=None):
            return pltpu.make_async_remote_copy(
                src_ref=rows(t, *block) if src is None else src, dst_ref=rows(t, *block),
                send_sem=send_sems.at[7 * t + k], recv_sem=recv_sems.at[7 * t + k], device_id=to, device_id_type=MESH)

        started = []
        for t in range(n):
            mine = pltpu.make_async_copy(sh[t], rows(t, *me), local_sems.at[t])
            mine.start()
            started.append(mine)
        sends = []
        for t in range(n):
            first = [copy(t, 0, me, sibling, src=sh[t])]
            first += [copy(t, 1 + j, me, (*chip, c), src=sh[t]) for j, chip in enumerate(chips)]
            for cp in first:
                cp.start()
            sends += first
        for t in range(n):
            for j, chip in enumerate(chips):
                copy(t, 1 + j, (*chip, c), me).wait_recv()
                fwd = copy(t, 4 + j, (*chip, c), sibling)
                fwd.start()
                sends.append(fwd)
        for t in range(n):
            copy(t, 0, sibling, me).wait_recv()
            for j, chip in enumerate(chips):
                copy(t, 4 + j, (*chip, 1 - c), me).wait_recv()
        for cp in sends:
            cp.wait_send()
        for cp in started:
            cp.wait()

    out_shape = [jax.ShapeDtypeStruct((s.shape[0], N_DEV * s.shape[1], s.shape[2]), s.dtype) for s in shards]
    return pl.pallas_call(
        body, name="all_gather_weights", out_shape=out_shape, in_specs=[ANY] * n, out_specs=[ANY] * n,
        scratch_shapes=[pltpu.SemaphoreType.DMA((7 * n,)), pltpu.SemaphoreType.DMA((7 * n,)), pltpu.SemaphoreType.DMA((n,))],
    )(*shards)


def _whole(ref_a, ref_b, send_sem, recv_sem, me):
    return pltpu.make_async_remote_copy(src_ref=ref_a, dst_ref=ref_b, send_sem=send_sem, recv_sem=recv_sem,
                                        device_id=me, device_id_type=MESH)


def _rs_sibling(grads):
    n = len(grads)

    def body(*refs):
        g, land = refs[:n], refs[n:2 * n]
        send_sems, recv_sems = refs[2 * n:]
        x, y, c = _place()
        for t in range(n):
            for k in range(4):
                pltpu.make_async_remote_copy(
                    src_ref=g[t].at[:, k, 1 - c], dst_ref=land[t].at[k], send_sem=send_sems.at[t], recv_sem=recv_sems.at[t],
                    device_id=(x, y, 1 - c), device_id_type=MESH).start()
        for t in range(n):
            w = _whole(land[t], land[t], send_sems.at[t], recv_sems.at[t], (x, y, c))
            w.wait_send()
            w.wait_recv()

    out_shape = [jax.ShapeDtypeStruct((4, s.shape[0], s.shape[3], s.shape[4]), s.dtype) for s in grads]
    return pl.pallas_call(
        body, name="reduce_scatter_sibling", out_shape=out_shape, in_specs=[ANY] * n, out_specs=[ANY] * n,
        scratch_shapes=[pltpu.SemaphoreType.DMA((n,)), pltpu.SemaphoreType.DMA((n,))],
    )(*grads)


def _rs_chips(sums):
    n = len(sums)

    def body(*refs):
        s, land = refs[:n], refs[n:2 * n]
        send_sems, recv_sems = refs[2 * n:]
        x, y, c = _place()
        chips = [(1 - x, y), (x, 1 - y), (1 - x, 1 - y)]
        for t in range(n):
            for j, (px, py) in enumerate(chips):
                pltpu.make_async_remote_copy(
                    src_ref=s[t].at[2 * px + py], dst_ref=land[t].at[j], send_sem=send_sems.at[t], recv_sem=recv_sems.at[t],
                    device_id=(px, py, c), device_id_type=MESH).start()
        for t in range(n):
            w = _whole(land[t], land[t], send_sems.at[t], recv_sems.at[t], (x, y, c))
            w.wait_send()
            w.wait_recv()

    out_shape = [jax.ShapeDtypeStruct((3,) + s.shape[1:], s.dtype) for s in sums]
    return pl.pallas_call(
        body, name="reduce_scatter_chips", out_shape=out_shape, in_specs=[ANY] * n, out_specs=[ANY] * n,
        scratch_shapes=[pltpu.SemaphoreType.DMA((n,)), pltpu.SemaphoreType.DMA((n,))],
    )(*sums)


def _chip_sum(g, land, core, *, name):
    L, _, _, r, C = g.shape

    def body(core_ref, g_ref, l_ref, o_ref):
        o_ref[...] = (g_ref[...].astype(F32) + l_ref[...].astype(F32)).astype(BF16)

    grid_spec = pltpu.PrefetchScalarGridSpec(
        num_scalar_prefetch=1, grid=(4, L),
        in_specs=[pl.BlockSpec((None, None, None, r, C), lambda k, l, core_ref: (l, k, core_ref[0], 0, 0)),
                  pl.BlockSpec((None, None, r, C), lambda k, l, core_ref: (k, l, 0, 0))],
        out_specs=pl.BlockSpec((None, None, r, C), lambda k, l, core_ref: (k, l, 0, 0)))
    return pl.pallas_call(body, name=name, out_shape=jax.ShapeDtypeStruct((4, L, r, C), BF16), grid_spec=grid_spec,
                          compiler_params=_cparams(("parallel", "parallel")))(core, g, land)


def _final_sum(sums, land, chip, *, name):
    _, L, r, C = sums.shape

    def body(chip_ref, s_ref, a_ref, b_ref, c_ref, o_ref):
        o_ref[...] = ((s_ref[...].astype(F32) + a_ref[...].astype(F32)) + b_ref[...].astype(F32)) + c_ref[...].astype(F32)

    slot = lambda j: pl.BlockSpec((None, None, r, C), lambda l, chip_ref: (j, l, 0, 0))
    grid_spec = pltpu.PrefetchScalarGridSpec(
        num_scalar_prefetch=1, grid=(L,),
        in_specs=[pl.BlockSpec((None, None, r, C), lambda l, chip_ref: (chip_ref[0], l, 0, 0)), slot(0), slot(1), slot(2)],
        out_specs=pl.BlockSpec((None, r, C), lambda l, chip_ref: (l, 0, 0)))
    return pl.pallas_call(body, name=name, out_shape=jax.ShapeDtypeStruct((L, r, C), F32), grid_spec=grid_spec,
                          compiler_params=_cparams(("parallel",)))(chip, sums, land, land, land)


def _exchange(v, reduce, *, name):
    R, C = v.shape

    def body(v_ref, o_ref, *scratch):
        if reduce:
            buf, send_sems, recv_sems = scratch
        else:
            buf = o_ref
            send_sems, recv_sems = scratch
        x, y, c = _place()
        me = 4 * x + 2 * y + c
        buf[me] = v_ref[...]
        copies = []
        for k in range(1, N_DEV):
            kx, ky, kc = (k >> 2) & 1, (k >> 1) & 1, k & 1
            peer = (1 - x if kx else x, 1 - y if ky else y, 1 - c if kc else c)
            cp = pltpu.make_async_remote_copy(src_ref=v_ref, dst_ref=buf.at[me], send_sem=send_sems.at[k - 1],
                                              recv_sem=recv_sems.at[k - 1], device_id=peer, device_id_type=MESH)
            cp.start()
            copies.append(cp)
        for cp in copies:
            cp.wait_recv()
        for cp in copies:
            cp.wait_send()
        if reduce:
            acc = buf[0]
            for d in range(1, N_DEV):
                acc = acc + buf[d]
            o_ref[...] = acc

    sems = [pltpu.SemaphoreType.DMA((N_DEV - 1,)), pltpu.SemaphoreType.DMA((N_DEV - 1,))]
    vm = pl.BlockSpec(memory_space=pltpu.VMEM)
    if reduce:
        return pl.pallas_call(body, name=name, out_shape=jax.ShapeDtypeStruct((R, C), F32), in_specs=[vm], out_specs=vm,
                              scratch_shapes=[pltpu.VMEM((N_DEV, R, C), F32)] + sems)(v)
    return pl.pallas_call(body, name=name, out_shape=jax.ShapeDtypeStruct((N_DEV, R, C), F32), in_specs=[vm], out_specs=vm,
                          scratch_shapes=sems)(v)


def _local_step(x, mem, target, norms, conv_w, W):
    S, D = x.shape
    depth = W["o"].shape[0]
    n_a = W["a"].shape[0]
    main = D - MEM_WIDTH
    heads = main // HEAD_DIM
    row = lambda v: v.reshape(1, D)

    mem_n = _rmsnorm(mem, row(norms["mem_norm"]), name="mem_norm")
    saved = []
    kv = hk = x_kv = None
    for i in range(depth):
        st = {"x": x}
        h = _rmsnorm(x, row(norms["mix_norm"][i]), name=f"mix_norm{i}")
        mkv = _mm(mem_n, W["mkv"], "nn", BF16, b_idx=i, name=f"mkv{i}")
        if i < n_a:
            p = _mm(h, W["a"], "nt", BF16, b_idx=i, name=f"a_in{i}")
            y_main = _conv_fwd(p, conv_w[i], main, name=f"conv{i}")
            qblk = 3 * main // MEM_WIDTH
        else:
            p = _mm(h, W["b"], "nn", BF16, b_idx=i - n_a, name=f"b_in{i}")
            y_main, st["o32"] = _sb_fwd(p, kv, heads, name=f"sb{i}")
            qblk = main // MEM_WIDTH
        y_mem = _memattn_fwd(p, qblk, mkv, name=f"memattn{i}")
        y = jnp.concatenate([y_main, y_mem], axis=1)
        xm = _mm(y, W["o"], "nn", F32, b_idx=i, residual=x, name=f"w_o{i}")
        h2 = _rmsnorm(xm, row(norms["ffn_norm"][i]), name=f"ffn_norm{i}")
        gate, up, act = _ffn_up(h2, W["g"], W["u"], i, name=f"ffn_up{i}")
        x = _mm(act, W["d"], "nn", F32, b_idx=i, residual=xm, name=f"w_down{i}")
        st.update(h=h, mkv=mkv, p=p, qblk=qblk, y=y, xm=xm, h2=h2, gate=gate, up=up, act=act)
        saved.append(st)
        if i == n_a - 1:
            x_kv = x
            hk = _rmsnorm(x, row(norms["kv_norm"]), name="kv_norm")
            kv = _mm(hk, W["kv"], "nt", BF16, b_idx=0, name="w_kv")

    dx, dxb, dg_final, loss = _loss_head(x, row(norms["final_norm"]), target, name="loss_head")

    gW = {k: lax.empty(v.shape, BF16) for k, v in W.items()}
    dg_mix, dg_ffn, dconv = [None] * depth, [None] * depth, [None] * n_a
    dmem_n = dk = dv = dg_kv = None
    for i in reversed(range(depth)):
        st = saved[i]
        dgate, dup = _ffn_down_bwd(dxb, W["d"], i, st["gate"], st["up"], name=f"ffn_down_bwd{i}")
        gW["d"] = _mm(st["act"], dxb, "tn", BF16, into=gW["d"], into_idx=i, name=f"g_w_down{i}")
        gW["g"] = _mm(dgate, st["h2"], "tn", BF16, into=gW["g"], into_idx=i, name=f"g_w_gate{i}")
        gW["u"] = _mm(dup, st["h2"], "tn", BF16, into=gW["u"], into_idx=i, name=f"g_w_up{i}")
        dh2 = _mm(dgate, W["g"], "nn", F32, b_idx=i, name=f"d_h2g{i}")
        dh2 = _mm(dup, W["u"], "nn", F32, b_idx=i, residual=dh2, name=f"d_h2u{i}")
        dx, dxb, dg_ffn[i] = _rmsnorm_bwd(st["xm"], row(norms["ffn_norm"][i]), dh2, dx, name=f"ffn_norm_bwd{i}")
        dy = _mm(dxb, W["o"], "nt", BF16, b_idx=i, name=f"d_y{i}")
        gW["o"] = _mm(st["y"], dxb, "tn", BF16, into=gW["o"], into_idx=i, name=f"g_w_o{i}")
        dqmem, dmk, dmv = _memattn_bwd(st["p"], st["qblk"], st["mkv"], dy, main // MEM_WIDTH, name=f"memattn_bwd{i}")
        dmkv = jnp.concatenate([dmk, dmv], axis=1)
        if i < n_a:
            db, dc, du, dconv[i] = _conv_bwd(st["p"], conv_w[i], dy, main, name=f"conv_bwd{i}")
            dp = jnp.concatenate([db, dc, du, dqmem], axis=1)
            gW["a"] = _mm(dp, st["h"], "tn", BF16, into=gW["a"], into_idx=i, name=f"g_a_in{i}")
            dh = _mm(dp, W["a"], "nn", F32, b_idx=i, name=f"d_h{i}")
        else:
            dq, dk, dv = _sb_bwd(st["p"], kv, st["o32"], dy, heads, dk, dv, name=f"sb_bwd{i}")
            dp = jnp.concatenate([dq, dqmem], axis=1)
            gW["b"] = _mm(st["h"], dp, "tn", BF16, into=gW["b"], into_idx=i - n_a, name=f"g_b_in{i}")
            dh = _mm(dp, W["b"], "nt", F32, b_idx=i - n_a, name=f"d_h{i}")
        dx, dxb, dg_mix[i] = _rmsnorm_bwd(st["x"], row(norms["mix_norm"][i]), dh, dx, name=f"mix_norm_bwd{i}")
        gW["mkv"] = _mm(mem_n, dmkv, "tn", BF16, into=gW["mkv"], into_idx=i, name=f"g_w_mem_kv{i}")
        dmem_n = _mm(dmkv, W["mkv"], "nt", F32, b_idx=i, residual=dmem_n, name=f"d_mem_n{i}")
        if i == n_a:
            dkv = jnp.concatenate([dk, dv], axis=1).astype(BF16)
            gW["kv"] = _mm(dkv, hk, "tn", BF16, into=gW["kv"], into_idx=0, name="g_w_kv")
            dhk = _mm(dkv, W["kv"], "nn", F32, b_idx=0, name="d_hk")
            dx, dxb, dg_kv = _rmsnorm_bwd(x_kv, row(norms["kv_norm"]), dhk, dx, name="kv_norm_bwd")
    _, _, dg_mem = _rmsnorm_bwd(mem, row(norms["mem_norm"]), dmem_n, None, name="mem_norm_bwd")

    small = {"mix_norm": jnp.concatenate(dg_mix, axis=0), "ffn_norm": jnp.concatenate(dg_ffn, axis=0), "kv_norm": dg_kv[0],
             "mem_norm": dg_mem[0], "final_norm": dg_final[0], "conv_w": jnp.stack(dconv, axis=0)}
    return loss, dx, gW, small


_COL_SHARDED = ("a", "kv", "g", "u")
_NAMES = {"a": "a_in", "kv": "w_kv_shared", "g": "w_gate", "u": "w_up", "b": "b_in", "o": "w_o", "d": "w_down", "mkv": "w_mem_kv"}
_ORDER = ("a", "kv", "g", "u", "d", "b", "o", "mkv")
_WEIGHTS = ("mix_norm", "a_in", "conv_w", "b_in", "kv_norm", "w_kv_shared", "w_mem_kv", "w_o", "ffn_norm", "w_gate", "w_up",
            "w_down", "mem_norm", "final_norm")


def _canonical(key, w):
    w3 = w if w.ndim == 3 else w[None]
    if key in _COL_SHARDED:
        w3 = jnp.transpose(w3, (0, 2, 1))
    return w3


def _uncanonical(key, g3, like):
    if key in _COL_SHARDED:
        g3 = jnp.transpose(g3, (0, 2, 1))
    return g3.reshape(like.shape)


def _pad_rows(flat, C):
    n = flat.shape[0]
    rows = -(-n // C)
    return jnp.pad(flat, (0, rows * C - n)).reshape(rows, C)


def kernel(x, mem, mix_norm, a_in, conv_w, b_in, kv_norm, w_kv_shared, w_mem_kv, w_o, ffn_norm, w_gate, w_up, w_down, mem_norm, final_norm, loss_target, m_mix_norm, m_a_in, m_conv_w, m_b_in, m_kv_norm, m_w_kv_shared, m_w_mem_kv, m_w_o, m_ffn_norm, m_w_gate, m_w_up, m_w_down, m_mem_norm, m_final_norm, v_mix_norm, v_a_in, v_conv_w, v_b_in, v_kv_norm, v_w_kv_shared, v_w_mem_kv, v_w_o, v_ffn_norm, v_w_gate, v_w_up, v_w_down, v_mem_norm, v_final_norm):
    weights = dict(mix_norm=mix_norm, a_in=a_in, conv_w=conv_w, b_in=b_in, kv_norm=kv_norm, w_kv_shared=w_kv_shared,
                   w_mem_kv=w_mem_kv, w_o=w_o, ffn_norm=ffn_norm, w_gate=w_gate, w_up=w_up, w_down=w_down,
                   mem_norm=mem_norm, final_norm=final_norm)
    moments_m = dict(mix_norm=m_mix_norm, a_in=m_a_in, conv_w=m_conv_w, b_in=m_b_in, kv_norm=m_kv_norm,
                     w_kv_shared=m_w_kv_shared, w_mem_kv=m_w_mem_kv, w_o=m_w_o, ffn_norm=m_ffn_norm, w_gate=m_w_gate,
                     w_up=m_w_up, w_down=m_w_down, mem_norm=m_mem_norm, final_norm=m_final_norm)
    moments_v = dict(mix_norm=v_mix_norm, a_in=v_a_in, conv_w=v_conv_w, b_in=v_b_in, kv_norm=v_kv_norm,
                     w_kv_shared=v_w_kv_shared, w_mem_kv=v_w_mem_kv, w_o=v_w_o, ffn_norm=v_ffn_norm, w_gate=v_w_gate,
                     w_up=v_w_up, w_down=v_w_down, mem_norm=v_mem_norm, final_norm=v_final_norm)
    D = x.shape[-1]
    xi, yi, ci = _place()
    me = 4 * xi + 2 * yi + ci

    shards = [_canonical(k, weights[_NAMES[k]]).astype(BF16) for k in _ORDER]
    W = dict(zip(_ORDER, _all_gather_weights(shards)))
    cw_shape = conv_w.shape
    cw_rows = _pad_rows(conv_w.reshape(-1), D)
    cw_rows = jnp.pad(cw_rows, ((0, 8 - cw_rows.shape[0]), (0, 0)))
    cw_all = _exchange(cw_rows, False, name="gather_conv_w")
    n_cw = cw_shape[0] * cw_shape[1] * cw_shape[2]
    cw_all = cw_all.reshape(N_DEV, -1)[:, :n_cw].reshape((N_DEV,) + cw_shape)
    conv_full = jnp.transpose(cw_all, (1, 2, 0, 3)).reshape(cw_shape[0], cw_shape[1], N_DEV * cw_shape[2])

    norms = {k: weights[k] for k in ("mix_norm", "ffn_norm", "kv_norm", "mem_norm", "final_norm")}
    loss, grad_x, gW, small = _local_step(x[0], mem[0], loss_target[0], norms, conv_full, W)

    g5 = [gW[k].reshape(gW[k].shape[0], 4, 2, gW[k].shape[1] // N_DEV, gW[k].shape[2]) for k in _ORDER]
    core = ci.reshape(1).astype(jnp.int32)
    chip = (2 * xi + yi).reshape(1).astype(jnp.int32)
    from_sibling = _rs_sibling(g5)
    chip_sums = [_chip_sum(g, l, core, name=f"chip_sum_{k}") for k, g, l in zip(_ORDER, g5, from_sibling)]
    from_chips = _rs_chips(chip_sums)
    grads = {}
    for k, s, l in zip(_ORDER, chip_sums, from_chips):
        g3 = _final_sum(s, l, chip, name=f"final_sum_{k}")
        grads[_NAMES[k]] = _uncanonical(k, g3, weights[_NAMES[k]])

    order = ("mix_norm", "ffn_norm", "kv_norm", "mem_norm", "final_norm", "conv_w")
    flat = jnp.concatenate([small[k].reshape(-1) for k in order] + [loss[0, :1]])
    n_flat = flat.shape[0]
    rows = _pad_rows(flat, D)
    rows = jnp.pad(rows, ((0, (-rows.shape[0]) % 8), (0, 0)))
    total = _exchange(rows, True, name="all_reduce_small").reshape(-1)[:n_flat]
    off = 0
    for k in order:
        n = small[k].size
        grads[k] = total[off:off + n].reshape(small[k].shape)
        off += n
    loss_total = total[off]
    grads["conv_w"] = lax.dynamic_slice_in_dim(grads["conv_w"], me * cw_shape[2], cw_shape[2], axis=2)

    deltas, new_m, new_v = {}, {}, {}
    for k in _WEIGHTS:
        w = weights[k]
        two = (lambda a: a.reshape(-1, a.shape[-1])) if w.ndim > 1 else (lambda a: a.reshape(1, -1))
        d, nm, nv = _adamw(two(w), two(grads[k]), two(moments_m[k]), two(moments_v[k]), name=f"adamw_{k}")
        deltas[k], new_m[k], new_v[k] = d.reshape(w.shape), nm.reshape(w.shape), nv.reshape(w.shape)

    return (loss_total, grad_x[None], *[grads[k] for k in _WEIGHTS], *[deltas[k] for k in _WEIGHTS],
            *[new_m[k] for k in _WEIGHTS], *[new_v[k] for k in _WEIGHTS])
```

```python
import functools
import math

import jax
import jax.numpy as jnp
from jax import lax
from jax.experimental import pallas as pl
from jax.experimental.pallas import tpu as pltpu

F32 = jnp.float32
BF16 = jnp.bfloat16
MESH = pl.DeviceIdType.MESH

HEAD_DIM = 64
MEM_HEADS = 4
MEM_WIDTH = MEM_HEADS * HEAD_DIM
EPS = 1e-6
LANES = 128
BF16_ROWS = 16
VMEM_LIMIT = 56 * 1024 * 1024
N_DEV = 8

ADAM_LR = 0.001
ADAM_B1 = 0.9
ADAM_B2 = 0.999
ADAM_EPS = 1e-08
ADAM_WD = 0.01
ADAM_STEP = 10

ANY = pl.BlockSpec(memory_space=pl.ANY)


def _cparams(sem=None):
    return pltpu.CompilerParams(dimension_semantics=sem, vmem_limit_bytes=VMEM_LIMIT)


def _pick(n, cands):
    for c in cands:
        if n % c == 0:
            return c
    raise ValueError(f"no tile for {n} in {cands}")


def _mm(a, b, form, out_dtype, *, name, residual=None):
    if form == "tn":
        K, M = a.shape
    else:
        M, K = a.shape
    if form == "nt":
        N, K2 = b.shape
    else:
        K2, N = b.shape
    assert K == K2, (name, a.shape, b.shape)
    wide = (1408, 1280, 1024, 768, 512, 256, 128)
    tm = _pick(M, wide if form == "tn" else (1024, 512, 256, 128))
    tn = _pick(N, wide)
    tk = _pick(K, (1024, 1408, 1280, 768, 512, 256))
    nk = K // tk
    dims = {"nn": (((1,), (0,)), ((), ())), "nt": (((1,), (1,)), ((), ())), "tn": (((0,), (0,)), ((), ()))}[form]
    a_bytes, b_bytes = M * K * a.dtype.itemsize, N * K * b.dtype.itemsize
    n_outer = nk == 1 and (N // tn) * a_bytes + b_bytes < a_bytes + (M // tm) * b_bytes
    ij = (lambda g0, g1: (g1, g0)) if n_outer else (lambda g0, g1: (g0, g1))

    def spec(block, f):
        return pl.BlockSpec(block, lambda g0, g1, k: f(*ij(g0, g1), k))

    a_spec = spec((tk, tm), lambda i, j, k: (k, i)) if form == "tn" else spec((tm, tk), lambda i, j, k: (i, k))
    b_spec = spec((tn, tk), lambda i, j, k: (j, k)) if form == "nt" else spec((tk, tn), lambda i, j, k: (k, j))
    out_spec = spec((tm, tn), lambda i, j, k: (i, j))
    operands, in_specs = [a, b], [a_spec, b_spec]
    has_res = residual is not None
    if has_res:
        operands.append(residual)
        in_specs.append(out_spec)
    grid = (N // tn, M // tm, nk) if n_outer else (M // tm, N // tn, nk)

    def body(*refs):
        a_ref, b_ref = refs[0], refs[1]
        r_ref = refs[2] if has_res else None
        o_ref = refs[2 + int(has_res)]
        acc_ref = refs[-1]
        part = lax.dot_general(a_ref[...].astype(BF16), b_ref[...].astype(BF16), dims, preferred_element_type=F32)

        def finish(total):
            if has_res:
                total = total + r_ref[...].astype(F32)
            o_ref[...] = total.astype(out_dtype)

        if nk == 1:
            finish(part)
        else:
            k = pl.program_id(2)

            @pl.when(k == 0)
            def _():
                acc_ref[...] = part

            @pl.when(jnp.logical_and(k > 0, k < nk - 1))
            def _():
                acc_ref[...] += part

            @pl.when(k == nk - 1)
            def _():
                finish(acc_ref[...] + part)

    return pl.pallas_call(
        body, name=name, out_shape=jax.ShapeDtypeStruct((M, N), out_dtype), grid=grid, in_specs=in_specs, out_specs=out_spec,
        scratch_shapes=[pltpu.VMEM((tm, tn), F32)], compiler_params=_cparams(("parallel", "parallel", "arbitrary")),
    )(*operands)


def _rmsnorm(x, g, *, name):
    R, D = x.shape
    tr = _pick(R, (512, 256))

    def body(x_ref, g_ref, o_ref):
        xv = x_ref[...]
        r = lax.rsqrt(jnp.mean(xv * xv, axis=-1, keepdims=True) + EPS)
        o_ref[...] = (xv * r * g_ref[...]).astype(BF16)

    return pl.pallas_call(
        body, name=name, out_shape=jax.ShapeDtypeStruct((R, D), BF16), grid=(R // tr,),
        in_specs=[pl.BlockSpec((tr, D), lambda i: (i, 0)), pl.BlockSpec((1, D), lambda i: (0, 0))],
        out_specs=pl.BlockSpec((tr, D), lambda i: (i, 0)), compiler_params=_cparams(("parallel",)),
    )(x, g)


def _rmsnorm_bwd(x, g, dh, dx_in, *, name):
    R, D = x.shape
    tr = _pick(R, (512, 256))
    has_in = dx_in is not None

    def body(*refs):
        x_ref, g_ref, dh_ref = refs[:3]
        dxi_ref = refs[3] if has_in else None
        dx_ref, dxb_ref, dg_ref = refs[3 + int(has_in):]
        xv = x_ref[...]
        r = lax.rsqrt(jnp.mean(xv * xv, axis=-1, keepdims=True) + EPS)
        xhat = xv * r
        dhv = dh_ref[...].astype(F32)
        dxh = dhv * g_ref[...]
        dx = r * (dxh - xhat * jnp.mean(dxh * xhat, axis=-1, keepdims=True))
        if has_in:
            dx = dx + dxi_ref[...]
        dx_ref[...] = dx
        dxb_ref[...] = dx.astype(BF16)
        part = jnp.sum(dhv * xhat, axis=0, keepdims=True)

        @pl.when(pl.program_id(0) == 0)
        def _():
            dg_ref[...] = part

        @pl.when(pl.program_id(0) > 0)
        def _():
            dg_ref[...] += part

    row = pl.BlockSpec((tr, D), lambda i: (i, 0))
    vec = pl.BlockSpec((1, D), lambda i: (0, 0))
    ops = [x, g, dh] + ([dx_in] if has_in else [])
    return pl.pallas_call(
        body, name=name,
        out_shape=(jax.ShapeDtypeStruct((R, D), F32), jax.ShapeDtypeStruct((R, D), BF16), jax.ShapeDtypeStruct((1, D), F32)),
        grid=(R // tr,), in_specs=[row, vec, row] + ([row] if has_in else []), out_specs=(row, row, vec),
        compiler_params=_cparams(("arbitrary",)),
    )(*ops)


def _loss_head(x, g, target, *, name):
    R, D = x.shape
    tr = _pick(R, (512, 256))

    def body(x_ref, g_ref, t_ref, dx_ref, dxb_ref, dg_ref, loss_ref):
        xv = x_ref[...]
        gv = g_ref[...]
        r = lax.rsqrt(jnp.mean(xv * xv, axis=-1, keepdims=True) + EPS)
        xhat = xv * r
        err = xhat * gv - t_ref[...]
        loss = 0.5 * jnp.sum(jnp.mean(err * err, axis=-1, keepdims=True), axis=0, keepdims=True)
        dy = err * (1.0 / D)
        dxh = dy * gv
        dx = r * (dxh - xhat * jnp.mean(dxh * xhat, axis=-1, keepdims=True))
        dx_ref[...] = dx
        dxb_ref[...] = dx.astype(BF16)
        dg = jnp.sum(dy * xhat, axis=0, keepdims=True)
        lossv = jnp.broadcast_to(loss, (1, LANES))

        @pl.when(pl.program_id(0) == 0)
        def _():
            dg_ref[...] = dg
            loss_ref[...] = lossv

        @pl.when(pl.program_id(0) > 0)
        def _():
            dg_ref[...] += dg
            loss_ref[...] += lossv

    row = pl.BlockSpec((tr, D), lambda i: (i, 0))
    vec = pl.BlockSpec((1, D), lambda i: (0, 0))
    return pl.pallas_call(
        body, name=name,
        out_shape=(jax.ShapeDtypeStruct((R, D), F32), jax.ShapeDtypeStruct((R, D), BF16), jax.ShapeDtypeStruct((1, D), F32),
                   jax.ShapeDtypeStruct((1, LANES), F32)),
        grid=(R // tr,), in_specs=[row, vec, row], out_specs=(row, row, vec, pl.BlockSpec((1, LANES), lambda i: (0, 0))),
        compiler_params=_cparams(("arbitrary",)),
    )(x, g, target)


def _conv_taps(gv, S):
    t = lax.broadcasted_iota(jnp.int32, gv.shape, 0)
    g1 = jnp.where(t >= 1, pltpu.roll(gv, 1, 0), 0.0)
    g2 = jnp.where(t >= 2, pltpu.roll(gv, 2, 0), 0.0)
    return g1, g2


def _conv_fwd(p, w, main, *, name):
    S = p.shape[0]
    tc = LANES
    nb = main // tc

    def body(b_ref, c_ref, u_ref, w_ref, y_ref):
        gv = c_ref[...].astype(F32) * u_ref[...].astype(F32)
        g1, g2 = _conv_taps(gv, S)
        cv = w_ref[0:1, :] * g2 + w_ref[1:2, :] * g1 + w_ref[2:3, :] * gv
        y_ref[...] = (b_ref[...].astype(F32) * cv).astype(BF16)

    col = lambda off: pl.BlockSpec((S, tc), lambda j: (0, off + j))
    return pl.pallas_call(
        body, name=name, out_shape=jax.ShapeDtypeStruct((S, main), BF16), grid=(nb,),
        in_specs=[col(0), col(nb), col(2 * nb), pl.BlockSpec((3, tc), lambda j: (0, j))],
        out_specs=pl.BlockSpec((S, tc), lambda j: (0, j)), compiler_params=_cparams(("parallel",)),
    )(p, p, p, w)


def _conv_bwd(p, w, dy, main, *, name):
    S = p.shape[0]
    tc = LANES
    nb = main // tc

    def body(b_ref, c_ref, u_ref, w_ref, dy_ref, db_ref, dc_ref, du_ref, dw_ref):
        cvv, uv = c_ref[...].astype(F32), u_ref[...].astype(F32)
        gv = cvv * uv
        g1, g2 = _conv_taps(gv, S)
        w0, w1, w2 = w_ref[0:1, :], w_ref[1:2, :], w_ref[2:3, :]
        dyv = dy_ref[...].astype(F32)
        db_ref[...] = (dyv * (w0 * g2 + w1 * g1 + w2 * gv)).astype(BF16)
        dcv = dyv * b_ref[...].astype(F32)
        t = lax.broadcasted_iota(jnp.int32, dcv.shape, 0)
        n1 = jnp.where(t <= S - 2, pltpu.roll(dcv, S - 1, 0), 0.0)
        n2 = jnp.where(t <= S - 3, pltpu.roll(dcv, S - 2, 0), 0.0)
        dg = w2 * dcv + w1 * n1 + w0 * n2
        dc_ref[...] = (dg * uv).astype(BF16)
        du_ref[...] = (dg * cvv).astype(BF16)
        dw_ref[0:1, :] = jnp.sum(dcv * g2, axis=0, keepdims=True)
        dw_ref[1:2, :] = jnp.sum(dcv * g1, axis=0, keepdims=True)
        dw_ref[2:3, :] = jnp.sum(dcv * gv, axis=0, keepdims=True)

    col = lambda off: pl.BlockSpec((S, tc), lambda j: (0, off + j))
    out = jax.ShapeDtypeStruct((S, main), BF16)
    return pl.pallas_call(
        body, name=name, out_shape=(out, out, out, jax.ShapeDtypeStruct((3, main), F32)), grid=(nb,),
        in_specs=[col(0), col(nb), col(2 * nb), pl.BlockSpec((3, tc), lambda j: (0, j)), col(0)],
        out_specs=(col(0), col(0), col(0), pl.BlockSpec((3, tc), lambda j: (0, j))),
        compiler_params=_cparams(("parallel",)),
    )(p, p, p, w, dy)


def _head_mask(width, h):
    lane = lax.broadcasted_iota(jnp.int32, (1, width), 1)
    return jnp.logical_and(lane >= h * HEAD_DIM, lane < (h + 1) * HEAD_DIM)


_NT = (((1,), (1,)), ((), ()))
_NN = (((1,), (0,)), ((), ()))
_TN = (((0,), (0,)), ((), ()))


def _dot(a, b, dims):
    return lax.dot_general(a, b, dims, preferred_element_type=F32)


def _mem_probs(qh, kv):
    s = _dot(qh, kv, _NT) * (1.0 / math.sqrt(HEAD_DIM))
    s = s - jnp.max(s, axis=-1, keepdims=True)
    e = jnp.exp(s)
    return e / jnp.sum(e, axis=-1, keepdims=True)


def _memattn_fwd(p, qblk, mkv, *, name):
    S = p.shape[0]
    M = mkv.shape[0]
    W = MEM_WIDTH
    tq = _pick(S, (512, 256))

    def body(q_ref, k_ref, v_ref, o_ref):
        q = q_ref[...].astype(BF16)
        kv, vv = k_ref[...], v_ref[...]
        out = jnp.zeros((tq, W), F32)
        for h in range(MEM_HEADS):
            m = _head_mask(W, h)
            pr = _mem_probs(jnp.where(m, q, jnp.zeros_like(q)), kv)
            out = jnp.where(m, _dot(pr.astype(BF16), vv, _NN), out)
        o_ref[...] = out.astype(BF16)

    return pl.pallas_call(
        body, name=name, out_shape=jax.ShapeDtypeStruct((S, W), BF16), grid=(S // tq,),
        in_specs=[pl.BlockSpec((tq, W), lambda i: (i, qblk)), pl.BlockSpec((M, W), lambda i: (0, 0)),
                  pl.BlockSpec((M, W), lambda i: (0, 1))],
        out_specs=pl.BlockSpec((tq, W), lambda i: (i, 0)), compiler_params=_cparams(("parallel",)),
    )(p, mkv, mkv)


def _memattn_bwd(p, qblk, mkv, dy, dyblk, *, name):
    S = p.shape[0]
    M = mkv.shape[0]
    W = MEM_WIDTH
    tq = _pick(S, (512, 256))
    scale = 1.0 / math.sqrt(HEAD_DIM)

    def body(q_ref, k_ref, v_ref, do_ref, dq_ref, dk_ref, dv_ref, dk_acc, dv_acc):
        q = q_ref[...].astype(BF16)
        do = do_ref[...].astype(BF16)
        kv, vv = k_ref[...], v_ref[...]
        dq = jnp.zeros((tq, W), F32)
        dk = jnp.zeros((M, W), F32)
        dv = jnp.zeros((M, W), F32)
        for h in range(MEM_HEADS):
            m = _head_mask(W, h)
            qh = jnp.where(m, q, jnp.zeros_like(q))
            doh = jnp.where(m, do, jnp.zeros_like(do))
            pr = _mem_probs(qh, kv)
            dpr = _dot(doh, vv, _NT)
            ds = (pr * (dpr - jnp.sum(dpr * pr, axis=-1, keepdims=True)) * scale).astype(BF16)
            dq = jnp.where(m, _dot(ds, kv, _NN), dq)
            dk = dk + _dot(ds, qh, _TN)
            dv = dv + _dot(pr.astype(BF16), doh, _TN)
        dq_ref[...] = dq.astype(BF16)
        i = pl.program_id(0)

        @pl.when(i == 0)
        def _():
            dk_acc[...] = dk
            dv_acc[...] = dv

        @pl.when(i > 0)
        def _():
            dk_acc[...] += dk
            dv_acc[...] += dv

        @pl.when(i == S // tq - 1)
        def _():
            dk_ref[...] = dk_acc[...].astype(BF16)
            dv_ref[...] = dv_acc[...].astype(BF16)

    kspec = lambda c: pl.BlockSpec((M, W), lambda i: (0, c))
    return pl.pallas_call(
        body, name=name,
        out_shape=(jax.ShapeDtypeStruct((S, W), BF16), jax.ShapeDtypeStruct((M, W), BF16), jax.ShapeDtypeStruct((M, W), BF16)),
        grid=(S // tq,),
        in_specs=[pl.BlockSpec((tq, W), lambda i: (i, qblk)), kspec(0), kspec(1), pl.BlockSpec((tq, W), lambda i: (i, dyblk))],
        out_specs=(pl.BlockSpec((tq, W), lambda i: (i, 0)), kspec(0), kspec(0)),
        scratch_shapes=[pltpu.VMEM((M, W), F32), pltpu.VMEM((M, W), F32)],
        compiler_params=_cparams(("arbitrary",)),
    )(p, mkv, mkv, dy)


SB_TQ = 256
SB_CLAMP = 80.0


SB_CHUNK = 64


def _by_rows(fn, *arrays):
    rows = next(a for a in arrays if a is not None).shape[0]
    outs = [fn(*[None if a is None else a[r0:r0 + SB_CHUNK] for a in arrays]) for r0 in range(0, rows, SB_CHUNK)]
    return tuple(jnp.concatenate(col, axis=0) for col in zip(*outs))


def _sb_scores(qh, kb, causal):
    def chain(z, mask):
        z = jnp.clip(z, -SB_CLAMP, SB_CLAMP)
        w = 1.0 + jnp.exp(z)
        sp = jnp.log(w)
        zs = z - sp
        if mask is not None:
            sp = jnp.where(mask, sp, 0.0)
            zs = jnp.where(mask, zs, -1e30)
            w = jnp.where(mask, w, 1.0)
        return zs, sp.astype(BF16), jnp.sum(sp, axis=1, keepdims=True), w

    return _by_rows(chain, _dot(qh, kb, _NT), causal)


def _sb_weights(zs, spb, tri, carry):
    return _by_rows(lambda zs_c, t_c, c_c: (jnp.exp(zs_c - (t_c + c_c)).astype(BF16),), zs, _dot(spb, tri, _NN), carry)[0]


def _stack_heads(v, m0):
    zero = jnp.zeros_like(v)
    return jnp.concatenate([jnp.where(m0, v, zero), jnp.where(m0, zero, v)], axis=0)


def _stacked_causal(tq):
    r = lax.broadcasted_iota(jnp.int32, (2 * tq, tq), 0)
    c = lax.broadcasted_iota(jnp.int32, (2 * tq, tq), 1)
    return c < jnp.where(r >= tq, r - tq, r)


def _sb_fwd(p, kv, heads, *, name):
    S = p.shape[0]
    tq = SB_TQ
    npair = heads // 2

    def body(q_ref, k_ref, v_ref, o_ref, o32_ref):
        qi = pl.program_id(1)
        r = lax.broadcasted_iota(jnp.int32, (tq, tq), 0)
        c = lax.broadcasted_iota(jnp.int32, (tq, tq), 1)
        tri = (r > c).astype(BF16)
        causal = _stacked_causal(tq)
        m0 = _head_mask(LANES, 0)
        qh = _stack_heads(q_ref[...] * jnp.asarray(1.0 / math.sqrt(HEAD_DIM), BF16), m0)

        def block(j, carry, acc, mask):
            off = pl.multiple_of(j * tq, tq)
            kb = k_ref[pl.ds(off, tq), :]
            vb = v_ref[pl.ds(off, tq), :]
            zs, spb, sp_sum, _ = _sb_scores(qh, kb, mask)
            acc = acc + _dot(_sb_weights(zs, spb, tri, carry), vb, _NN)
            return carry + sp_sum, acc

        st = block(qi, jnp.zeros((2 * tq, 1), F32), jnp.zeros((2 * tq, LANES), F32), causal)
        odd = qi % 2
        st = lax.cond(odd == 1, lambda s: block(qi - 1, s[0], s[1], None), lambda s: s, st)

        def pair(it, s):
            j = qi - 1 - odd - 2 * it
            s = block(j, s[0], s[1], None)
            return block(j - 1, s[0], s[1], None)

        carry, acc = lax.fori_loop(0, qi // 2, pair, st)
        out = jnp.where(m0, acc[:tq], acc[tq:])
        o_ref[...] = out.astype(BF16)
        o32_ref[...] = out

    W = heads * HEAD_DIM
    qspec = pl.BlockSpec((tq, LANES), lambda hp, i: (i, hp))
    return pl.pallas_call(
        body, name=name, out_shape=(jax.ShapeDtypeStruct((S, W), BF16), jax.ShapeDtypeStruct((S, W), F32)), grid=(npair, S // tq),
        in_specs=[qspec, pl.BlockSpec((S, LANES), lambda hp, i: (0, hp)), pl.BlockSpec((S, LANES), lambda hp, i: (0, npair + hp))],
        out_specs=(qspec, qspec), compiler_params=_cparams(("parallel", "arbitrary")),
    )(p, kv, kv)


def _sb_bwd(p, kv, o32, dy, heads, dk_in, dv_in, *, name):
    S = p.shape[0]
    tq = SB_TQ
    npair = heads // 2
    has_in = dk_in is not None
    scale = 1.0 / math.sqrt(HEAD_DIM)

    def body(*refs):
        q_ref, k_ref, v_ref, o_ref, do_ref = refs[:5]
        dq_ref, dk_ref, dv_ref = refs[5 + 2 * int(has_in):]
        qi = pl.program_id(1)

        @pl.when(qi == 0)
        def _():
            if has_in:
                dk_ref[...] = refs[5][...]
                dv_ref[...] = refs[6][...]
            else:
                dk_ref[...] = jnp.zeros_like(dk_ref)
                dv_ref[...] = jnp.zeros_like(dv_ref)

        r = lax.broadcasted_iota(jnp.int32, (tq, tq), 0)
        c = lax.broadcasted_iota(jnp.int32, (tq, tq), 1)
        tri = (r > c).astype(BF16)
        tri_low = (r < c).astype(BF16)
        causal = _stacked_causal(tq)
        m0 = _head_mask(LANES, 0)
        qh = _stack_heads(q_ref[...] * jnp.asarray(scale, BF16), m0)
        do = do_ref[...]
        doh = _stack_heads(do, m0)
        dov = do.astype(F32) * o_ref[...]
        dsum = jnp.concatenate([jnp.sum(jnp.where(m0, dov, 0.0), axis=1, keepdims=True),
                                jnp.sum(jnp.where(m0, 0.0, dov), axis=1, keepdims=True)], axis=0)

        def block(j, carry, gcarry, acc, mask):
            off = pl.multiple_of(j * tq, tq)
            kb = k_ref[pl.ds(off, tq), :]
            vb = v_ref[pl.ds(off, tq), :]
            zs, spb, sp_sum, w = _sb_scores(qh, kb, mask)
            ab = _sb_weights(zs, spb, tri, carry)

            def grads(ab_c, da_c):
                g = ab_c.astype(F32) * da_c
                return g, g.astype(BF16), jnp.sum(g, axis=1, keepdims=True)

            g, gb, g_sum = _by_rows(grads, ab, _dot(doh, vb, _NT))
            gcarry = gcarry + g_sum

            def logit_grads(g_c, w_c, low_c, left_c):
                rinv = 1.0 / w_c
                return ((g_c * rinv - (left_c + low_c) * (1.0 - rinv)).astype(BF16),)

            dzs = _by_rows(logit_grads, g, w, _dot(gb, tri_low, _NN), dsum - gcarry)[0]
            acc = acc + _dot(dzs, kb, _NN)
            dk_ref[pl.ds(off, tq), :] += _dot(dzs, qh, _TN)
            dv_ref[pl.ds(off, tq), :] += _dot(ab, doh, _TN)
            return (carry + sp_sum, gcarry, acc)

        zero = jnp.zeros((2 * tq, 1), F32)
        st = block(qi, zero, zero, jnp.zeros((2 * tq, LANES), F32), causal)
        odd = qi % 2
        st = lax.cond(odd == 1, lambda s: block(qi - 1, s[0], s[1], s[2], None), lambda s: s, st)

        def pair(it, s):
            j = qi - 1 - odd - 2 * it
            s = block(j, s[0], s[1], s[2], None)
            return block(j - 1, s[0], s[1], s[2], None)

        st = lax.fori_loop(0, qi // 2, pair, st)
        dq_ref[...] = (jnp.where(m0, st[2][:tq], st[2][tq:]) * scale).astype(BF16)

    W = heads * HEAD_DIM
    qspec = pl.BlockSpec((tq, LANES), lambda hp, i: (i, hp))
    seq = lambda off: pl.BlockSpec((S, LANES), lambda hp, i: (0, off + hp))
    ops = [p, kv, kv, o32, dy] + ([dk_in, dv_in] if has_in else [])
    return pl.pallas_call(
        body, name=name,
        out_shape=(jax.ShapeDtypeStruct((S, W), BF16), jax.ShapeDtypeStruct((S, W), F32), jax.ShapeDtypeStruct((S, W), F32)),
        grid=(npair, S // tq),
        in_specs=[qspec, seq(0), seq(npair), qspec, qspec] + ([seq(0), seq(0)] if has_in else []),
        out_specs=(qspec, seq(0), seq(0)),
        compiler_params=_cparams(("parallel", "arbitrary")),
    )(*ops)


def _ffn_up(h, wg, wu, *, name):
    S, D = h.shape
    F = wg.shape[0]
    tm = _pick(S, (512, 256))
    tn = _pick(F, (1408, 1024, 512, 256, 128))

    def body(h_ref, g_ref, u_ref, gate_ref, up_ref, act_ref):
        hv = h_ref[...]
        g = _dot(hv, g_ref[...], _NT)
        u = _dot(hv, u_ref[...], _NT)
        gate_ref[...] = g.astype(BF16)
        up_ref[...] = u.astype(BF16)
        act_ref[...] = (g * jax.nn.sigmoid(g) * u).astype(BF16)

    wspec = pl.BlockSpec((tn, D), lambda j, i: (j, 0))
    ospec = pl.BlockSpec((tm, tn), lambda j, i: (i, j))
    out = jax.ShapeDtypeStruct((S, F), BF16)
    return pl.pallas_call(
        body, name=name, out_shape=(out, out, out), grid=(F // tn, S // tm),
        in_specs=[pl.BlockSpec((tm, D), lambda j, i: (i, 0)), wspec, wspec], out_specs=(ospec, ospec, ospec),
        compiler_params=_cparams(("parallel", "parallel")),
    )(h, wg, wu)


def _ffn_down_bwd(dx, wd, gate, up, *, name):
    S, D = dx.shape
    F = wd.shape[0]
    tm = _pick(S, (512, 256))
    tn = _pick(F, (1408, 1024, 512, 256, 128))

    def body(dx_ref, w_ref, g_ref, u_ref, dg_ref, du_ref):
        da = _dot(dx_ref[...], w_ref[...], _NT)
        gv, uv = g_ref[...].astype(F32), u_ref[...].astype(F32)
        s = jax.nn.sigmoid(gv)
        silu = gv * s
        dg_ref[...] = (da * uv * (s + silu * (1.0 - s))).astype(BF16)
        du_ref[...] = (da * silu).astype(BF16)

    ospec = pl.BlockSpec((tm, tn), lambda j, i: (i, j))
    out = jax.ShapeDtypeStruct((S, F), BF16)
    return pl.pallas_call(
        body, name=name, out_shape=(out, out), grid=(F // tn, S // tm),
        in_specs=[pl.BlockSpec((tm, D), lambda j, i: (i, 0)), pl.BlockSpec((tn, D), lambda j, i: (j, 0)), ospec, ospec],
        out_specs=(ospec, ospec), compiler_params=_cparams(("parallel", "parallel")),
    )(dx, wd, gate, up)


def _adamw(w, g, m, v, *, name):
    R, C = w.shape
    tr = R
    for cand in (1024, 512, 256, 128, 64, 32, 16, 8):
        if R % cand == 0 and cand * C * 4 <= (1 << 20):
            tr = cand
            break
    bc1 = 1.0 - ADAM_B1 ** ADAM_STEP
    bc2 = 1.0 - ADAM_B2 ** ADAM_STEP

    def body(w_ref, g_ref, m_ref, v_ref, d_ref, nm_ref, nv_ref):
        gv = g_ref[...]
        nm = ADAM_B1 * m_ref[...] + (1.0 - ADAM_B1) * gv
        nv = ADAM_B2 * v_ref[...] + (1.0 - ADAM_B2) * (gv * gv)
        nm_ref[...] = nm
        nv_ref[...] = nv
        d_ref[...] = -ADAM_LR * ((nm / bc1) / (jnp.sqrt(nv / bc2) + ADAM_EPS) + ADAM_WD * w_ref[...])

    blk = pl.BlockSpec((tr, C), lambda i: (i, 0))
    out = jax.ShapeDtypeStruct((R, C), F32)
    return pl.pallas_call(body, name=name, out_shape=(out, out, out), grid=(R // tr,), in_specs=[blk] * 4,
                          out_specs=(blk, blk, blk), compiler_params=_cparams(("parallel",)))(w, g, m, v)


def _place():
    x, y, c = lax.axis_index("x"), lax.axis_index("y"), lax.axis_index("c")
    return x, y, c


def _all_gather_weights(shards, *, name):
    n = len(shards)

    def body(*refs):
        sh, full = refs[:n], refs[n:2 * n]
        send_sems, recv_sems, local_sems = refs[2 * n:]
        x, y, c = _place()
        me, sibling = (x, y, c), (x, y, 1 - c)
        chips = [(1 - x, y), (x, 1 - y), (1 - x, 1 - y)]

        def rows(t, px, py, pc):
            r = sh[t].shape[1]
            return full[t].at[:, pl.ds(pl.multiple_of((4 * px + 2 * py + pc) * r, BF16_ROWS), r), :]

        def copy(t, k, block, to, src=None):
            return pltpu.make_async_remote_copy(
                src_ref=rows(t, *block) if src is None else src, dst_ref=rows(t, *block),
                send_sem=send_sems.at[7 * t + k], recv_sem=recv_sems.at[7 * t + k], device_id=to, device_id_type=MESH)

        started = []
        for t in range(n):
            mine = pltpu.make_async_copy(sh[t], rows(t, *me), local_sems.at[t])
            mine.start()
            started.append(mine)
        sends = []
        for t in range(n):
            first = [copy(t, 0, me, sibling, src=sh[t])]
            first += [copy(t, 1 + j, me, (*chip, c), src=sh[t]) for j, chip in enumerate(chips)]
            for cp in first:
                cp.start()
            sends += first
        for t in range(n):
            for j, chip in enumerate(chips):
                copy(t, 1 + j, (*chip, c), me).wait_recv()
                fwd = copy(t, 4 + j, (*chip, c), sibling)
                fwd.start()
                sends.append(fwd)
        for t in range(n):
            copy(t, 0, sibling, me).wait_recv()
            for j, chip in enumerate(chips):
                copy(t, 4 + j, (*chip, 1 - c), me).wait_recv()
        for cp in sends:
            cp.wait_send()
        for cp in started:
            cp.wait()

    out_shape = [jax.ShapeDtypeStruct((s.shape[0], N_DEV * s.shape[1], s.shape[2]), s.dtype) for s in shards]
    return pl.pallas_call(
        body, name=name, out_shape=out_shape, in_specs=[ANY] * n, out_specs=[ANY] * n,
        scratch_shapes=[pltpu.SemaphoreType.DMA((7 * n,)), pltpu.SemaphoreType.DMA((7 * n,)), pltpu.SemaphoreType.DMA((n,))],
    )(*shards)


def _whole(ref_a, ref_b, send_sem, recv_sem, me):
    return pltpu.make_async_remote_copy(src_ref=ref_a, dst_ref=ref_b, send_sem=send_sem, recv_sem=recv_sem,
                                        device_id=me, device_id_type=MESH)


def _rs_sibling(grads, *, name):
    n = len(grads)

    def body(*refs):
        g, land = refs[:n], refs[n:2 * n]
        send_sems, recv_sems = refs[2 * n:]
        x, y, c = _place()
        for t in range(n):
            for k in range(4):
                pltpu.make_async_remote_copy(
                    src_ref=g[t].at[:, k, 1 - c], dst_ref=land[t].at[k], send_sem=send_sems.at[t], recv_sem=recv_sems.at[t],
                    device_id=(x, y, 1 - c), device_id_type=MESH).start()
        for t in range(n):
            w = _whole(land[t], land[t], send_sems.at[t], recv_sems.at[t], (x, y, c))
            w.wait_send()
            w.wait_recv()

    out_shape = [jax.ShapeDtypeStruct((4, s.shape[0], s.shape[3], s.shape[4]), s.dtype) for s in grads]
    return pl.pallas_call(
        body, name=name, out_shape=out_shape, in_specs=[ANY] * n, out_specs=[ANY] * n,
        scratch_shapes=[pltpu.SemaphoreType.DMA((n,)), pltpu.SemaphoreType.DMA((n,))],
    )(*grads)


def _rs_chips(sums, *, name):
    n = len(sums)

    def body(*refs):
        s, land = refs[:n], refs[n:2 * n]
        send_sems, recv_sems = refs[2 * n:]
        x, y, c = _place()
        chips = [(1 - x, y), (x, 1 - y), (1 - x, 1 - y)]
        for t in range(n):
            for j, (px, py) in enumerate(chips):
                pltpu.make_async_remote_copy(
                    src_ref=s[t].at[2 * px + py], dst_ref=land[t].at[j], send_sem=send_sems.at[t], recv_sem=recv_sems.at[t],
                    device_id=(px, py, c), device_id_type=MESH).start()
        for t in range(n):
            w = _whole(land[t], land[t], send_sems.at[t], recv_sems.at[t], (x, y, c))
            w.wait_send()
            w.wait_recv()

    out_shape = [jax.ShapeDtypeStruct((3,) + s.shape[1:], s.dtype) for s in sums]
    return pl.pallas_call(
        body, name=name, out_shape=out_shape, in_specs=[ANY] * n, out_specs=[ANY] * n,
        scratch_shapes=[pltpu.SemaphoreType.DMA((n,)), pltpu.SemaphoreType.DMA((n,))],
    )(*sums)


_HBM = pl.BlockSpec(memory_space=pltpu.HBM)
_SEM = pl.BlockSpec(memory_space=pltpu.SEMAPHORE)
_SIDE_EFFECT = pltpu.CompilerParams(has_side_effects=pltpu.SideEffectType.DATAFLOW_SIDE_EFFECTING)


def _in_hbm(arrays):
    return [pltpu.with_memory_space_constraint(a, pltpu.HBM) for a in arrays]


def _hbm_like(arrays):
    return [pltpu.HBM(a.shape, a.dtype) for a in arrays]


def _split_start(body, groups, srcs, dsts, after, *, name):
    n = len(srcs)
    extra = [] if after is None else [after]

    def kernel_body(*refs):
        sems = refs[2 * n + len(extra):2 * n + len(extra) + 2 * len(groups)]
        body(refs[:n], refs[n:2 * n], sems[0::2], sems[1::2])
        refs[-1][...] = jnp.zeros_like(refs[-1])

    sem_shapes = [pltpu.SemaphoreType.DMA((g,)) for g in groups for _ in range(2)]
    outs = pl.pallas_call(
        kernel_body, name=name,
        out_shape=(*sem_shapes, *_hbm_like(srcs), *_hbm_like(dsts), jax.ShapeDtypeStruct((8, LANES), F32)),
        in_specs=[_HBM] * (2 * n) + [ANY] * len(extra),
        out_specs=(*[_SEM] * len(sem_shapes), *[_HBM] * (2 * n), pl.BlockSpec(memory_space=pltpu.VMEM)),
        input_output_aliases={i: len(sem_shapes) + i for i in range(2 * n)}, compiler_params=_SIDE_EFFECT,
    )(*_in_hbm(srcs), *_in_hbm(dsts), *extra)
    ns = len(sem_shapes)
    sems = [(outs[2 * g], outs[2 * g + 1]) for g in range(len(groups))]
    return sems, list(outs[ns:ns + n]), list(outs[ns + n:ns + 2 * n]), outs[-1]


def _split_wait(whole, srcs, dsts, send_sems, recv_sems, after, *, name):
    n = len(srcs)

    def kernel_body(*refs):
        dst, send, recv = refs[n:2 * n], refs[2 * n], refs[2 * n + 1]
        x, y, c = _place()
        for t in range(n):
            ref = whole(dst[t])
            w = _whole(ref, ref, send.at[t], recv.at[t], (x, y, c))
            w.wait_send()
            w.wait_recv()

    outs = pl.pallas_call(
        kernel_body, name=name, out_shape=(*_hbm_like(srcs), *_hbm_like(dsts)),
        in_specs=[_HBM] * (2 * n) + [_SEM, _SEM, ANY], out_specs=[_HBM] * (2 * n),
        input_output_aliases={i: i for i in range(2 * n)}, compiler_params=_SIDE_EFFECT,
    )(*srcs, *dsts, send_sems, recv_sems, after)
    return list(outs[:n]), list(outs[n:])


def _gather_start(groups, after, *, name):
    shards = [s for g in groups for s, _ in g]
    fulls = [f for g in groups for _, f in g]

    def body(sh, full, send_sems, recv_sems):
        x, y, c = _place()
        me = 4 * x + 2 * y + c
        t = 0
        for gi, g in enumerate(groups):
            for ti in range(len(g)):
                r = sh[t].shape[0]
                dst = full[t].at[pl.ds(pl.multiple_of(me * r, BF16_ROWS), r), :]
                for k in (1, 2, 4, 6, 3, 5, 7):
                    peer = (1 - x if k & 4 else x, 1 - y if k & 2 else y, 1 - c if k & 1 else c)
                    pltpu.make_async_remote_copy(src_ref=sh[t], dst_ref=dst, send_sem=send_sems[gi].at[ti],
                                                 recv_sem=recv_sems[gi].at[ti], device_id=peer, device_id_type=MESH).start()
                t += 1

    sems, shards, fulls, token = _split_start(body, [len(g) for g in groups], shards, fulls, after, name=name)
    out, t = [], 0
    for g, sem in zip(groups, sems):
        out.append((sem, shards[t:t + len(g)], fulls[t:t + len(g)]))
        t += len(g)
    return out, token


def _gather_wait(pending, after, *, name):
    (send_sems, recv_sems), shards, fulls = pending
    seven = lambda full: full.at[pl.ds(0, 7 * (full.shape[0] // N_DEV)), :]
    return _split_wait(seven, shards, fulls, send_sems, recv_sems, after, name=name)


def _chips_start(sums, after, *, name):
    lands = [lax.empty((3,) + s.shape[1:], s.dtype) for s in sums]

    def body(s, land, send_sems, recv_sems):
        x, y, c = _place()
        for t in range(len(sums)):
            for j, (px, py) in enumerate([(1 - x, y), (x, 1 - y), (1 - x, 1 - y)]):
                pltpu.make_async_remote_copy(src_ref=s[t].at[2 * px + py], dst_ref=land[t].at[j], send_sem=send_sems[0].at[t],
                                             recv_sem=recv_sems[0].at[t], device_id=(px, py, c), device_id_type=MESH).start()

    sems, sums, lands, token = _split_start(body, [len(sums)], sums, lands, after, name=name)
    return (sems[0], sums, lands), token


def _chips_wait(pending, after, *, name):
    (send_sems, recv_sems), sums, lands = pending
    return _split_wait(lambda land: land, sums, lands, send_sems, recv_sems, after, name=name)


def _chip_sum(g, land, core, *, name):
    L, _, _, r, C = g.shape

    def body(core_ref, g_ref, l_ref, o_ref):
        o_ref[...] = (g_ref[...].astype(F32) + l_ref[...].astype(F32)).astype(BF16)

    grid_spec = pltpu.PrefetchScalarGridSpec(
        num_scalar_prefetch=1, grid=(4, L),
        in_specs=[pl.BlockSpec((None, None, None, r, C), lambda k, l, core_ref: (l, k, core_ref[0], 0, 0)),
                  pl.BlockSpec((None, None, r, C), lambda k, l, core_ref: (k, l, 0, 0))],
        out_specs=pl.BlockSpec((None, None, r, C), lambda k, l, core_ref: (k, l, 0, 0)))
    return pl.pallas_call(body, name=name, out_shape=jax.ShapeDtypeStruct((4, L, r, C), BF16), grid_spec=grid_spec,
                          compiler_params=_cparams(("parallel", "parallel")))(core, g, land)


def _final_sum(sums, land, chip, *, name):
    _, L, r, C = sums.shape

    def body(chip_ref, s_ref, a_ref, b_ref, c_ref, o_ref):
        o_ref[...] = ((s_ref[...].astype(F32) + a_ref[...].astype(F32)) + b_ref[...].astype(F32)) + c_ref[...].astype(F32)

    slot = lambda j: pl.BlockSpec((None, None, r, C), lambda l, chip_ref: (j, l, 0, 0))
    grid_spec = pltpu.PrefetchScalarGridSpec(
        num_scalar_prefetch=1, grid=(L,),
        in_specs=[pl.BlockSpec((None, None, r, C), lambda l, chip_ref: (chip_ref[0], l, 0, 0)), slot(0), slot(1), slot(2)],
        out_specs=pl.BlockSpec((None, r, C), lambda l, chip_ref: (l, 0, 0)))
    return pl.pallas_call(body, name=name, out_shape=jax.ShapeDtypeStruct((L, r, C), F32), grid_spec=grid_spec,
                          compiler_params=_cparams(("parallel",)))(chip, sums, land, land, land)


def _exchange(v, reduce, *, name):
    R, C = v.shape

    def body(v_ref, o_ref, *scratch):
        if reduce:
            buf, send_sems, recv_sems = scratch
        else:
            buf = o_ref
            send_sems, recv_sems = scratch
        x, y, c = _place()
        me = 4 * x + 2 * y + c
        buf[me] = v_ref[...]
        copies = []
        for k in range(1, N_DEV):
            kx, ky, kc = (k >> 2) & 1, (k >> 1) & 1, k & 1
            peer = (1 - x if kx else x, 1 - y if ky else y, 1 - c if kc else c)
            cp = pltpu.make_async_remote_copy(src_ref=v_ref, dst_ref=buf.at[me], send_sem=send_sems.at[k - 1],
                                              recv_sem=recv_sems.at[k - 1], device_id=peer, device_id_type=MESH)
            cp.start()
            copies.append(cp)
        for cp in copies:
            cp.wait_recv()
        for cp in copies:
            cp.wait_send()
        if reduce:
            acc = buf[0]
            for d in range(1, N_DEV):
                acc = acc + buf[d]
            o_ref[...] = acc

    sems = [pltpu.SemaphoreType.DMA((N_DEV - 1,)), pltpu.SemaphoreType.DMA((N_DEV - 1,))]
    vm = pl.BlockSpec(memory_space=pltpu.VMEM)
    if reduce:
        return pl.pallas_call(body, name=name, out_shape=jax.ShapeDtypeStruct((R, C), F32), in_specs=[vm], out_specs=vm,
                              scratch_shapes=[pltpu.VMEM((N_DEV, R, C), F32)] + sems)(v)
    return pl.pallas_call(body, name=name, out_shape=jax.ShapeDtypeStruct((N_DEV, R, C), F32), in_specs=[vm], out_specs=vm,
                          scratch_shapes=sems)(v)


def _local_step(x, mem, target, norms, conv_w, depth, n_a, get_w, put_g):
    S, D = x.shape
    main = D - MEM_WIDTH
    heads = main // HEAD_DIM
    row = lambda v: v.reshape(1, D)

    mem_n = _rmsnorm(mem, row(norms["mem_norm"]), name="mem_norm")
    saved = []
    kv = hk = x_kv = w_kv = None
    for i in range(depth):
        W = get_w(i, x)
        st = {"x": x, "W": W}
        h = _rmsnorm(x, row(norms["mix_norm"][i]), name=f"mix_norm{i}")
        mkv = _mm(mem_n, W["mkv"], "nn", BF16, name=f"mkv{i}")
        if i < n_a:
            p = _mm(h, W["a"], "nt", BF16, name=f"a_in{i}")
            y_main = _conv_fwd(p, conv_w[i], main, name=f"conv{i}")
            qblk = 3 * main // MEM_WIDTH
        else:
            p = _mm(h, W["b"], "nn", BF16, name=f"b_in{i}")
            y_main, st["o32"] = _sb_fwd(p, kv, heads, name=f"sb{i}")
            qblk = main // MEM_WIDTH
        y_mem = _memattn_fwd(p, qblk, mkv, name=f"memattn{i}")
        y = jnp.concatenate([y_main, y_mem], axis=1)
        xm = _mm(y, W["o"], "nn", F32, residual=x, name=f"w_o{i}")
        h2 = _rmsnorm(xm, row(norms["ffn_norm"][i]), name=f"ffn_norm{i}")
        gate, up, act = _ffn_up(h2, W["g"], W["u"], name=f"ffn_up{i}")
        x = _mm(act, W["d"], "nn", F32, residual=xm, name=f"w_down{i}")
        st.update(h=h, mkv=mkv, p=p, qblk=qblk, y=y, xm=xm, h2=h2, gate=gate, up=up, act=act)
        saved.append(st)
        if i == n_a - 1:
            x_kv, w_kv = x, W["kv"]
            hk = _rmsnorm(x, row(norms["kv_norm"]), name="kv_norm")
            kv = _mm(hk, w_kv, "nt", BF16, name="w_kv")

    dx, dxb, dg_final, loss = _loss_head(x, row(norms["final_norm"]), target, name="loss_head")

    dg_mix, dg_ffn, dconv = [None] * depth, [None] * depth, [None] * n_a
    dmem_n = dk = dv = dg_kv = g_kv = None
    for i in reversed(range(depth)):
        st = saved[i]
        W, g = st["W"], {}
        dgate, dup = _ffn_down_bwd(dxb, W["d"], st["gate"], st["up"], name=f"ffn_down_bwd{i}")
        g["d"] = _mm(st["act"], dxb, "tn", BF16, name=f"g_w_down{i}")
        g["g"] = _mm(dgate, st["h2"], "tn", BF16, name=f"g_w_gate{i}")
        g["u"] = _mm(dup, st["h2"], "tn", BF16, name=f"g_w_up{i}")
        dh2 = _mm(dgate, W["g"], "nn", F32, name=f"d_h2g{i}")
        dh2 = _mm(dup, W["u"], "nn", F32, residual=dh2, name=f"d_h2u{i}")
        dx, dxb, dg_ffn[i] = _rmsnorm_bwd(st["xm"], row(norms["ffn_norm"][i]), dh2, dx, name=f"ffn_norm_bwd{i}")
        dy = _mm(dxb, W["o"], "nt", BF16, name=f"d_y{i}")
        g["o"] = _mm(st["y"], dxb, "tn", BF16, name=f"g_w_o{i}")
        dqmem, dmk, dmv = _memattn_bwd(st["p"], st["qblk"], st["mkv"], dy, main // MEM_WIDTH, name=f"memattn_bwd{i}")
        dmkv = jnp.concatenate([dmk, dmv], axis=1)
        g["mkv"] = _mm(mem_n, dmkv, "tn", BF16, name=f"g_w_mem_kv{i}")
        dmem_n = _mm(dmkv, W["mkv"], "nt", F32, residual=dmem_n, name=f"d_mem_n{i}")
        if i < n_a:
            db, dc, du, dconv[i] = _conv_bwd(st["p"], conv_w[i], dy, main, name=f"conv_bwd{i}")
            dp = jnp.concatenate([db, dc, du, dqmem], axis=1)
            g["a"] = _mm(dp, st["h"], "tn", BF16, name=f"g_a_in{i}")
            dh = _mm(dp, W["a"], "nn", F32, name=f"d_h{i}")
        else:
            dq, dk, dv = _sb_bwd(st["p"], kv, st["o32"], dy, heads, dk, dv, name=f"sb_bwd{i}")
            dp = jnp.concatenate([dq, dqmem], axis=1)
            g["b"] = _mm(st["h"], dp, "tn", BF16, name=f"g_b_in{i}")
            dh = _mm(dp, W["b"], "nt", F32, name=f"d_h{i}")
        if i == n_a - 1:
            g["kv"] = g_kv
        zero = put_g(i, g)
        dx, dxb, dg_mix[i] = _rmsnorm_bwd(st["x"], row(norms["mix_norm"][i]) + zero, dh, dx, name=f"mix_norm_bwd{i}")
        if i == n_a:
            dkv = jnp.concatenate([dk, dv], axis=1).astype(BF16)
            g_kv = _mm(dkv, hk, "tn", BF16, name="g_w_kv")
            dhk = _mm(dkv, w_kv, "nn", F32, name="d_hk")
            dx, dxb, dg_kv = _rmsnorm_bwd(x_kv, row(norms["kv_norm"]), dhk, dx, name="kv_norm_bwd")
    _, _, dg_mem = _rmsnorm_bwd(mem, row(norms["mem_norm"]), dmem_n, None, name="mem_norm_bwd")

    small = {"mix_norm": jnp.concatenate(dg_mix, axis=0), "ffn_norm": jnp.concatenate(dg_ffn, axis=0), "kv_norm": dg_kv[0],
             "mem_norm": dg_mem[0], "final_norm": dg_final[0], "conv_w": jnp.stack(dconv, axis=0)}
    return loss, dx, small


_COL_SHARDED = ("a", "kv", "g", "u")
_NAMES = {"a": "a_in", "kv": "w_kv_shared", "g": "w_gate", "u": "w_up", "b": "b_in", "o": "w_o", "d": "w_down", "mkv": "w_mem_kv"}
_ORDER = ("a", "kv", "g", "u", "d", "b", "o", "mkv")
_WEIGHTS = ("mix_norm", "a_in", "conv_w", "b_in", "kv_norm", "w_kv_shared", "w_mem_kv", "w_o", "ffn_norm", "w_gate", "w_up",
            "w_down", "mem_norm", "final_norm")


def _layer_keys(i, n_a):
    keys = [("a", i) if i < n_a else ("b", i - n_a), ("g", i), ("u", i), ("d", i), ("o", i), ("mkv", i)]
    return keys + [("kv", 0)] if i == n_a - 1 else keys


def _canonical(key, w):
    w3 = w if w.ndim == 3 else w[None]
    if key in _COL_SHARDED:
        w3 = jnp.transpose(w3, (0, 2, 1))
    return w3


def _uncanonical(key, g3, like):
    if key in _COL_SHARDED:
        g3 = jnp.transpose(g3, (0, 2, 1))
    return g3.reshape(like.shape)


def _pad_rows(flat, C):
    n = flat.shape[0]
    rows = -(-n // C)
    return jnp.pad(flat, (0, rows * C - n)).reshape(rows, C)


def kernel(x, mem, mix_norm, a_in, conv_w, b_in, kv_norm, w_kv_shared, w_mem_kv, w_o, ffn_norm, w_gate, w_up, w_down, mem_norm, final_norm, loss_target, m_mix_norm, m_a_in, m_conv_w, m_b_in, m_kv_norm, m_w_kv_shared, m_w_mem_kv, m_w_o, m_ffn_norm, m_w_gate, m_w_up, m_w_down, m_mem_norm, m_final_norm, v_mix_norm, v_a_in, v_conv_w, v_b_in, v_kv_norm, v_w_kv_shared, v_w_mem_kv, v_w_o, v_ffn_norm, v_w_gate, v_w_up, v_w_down, v_mem_norm, v_final_norm):
    weights = dict(mix_norm=mix_norm, a_in=a_in, conv_w=conv_w, b_in=b_in, kv_norm=kv_norm, w_kv_shared=w_kv_shared,
                   w_mem_kv=w_mem_kv, w_o=w_o, ffn_norm=ffn_norm, w_gate=w_gate, w_up=w_up, w_down=w_down,
                   mem_norm=mem_norm, final_norm=final_norm)
    moments_m = dict(mix_norm=m_mix_norm, a_in=m_a_in, conv_w=m_conv_w, b_in=m_b_in, kv_norm=m_kv_norm,
                     w_kv_shared=m_w_kv_shared, w_mem_kv=m_w_mem_kv, w_o=m_w_o, ffn_norm=m_ffn_norm, w_gate=m_w_gate,
                     w_up=m_w_up, w_down=m_w_down, mem_norm=m_mem_norm, final_norm=m_final_norm)
    moments_v = dict(mix_norm=v_mix_norm, a_in=v_a_in, conv_w=v_conv_w, b_in=v_b_in, kv_norm=v_kv_norm,
                     w_kv_shared=v_w_kv_shared, w_mem_kv=v_w_mem_kv, w_o=v_w_o, ffn_norm=v_ffn_norm, w_gate=v_w_gate,
                     w_up=v_w_up, w_down=v_w_down, mem_norm=v_mem_norm, final_norm=v_final_norm)
    D = x.shape[-1]
    depth, n_a = w_o.shape[0], a_in.shape[0]
    xi, yi, ci = _place()
    me = 4 * xi + 2 * yi + ci
    core = ci.reshape(1).astype(jnp.int32)
    chip = (2 * xi + yi).reshape(1).astype(jnp.int32)

    shard3 = {k: _canonical(k, weights[_NAMES[k]]).astype(BF16) for k in _ORDER}
    shard = lambda kl: shard3[kl[0]][kl[1]]
    keys0 = _layer_keys(0, n_a)
    full0 = _all_gather_weights([shard(kl)[None] for kl in keys0], name="all_gather_layer0")
    layer_w = {0: {kl[0]: f[0] for kl, f in zip(keys0, full0)}}

    def placed(s):
        r, cols = s.shape
        return lax.dynamic_update_slice(lax.empty((N_DEV * r, cols), s.dtype), s, (me * r, 0))

    groups = [[(shard(kl), placed(shard(kl))) for kl in _layer_keys(i, n_a)] for i in range(1, depth)]
    pending_w, started = _gather_start(groups, full0[0], name="gather_start")

    def get_w(i, after):
        if i not in layer_w:
            _, fulls = _gather_wait(pending_w[i - 1], after, name=f"gather_wait{i}")
            layer_w[i] = {kl[0]: f for kl, f in zip(_layer_keys(i, n_a), fulls)}
        return layer_w[i]

    reduced, pending_g = {}, {}

    def put_g(i, g):
        keys = _layer_keys(i, n_a)
        g5 = [g[k].reshape(1, 4, 2, g[k].shape[0] // N_DEV, g[k].shape[1]) for k, _ in keys]
        from_sibling = _rs_sibling(g5, name=f"rs_sibling{i}")
        sums = [_chip_sum(a, l, core, name=f"chip_sum{i}_{k}") for (k, _), a, l in zip(keys, g5, from_sibling)]
        if i == 0:
            reduced[0] = (sums, _rs_chips(sums, name="rs_chips0"))
            return jnp.zeros((1, 1), F32)
        pending_g[i], token = _chips_start(sums, None, name=f"rs_chips_start{i}")
        return token[:1, :1]

    cw_shape = conv_w.shape
    cw_rows = _pad_rows(conv_w.reshape(-1), D)
    cw_rows = jnp.pad(cw_rows, ((0, 8 - cw_rows.shape[0]), (0, 0)))
    cw_all = _exchange(cw_rows, False, name="gather_conv_w")
    n_cw = cw_shape[0] * cw_shape[1] * cw_shape[2]
    cw_all = cw_all.reshape(N_DEV, -1)[:, :n_cw].reshape((N_DEV,) + cw_shape)
    conv_full = jnp.transpose(cw_all, (1, 2, 0, 3)).reshape(cw_shape[0], cw_shape[1], N_DEV * cw_shape[2])

    norms = {k: weights[k] for k in ("mix_norm", "ffn_norm", "kv_norm", "mem_norm", "final_norm")}
    norms["mix_norm"] = mix_norm + started[:1, :1]
    loss, grad_x, small = _local_step(x[0], mem[0], loss_target[0], norms, conv_full, depth, n_a, get_w, put_g)

    shard_grads = {}
    for i in range(depth):
        if i not in reduced:
            reduced[i] = _chips_wait(pending_g[i], grad_x, name=f"rs_chips_wait{i}")
        for kl, s, l in zip(_layer_keys(i, n_a), *reduced[i]):
            shard_grads[kl] = _final_sum(s, l, chip, name=f"final_sum{i}_{kl[0]}")[0]
    grads = {}
    for k in _ORDER:
        g3 = jnp.stack([shard_grads[(k, l)] for l in range(shard3[k].shape[0])])
        grads[_NAMES[k]] = _uncanonical(k, g3, weights[_NAMES[k]])

    order = ("mix_norm", "ffn_norm", "kv_norm", "mem_norm", "final_norm", "conv_w")
    flat = jnp.concatenate([small[k].reshape(-1) for k in order] + [loss[0, :1]])
    n_flat = flat.shape[0]
    rows = _pad_rows(flat, D)
    rows = jnp.pad(rows, ((0, (-rows.shape[0]) % 8), (0, 0)))
    total = _exchange(rows, True, name="all_reduce_small").reshape(-1)[:n_flat]
    off = 0
    for k in order:
        n = small[k].size
        grads[k] = total[off:off + n].reshape(small[k].shape)
        off += n
    loss_total = total[off]
    grads["conv_w"] = lax.dynamic_slice_in_dim(grads["conv_w"], me * cw_shape[2], cw_shape[2], axis=2)

    deltas, new_m, new_v = {}, {}, {}
    for k in _WEIGHTS:
        w = weights[k]
        two = (lambda a: a.reshape(-1, a.shape[-1])) if w.ndim > 1 else (lambda a: a.reshape(1, -1))
        d, nm, nv = _adamw(two(w), two(grads[k]), two(moments_m[k]), two(moments_v[k]), name=f"adamw_{k}")
        deltas[k], new_m[k], new_v[k] = d.reshape(w.shape), nm.reshape(w.shape), nv.reshape(w.shape)

    return (loss_total, grad_x[None], *[grads[k] for k in _WEIGHTS], *[deltas[k] for k in _WEIGHTS],
            *[new_m[k] for k in _WEIGHTS], *[new_v[k] for k in _WEIGHTS])
```

```python
import functools
import math

import jax
import jax.numpy as jnp
from jax import lax
from jax.experimental import pallas as pl
from jax.experimental.pallas import tpu as pltpu

F32 = jnp.float32
BF16 = jnp.bfloat16
MESH = pl.DeviceIdType.MESH

HEAD_DIM = 64
MEM_HEADS = 4
MEM_WIDTH = MEM_HEADS * HEAD_DIM
EPS = 1e-6
LANES = 128
BF16_ROWS = 16
VMEM_LIMIT = 56 * 1024 * 1024
N_DEV = 8

ADAM_LR = 0.001
ADAM_B1 = 0.9
ADAM_B2 = 0.999
ADAM_EPS = 1e-08
ADAM_WD = 0.01
ADAM_STEP = 10

ANY = pl.BlockSpec(memory_space=pl.ANY)


def _cparams(sem=None):
    return pltpu.CompilerParams(dimension_semantics=sem, vmem_limit_bytes=VMEM_LIMIT)


def _pick(n, cands):
    for c in cands:
        if n % c == 0:
            return c
    raise ValueError(f"no tile for {n} in {cands}")


def _mm(a, b, form, out_dtype, *, name, residual=None):
    if form == "tn":
        K, M = a.shape
    else:
        M, K = a.shape
    if form == "nt":
        N, K2 = b.shape
    else:
        K2, N = b.shape
    assert K == K2, (name, a.shape, b.shape)
    wide = (1408, 1280, 1024, 768, 512, 256, 128)
    tm = _pick(M, wide if form == "tn" else (1024, 512, 256, 128))
    tn = _pick(N, wide)
    tk = _pick(K, (1024, 1408, 1280, 768, 512, 256))
    nk = K // tk
    dims = {"nn": (((1,), (0,)), ((), ())), "nt": (((1,), (1,)), ((), ())), "tn": (((0,), (0,)), ((), ()))}[form]
    a_bytes, b_bytes = M * K * a.dtype.itemsize, N * K * b.dtype.itemsize
    n_outer = nk == 1 and (N // tn) * a_bytes + b_bytes < a_bytes + (M // tm) * b_bytes
    ij = (lambda g0, g1: (g1, g0)) if n_outer else (lambda g0, g1: (g0, g1))

    def spec(block, f):
        return pl.BlockSpec(block, lambda g0, g1, k: f(*ij(g0, g1), k))

    a_spec = spec((tk, tm), lambda i, j, k: (k, i)) if form == "tn" else spec((tm, tk), lambda i, j, k: (i, k))
    b_spec = spec((tn, tk), lambda i, j, k: (j, k)) if form == "nt" else spec((tk, tn), lambda i, j, k: (k, j))
    out_spec = spec((tm, tn), lambda i, j, k: (i, j))
    operands, in_specs = [a, b], [a_spec, b_spec]
    has_res = residual is not None
    if has_res:
        operands.append(residual)
        in_specs.append(out_spec)
    grid = (N // tn, M // tm, nk) if n_outer else (M // tm, N // tn, nk)

    def body(*refs):
        a_ref, b_ref = refs[0], refs[1]
        r_ref = refs[2] if has_res else None
        o_ref = refs[2 + int(has_res)]
        acc_ref = refs[-1]
        part = lax.dot_general(a_ref[...].astype(BF16), b_ref[...].astype(BF16), dims, preferred_element_type=F32)

        def finish(total):
            if has_res:
                total = total + r_ref[...].astype(F32)
            o_ref[...] = total.astype(out_dtype)

        if nk == 1:
            finish(part)
        else:
            k = pl.program_id(2)

            @pl.when(k == 0)
            def _():
                acc_ref[...] = part

            @pl.when(jnp.logical_and(k > 0, k < nk - 1))
            def _():
                acc_ref[...] += part

            @pl.when(k == nk - 1)
            def _():
                finish(acc_ref[...] + part)

    return pl.pallas_call(
        body, name=name, out_shape=jax.ShapeDtypeStruct((M, N), out_dtype), grid=grid, in_specs=in_specs, out_specs=out_spec,
        scratch_shapes=[pltpu.VMEM((tm, tn), F32)], compiler_params=_cparams(("parallel", "parallel", "arbitrary")),
    )(*operands)


def _rmsnorm(x, g, *, name):
    R, D = x.shape
    tr = _pick(R, (512, 256))

    def body(x_ref, g_ref, o_ref):
        xv = x_ref[...]
        r = lax.rsqrt(jnp.mean(xv * xv, axis=-1, keepdims=True) + EPS)
        o_ref[...] = (xv * r * g_ref[...]).astype(BF16)

    return pl.pallas_call(
        body, name=name, out_shape=jax.ShapeDtypeStruct((R, D), BF16), grid=(R // tr,),
        in_specs=[pl.BlockSpec((tr, D), lambda i: (i, 0)), pl.BlockSpec((1, D), lambda i: (0, 0))],
        out_specs=pl.BlockSpec((tr, D), lambda i: (i, 0)), compiler_params=_cparams(("parallel",)),
    )(x, g)


def _rmsnorm_bwd(x, g, dh, dx_in, *, name):
    R, D = x.shape
    tr = _pick(R, (512, 256))
    has_in = dx_in is not None

    def body(*refs):
        x_ref, g_ref, dh_ref = refs[:3]
        dxi_ref = refs[3] if has_in else None
        dx_ref, dxb_ref, dg_ref = refs[3 + int(has_in):]
        xv = x_ref[...]
        r = lax.rsqrt(jnp.mean(xv * xv, axis=-1, keepdims=True) + EPS)
        xhat = xv * r
        dhv = dh_ref[...].astype(F32)
        dxh = dhv * g_ref[...]
        dx = r * (dxh - xhat * jnp.mean(dxh * xhat, axis=-1, keepdims=True))
        if has_in:
            dx = dx + dxi_ref[...]
        dx_ref[...] = dx
        dxb_ref[...] = dx.astype(BF16)
        part = jnp.sum(dhv * xhat, axis=0, keepdims=True)

        @pl.when(pl.program_id(0) == 0)
        def _():
            dg_ref[...] = part

        @pl.when(pl.program_id(0) > 0)
        def _():
            dg_ref[...] += part

    row = pl.BlockSpec((tr, D), lambda i: (i, 0))
    vec = pl.BlockSpec((1, D), lambda i: (0, 0))
    ops = [x, g, dh] + ([dx_in] if has_in else [])
    return pl.pallas_call(
        body, name=name,
        out_shape=(jax.ShapeDtypeStruct((R, D), F32), jax.ShapeDtypeStruct((R, D), BF16), jax.ShapeDtypeStruct((1, D), F32)),
        grid=(R // tr,), in_specs=[row, vec, row] + ([row] if has_in else []), out_specs=(row, row, vec),
        compiler_params=_cparams(("arbitrary",)),
    )(*ops)


def _loss_head(x, g, target, *, name):
    R, D = x.shape
    tr = _pick(R, (512, 256))

    def body(x_ref, g_ref, t_ref, dx_ref, dxb_ref, dg_ref, loss_ref):
        xv = x_ref[...]
        gv = g_ref[...]
        r = lax.rsqrt(jnp.mean(xv * xv, axis=-1, keepdims=True) + EPS)
        xhat = xv * r
        err = xhat * gv - t_ref[...]
        loss = 0.5 * jnp.sum(jnp.mean(err * err, axis=-1, keepdims=True), axis=0, keepdims=True)
        dy = err * (1.0 / D)
        dxh = dy * gv
        dx = r * (dxh - xhat * jnp.mean(dxh * xhat, axis=-1, keepdims=True))
        dx_ref[...] = dx
        dxb_ref[...] = dx.astype(BF16)
        dg = jnp.sum(dy * xhat, axis=0, keepdims=True)
        lossv = jnp.broadcast_to(loss, (1, LANES))

        @pl.when(pl.program_id(0) == 0)
        def _():
            dg_ref[...] = dg
            loss_ref[...] = lossv

        @pl.when(pl.program_id(0) > 0)
        def _():
            dg_ref[...] += dg
            loss_ref[...] += lossv

    row = pl.BlockSpec((tr, D), lambda i: (i, 0))
    vec = pl.BlockSpec((1, D), lambda i: (0, 0))
    return pl.pallas_call(
        body, name=name,
        out_shape=(jax.ShapeDtypeStruct((R, D), F32), jax.ShapeDtypeStruct((R, D), BF16), jax.ShapeDtypeStruct((1, D), F32),
                   jax.ShapeDtypeStruct((1, LANES), F32)),
        grid=(R // tr,), in_specs=[row, vec, row], out_specs=(row, row, vec, pl.BlockSpec((1, LANES), lambda i: (0, 0))),
        compiler_params=_cparams(("arbitrary",)),
    )(x, g, target)


def _conv_taps(gv, S):
    t = lax.broadcasted_iota(jnp.int32, gv.shape, 0)
    g1 = jnp.where(t >= 1, pltpu.roll(gv, 1, 0), 0.0)
    g2 = jnp.where(t >= 2, pltpu.roll(gv, 2, 0), 0.0)
    return g1, g2


def _conv_fwd(p, w, main, *, name):
    S = p.shape[0]
    tc = LANES
    nb = main // tc

    def body(b_ref, c_ref, u_ref, w_ref, y_ref):
        gv = c_ref[...].astype(F32) * u_ref[...].astype(F32)
        g1, g2 = _conv_taps(gv, S)
        cv = w_ref[0:1, :] * g2 + w_ref[1:2, :] * g1 + w_ref[2:3, :] * gv
        y_ref[...] = (b_ref[...].astype(F32) * cv).astype(BF16)

    col = lambda off: pl.BlockSpec((S, tc), lambda j: (0, off + j))
    return pl.pallas_call(
        body, name=name, out_shape=jax.ShapeDtypeStruct((S, main), BF16), grid=(nb,),
        in_specs=[col(0), col(nb), col(2 * nb), pl.BlockSpec((3, tc), lambda j: (0, j))],
        out_specs=pl.BlockSpec((S, tc), lambda j: (0, j)), compiler_params=_cparams(("parallel",)),
    )(p, p, p, w)


def _conv_bwd(p, w, dy, main, *, name):
    S = p.shape[0]
    tc = LANES
    nb = main // tc

    def body(b_ref, c_ref, u_ref, w_ref, dy_ref, db_ref, dc_ref, du_ref, dw_ref):
        cvv, uv = c_ref[...].astype(F32), u_ref[...].astype(F32)
        gv = cvv * uv
        g1, g2 = _conv_taps(gv, S)
        w0, w1, w2 = w_ref[0:1, :], w_ref[1:2, :], w_ref[2:3, :]
        dyv = dy_ref[...].astype(F32)
        db_ref[...] = (dyv * (w0 * g2 + w1 * g1 + w2 * gv)).astype(BF16)
        dcv = dyv * b_ref[...].astype(F32)
        t = lax.broadcasted_iota(jnp.int32, dcv.shape, 0)
        n1 = jnp.where(t <= S - 2, pltpu.roll(dcv, S - 1, 0), 0.0)
        n2 = jnp.where(t <= S - 3, pltpu.roll(dcv, S - 2, 0), 0.0)
        dg = w2 * dcv + w1 * n1 + w0 * n2
        dc_ref[...] = (dg * uv).astype(BF16)
        du_ref[...] = (dg * cvv).astype(BF16)
        dw_ref[0:1, :] = jnp.sum(dcv * g2, axis=0, keepdims=True)
        dw_ref[1:2, :] = jnp.sum(dcv * g1, axis=0, keepdims=True)
        dw_ref[2:3, :] = jnp.sum(dcv * gv, axis=0, keepdims=True)

    col = lambda off: pl.BlockSpec((S, tc), lambda j: (0, off + j))
    out = jax.ShapeDtypeStruct((S, main), BF16)
    return pl.pallas_call(
        body, name=name, out_shape=(out, out, out, jax.ShapeDtypeStruct((3, main), F32)), grid=(nb,),
        in_specs=[col(0), col(nb), col(2 * nb), pl.BlockSpec((3, tc), lambda j: (0, j)), col(0)],
        out_specs=(col(0), col(0), col(0), pl.BlockSpec((3, tc), lambda j: (0, j))),
        compiler_params=_cparams(("parallel",)),
    )(p, p, p, w, dy)


def _head_mask(width, h):
    lane = lax.broadcasted_iota(jnp.int32, (1, width), 1)
    return jnp.logical_and(lane >= h * HEAD_DIM, lane < (h + 1) * HEAD_DIM)


_NT = (((1,), (1,)), ((), ()))
_NN = (((1,), (0,)), ((), ()))
_TN = (((0,), (0,)), ((), ()))


def _dot(a, b, dims):
    return lax.dot_general(a, b, dims, preferred_element_type=F32)


def _mem_probs(qh, kv):
    s = _dot(qh, kv, _NT) * (1.0 / math.sqrt(HEAD_DIM))
    s = s - jnp.max(s, axis=-1, keepdims=True)
    e = jnp.exp(s)
    return e / jnp.sum(e, axis=-1, keepdims=True)


def _memattn_fwd(p, qblk, mkv, *, name):
    S = p.shape[0]
    M = mkv.shape[0]
    W = MEM_WIDTH
    tq = _pick(S, (512, 256))

    def body(q_ref, k_ref, v_ref, o_ref):
        q = q_ref[...].astype(BF16)
        kv, vv = k_ref[...], v_ref[...]
        out = jnp.zeros((tq, W), F32)
        for h in range(MEM_HEADS):
            m = _head_mask(W, h)
            pr = _mem_probs(jnp.where(m, q, jnp.zeros_like(q)), kv)
            out = jnp.where(m, _dot(pr.astype(BF16), vv, _NN), out)
        o_ref[...] = out.astype(BF16)

    return pl.pallas_call(
        body, name=name, out_shape=jax.ShapeDtypeStruct((S, W), BF16), grid=(S // tq,),
        in_specs=[pl.BlockSpec((tq, W), lambda i: (i, qblk)), pl.BlockSpec((M, W), lambda i: (0, 0)),
                  pl.BlockSpec((M, W), lambda i: (0, 1))],
        out_specs=pl.BlockSpec((tq, W), lambda i: (i, 0)), compiler_params=_cparams(("parallel",)),
    )(p, mkv, mkv)


def _memattn_bwd(p, qblk, mkv, dy, dyblk, *, name):
    S = p.shape[0]
    M = mkv.shape[0]
    W = MEM_WIDTH
    tq = _pick(S, (512, 256))
    scale = 1.0 / math.sqrt(HEAD_DIM)

    def body(q_ref, k_ref, v_ref, do_ref, dq_ref, dk_ref, dv_ref, dk_acc, dv_acc):
        q = q_ref[...].astype(BF16)
        do = do_ref[...].astype(BF16)
        kv, vv = k_ref[...], v_ref[...]
        dq = jnp.zeros((tq, W), F32)
        dk = jnp.zeros((M, W), F32)
        dv = jnp.zeros((M, W), F32)
        for h in range(MEM_HEADS):
            m = _head_mask(W, h)
            qh = jnp.where(m, q, jnp.zeros_like(q))
            doh = jnp.where(m, do, jnp.zeros_like(do))
            pr = _mem_probs(qh, kv)
            dpr = _dot(doh, vv, _NT)
            ds = (pr * (dpr - jnp.sum(dpr * pr, axis=-1, keepdims=True)) * scale).astype(BF16)
            dq = jnp.where(m, _dot(ds, kv, _NN), dq)
            dk = dk + _dot(ds, qh, _TN)
            dv = dv + _dot(pr.astype(BF16), doh, _TN)
        dq_ref[...] = dq.astype(BF16)
        i = pl.program_id(0)

        @pl.when(i == 0)
        def _():
            dk_acc[...] = dk
            dv_acc[...] = dv

        @pl.when(i > 0)
        def _():
            dk_acc[...] += dk
            dv_acc[...] += dv

        @pl.when(i == S // tq - 1)
        def _():
            dk_ref[...] = dk_acc[...].astype(BF16)
            dv_ref[...] = dv_acc[...].astype(BF16)

    kspec = lambda c: pl.BlockSpec((M, W), lambda i: (0, c))
    return pl.pallas_call(
        body, name=name,
        out_shape=(jax.ShapeDtypeStruct((S, W), BF16), jax.ShapeDtypeStruct((M, W), BF16), jax.ShapeDtypeStruct((M, W), BF16)),
        grid=(S // tq,),
        in_specs=[pl.BlockSpec((tq, W), lambda i: (i, qblk)), kspec(0), kspec(1), pl.BlockSpec((tq, W), lambda i: (i, dyblk))],
        out_specs=(pl.BlockSpec((tq, W), lambda i: (i, 0)), kspec(0), kspec(0)),
        scratch_shapes=[pltpu.VMEM((M, W), F32), pltpu.VMEM((M, W), F32)],
        compiler_params=_cparams(("arbitrary",)),
    )(p, mkv, mkv, dy)


SB_TQ = 256
SB_CLAMP = 80.0


SB_CHUNK = 64


def _by_rows(fn, *arrays):
    rows = next(a for a in arrays if a is not None).shape[0]
    outs = [fn(*[None if a is None else a[r0:r0 + SB_CHUNK] for a in arrays]) for r0 in range(0, rows, SB_CHUNK)]
    return tuple(jnp.concatenate(col, axis=0) for col in zip(*outs))


def _sb_scores(qh, kb, causal):
    def chain(z, mask):
        z = jnp.clip(z, -SB_CLAMP, SB_CLAMP)
        w = 1.0 + jnp.exp(z)
        sp = jnp.log(w)
        zs = z - sp
        if mask is not None:
            sp = jnp.where(mask, sp, 0.0)
            zs = jnp.where(mask, zs, -1e30)
            w = jnp.where(mask, w, 1.0)
        return zs, sp.astype(BF16), jnp.sum(sp, axis=1, keepdims=True), w

    return _by_rows(chain, _dot(qh, kb, _NT), causal)


def _sb_weights(zs, spb, tri, carry):
    return _by_rows(lambda zs_c, t_c, c_c: (jnp.exp(zs_c - (t_c + c_c)).astype(BF16),), zs, _dot(spb, tri, _NN), carry)[0]


def _stack_heads(v, m0):
    zero = jnp.zeros_like(v)
    return jnp.concatenate([jnp.where(m0, v, zero), jnp.where(m0, zero, v)], axis=0)


def _stacked_causal(tq):
    r = lax.broadcasted_iota(jnp.int32, (2 * tq, tq), 0)
    c = lax.broadcasted_iota(jnp.int32, (2 * tq, tq), 1)
    return c < jnp.where(r >= tq, r - tq, r)


def _sb_fwd(p, kv, heads, *, name):
    S = p.shape[0]
    tq = SB_TQ
    npair = heads // 2

    def body(q_ref, k_ref, v_ref, o_ref, o32_ref):
        qi = pl.program_id(1)
        r = lax.broadcasted_iota(jnp.int32, (tq, tq), 0)
        c = lax.broadcasted_iota(jnp.int32, (tq, tq), 1)
        tri = (r > c).astype(BF16)
        causal = _stacked_causal(tq)
        m0 = _head_mask(LANES, 0)
        qh = _stack_heads(q_ref[...] * jnp.asarray(1.0 / math.sqrt(HEAD_DIM), BF16), m0)

        def block(j, carry, acc, mask):
            off = pl.multiple_of(j * tq, tq)
            kb = k_ref[pl.ds(off, tq), :]
            vb = v_ref[pl.ds(off, tq), :]
            zs, spb, sp_sum, _ = _sb_scores(qh, kb, mask)
            acc = acc + _dot(_sb_weights(zs, spb, tri, carry), vb, _NN)
            return carry + sp_sum, acc

        st = block(qi, jnp.zeros((2 * tq, 1), F32), jnp.zeros((2 * tq, LANES), F32), causal)
        odd = qi % 2
        st = lax.cond(odd == 1, lambda s: block(qi - 1, s[0], s[1], None), lambda s: s, st)

        def pair(it, s):
            j = qi - 1 - odd - 2 * it
            s = block(j, s[0], s[1], None)
            return block(j - 1, s[0], s[1], None)

        carry, acc = lax.fori_loop(0, qi // 2, pair, st)
        out = jnp.where(m0, acc[:tq], acc[tq:])
        o_ref[...] = out.astype(BF16)
        o32_ref[...] = out

    W = heads * HEAD_DIM
    qspec = pl.BlockSpec((tq, LANES), lambda hp, i: (i, hp))
    return pl.pallas_call(
        body, name=name, out_shape=(jax.ShapeDtypeStruct((S, W), BF16), jax.ShapeDtypeStruct((S, W), F32)), grid=(npair, S // tq),
        in_specs=[qspec, pl.BlockSpec((S, LANES), lambda hp, i: (0, hp)), pl.BlockSpec((S, LANES), lambda hp, i: (0, npair + hp))],
        out_specs=(qspec, qspec), compiler_params=_cparams(("parallel", "arbitrary")),
    )(p, kv, kv)


def _sb_bwd(p, kv, o32, dy, heads, dk_in, dv_in, *, name):
    S = p.shape[0]
    tq = SB_TQ
    npair = heads // 2
    has_in = dk_in is not None
    scale = 1.0 / math.sqrt(HEAD_DIM)

    def body(*refs):
        q_ref, k_ref, v_ref, o_ref, do_ref = refs[:5]
        dq_ref, dk_ref, dv_ref = refs[5 + 2 * int(has_in):]
        qi = pl.program_id(1)

        @pl.when(qi == 0)
        def _():
            if has_in:
                dk_ref[...] = refs[5][...]
                dv_ref[...] = refs[6][...]
            else:
                dk_ref[...] = jnp.zeros_like(dk_ref)
                dv_ref[...] = jnp.zeros_like(dv_ref)

        r = lax.broadcasted_iota(jnp.int32, (tq, tq), 0)
        c = lax.broadcasted_iota(jnp.int32, (tq, tq), 1)
        tri = (r > c).astype(BF16)
        tri_low = (r < c).astype(BF16)
        causal = _stacked_causal(tq)
        m0 = _head_mask(LANES, 0)
        qh = _stack_heads(q_ref[...] * jnp.asarray(scale, BF16), m0)
        do = do_ref[...]
        doh = _stack_heads(do, m0)
        dov = do.astype(F32) * o_ref[...]
        dsum = jnp.concatenate([jnp.sum(jnp.where(m0, dov, 0.0), axis=1, keepdims=True),
                                jnp.sum(jnp.where(m0, 0.0, dov), axis=1, keepdims=True)], axis=0)

        def block(j, carry, gcarry, acc, mask):
            off = pl.multiple_of(j * tq, tq)
            kb = k_ref[pl.ds(off, tq), :]
            vb = v_ref[pl.ds(off, tq), :]
            zs, spb, sp_sum, w = _sb_scores(qh, kb, mask)
            ab = _sb_weights(zs, spb, tri, carry)

            def grads(ab_c, da_c):
                g = ab_c.astype(F32) * da_c
                return g, g.astype(BF16), jnp.sum(g, axis=1, keepdims=True)

            g, gb, g_sum = _by_rows(grads, ab, _dot(doh, vb, _NT))
            gcarry = gcarry + g_sum

            def logit_grads(g_c, w_c, low_c, left_c):
                rinv = 1.0 / w_c
                return ((g_c * rinv - (left_c + low_c) * (1.0 - rinv)).astype(BF16),)

            dzs = _by_rows(logit_grads, g, w, _dot(gb, tri_low, _NN), dsum - gcarry)[0]
            acc = acc + _dot(dzs, kb, _NN)
            dk_ref[pl.ds(off, tq), :] += _dot(dzs, qh, _TN)
            dv_ref[pl.ds(off, tq), :] += _dot(ab, doh, _TN)
            return (carry + sp_sum, gcarry, acc)

        zero = jnp.zeros((2 * tq, 1), F32)
        st = block(qi, zero, zero, jnp.zeros((2 * tq, LANES), F32), causal)
        odd = qi % 2
        st = lax.cond(odd == 1, lambda s: block(qi - 1, s[0], s[1], s[2], None), lambda s: s, st)

        def pair(it, s):
            j = qi - 1 - odd - 2 * it
            s = block(j, s[0], s[1], s[2], None)
            return block(j - 1, s[0], s[1], s[2], None)

        st = lax.fori_loop(0, qi // 2, pair, st)
        dq_ref[...] = (jnp.where(m0, st[2][:tq], st[2][tq:]) * scale).astype(BF16)

    W = heads * HEAD_DIM
    qspec = pl.BlockSpec((tq, LANES), lambda hp, i: (i, hp))
    seq = lambda off: pl.BlockSpec((S, LANES), lambda hp, i: (0, off + hp))
    ops = [p, kv, kv, o32, dy] + ([dk_in, dv_in] if has_in else [])
    return pl.pallas_call(
        body, name=name,
        out_shape=(jax.ShapeDtypeStruct((S, W), BF16), jax.ShapeDtypeStruct((S, W), F32), jax.ShapeDtypeStruct((S, W), F32)),
        grid=(npair, S // tq),
        in_specs=[qspec, seq(0), seq(npair), qspec, qspec] + ([seq(0), seq(0)] if has_in else []),
        out_specs=(qspec, seq(0), seq(0)),
        compiler_params=_cparams(("parallel", "arbitrary")),
    )(*ops)


def _ffn_up(h, wg, wu, *, name):
    S, D = h.shape
    F = wg.shape[0]
    tm = _pick(S, (512, 256))
    tn = _pick(F, (1408, 1024, 512, 256, 128))

    def body(h_ref, g_ref, u_ref, gate_ref, up_ref, act_ref):
        hv = h_ref[...]
        g = _dot(hv, g_ref[...], _NT)
        u = _dot(hv, u_ref[...], _NT)
        gate_ref[...] = g.astype(BF16)
        up_ref[...] = u.astype(BF16)
        act_ref[...] = (g * jax.nn.sigmoid(g) * u).astype(BF16)

    wspec = pl.BlockSpec((tn, D), lambda j, i: (j, 0))
    ospec = pl.BlockSpec((tm, tn), lambda j, i: (i, j))
    out = jax.ShapeDtypeStruct((S, F), BF16)
    return pl.pallas_call(
        body, name=name, out_shape=(out, out, out), grid=(F // tn, S // tm),
        in_specs=[pl.BlockSpec((tm, D), lambda j, i: (i, 0)), wspec, wspec], out_specs=(ospec, ospec, ospec),
        compiler_params=_cparams(("parallel", "parallel")),
    )(h, wg, wu)


def _ffn_down_bwd(dx, wd, gate, up, *, name):
    S, D = dx.shape
    F = wd.shape[0]
    tm = _pick(S, (512, 256))
    tn = _pick(F, (1408, 1024, 512, 256, 128))

    def body(dx_ref, w_ref, g_ref, u_ref, dg_ref, du_ref):
        da = _dot(dx_ref[...], w_ref[...], _NT)
        gv, uv = g_ref[...].astype(F32), u_ref[...].astype(F32)
        s = jax.nn.sigmoid(gv)
        silu = gv * s
        dg_ref[...] = (da * uv * (s + silu * (1.0 - s))).astype(BF16)
        du_ref[...] = (da * silu).astype(BF16)

    ospec = pl.BlockSpec((tm, tn), lambda j, i: (i, j))
    out = jax.ShapeDtypeStruct((S, F), BF16)
    return pl.pallas_call(
        body, name=name, out_shape=(out, out), grid=(F // tn, S // tm),
        in_specs=[pl.BlockSpec((tm, D), lambda j, i: (i, 0)), pl.BlockSpec((tn, D), lambda j, i: (j, 0)), ospec, ospec],
        out_specs=(ospec, ospec), compiler_params=_cparams(("parallel", "parallel")),
    )(dx, wd, gate, up)


def _adamw(w, g, m, v, *, name):
    R, C = w.shape
    tr = R
    for cand in (1024, 512, 256, 128, 64, 32, 16, 8):
        if R % cand == 0 and cand * C * 4 <= (1 << 20):
            tr = cand
            break
    bc1 = 1.0 - ADAM_B1 ** ADAM_STEP
    bc2 = 1.0 - ADAM_B2 ** ADAM_STEP

    def body(w_ref, g_ref, m_ref, v_ref, d_ref, nm_ref, nv_ref):
        gv = g_ref[...]
        nm = ADAM_B1 * m_ref[...] + (1.0 - ADAM_B1) * gv
        nv = ADAM_B2 * v_ref[...] + (1.0 - ADAM_B2) * (gv * gv)
        nm_ref[...] = nm
        nv_ref[...] = nv
        d_ref[...] = -ADAM_LR * ((nm / bc1) / (jnp.sqrt(nv / bc2) + ADAM_EPS) + ADAM_WD * w_ref[...])

    blk = pl.BlockSpec((tr, C), lambda i: (i, 0))
    out = jax.ShapeDtypeStruct((R, C), F32)
    return pl.pallas_call(body, name=name, out_shape=(out, out, out), grid=(R // tr,), in_specs=[blk] * 4,
                          out_specs=(blk, blk, blk), compiler_params=_cparams(("parallel",)))(w, g, m, v)


def _place():
    x, y, c = lax.axis_index("x"), lax.axis_index("y"), lax.axis_index("c")
    return x, y, c


def _all_gather_weights(shards, *, name):
    n = len(shards)

    def body(*refs):
        sh, full = refs[:n], refs[n:2 * n]
        send_sems, recv_sems, local_sems = refs[2 * n:]
        x, y, c = _place()
        me, sibling = (x, y, c), (x, y, 1 - c)
        chips = [(1 - x, y), (x, 1 - y), (1 - x, 1 - y)]

        def rows(t, px, py, pc):
            r = sh[t].shape[1]
            return full[t].at[:, pl.ds(pl.multiple_of((4 * px + 2 * py + pc) * r, BF16_ROWS), r), :]

        def copy(t, k, block, to, src=None):
            return pltpu.make_async_remote_copy(
                src_ref=rows(t, *block) if src is None else src, dst_ref=rows(t, *block),
                send_sem=send_sems.at[7 * t + k], recv_sem=recv_sems.at[7 * t + k], device_id=to, device_id_type=MESH)

        started = []
        for t in range(n):
            mine = pltpu.make_async_copy(sh[t], rows(t, *me), local_sems.at[t])
            mine.start()
            started.append(mine)
        sends = []
        for t in range(n):
            first = [copy(t, 0, me, sibling, src=sh[t])]
            first += [copy(t, 1 + j, me, (*chip, c), src=sh[t]) for j, chip in enumerate(chips)]
            for cp in first:
                cp.start()
            sends += first
        for t in range(n):
            for j, chip in enumerate(chips):
                copy(t, 1 + j, (*chip, c), me).wait_recv()
                fwd = copy(t, 4 + j, (*chip, c), sibling)
                fwd.start()
                sends.append(fwd)
        for t in range(n):
            copy(t, 0, sibling, me).wait_recv()
            for j, chip in enumerate(chips):
                copy(t, 4 + j, (*chip, 1 - c), me).wait_recv()
        for cp in sends:
            cp.wait_send()
        for cp in started:
            cp.wait()

    out_shape = [jax.ShapeDtypeStruct((s.shape[0], N_DEV * s.shape[1], s.shape[2]), s.dtype) for s in shards]
    return pl.pallas_call(
        body, name=name, out_shape=out_shape, in_specs=[ANY] * n, out_specs=[ANY] * n,
        scratch_shapes=[pltpu.SemaphoreType.DMA((7 * n,)), pltpu.SemaphoreType.DMA((7 * n,)), pltpu.SemaphoreType.DMA((n,))],
    )(*shards)


def _whole(ref_a, ref_b, send_sem, recv_sem, me):
    return pltpu.make_async_remote_copy(src_ref=ref_a, dst_ref=ref_b, send_sem=send_sem, recv_sem=recv_sem,
                                        device_id=me, device_id_type=MESH)


def _rs_sibling(grads, *, name):
    n = len(grads)

    def body(*refs):
        g, land = refs[:n], refs[n:2 * n]
        send_sems, recv_sems = refs[2 * n:]
        x, y, c = _place()
        for t in range(n):
            for k in range(4):
                pltpu.make_async_remote_copy(
                    src_ref=g[t].at[:, k, 1 - c], dst_ref=land[t].at[k], send_sem=send_sems.at[t], recv_sem=recv_sems.at[t],
                    device_id=(x, y, 1 - c), device_id_type=MESH).start()
        for t in range(n):
            w = _whole(land[t], land[t], send_sems.at[t], recv_sems.at[t], (x, y, c))
            w.wait_send()
            w.wait_recv()

    out_shape = [jax.ShapeDtypeStruct((4, s.shape[0], s.shape[3], s.shape[4]), s.dtype) for s in grads]
    return pl.pallas_call(
        body, name=name, out_shape=out_shape, in_specs=[ANY] * n, out_specs=[ANY] * n,
        scratch_shapes=[pltpu.SemaphoreType.DMA((n,)), pltpu.SemaphoreType.DMA((n,))],
    )(*grads)


def _rs_chips(sums, *, name):
    n = len(sums)

    def body(*refs):
        s, land = refs[:n], refs[n:2 * n]
        send_sems, recv_sems = refs[2 * n:]
        x, y, c = _place()
        chips = [(1 - x, y), (x, 1 - y), (1 - x, 1 - y)]
        for t in range(n):
            for j, (px, py) in enumerate(chips):
                pltpu.make_async_remote_copy(
                    src_ref=s[t].at[2 * px + py], dst_ref=land[t].at[j], send_sem=send_sems.at[t], recv_sem=recv_sems.at[t],
                    device_id=(px, py, c), device_id_type=MESH).start()
        for t in range(n):
            w = _whole(land[t], land[t], send_sems.at[t], recv_sems.at[t], (x, y, c))
            w.wait_send()
            w.wait_recv()

    out_shape = [jax.ShapeDtypeStruct((3,) + s.shape[1:], s.dtype) for s in sums]
    return pl.pallas_call(
        body, name=name, out_shape=out_shape, in_specs=[ANY] * n, out_specs=[ANY] * n,
        scratch_shapes=[pltpu.SemaphoreType.DMA((n,)), pltpu.SemaphoreType.DMA((n,))],
    )(*sums)


_HBM = pl.BlockSpec(memory_space=pltpu.HBM)
_SEM = pl.BlockSpec(memory_space=pltpu.SEMAPHORE)
_SIDE_EFFECT = pltpu.CompilerParams(has_side_effects=pltpu.SideEffectType.DATAFLOW_SIDE_EFFECTING)


def _in_hbm(arrays):
    return [pltpu.with_memory_space_constraint(a, pltpu.HBM) for a in arrays]


def _hbm_like(arrays):
    return [pltpu.HBM(a.shape, a.dtype) for a in arrays]


def _split_start(body, groups, srcs, dsts, after, *, name):
    n = len(srcs)
    extra = list(after)

    def kernel_body(*refs):
        sems = refs[2 * n + len(extra):2 * n + len(extra) + 2 * len(groups)]
        body(refs[:n], refs[n:2 * n], sems[0::2], sems[1::2])
        refs[-1][...] = jnp.zeros_like(refs[-1])

    sem_shapes = [pltpu.SemaphoreType.DMA((g,)) for g in groups for _ in range(2)]
    outs = pl.pallas_call(
        kernel_body, name=name,
        out_shape=(*sem_shapes, *_hbm_like(srcs), *_hbm_like(dsts), jax.ShapeDtypeStruct((8, LANES), F32)),
        in_specs=[_HBM] * (2 * n) + [ANY] * len(extra),
        out_specs=(*[_SEM] * len(sem_shapes), *[_HBM] * (2 * n), pl.BlockSpec(memory_space=pltpu.VMEM)),
        input_output_aliases={i: len(sem_shapes) + i for i in range(2 * n)}, compiler_params=_SIDE_EFFECT,
    )(*_in_hbm(srcs), *_in_hbm(dsts), *extra)
    ns = len(sem_shapes)
    sems = [(outs[2 * g], outs[2 * g + 1]) for g in range(len(groups))]
    return sems, list(outs[ns:ns + n]), list(outs[ns + n:ns + 2 * n]), outs[-1]


def _split_wait(sent, landed, srcs, dsts, send_sems, recv_sems, after, *, name):
    n = len(srcs)

    def kernel_body(*refs):
        dst, send, recv = refs[n:2 * n], refs[2 * n], refs[2 * n + 1]
        x, y, c = _place()
        for t in range(n):
            _whole(sent(dst[t]), sent(dst[t]), send.at[t], recv.at[t], (x, y, c)).wait_send()
            _whole(landed(dst[t]), landed(dst[t]), send.at[t], recv.at[t], (x, y, c)).wait_recv()

    outs = pl.pallas_call(
        kernel_body, name=name, out_shape=(*_hbm_like(srcs), *_hbm_like(dsts)),
        in_specs=[_HBM] * (2 * n) + [_SEM, _SEM, ANY], out_specs=[_HBM] * (2 * n),
        input_output_aliases={i: i for i in range(2 * n)}, compiler_params=_SIDE_EFFECT,
    )(*srcs, *dsts, send_sems, recv_sems, after)
    return list(outs[:n]), list(outs[n:])


def _gather_start(groups, after, *, name):
    shards = [s for g in groups for s in g]
    fulls = [lax.empty((N_DEV * s.shape[0], s.shape[1]), s.dtype) for s in shards]

    def body(sh, full, send_sems, recv_sems):
        x, y, c = _place()
        me = 4 * x + 2 * y + c
        t = 0
        for gi, g in enumerate(groups):
            for ti in range(len(g)):
                r = sh[t].shape[0]
                dst = full[t].at[pl.ds(pl.multiple_of(me * r, BF16_ROWS), r), :]
                pltpu.make_async_copy(sh[t], dst, recv_sems[gi].at[ti]).start()
                for k in (1, 2, 4, 6, 3, 5, 7):
                    peer = (1 - x if k & 4 else x, 1 - y if k & 2 else y, 1 - c if k & 1 else c)
                    pltpu.make_async_remote_copy(src_ref=sh[t], dst_ref=dst, send_sem=send_sems[gi].at[ti],
                                                 recv_sem=recv_sems[gi].at[ti], device_id=peer, device_id_type=MESH).start()
                t += 1

    sems, shards, fulls, token = _split_start(body, [len(g) for g in groups], shards, fulls, after, name=name)
    out, t = [], 0
    for g, sem in zip(groups, sems):
        out.append((sem, shards[t:t + len(g)], fulls[t:t + len(g)]))
        t += len(g)
    return out, token


def _gather_wait(pending, after, *, name):
    (send_sems, recv_sems), shards, fulls = pending
    seven = lambda full: full.at[pl.ds(0, 7 * (full.shape[0] // N_DEV)), :]
    return _split_wait(seven, lambda full: full, shards, fulls, send_sems, recv_sems, after, name=name)


def _chips_start(sums, after, *, name):
    lands = [lax.empty((3,) + s.shape[1:], s.dtype) for s in sums]

    def body(s, land, send_sems, recv_sems):
        x, y, c = _place()
        for t in range(len(sums)):
            for j, (px, py) in enumerate([(1 - x, y), (x, 1 - y), (1 - x, 1 - y)]):
                pltpu.make_async_remote_copy(src_ref=s[t].at[2 * px + py], dst_ref=land[t].at[j], send_sem=send_sems[0].at[t],
                                             recv_sem=recv_sems[0].at[t], device_id=(px, py, c), device_id_type=MESH).start()

    sems, sums, lands, token = _split_start(body, [len(sums)], sums, lands, after, name=name)
    return (sems[0], sums, lands), token


def _chips_wait(pending, after, *, name):
    (send_sems, recv_sems), sums, lands = pending
    return _split_wait(lambda land: land, lambda land: land, sums, lands, send_sems, recv_sems, after, name=name)


def _chip_sum(g, land, core, *, name):
    L, _, _, r, C = g.shape

    def body(core_ref, g_ref, l_ref, o_ref):
        o_ref[...] = (g_ref[...].astype(F32) + l_ref[...].astype(F32)).astype(BF16)

    grid_spec = pltpu.PrefetchScalarGridSpec(
        num_scalar_prefetch=1, grid=(4, L),
        in_specs=[pl.BlockSpec((None, None, None, r, C), lambda k, l, core_ref: (l, k, core_ref[0], 0, 0)),
                  pl.BlockSpec((None, None, r, C), lambda k, l, core_ref: (k, l, 0, 0))],
        out_specs=pl.BlockSpec((None, None, r, C), lambda k, l, core_ref: (k, l, 0, 0)))
    return pl.pallas_call(body, name=name, out_shape=jax.ShapeDtypeStruct((4, L, r, C), BF16), grid_spec=grid_spec,
                          compiler_params=_cparams(("parallel", "parallel")))(core, g, land)


def _final_sum(sums, land, chip, *, name):
    _, L, r, C = sums.shape

    def body(chip_ref, s_ref, a_ref, b_ref, c_ref, o_ref):
        o_ref[...] = ((s_ref[...].astype(F32) + a_ref[...].astype(F32)) + b_ref[...].astype(F32)) + c_ref[...].astype(F32)

    slot = lambda j: pl.BlockSpec((None, None, r, C), lambda l, chip_ref: (j, l, 0, 0))
    grid_spec = pltpu.PrefetchScalarGridSpec(
        num_scalar_prefetch=1, grid=(L,),
        in_specs=[pl.BlockSpec((None, None, r, C), lambda l, chip_ref: (chip_ref[0], l, 0, 0)), slot(0), slot(1), slot(2)],
        out_specs=pl.BlockSpec((None, r, C), lambda l, chip_ref: (l, 0, 0)))
    return pl.pallas_call(body, name=name, out_shape=jax.ShapeDtypeStruct((L, r, C), F32), grid_spec=grid_spec,
                          compiler_params=_cparams(("parallel",)))(chip, sums, land, land, land)


def _exchange(v, reduce, *, name):
    R, C = v.shape

    def body(v_ref, o_ref, *scratch):
        if reduce:
            buf, send_sems, recv_sems = scratch
        else:
            buf = o_ref
            send_sems, recv_sems = scratch
        x, y, c = _place()
        me = 4 * x + 2 * y + c
        buf[me] = v_ref[...]
        copies = []
        for k in range(1, N_DEV):
            kx, ky, kc = (k >> 2) & 1, (k >> 1) & 1, k & 1
            peer = (1 - x if kx else x, 1 - y if ky else y, 1 - c if kc else c)
            cp = pltpu.make_async_remote_copy(src_ref=v_ref, dst_ref=buf.at[me], send_sem=send_sems.at[k - 1],
                                              recv_sem=recv_sems.at[k - 1], device_id=peer, device_id_type=MESH)
            cp.start()
            copies.append(cp)
        for cp in copies:
            cp.wait_recv()
        for cp in copies:
            cp.wait_send()
        if reduce:
            acc = buf[0]
            for d in range(1, N_DEV):
                acc = acc + buf[d]
            o_ref[...] = acc

    sems = [pltpu.SemaphoreType.DMA((N_DEV - 1,)), pltpu.SemaphoreType.DMA((N_DEV - 1,))]
    vm = pl.BlockSpec(memory_space=pltpu.VMEM)
    if reduce:
        return pl.pallas_call(body, name=name, out_shape=jax.ShapeDtypeStruct((R, C), F32), in_specs=[vm], out_specs=vm,
                              scratch_shapes=[pltpu.VMEM((N_DEV, R, C), F32)] + sems)(v)
    return pl.pallas_call(body, name=name, out_shape=jax.ShapeDtypeStruct((N_DEV, R, C), F32), in_specs=[vm], out_specs=vm,
                          scratch_shapes=sems)(v)


def _local_step(x, mem, target, norms, conv_w, depth, n_a, get_w, put_g):
    S, D = x.shape
    main = D - MEM_WIDTH
    heads = main // HEAD_DIM
    row = lambda v: v.reshape(1, D)

    mem_n = _rmsnorm(mem, row(norms["mem_norm"]), name="mem_norm")
    saved = []
    kv = hk = x_kv = w_kv = None
    for i in range(depth):
        W = get_w(i, x)
        st = {"x": x, "W": W}
        h = _rmsnorm(x, row(norms["mix_norm"][i]), name=f"mix_norm{i}")
        mkv = _mm(mem_n, W["mkv"], "nn", BF16, name=f"mkv{i}")
        if i < n_a:
            p = _mm(h, W["a"], "nt", BF16, name=f"a_in{i}")
            y_main = _conv_fwd(p, conv_w[i], main, name=f"conv{i}")
            qblk = 3 * main // MEM_WIDTH
        else:
            p = _mm(h, W["b"], "nn", BF16, name=f"b_in{i}")
            y_main, st["o32"] = _sb_fwd(p, kv, heads, name=f"sb{i}")
            qblk = main // MEM_WIDTH
        y_mem = _memattn_fwd(p, qblk, mkv, name=f"memattn{i}")
        y = jnp.concatenate([y_main, y_mem], axis=1)
        xm = _mm(y, W["o"], "nn", F32, residual=x, name=f"w_o{i}")
        h2 = _rmsnorm(xm, row(norms["ffn_norm"][i]), name=f"ffn_norm{i}")
        gate, up, act = _ffn_up(h2, W["g"], W["u"], name=f"ffn_up{i}")
        x = _mm(act, W["d"], "nn", F32, residual=xm, name=f"w_down{i}")
        st.update(h=h, mkv=mkv, p=p, qblk=qblk, y=y, xm=xm, h2=h2, gate=gate, up=up, act=act)
        saved.append(st)
        if i == n_a - 1:
            x_kv, w_kv = x, W["kv"]
            hk = _rmsnorm(x, row(norms["kv_norm"]), name="kv_norm")
            kv = _mm(hk, w_kv, "nt", BF16, name="w_kv")

    dx, dxb, dg_final, loss = _loss_head(x, row(norms["final_norm"]), target, name="loss_head")

    dg_mix, dg_ffn, dconv = [None] * depth, [None] * depth, [None] * n_a
    dmem_n = dk = dv = dg_kv = g_kv = None
    for i in reversed(range(depth)):
        st = saved[i]
        W, g = st["W"], {}
        dgate, dup = _ffn_down_bwd(dxb, W["d"], st["gate"], st["up"], name=f"ffn_down_bwd{i}")
        g["d"] = _mm(st["act"], dxb, "tn", BF16, name=f"g_w_down{i}")
        g["g"] = _mm(dgate, st["h2"], "tn", BF16, name=f"g_w_gate{i}")
        g["u"] = _mm(dup, st["h2"], "tn", BF16, name=f"g_w_up{i}")
        dh2 = _mm(dgate, W["g"], "nn", F32, name=f"d_h2g{i}")
        dh2 = _mm(dup, W["u"], "nn", F32, residual=dh2, name=f"d_h2u{i}")
        dx, dxb, dg_ffn[i] = _rmsnorm_bwd(st["xm"], row(norms["ffn_norm"][i]), dh2, dx, name=f"ffn_norm_bwd{i}")
        dy = _mm(dxb, W["o"], "nt", BF16, name=f"d_y{i}")
        g["o"] = _mm(st["y"], dxb, "tn", BF16, name=f"g_w_o{i}")
        zero = 0.0
        if i == 0:
            early = [kl for kl in _layer_keys(i, n_a) if kl[0] in g]
            zero = put_g(early, g, False)
            g = {}
        dqmem, dmk, dmv = _memattn_bwd(st["p"], st["qblk"], st["mkv"], dy, main // MEM_WIDTH, name=f"memattn_bwd{i}")
        dmkv = jnp.concatenate([dmk, dmv], axis=1)
        g["mkv"] = _mm(mem_n, dmkv, "tn", BF16, name=f"g_w_mem_kv{i}")
        dmem_n = _mm(dmkv, W["mkv"], "nt", F32, residual=dmem_n, name=f"d_mem_n{i}")
        if i < n_a:
            db, dc, du, dconv[i] = _conv_bwd(st["p"], conv_w[i] + zero, dy, main, name=f"conv_bwd{i}")
            dp = jnp.concatenate([db, dc, du, dqmem], axis=1)
            g["a"] = _mm(dp, st["h"], "tn", BF16, name=f"g_a_in{i}")
            dh = _mm(dp, W["a"], "nn", F32, name=f"d_h{i}")
        else:
            dq, dk, dv = _sb_bwd(st["p"], kv, st["o32"], dy, heads, dk, dv, name=f"sb_bwd{i}")
            dp = jnp.concatenate([dq, dqmem], axis=1)
            g["b"] = _mm(st["h"], dp, "tn", BF16, name=f"g_b_in{i}")
            dh = _mm(dp, W["b"], "nt", F32, name=f"d_h{i}")
        if i == n_a - 1:
            g["kv"] = g_kv
        zero = put_g([kl for kl in _layer_keys(i, n_a) if kl[0] in g], g, i == 0)
        dx, dxb, dg_mix[i] = _rmsnorm_bwd(st["x"], row(norms["mix_norm"][i]) + zero, dh, dx, name=f"mix_norm_bwd{i}")
        if i == n_a:
            dkv = jnp.concatenate([dk, dv], axis=1).astype(BF16)
            g_kv = _mm(dkv, hk, "tn", BF16, name="g_w_kv")
            dhk = _mm(dkv, w_kv, "nn", F32, name="d_hk")
            dx, dxb, dg_kv = _rmsnorm_bwd(x_kv, row(norms["kv_norm"]), dhk, dx, name="kv_norm_bwd")
    _, _, dg_mem = _rmsnorm_bwd(mem, row(norms["mem_norm"]), dmem_n, None, name="mem_norm_bwd")

    small = {"mix_norm": jnp.concatenate(dg_mix, axis=0), "ffn_norm": jnp.concatenate(dg_ffn, axis=0), "kv_norm": dg_kv[0],
             "mem_norm": dg_mem[0], "final_norm": dg_final[0], "conv_w": jnp.stack(dconv, axis=0)}
    return loss, dx, small


_COL_SHARDED = ("a", "kv", "g", "u")
_NAMES = {"a": "a_in", "kv": "w_kv_shared", "g": "w_gate", "u": "w_up", "b": "b_in", "o": "w_o", "d": "w_down", "mkv": "w_mem_kv"}
_ORDER = ("a", "kv", "g", "u", "d", "b", "o", "mkv")
_WEIGHTS = ("mix_norm", "a_in", "conv_w", "b_in", "kv_norm", "w_kv_shared", "w_mem_kv", "w_o", "ffn_norm", "w_gate", "w_up",
            "w_down", "mem_norm", "final_norm")


def _layer_keys(i, n_a):
    keys = [("a", i) if i < n_a else ("b", i - n_a), ("g", i), ("u", i), ("d", i), ("o", i), ("mkv", i)]
    return keys + [("kv", 0)] if i == n_a - 1 else keys


def _canonical(key, w):
    w3 = w if w.ndim == 3 else w[None]
    if key in _COL_SHARDED:
        w3 = jnp.transpose(w3, (0, 2, 1))
    return w3


def _uncanonical(key, g3, like):
    if key in _COL_SHARDED:
        g3 = jnp.transpose(g3, (0, 2, 1))
    return g3.reshape(like.shape)


def _pad_rows(flat, C):
    n = flat.shape[0]
    rows = -(-n // C)
    return jnp.pad(flat, (0, rows * C - n)).reshape(rows, C)


def kernel(x, mem, mix_norm, a_in, conv_w, b_in, kv_norm, w_kv_shared, w_mem_kv, w_o, ffn_norm, w_gate, w_up, w_down, mem_norm, final_norm, loss_target, m_mix_norm, m_a_in, m_conv_w, m_b_in, m_kv_norm, m_w_kv_shared, m_w_mem_kv, m_w_o, m_ffn_norm, m_w_gate, m_w_up, m_w_down, m_mem_norm, m_final_norm, v_mix_norm, v_a_in, v_conv_w, v_b_in, v_kv_norm, v_w_kv_shared, v_w_mem_kv, v_w_o, v_ffn_norm, v_w_gate, v_w_up, v_w_down, v_mem_norm, v_final_norm):
    weights = dict(mix_norm=mix_norm, a_in=a_in, conv_w=conv_w, b_in=b_in, kv_norm=kv_norm, w_kv_shared=w_kv_shared,
                   w_mem_kv=w_mem_kv, w_o=w_o, ffn_norm=ffn_norm, w_gate=w_gate, w_up=w_up, w_down=w_down,
                   mem_norm=mem_norm, final_norm=final_norm)
    moments_m = dict(mix_norm=m_mix_norm, a_in=m_a_in, conv_w=m_conv_w, b_in=m_b_in, kv_norm=m_kv_norm,
                     w_kv_shared=m_w_kv_shared, w_mem_kv=m_w_mem_kv, w_o=m_w_o, ffn_norm=m_ffn_norm, w_gate=m_w_gate,
                     w_up=m_w_up, w_down=m_w_down, mem_norm=m_mem_norm, final_norm=m_final_norm)
    moments_v = dict(mix_norm=v_mix_norm, a_in=v_a_in, conv_w=v_conv_w, b_in=v_b_in, kv_norm=v_kv_norm,
                     w_kv_shared=v_w_kv_shared, w_mem_kv=v_w_mem_kv, w_o=v_w_o, ffn_norm=v_ffn_norm, w_gate=v_w_gate,
                     w_up=v_w_up, w_down=v_w_down, mem_norm=v_mem_norm, final_norm=v_final_norm)
    D = x.shape[-1]
    depth, n_a = w_o.shape[0], a_in.shape[0]
    xi, yi, ci = _place()
    me = 4 * xi + 2 * yi + ci
    core = ci.reshape(1).astype(jnp.int32)
    chip = (2 * xi + yi).reshape(1).astype(jnp.int32)

    cw_shape = conv_w.shape
    cw_rows = _pad_rows(conv_w.reshape(-1), D)
    cw_rows = jnp.pad(cw_rows, ((0, 8 - cw_rows.shape[0]), (0, 0)))
    cw_gathered = _exchange(cw_rows, False, name="gather_conv_w")
    n_cw = cw_shape[0] * cw_shape[1] * cw_shape[2]
    cw_all = cw_gathered.reshape(N_DEV, -1)[:, :n_cw].reshape((N_DEV,) + cw_shape)
    conv_full = jnp.transpose(cw_all, (1, 2, 0, 3)).reshape(cw_shape[0], cw_shape[1], N_DEV * cw_shape[2])

    shard3 = {k: _canonical(k, weights[_NAMES[k]]).astype(BF16) for k in _ORDER}
    shard = lambda kl: shard3[kl[0]][kl[1]]
    keys0 = _layer_keys(0, n_a)
    full0 = _all_gather_weights([shard(kl)[None] for kl in keys0], name="all_gather_layer0")
    layer_w = {0: {kl[0]: f[0] for kl, f in zip(keys0, full0)}}
    groups = [[shard(kl) for kl in _layer_keys(i, n_a)] for i in range(1, depth)]
    pending_w, started = _gather_start(groups, [full0[0], cw_gathered], name="gather_start")

    def get_w(i, after):
        if i not in layer_w:
            _, fulls = _gather_wait(pending_w[i - 1], after, name=f"gather_wait{i}")
            layer_w[i] = {kl[0]: f for kl, f in zip(_layer_keys(i, n_a), fulls)}
        return layer_w[i]

    batches = []

    def put_g(keys, g, last):
        g5 = [g[k].reshape(1, 4, 2, g[k].shape[0] // N_DEV, g[k].shape[1]) for k, _ in keys]
        tag = len(batches)
        from_sibling = _rs_sibling(g5, name=f"rs_sibling{tag}")
        sums = [_chip_sum(a, l, core, name=f"chip_sum{tag}_{k}") for (k, _), a, l in zip(keys, g5, from_sibling)]
        if last:
            batches.append((keys, (sums, _rs_chips(sums, name=f"rs_chips{tag}")), None))
            return jnp.zeros((1, 1), F32)
        pending, token = _chips_start(sums, [], name=f"rs_chips_start{tag}")
        batches.append((keys, None, pending))
        return token[:1, :1]

    norms = {k: weights[k] for k in ("mix_norm", "ffn_norm", "kv_norm", "mem_norm", "final_norm")}
    norms["mix_norm"] = mix_norm + started[:1, :1]
    loss, grad_x, small = _local_step(x[0], mem[0], loss_target[0], norms, conv_full, depth, n_a, get_w, put_g)

    shard_grads = {}
    for tag, (keys, done, pending) in enumerate(batches):
        if done is None:
            done = _chips_wait(pending, grad_x, name=f"rs_chips_wait{tag}")
        for (k, l), s, land in zip(keys, *done):
            shard_grads[(k, l)] = _final_sum(s, land, chip, name=f"final_sum_{k}{l}")[0]
    grads = {}
    for k in _ORDER:
        g3 = jnp.stack([shard_grads[(k, l)] for l in range(shard3[k].shape[0])])
        grads[_NAMES[k]] = _uncanonical(k, g3, weights[_NAMES[k]])

    order = ("mix_norm", "ffn_norm", "kv_norm", "mem_norm", "final_norm", "conv_w")
    flat = jnp.concatenate([small[k].reshape(-1) for k in order] + [loss[0, :1]])
    n_flat = flat.shape[0]
    rows = _pad_rows(flat, D)
    rows = jnp.pad(rows, ((0, (-rows.shape[0]) % 8), (0, 0)))
    total = _exchange(rows, True, name="all_reduce_small").reshape(-1)[:n_flat]
    off = 0
    for k in order:
        n = small[k].size
        grads[k] = total[off:off + n].reshape(small[k].shape)
        off += n
    loss_total = total[off]
    grads["conv_w"] = lax.dynamic_slice_in_dim(grads["conv_w"], me * cw_shape[2], cw_shape[2], axis=2)

    deltas, new_m, new_v = {}, {}, {}
    for k in _WEIGHTS:
        w = weights[k]
        two = (lambda a: a.reshape(-1, a.shape[-1])) if w.ndim > 1 else (lambda a: a.reshape(1, -1))
        d, nm, nv = _adamw(two(w), two(grads[k]), two(moments_m[k]), two(moments_v[k]), name=f"adamw_{k}")
        deltas[k], new_m[k], new_v[k] = d.reshape(w.shape), nm.reshape(w.shape), nv.reshape(w.shape)

    return (loss_total, grad_x[None], *[grads[k] for k in _WEIGHTS], *[deltas[k] for k in _WEIGHTS],
            *[new_m[k] for k in _WEIGHTS], *[new_v[k] for k in _WEIGHTS])
```

```python
import functools
import math

import jax
import jax.numpy as jnp
from jax import lax
from jax.experimental import pallas as pl
from jax.experimental.pallas import tpu as pltpu

F32 = jnp.float32
BF16 = jnp.bfloat16
MESH = pl.DeviceIdType.MESH

HEAD_DIM = 64
MEM_HEADS = 4
MEM_WIDTH = MEM_HEADS * HEAD_DIM
EPS = 1e-6
LANES = 128
BF16_ROWS = 16
VMEM_LIMIT = 56 * 1024 * 1024
N_DEV = 8

ADAM_LR = 0.001
ADAM_B1 = 0.9
ADAM_B2 = 0.999
ADAM_EPS = 1e-08
ADAM_WD = 0.01
ADAM_STEP = 10

ANY = pl.BlockSpec(memory_space=pl.ANY)


def _cparams(sem=None):
    return pltpu.CompilerParams(dimension_semantics=sem, vmem_limit_bytes=VMEM_LIMIT)


def _pick(n, cands):
    for c in cands:
        if n % c == 0:
            return c
    raise ValueError(f"no tile for {n} in {cands}")


def _mm(a, b, form, out_dtype, *, name, residual=None):
    if form == "tn":
        K, M = a.shape
    else:
        M, K = a.shape
    if form == "nt":
        N, K2 = b.shape
    else:
        K2, N = b.shape
    assert K == K2, (name, a.shape, b.shape)
    wide = (1408, 1280, 1024, 768, 512, 256, 128)
    tm = _pick(M, wide if form == "tn" else (1024, 512, 256, 128))
    tn = _pick(N, wide)
    tk = _pick(K, (1024, 1408, 1280, 768, 512, 256))
    nk = K // tk
    dims = {"nn": (((1,), (0,)), ((), ())), "nt": (((1,), (1,)), ((), ())), "tn": (((0,), (0,)), ((), ()))}[form]
    a_bytes, b_bytes = M * K * a.dtype.itemsize, N * K * b.dtype.itemsize
    n_outer = nk == 1 and (N // tn) * a_bytes + b_bytes < a_bytes + (M // tm) * b_bytes
    ij = (lambda g0, g1: (g1, g0)) if n_outer else (lambda g0, g1: (g0, g1))

    def spec(block, f):
        return pl.BlockSpec(block, lambda g0, g1, k: f(*ij(g0, g1), k))

    a_spec = spec((tk, tm), lambda i, j, k: (k, i)) if form == "tn" else spec((tm, tk), lambda i, j, k: (i, k))
    b_spec = spec((tn, tk), lambda i, j, k: (j, k)) if form == "nt" else spec((tk, tn), lambda i, j, k: (k, j))
    out_spec = spec((tm, tn), lambda i, j, k: (i, j))
    operands, in_specs = [a, b], [a_spec, b_spec]
    has_res = residual is not None
    if has_res:
        operands.append(residual)
        in_specs.append(out_spec)
    grid = (N // tn, M // tm, nk) if n_outer else (M // tm, N // tn, nk)

    def body(*refs):
        a_ref, b_ref = refs[0], refs[1]
        r_ref = refs[2] if has_res else None
        o_ref = refs[2 + int(has_res)]
        acc_ref = refs[-1]
        part = lax.dot_general(a_ref[...].astype(BF16), b_ref[...].astype(BF16), dims, preferred_element_type=F32)

        def finish(total):
            if has_res:
                total = total + r_ref[...].astype(F32)
            o_ref[...] = total.astype(out_dtype)

        if nk == 1:
            finish(part)
        else:
            k = pl.program_id(2)

            @pl.when(k == 0)
            def _():
                acc_ref[...] = part

            @pl.when(jnp.logical_and(k > 0, k < nk - 1))
            def _():
                acc_ref[...] += part

            @pl.when(k == nk - 1)
            def _():
                finish(acc_ref[...] + part)

    return pl.pallas_call(
        body, name=name, out_shape=jax.ShapeDtypeStruct((M, N), out_dtype), grid=grid, in_specs=in_specs, out_specs=out_spec,
        scratch_shapes=[pltpu.VMEM((tm, tn), F32)], compiler_params=_cparams(("parallel", "parallel", "arbitrary")),
    )(*operands)


def _rmsnorm(x, g, *, name):
    R, D = x.shape
    tr = _pick(R, (512, 256))

    def body(x_ref, g_ref, o_ref):
        xv = x_ref[...]
        r = lax.rsqrt(jnp.mean(xv * xv, axis=-1, keepdims=True) + EPS)
        o_ref[...] = (xv * r * g_ref[...]).astype(BF16)

    return pl.pallas_call(
        body, name=name, out_shape=jax.ShapeDtypeStruct((R, D), BF16), grid=(R // tr,),
        in_specs=[pl.BlockSpec((tr, D), lambda i: (i, 0)), pl.BlockSpec((1, D), lambda i: (0, 0))],
        out_specs=pl.BlockSpec((tr, D), lambda i: (i, 0)), compiler_params=_cparams(("parallel",)),
    )(x, g)


def _rmsnorm_bwd(x, g, dh, dx_in, *, name):
    R, D = x.shape
    tr = _pick(R, (512, 256))
    has_in = dx_in is not None

    def body(*refs):
        x_ref, g_ref, dh_ref = refs[:3]
        dxi_ref = refs[3] if has_in else None
        dx_ref, dxb_ref, dg_ref = refs[3 + int(has_in):]
        xv = x_ref[...]
        r = lax.rsqrt(jnp.mean(xv * xv, axis=-1, keepdims=True) + EPS)
        xhat = xv * r
        dhv = dh_ref[...].astype(F32)
        dxh = dhv * g_ref[...]
        dx = r * (dxh - xhat * jnp.mean(dxh * xhat, axis=-1, keepdims=True))
        if has_in:
            dx = dx + dxi_ref[...]
        dx_ref[...] = dx
        dxb_ref[...] = dx.astype(BF16)
        part = jnp.sum(dhv * xhat, axis=0, keepdims=True)

        @pl.when(pl.program_id(0) == 0)
        def _():
            dg_ref[...] = part

        @pl.when(pl.program_id(0) > 0)
        def _():
            dg_ref[...] += part

    row = pl.BlockSpec((tr, D), lambda i: (i, 0))
    vec = pl.BlockSpec((1, D), lambda i: (0, 0))
    ops = [x, g, dh] + ([dx_in] if has_in else [])
    return pl.pallas_call(
        body, name=name,
        out_shape=(jax.ShapeDtypeStruct((R, D), F32), jax.ShapeDtypeStruct((R, D), BF16), jax.ShapeDtypeStruct((1, D), F32)),
        grid=(R // tr,), in_specs=[row, vec, row] + ([row] if has_in else []), out_specs=(row, row, vec),
        compiler_params=_cparams(("arbitrary",)),
    )(*ops)


def _loss_head(x, g, target, *, name):
    R, D = x.shape
    tr = _pick(R, (512, 256))

    def body(x_ref, g_ref, t_ref, dx_ref, dxb_ref, dg_ref, loss_ref):
        xv = x_ref[...]
        gv = g_ref[...]
        r = lax.rsqrt(jnp.mean(xv * xv, axis=-1, keepdims=True) + EPS)
        xhat = xv * r
        err = xhat * gv - t_ref[...]
        loss = 0.5 * jnp.sum(jnp.mean(err * err, axis=-1, keepdims=True), axis=0, keepdims=True)
        dy = err * (1.0 / D)
        dxh = dy * gv
        dx = r * (dxh - xhat * jnp.mean(dxh * xhat, axis=-1, keepdims=True))
        dx_ref[...] = dx
        dxb_ref[...] = dx.astype(BF16)
        dg = jnp.sum(dy * xhat, axis=0, keepdims=True)
        lossv = jnp.broadcast_to(loss, (1, LANES))

        @pl.when(pl.program_id(0) == 0)
        def _():
            dg_ref[...] = dg
            loss_ref[...] = lossv

        @pl.when(pl.program_id(0) > 0)
        def _():
            dg_ref[...] += dg
            loss_ref[...] += lossv

    row = pl.BlockSpec((tr, D), lambda i: (i, 0))
    vec = pl.BlockSpec((1, D), lambda i: (0, 0))
    return pl.pallas_call(
        body, name=name,
        out_shape=(jax.ShapeDtypeStruct((R, D), F32), jax.ShapeDtypeStruct((R, D), BF16), jax.ShapeDtypeStruct((1, D), F32),
                   jax.ShapeDtypeStruct((1, LANES), F32)),
        grid=(R // tr,), in_specs=[row, vec, row], out_specs=(row, row, vec, pl.BlockSpec((1, LANES), lambda i: (0, 0))),
        compiler_params=_cparams(("arbitrary",)),
    )(x, g, target)


def _conv_taps(gv, S):
    t = lax.broadcasted_iota(jnp.int32, gv.shape, 0)
    g1 = jnp.where(t >= 1, pltpu.roll(gv, 1, 0), 0.0)
    g2 = jnp.where(t >= 2, pltpu.roll(gv, 2, 0), 0.0)
    return g1, g2


def _conv_fwd(p, w, main, *, name):
    S = p.shape[0]
    tc = LANES
    nb = main // tc

    def body(b_ref, c_ref, u_ref, w_ref, y_ref):
        gv = c_ref[...].astype(F32) * u_ref[...].astype(F32)
        g1, g2 = _conv_taps(gv, S)
        cv = w_ref[0:1, :] * g2 + w_ref[1:2, :] * g1 + w_ref[2:3, :] * gv
        y_ref[...] = (b_ref[...].astype(F32) * cv).astype(BF16)

    col = lambda off: pl.BlockSpec((S, tc), lambda j: (0, off + j))
    return pl.pallas_call(
        body, name=name, out_shape=jax.ShapeDtypeStruct((S, main), BF16), grid=(nb,),
        in_specs=[col(0), col(nb), col(2 * nb), pl.BlockSpec((3, tc), lambda j: (0, j))],
        out_specs=pl.BlockSpec((S, tc), lambda j: (0, j)), compiler_params=_cparams(("parallel",)),
    )(p, p, p, w)


def _conv_bwd(p, w, dy, main, *, name):
    S = p.shape[0]
    tc = LANES
    nb = main // tc

    def body(b_ref, c_ref, u_ref, w_ref, dy_ref, db_ref, dc_ref, du_ref, dw_ref):
        cvv, uv = c_ref[...].astype(F32), u_ref[...].astype(F32)
        gv = cvv * uv
        g1, g2 = _conv_taps(gv, S)
        w0, w1, w2 = w_ref[0:1, :], w_ref[1:2, :], w_ref[2:3, :]
        dyv = dy_ref[...].astype(F32)
        db_ref[...] = (dyv * (w0 * g2 + w1 * g1 + w2 * gv)).astype(BF16)
        dcv = dyv * b_ref[...].astype(F32)
        t = lax.broadcasted_iota(jnp.int32, dcv.shape, 0)
        n1 = jnp.where(t <= S - 2, pltpu.roll(dcv, S - 1, 0), 0.0)
        n2 = jnp.where(t <= S - 3, pltpu.roll(dcv, S - 2, 0), 0.0)
        dg = w2 * dcv + w1 * n1 + w0 * n2
        dc_ref[...] = (dg * uv).astype(BF16)
        du_ref[...] = (dg * cvv).astype(BF16)
        dw_ref[0:1, :] = jnp.sum(dcv * g2, axis=0, keepdims=True)
        dw_ref[1:2, :] = jnp.sum(dcv * g1, axis=0, keepdims=True)
        dw_ref[2:3, :] = jnp.sum(dcv * gv, axis=0, keepdims=True)

    col = lambda off: pl.BlockSpec((S, tc), lambda j: (0, off + j))
    out = jax.ShapeDtypeStruct((S, main), BF16)
    return pl.pallas_call(
        body, name=name, out_shape=(out, out, out, jax.ShapeDtypeStruct((3, main), F32)), grid=(nb,),
        in_specs=[col(0), col(nb), col(2 * nb), pl.BlockSpec((3, tc), lambda j: (0, j)), col(0)],
        out_specs=(col(0), col(0), col(0), pl.BlockSpec((3, tc), lambda j: (0, j))),
        compiler_params=_cparams(("parallel",)),
    )(p, p, p, w, dy)


def _head_mask(width, h):
    lane = lax.broadcasted_iota(jnp.int32, (1, width), 1)
    return jnp.logical_and(lane >= h * HEAD_DIM, lane < (h + 1) * HEAD_DIM)


_NT = (((1,), (1,)), ((), ()))
_NN = (((1,), (0,)), ((), ()))
_TN = (((0,), (0,)), ((), ()))


def _dot(a, b, dims):
    return lax.dot_general(a, b, dims, preferred_element_type=F32)


def _mem_probs(qh, kv):
    s = _dot(qh, kv, _NT) * (1.0 / math.sqrt(HEAD_DIM))
    s = s - jnp.max(s, axis=-1, keepdims=True)
    e = jnp.exp(s)
    return e / jnp.sum(e, axis=-1, keepdims=True)


def _memattn_fwd(p, qblk, mkv, *, name):
    S = p.shape[0]
    M = mkv.shape[0]
    W = MEM_WIDTH
    tq = _pick(S, (512, 256))

    def body(q_ref, k_ref, v_ref, o_ref):
        q = q_ref[...].astype(BF16)
        kv, vv = k_ref[...], v_ref[...]
        out = jnp.zeros((tq, W), F32)
        for h in range(MEM_HEADS):
            m = _head_mask(W, h)
            pr = _mem_probs(jnp.where(m, q, jnp.zeros_like(q)), kv)
            out = jnp.where(m, _dot(pr.astype(BF16), vv, _NN), out)
        o_ref[...] = out.astype(BF16)

    return pl.pallas_call(
        body, name=name, out_shape=jax.ShapeDtypeStruct((S, W), BF16), grid=(S // tq,),
        in_specs=[pl.BlockSpec((tq, W), lambda i: (i, qblk)), pl.BlockSpec((M, W), lambda i: (0, 0)),
                  pl.BlockSpec((M, W), lambda i: (0, 1))],
        out_specs=pl.BlockSpec((tq, W), lambda i: (i, 0)), compiler_params=_cparams(("parallel",)),
    )(p, mkv, mkv)


def _memattn_bwd(p, qblk, mkv, dy, dyblk, *, name):
    S = p.shape[0]
    M = mkv.shape[0]
    W = MEM_WIDTH
    tq = _pick(S, (512, 256))
    scale = 1.0 / math.sqrt(HEAD_DIM)

    def body(q_ref, k_ref, v_ref, do_ref, dq_ref, dk_ref, dv_ref, dk_acc, dv_acc):
        q = q_ref[...].astype(BF16)
        do = do_ref[...].astype(BF16)
        kv, vv = k_ref[...], v_ref[...]
        dq = jnp.zeros((tq, W), F32)
        dk = jnp.zeros((M, W), F32)
        dv = jnp.zeros((M, W), F32)
        for h in range(MEM_HEADS):
            m = _head_mask(W, h)
            qh = jnp.where(m, q, jnp.zeros_like(q))
            doh = jnp.where(m, do, jnp.zeros_like(do))
            pr = _mem_probs(qh, kv)
            dpr = _dot(doh, vv, _NT)
            ds = (pr * (dpr - jnp.sum(dpr * pr, axis=-1, keepdims=True)) * scale).astype(BF16)
            dq = jnp.where(m, _dot(ds, kv, _NN), dq)
            dk = dk + _dot(ds, qh, _TN)
            dv = dv + _dot(pr.astype(BF16), doh, _TN)
        dq_ref[...] = dq.astype(BF16)
        i = pl.program_id(0)

        @pl.when(i == 0)
        def _():
            dk_acc[...] = dk
            dv_acc[...] = dv

        @pl.when(i > 0)
        def _():
            dk_acc[...] += dk
            dv_acc[...] += dv

        @pl.when(i == S // tq - 1)
        def _():
            dk_ref[...] = dk_acc[...].astype(BF16)
            dv_ref[...] = dv_acc[...].astype(BF16)

    kspec = lambda c: pl.BlockSpec((M, W), lambda i: (0, c))
    return pl.pallas_call(
        body, name=name,
        out_shape=(jax.ShapeDtypeStruct((S, W), BF16), jax.ShapeDtypeStruct((M, W), BF16), jax.ShapeDtypeStruct((M, W), BF16)),
        grid=(S // tq,),
        in_specs=[pl.BlockSpec((tq, W), lambda i: (i, qblk)), kspec(0), kspec(1), pl.BlockSpec((tq, W), lambda i: (i, dyblk))],
        out_specs=(pl.BlockSpec((tq, W), lambda i: (i, 0)), kspec(0), kspec(0)),
        scratch_shapes=[pltpu.VMEM((M, W), F32), pltpu.VMEM((M, W), F32)],
        compiler_params=_cparams(("arbitrary",)),
    )(p, mkv, mkv, dy)


SB_TQ = 256
SB_CLAMP = 80.0
SB_DEAD = 110.0


SB_CHUNK = 64


def _by_rows(fn, *arrays):
    rows = next(a for a in arrays if a is not None).shape[0]
    outs = [fn(*[None if a is None else a[r0:r0 + SB_CHUNK] for a in arrays]) for r0 in range(0, rows, SB_CHUNK)]
    return tuple(jnp.concatenate(col, axis=0) for col in zip(*outs))


def _sb_scores(qh, kb, causal):
    def chain(z, mask):
        z = jnp.clip(z, -SB_CLAMP, SB_CLAMP)
        w = 1.0 + jnp.exp(z)
        sp = jnp.log(w)
        zs = z - sp
        if mask is not None:
            sp = jnp.where(mask, sp, 0.0)
            zs = jnp.where(mask, zs, -1e30)
            w = jnp.where(mask, w, 1.0)
        return zs, sp.astype(BF16), jnp.sum(sp, axis=1, keepdims=True), w

    return _by_rows(chain, _dot(qh, kb, _NT), causal)


def _sb_weights(zs, spb, tri, carry):
    return _by_rows(lambda zs_c, t_c, c_c: (jnp.exp(zs_c - (t_c + c_c)).astype(BF16),), zs, _dot(spb, tri, _NN), carry)[0]


def _sb_live(carry):
    return jnp.min(carry) <= SB_DEAD


def _stack_heads(v, m0):
    zero = jnp.zeros_like(v)
    return jnp.concatenate([jnp.where(m0, v, zero), jnp.where(m0, zero, v)], axis=0)


def _stacked_causal(tq):
    r = lax.broadcasted_iota(jnp.int32, (2 * tq, tq), 0)
    c = lax.broadcasted_iota(jnp.int32, (2 * tq, tq), 1)
    return c < jnp.where(r >= tq, r - tq, r)


def _sb_fwd(p, kv, heads, *, name):
    S = p.shape[0]
    tq = SB_TQ
    npair = heads // 2

    def body(q_ref, k_ref, v_ref, o_ref, o32_ref):
        qi = pl.program_id(1)
        r = lax.broadcasted_iota(jnp.int32, (tq, tq), 0)
        c = lax.broadcasted_iota(jnp.int32, (tq, tq), 1)
        tri = (r > c).astype(BF16)
        causal = _stacked_causal(tq)
        m0 = _head_mask(LANES, 0)
        qh = _stack_heads(q_ref[...] * jnp.asarray(1.0 / math.sqrt(HEAD_DIM), BF16), m0)

        def block(j, carry, acc, mask):
            off = pl.multiple_of(j * tq, tq)
            kb = k_ref[pl.ds(off, tq), :]
            vb = v_ref[pl.ds(off, tq), :]
            zs, spb, sp_sum, _ = _sb_scores(qh, kb, mask)
            acc = acc + _dot(_sb_weights(zs, spb, tri, carry), vb, _NN)
            return carry + sp_sum, acc

        st = block(qi, jnp.zeros((2 * tq, 1), F32), jnp.zeros((2 * tq, LANES), F32), causal)
        odd = qi % 2
        st = lax.cond(odd == 1, lambda s: block(qi - 1, s[0], s[1], None), lambda s: s, st)

        def pair(s):
            it, _, carry, acc = s
            j = qi - 1 - odd - 2 * it
            carry, acc = block(j, carry, acc, None)
            carry, acc = block(j - 1, carry, acc, None)
            return it + 1, _sb_live(carry), carry, acc

        _, _, carry, acc = lax.while_loop(lambda s: jnp.logical_and(s[0] < qi // 2, s[1]), pair,
                                          (jnp.int32(0), _sb_live(st[0]), st[0], st[1]))
        out = jnp.where(m0, acc[:tq], acc[tq:])
        o_ref[...] = out.astype(BF16)
        o32_ref[...] = out

    W = heads * HEAD_DIM
    qspec = pl.BlockSpec((tq, LANES), lambda hp, i: (i, hp))
    return pl.pallas_call(
        body, name=name, out_shape=(jax.ShapeDtypeStruct((S, W), BF16), jax.ShapeDtypeStruct((S, W), F32)), grid=(npair, S // tq),
        in_specs=[qspec, pl.BlockSpec((S, LANES), lambda hp, i: (0, hp)), pl.BlockSpec((S, LANES), lambda hp, i: (0, npair + hp))],
        out_specs=(qspec, qspec), compiler_params=_cparams(("parallel", "arbitrary")),
    )(p, kv, kv)


def _sb_bwd(p, kv, o32, dy, heads, dk_in, dv_in, *, name):
    S = p.shape[0]
    tq = SB_TQ
    npair = heads // 2
    has_in = dk_in is not None
    scale = 1.0 / math.sqrt(HEAD_DIM)

    def body(*refs):
        q_ref, k_ref, v_ref, o_ref, do_ref = refs[:5]
        dq_ref, dk_ref, dv_ref = refs[5 + 2 * int(has_in):]
        qi = pl.program_id(1)

        @pl.when(qi == 0)
        def _():
            if has_in:
                dk_ref[...] = refs[5][...]
                dv_ref[...] = refs[6][...]
            else:
                dk_ref[...] = jnp.zeros_like(dk_ref)
                dv_ref[...] = jnp.zeros_like(dv_ref)

        r = lax.broadcasted_iota(jnp.int32, (tq, tq), 0)
        c = lax.broadcasted_iota(jnp.int32, (tq, tq), 1)
        tri = (r > c).astype(BF16)
        tri_low = (r < c).astype(BF16)
        causal = _stacked_causal(tq)
        m0 = _head_mask(LANES, 0)
        qh = _stack_heads(q_ref[...] * jnp.asarray(scale, BF16), m0)
        do = do_ref[...]
        doh = _stack_heads(do, m0)
        dov = do.astype(F32) * o_ref[...]
        dsum = jnp.concatenate([jnp.sum(jnp.where(m0, dov, 0.0), axis=1, keepdims=True),
                                jnp.sum(jnp.where(m0, 0.0, dov), axis=1, keepdims=True)], axis=0)

        def block(j, carry, gcarry, acc, mask):
            off = pl.multiple_of(j * tq, tq)
            kb = k_ref[pl.ds(off, tq), :]
            vb = v_ref[pl.ds(off, tq), :]
            zs, spb, sp_sum, w = _sb_scores(qh, kb, mask)
            ab = _sb_weights(zs, spb, tri, carry)

            def grads(ab_c, da_c):
                g = ab_c.astype(F32) * da_c
                return g, g.astype(BF16), jnp.sum(g, axis=1, keepdims=True)

            g, gb, g_sum = _by_rows(grads, ab, _dot(doh, vb, _NT))
            gcarry = gcarry + g_sum

            def logit_grads(g_c, w_c, low_c, left_c):
                rinv = 1.0 / w_c
                return ((g_c * rinv - (left_c + low_c) * (1.0 - rinv)).astype(BF16),)

            dzs = _by_rows(logit_grads, g, w, _dot(gb, tri_low, _NN), dsum - gcarry)[0]
            acc = acc + _dot(dzs, kb, _NN)
            dk_ref[pl.ds(off, tq), :] += _dot(dzs, qh, _TN)
            dv_ref[pl.ds(off, tq), :] += _dot(ab, doh, _TN)
            return (carry + sp_sum, gcarry, acc)

        zero = jnp.zeros((2 * tq, 1), F32)
        st = block(qi, zero, zero, jnp.zeros((2 * tq, LANES), F32), causal)
        odd = qi % 2
        st = lax.cond(odd == 1, lambda s: block(qi - 1, s[0], s[1], s[2], None), lambda s: s, st)

        def pair(s):
            j = qi - 1 - odd - 2 * s[0]
            b = block(j, s[2], s[3], s[4], None)
            b = block(j - 1, b[0], b[1], b[2], None)
            return (s[0] + 1, _sb_live(b[0])) + b

        st = lax.while_loop(lambda s: jnp.logical_and(s[0] < qi // 2, s[1]), pair, (jnp.int32(0), _sb_live(st[0])) + st)[2:]
        dq_ref[...] = (jnp.where(m0, st[2][:tq], st[2][tq:]) * scale).astype(BF16)

    W = heads * HEAD_DIM
    qspec = pl.BlockSpec((tq, LANES), lambda hp, i: (i, hp))
    seq = lambda off: pl.BlockSpec((S, LANES), lambda hp, i: (0, off + hp))
    ops = [p, kv, kv, o32, dy] + ([dk_in, dv_in] if has_in else [])
    return pl.pallas_call(
        body, name=name,
        out_shape=(jax.ShapeDtypeStruct((S, W), BF16), jax.ShapeDtypeStruct((S, W), F32), jax.ShapeDtypeStruct((S, W), F32)),
        grid=(npair, S // tq),
        in_specs=[qspec, seq(0), seq(npair), qspec, qspec] + ([seq(0), seq(0)] if has_in else []),
        out_specs=(qspec, seq(0), seq(0)),
        compiler_params=_cparams(("parallel", "arbitrary")),
    )(*ops)


def _ffn_up(h, wg, wu, *, name):
    S, D = h.shape
    F = wg.shape[0]
    tm = _pick(S, (512, 256))
    tn = _pick(F, (1408, 1024, 512, 256, 128))

    def body(h_ref, g_ref, u_ref, gate_ref, up_ref, act_ref):
        hv = h_ref[...]
        g = _dot(hv, g_ref[...], _NT)
        u = _dot(hv, u_ref[...], _NT)
        gate_ref[...] = g.astype(BF16)
        up_ref[...] = u.astype(BF16)
        act_ref[...] = (g * jax.nn.sigmoid(g) * u).astype(BF16)

    wspec = pl.BlockSpec((tn, D), lambda j, i: (j, 0))
    ospec = pl.BlockSpec((tm, tn), lambda j, i: (i, j))
    out = jax.ShapeDtypeStruct((S, F), BF16)
    return pl.pallas_call(
        body, name=name, out_shape=(out, out, out), grid=(F // tn, S // tm),
        in_specs=[pl.BlockSpec((tm, D), lambda j, i: (i, 0)), wspec, wspec], out_specs=(ospec, ospec, ospec),
        compiler_params=_cparams(("parallel", "parallel")),
    )(h, wg, wu)


def _ffn_down_bwd(dx, wd, gate, up, *, name):
    S, D = dx.shape
    F = wd.shape[0]
    tm = _pick(S, (512, 256))
    tn = _pick(F, (1408, 1024, 512, 256, 128))

    def body(dx_ref, w_ref, g_ref, u_ref, dg_ref, du_ref):
        da = _dot(dx_ref[...], w_ref[...], _NT)
        gv, uv = g_ref[...].astype(F32), u_ref[...].astype(F32)
        s = jax.nn.sigmoid(gv)
        silu = gv * s
        dg_ref[...] = (da * uv * (s + silu * (1.0 - s))).astype(BF16)
        du_ref[...] = (da * silu).astype(BF16)

    ospec = pl.BlockSpec((tm, tn), lambda j, i: (i, j))
    out = jax.ShapeDtypeStruct((S, F), BF16)
    return pl.pallas_call(
        body, name=name, out_shape=(out, out), grid=(F // tn, S // tm),
        in_specs=[pl.BlockSpec((tm, D), lambda j, i: (i, 0)), pl.BlockSpec((tn, D), lambda j, i: (j, 0)), ospec, ospec],
        out_specs=(ospec, ospec), compiler_params=_cparams(("parallel", "parallel")),
    )(dx, wd, gate, up)


def _adamw(w, g, m, v, *, name):
    R, C = w.shape
    tr = R
    for cand in (1024, 512, 256, 128, 64, 32, 16, 8):
        if R % cand == 0 and cand * C * 4 <= (1 << 20):
            tr = cand
            break
    bc1 = 1.0 - ADAM_B1 ** ADAM_STEP
    bc2 = 1.0 - ADAM_B2 ** ADAM_STEP

    def body(w_ref, g_ref, m_ref, v_ref, d_ref, nm_ref, nv_ref):
        gv = g_ref[...]
        nm = ADAM_B1 * m_ref[...] + (1.0 - ADAM_B1) * gv
        nv = ADAM_B2 * v_ref[...] + (1.0 - ADAM_B2) * (gv * gv)
        nm_ref[...] = nm
        nv_ref[...] = nv
        d_ref[...] = -ADAM_LR * ((nm / bc1) / (jnp.sqrt(nv / bc2) + ADAM_EPS) + ADAM_WD * w_ref[...])

    blk = pl.BlockSpec((tr, C), lambda i: (i, 0))
    out = jax.ShapeDtypeStruct((R, C), F32)
    return pl.pallas_call(body, name=name, out_shape=(out, out, out), grid=(R // tr,), in_specs=[blk] * 4,
                          out_specs=(blk, blk, blk), compiler_params=_cparams(("parallel",)))(w, g, m, v)


def _place():
    x, y, c = lax.axis_index("x"), lax.axis_index("y"), lax.axis_index("c")
    return x, y, c


def _all_gather_weights(shards, *, name):
    n = len(shards)

    def body(*refs):
        sh, full = refs[:n], refs[n:2 * n]
        send_sems, recv_sems, local_sems = refs[2 * n:]
        x, y, c = _place()
        me, sibling = (x, y, c), (x, y, 1 - c)
        chips = [(1 - x, y), (x, 1 - y), (1 - x, 1 - y)]

        def rows(t, px, py, pc):
            r = sh[t].shape[1]
            return full[t].at[:, pl.ds(pl.multiple_of((4 * px + 2 * py + pc) * r, BF16_ROWS), r), :]

        def copy(t, k, block, to, src=None):
            return pltpu.make_async_remote_copy(
                src_ref=rows(t, *block) if src is None else src, dst_ref=rows(t, *block),
                send_sem=send_sems.at[7 * t + k], recv_sem=recv_sems.at[7 * t + k], device_id=to, device_id_type=MESH)

        started = []
        for t in range(n):
            mine = pltpu.make_async_copy(sh[t], rows(t, *me), local_sems.at[t])
            mine.start()
            started.append(mine)
        sends = []
        for t in range(n):
            first = [copy(t, 0, me, sibling, src=sh[t])]
            first += [copy(t, 1 + j, me, (*chip, c), src=sh[t]) for j, chip in enumerate(chips)]
            for cp in first:
                cp.start()
            sends += first
        for t in range(n):
            for j, chip in enumerate(chips):
                copy(t, 1 + j, (*chip, c), me).wait_recv()
                fwd = copy(t, 4 + j, (*chip, c), sibling)
                fwd.start()
                sends.append(fwd)
        for t in range(n):
            copy(t, 0, sibling, me).wait_recv()
            for j, chip in enumerate(chips):
                copy(t, 4 + j, (*chip, 1 - c), me).wait_recv()
        for cp in sends:
            cp.wait_send()
        for cp in started:
            cp.wait()

    out_shape = [jax.ShapeDtypeStruct((s.shape[0], N_DEV * s.shape[1], s.shape[2]), s.dtype) for s in shards]
    return pl.pallas_call(
        body, name=name, out_shape=out_shape, in_specs=[ANY] * n, out_specs=[ANY] * n,
        scratch_shapes=[pltpu.SemaphoreType.DMA((7 * n,)), pltpu.SemaphoreType.DMA((7 * n,)), pltpu.SemaphoreType.DMA((n,))],
    )(*shards)


def _whole(ref_a, ref_b, send_sem, recv_sem, me):
    return pltpu.make_async_remote_copy(src_ref=ref_a, dst_ref=ref_b, send_sem=send_sem, recv_sem=recv_sem,
                                        device_id=me, device_id_type=MESH)


def _rs_sibling(grads, *, name):
    n = len(grads)

    def body(*refs):
        g, land = refs[:n], refs[n:2 * n]
        send_sems, recv_sems = refs[2 * n:]
        x, y, c = _place()
        for t in range(n):
            for k in range(4):
                pltpu.make_async_remote_copy(
                    src_ref=g[t].at[:, k, 1 - c], dst_ref=land[t].at[k], send_sem=send_sems.at[t], recv_sem=recv_sems.at[t],
                    device_id=(x, y, 1 - c), device_id_type=MESH).start()
        for t in range(n):
            w = _whole(land[t], land[t], send_sems.at[t], recv_sems.at[t], (x, y, c))
            w.wait_send()
            w.wait_recv()

    out_shape = [jax.ShapeDtypeStruct((4, s.shape[0], s.shape[3], s.shape[4]), s.dtype) for s in grads]
    return pl.pallas_call(
        body, name=name, out_shape=out_shape, in_specs=[ANY] * n, out_specs=[ANY] * n,
        scratch_shapes=[pltpu.SemaphoreType.DMA((n,)), pltpu.SemaphoreType.DMA((n,))],
    )(*grads)


def _rs_chips(sums, *, name):
    n = len(sums)

    def body(*refs):
        s, land = refs[:n], refs[n:2 * n]
        send_sems, recv_sems = refs[2 * n:]
        x, y, c = _place()
        chips = [(1 - x, y), (x, 1 - y), (1 - x, 1 - y)]
        for t in range(n):
            for j, (px, py) in enumerate(chips):
                pltpu.make_async_remote_copy(
                    src_ref=s[t].at[2 * px + py], dst_ref=land[t].at[j], send_sem=send_sems.at[t], recv_sem=recv_sems.at[t],
                    device_id=(px, py, c), device_id_type=MESH).start()
        for t in range(n):
            w = _whole(land[t], land[t], send_sems.at[t], recv_sems.at[t], (x, y, c))
            w.wait_send()
            w.wait_recv()

    out_shape = [jax.ShapeDtypeStruct((3,) + s.shape[1:], s.dtype) for s in sums]
    return pl.pallas_call(
        body, name=name, out_shape=out_shape, in_specs=[ANY] * n, out_specs=[ANY] * n,
        scratch_shapes=[pltpu.SemaphoreType.DMA((n,)), pltpu.SemaphoreType.DMA((n,))],
    )(*sums)


def _chip_sum(g, land, core, *, name):
    L, _, _, r, C = g.shape

    def body(core_ref, g_ref, l_ref, o_ref):
        o_ref[...] = (g_ref[...].astype(F32) + l_ref[...].astype(F32)).astype(BF16)

    grid_spec = pltpu.PrefetchScalarGridSpec(
        num_scalar_prefetch=1, grid=(4, L),
        in_specs=[pl.BlockSpec((None, None, None, r, C), lambda k, l, core_ref: (l, k, core_ref[0], 0, 0)),
                  pl.BlockSpec((None, None, r, C), lambda k, l, core_ref: (k, l, 0, 0))],
        out_specs=pl.BlockSpec((None, None, r, C), lambda k, l, core_ref: (k, l, 0, 0)))
    return pl.pallas_call(body, name=name, out_shape=jax.ShapeDtypeStruct((4, L, r, C), BF16), grid_spec=grid_spec,
                          compiler_params=_cparams(("parallel", "parallel")))(core, g, land)


def _final_sum(sums, land, chip, *, name):
    _, L, r, C = sums.shape

    def body(chip_ref, s_ref, a_ref, b_ref, c_ref, o_ref):
        o_ref[...] = ((s_ref[...].astype(F32) + a_ref[...].astype(F32)) + b_ref[...].astype(F32)) + c_ref[...].astype(F32)

    slot = lambda j: pl.BlockSpec((None, None, r, C), lambda l, chip_ref: (j, l, 0, 0))
    grid_spec = pltpu.PrefetchScalarGridSpec(
        num_scalar_prefetch=1, grid=(L,),
        in_specs=[pl.BlockSpec((None, None, r, C), lambda l, chip_ref: (chip_ref[0], l, 0, 0)), slot(0), slot(1), slot(2)],
        out_specs=pl.BlockSpec((None, r, C), lambda l, chip_ref: (l, 0, 0)))
    return pl.pallas_call(body, name=name, out_shape=jax.ShapeDtypeStruct((L, r, C), F32), grid_spec=grid_spec,
                          compiler_params=_cparams(("parallel",)))(chip, sums, land, land, land)


def _exchange(v, reduce, *, name):
    R, C = v.shape

    def body(v_ref, o_ref, *scratch):
        if reduce:
            buf, send_sems, recv_sems = scratch
        else:
            buf = o_ref
            send_sems, recv_sems = scratch
        x, y, c = _place()
        me = 4 * x + 2 * y + c
        buf[me] = v_ref[...]
        copies = []
        for k in range(1, N_DEV):
            kx, ky, kc = (k >> 2) & 1, (k >> 1) & 1, k & 1
            peer = (1 - x if kx else x, 1 - y if ky else y, 1 - c if kc else c)
            cp = pltpu.make_async_remote_copy(src_ref=v_ref, dst_ref=buf.at[me], send_sem=send_sems.at[k - 1],
                                              recv_sem=recv_sems.at[k - 1], device_id=peer, device_id_type=MESH)
            cp.start()
            copies.append(cp)
        for cp in copies:
            cp.wait_recv()
        for cp in copies:
            cp.wait_send()
        if reduce:
            acc = buf[0]
            for d in range(1, N_DEV):
                acc = acc + buf[d]
            o_ref[...] = acc

    sems = [pltpu.SemaphoreType.DMA((N_DEV - 1,)), pltpu.SemaphoreType.DMA((N_DEV - 1,))]
    vm = pl.BlockSpec(memory_space=pltpu.VMEM)
    if reduce:
        return pl.pallas_call(body, name=name, out_shape=jax.ShapeDtypeStruct((R, C), F32), in_specs=[vm], out_specs=vm,
                              scratch_shapes=[pltpu.VMEM((N_DEV, R, C), F32)] + sems)(v)
    return pl.pallas_call(body, name=name, out_shape=jax.ShapeDtypeStruct((N_DEV, R, C), F32), in_specs=[vm], out_specs=vm,
                          scratch_shapes=sems)(v)


def _local_step(x, mem, target, norms, conv_w, depth, n_a, get_w, put_g):
    S, D = x.shape
    main = D - MEM_WIDTH
    heads = main // HEAD_DIM
    row = lambda v: v.reshape(1, D)

    mem_n = _rmsnorm(mem, row(norms["mem_norm"]), name="mem_norm")
    saved = []
    kv = hk = x_kv = w_kv = None
    for i in range(depth):
        W = get_w(i)
        st = {"x": x, "W": W}
        h = _rmsnorm(x, row(norms["mix_norm"][i]), name=f"mix_norm{i}")
        mkv = _mm(mem_n, W["mkv"], "nn", BF16, name=f"mkv{i}")
        if i < n_a:
            p = _mm(h, W["a"], "nt", BF16, name=f"a_in{i}")
            y_main = _conv_fwd(p, conv_w[i], main, name=f"conv{i}")
            qblk = 3 * main // MEM_WIDTH
        else:
            p = _mm(h, W["b"], "nn", BF16, name=f"b_in{i}")
            y_main, st["o32"] = _sb_fwd(p, kv, heads, name=f"sb{i}")
            qblk = main // MEM_WIDTH
        y_mem = _memattn_fwd(p, qblk, mkv, name=f"memattn{i}")
        y = jnp.concatenate([y_main, y_mem], axis=1)
        xm = _mm(y, W["o"], "nn", F32, residual=x, name=f"w_o{i}")
        h2 = _rmsnorm(xm, row(norms["ffn_norm"][i]), name=f"ffn_norm{i}")
        gate, up, act = _ffn_up(h2, W["g"], W["u"], name=f"ffn_up{i}")
        x = _mm(act, W["d"], "nn", F32, residual=xm, name=f"w_down{i}")
        st.update(h=h, mkv=mkv, p=p, qblk=qblk, y=y, xm=xm, h2=h2, gate=gate, up=up, act=act)
        saved.append(st)
        if i == n_a - 1:
            x_kv, w_kv = x, W["kv"]
            hk = _rmsnorm(x, row(norms["kv_norm"]), name="kv_norm")
            kv = _mm(hk, w_kv, "nt", BF16, name="w_kv")

    dx, dxb, dg_final, loss = _loss_head(x, row(norms["final_norm"]), target, name="loss_head")

    dg_mix, dg_ffn, dconv = [None] * depth, [None] * depth, [None] * n_a
    dmem_n = dk = dv = dg_kv = g_kv = None
    for i in reversed(range(depth)):
        st = saved[i]
        W, g = st["W"], {}
        dgate, dup = _ffn_down_bwd(dxb, W["d"], st["gate"], st["up"], name=f"ffn_down_bwd{i}")
        g["d"] = _mm(st["act"], dxb, "tn", BF16, name=f"g_w_down{i}")
        g["g"] = _mm(dgate, st["h2"], "tn", BF16, name=f"g_w_gate{i}")
        g["u"] = _mm(dup, st["h2"], "tn", BF16, name=f"g_w_up{i}")
        dh2 = _mm(dgate, W["g"], "nn", F32, name=f"d_h2g{i}")
        dh2 = _mm(dup, W["u"], "nn", F32, residual=dh2, name=f"d_h2u{i}")
        dx, dxb, dg_ffn[i] = _rmsnorm_bwd(st["xm"], row(norms["ffn_norm"][i]), dh2, dx, name=f"ffn_norm_bwd{i}")
        dy = _mm(dxb, W["o"], "nt", BF16, name=f"d_y{i}")
        g["o"] = _mm(st["y"], dxb, "tn", BF16, name=f"g_w_o{i}")
        dqmem, dmk, dmv = _memattn_bwd(st["p"], st["qblk"], st["mkv"], dy, main // MEM_WIDTH, name=f"memattn_bwd{i}")
        dmkv = jnp.concatenate([dmk, dmv], axis=1)
        g["mkv"] = _mm(mem_n, dmkv, "tn", BF16, name=f"g_w_mem_kv{i}")
        dmem_n = _mm(dmkv, W["mkv"], "nt", F32, residual=dmem_n, name=f"d_mem_n{i}")
        if i < n_a:
            db, dc, du, dconv[i] = _conv_bwd(st["p"], conv_w[i], dy, main, name=f"conv_bwd{i}")
            dp = jnp.concatenate([db, dc, du, dqmem], axis=1)
            g["a"] = _mm(dp, st["h"], "tn", BF16, name=f"g_a_in{i}")
            dh = _mm(dp, W["a"], "nn", F32, name=f"d_h{i}")
        else:
            dq, dk, dv = _sb_bwd(st["p"], kv, st["o32"], dy, heads, dk, dv, name=f"sb_bwd{i}")
            dp = jnp.concatenate([dq, dqmem], axis=1)
            g["b"] = _mm(st["h"], dp, "tn", BF16, name=f"g_b_in{i}")
            dh = _mm(dp, W["b"], "nt", F32, name=f"d_h{i}")
        if i == n_a - 1:
            g["kv"] = g_kv
        put_g(_layer_keys(i, n_a), g)
        dx, dxb, dg_mix[i] = _rmsnorm_bwd(st["x"], row(norms["mix_norm"][i]), dh, dx, name=f"mix_norm_bwd{i}")
        if i == n_a:
            dkv = jnp.concatenate([dk, dv], axis=1).astype(BF16)
            g_kv = _mm(dkv, hk, "tn", BF16, name="g_w_kv")
            dhk = _mm(dkv, w_kv, "nn", F32, name="d_hk")
            dx, dxb, dg_kv = _rmsnorm_bwd(x_kv, row(norms["kv_norm"]), dhk, dx, name="kv_norm_bwd")
    _, _, dg_mem = _rmsnorm_bwd(mem, row(norms["mem_norm"]), dmem_n, None, name="mem_norm_bwd")

    small = {"mix_norm": jnp.concatenate(dg_mix, axis=0), "ffn_norm": jnp.concatenate(dg_ffn, axis=0), "kv_norm": dg_kv[0],
             "mem_norm": dg_mem[0], "final_norm": dg_final[0], "conv_w": jnp.stack(dconv, axis=0)}
    return loss, dx, small


_COL_SHARDED = ("a", "kv", "g", "u")
_NAMES = {"a": "a_in", "kv": "w_kv_shared", "g": "w_gate", "u": "w_up", "b": "b_in", "o": "w_o", "d": "w_down", "mkv": "w_mem_kv"}
_ORDER = ("a", "kv", "g", "u", "d", "b", "o", "mkv")
_WEIGHTS = ("mix_norm", "a_in", "conv_w", "b_in", "kv_norm", "w_kv_shared", "w_mem_kv", "w_o", "ffn_norm", "w_gate", "w_up",
            "w_down", "mem_norm", "final_norm")


def _layer_keys(i, n_a):
    keys = [("a", i) if i < n_a else ("b", i - n_a), ("g", i), ("u", i), ("d", i), ("o", i), ("mkv", i)]
    return keys + [("kv", 0)] if i == n_a - 1 else keys


def _canonical(key, w):
    w3 = w if w.ndim == 3 else w[None]
    if key in _COL_SHARDED:
        w3 = jnp.transpose(w3, (0, 2, 1))
    return w3


def _uncanonical(key, g3, like):
    if key in _COL_SHARDED:
        g3 = jnp.transpose(g3, (0, 2, 1))
    return g3.reshape(like.shape)


def _pad_rows(flat, C):
    n = flat.shape[0]
    rows = -(-n // C)
    return jnp.pad(flat, (0, rows * C - n)).reshape(rows, C)


def kernel(x, mem, mix_norm, a_in, conv_w, b_in, kv_norm, w_kv_shared, w_mem_kv, w_o, ffn_norm, w_gate, w_up, w_down, mem_norm, final_norm, loss_target, m_mix_norm, m_a_in, m_conv_w, m_b_in, m_kv_norm, m_w_kv_shared, m_w_mem_kv, m_w_o, m_ffn_norm, m_w_gate, m_w_up, m_w_down, m_mem_norm, m_final_norm, v_mix_norm, v_a_in, v_conv_w, v_b_in, v_kv_norm, v_w_kv_shared, v_w_mem_kv, v_w_o, v_ffn_norm, v_w_gate, v_w_up, v_w_down, v_mem_norm, v_final_norm):
    weights = dict(mix_norm=mix_norm, a_in=a_in, conv_w=conv_w, b_in=b_in, kv_norm=kv_norm, w_kv_shared=w_kv_shared,
                   w_mem_kv=w_mem_kv, w_o=w_o, ffn_norm=ffn_norm, w_gate=w_gate, w_up=w_up, w_down=w_down,
                   mem_norm=mem_norm, final_norm=final_norm)
    moments_m = dict(mix_norm=m_mix_norm, a_in=m_a_in, conv_w=m_conv_w, b_in=m_b_in, kv_norm=m_kv_norm,
                     w_kv_shared=m_w_kv_shared, w_mem_kv=m_w_mem_kv, w_o=m_w_o, ffn_norm=m_ffn_norm, w_gate=m_w_gate,
                     w_up=m_w_up, w_down=m_w_down, mem_norm=m_mem_norm, final_norm=m_final_norm)
    moments_v = dict(mix_norm=v_mix_norm, a_in=v_a_in, conv_w=v_conv_w, b_in=v_b_in, kv_norm=v_kv_norm,
                     w_kv_shared=v_w_kv_shared, w_mem_kv=v_w_mem_kv, w_o=v_w_o, ffn_norm=v_ffn_norm, w_gate=v_w_gate,
                     w_up=v_w_up, w_down=v_w_down, mem_norm=v_mem_norm, final_norm=v_final_norm)
    D = x.shape[-1]
    depth, n_a = w_o.shape[0], a_in.shape[0]
    xi, yi, ci = _place()
    me = 4 * xi + 2 * yi + ci
    core = ci.reshape(1).astype(jnp.int32)
    chip = (2 * xi + yi).reshape(1).astype(jnp.int32)

    cw_shape = conv_w.shape
    cw_rows = _pad_rows(conv_w.reshape(-1), D)
    cw_rows = jnp.pad(cw_rows, ((0, 8 - cw_rows.shape[0]), (0, 0)))
    cw_gathered = _exchange(cw_rows, False, name="gather_conv_w")
    n_cw = cw_shape[0] * cw_shape[1] * cw_shape[2]
    cw_all = cw_gathered.reshape(N_DEV, -1)[:, :n_cw].reshape((N_DEV,) + cw_shape)
    conv_full = jnp.transpose(cw_all, (1, 2, 0, 3)).reshape(cw_shape[0], cw_shape[1], N_DEV * cw_shape[2])

    shard3 = {k: _canonical(k, weights[_NAMES[k]]).astype(BF16) for k in _ORDER}
    all_keys = [kl for i in range(depth) for kl in _layer_keys(i, n_a)]
    fulls = _all_gather_weights([shard3[k][l][None] for k, l in all_keys], name="all_gather_weights")
    full = {kl: f[0] for kl, f in zip(all_keys, fulls)}
    bf16_grads = {}

    def get_w(i):
        return {k: full[(k, l)] for k, l in _layer_keys(i, n_a)}

    def put_g(keys, g):
        bf16_grads.update({(k, l): g[k] for k, l in keys})

    norms = {k: weights[k] for k in ("mix_norm", "ffn_norm", "kv_norm", "mem_norm", "final_norm")}
    loss, grad_x, small = _local_step(x[0], mem[0], loss_target[0], norms, conv_full, depth, n_a, get_w, put_g)

    g5 = [bf16_grads[kl].reshape(1, 4, 2, bf16_grads[kl].shape[0] // N_DEV, bf16_grads[kl].shape[1]) for kl in all_keys]
    from_sibling = _rs_sibling(g5, name="reduce_scatter_sibling")
    sums = [_chip_sum(a, s, core, name=f"chip_sum_{k}{l}") for (k, l), a, s in zip(all_keys, g5, from_sibling)]
    from_chips = _rs_chips(sums, name="reduce_scatter_chips")
    shard_grads = {(k, l): _final_sum(s, land, chip, name=f"final_sum_{k}{l}")[0]
                   for (k, l), s, land in zip(all_keys, sums, from_chips)}
    grads = {}
    for k in _ORDER:
        g3 = jnp.stack([shard_grads[(k, l)] for l in range(shard3[k].shape[0])])
        grads[_NAMES[k]] = _uncanonical(k, g3, weights[_NAMES[k]])

    order = ("mix_norm", "ffn_norm", "kv_norm", "mem_norm", "final_norm", "conv_w")
    flat = jnp.concatenate([small[k].reshape(-1) for k in order] + [loss[0, :1]])
    n_flat = flat.shape[0]
    rows = _pad_rows(flat, D)
    rows = jnp.pad(rows, ((0, (-rows.shape[0]) % 8), (0, 0)))
    total = _exchange(rows, True, name="all_reduce_small").reshape(-1)[:n_flat]
    off = 0
    for k in order:
        n = small[k].size
        grads[k] = total[off:off + n].reshape(small[k].shape)
        off += n
    loss_total = total[off]
    grads["conv_w"] = lax.dynamic_slice_in_dim(grads["conv_w"], me * cw_shape[2], cw_shape[2], axis=2)

    deltas, new_m, new_v = {}, {}, {}
    for k in _WEIGHTS:
        w = weights[k]
        two = (lambda a: a.reshape(-1, a.shape[-1])) if w.ndim > 1 else (lambda a: a.reshape(1, -1))
        d, nm, nv = _adamw(two(w), two(grads[k]), two(moments_m[k]), two(moments_v[k]), name=f"adamw_{k}")
        deltas[k], new_m[k], new_v[k] = d.reshape(w.shape), nm.reshape(w.shape), nv.reshape(w.shape)

    return (loss_total, grad_x[None], *[grads[k] for k in _WEIGHTS], *[deltas[k] for k in _WEIGHTS],
            *[new_m[k] for k in _WEIGHTS], *[new_v[k] for k in _WEIGHTS])
```

```python
import functools
import math

import jax
import jax.numpy as jnp
from jax import lax
from jax.experimental import pallas as pl
from jax.experimental.pallas import tpu as pltpu

F32 = jnp.float32
BF16 = jnp.bfloat16
MESH = pl.DeviceIdType.MESH

HEAD_DIM = 64
MEM_HEADS = 4
MEM_WIDTH = MEM_HEADS * HEAD_DIM
EPS = 1e-6
LANES = 128
BF16_ROWS = 16
VMEM_LIMIT = 56 * 1024 * 1024
N_DEV = 8

ADAM_LR = 0.001
ADAM_B1 = 0.9
ADAM_B2 = 0.999
ADAM_EPS = 1e-08
ADAM_WD = 0.01
ADAM_STEP = 10

ANY = pl.BlockSpec(memory_space=pl.ANY)


def _cparams(sem=None):
    return pltpu.CompilerParams(dimension_semantics=sem, vmem_limit_bytes=VMEM_LIMIT)


def _pick(n, cands):
    for c in cands:
        if n % c == 0:
            return c
    raise ValueError(f"no tile for {n} in {cands}")


def _mm(a, b, form, out_dtype, *, name, residual=None):
    if form == "tn":
        K, M = a.shape
    else:
        M, K = a.shape
    if form == "nt":
        N, K2 = b.shape
    else:
        K2, N = b.shape
    assert K == K2, (name, a.shape, b.shape)
    wide = (1408, 1280, 1024, 768, 512, 256, 128)
    tm = _pick(M, wide if form == "tn" else (1024, 512, 256, 128))
    tn = _pick(N, wide)
    tk = _pick(K, (1024, 1408, 1280, 768, 512, 256))
    nk = K // tk
    dims = {"nn": (((1,), (0,)), ((), ())), "nt": (((1,), (1,)), ((), ())), "tn": (((0,), (0,)), ((), ()))}[form]
    a_bytes, b_bytes = M * K * a.dtype.itemsize, N * K * b.dtype.itemsize
    n_outer = nk == 1 and (N // tn) * a_bytes + b_bytes < a_bytes + (M // tm) * b_bytes
    ij = (lambda g0, g1: (g1, g0)) if n_outer else (lambda g0, g1: (g0, g1))

    def spec(block, f):
        return pl.BlockSpec(block, lambda g0, g1, k: f(*ij(g0, g1), k))

    a_spec = spec((tk, tm), lambda i, j, k: (k, i)) if form == "tn" else spec((tm, tk), lambda i, j, k: (i, k))
    b_spec = spec((tn, tk), lambda i, j, k: (j, k)) if form == "nt" else spec((tk, tn), lambda i, j, k: (k, j))
    out_spec = spec((tm, tn), lambda i, j, k: (i, j))
    operands, in_specs = [a, b], [a_spec, b_spec]
    has_res = residual is not None
    if has_res:
        operands.append(residual)
        in_specs.append(out_spec)
    grid = (N // tn, M // tm, nk) if n_outer else (M // tm, N // tn, nk)

    def body(*refs):
        a_ref, b_ref = refs[0], refs[1]
        r_ref = refs[2] if has_res else None
        o_ref = refs[2 + int(has_res)]
        acc_ref = refs[-1]
        part = lax.dot_general(a_ref[...].astype(BF16), b_ref[...].astype(BF16), dims, preferred_element_type=F32)

        def finish(total):
            if has_res:
                total = total + r_ref[...].astype(F32)
            o_ref[...] = total.astype(out_dtype)

        if nk == 1:
            finish(part)
        else:
            k = pl.program_id(2)

            @pl.when(k == 0)
            def _():
                acc_ref[...] = part

            @pl.when(jnp.logical_and(k > 0, k < nk - 1))
            def _():
                acc_ref[...] += part

            @pl.when(k == nk - 1)
            def _():
                finish(acc_ref[...] + part)

    return pl.pallas_call(
        body, name=name, out_shape=jax.ShapeDtypeStruct((M, N), out_dtype), grid=grid, in_specs=in_specs, out_specs=out_spec,
        scratch_shapes=[pltpu.VMEM((tm, tn), F32)], compiler_params=_cparams(("parallel", "parallel", "arbitrary")),
    )(*operands)


def _rmsnorm(x, g, *, name):
    R, D = x.shape
    tr = _pick(R, (512, 256))

    def body(x_ref, g_ref, o_ref):
        xv = x_ref[...]
        r = lax.rsqrt(jnp.mean(xv * xv, axis=-1, keepdims=True) + EPS)
        o_ref[...] = (xv * r * g_ref[...]).astype(BF16)

    return pl.pallas_call(
        body, name=name, out_shape=jax.ShapeDtypeStruct((R, D), BF16), grid=(R // tr,),
        in_specs=[pl.BlockSpec((tr, D), lambda i: (i, 0)), pl.BlockSpec((1, D), lambda i: (0, 0))],
        out_specs=pl.BlockSpec((tr, D), lambda i: (i, 0)), compiler_params=_cparams(("parallel",)),
    )(x, g)


def _rmsnorm_bwd(x, g, dh, dx_in, *, name):
    R, D = x.shape
    tr = _pick(R, (512, 256))
    has_in = dx_in is not None

    def body(*refs):
        x_ref, g_ref, dh_ref = refs[:3]
        dxi_ref = refs[3] if has_in else None
        dx_ref, dxb_ref, dg_ref = refs[3 + int(has_in):]
        xv = x_ref[...]
        r = lax.rsqrt(jnp.mean(xv * xv, axis=-1, keepdims=True) + EPS)
        xhat = xv * r
        dhv = dh_ref[...].astype(F32)
        dxh = dhv * g_ref[...]
        dx = r * (dxh - xhat * jnp.mean(dxh * xhat, axis=-1, keepdims=True))
        if has_in:
            dx = dx + dxi_ref[...]
        dx_ref[...] = dx
        dxb_ref[...] = dx.astype(BF16)
        part = jnp.sum(dhv * xhat, axis=0, keepdims=True)

        @pl.when(pl.program_id(0) == 0)
        def _():
            dg_ref[...] = part

        @pl.when(pl.program_id(0) > 0)
        def _():
            dg_ref[...] += part

    row = pl.BlockSpec((tr, D), lambda i: (i, 0))
    vec = pl.BlockSpec((1, D), lambda i: (0, 0))
    ops = [x, g, dh] + ([dx_in] if has_in else [])
    return pl.pallas_call(
        body, name=name,
        out_shape=(jax.ShapeDtypeStruct((R, D), F32), jax.ShapeDtypeStruct((R, D), BF16), jax.ShapeDtypeStruct((1, D), F32)),
        grid=(R // tr,), in_specs=[row, vec, row] + ([row] if has_in else []), out_specs=(row, row, vec),
        compiler_params=_cparams(("arbitrary",)),
    )(*ops)


def _mm_norm_bwd(parts, form, x, g, dx_in, *, name):
    S, K = parts[0][0].shape
    D = x.shape[1]
    tm = _pick(S, (512, 256))
    tk = _pick(K, (1024, 1408, 1280, 768, 512, 256))
    nk, P = K // tk, len(parts)
    dims = _NN if form == "nn" else _NT

    def body(*refs):
        ab = refs[:2 * P]
        x_ref, g_ref, dxi_ref, dx_ref, dxb_ref, dg_ref, acc_ref = refs[2 * P:]
        i, k = pl.program_id(0), pl.program_id(1)

        @pl.when(k == 0)
        def _():
            acc_ref[...] = jnp.zeros_like(acc_ref)

        for p in range(P):
            @pl.when(jnp.logical_and(k >= p * nk, k < (p + 1) * nk))
            def _():
                acc_ref[...] += _dot(ab[2 * p][...], ab[2 * p + 1][...], dims)

        @pl.when(k == P * nk - 1)
        def _():
            xv = x_ref[...]
            r = lax.rsqrt(jnp.mean(xv * xv, axis=-1, keepdims=True) + EPS)
            xhat = xv * r
            dhv = acc_ref[...]
            dxh = dhv * g_ref[...]
            dx = r * (dxh - xhat * jnp.mean(dxh * xhat, axis=-1, keepdims=True)) + dxi_ref[...]
            dx_ref[...] = dx
            dxb_ref[...] = dx.astype(BF16)
            part = jnp.sum(dhv * xhat, axis=0, keepdims=True)

            @pl.when(i == 0)
            def _():
                dg_ref[...] = part

            @pl.when(i > 0)
            def _():
                dg_ref[...] += part

    def kk(p):
        return lambda k: jnp.clip(k - p * nk, 0, nk - 1)

    in_specs, operands = [], []
    for p, (a, b) in enumerate(parts):
        in_specs.append(pl.BlockSpec((tm, tk), lambda i, k, f=kk(p): (i, f(k))))
        in_specs.append(pl.BlockSpec((tk, D), lambda i, k, f=kk(p): (f(k), 0)) if form == "nn"
                        else pl.BlockSpec((D, tk), lambda i, k, f=kk(p): (0, f(k))))
        operands += [a, b]
    row = pl.BlockSpec((tm, D), lambda i, k: (i, 0))
    vec = pl.BlockSpec((1, D), lambda i, k: (0, 0))
    return pl.pallas_call(
        body, name=name,
        out_shape=(jax.ShapeDtypeStruct((S, D), F32), jax.ShapeDtypeStruct((S, D), BF16), jax.ShapeDtypeStruct((1, D), F32)),
        grid=(S // tm, P * nk), in_specs=in_specs + [row, vec, row], out_specs=(row, row, vec),
        scratch_shapes=[pltpu.VMEM((tm, D), F32)], compiler_params=_cparams(("arbitrary", "arbitrary")),
    )(*operands, x, g, dx_in)


def _loss_head(x, g, target, *, name):
    R, D = x.shape
    tr = _pick(R, (512, 256))

    def body(x_ref, g_ref, t_ref, dx_ref, dxb_ref, dg_ref, loss_ref):
        xv = x_ref[...]
        gv = g_ref[...]
        r = lax.rsqrt(jnp.mean(xv * xv, axis=-1, keepdims=True) + EPS)
        xhat = xv * r
        err = xhat * gv - t_ref[...]
        loss = 0.5 * jnp.sum(jnp.mean(err * err, axis=-1, keepdims=True), axis=0, keepdims=True)
        dy = err * (1.0 / D)
        dxh = dy * gv
        dx = r * (dxh - xhat * jnp.mean(dxh * xhat, axis=-1, keepdims=True))
        dx_ref[...] = dx
        dxb_ref[...] = dx.astype(BF16)
        dg = jnp.sum(dy * xhat, axis=0, keepdims=True)
        lossv = jnp.broadcast_to(loss, (1, LANES))

        @pl.when(pl.program_id(0) == 0)
        def _():
            dg_ref[...] = dg
            loss_ref[...] = lossv

        @pl.when(pl.program_id(0) > 0)
        def _():
            dg_ref[...] += dg
            loss_ref[...] += lossv

    row = pl.BlockSpec((tr, D), lambda i: (i, 0))
    vec = pl.BlockSpec((1, D), lambda i: (0, 0))
    return pl.pallas_call(
        body, name=name,
        out_shape=(jax.ShapeDtypeStruct((R, D), F32), jax.ShapeDtypeStruct((R, D), BF16), jax.ShapeDtypeStruct((1, D), F32),
                   jax.ShapeDtypeStruct((1, LANES), F32)),
        grid=(R // tr,), in_specs=[row, vec, row], out_specs=(row, row, vec, pl.BlockSpec((1, LANES), lambda i: (0, 0))),
        compiler_params=_cparams(("arbitrary",)),
    )(x, g, target)


def _conv_taps(gv, S):
    t = lax.broadcasted_iota(jnp.int32, gv.shape, 0)
    g1 = jnp.where(t >= 1, pltpu.roll(gv, 1, 0), 0.0)
    g2 = jnp.where(t >= 2, pltpu.roll(gv, 2, 0), 0.0)
    return g1, g2


def _conv_fwd(p, w, main, *, name):
    S = p.shape[0]
    tc = LANES
    nb = main // tc

    def body(b_ref, c_ref, u_ref, w_ref, y_ref):
        gv = c_ref[...].astype(F32) * u_ref[...].astype(F32)
        g1, g2 = _conv_taps(gv, S)
        cv = w_ref[0:1, :] * g2 + w_ref[1:2, :] * g1 + w_ref[2:3, :] * gv
        y_ref[...] = (b_ref[...].astype(F32) * cv).astype(BF16)

    col = lambda off: pl.BlockSpec((S, tc), lambda j: (0, off + j))
    return pl.pallas_call(
        body, name=name, out_shape=jax.ShapeDtypeStruct((S, main), BF16), grid=(nb,),
        in_specs=[col(0), col(nb), col(2 * nb), pl.BlockSpec((3, tc), lambda j: (0, j))],
        out_specs=pl.BlockSpec((S, tc), lambda j: (0, j)), compiler_params=_cparams(("parallel",)),
    )(p, p, p, w)


def _conv_bwd(p, w, dy, main, *, name):
    S = p.shape[0]
    tc = LANES
    nb = main // tc

    def body(b_ref, c_ref, u_ref, w_ref, dy_ref, db_ref, dc_ref, du_ref, dw_ref):
        cvv, uv = c_ref[...].astype(F32), u_ref[...].astype(F32)
        gv = cvv * uv
        g1, g2 = _conv_taps(gv, S)
        w0, w1, w2 = w_ref[0:1, :], w_ref[1:2, :], w_ref[2:3, :]
        dyv = dy_ref[...].astype(F32)
        db_ref[...] = (dyv * (w0 * g2 + w1 * g1 + w2 * gv)).astype(BF16)
        dcv = dyv * b_ref[...].astype(F32)
        t = lax.broadcasted_iota(jnp.int32, dcv.shape, 0)
        n1 = jnp.where(t <= S - 2, pltpu.roll(dcv, S - 1, 0), 0.0)
        n2 = jnp.where(t <= S - 3, pltpu.roll(dcv, S - 2, 0), 0.0)
        dg = w2 * dcv + w1 * n1 + w0 * n2
        dc_ref[...] = (dg * uv).astype(BF16)
        du_ref[...] = (dg * cvv).astype(BF16)
        dw_ref[0:1, :] = jnp.sum(dcv * g2, axis=0, keepdims=True)
        dw_ref[1:2, :] = jnp.sum(dcv * g1, axis=0, keepdims=True)
        dw_ref[2:3, :] = jnp.sum(dcv * gv, axis=0, keepdims=True)

    col = lambda off: pl.BlockSpec((S, tc), lambda j: (0, off + j))
    out = jax.ShapeDtypeStruct((S, main), BF16)
    return pl.pallas_call(
        body, name=name, out_shape=(out, out, out, jax.ShapeDtypeStruct((3, main), F32)), grid=(nb,),
        in_specs=[col(0), col(nb), col(2 * nb), pl.BlockSpec((3, tc), lambda j: (0, j)), col(0)],
        out_specs=(col(0), col(0), col(0), pl.BlockSpec((3, tc), lambda j: (0, j))),
        compiler_params=_cparams(("parallel",)),
    )(p, p, p, w, dy)


def _head_mask(width, h):
    lane = lax.broadcasted_iota(jnp.int32, (1, width), 1)
    return jnp.logical_and(lane >= h * HEAD_DIM, lane < (h + 1) * HEAD_DIM)


_NT = (((1,), (1,)), ((), ()))
_NN = (((1,), (0,)), ((), ()))
_TN = (((0,), (0,)), ((), ()))


def _dot(a, b, dims):
    return lax.dot_general(a, b, dims, preferred_element_type=F32)


def _mem_probs(qh, kv):
    s = _dot(qh, kv, _NT) * (1.0 / math.sqrt(HEAD_DIM))
    s = s - jnp.max(s, axis=-1, keepdims=True)
    e = jnp.exp(s)
    return e / jnp.sum(e, axis=-1, keepdims=True)


def _memattn_fwd(p, qblk, mkv, *, name):
    S = p.shape[0]
    M = mkv.shape[0]
    W = MEM_WIDTH
    tq = _pick(S, (512, 256))

    def body(q_ref, k_ref, v_ref, o_ref):
        q = q_ref[...].astype(BF16)
        kv, vv = k_ref[...], v_ref[...]
        out = jnp.zeros((tq, W), F32)
        for h in range(MEM_HEADS):
            m = _head_mask(W, h)
            pr = _mem_probs(jnp.where(m, q, jnp.zeros_like(q)), kv)
            out = jnp.where(m, _dot(pr.astype(BF16), vv, _NN), out)
        o_ref[...] = out.astype(BF16)

    return pl.pallas_call(
        body, name=name, out_shape=jax.ShapeDtypeStruct((S, W), BF16), grid=(S // tq,),
        in_specs=[pl.BlockSpec((tq, W), lambda i: (i, qblk)), pl.BlockSpec((M, W), lambda i: (0, 0)),
                  pl.BlockSpec((M, W), lambda i: (0, 1))],
        out_specs=pl.BlockSpec((tq, W), lambda i: (i, 0)), compiler_params=_cparams(("parallel",)),
    )(p, mkv, mkv)


def _memattn_bwd(p, qblk, mkv, dy, dyblk, *, name):
    S = p.shape[0]
    M = mkv.shape[0]
    W = MEM_WIDTH
    tq = _pick(S, (512, 256))
    scale = 1.0 / math.sqrt(HEAD_DIM)

    def body(q_ref, k_ref, v_ref, do_ref, dq_ref, dk_ref, dv_ref, dk_acc, dv_acc):
        q = q_ref[...].astype(BF16)
        do = do_ref[...].astype(BF16)
        kv, vv = k_ref[...], v_ref[...]
        dq = jnp.zeros((tq, W), F32)
        dk = jnp.zeros((M, W), F32)
        dv = jnp.zeros((M, W), F32)
        for h in range(MEM_HEADS):
            m = _head_mask(W, h)
            qh = jnp.where(m, q, jnp.zeros_like(q))
            doh = jnp.where(m, do, jnp.zeros_like(do))
            pr = _mem_probs(qh, kv)
            dpr = _dot(doh, vv, _NT)
            ds = (pr * (dpr - jnp.sum(dpr * pr, axis=-1, keepdims=True)) * scale).astype(BF16)
            dq = jnp.where(m, _dot(ds, kv, _NN), dq)
            dk = dk + _dot(ds, qh, _TN)
            dv = dv + _dot(pr.astype(BF16), doh, _TN)
        dq_ref[...] = dq.astype(BF16)
        i = pl.program_id(0)

        @pl.when(i == 0)
        def _():
            dk_acc[...] = dk
            dv_acc[...] = dv

        @pl.when(i > 0)
        def _():
            dk_acc[...] += dk
            dv_acc[...] += dv

        @pl.when(i == S // tq - 1)
        def _():
            dk_ref[...] = dk_acc[...].astype(BF16)
            dv_ref[...] = dv_acc[...].astype(BF16)

    kspec = lambda c: pl.BlockSpec((M, W), lambda i: (0, c))
    return pl.pallas_call(
        body, name=name,
        out_shape=(jax.ShapeDtypeStruct((S, W), BF16), jax.ShapeDtypeStruct((M, W), BF16), jax.ShapeDtypeStruct((M, W), BF16)),
        grid=(S // tq,),
        in_specs=[pl.BlockSpec((tq, W), lambda i: (i, qblk)), kspec(0), kspec(1), pl.BlockSpec((tq, W), lambda i: (i, dyblk))],
        out_specs=(pl.BlockSpec((tq, W), lambda i: (i, 0)), kspec(0), kspec(0)),
        scratch_shapes=[pltpu.VMEM((M, W), F32), pltpu.VMEM((M, W), F32)],
        compiler_params=_cparams(("arbitrary",)),
    )(p, mkv, mkv, dy)


SB_TQ = 256
SB_CLAMP = 80.0
SB_DEAD = 110.0


SB_CHUNK = 64


def _by_rows(fn, *arrays):
    rows = next(a for a in arrays if a is not None).shape[0]
    outs = [fn(*[None if a is None else a[r0:r0 + SB_CHUNK] for a in arrays]) for r0 in range(0, rows, SB_CHUNK)]
    return tuple(jnp.concatenate(col, axis=0) for col in zip(*outs))


def _sb_scores(qh, kb, causal):
    def chain(z, mask):
        z = jnp.clip(z, -SB_CLAMP, SB_CLAMP)
        w = 1.0 + jnp.exp(z)
        sp = jnp.log(w)
        zs = z - sp
        if mask is not None:
            sp = jnp.where(mask, sp, 0.0)
            zs = jnp.where(mask, zs, -1e30)
            w = jnp.where(mask, w, 1.0)
        return zs, sp.astype(BF16), jnp.sum(sp, axis=1, keepdims=True), w

    return _by_rows(chain, _dot(qh, kb, _NT), causal)


def _sb_weights(zs, spb, tri, carry):
    return _by_rows(lambda zs_c, t_c, c_c: (jnp.exp(zs_c - (t_c + c_c)).astype(BF16),), zs, _dot(spb, tri, _NN), carry)[0]


def _sb_live(carry):
    return jnp.min(carry) <= SB_DEAD


def _stack_heads(v, m0):
    zero = jnp.zeros_like(v)
    return jnp.concatenate([jnp.where(m0, v, zero), jnp.where(m0, zero, v)], axis=0)


def _stacked_causal(tq):
    r = lax.broadcasted_iota(jnp.int32, (2 * tq, tq), 0)
    c = lax.broadcasted_iota(jnp.int32, (2 * tq, tq), 1)
    return c < jnp.where(r >= tq, r - tq, r)


def _sb_fwd(p, kv, heads, *, name):
    S = p.shape[0]
    tq = SB_TQ
    npair = heads // 2

    def body(q_ref, k_ref, v_ref, o_ref, o32_ref):
        qi = pl.program_id(1)
        r = lax.broadcasted_iota(jnp.int32, (tq, tq), 0)
        c = lax.broadcasted_iota(jnp.int32, (tq, tq), 1)
        tri = (r > c).astype(BF16)
        causal = _stacked_causal(tq)
        m0 = _head_mask(LANES, 0)
        qh = _stack_heads(q_ref[...] * jnp.asarray(1.0 / math.sqrt(HEAD_DIM), BF16), m0)

        def block(j, carry, acc, mask):
            off = pl.multiple_of(j * tq, tq)
            kb = k_ref[pl.ds(off, tq), :]
            vb = v_ref[pl.ds(off, tq), :]
            zs, spb, sp_sum, _ = _sb_scores(qh, kb, mask)
            acc = acc + _dot(_sb_weights(zs, spb, tri, carry), vb, _NN)
            return carry + sp_sum, acc

        st = (jnp.zeros((2 * tq, 1), F32), jnp.zeros((2 * tq, LANES), F32))
        st = lax.cond(qi >= 1, lambda s: block(qi - 1, *block(qi, *s, causal), None), lambda s: block(qi, *s, causal), st)
        left = jnp.maximum(qi - 1, 0)
        odd = left % 2
        st = lax.cond(jnp.logical_and(odd == 1, _sb_live(st[0])), lambda s: block(qi - 2, *s, None), lambda s: s, st)

        def pair(s):
            it, _, carry, acc = s
            j = qi - 2 - odd - 2 * it
            carry, acc = block(j, carry, acc, None)
            carry, acc = block(j - 1, carry, acc, None)
            return it + 1, _sb_live(carry), carry, acc

        _, _, carry, acc = lax.while_loop(lambda s: jnp.logical_and(s[0] < left // 2, s[1]), pair,
                                          (jnp.int32(0), _sb_live(st[0]), st[0], st[1]))
        out = jnp.where(m0, acc[:tq], acc[tq:])
        o_ref[...] = out.astype(BF16)
        o32_ref[...] = out

    W = heads * HEAD_DIM
    qspec = pl.BlockSpec((tq, LANES), lambda hp, i: (i, hp))
    return pl.pallas_call(
        body, name=name, out_shape=(jax.ShapeDtypeStruct((S, W), BF16), jax.ShapeDtypeStruct((S, W), F32)), grid=(npair, S // tq),
        in_specs=[qspec, pl.BlockSpec((S, LANES), lambda hp, i: (0, hp)), pl.BlockSpec((S, LANES), lambda hp, i: (0, npair + hp))],
        out_specs=(qspec, qspec), compiler_params=_cparams(("parallel", "arbitrary")),
    )(p, kv, kv)


def _sb_bwd(p, kv, o32, dy, heads, dk_in, dv_in, *, name):
    S = p.shape[0]
    tq = SB_TQ
    npair = heads // 2
    has_in = dk_in is not None
    scale = 1.0 / math.sqrt(HEAD_DIM)

    def body(*refs):
        q_ref, k_ref, v_ref, o_ref, do_ref = refs[:5]
        dq_ref, dk_ref, dv_ref = refs[5 + 2 * int(has_in):]
        qi = pl.program_id(1)

        @pl.when(qi == 0)
        def _():
            if has_in:
                dk_ref[...] = refs[5][...]
                dv_ref[...] = refs[6][...]
            else:
                dk_ref[...] = jnp.zeros_like(dk_ref)
                dv_ref[...] = jnp.zeros_like(dv_ref)

        r = lax.broadcasted_iota(jnp.int32, (tq, tq), 0)
        c = lax.broadcasted_iota(jnp.int32, (tq, tq), 1)
        tri = (r > c).astype(BF16)
        tri_low = (r < c).astype(BF16)
        causal = _stacked_causal(tq)
        m0 = _head_mask(LANES, 0)
        qh = _stack_heads(q_ref[...] * jnp.asarray(scale, BF16), m0)
        do = do_ref[...]
        doh = _stack_heads(do, m0)
        dov = do.astype(F32) * o_ref[...]
        dsum = jnp.concatenate([jnp.sum(jnp.where(m0, dov, 0.0), axis=1, keepdims=True),
                                jnp.sum(jnp.where(m0, 0.0, dov), axis=1, keepdims=True)], axis=0)

        def block(j, carry, gcarry, acc, mask):
            off = pl.multiple_of(j * tq, tq)
            kb = k_ref[pl.ds(off, tq), :]
            vb = v_ref[pl.ds(off, tq), :]
            zs, spb, sp_sum, w = _sb_scores(qh, kb, mask)
            ab = _sb_weights(zs, spb, tri, carry)

            def grads(ab_c, da_c):
                g = ab_c.astype(F32) * da_c
                return g, g.astype(BF16), jnp.sum(g, axis=1, keepdims=True)

            g, gb, g_sum = _by_rows(grads, ab, _dot(doh, vb, _NT))
            gcarry = gcarry + g_sum

            def logit_grads(g_c, w_c, low_c, left_c):
                rinv = 1.0 / w_c
                return ((g_c * rinv - (left_c + low_c) * (1.0 - rinv)).astype(BF16),)

            dzs = _by_rows(logit_grads, g, w, _dot(gb, tri_low, _NN), dsum - gcarry)[0]
            acc = acc + _dot(dzs, kb, _NN)
            dk_ref[pl.ds(off, tq), :] += _dot(dzs, qh, _TN)
            dv_ref[pl.ds(off, tq), :] += _dot(ab, doh, _TN)
            return (carry + sp_sum, gcarry, acc)

        zero = jnp.zeros((2 * tq, 1), F32)
        st = (zero, zero, jnp.zeros((2 * tq, LANES), F32))
        st = lax.cond(qi >= 1, lambda s: block(qi - 1, *block(qi, *s, causal), None), lambda s: block(qi, *s, causal), st)
        left = jnp.maximum(qi - 1, 0)
        odd = left % 2
        st = lax.cond(jnp.logical_and(odd == 1, _sb_live(st[0])), lambda s: block(qi - 2, *s, None), lambda s: s, st)

        def pair(s):
            j = qi - 2 - odd - 2 * s[0]
            b = block(j, s[2], s[3], s[4], None)
            b = block(j - 1, b[0], b[1], b[2], None)
            return (s[0] + 1, _sb_live(b[0])) + b

        st = lax.while_loop(lambda s: jnp.logical_and(s[0] < left // 2, s[1]), pair, (jnp.int32(0), _sb_live(st[0])) + st)[2:]
        dq_ref[...] = (jnp.where(m0, st[2][:tq], st[2][tq:]) * scale).astype(BF16)

    W = heads * HEAD_DIM
    qspec = pl.BlockSpec((tq, LANES), lambda hp, i: (i, hp))
    seq = lambda off: pl.BlockSpec((S, LANES), lambda hp, i: (0, off + hp))
    ops = [p, kv, kv, o32, dy] + ([dk_in, dv_in] if has_in else [])
    return pl.pallas_call(
        body, name=name,
        out_shape=(jax.ShapeDtypeStruct((S, W), BF16), jax.ShapeDtypeStruct((S, W), F32), jax.ShapeDtypeStruct((S, W), F32)),
        grid=(npair, S // tq),
        in_specs=[qspec, seq(0), seq(npair), qspec, qspec] + ([seq(0), seq(0)] if has_in else []),
        out_specs=(qspec, seq(0), seq(0)),
        compiler_params=_cparams(("parallel", "arbitrary")),
    )(*ops)


def _ffn_up(h, wg, wu, *, name):
    S, D = h.shape
    F = wg.shape[0]
    tm = _pick(S, (512, 256))
    tn = _pick(F, (1408, 1024, 512, 256, 128))

    def body(h_ref, g_ref, u_ref, gate_ref, up_ref, act_ref):
        hv = h_ref[...]
        g = _dot(hv, g_ref[...], _NT)
        u = _dot(hv, u_ref[...], _NT)
        gate_ref[...] = g.astype(BF16)
        up_ref[...] = u.astype(BF16)
        act_ref[...] = (g * jax.nn.sigmoid(g) * u).astype(BF16)

    wspec = pl.BlockSpec((tn, D), lambda j, i: (j, 0))
    ospec = pl.BlockSpec((tm, tn), lambda j, i: (i, j))
    out = jax.ShapeDtypeStruct((S, F), BF16)
    return pl.pallas_call(
        body, name=name, out_shape=(out, out, out), grid=(F // tn, S // tm),
        in_specs=[pl.BlockSpec((tm, D), lambda j, i: (i, 0)), wspec, wspec], out_specs=(ospec, ospec, ospec),
        compiler_params=_cparams(("parallel", "parallel")),
    )(h, wg, wu)


def _ffn_down_bwd(dx, wd, gate, up, *, name):
    S, D = dx.shape
    F = wd.shape[0]
    tm = _pick(S, (512, 256))
    tn = _pick(F, (1408, 1024, 512, 256, 128))

    def body(dx_ref, w_ref, g_ref, u_ref, dg_ref, du_ref):
        da = _dot(dx_ref[...], w_ref[...], _NT)
        gv, uv = g_ref[...].astype(F32), u_ref[...].astype(F32)
        s = jax.nn.sigmoid(gv)
        silu = gv * s
        dg_ref[...] = (da * uv * (s + silu * (1.0 - s))).astype(BF16)
        du_ref[...] = (da * silu).astype(BF16)

    ospec = pl.BlockSpec((tm, tn), lambda j, i: (i, j))
    out = jax.ShapeDtypeStruct((S, F), BF16)
    return pl.pallas_call(
        body, name=name, out_shape=(out, out), grid=(F // tn, S // tm),
        in_specs=[pl.BlockSpec((tm, D), lambda j, i: (i, 0)), pl.BlockSpec((tn, D), lambda j, i: (j, 0)), ospec, ospec],
        out_specs=(ospec, ospec), compiler_params=_cparams(("parallel", "parallel")),
    )(dx, wd, gate, up)


def _adamw(w, g, m, v, *, name):
    R, C = w.shape
    tr = R
    for cand in (1024, 512, 256, 128, 64, 32, 16, 8):
        if R % cand == 0 and cand * C * 4 <= (1 << 20):
            tr = cand
            break
    bc1 = 1.0 - ADAM_B1 ** ADAM_STEP
    bc2 = 1.0 - ADAM_B2 ** ADAM_STEP

    def body(w_ref, g_ref, m_ref, v_ref, d_ref, nm_ref, nv_ref):
        gv = g_ref[...]
        nm = ADAM_B1 * m_ref[...] + (1.0 - ADAM_B1) * gv
        nv = ADAM_B2 * v_ref[...] + (1.0 - ADAM_B2) * (gv * gv)
        nm_ref[...] = nm
        nv_ref[...] = nv
        d_ref[...] = -ADAM_LR * ((nm / bc1) / (jnp.sqrt(nv / bc2) + ADAM_EPS) + ADAM_WD * w_ref[...])

    blk = pl.BlockSpec((tr, C), lambda i: (i, 0))
    out = jax.ShapeDtypeStruct((R, C), F32)
    return pl.pallas_call(body, name=name, out_shape=(out, out, out), grid=(R // tr,), in_specs=[blk] * 4,
                          out_specs=(blk, blk, blk), compiler_params=_cparams(("parallel",)))(w, g, m, v)


def _place():
    x, y, c = lax.axis_index("x"), lax.axis_index("y"), lax.axis_index("c")
    return x, y, c


def _all_gather_weights(shards, *, name):
    n = len(shards)

    def body(*refs):
        sh, full = refs[:n], refs[n:2 * n]
        send_sems, recv_sems, local_sems = refs[2 * n:]
        x, y, c = _place()
        me, sibling = (x, y, c), (x, y, 1 - c)
        chips = [(1 - x, y), (x, 1 - y), (1 - x, 1 - y)]

        def rows(t, px, py, pc):
            r = sh[t].shape[1]
            return full[t].at[:, pl.ds(pl.multiple_of((4 * px + 2 * py + pc) * r, BF16_ROWS), r), :]

        def copy(t, k, block, to, src=None):
            return pltpu.make_async_remote_copy(
                src_ref=rows(t, *block) if src is None else src, dst_ref=rows(t, *block),
                send_sem=send_sems.at[7 * t + k], recv_sem=recv_sems.at[7 * t + k], device_id=to, device_id_type=MESH)

        started = []
        for t in range(n):
            mine = pltpu.make_async_copy(sh[t], rows(t, *me), local_sems.at[t])
            mine.start()
            started.append(mine)
        sends = []
        for t in range(n):
            first = [copy(t, 0, me, sibling, src=sh[t])]
            first += [copy(t, 1 + j, me, (*chip, c), src=sh[t]) for j, chip in enumerate(chips)]
            for cp in first:
                cp.start()
            sends += first
        for t in range(n):
            for j, chip in enumerate(chips):
                copy(t, 1 + j, (*chip, c), me).wait_recv()
                fwd = copy(t, 4 + j, (*chip, c), sibling)
                fwd.start()
                sends.append(fwd)
        for t in range(n):
            copy(t, 0, sibling, me).wait_recv()
            for j, chip in enumerate(chips):
                copy(t, 4 + j, (*chip, 1 - c), me).wait_recv()
        for cp in sends:
            cp.wait_send()
        for cp in started:
            cp.wait()

    out_shape = [jax.ShapeDtypeStruct((s.shape[0], N_DEV * s.shape[1], s.shape[2]), s.dtype) for s in shards]
    return pl.pallas_call(
        body, name=name, out_shape=out_shape, in_specs=[ANY] * n, out_specs=[ANY] * n,
        scratch_shapes=[pltpu.SemaphoreType.DMA((7 * n,)), pltpu.SemaphoreType.DMA((7 * n,)), pltpu.SemaphoreType.DMA((n,))],
    )(*shards)


def _whole(ref_a, ref_b, send_sem, recv_sem, me):
    return pltpu.make_async_remote_copy(src_ref=ref_a, dst_ref=ref_b, send_sem=send_sem, recv_sem=recv_sem,
                                        device_id=me, device_id_type=MESH)


def _rs_sibling(grads, *, name):
    n = len(grads)

    def body(*refs):
        g, land = refs[:n], refs[n:2 * n]
        send_sems, recv_sems = refs[2 * n:]
        x, y, c = _place()
        for t in range(n):
            for k in range(4):
                pltpu.make_async_remote_copy(
                    src_ref=g[t].at[:, k, 1 - c], dst_ref=land[t].at[k], send_sem=send_sems.at[t], recv_sem=recv_sems.at[t],
                    device_id=(x, y, 1 - c), device_id_type=MESH).start()
        for t in range(n):
            w = _whole(land[t], land[t], send_sems.at[t], recv_sems.at[t], (x, y, c))
            w.wait_send()
            w.wait_recv()

    out_shape = [jax.ShapeDtypeStruct((4, s.shape[0], s.shape[3], s.shape[4]), s.dtype) for s in grads]
    return pl.pallas_call(
        body, name=name, out_shape=out_shape, in_specs=[ANY] * n, out_specs=[ANY] * n,
        scratch_shapes=[pltpu.SemaphoreType.DMA((n,)), pltpu.SemaphoreType.DMA((n,))],
    )(*grads)


def _rs_chips(sums, *, name):
    n = len(sums)

    def body(*refs):
        s, land = refs[:n], refs[n:2 * n]
        send_sems, recv_sems = refs[2 * n:]
        x, y, c = _place()
        chips = [(1 - x, y), (x, 1 - y), (1 - x, 1 - y)]
        for t in range(n):
            for j, (px, py) in enumerate(chips):
                pltpu.make_async_remote_copy(
                    src_ref=s[t].at[2 * px + py], dst_ref=land[t].at[j], send_sem=send_sems.at[t], recv_sem=recv_sems.at[t],
                    device_id=(px, py, c), device_id_type=MESH).start()
        for t in range(n):
            w = _whole(land[t], land[t], send_sems.at[t], recv_sems.at[t], (x, y, c))
            w.wait_send()
            w.wait_recv()

    out_shape = [jax.ShapeDtypeStruct((3,) + s.shape[1:], s.dtype) for s in sums]
    return pl.pallas_call(
        body, name=name, out_shape=out_shape, in_specs=[ANY] * n, out_specs=[ANY] * n,
        scratch_shapes=[pltpu.SemaphoreType.DMA((n,)), pltpu.SemaphoreType.DMA((n,))],
    )(*sums)


def _chip_sum(g, land, core, *, name):
    L, _, _, r, C = g.shape

    def body(core_ref, g_ref, l_ref, o_ref):
        o_ref[...] = (g_ref[...].astype(F32) + l_ref[...].astype(F32)).astype(BF16)

    grid_spec = pltpu.PrefetchScalarGridSpec(
        num_scalar_prefetch=1, grid=(4, L),
        in_specs=[pl.BlockSpec((None, None, None, r, C), lambda k, l, core_ref: (l, k, core_ref[0], 0, 0)),
                  pl.BlockSpec((None, None, r, C), lambda k, l, core_ref: (k, l, 0, 0))],
        out_specs=pl.BlockSpec((None, None, r, C), lambda k, l, core_ref: (k, l, 0, 0)))
    return pl.pallas_call(body, name=name, out_shape=jax.ShapeDtypeStruct((4, L, r, C), BF16), grid_spec=grid_spec,
                          compiler_params=_cparams(("parallel", "parallel")))(core, g, land)


def _final_sum(sums, land, chip, *, name):
    _, L, r, C = sums.shape

    def body(chip_ref, s_ref, a_ref, b_ref, c_ref, o_ref):
        o_ref[...] = ((s_ref[...].astype(F32) + a_ref[...].astype(F32)) + b_ref[...].astype(F32)) + c_ref[...].astype(F32)

    slot = lambda j: pl.BlockSpec((None, None, r, C), lambda l, chip_ref: (j, l, 0, 0))
    grid_spec = pltpu.PrefetchScalarGridSpec(
        num_scalar_prefetch=1, grid=(L,),
        in_specs=[pl.BlockSpec((None, None, r, C), lambda l, chip_ref: (chip_ref[0], l, 0, 0)), slot(0), slot(1), slot(2)],
        out_specs=pl.BlockSpec((None, r, C), lambda l, chip_ref: (l, 0, 0)))
    return pl.pallas_call(body, name=name, out_shape=jax.ShapeDtypeStruct((L, r, C), F32), grid_spec=grid_spec,
                          compiler_params=_cparams(("parallel",)))(chip, sums, land, land, land)


def _exchange(v, reduce, *, name):
    R, C = v.shape

    def body(v_ref, o_ref, *scratch):
        if reduce:
            buf, send_sems, recv_sems = scratch
        else:
            buf = o_ref
            send_sems, recv_sems = scratch
        x, y, c = _place()
        me = 4 * x + 2 * y + c
        buf[me] = v_ref[...]
        copies = []
        for k in range(1, N_DEV):
            kx, ky, kc = (k >> 2) & 1, (k >> 1) & 1, k & 1
            peer = (1 - x if kx else x, 1 - y if ky else y, 1 - c if kc else c)
            cp = pltpu.make_async_remote_copy(src_ref=v_ref, dst_ref=buf.at[me], send_sem=send_sems.at[k - 1],
                                              recv_sem=recv_sems.at[k - 1], device_id=peer, device_id_type=MESH)
            cp.start()
            copies.append(cp)
        for cp in copies:
            cp.wait_recv()
        for cp in copies:
            cp.wait_send()
        if reduce:
            acc = buf[0]
            for d in range(1, N_DEV):
                acc = acc + buf[d]
            o_ref[...] = acc

    sems = [pltpu.SemaphoreType.DMA((N_DEV - 1,)), pltpu.SemaphoreType.DMA((N_DEV - 1,))]
    vm = pl.BlockSpec(memory_space=pltpu.VMEM)
    if reduce:
        return pl.pallas_call(body, name=name, out_shape=jax.ShapeDtypeStruct((R, C), F32), in_specs=[vm], out_specs=vm,
                              scratch_shapes=[pltpu.VMEM((N_DEV, R, C), F32)] + sems)(v)
    return pl.pallas_call(body, name=name, out_shape=jax.ShapeDtypeStruct((N_DEV, R, C), F32), in_specs=[vm], out_specs=vm,
                          scratch_shapes=sems)(v)


def _local_step(x, mem, target, norms, conv_w, depth, n_a, get_w, put_g):
    S, D = x.shape
    main = D - MEM_WIDTH
    heads = main // HEAD_DIM
    row = lambda v: v.reshape(1, D)

    mem_n = _rmsnorm(mem, row(norms["mem_norm"]), name="mem_norm")
    saved = []
    kv = hk = x_kv = w_kv = None
    for i in range(depth):
        W = get_w(i)
        st = {"x": x, "W": W}
        h = _rmsnorm(x, row(norms["mix_norm"][i]), name=f"mix_norm{i}")
        mkv = _mm(mem_n, W["mkv"], "nn", BF16, name=f"mkv{i}")
        if i < n_a:
            p = _mm(h, W["a"], "nt", BF16, name=f"a_in{i}")
            y_main = _conv_fwd(p, conv_w[i], main, name=f"conv{i}")
            qblk = 3 * main // MEM_WIDTH
        else:
            p = _mm(h, W["b"], "nn", BF16, name=f"b_in{i}")
            y_main, st["o32"] = _sb_fwd(p, kv, heads, name=f"sb{i}")
            qblk = main // MEM_WIDTH
        y_mem = _memattn_fwd(p, qblk, mkv, name=f"memattn{i}")
        y = jnp.concatenate([y_main, y_mem], axis=1)
        xm = _mm(y, W["o"], "nn", F32, residual=x, name=f"w_o{i}")
        h2 = _rmsnorm(xm, row(norms["ffn_norm"][i]), name=f"ffn_norm{i}")
        gate, up, act = _ffn_up(h2, W["g"], W["u"], name=f"ffn_up{i}")
        x = _mm(act, W["d"], "nn", F32, residual=xm, name=f"w_down{i}")
        st.update(h=h, mkv=mkv, p=p, qblk=qblk, y=y, xm=xm, h2=h2, gate=gate, up=up, act=act)
        saved.append(st)
        if i == n_a - 1:
            x_kv, w_kv = x, W["kv"]
            hk = _rmsnorm(x, row(norms["kv_norm"]), name="kv_norm")
            kv = _mm(hk, w_kv, "nt", BF16, name="w_kv")

    dx, dxb, dg_final, loss = _loss_head(x, row(norms["final_norm"]), target, name="loss_head")

    dg_mix, dg_ffn, dconv = [None] * depth, [None] * depth, [None] * n_a
    dmem_n = dk = dv = dg_kv = g_kv = None
    for i in reversed(range(depth)):
        st = saved[i]
        W, g = st["W"], {}
        dgate, dup = _ffn_down_bwd(dxb, W["d"], st["gate"], st["up"], name=f"ffn_down_bwd{i}")
        g["d"] = _mm(st["act"], dxb, "tn", BF16, name=f"g_w_down{i}")
        g["g"] = _mm(dgate, st["h2"], "tn", BF16, name=f"g_w_gate{i}")
        g["u"] = _mm(dup, st["h2"], "tn", BF16, name=f"g_w_up{i}")
        dx, dxb, dg_ffn[i] = _mm_norm_bwd([(dgate, W["g"]), (dup, W["u"])], "nn", st["xm"], row(norms["ffn_norm"][i]), dx,
                                          name=f"d_h2_{i}")
        dy = _mm(dxb, W["o"], "nt", BF16, name=f"d_y{i}")
        g["o"] = _mm(st["y"], dxb, "tn", BF16, name=f"g_w_o{i}")
        dqmem, dmk, dmv = _memattn_bwd(st["p"], st["qblk"], st["mkv"], dy, main // MEM_WIDTH, name=f"memattn_bwd{i}")
        dmkv = jnp.concatenate([dmk, dmv], axis=1)
        g["mkv"] = _mm(mem_n, dmkv, "tn", BF16, name=f"g_w_mem_kv{i}")
        dmem_n = _mm(dmkv, W["mkv"], "nt", F32, residual=dmem_n, name=f"d_mem_n{i}")
        if i < n_a:
            db, dc, du, dconv[i] = _conv_bwd(st["p"], conv_w[i], dy, main, name=f"conv_bwd{i}")
            dp = jnp.concatenate([db, dc, du, dqmem], axis=1)
            g["a"] = _mm(dp, st["h"], "tn", BF16, name=f"g_a_in{i}")
            w_in, form = W["a"], "nn"
        else:
            dq, dk, dv = _sb_bwd(st["p"], kv, st["o32"], dy, heads, dk, dv, name=f"sb_bwd{i}")
            dp = jnp.concatenate([dq, dqmem], axis=1)
            g["b"] = _mm(st["h"], dp, "tn", BF16, name=f"g_b_in{i}")
            w_in, form = W["b"], "nt"
        if i == n_a - 1:
            g["kv"] = g_kv
        put_g(_layer_keys(i, n_a), g)
        dx, dxb, dg_mix[i] = _mm_norm_bwd([(dp, w_in)], form, st["x"], row(norms["mix_norm"][i]), dx, name=f"d_h{i}")
        if i == n_a:
            dkv = jnp.concatenate([dk, dv], axis=1).astype(BF16)
            g_kv = _mm(dkv, hk, "tn", BF16, name="g_w_kv")
            dx, dxb, dg_kv = _mm_norm_bwd([(dkv, w_kv)], "nn", x_kv, row(norms["kv_norm"]), dx, name="d_hk")
    _, _, dg_mem = _rmsnorm_bwd(mem, row(norms["mem_norm"]), dmem_n, None, name="mem_norm_bwd")

    small = {"mix_norm": jnp.concatenate(dg_mix, axis=0), "ffn_norm": jnp.concatenate(dg_ffn, axis=0), "kv_norm": dg_kv[0],
             "mem_norm": dg_mem[0], "final_norm": dg_final[0], "conv_w": jnp.stack(dconv, axis=0)}
    return loss, dx, small


_COL_SHARDED = ("a", "kv", "g", "u")
_NAMES = {"a": "a_in", "kv": "w_kv_shared", "g": "w_gate", "u": "w_up", "b": "b_in", "o": "w_o", "d": "w_down", "mkv": "w_mem_kv"}
_ORDER = ("a", "kv", "g", "u", "d", "b", "o", "mkv")
_WEIGHTS = ("mix_norm", "a_in", "conv_w", "b_in", "kv_norm", "w_kv_shared", "w_mem_kv", "w_o", "ffn_norm", "w_gate", "w_up",
            "w_down", "mem_norm", "final_norm")


def _layer_keys(i, n_a):
    keys = [("a", i) if i < n_a else ("b", i - n_a), ("g", i), ("u", i), ("d", i), ("o", i), ("mkv", i)]
    return keys + [("kv", 0)] if i == n_a - 1 else keys


def _canonical(key, w):
    w3 = w if w.ndim == 3 else w[None]
    if key in _COL_SHARDED:
        w3 = jnp.transpose(w3, (0, 2, 1))
    return w3


def _uncanonical(key, g3, like):
    if key in _COL_SHARDED:
        g3 = jnp.transpose(g3, (0, 2, 1))
    return g3.reshape(like.shape)


def _pad_rows(flat, C):
    n = flat.shape[0]
    rows = -(-n // C)
    return jnp.pad(flat, (0, rows * C - n)).reshape(rows, C)


def kernel(x, mem, mix_norm, a_in, conv_w, b_in, kv_norm, w_kv_shared, w_mem_kv, w_o, ffn_norm, w_gate, w_up, w_down, mem_norm, final_norm, loss_target, m_mix_norm, m_a_in, m_conv_w, m_b_in, m_kv_norm, m_w_kv_shared, m_w_mem_kv, m_w_o, m_ffn_norm, m_w_gate, m_w_up, m_w_down, m_mem_norm, m_final_norm, v_mix_norm, v_a_in, v_conv_w, v_b_in, v_kv_norm, v_w_kv_shared, v_w_mem_kv, v_w_o, v_ffn_norm, v_w_gate, v_w_up, v_w_down, v_mem_norm, v_final_norm):
    weights = dict(mix_norm=mix_norm, a_in=a_in, conv_w=conv_w, b_in=b_in, kv_norm=kv_norm, w_kv_shared=w_kv_shared,
                   w_mem_kv=w_mem_kv, w_o=w_o, ffn_norm=ffn_norm, w_gate=w_gate, w_up=w_up, w_down=w_down,
                   mem_norm=mem_norm, final_norm=final_norm)
    moments_m = dict(mix_norm=m_mix_norm, a_in=m_a_in, conv_w=m_conv_w, b_in=m_b_in, kv_norm=m_kv_norm,
                     w_kv_shared=m_w_kv_shared, w_mem_kv=m_w_mem_kv, w_o=m_w_o, ffn_norm=m_ffn_norm, w_gate=m_w_gate,
                     w_up=m_w_up, w_down=m_w_down, mem_norm=m_mem_norm, final_norm=m_final_norm)
    moments_v = dict(mix_norm=v_mix_norm, a_in=v_a_in, conv_w=v_conv_w, b_in=v_b_in, kv_norm=v_kv_norm,
                     w_kv_shared=v_w_kv_shared, w_mem_kv=v_w_mem_kv, w_o=v_w_o, ffn_norm=v_ffn_norm, w_gate=v_w_gate,
                     w_up=v_w_up, w_down=v_w_down, mem_norm=v_mem_norm, final_norm=v_final_norm)
    D = x.shape[-1]
    depth, n_a = w_o.shape[0], a_in.shape[0]
    xi, yi, ci = _place()
    me = 4 * xi + 2 * yi + ci
    core = ci.reshape(1).astype(jnp.int32)
    chip = (2 * xi + yi).reshape(1).astype(jnp.int32)

    cw_shape = conv_w.shape
    cw_rows = _pad_rows(conv_w.reshape(-1), D)
    cw_rows = jnp.pad(cw_rows, ((0, 8 - cw_rows.shape[0]), (0, 0)))
    cw_gathered = _exchange(cw_rows, False, name="gather_conv_w")
    n_cw = cw_shape[0] * cw_shape[1] * cw_shape[2]
    cw_all = cw_gathered.reshape(N_DEV, -1)[:, :n_cw].reshape((N_DEV,) + cw_shape)
    conv_full = jnp.transpose(cw_all, (1, 2, 0, 3)).reshape(cw_shape[0], cw_shape[1], N_DEV * cw_shape[2])

    shard3 = {k: _canonical(k, weights[_NAMES[k]]).astype(BF16) for k in _ORDER}
    all_keys = [kl for i in range(depth) for kl in _layer_keys(i, n_a)]
    fulls = _all_gather_weights([shard3[k][l][None] for k, l in all_keys], name="all_gather_weights")
    full = {kl: f[0] for kl, f in zip(all_keys, fulls)}
    bf16_grads = {}

    def get_w(i):
        return {k: full[(k, l)] for k, l in _layer_keys(i, n_a)}

    def put_g(keys, g):
        bf16_grads.update({(k, l): g[k] for k, l in keys})

    norms = {k: weights[k] for k in ("mix_norm", "ffn_norm", "kv_norm", "mem_norm", "final_norm")}
    loss, grad_x, small = _local_step(x[0], mem[0], loss_target[0], norms, conv_full, depth, n_a, get_w, put_g)

    g5 = [bf16_grads[kl].reshape(1, 4, 2, bf16_grads[kl].shape[0] // N_DEV, bf16_grads[kl].shape[1]) for kl in all_keys]
    from_sibling = _rs_sibling(g5, name="reduce_scatter_sibling")
    sums = [_chip_sum(a, s, core, name=f"chip_sum_{k}{l}") for (k, l), a, s in zip(all_keys, g5, from_sibling)]
    from_chips = _rs_chips(sums, name="reduce_scatter_chips")
    shard_grads = {(k, l): _final_sum(s, land, chip, name=f"final_sum_{k}{l}")[0]
                   for (k, l), s, land in zip(all_keys, sums, from_chips)}
    grads = {}
    for k in _ORDER:
        g3 = jnp.stack([shard_grads[(k, l)] for l in range(shard3[k].shape[0])])
        grads[_NAMES[k]] = _uncanonical(k, g3, weights[_NAMES[k]])

    order = ("mix_norm", "ffn_norm", "kv_norm", "mem_norm", "final_norm", "conv_w")
    flat = jnp.concatenate([small[k].reshape(-1) for k in order] + [loss[0, :1]])
    n_flat = flat.shape[0]
    rows = _pad_rows(flat, D)
    rows = jnp.pad(rows, ((0, (-rows.shape[0]) % 8), (0, 0)))
    total = _exchange(rows, True, name="all_reduce_small").reshape(-1)[:n_flat]
    off = 0
    for k in order:
        n = small[k].size
        grads[k] = total[off:off + n].reshape(small[k].shape)
        off += n
    loss_total = total[off]
    grads["conv_w"] = lax.dynamic_slice_in_dim(grads["conv_w"], me * cw_shape[2], cw_shape[2], axis=2)

    deltas, new_m, new_v = {}, {}, {}
    for k in _WEIGHTS:
        w = weights[k]
        two = (lambda a: a.reshape(-1, a.shape[-1])) if w.ndim > 1 else (lambda a: a.reshape(1, -1))
        d, nm, nv = _adamw(two(w), two(grads[k]), two(moments_m[k]), two(moments_v[k]), name=f"adamw_{k}")
        deltas[k], new_m[k], new_v[k] = d.reshape(w.shape), nm.reshape(w.shape), nv.reshape(w.shape)

    return (loss_total, grad_x[None], *[grads[k] for k in _WEIGHTS], *[deltas[k] for k in _WEIGHTS],
            *[new_m[k] for k in _WEIGHTS], *[new_v[k] for k in _WEIGHTS])
```

```python
import functools
import math

import jax
import jax.numpy as jnp
from jax import lax
from jax.experimental import pallas as pl
from jax.experimental.pallas import tpu as pltpu

F32 = jnp.float32
BF16 = jnp.bfloat16
MESH = pl.DeviceIdType.MESH

HEAD_DIM = 64
MEM_HEADS = 4
MEM_WIDTH = MEM_HEADS * HEAD_DIM
EPS = 1e-6
LANES = 128
BF16_ROWS = 16
VMEM_LIMIT = 56 * 1024 * 1024
N_DEV = 8

ADAM_LR = 0.001
ADAM_B1 = 0.9
ADAM_B2 = 0.999
ADAM_EPS = 1e-08
ADAM_WD = 0.01
ADAM_STEP = 10

ANY = pl.BlockSpec(memory_space=pl.ANY)


def _cparams(sem=None):
    return pltpu.CompilerParams(dimension_semantics=sem, vmem_limit_bytes=VMEM_LIMIT)


def _pick(n, cands):
    for c in cands:
        if n % c == 0:
            return c
    raise ValueError(f"no tile for {n} in {cands}")


def _mm(a, b, form, out_dtype, *, name, residual=None):
    if form == "tn":
        K, M = a.shape
    else:
        M, K = a.shape
    if form == "nt":
        N, K2 = b.shape
    else:
        K2, N = b.shape
    assert K == K2, (name, a.shape, b.shape)
    wide = (1408, 1280, 1024, 768, 512, 256, 128)
    tm = _pick(M, wide if form == "tn" else (1024, 512, 256, 128))
    tn = _pick(N, wide)
    tk = _pick(K, (1024, 1408, 1280, 768, 512, 256))
    nk = K // tk
    dims = {"nn": (((1,), (0,)), ((), ())), "nt": (((1,), (1,)), ((), ())), "tn": (((0,), (0,)), ((), ()))}[form]
    a_bytes, b_bytes = M * K * a.dtype.itemsize, N * K * b.dtype.itemsize
    n_outer = nk == 1 and (N // tn) * a_bytes + b_bytes < a_bytes + (M // tm) * b_bytes
    ij = (lambda g0, g1: (g1, g0)) if n_outer else (lambda g0, g1: (g0, g1))

    def spec(block, f):
        return pl.BlockSpec(block, lambda g0, g1, k: f(*ij(g0, g1), k))

    a_spec = spec((tk, tm), lambda i, j, k: (k, i)) if form == "tn" else spec((tm, tk), lambda i, j, k: (i, k))
    b_spec = spec((tn, tk), lambda i, j, k: (j, k)) if form == "nt" else spec((tk, tn), lambda i, j, k: (k, j))
    out_spec = spec((tm, tn), lambda i, j, k: (i, j))
    operands, in_specs = [a, b], [a_spec, b_spec]
    has_res = residual is not None
    if has_res:
        operands.append(residual)
        in_specs.append(out_spec)
    grid = (N // tn, M // tm, nk) if n_outer else (M // tm, N // tn, nk)

    def body(*refs):
        a_ref, b_ref = refs[0], refs[1]
        r_ref = refs[2] if has_res else None
        o_ref = refs[2 + int(has_res)]
        acc_ref = refs[-1]
        part = lax.dot_general(a_ref[...].astype(BF16), b_ref[...].astype(BF16), dims, preferred_element_type=F32)

        def finish(total):
            if has_res:
                total = total + r_ref[...].astype(F32)
            o_ref[...] = total.astype(out_dtype)

        if nk == 1:
            finish(part)
        else:
            k = pl.program_id(2)

            @pl.when(k == 0)
            def _():
                acc_ref[...] = part

            @pl.when(jnp.logical_and(k > 0, k < nk - 1))
            def _():
                acc_ref[...] += part

            @pl.when(k == nk - 1)
            def _():
                finish(acc_ref[...] + part)

    return pl.pallas_call(
        body, name=name, out_shape=jax.ShapeDtypeStruct((M, N), out_dtype), grid=grid, in_specs=in_specs, out_specs=out_spec,
        scratch_shapes=[pltpu.VMEM((tm, tn), F32)], compiler_params=_cparams(("parallel", "parallel", "arbitrary")),
    )(*operands)


def _rmsnorm(x, g, *, name):
    R, D = x.shape
    tr = _pick(R, (512, 256))

    def body(x_ref, g_ref, o_ref):
        xv = x_ref[...]
        r = lax.rsqrt(jnp.mean(xv * xv, axis=-1, keepdims=True) + EPS)
        o_ref[...] = (xv * r * g_ref[...]).astype(BF16)

    return pl.pallas_call(
        body, name=name, out_shape=jax.ShapeDtypeStruct((R, D), BF16), grid=(R // tr,),
        in_specs=[pl.BlockSpec((tr, D), lambda i: (i, 0)), pl.BlockSpec((1, D), lambda i: (0, 0))],
        out_specs=pl.BlockSpec((tr, D), lambda i: (i, 0)), compiler_params=_cparams(("parallel",)),
    )(x, g)


def _rmsnorm_bwd(x, g, dh, dx_in, *, name):
    R, D = x.shape
    tr = _pick(R, (512, 256))
    has_in = dx_in is not None

    def body(*refs):
        x_ref, g_ref, dh_ref = refs[:3]
        dxi_ref = refs[3] if has_in else None
        dx_ref, dxb_ref, dg_ref = refs[3 + int(has_in):]
        xv = x_ref[...]
        r = lax.rsqrt(jnp.mean(xv * xv, axis=-1, keepdims=True) + EPS)
        xhat = xv * r
        dhv = dh_ref[...].astype(F32)
        dxh = dhv * g_ref[...]
        dx = r * (dxh - xhat * jnp.mean(dxh * xhat, axis=-1, keepdims=True))
        if has_in:
            dx = dx + dxi_ref[...]
        dx_ref[...] = dx
        dxb_ref[...] = dx.astype(BF16)
        part = jnp.sum(dhv * xhat, axis=0, keepdims=True)

        @pl.when(pl.program_id(0) == 0)
        def _():
            dg_ref[...] = part

        @pl.when(pl.program_id(0) > 0)
        def _():
            dg_ref[...] += part

    row = pl.BlockSpec((tr, D), lambda i: (i, 0))
    vec = pl.BlockSpec((1, D), lambda i: (0, 0))
    ops = [x, g, dh] + ([dx_in] if has_in else [])
    return pl.pallas_call(
        body, name=name,
        out_shape=(jax.ShapeDtypeStruct((R, D), F32), jax.ShapeDtypeStruct((R, D), BF16), jax.ShapeDtypeStruct((1, D), F32)),
        grid=(R // tr,), in_specs=[row, vec, row] + ([row] if has_in else []), out_specs=(row, row, vec),
        compiler_params=_cparams(("arbitrary",)),
    )(*ops)


class _Ride:
    def __init__(self, srcs, dsts, start, wait, sems):
        self.srcs, self.dsts, self.start, self.wait, self.sems = list(srcs), list(dsts), start, wait, list(sems)


def _ride_call(body, ride, edges, *, name, out_shape, grid, in_specs, out_specs, scratch_shapes, compiler_params, operands):
    n_in, n_out, n_scr = len(in_specs), len(out_specs), len(scratch_shapes)
    if ride is None:
        outs = pl.pallas_call(body, name=name, out_shape=tuple(out_shape), grid=grid, in_specs=list(in_specs),
                              out_specs=tuple(out_specs), scratch_shapes=list(scratch_shapes),
                              compiler_params=compiler_params)(*operands)
        return tuple(outs), []
    ns, nd = len(ride.srcs), len(ride.dsts)

    def riding(*refs):
        ins, rin = refs[:n_in], refs[n_in:n_in + ns + nd]
        outs = refs[n_in + ns + nd:n_in + ns + nd + n_out]
        scratch = refs[n_in + ns + 2 * nd + n_out:]
        sems = scratch[n_scr:]
        first, last = edges()

        @pl.when(first)
        def _():
            ride.start(rin[:ns], rin[ns:], sems)

        body(*ins, *outs, *scratch[:n_scr])

        @pl.when(last)
        def _():
            ride.wait(rin[:ns], rin[ns:], sems)

    outs = pl.pallas_call(
        riding, name=name, out_shape=(*out_shape, *[jax.ShapeDtypeStruct(d.shape, d.dtype) for d in ride.dsts]), grid=grid,
        in_specs=[*in_specs, *[ANY] * (ns + nd)], out_specs=(*out_specs, *[ANY] * nd),
        scratch_shapes=[*scratch_shapes, *[pltpu.SemaphoreType.DMA(s) for s in ride.sems]],
        input_output_aliases={n_in + ns + d: n_out + d for d in range(nd)}, compiler_params=compiler_params,
    )(*operands, *ride.srcs, *ride.dsts)
    return tuple(outs[:n_out]), list(outs[n_out:])


def _mm_norm_bwd(parts, form, x, g, dx_in, *, name, ride=None):
    S, K = parts[0][0].shape
    D = x.shape[1]
    tm = _pick(S, (512, 256))
    tk = _pick(K, (1024, 1408, 1280, 768, 512, 256))
    nk, P = K // tk, len(parts)
    dims = _NN if form == "nn" else _NT

    def body(*refs):
        ab = refs[:2 * P]
        x_ref, g_ref, dxi_ref, dx_ref, dxb_ref, dg_ref, acc_ref = refs[2 * P:]
        i, k = pl.program_id(0), pl.program_id(1)

        @pl.when(k == 0)
        def _():
            acc_ref[...] = jnp.zeros_like(acc_ref)

        for p in range(P):
            @pl.when(jnp.logical_and(k >= p * nk, k < (p + 1) * nk))
            def _():
                acc_ref[...] += _dot(ab[2 * p][...], ab[2 * p + 1][...], dims)

        @pl.when(k == P * nk - 1)
        def _():
            xv = x_ref[...]
            r = lax.rsqrt(jnp.mean(xv * xv, axis=-1, keepdims=True) + EPS)
            xhat = xv * r
            dhv = acc_ref[...]
            dxh = dhv * g_ref[...]
            dx = r * (dxh - xhat * jnp.mean(dxh * xhat, axis=-1, keepdims=True)) + dxi_ref[...]
            dx_ref[...] = dx
            dxb_ref[...] = dx.astype(BF16)
            part = jnp.sum(dhv * xhat, axis=0, keepdims=True)

            @pl.when(i == 0)
            def _():
                dg_ref[...] = part

            @pl.when(i > 0)
            def _():
                dg_ref[...] += part

    def kk(p):
        return lambda k: jnp.clip(k - p * nk, 0, nk - 1)

    in_specs, operands = [], []
    for p, (a, b) in enumerate(parts):
        in_specs.append(pl.BlockSpec((tm, tk), lambda i, k, f=kk(p): (i, f(k))))
        in_specs.append(pl.BlockSpec((tk, D), lambda i, k, f=kk(p): (f(k), 0)) if form == "nn"
                        else pl.BlockSpec((D, tk), lambda i, k, f=kk(p): (0, f(k))))
        operands += [a, b]
    row = pl.BlockSpec((tm, D), lambda i, k: (i, 0))
    vec = pl.BlockSpec((1, D), lambda i, k: (0, 0))
    ni, nsteps = S // tm, P * nk

    def edges():
        i, k = pl.program_id(0), pl.program_id(1)
        return jnp.logical_and(i == 0, k == 0), jnp.logical_and(i == ni - 1, k == nsteps - 1)

    return _ride_call(
        body, ride, edges, name=name,
        out_shape=(jax.ShapeDtypeStruct((S, D), F32), jax.ShapeDtypeStruct((S, D), BF16), jax.ShapeDtypeStruct((1, D), F32)),
        grid=(ni, nsteps), in_specs=in_specs + [row, vec, row], out_specs=(row, row, vec),
        scratch_shapes=[pltpu.VMEM((tm, D), F32)], compiler_params=_cparams(("arbitrary", "arbitrary")),
        operands=[*operands, x, g, dx_in])


def _loss_head(x, g, target, *, name):
    R, D = x.shape
    tr = _pick(R, (512, 256))

    def body(x_ref, g_ref, t_ref, dx_ref, dxb_ref, dg_ref, loss_ref):
        xv = x_ref[...]
        gv = g_ref[...]
        r = lax.rsqrt(jnp.mean(xv * xv, axis=-1, keepdims=True) + EPS)
        xhat = xv * r
        err = xhat * gv - t_ref[...]
        loss = 0.5 * jnp.sum(jnp.mean(err * err, axis=-1, keepdims=True), axis=0, keepdims=True)
        dy = err * (1.0 / D)
        dxh = dy * gv
        dx = r * (dxh - xhat * jnp.mean(dxh * xhat, axis=-1, keepdims=True))
        dx_ref[...] = dx
        dxb_ref[...] = dx.astype(BF16)
        dg = jnp.sum(dy * xhat, axis=0, keepdims=True)
        lossv = jnp.broadcast_to(loss, (1, LANES))

        @pl.when(pl.program_id(0) == 0)
        def _():
            dg_ref[...] = dg
            loss_ref[...] = lossv

        @pl.when(pl.program_id(0) > 0)
        def _():
            dg_ref[...] += dg
            loss_ref[...] += lossv

    row = pl.BlockSpec((tr, D), lambda i: (i, 0))
    vec = pl.BlockSpec((1, D), lambda i: (0, 0))
    return pl.pallas_call(
        body, name=name,
        out_shape=(jax.ShapeDtypeStruct((R, D), F32), jax.ShapeDtypeStruct((R, D), BF16), jax.ShapeDtypeStruct((1, D), F32),
                   jax.ShapeDtypeStruct((1, LANES), F32)),
        grid=(R // tr,), in_specs=[row, vec, row], out_specs=(row, row, vec, pl.BlockSpec((1, LANES), lambda i: (0, 0))),
        compiler_params=_cparams(("arbitrary",)),
    )(x, g, target)


def _conv_taps(gv, S):
    t = lax.broadcasted_iota(jnp.int32, gv.shape, 0)
    g1 = jnp.where(t >= 1, pltpu.roll(gv, 1, 0), 0.0)
    g2 = jnp.where(t >= 2, pltpu.roll(gv, 2, 0), 0.0)
    return g1, g2


def _conv_fwd(p, w, main, *, name):
    S = p.shape[0]
    tc = LANES
    nb = main // tc

    def body(b_ref, c_ref, u_ref, w_ref, y_ref):
        gv = c_ref[...].astype(F32) * u_ref[...].astype(F32)
        g1, g2 = _conv_taps(gv, S)
        cv = w_ref[0:1, :] * g2 + w_ref[1:2, :] * g1 + w_ref[2:3, :] * gv
        y_ref[...] = (b_ref[...].astype(F32) * cv).astype(BF16)

    col = lambda off: pl.BlockSpec((S, tc), lambda j: (0, off + j))
    return pl.pallas_call(
        body, name=name, out_shape=jax.ShapeDtypeStruct((S, main), BF16), grid=(nb,),
        in_specs=[col(0), col(nb), col(2 * nb), pl.BlockSpec((3, tc), lambda j: (0, j))],
        out_specs=pl.BlockSpec((S, tc), lambda j: (0, j)), compiler_params=_cparams(("parallel",)),
    )(p, p, p, w)


def _conv_bwd(p, w, dy, main, *, name):
    S = p.shape[0]
    tc = LANES
    nb = main // tc

    def body(b_ref, c_ref, u_ref, w_ref, dy_ref, db_ref, dc_ref, du_ref, dw_ref):
        cvv, uv = c_ref[...].astype(F32), u_ref[...].astype(F32)
        gv = cvv * uv
        g1, g2 = _conv_taps(gv, S)
        w0, w1, w2 = w_ref[0:1, :], w_ref[1:2, :], w_ref[2:3, :]
        dyv = dy_ref[...].astype(F32)
        db_ref[...] = (dyv * (w0 * g2 + w1 * g1 + w2 * gv)).astype(BF16)
        dcv = dyv * b_ref[...].astype(F32)
        t = lax.broadcasted_iota(jnp.int32, dcv.shape, 0)
        n1 = jnp.where(t <= S - 2, pltpu.roll(dcv, S - 1, 0), 0.0)
        n2 = jnp.where(t <= S - 3, pltpu.roll(dcv, S - 2, 0), 0.0)
        dg = w2 * dcv + w1 * n1 + w0 * n2
        dc_ref[...] = (dg * uv).astype(BF16)
        du_ref[...] = (dg * cvv).astype(BF16)
        dw_ref[0:1, :] = jnp.sum(dcv * g2, axis=0, keepdims=True)
        dw_ref[1:2, :] = jnp.sum(dcv * g1, axis=0, keepdims=True)
        dw_ref[2:3, :] = jnp.sum(dcv * gv, axis=0, keepdims=True)

    col = lambda off: pl.BlockSpec((S, tc), lambda j: (0, off + j))
    out = jax.ShapeDtypeStruct((S, main), BF16)
    return pl.pallas_call(
        body, name=name, out_shape=(out, out, out, jax.ShapeDtypeStruct((3, main), F32)), grid=(nb,),
        in_specs=[col(0), col(nb), col(2 * nb), pl.BlockSpec((3, tc), lambda j: (0, j)), col(0)],
        out_specs=(col(0), col(0), col(0), pl.BlockSpec((3, tc), lambda j: (0, j))),
        compiler_params=_cparams(("parallel",)),
    )(p, p, p, w, dy)


def _head_mask(width, h):
    lane = lax.broadcasted_iota(jnp.int32, (1, width), 1)
    return jnp.logical_and(lane >= h * HEAD_DIM, lane < (h + 1) * HEAD_DIM)


_NT = (((1,), (1,)), ((), ()))
_NN = (((1,), (0,)), ((), ()))
_TN = (((0,), (0,)), ((), ()))


def _dot(a, b, dims):
    return lax.dot_general(a, b, dims, preferred_element_type=F32)


def _mem_probs(qh, kv):
    s = _dot(qh, kv, _NT) * (1.0 / math.sqrt(HEAD_DIM))
    s = s - jnp.max(s, axis=-1, keepdims=True)
    e = jnp.exp(s)
    return e / jnp.sum(e, axis=-1, keepdims=True)


def _memattn_fwd(p, qblk, mkv, *, name):
    S = p.shape[0]
    M = mkv.shape[0]
    W = MEM_WIDTH
    tq = _pick(S, (512, 256))

    def body(q_ref, k_ref, v_ref, o_ref):
        q = q_ref[...].astype(BF16)
        kv, vv = k_ref[...], v_ref[...]
        out = jnp.zeros((tq, W), F32)
        for h in range(MEM_HEADS):
            m = _head_mask(W, h)
            pr = _mem_probs(jnp.where(m, q, jnp.zeros_like(q)), kv)
            out = jnp.where(m, _dot(pr.astype(BF16), vv, _NN), out)
        o_ref[...] = out.astype(BF16)

    return pl.pallas_call(
        body, name=name, out_shape=jax.ShapeDtypeStruct((S, W), BF16), grid=(S // tq,),
        in_specs=[pl.BlockSpec((tq, W), lambda i: (i, qblk)), pl.BlockSpec((M, W), lambda i: (0, 0)),
                  pl.BlockSpec((M, W), lambda i: (0, 1))],
        out_specs=pl.BlockSpec((tq, W), lambda i: (i, 0)), compiler_params=_cparams(("parallel",)),
    )(p, mkv, mkv)


def _memattn_bwd(p, qblk, mkv, dy, dyblk, *, name):
    S = p.shape[0]
    M = mkv.shape[0]
    W = MEM_WIDTH
    tq = _pick(S, (512, 256))
    scale = 1.0 / math.sqrt(HEAD_DIM)

    def body(q_ref, k_ref, v_ref, do_ref, dq_ref, dk_ref, dv_ref, dk_acc, dv_acc):
        q = q_ref[...].astype(BF16)
        do = do_ref[...].astype(BF16)
        kv, vv = k_ref[...], v_ref[...]
        dq = jnp.zeros((tq, W), F32)
        dk = jnp.zeros((M, W), F32)
        dv = jnp.zeros((M, W), F32)
        for h in range(MEM_HEADS):
            m = _head_mask(W, h)
            qh = jnp.where(m, q, jnp.zeros_like(q))
            doh = jnp.where(m, do, jnp.zeros_like(do))
            pr = _mem_probs(qh, kv)
            dpr = _dot(doh, vv, _NT)
            ds = (pr * (dpr - jnp.sum(dpr * pr, axis=-1, keepdims=True)) * scale).astype(BF16)
            dq = jnp.where(m, _dot(ds, kv, _NN), dq)
            dk = dk + _dot(ds, qh, _TN)
            dv = dv + _dot(pr.astype(BF16), doh, _TN)
        dq_ref[...] = dq.astype(BF16)
        i = pl.program_id(0)

        @pl.when(i == 0)
        def _():
            dk_acc[...] = dk
            dv_acc[...] = dv

        @pl.when(i > 0)
        def _():
            dk_acc[...] += dk
            dv_acc[...] += dv

        @pl.when(i == S // tq - 1)
        def _():
            dk_ref[...] = dk_acc[...].astype(BF16)
            dv_ref[...] = dv_acc[...].astype(BF16)

    kspec = lambda c: pl.BlockSpec((M, W), lambda i: (0, c))
    return pl.pallas_call(
        body, name=name,
        out_shape=(jax.ShapeDtypeStruct((S, W), BF16), jax.ShapeDtypeStruct((M, W), BF16), jax.ShapeDtypeStruct((M, W), BF16)),
        grid=(S // tq,),
        in_specs=[pl.BlockSpec((tq, W), lambda i: (i, qblk)), kspec(0), kspec(1), pl.BlockSpec((tq, W), lambda i: (i, dyblk))],
        out_specs=(pl.BlockSpec((tq, W), lambda i: (i, 0)), kspec(0), kspec(0)),
        scratch_shapes=[pltpu.VMEM((M, W), F32), pltpu.VMEM((M, W), F32)],
        compiler_params=_cparams(("arbitrary",)),
    )(p, mkv, mkv, dy)


SB_TQ = 256
SB_CLAMP = 80.0
SB_DEAD = 110.0


SB_CHUNK = 64


def _by_rows(fn, *arrays):
    rows = next(a for a in arrays if a is not None).shape[0]
    outs = [fn(*[None if a is None else a[r0:r0 + SB_CHUNK] for a in arrays]) for r0 in range(0, rows, SB_CHUNK)]
    return tuple(jnp.concatenate(col, axis=0) for col in zip(*outs))


def _sb_scores(qh, kb, causal):
    def chain(z, mask):
        z = jnp.clip(z, -SB_CLAMP, SB_CLAMP)
        w = 1.0 + jnp.exp(z)
        sp = jnp.log(w)
        zs = z - sp
        if mask is not None:
            sp = jnp.where(mask, sp, 0.0)
            zs = jnp.where(mask, zs, -1e30)
            w = jnp.where(mask, w, 1.0)
        return zs, sp.astype(BF16), jnp.sum(sp, axis=1, keepdims=True), w

    return _by_rows(chain, _dot(qh, kb, _NT), causal)


def _sb_weights(zs, spb, tri, carry):
    return _by_rows(lambda zs_c, t_c, c_c: (jnp.exp(zs_c - (t_c + c_c)).astype(BF16),), zs, _dot(spb, tri, _NN), carry)[0]


def _sb_live(carry):
    return jnp.min(carry) <= SB_DEAD


def _stack_heads(v, m0):
    zero = jnp.zeros_like(v)
    return jnp.concatenate([jnp.where(m0, v, zero), jnp.where(m0, zero, v)], axis=0)


def _stacked_causal(tq):
    r = lax.broadcasted_iota(jnp.int32, (2 * tq, tq), 0)
    c = lax.broadcasted_iota(jnp.int32, (2 * tq, tq), 1)
    return c < jnp.where(r >= tq, r - tq, r)


def _sb_fwd(p, kv, heads, *, name):
    S = p.shape[0]
    tq = SB_TQ
    npair = heads // 2

    def body(q_ref, k_ref, v_ref, o_ref, o32_ref):
        qi = pl.program_id(1)
        r = lax.broadcasted_iota(jnp.int32, (tq, tq), 0)
        c = lax.broadcasted_iota(jnp.int32, (tq, tq), 1)
        tri = (r > c).astype(BF16)
        causal = _stacked_causal(tq)
        m0 = _head_mask(LANES, 0)
        qh = _stack_heads(q_ref[...] * jnp.asarray(1.0 / math.sqrt(HEAD_DIM), BF16), m0)

        def block(j, carry, acc, mask):
            off = pl.multiple_of(j * tq, tq)
            kb = k_ref[pl.ds(off, tq), :]
            vb = v_ref[pl.ds(off, tq), :]
            zs, spb, sp_sum, _ = _sb_scores(qh, kb, mask)
            acc = acc + _dot(_sb_weights(zs, spb, tri, carry), vb, _NN)
            return carry + sp_sum, acc

        st = (jnp.zeros((2 * tq, 1), F32), jnp.zeros((2 * tq, LANES), F32))
        st = lax.cond(qi >= 1, lambda s: block(qi - 1, *block(qi, *s, causal), None), lambda s: block(qi, *s, causal), st)
        left = jnp.maximum(qi - 1, 0)
        odd = left % 2
        st = lax.cond(jnp.logical_and(odd == 1, _sb_live(st[0])), lambda s: block(qi - 2, *s, None), lambda s: s, st)

        def pair(s):
            it, _, carry, acc = s
            j = qi - 2 - odd - 2 * it
            carry, acc = block(j, carry, acc, None)
            carry, acc = block(j - 1, carry, acc, None)
            return it + 1, _sb_live(carry), carry, acc

        _, _, carry, acc = lax.while_loop(lambda s: jnp.logical_and(s[0] < left // 2, s[1]), pair,
                                          (jnp.int32(0), _sb_live(st[0]), st[0], st[1]))
        out = jnp.where(m0, acc[:tq], acc[tq:])
        o_ref[...] = out.astype(BF16)
        o32_ref[...] = out

    W = heads * HEAD_DIM
    qspec = pl.BlockSpec((tq, LANES), lambda hp, i: (i, hp))
    return pl.pallas_call(
        body, name=name, out_shape=(jax.ShapeDtypeStruct((S, W), BF16), jax.ShapeDtypeStruct((S, W), F32)), grid=(npair, S // tq),
        in_specs=[qspec, pl.BlockSpec((S, LANES), lambda hp, i: (0, hp)), pl.BlockSpec((S, LANES), lambda hp, i: (0, npair + hp))],
        out_specs=(qspec, qspec), compiler_params=_cparams(("parallel", "arbitrary")),
    )(p, kv, kv)


def _sb_bwd(p, kv, o32, dy, heads, dk_in, dv_in, *, name):
    S = p.shape[0]
    tq = SB_TQ
    npair = heads // 2
    has_in = dk_in is not None
    scale = 1.0 / math.sqrt(HEAD_DIM)

    def body(*refs):
        q_ref, k_ref, v_ref, o_ref, do_ref = refs[:5]
        dq_ref, dk_ref, dv_ref = refs[5 + 2 * int(has_in):]
        qi = pl.program_id(1)

        @pl.when(qi == 0)
        def _():
            if has_in:
                dk_ref[...] = refs[5][...]
                dv_ref[...] = refs[6][...]
            else:
                dk_ref[...] = jnp.zeros_like(dk_ref)
                dv_ref[...] = jnp.zeros_like(dv_ref)

        r = lax.broadcasted_iota(jnp.int32, (tq, tq), 0)
        c = lax.broadcasted_iota(jnp.int32, (tq, tq), 1)
        tri = (r > c).astype(BF16)
        tri_low = (r < c).astype(BF16)
        causal = _stacked_causal(tq)
        m0 = _head_mask(LANES, 0)
        qh = _stack_heads(q_ref[...] * jnp.asarray(scale, BF16), m0)
        do = do_ref[...]
        doh = _stack_heads(do, m0)
        dov = do.astype(F32) * o_ref[...]
        dsum = jnp.concatenate([jnp.sum(jnp.where(m0, dov, 0.0), axis=1, keepdims=True),
                                jnp.sum(jnp.where(m0, 0.0, dov), axis=1, keepdims=True)], axis=0)

        def block(j, carry, gcarry, acc, mask):
            off = pl.multiple_of(j * tq, tq)
            kb = k_ref[pl.ds(off, tq), :]
            vb = v_ref[pl.ds(off, tq), :]
            zs, spb, sp_sum, w = _sb_scores(qh, kb, mask)
            ab = _sb_weights(zs, spb, tri, carry)

            def grads(ab_c, da_c):
                g = ab_c.astype(F32) * da_c
                return g, g.astype(BF16), jnp.sum(g, axis=1, keepdims=True)

            g, gb, g_sum = _by_rows(grads, ab, _dot(doh, vb, _NT))
            gcarry = gcarry + g_sum

            def logit_grads(g_c, w_c, low_c, left_c):
                rinv = 1.0 / w_c
                return ((g_c * rinv - (left_c + low_c) * (1.0 - rinv)).astype(BF16),)

            dzs = _by_rows(logit_grads, g, w, _dot(gb, tri_low, _NN), dsum - gcarry)[0]
            acc = acc + _dot(dzs, kb, _NN)
            dk_ref[pl.ds(off, tq), :] += _dot(dzs, qh, _TN)
            dv_ref[pl.ds(off, tq), :] += _dot(ab, doh, _TN)
            return (carry + sp_sum, gcarry, acc)

        zero = jnp.zeros((2 * tq, 1), F32)
        st = (zero, zero, jnp.zeros((2 * tq, LANES), F32))
        st = lax.cond(qi >= 1, lambda s: block(qi - 1, *block(qi, *s, causal), None), lambda s: block(qi, *s, causal), st)
        left = jnp.maximum(qi - 1, 0)
        odd = left % 2
        st = lax.cond(jnp.logical_and(odd == 1, _sb_live(st[0])), lambda s: block(qi - 2, *s, None), lambda s: s, st)

        def pair(s):
            j = qi - 2 - odd - 2 * s[0]
            b = block(j, s[2], s[3], s[4], None)
            b = block(j - 1, b[0], b[1], b[2], None)
            return (s[0] + 1, _sb_live(b[0])) + b

        st = lax.while_loop(lambda s: jnp.logical_and(s[0] < left // 2, s[1]), pair, (jnp.int32(0), _sb_live(st[0])) + st)[2:]
        dq_ref[...] = (jnp.where(m0, st[2][:tq], st[2][tq:]) * scale).astype(BF16)

    W = heads * HEAD_DIM
    qspec = pl.BlockSpec((tq, LANES), lambda hp, i: (i, hp))
    seq = lambda off: pl.BlockSpec((S, LANES), lambda hp, i: (0, off + hp))
    ops = [p, kv, kv, o32, dy] + ([dk_in, dv_in] if has_in else [])
    return pl.pallas_call(
        body, name=name,
        out_shape=(jax.ShapeDtypeStruct((S, W), BF16), jax.ShapeDtypeStruct((S, W), F32), jax.ShapeDtypeStruct((S, W), F32)),
        grid=(npair, S // tq),
        in_specs=[qspec, seq(0), seq(npair), qspec, qspec] + ([seq(0), seq(0)] if has_in else []),
        out_specs=(qspec, seq(0), seq(0)),
        compiler_params=_cparams(("parallel", "arbitrary")),
    )(*ops)


def _ffn_up(h, wg, wu, *, name):
    S, D = h.shape
    F = wg.shape[0]
    tm = _pick(S, (512, 256))
    tn = _pick(F, (1408, 1024, 512, 256, 128))

    def body(h_ref, g_ref, u_ref, gate_ref, up_ref, act_ref):
        hv = h_ref[...]
        g = _dot(hv, g_ref[...], _NT)
        u = _dot(hv, u_ref[...], _NT)
        gate_ref[...] = g.astype(BF16)
        up_ref[...] = u.astype(BF16)
        act_ref[...] = (g * jax.nn.sigmoid(g) * u).astype(BF16)

    wspec = pl.BlockSpec((tn, D), lambda j, i: (j, 0))
    ospec = pl.BlockSpec((tm, tn), lambda j, i: (i, j))
    out = jax.ShapeDtypeStruct((S, F), BF16)
    return pl.pallas_call(
        body, name=name, out_shape=(out, out, out), grid=(F // tn, S // tm),
        in_specs=[pl.BlockSpec((tm, D), lambda j, i: (i, 0)), wspec, wspec], out_specs=(ospec, ospec, ospec),
        compiler_params=_cparams(("parallel", "parallel")),
    )(h, wg, wu)


def _ffn_down_bwd(dx, wd, gate, up, *, name):
    S, D = dx.shape
    F = wd.shape[0]
    tm = _pick(S, (512, 256))
    tn = _pick(F, (1408, 1024, 512, 256, 128))

    def body(dx_ref, w_ref, g_ref, u_ref, dg_ref, du_ref):
        da = _dot(dx_ref[...], w_ref[...], _NT)
        gv, uv = g_ref[...].astype(F32), u_ref[...].astype(F32)
        s = jax.nn.sigmoid(gv)
        silu = gv * s
        dg_ref[...] = (da * uv * (s + silu * (1.0 - s))).astype(BF16)
        du_ref[...] = (da * silu).astype(BF16)

    ospec = pl.BlockSpec((tm, tn), lambda j, i: (i, j))
    out = jax.ShapeDtypeStruct((S, F), BF16)
    return pl.pallas_call(
        body, name=name, out_shape=(out, out), grid=(F // tn, S // tm),
        in_specs=[pl.BlockSpec((tm, D), lambda j, i: (i, 0)), pl.BlockSpec((tn, D), lambda j, i: (j, 0)), ospec, ospec],
        out_specs=(ospec, ospec), compiler_params=_cparams(("parallel", "parallel")),
    )(dx, wd, gate, up)


def _adamw(w, g, m, v, *, name):
    R, C = w.shape
    tr = R
    for cand in (1024, 512, 256, 128, 64, 32, 16, 8):
        if R % cand == 0 and cand * C * 4 <= (1 << 20):
            tr = cand
            break
    bc1 = 1.0 - ADAM_B1 ** ADAM_STEP
    bc2 = 1.0 - ADAM_B2 ** ADAM_STEP

    def body(w_ref, g_ref, m_ref, v_ref, d_ref, nm_ref, nv_ref):
        gv = g_ref[...]
        nm = ADAM_B1 * m_ref[...] + (1.0 - ADAM_B1) * gv
        nv = ADAM_B2 * v_ref[...] + (1.0 - ADAM_B2) * (gv * gv)
        nm_ref[...] = nm
        nv_ref[...] = nv
        d_ref[...] = -ADAM_LR * ((nm / bc1) / (jnp.sqrt(nv / bc2) + ADAM_EPS) + ADAM_WD * w_ref[...])

    blk = pl.BlockSpec((tr, C), lambda i: (i, 0))
    out = jax.ShapeDtypeStruct((R, C), F32)
    return pl.pallas_call(body, name=name, out_shape=(out, out, out), grid=(R // tr,), in_specs=[blk] * 4,
                          out_specs=(blk, blk, blk), compiler_params=_cparams(("parallel",)))(w, g, m, v)


def _place():
    x, y, c = lax.axis_index("x"), lax.axis_index("y"), lax.axis_index("c")
    return x, y, c


def _all_gather_weights(shards, *, name):
    n = len(shards)

    def body(*refs):
        sh, full = refs[:n], refs[n:2 * n]
        send_sems, recv_sems, local_sems = refs[2 * n:]
        x, y, c = _place()
        me, sibling = (x, y, c), (x, y, 1 - c)
        chips = [(1 - x, y), (x, 1 - y), (1 - x, 1 - y)]

        def rows(t, px, py, pc):
            r = sh[t].shape[1]
            return full[t].at[:, pl.ds(pl.multiple_of((4 * px + 2 * py + pc) * r, BF16_ROWS), r), :]

        def copy(t, k, block, to, src=None):
            return pltpu.make_async_remote_copy(
                src_ref=rows(t, *block) if src is None else src, dst_ref=rows(t, *block),
                send_sem=send_sems.at[7 * t + k], recv_sem=recv_sems.at[7 * t + k], device_id=to, device_id_type=MESH)

        started = []
        for t in range(n):
            mine = pltpu.make_async_copy(sh[t], rows(t, *me), local_sems.at[t])
            mine.start()
            started.append(mine)
        sends = []
        for t in range(n):
            first = [copy(t, 0, me, sibling, src=sh[t])]
            first += [copy(t, 1 + j, me, (*chip, c), src=sh[t]) for j, chip in enumerate(chips)]
            for cp in first:
                cp.start()
            sends += first
        for t in range(n):
            for j, chip in enumerate(chips):
                copy(t, 1 + j, (*chip, c), me).wait_recv()
                fwd = copy(t, 4 + j, (*chip, c), sibling)
                fwd.start()
                sends.append(fwd)
        for t in range(n):
            copy(t, 0, sibling, me).wait_recv()
            for j, chip in enumerate(chips):
                copy(t, 4 + j, (*chip, 1 - c), me).wait_recv()
        for cp in sends:
            cp.wait_send()
        for cp in started:
            cp.wait()

    out_shape = [jax.ShapeDtypeStruct((s.shape[0], N_DEV * s.shape[1], s.shape[2]), s.dtype) for s in shards]
    return pl.pallas_call(
        body, name=name, out_shape=out_shape, in_specs=[ANY] * n, out_specs=[ANY] * n,
        scratch_shapes=[pltpu.SemaphoreType.DMA((7 * n,)), pltpu.SemaphoreType.DMA((7 * n,)), pltpu.SemaphoreType.DMA((n,))],
    )(*shards)


def _whole(ref_a, ref_b, send_sem, recv_sem, me):
    return pltpu.make_async_remote_copy(src_ref=ref_a, dst_ref=ref_b, send_sem=send_sem, recv_sem=recv_sem,
                                        device_id=me, device_id_type=MESH)


def _rs_sibling(grads, *, name):
    n = len(grads)

    def body(*refs):
        g, land = refs[:n], refs[n:2 * n]
        send_sems, recv_sems = refs[2 * n:]
        x, y, c = _place()
        for t in range(n):
            for k in range(4):
                pltpu.make_async_remote_copy(
                    src_ref=g[t].at[:, k, 1 - c], dst_ref=land[t].at[k], send_sem=send_sems.at[t], recv_sem=recv_sems.at[t],
                    device_id=(x, y, 1 - c), device_id_type=MESH).start()
        for t in range(n):
            w = _whole(land[t], land[t], send_sems.at[t], recv_sems.at[t], (x, y, c))
            w.wait_send()
            w.wait_recv()

    out_shape = [jax.ShapeDtypeStruct((4, s.shape[0], s.shape[3], s.shape[4]), s.dtype) for s in grads]
    return pl.pallas_call(
        body, name=name, out_shape=out_shape, in_specs=[ANY] * n, out_specs=[ANY] * n,
        scratch_shapes=[pltpu.SemaphoreType.DMA((n,)), pltpu.SemaphoreType.DMA((n,))],
    )(*grads)


def _chips_ride(sums):
    n = len(sums)
    lands = [lax.empty((3,) + s.shape[1:], s.dtype) for s in sums]

    def start(s, land, sems):
        x, y, c = _place()
        for t in range(n):
            for j, (px, py) in enumerate([(1 - x, y), (x, 1 - y), (1 - x, 1 - y)]):
                pltpu.make_async_remote_copy(
                    src_ref=s[t].at[2 * px + py], dst_ref=land[t].at[j], send_sem=sems[0].at[t], recv_sem=sems[1].at[t],
                    device_id=(px, py, c), device_id_type=MESH).start()

    def wait(s, land, sems):
        x, y, c = _place()
        for t in range(n):
            w = _whole(land[t], land[t], sems[0].at[t], sems[1].at[t], (x, y, c))
            w.wait_send()
            w.wait_recv()

    return _Ride(sums, lands, start, wait, [(n,), (n,)])


def _chip_sum(g, land, core, *, name):
    L, _, _, r, C = g.shape

    def body(core_ref, g_ref, l_ref, o_ref):
        o_ref[...] = (g_ref[...].astype(F32) + l_ref[...].astype(F32)).astype(BF16)

    grid_spec = pltpu.PrefetchScalarGridSpec(
        num_scalar_prefetch=1, grid=(4, L),
        in_specs=[pl.BlockSpec((None, None, None, r, C), lambda k, l, core_ref: (l, k, core_ref[0], 0, 0)),
                  pl.BlockSpec((None, None, r, C), lambda k, l, core_ref: (k, l, 0, 0))],
        out_specs=pl.BlockSpec((None, None, r, C), lambda k, l, core_ref: (k, l, 0, 0)))
    return pl.pallas_call(body, name=name, out_shape=jax.ShapeDtypeStruct((4, L, r, C), BF16), grid_spec=grid_spec,
                          compiler_params=_cparams(("parallel", "parallel")))(core, g, land)


def _final_sum(sums, land, chip, *, name):
    _, L, r, C = sums.shape

    def body(chip_ref, s_ref, a_ref, b_ref, c_ref, o_ref):
        o_ref[...] = ((s_ref[...].astype(F32) + a_ref[...].astype(F32)) + b_ref[...].astype(F32)) + c_ref[...].astype(F32)

    slot = lambda j: pl.BlockSpec((None, None, r, C), lambda l, chip_ref: (j, l, 0, 0))
    grid_spec = pltpu.PrefetchScalarGridSpec(
        num_scalar_prefetch=1, grid=(L,),
        in_specs=[pl.BlockSpec((None, None, r, C), lambda l, chip_ref: (chip_ref[0], l, 0, 0)), slot(0), slot(1), slot(2)],
        out_specs=pl.BlockSpec((None, r, C), lambda l, chip_ref: (l, 0, 0)))
    return pl.pallas_call(body, name=name, out_shape=jax.ShapeDtypeStruct((L, r, C), F32), grid_spec=grid_spec,
                          compiler_params=_cparams(("parallel",)))(chip, sums, land, land, land)


def _exchange(v, reduce, *, name):
    R, C = v.shape

    def body(v_ref, o_ref, *scratch):
        if reduce:
            buf, send_sems, recv_sems = scratch
        else:
            buf = o_ref
            send_sems, recv_sems = scratch
        x, y, c = _place()
        me = 4 * x + 2 * y + c
        buf[me] = v_ref[...]
        copies = []
        for k in range(1, N_DEV):
            kx, ky, kc = (k >> 2) & 1, (k >> 1) & 1, k & 1
            peer = (1 - x if kx else x, 1 - y if ky else y, 1 - c if kc else c)
            cp = pltpu.make_async_remote_copy(src_ref=v_ref, dst_ref=buf.at[me], send_sem=send_sems.at[k - 1],
                                              recv_sem=recv_sems.at[k - 1], device_id=peer, device_id_type=MESH)
            cp.start()
            copies.append(cp)
        for cp in copies:
            cp.wait_recv()
        for cp in copies:
            cp.wait_send()
        if reduce:
            acc = buf[0]
            for d in range(1, N_DEV):
                acc = acc + buf[d]
            o_ref[...] = acc

    sems = [pltpu.SemaphoreType.DMA((N_DEV - 1,)), pltpu.SemaphoreType.DMA((N_DEV - 1,))]
    vm = pl.BlockSpec(memory_space=pltpu.VMEM)
    if reduce:
        return pl.pallas_call(body, name=name, out_shape=jax.ShapeDtypeStruct((R, C), F32), in_specs=[vm], out_specs=vm,
                              scratch_shapes=[pltpu.VMEM((N_DEV, R, C), F32)] + sems)(v)
    return pl.pallas_call(body, name=name, out_shape=jax.ShapeDtypeStruct((N_DEV, R, C), F32), in_specs=[vm], out_specs=vm,
                          scratch_shapes=sems)(v)


def _local_step(x, mem, target, norms, conv_w, depth, n_a, get_w, put_g, done_g):
    S, D = x.shape
    main = D - MEM_WIDTH
    heads = main // HEAD_DIM
    row = lambda v: v.reshape(1, D)

    mem_n = _rmsnorm(mem, row(norms["mem_norm"]), name="mem_norm")
    saved = []
    kv = hk = x_kv = w_kv = None
    for i in range(depth):
        W = get_w(i)
        st = {"x": x, "W": W}
        h = _rmsnorm(x, row(norms["mix_norm"][i]), name=f"mix_norm{i}")
        mkv = _mm(mem_n, W["mkv"], "nn", BF16, name=f"mkv{i}")
        if i < n_a:
            p = _mm(h, W["a"], "nt", BF16, name=f"a_in{i}")
            y_main = _conv_fwd(p, conv_w[i], main, name=f"conv{i}")
            qblk = 3 * main // MEM_WIDTH
        else:
            p = _mm(h, W["b"], "nn", BF16, name=f"b_in{i}")
            y_main, st["o32"] = _sb_fwd(p, kv, heads, name=f"sb{i}")
            qblk = main // MEM_WIDTH
        y_mem = _memattn_fwd(p, qblk, mkv, name=f"memattn{i}")
        y = jnp.concatenate([y_main, y_mem], axis=1)
        xm = _mm(y, W["o"], "nn", F32, residual=x, name=f"w_o{i}")
        h2 = _rmsnorm(xm, row(norms["ffn_norm"][i]), name=f"ffn_norm{i}")
        gate, up, act = _ffn_up(h2, W["g"], W["u"], name=f"ffn_up{i}")
        x = _mm(act, W["d"], "nn", F32, residual=xm, name=f"w_down{i}")
        st.update(h=h, mkv=mkv, p=p, qblk=qblk, y=y, xm=xm, h2=h2, gate=gate, up=up, act=act)
        saved.append(st)
        if i == n_a - 1:
            x_kv, w_kv = x, W["kv"]
            hk = _rmsnorm(x, row(norms["kv_norm"]), name="kv_norm")
            kv = _mm(hk, w_kv, "nt", BF16, name="w_kv")

    dx, dxb, dg_final, loss = _loss_head(x, row(norms["final_norm"]), target, name="loss_head")

    dg_mix, dg_ffn, dconv = [None] * depth, [None] * depth, [None] * n_a
    dmem_n = dk = dv = dg_kv = g_kv = None
    for i in reversed(range(depth)):
        st = saved[i]
        W, g = st["W"], {}
        dgate, dup = _ffn_down_bwd(dxb, W["d"], st["gate"], st["up"], name=f"ffn_down_bwd{i}")
        g["d"] = _mm(st["act"], dxb, "tn", BF16, name=f"g_w_down{i}")
        g["g"] = _mm(dgate, st["h2"], "tn", BF16, name=f"g_w_gate{i}")
        g["u"] = _mm(dup, st["h2"], "tn", BF16, name=f"g_w_up{i}")
        keys = dict(_layer_keys(i, n_a))
        ride = put_g([(k, keys[k]) for k in ("d", "g", "u")], g)
        (dx, dxb, dg_ffn[i]), landed = _mm_norm_bwd([(dgate, W["g"]), (dup, W["u"])], "nn", st["xm"],
                                                    row(norms["ffn_norm"][i]), dx, name=f"d_h2_{i}", ride=ride)
        done_g(landed)
        g = {}
        dy = _mm(dxb, W["o"], "nt", BF16, name=f"d_y{i}")
        g["o"] = _mm(st["y"], dxb, "tn", BF16, name=f"g_w_o{i}")
        dqmem, dmk, dmv = _memattn_bwd(st["p"], st["qblk"], st["mkv"], dy, main // MEM_WIDTH, name=f"memattn_bwd{i}")
        dmkv = jnp.concatenate([dmk, dmv], axis=1)
        g["mkv"] = _mm(mem_n, dmkv, "tn", BF16, name=f"g_w_mem_kv{i}")
        dmem_n = _mm(dmkv, W["mkv"], "nt", F32, residual=dmem_n, name=f"d_mem_n{i}")
        if i < n_a:
            db, dc, du, dconv[i] = _conv_bwd(st["p"], conv_w[i], dy, main, name=f"conv_bwd{i}")
            dp = jnp.concatenate([db, dc, du, dqmem], axis=1)
            g["a"] = _mm(dp, st["h"], "tn", BF16, name=f"g_a_in{i}")
            w_in, form = W["a"], "nn"
        else:
            dq, dk, dv = _sb_bwd(st["p"], kv, st["o32"], dy, heads, dk, dv, name=f"sb_bwd{i}")
            dp = jnp.concatenate([dq, dqmem], axis=1)
            g["b"] = _mm(st["h"], dp, "tn", BF16, name=f"g_b_in{i}")
            w_in, form = W["b"], "nt"
        if i == n_a - 1:
            g["kv"] = g_kv
        ride = put_g([kl for kl in _layer_keys(i, n_a) if kl[0] in g], g)
        (dx, dxb, dg_mix[i]), landed = _mm_norm_bwd([(dp, w_in)], form, st["x"], row(norms["mix_norm"][i]), dx,
                                                    name=f"d_h{i}", ride=ride)
        done_g(landed)
        if i == n_a:
            dkv = jnp.concatenate([dk, dv], axis=1).astype(BF16)
            g_kv = _mm(dkv, hk, "tn", BF16, name="g_w_kv")
            (dx, dxb, dg_kv), _ = _mm_norm_bwd([(dkv, w_kv)], "nn", x_kv, row(norms["kv_norm"]), dx, name="d_hk")
    _, _, dg_mem = _rmsnorm_bwd(mem, row(norms["mem_norm"]), dmem_n, None, name="mem_norm_bwd")

    small = {"mix_norm": jnp.concatenate(dg_mix, axis=0), "ffn_norm": jnp.concatenate(dg_ffn, axis=0), "kv_norm": dg_kv[0],
             "mem_norm": dg_mem[0], "final_norm": dg_final[0], "conv_w": jnp.stack(dconv, axis=0)}
    return loss, dx, small


_COL_SHARDED = ("a", "kv", "g", "u")
_NAMES = {"a": "a_in", "kv": "w_kv_shared", "g": "w_gate", "u": "w_up", "b": "b_in", "o": "w_o", "d": "w_down", "mkv": "w_mem_kv"}
_ORDER = ("a", "kv", "g", "u", "d", "b", "o", "mkv")
_WEIGHTS = ("mix_norm", "a_in", "conv_w", "b_in", "kv_norm", "w_kv_shared", "w_mem_kv", "w_o", "ffn_norm", "w_gate", "w_up",
            "w_down", "mem_norm", "final_norm")


def _layer_keys(i, n_a):
    keys = [("a", i) if i < n_a else ("b", i - n_a), ("g", i), ("u", i), ("d", i), ("o", i), ("mkv", i)]
    return keys + [("kv", 0)] if i == n_a - 1 else keys


def _canonical(key, w):
    w3 = w if w.ndim == 3 else w[None]
    if key in _COL_SHARDED:
        w3 = jnp.transpose(w3, (0, 2, 1))
    return w3


def _uncanonical(key, g3, like):
    if key in _COL_SHARDED:
        g3 = jnp.transpose(g3, (0, 2, 1))
    return g3.reshape(like.shape)


def _pad_rows(flat, C):
    n = flat.shape[0]
    rows = -(-n // C)
    return jnp.pad(flat, (0, rows * C - n)).reshape(rows, C)


def kernel(x, mem, mix_norm, a_in, conv_w, b_in, kv_norm, w_kv_shared, w_mem_kv, w_o, ffn_norm, w_gate, w_up, w_down, mem_norm, final_norm, loss_target, m_mix_norm, m_a_in, m_conv_w, m_b_in, m_kv_norm, m_w_kv_shared, m_w_mem_kv, m_w_o, m_ffn_norm, m_w_gate, m_w_up, m_w_down, m_mem_norm, m_final_norm, v_mix_norm, v_a_in, v_conv_w, v_b_in, v_kv_norm, v_w_kv_shared, v_w_mem_kv, v_w_o, v_ffn_norm, v_w_gate, v_w_up, v_w_down, v_mem_norm, v_final_norm):
    weights = dict(mix_norm=mix_norm, a_in=a_in, conv_w=conv_w, b_in=b_in, kv_norm=kv_norm, w_kv_shared=w_kv_shared,
                   w_mem_kv=w_mem_kv, w_o=w_o, ffn_norm=ffn_norm, w_gate=w_gate, w_up=w_up, w_down=w_down,
                   mem_norm=mem_norm, final_norm=final_norm)
    moments_m = dict(mix_norm=m_mix_norm, a_in=m_a_in, conv_w=m_conv_w, b_in=m_b_in, kv_norm=m_kv_norm,
                     w_kv_shared=m_w_kv_shared, w_mem_kv=m_w_mem_kv, w_o=m_w_o, ffn_norm=m_ffn_norm, w_gate=m_w_gate,
                     w_up=m_w_up, w_down=m_w_down, mem_norm=m_mem_norm, final_norm=m_final_norm)
    moments_v = dict(mix_norm=v_mix_norm, a_in=v_a_in, conv_w=v_conv_w, b_in=v_b_in, kv_norm=v_kv_norm,
                     w_kv_shared=v_w_kv_shared, w_mem_kv=v_w_mem_kv, w_o=v_w_o, ffn_norm=v_ffn_norm, w_gate=v_w_gate,
                     w_up=v_w_up, w_down=v_w_down, mem_norm=v_mem_norm, final_norm=v_final_norm)
    D = x.shape[-1]
    depth, n_a = w_o.shape[0], a_in.shape[0]
    xi, yi, ci = _place()
    me = 4 * xi + 2 * yi + ci
    core = ci.reshape(1).astype(jnp.int32)
    chip = (2 * xi + yi).reshape(1).astype(jnp.int32)

    cw_shape = conv_w.shape
    cw_rows = _pad_rows(conv_w.reshape(-1), D)
    cw_rows = jnp.pad(cw_rows, ((0, 8 - cw_rows.shape[0]), (0, 0)))
    cw_gathered = _exchange(cw_rows, False, name="gather_conv_w")
    n_cw = cw_shape[0] * cw_shape[1] * cw_shape[2]
    cw_all = cw_gathered.reshape(N_DEV, -1)[:, :n_cw].reshape((N_DEV,) + cw_shape)
    conv_full = jnp.transpose(cw_all, (1, 2, 0, 3)).reshape(cw_shape[0], cw_shape[1], N_DEV * cw_shape[2])

    shard3 = {k: _canonical(k, weights[_NAMES[k]]).astype(BF16) for k in _ORDER}
    all_keys = [kl for i in range(depth) for kl in _layer_keys(i, n_a)]
    fulls = _all_gather_weights([shard3[k][l][None] for k, l in all_keys], name="all_gather_weights")
    full = {kl: f[0] for kl, f in zip(all_keys, fulls)}

    def get_w(i):
        return {k: full[(k, l)] for k, l in _layer_keys(i, n_a)}

    batches = []

    def put_g(keys, g):
        g5 = [g[k].reshape(1, 4, 2, g[k].shape[0] // N_DEV, g[k].shape[1]) for k, _ in keys]
        from_sibling = _rs_sibling(g5, name=f"rs_sibling{len(batches)}")
        sums = [_chip_sum(a, s, core, name=f"chip_sum_{k}{l}") for (k, l), a, s in zip(keys, g5, from_sibling)]
        batches.append([keys, sums, None])
        return _chips_ride(sums)

    def done_g(landed):
        batches[-1][2] = landed

    norms = {k: weights[k] for k in ("mix_norm", "ffn_norm", "kv_norm", "mem_norm", "final_norm")}
    loss, grad_x, small = _local_step(x[0], mem[0], loss_target[0], norms, conv_full, depth, n_a, get_w, put_g, done_g)

    shard_grads = {(k, l): _final_sum(s, land, chip, name=f"final_sum_{k}{l}")[0]
                   for keys, sums, landed in batches for (k, l), s, land in zip(keys, sums, landed)}
    grads = {}
    for k in _ORDER:
        g3 = jnp.stack([shard_grads[(k, l)] for l in range(shard3[k].shape[0])])
        grads[_NAMES[k]] = _uncanonical(k, g3, weights[_NAMES[k]])

    order = ("mix_norm", "ffn_norm", "kv_norm", "mem_norm", "final_norm", "conv_w")
    flat = jnp.concatenate([small[k].reshape(-1) for k in order] + [loss[0, :1]])
    n_flat = flat.shape[0]
    rows = _pad_rows(flat, D)
    rows = jnp.pad(rows, ((0, (-rows.shape[0]) % 8), (0, 0)))
    total = _exchange(rows, True, name="all_reduce_small").reshape(-1)[:n_flat]
    off = 0
    for k in order:
        n = small[k].size
        grads[k] = total[off:off + n].reshape(small[k].shape)
        off += n
    loss_total = total[off]
    grads["conv_w"] = lax.dynamic_slice_in_dim(grads["conv_w"], me * cw_shape[2], cw_shape[2], axis=2)

    deltas, new_m, new_v = {}, {}, {}
    for k in _WEIGHTS:
        w = weights[k]
        two = (lambda a: a.reshape(-1, a.shape[-1])) if w.ndim > 1 else (lambda a: a.reshape(1, -1))
        d, nm, nv = _adamw(two(w), two(grads[k]), two(moments_m[k]), two(moments_v[k]), name=f"adamw_{k}")
        deltas[k], new_m[k], new_v[k] = d.reshape(w.shape), nm.reshape(w.shape), nv.reshape(w.shape)

    return (loss_total, grad_x[None], *[grads[k] for k in _WEIGHTS], *[deltas[k] for k in _WEIGHTS],
            *[new_m[k] for k in _WEIGHTS], *[new_v[k] for k in _WEIGHTS])
```

```python
import functools
import math

import jax
import jax.numpy as jnp
from jax import lax
from jax.experimental import pallas as pl
from jax.experimental.pallas import tpu as pltpu

F32 = jnp.float32
BF16 = jnp.bfloat16
MESH = pl.DeviceIdType.MESH

HEAD_DIM = 64
MEM_HEADS = 4
MEM_WIDTH = MEM_HEADS * HEAD_DIM
EPS = 1e-6
LANES = 128
BF16_ROWS = 16
VMEM_LIMIT = 56 * 1024 * 1024
N_DEV = 8

ADAM_LR = 0.001
ADAM_B1 = 0.9
ADAM_B2 = 0.999
ADAM_EPS = 1e-08
ADAM_WD = 0.01
ADAM_STEP = 10

ANY = pl.BlockSpec(memory_space=pl.ANY)


def _cparams(sem=None):
    return pltpu.CompilerParams(dimension_semantics=sem, vmem_limit_bytes=VMEM_LIMIT)


def _pick(n, cands):
    for c in cands:
        if n % c == 0:
            return c
    raise ValueError(f"no tile for {n} in {cands}")


def _mm(a, b, form, out_dtype, *, name, residual=None, ride=None):
    if form == "tn":
        K, M = a.shape
    else:
        M, K = a.shape
    if form == "nt":
        N, K2 = b.shape
    else:
        K2, N = b.shape
    assert K == K2, (name, a.shape, b.shape)
    wide = (1408, 1280, 1024, 768, 512, 256, 128)
    tm = _pick(M, wide if form == "tn" else (1024, 512, 256, 128))
    tn = _pick(N, wide)
    tk = _pick(K, (1024, 1408, 1280, 768, 512, 256))
    nk = K // tk
    dims = {"nn": (((1,), (0,)), ((), ())), "nt": (((1,), (1,)), ((), ())), "tn": (((0,), (0,)), ((), ()))}[form]
    a_bytes, b_bytes = M * K * a.dtype.itemsize, N * K * b.dtype.itemsize
    n_outer = nk == 1 and (N // tn) * a_bytes + b_bytes < a_bytes + (M // tm) * b_bytes
    ij = (lambda g0, g1: (g1, g0)) if n_outer else (lambda g0, g1: (g0, g1))

    def spec(block, f):
        return pl.BlockSpec(block, lambda g0, g1, k: f(*ij(g0, g1), k))

    a_spec = spec((tk, tm), lambda i, j, k: (k, i)) if form == "tn" else spec((tm, tk), lambda i, j, k: (i, k))
    b_spec = spec((tn, tk), lambda i, j, k: (j, k)) if form == "nt" else spec((tk, tn), lambda i, j, k: (k, j))
    out_spec = spec((tm, tn), lambda i, j, k: (i, j))
    operands, in_specs = [a, b], [a_spec, b_spec]
    has_res = residual is not None
    if has_res:
        operands.append(residual)
        in_specs.append(out_spec)
    grid = (N // tn, M // tm, nk) if n_outer else (M // tm, N // tn, nk)

    def body(*refs):
        a_ref, b_ref = refs[0], refs[1]
        r_ref = refs[2] if has_res else None
        o_ref = refs[2 + int(has_res)]
        acc_ref = refs[-1]
        part = lax.dot_general(a_ref[...].astype(BF16), b_ref[...].astype(BF16), dims, preferred_element_type=F32)

        def finish(total):
            if has_res:
                total = total + r_ref[...].astype(F32)
            o_ref[...] = total.astype(out_dtype)

        if nk == 1:
            finish(part)
        else:
            k = pl.program_id(2)

            @pl.when(k == 0)
            def _():
                acc_ref[...] = part

            @pl.when(jnp.logical_and(k > 0, k < nk - 1))
            def _():
                acc_ref[...] += part

            @pl.when(k == nk - 1)
            def _():
                finish(acc_ref[...] + part)

    def edges():
        ids = [pl.program_id(d) for d in range(3)]
        first = jnp.logical_and(jnp.logical_and(ids[0] == 0, ids[1] == 0), ids[2] == 0)
        last = jnp.logical_and(jnp.logical_and(ids[0] == grid[0] - 1, ids[1] == grid[1] - 1), ids[2] == grid[2] - 1)
        return first, last

    (out,), landed = _ride_call(
        body, ride, edges, name=name, out_shape=[jax.ShapeDtypeStruct((M, N), out_dtype)], grid=grid, in_specs=in_specs,
        out_specs=[out_spec], scratch_shapes=[pltpu.VMEM((tm, tn), F32)],
        compiler_params=_cparams(("parallel", "parallel", "arbitrary")), operands=operands)
    return out if ride is None else (out, landed)


def _rmsnorm(x, g, *, name):
    R, D = x.shape
    tr = _pick(R, (512, 256))

    def body(x_ref, g_ref, o_ref):
        xv = x_ref[...]
        r = lax.rsqrt(jnp.mean(xv * xv, axis=-1, keepdims=True) + EPS)
        o_ref[...] = (xv * r * g_ref[...]).astype(BF16)

    return pl.pallas_call(
        body, name=name, out_shape=jax.ShapeDtypeStruct((R, D), BF16), grid=(R // tr,),
        in_specs=[pl.BlockSpec((tr, D), lambda i: (i, 0)), pl.BlockSpec((1, D), lambda i: (0, 0))],
        out_specs=pl.BlockSpec((tr, D), lambda i: (i, 0)), compiler_params=_cparams(("parallel",)),
    )(x, g)


def _rmsnorm_bwd(x, g, dh, dx_in, *, name):
    R, D = x.shape
    tr = _pick(R, (512, 256))
    has_in = dx_in is not None

    def body(*refs):
        x_ref, g_ref, dh_ref = refs[:3]
        dxi_ref = refs[3] if has_in else None
        dx_ref, dxb_ref, dg_ref = refs[3 + int(has_in):]
        xv = x_ref[...]
        r = lax.rsqrt(jnp.mean(xv * xv, axis=-1, keepdims=True) + EPS)
        xhat = xv * r
        dhv = dh_ref[...].astype(F32)
        dxh = dhv * g_ref[...]
        dx = r * (dxh - xhat * jnp.mean(dxh * xhat, axis=-1, keepdims=True))
        if has_in:
            dx = dx + dxi_ref[...]
        dx_ref[...] = dx
        dxb_ref[...] = dx.astype(BF16)
        part = jnp.sum(dhv * xhat, axis=0, keepdims=True)

        @pl.when(pl.program_id(0) == 0)
        def _():
            dg_ref[...] = part

        @pl.when(pl.program_id(0) > 0)
        def _():
            dg_ref[...] += part

    row = pl.BlockSpec((tr, D), lambda i: (i, 0))
    vec = pl.BlockSpec((1, D), lambda i: (0, 0))
    ops = [x, g, dh] + ([dx_in] if has_in else [])
    return pl.pallas_call(
        body, name=name,
        out_shape=(jax.ShapeDtypeStruct((R, D), F32), jax.ShapeDtypeStruct((R, D), BF16), jax.ShapeDtypeStruct((1, D), F32)),
        grid=(R // tr,), in_specs=[row, vec, row] + ([row] if has_in else []), out_specs=(row, row, vec),
        compiler_params=_cparams(("arbitrary",)),
    )(*ops)


class _Ride:
    def __init__(self, srcs, dsts, start, wait, sems):
        self.srcs, self.dsts, self.start, self.wait, self.sems = list(srcs), list(dsts), start, wait, list(sems)


def _ride_call(body, ride, edges, *, name, out_shape, grid, in_specs, out_specs, scratch_shapes, compiler_params, operands):
    n_in, n_out, n_scr = len(in_specs), len(out_specs), len(scratch_shapes)
    if ride is None:
        outs = pl.pallas_call(body, name=name, out_shape=tuple(out_shape), grid=grid, in_specs=list(in_specs),
                              out_specs=tuple(out_specs), scratch_shapes=list(scratch_shapes),
                              compiler_params=compiler_params)(*operands)
        return tuple(outs), []
    ns, nd = len(ride.srcs), len(ride.dsts)

    def riding(*refs):
        ins, rin = refs[:n_in], refs[n_in:n_in + ns + nd]
        outs = refs[n_in + ns + nd:n_in + ns + nd + n_out]
        scratch = refs[n_in + ns + 2 * nd + n_out:]
        sems = scratch[n_scr:]
        first, last = edges()

        @pl.when(first)
        def _():
            ride.start(rin[:ns], rin[ns:], sems)

        body(*ins, *outs, *scratch[:n_scr])

        @pl.when(last)
        def _():
            ride.wait(rin[:ns], rin[ns:], sems)

    outs = pl.pallas_call(
        riding, name=name, out_shape=(*out_shape, *[jax.ShapeDtypeStruct(d.shape, d.dtype) for d in ride.dsts]), grid=grid,
        in_specs=[*in_specs, *[ANY] * (ns + nd)], out_specs=(*out_specs, *[ANY] * nd),
        scratch_shapes=[*scratch_shapes, *[pltpu.SemaphoreType.DMA(s) for s in ride.sems]],
        input_output_aliases={n_in + ns + d: n_out + d for d in range(nd)}, compiler_params=compiler_params,
    )(*operands, *ride.srcs, *ride.dsts)
    return tuple(outs[:n_out]), list(outs[n_out:])


def _mm_norm_bwd(parts, form, x, g, dx_in, *, name, ride=None):
    S, K = parts[0][0].shape
    D = x.shape[1]
    tm = _pick(S, (512, 256))
    tk = _pick(K, (1024, 1408, 1280, 768, 512, 256))
    nk, P = K // tk, len(parts)
    dims = _NN if form == "nn" else _NT

    def body(*refs):
        ab = refs[:2 * P]
        x_ref, g_ref, dxi_ref, dx_ref, dxb_ref, dg_ref, acc_ref = refs[2 * P:]
        i, k = pl.program_id(0), pl.program_id(1)

        @pl.when(k == 0)
        def _():
            acc_ref[...] = jnp.zeros_like(acc_ref)

        for p in range(P):
            @pl.when(jnp.logical_and(k >= p * nk, k < (p + 1) * nk))
            def _():
                acc_ref[...] += _dot(ab[2 * p][...], ab[2 * p + 1][...], dims)

        @pl.when(k == P * nk - 1)
        def _():
            xv = x_ref[...]
            r = lax.rsqrt(jnp.mean(xv * xv, axis=-1, keepdims=True) + EPS)
            xhat = xv * r
            dhv = acc_ref[...]
            dxh = dhv * g_ref[...]
            dx = r * (dxh - xhat * jnp.mean(dxh * xhat, axis=-1, keepdims=True)) + dxi_ref[...]
            dx_ref[...] = dx
            dxb_ref[...] = dx.astype(BF16)
            part = jnp.sum(dhv * xhat, axis=0, keepdims=True)

            @pl.when(i == 0)
            def _():
                dg_ref[...] = part

            @pl.when(i > 0)
            def _():
                dg_ref[...] += part

    def kk(p):
        return lambda k: jnp.clip(k - p * nk, 0, nk - 1)

    in_specs, operands = [], []
    for p, (a, b) in enumerate(parts):
        in_specs.append(pl.BlockSpec((tm, tk), lambda i, k, f=kk(p): (i, f(k))))
        in_specs.append(pl.BlockSpec((tk, D), lambda i, k, f=kk(p): (f(k), 0)) if form == "nn"
                        else pl.BlockSpec((D, tk), lambda i, k, f=kk(p): (0, f(k))))
        operands += [a, b]
    row = pl.BlockSpec((tm, D), lambda i, k: (i, 0))
    vec = pl.BlockSpec((1, D), lambda i, k: (0, 0))
    ni, nsteps = S // tm, P * nk

    def edges():
        i, k = pl.program_id(0), pl.program_id(1)
        return jnp.logical_and(i == 0, k == 0), jnp.logical_and(i == ni - 1, k == nsteps - 1)

    return _ride_call(
        body, ride, edges, name=name,
        out_shape=(jax.ShapeDtypeStruct((S, D), F32), jax.ShapeDtypeStruct((S, D), BF16), jax.ShapeDtypeStruct((1, D), F32)),
        grid=(ni, nsteps), in_specs=in_specs + [row, vec, row], out_specs=(row, row, vec),
        scratch_shapes=[pltpu.VMEM((tm, D), F32)], compiler_params=_cparams(("arbitrary", "arbitrary")),
        operands=[*operands, x, g, dx_in])


def _loss_head(x, g, target, *, name):
    R, D = x.shape
    tr = _pick(R, (512, 256))

    def body(x_ref, g_ref, t_ref, dx_ref, dxb_ref, dg_ref, loss_ref):
        xv = x_ref[...]
        gv = g_ref[...]
        r = lax.rsqrt(jnp.mean(xv * xv, axis=-1, keepdims=True) + EPS)
        xhat = xv * r
        err = xhat * gv - t_ref[...]
        loss = 0.5 * jnp.sum(jnp.mean(err * err, axis=-1, keepdims=True), axis=0, keepdims=True)
        dy = err * (1.0 / D)
        dxh = dy * gv
        dx = r * (dxh - xhat * jnp.mean(dxh * xhat, axis=-1, keepdims=True))
        dx_ref[...] = dx
        dxb_ref[...] = dx.astype(BF16)
        dg = jnp.sum(dy * xhat, axis=0, keepdims=True)
        lossv = jnp.broadcast_to(loss, (1, LANES))

        @pl.when(pl.program_id(0) == 0)
        def _():
            dg_ref[...] = dg
            loss_ref[...] = lossv

        @pl.when(pl.program_id(0) > 0)
        def _():
            dg_ref[...] += dg
            loss_ref[...] += lossv

    row = pl.BlockSpec((tr, D), lambda i: (i, 0))
    vec = pl.BlockSpec((1, D), lambda i: (0, 0))
    return pl.pallas_call(
        body, name=name,
        out_shape=(jax.ShapeDtypeStruct((R, D), F32), jax.ShapeDtypeStruct((R, D), BF16), jax.ShapeDtypeStruct((1, D), F32),
                   jax.ShapeDtypeStruct((1, LANES), F32)),
        grid=(R // tr,), in_specs=[row, vec, row], out_specs=(row, row, vec, pl.BlockSpec((1, LANES), lambda i: (0, 0))),
        compiler_params=_cparams(("arbitrary",)),
    )(x, g, target)


def _conv_taps(gv, S):
    t = lax.broadcasted_iota(jnp.int32, gv.shape, 0)
    g1 = jnp.where(t >= 1, pltpu.roll(gv, 1, 0), 0.0)
    g2 = jnp.where(t >= 2, pltpu.roll(gv, 2, 0), 0.0)
    return g1, g2


def _conv_fwd(p, w, main, *, name):
    S = p.shape[0]
    tc = LANES
    nb = main // tc

    def body(b_ref, c_ref, u_ref, w_ref, y_ref):
        gv = c_ref[...].astype(F32) * u_ref[...].astype(F32)
        g1, g2 = _conv_taps(gv, S)
        cv = w_ref[0:1, :] * g2 + w_ref[1:2, :] * g1 + w_ref[2:3, :] * gv
        y_ref[...] = (b_ref[...].astype(F32) * cv).astype(BF16)

    col = lambda off: pl.BlockSpec((S, tc), lambda j: (0, off + j))
    return pl.pallas_call(
        body, name=name, out_shape=jax.ShapeDtypeStruct((S, main), BF16), grid=(nb,),
        in_specs=[col(0), col(nb), col(2 * nb), pl.BlockSpec((3, tc), lambda j: (0, j))],
        out_specs=pl.BlockSpec((S, tc), lambda j: (0, j)), compiler_params=_cparams(("parallel",)),
    )(p, p, p, w)


def _conv_bwd(p, w, dy, main, *, name):
    S = p.shape[0]
    tc = LANES
    nb = main // tc

    def body(b_ref, c_ref, u_ref, w_ref, dy_ref, db_ref, dc_ref, du_ref, dw_ref):
        cvv, uv = c_ref[...].astype(F32), u_ref[...].astype(F32)
        gv = cvv * uv
        g1, g2 = _conv_taps(gv, S)
        w0, w1, w2 = w_ref[0:1, :], w_ref[1:2, :], w_ref[2:3, :]
        dyv = dy_ref[...].astype(F32)
        db_ref[...] = (dyv * (w0 * g2 + w1 * g1 + w2 * gv)).astype(BF16)
        dcv = dyv * b_ref[...].astype(F32)
        t = lax.broadcasted_iota(jnp.int32, dcv.shape, 0)
        n1 = jnp.where(t <= S - 2, pltpu.roll(dcv, S - 1, 0), 0.0)
        n2 = jnp.where(t <= S - 3, pltpu.roll(dcv, S - 2, 0), 0.0)
        dg = w2 * dcv + w1 * n1 + w0 * n2
        dc_ref[...] = (dg * uv).astype(BF16)
        du_ref[...] = (dg * cvv).astype(BF16)
        dw_ref[0:1, :] = jnp.sum(dcv * g2, axis=0, keepdims=True)
        dw_ref[1:2, :] = jnp.sum(dcv * g1, axis=0, keepdims=True)
        dw_ref[2:3, :] = jnp.sum(dcv * gv, axis=0, keepdims=True)

    col = lambda off: pl.BlockSpec((S, tc), lambda j: (0, off + j))
    out = jax.ShapeDtypeStruct((S, main), BF16)
    return pl.pallas_call(
        body, name=name, out_shape=(out, out, out, jax.ShapeDtypeStruct((3, main), F32)), grid=(nb,),
        in_specs=[col(0), col(nb), col(2 * nb), pl.BlockSpec((3, tc), lambda j: (0, j)), col(0)],
        out_specs=(col(0), col(0), col(0), pl.BlockSpec((3, tc), lambda j: (0, j))),
        compiler_params=_cparams(("parallel",)),
    )(p, p, p, w, dy)


def _head_mask(width, h):
    lane = lax.broadcasted_iota(jnp.int32, (1, width), 1)
    return jnp.logical_and(lane >= h * HEAD_DIM, lane < (h + 1) * HEAD_DIM)


_NT = (((1,), (1,)), ((), ()))
_NN = (((1,), (0,)), ((), ()))
_TN = (((0,), (0,)), ((), ()))


def _dot(a, b, dims):
    return lax.dot_general(a, b, dims, preferred_element_type=F32)


def _mem_probs(qh, kv):
    s = _dot(qh, kv, _NT) * (1.0 / math.sqrt(HEAD_DIM))
    s = s - jnp.max(s, axis=-1, keepdims=True)
    e = jnp.exp(s)
    return e / jnp.sum(e, axis=-1, keepdims=True)


def _memattn_fwd(p, qblk, mkv, *, name):
    S = p.shape[0]
    M = mkv.shape[0]
    W = MEM_WIDTH
    tq = _pick(S, (512, 256))

    def body(q_ref, k_ref, v_ref, o_ref):
        q = q_ref[...].astype(BF16)
        kv, vv = k_ref[...], v_ref[...]
        out = jnp.zeros((tq, W), F32)
        for h in range(MEM_HEADS):
            m = _head_mask(W, h)
            pr = _mem_probs(jnp.where(m, q, jnp.zeros_like(q)), kv)
            out = jnp.where(m, _dot(pr.astype(BF16), vv, _NN), out)
        o_ref[...] = out.astype(BF16)

    return pl.pallas_call(
        body, name=name, out_shape=jax.ShapeDtypeStruct((S, W), BF16), grid=(S // tq,),
        in_specs=[pl.BlockSpec((tq, W), lambda i: (i, qblk)), pl.BlockSpec((M, W), lambda i: (0, 0)),
                  pl.BlockSpec((M, W), lambda i: (0, 1))],
        out_specs=pl.BlockSpec((tq, W), lambda i: (i, 0)), compiler_params=_cparams(("parallel",)),
    )(p, mkv, mkv)


def _memattn_bwd(p, qblk, mkv, dy, dyblk, *, name):
    S = p.shape[0]
    M = mkv.shape[0]
    W = MEM_WIDTH
    tq = _pick(S, (512, 256))
    scale = 1.0 / math.sqrt(HEAD_DIM)

    def body(q_ref, k_ref, v_ref, do_ref, dq_ref, dk_ref, dv_ref, dk_acc, dv_acc):
        q = q_ref[...].astype(BF16)
        do = do_ref[...].astype(BF16)
        kv, vv = k_ref[...], v_ref[...]
        dq = jnp.zeros((tq, W), F32)
        dk = jnp.zeros((M, W), F32)
        dv = jnp.zeros((M, W), F32)
        for h in range(MEM_HEADS):
            m = _head_mask(W, h)
            qh = jnp.where(m, q, jnp.zeros_like(q))
            doh = jnp.where(m, do, jnp.zeros_like(do))
            pr = _mem_probs(qh, kv)
            dpr = _dot(doh, vv, _NT)
            ds = (pr * (dpr - jnp.sum(dpr * pr, axis=-1, keepdims=True)) * scale).astype(BF16)
            dq = jnp.where(m, _dot(ds, kv, _NN), dq)
            dk = dk + _dot(ds, qh, _TN)
            dv = dv + _dot(pr.astype(BF16), doh, _TN)
        dq_ref[...] = dq.astype(BF16)
        i = pl.program_id(0)

        @pl.when(i == 0)
        def _():
            dk_acc[...] = dk
            dv_acc[...] = dv

        @pl.when(i > 0)
        def _():
            dk_acc[...] += dk
            dv_acc[...] += dv

        @pl.when(i == S // tq - 1)
        def _():
            dk_ref[...] = dk_acc[...].astype(BF16)
            dv_ref[...] = dv_acc[...].astype(BF16)

    kspec = lambda c: pl.BlockSpec((M, W), lambda i: (0, c))
    return pl.pallas_call(
        body, name=name,
        out_shape=(jax.ShapeDtypeStruct((S, W), BF16), jax.ShapeDtypeStruct((M, W), BF16), jax.ShapeDtypeStruct((M, W), BF16)),
        grid=(S // tq,),
        in_specs=[pl.BlockSpec((tq, W), lambda i: (i, qblk)), kspec(0), kspec(1), pl.BlockSpec((tq, W), lambda i: (i, dyblk))],
        out_specs=(pl.BlockSpec((tq, W), lambda i: (i, 0)), kspec(0), kspec(0)),
        scratch_shapes=[pltpu.VMEM((M, W), F32), pltpu.VMEM((M, W), F32)],
        compiler_params=_cparams(("arbitrary",)),
    )(p, mkv, mkv, dy)


SB_TQ = 256
SB_CLAMP = 80.0
SB_DEAD = 110.0


SB_CHUNK = 64


def _by_rows(fn, *arrays):
    rows = next(a for a in arrays if a is not None).shape[0]
    outs = [fn(*[None if a is None else a[r0:r0 + SB_CHUNK] for a in arrays]) for r0 in range(0, rows, SB_CHUNK)]
    return tuple(jnp.concatenate(col, axis=0) for col in zip(*outs))


def _sb_scores(qh, kb, causal):
    def chain(z, mask):
        z = jnp.clip(z, -SB_CLAMP, SB_CLAMP)
        w = 1.0 + jnp.exp(z)
        sp = jnp.log(w)
        zs = z - sp
        if mask is not None:
            sp = jnp.where(mask, sp, 0.0)
            zs = jnp.where(mask, zs, -1e30)
            w = jnp.where(mask, w, 1.0)
        return zs, sp.astype(BF16), jnp.sum(sp, axis=1, keepdims=True), w

    return _by_rows(chain, _dot(qh, kb, _NT), causal)


def _sb_weights(zs, spb, tri, carry):
    return _by_rows(lambda zs_c, t_c, c_c: (jnp.exp(zs_c - (t_c + c_c)).astype(BF16),), zs, _dot(spb, tri, _NN), carry)[0]


def _sb_live(carry):
    return jnp.min(carry) <= SB_DEAD


def _stack_heads(v, m0):
    zero = jnp.zeros_like(v)
    return jnp.concatenate([jnp.where(m0, v, zero), jnp.where(m0, zero, v)], axis=0)


def _stacked_causal(tq):
    r = lax.broadcasted_iota(jnp.int32, (2 * tq, tq), 0)
    c = lax.broadcasted_iota(jnp.int32, (2 * tq, tq), 1)
    return c < jnp.where(r >= tq, r - tq, r)


def _sb_fwd(p, kv, heads, *, name):
    S = p.shape[0]
    tq = SB_TQ
    npair = heads // 2

    def body(q_ref, k_ref, v_ref, o_ref, o32_ref):
        qi = pl.program_id(1)
        r = lax.broadcasted_iota(jnp.int32, (tq, tq), 0)
        c = lax.broadcasted_iota(jnp.int32, (tq, tq), 1)
        tri = (r > c).astype(BF16)
        causal = _stacked_causal(tq)
        m0 = _head_mask(LANES, 0)
        qh = _stack_heads(q_ref[...] * jnp.asarray(1.0 / math.sqrt(HEAD_DIM), BF16), m0)

        def block(j, carry, acc, mask):
            off = pl.multiple_of(j * tq, tq)
            kb = k_ref[pl.ds(off, tq), :]
            vb = v_ref[pl.ds(off, tq), :]
            zs, spb, sp_sum, _ = _sb_scores(qh, kb, mask)
            acc = acc + _dot(_sb_weights(zs, spb, tri, carry), vb, _NN)
            return carry + sp_sum, acc

        st = (jnp.zeros((2 * tq, 1), F32), jnp.zeros((2 * tq, LANES), F32))
        st = lax.cond(qi >= 1, lambda s: block(qi - 1, *block(qi, *s, causal), None), lambda s: block(qi, *s, causal), st)
        left = jnp.maximum(qi - 1, 0)
        odd = left % 2
        st = lax.cond(jnp.logical_and(odd == 1, _sb_live(st[0])), lambda s: block(qi - 2, *s, None), lambda s: s, st)

        def pair(s):
            it, _, carry, acc = s
            j = qi - 2 - odd - 2 * it
            carry, acc = block(j, carry, acc, None)
            carry, acc = block(j - 1, carry, acc, None)
            return it + 1, _sb_live(carry), carry, acc

        _, _, carry, acc = lax.while_loop(lambda s: jnp.logical_and(s[0] < left // 2, s[1]), pair,
                                          (jnp.int32(0), _sb_live(st[0]), st[0], st[1]))
        out = jnp.where(m0, acc[:tq], acc[tq:])
        o_ref[...] = out.astype(BF16)
        o32_ref[...] = out

    W = heads * HEAD_DIM
    qspec = pl.BlockSpec((tq, LANES), lambda hp, i: (i, hp))
    return pl.pallas_call(
        body, name=name, out_shape=(jax.ShapeDtypeStruct((S, W), BF16), jax.ShapeDtypeStruct((S, W), F32)), grid=(npair, S // tq),
        in_specs=[qspec, pl.BlockSpec((S, LANES), lambda hp, i: (0, hp)), pl.BlockSpec((S, LANES), lambda hp, i: (0, npair + hp))],
        out_specs=(qspec, qspec), compiler_params=_cparams(("parallel", "arbitrary")),
    )(p, kv, kv)


def _sb_bwd(p, kv, o32, dy, heads, dk_in, dv_in, *, name):
    S = p.shape[0]
    tq = SB_TQ
    npair = heads // 2
    has_in = dk_in is not None
    scale = 1.0 / math.sqrt(HEAD_DIM)

    def body(*refs):
        q_ref, k_ref, v_ref, o_ref, do_ref = refs[:5]
        dq_ref, dk_ref, dv_ref = refs[5 + 2 * int(has_in):]
        qi = pl.program_id(1)

        @pl.when(qi == 0)
        def _():
            if has_in:
                dk_ref[...] = refs[5][...]
                dv_ref[...] = refs[6][...]
            else:
                dk_ref[...] = jnp.zeros_like(dk_ref)
                dv_ref[...] = jnp.zeros_like(dv_ref)

        r = lax.broadcasted_iota(jnp.int32, (tq, tq), 0)
        c = lax.broadcasted_iota(jnp.int32, (tq, tq), 1)
        tri = (r > c).astype(BF16)
        tri_low = (r < c).astype(BF16)
        causal = _stacked_causal(tq)
        m0 = _head_mask(LANES, 0)
        qh = _stack_heads(q_ref[...] * jnp.asarray(scale, BF16), m0)
        do = do_ref[...]
        doh = _stack_heads(do, m0)
        dov = do.astype(F32) * o_ref[...]
        dsum = jnp.concatenate([jnp.sum(jnp.where(m0, dov, 0.0), axis=1, keepdims=True),
                                jnp.sum(jnp.where(m0, 0.0, dov), axis=1, keepdims=True)], axis=0)

        def block(j, carry, gcarry, acc, mask):
            off = pl.multiple_of(j * tq, tq)
            kb = k_ref[pl.ds(off, tq), :]
            vb = v_ref[pl.ds(off, tq), :]
            zs, spb, sp_sum, w = _sb_scores(qh, kb, mask)
            ab = _sb_weights(zs, spb, tri, carry)

            def grads(ab_c, da_c):
                g = ab_c.astype(F32) * da_c
                return g, g.astype(BF16), jnp.sum(g, axis=1, keepdims=True)

            g, gb, g_sum = _by_rows(grads, ab, _dot(doh, vb, _NT))
            gcarry = gcarry + g_sum

            def logit_grads(g_c, w_c, low_c, left_c):
                rinv = 1.0 / w_c
                return ((g_c * rinv - (left_c + low_c) * (1.0 - rinv)).astype(BF16),)

            dzs = _by_rows(logit_grads, g, w, _dot(gb, tri_low, _NN), dsum - gcarry)[0]
            acc = acc + _dot(dzs, kb, _NN)
            dk_ref[pl.ds(off, tq), :] += _dot(dzs, qh, _TN)
            dv_ref[pl.ds(off, tq), :] += _dot(ab, doh, _TN)
            return (carry + sp_sum, gcarry, acc)

        zero = jnp.zeros((2 * tq, 1), F32)
        st = (zero, zero, jnp.zeros((2 * tq, LANES), F32))
        st = lax.cond(qi >= 1, lambda s: block(qi - 1, *block(qi, *s, causal), None), lambda s: block(qi, *s, causal), st)
        left = jnp.maximum(qi - 1, 0)
        odd = left % 2
        st = lax.cond(jnp.logical_and(odd == 1, _sb_live(st[0])), lambda s: block(qi - 2, *s, None), lambda s: s, st)

        def pair(s):
            j = qi - 2 - odd - 2 * s[0]
            b = block(j, s[2], s[3], s[4], None)
            b = block(j - 1, b[0], b[1], b[2], None)
            return (s[0] + 1, _sb_live(b[0])) + b

        st = lax.while_loop(lambda s: jnp.logical_and(s[0] < left // 2, s[1]), pair, (jnp.int32(0), _sb_live(st[0])) + st)[2:]
        dq_ref[...] = (jnp.where(m0, st[2][:tq], st[2][tq:]) * scale).astype(BF16)

    W = heads * HEAD_DIM
    qspec = pl.BlockSpec((tq, LANES), lambda hp, i: (i, hp))
    seq = lambda off: pl.BlockSpec((S, LANES), lambda hp, i: (0, off + hp))
    ops = [p, kv, kv, o32, dy] + ([dk_in, dv_in] if has_in else [])
    return pl.pallas_call(
        body, name=name,
        out_shape=(jax.ShapeDtypeStruct((S, W), BF16), jax.ShapeDtypeStruct((S, W), F32), jax.ShapeDtypeStruct((S, W), F32)),
        grid=(npair, S // tq),
        in_specs=[qspec, seq(0), seq(npair), qspec, qspec] + ([seq(0), seq(0)] if has_in else []),
        out_specs=(qspec, seq(0), seq(0)),
        compiler_params=_cparams(("parallel", "arbitrary")),
    )(*ops)


def _ffn_up(h, wg, wu, *, name, ride=None):
    S, D = h.shape
    F = wg.shape[0]
    tm = _pick(S, (512, 256))
    tn = _pick(F, (1408, 1024, 512, 256, 128))

    def body(h_ref, g_ref, u_ref, gate_ref, up_ref, act_ref):
        hv = h_ref[...]
        g = _dot(hv, g_ref[...], _NT)
        u = _dot(hv, u_ref[...], _NT)
        gate_ref[...] = g.astype(BF16)
        up_ref[...] = u.astype(BF16)
        act_ref[...] = (g * jax.nn.sigmoid(g) * u).astype(BF16)

    wspec = pl.BlockSpec((tn, D), lambda j, i: (j, 0))
    ospec = pl.BlockSpec((tm, tn), lambda j, i: (i, j))
    out = jax.ShapeDtypeStruct((S, F), BF16)
    grid = (F // tn, S // tm)

    def edges():
        j, i = pl.program_id(0), pl.program_id(1)
        return jnp.logical_and(j == 0, i == 0), jnp.logical_and(j == grid[0] - 1, i == grid[1] - 1)

    return _ride_call(
        body, ride, edges, name=name, out_shape=(out, out, out), grid=grid,
        in_specs=[pl.BlockSpec((tm, D), lambda j, i: (i, 0)), wspec, wspec], out_specs=(ospec, ospec, ospec),
        scratch_shapes=[], compiler_params=_cparams(("parallel", "parallel")), operands=[h, wg, wu])


def _ffn_down_bwd(dx, wd, gate, up, *, name):
    S, D = dx.shape
    F = wd.shape[0]
    tm = _pick(S, (512, 256))
    tn = _pick(F, (1408, 1024, 512, 256, 128))

    def body(dx_ref, w_ref, g_ref, u_ref, dg_ref, du_ref):
        da = _dot(dx_ref[...], w_ref[...], _NT)
        gv, uv = g_ref[...].astype(F32), u_ref[...].astype(F32)
        s = jax.nn.sigmoid(gv)
        silu = gv * s
        dg_ref[...] = (da * uv * (s + silu * (1.0 - s))).astype(BF16)
        du_ref[...] = (da * silu).astype(BF16)

    ospec = pl.BlockSpec((tm, tn), lambda j, i: (i, j))
    out = jax.ShapeDtypeStruct((S, F), BF16)
    return pl.pallas_call(
        body, name=name, out_shape=(out, out), grid=(F // tn, S // tm),
        in_specs=[pl.BlockSpec((tm, D), lambda j, i: (i, 0)), pl.BlockSpec((tn, D), lambda j, i: (j, 0)), ospec, ospec],
        out_specs=(ospec, ospec), compiler_params=_cparams(("parallel", "parallel")),
    )(dx, wd, gate, up)


def _adamw(w, g, m, v, *, name):
    R, C = w.shape
    tr = R
    for cand in (1024, 512, 256, 128, 64, 32, 16, 8):
        if R % cand == 0 and cand * C * 4 <= (1 << 20):
            tr = cand
            break
    bc1 = 1.0 - ADAM_B1 ** ADAM_STEP
    bc2 = 1.0 - ADAM_B2 ** ADAM_STEP

    def body(w_ref, g_ref, m_ref, v_ref, d_ref, nm_ref, nv_ref):
        gv = g_ref[...]
        nm = ADAM_B1 * m_ref[...] + (1.0 - ADAM_B1) * gv
        nv = ADAM_B2 * v_ref[...] + (1.0 - ADAM_B2) * (gv * gv)
        nm_ref[...] = nm
        nv_ref[...] = nv
        d_ref[...] = -ADAM_LR * ((nm / bc1) / (jnp.sqrt(nv / bc2) + ADAM_EPS) + ADAM_WD * w_ref[...])

    blk = pl.BlockSpec((tr, C), lambda i: (i, 0))
    out = jax.ShapeDtypeStruct((R, C), F32)
    return pl.pallas_call(body, name=name, out_shape=(out, out, out), grid=(R // tr,), in_specs=[blk] * 4,
                          out_specs=(blk, blk, blk), compiler_params=_cparams(("parallel",)))(w, g, m, v)


def _place():
    x, y, c = lax.axis_index("x"), lax.axis_index("y"), lax.axis_index("c")
    return x, y, c


def _all_gather_weights(shards, *, name):
    n = len(shards)

    def body(*refs):
        sh, full = refs[:n], refs[n:2 * n]
        send_sems, recv_sems, local_sems = refs[2 * n:]
        x, y, c = _place()
        me, sibling = (x, y, c), (x, y, 1 - c)
        chips = [(1 - x, y), (x, 1 - y), (1 - x, 1 - y)]

        def rows(t, px, py, pc):
            r = sh[t].shape[1]
            return full[t].at[:, pl.ds(pl.multiple_of((4 * px + 2 * py + pc) * r, BF16_ROWS), r), :]

        def copy(t, k, block, to, src=None):
            return pltpu.make_async_remote_copy(
                src_ref=rows(t, *block) if src is None else src, dst_ref=rows(t, *block),
                send_sem=send_sems.at[7 * t + k], recv_sem=recv_sems.at[7 * t + k], device_id=to, device_id_type=MESH)

        started = []
        for t in range(n):
            mine = pltpu.make_async_copy(sh[t], rows(t, *me), local_sems.at[t])
            mine.start()
            started.append(mine)
        sends = []
        for t in range(n):
            first = [copy(t, 0, me, sibling, src=sh[t])]
            first += [copy(t, 1 + j, me, (*chip, c), src=sh[t]) for j, chip in enumerate(chips)]
            for cp in first:
                cp.start()
            sends += first
        for t in range(n):
            for j, chip in enumerate(chips):
                copy(t, 1 + j, (*chip, c), me).wait_recv()
                fwd = copy(t, 4 + j, (*chip, c), sibling)
                fwd.start()
                sends.append(fwd)
        for t in range(n):
            copy(t, 0, sibling, me).wait_recv()
            for j, chip in enumerate(chips):
                copy(t, 4 + j, (*chip, 1 - c), me).wait_recv()
        for cp in sends:
            cp.wait_send()
        for cp in started:
            cp.wait()

    out_shape = [jax.ShapeDtypeStruct((s.shape[0], N_DEV * s.shape[1], s.shape[2]), s.dtype) for s in shards]
    return pl.pallas_call(
        body, name=name, out_shape=out_shape, in_specs=[ANY] * n, out_specs=[ANY] * n,
        scratch_shapes=[pltpu.SemaphoreType.DMA((7 * n,)), pltpu.SemaphoreType.DMA((7 * n,)), pltpu.SemaphoreType.DMA((n,))],
    )(*shards)


def _whole(ref_a, ref_b, send_sem, recv_sem, me):
    return pltpu.make_async_remote_copy(src_ref=ref_a, dst_ref=ref_b, send_sem=send_sem, recv_sem=recv_sem,
                                        device_id=me, device_id_type=MESH)


def _rs_sibling(grads, *, name):
    n = len(grads)

    def body(*refs):
        g, land = refs[:n], refs[n:2 * n]
        send_sems, recv_sems = refs[2 * n:]
        x, y, c = _place()
        for t in range(n):
            for k in range(4):
                pltpu.make_async_remote_copy(
                    src_ref=g[t].at[:, k, 1 - c], dst_ref=land[t].at[k], send_sem=send_sems.at[t], recv_sem=recv_sems.at[t],
                    device_id=(x, y, 1 - c), device_id_type=MESH).start()
        for t in range(n):
            w = _whole(land[t], land[t], send_sems.at[t], recv_sems.at[t], (x, y, c))
            w.wait_send()
            w.wait_recv()

    out_shape = [jax.ShapeDtypeStruct((4, s.shape[0], s.shape[3], s.shape[4]), s.dtype) for s in grads]
    return pl.pallas_call(
        body, name=name, out_shape=out_shape, in_specs=[ANY] * n, out_specs=[ANY] * n,
        scratch_shapes=[pltpu.SemaphoreType.DMA((n,)), pltpu.SemaphoreType.DMA((n,))],
    )(*grads)


def _chips_ride(sums):
    n = len(sums)
    lands = [lax.empty((3,) + s.shape[1:], s.dtype) for s in sums]

    def start(s, land, sems):
        x, y, c = _place()
        for t in range(n):
            for j, (px, py) in enumerate([(1 - x, y), (x, 1 - y), (1 - x, 1 - y)]):
                pltpu.make_async_remote_copy(
                    src_ref=s[t].at[2 * px + py], dst_ref=land[t].at[j], send_sem=sems[0].at[t], recv_sem=sems[1].at[t],
                    device_id=(px, py, c), device_id_type=MESH).start()

    def wait(s, land, sems):
        x, y, c = _place()
        for t in range(n):
            w = _whole(land[t], land[t], sems[0].at[t], sems[1].at[t], (x, y, c))
            w.wait_send()
            w.wait_recv()

    return _Ride(sums, lands, start, wait, [(n,), (n,)])


def _gather_ride_1(shards):
    n = len(shards)
    fulls = [lax.empty((N_DEV * s.shape[0], s.shape[1]), s.dtype) for s in shards]

    def rows(full, r, px, py, pc, count=1):
        return full.at[pl.ds(pl.multiple_of((4 * px + 2 * py + pc) * r, BF16_ROWS), count * r), :]

    def start(sh, full, sems):
        x, y, c = _place()
        for t in range(n):
            r = sh[t].shape[0]
            mine = rows(full[t], r, x, y, c)
            pltpu.make_async_copy(sh[t], mine, sems[2].at[t]).start()
            for peer in [(x, y, 1 - c), (1 - x, y, c), (x, 1 - y, c), (1 - x, 1 - y, c)]:
                pltpu.make_async_remote_copy(src_ref=sh[t], dst_ref=mine, send_sem=sems[0].at[t], recv_sem=sems[1].at[t],
                                             device_id=peer, device_id_type=MESH).start()

    def wait(sh, full, sems):
        x, y, c = _place()
        for t in range(n):
            r = sh[t].shape[0]
            pltpu.make_async_copy(sh[t], rows(full[t], r, x, y, c), sems[2].at[t]).wait()
            four = full[t].at[pl.ds(0, 4 * r), :]
            w = _whole(four, four, sems[0].at[t], sems[1].at[t], (x, y, c))
            w.wait_send()
            w.wait_recv()

    return _Ride(shards, fulls, start, wait, [(n,), (n,), (n,)])


def _gather_ride_2(fulls):
    n = len(fulls)

    def start(_, full, sems):
        x, y, c = _place()
        for t in range(n):
            r = full[t].shape[0] // N_DEV
            for px, py in [(1 - x, y), (x, 1 - y), (1 - x, 1 - y)]:
                block = full[t].at[pl.ds(pl.multiple_of((4 * px + 2 * py + c) * r, BF16_ROWS), r), :]
                pltpu.make_async_remote_copy(src_ref=block, dst_ref=block, send_sem=sems[0].at[t], recv_sem=sems[1].at[t],
                                             device_id=(x, y, 1 - c), device_id_type=MESH).start()

    def wait(_, full, sems):
        x, y, c = _place()
        for t in range(n):
            three = full[t].at[pl.ds(0, 3 * (full[t].shape[0] // N_DEV)), :]
            w = _whole(three, three, sems[0].at[t], sems[1].at[t], (x, y, c))
            w.wait_send()
            w.wait_recv()

    return _Ride([], fulls, start, wait, [(n,), (n,)])


def _join_rides(rides):
    rides = [r for r in rides if r is not None]
    if len(rides) <= 1:
        return rides[0] if rides else None

    def parts(src, dst, sems):
        so = do = mo = 0
        for r in rides:
            yield r, src[so:so + len(r.srcs)], dst[do:do + len(r.dsts)], sems[mo:mo + len(r.sems)]
            so, do, mo = so + len(r.srcs), do + len(r.dsts), mo + len(r.sems)

    def start(src, dst, sems):
        for r, s, d, m in parts(src, dst, sems):
            r.start(s, d, m)

    def wait(src, dst, sems):
        for r, s, d, m in parts(src, dst, sems):
            r.wait(s, d, m)

    return _Ride([a for r in rides for a in r.srcs], [a for r in rides for a in r.dsts], start, wait,
                 [m for r in rides for m in r.sems])


def _chip_sum(g, land, core, *, name):
    L, _, _, r, C = g.shape

    def body(core_ref, g_ref, l_ref, o_ref):
        o_ref[...] = (g_ref[...].astype(F32) + l_ref[...].astype(F32)).astype(BF16)

    grid_spec = pltpu.PrefetchScalarGridSpec(
        num_scalar_prefetch=1, grid=(4, L),
        in_specs=[pl.BlockSpec((None, None, None, r, C), lambda k, l, core_ref: (l, k, core_ref[0], 0, 0)),
                  pl.BlockSpec((None, None, r, C), lambda k, l, core_ref: (k, l, 0, 0))],
        out_specs=pl.BlockSpec((None, None, r, C), lambda k, l, core_ref: (k, l, 0, 0)))
    return pl.pallas_call(body, name=name, out_shape=jax.ShapeDtypeStruct((4, L, r, C), BF16), grid_spec=grid_spec,
                          compiler_params=_cparams(("parallel", "parallel")))(core, g, land)


def _final_sum(sums, land, chip, *, name):
    _, L, r, C = sums.shape

    def body(chip_ref, s_ref, a_ref, b_ref, c_ref, o_ref):
        o_ref[...] = ((s_ref[...].astype(F32) + a_ref[...].astype(F32)) + b_ref[...].astype(F32)) + c_ref[...].astype(F32)

    slot = lambda j: pl.BlockSpec((None, None, r, C), lambda l, chip_ref: (j, l, 0, 0))
    grid_spec = pltpu.PrefetchScalarGridSpec(
        num_scalar_prefetch=1, grid=(L,),
        in_specs=[pl.BlockSpec((None, None, r, C), lambda l, chip_ref: (chip_ref[0], l, 0, 0)), slot(0), slot(1), slot(2)],
        out_specs=pl.BlockSpec((None, r, C), lambda l, chip_ref: (l, 0, 0)))
    return pl.pallas_call(body, name=name, out_shape=jax.ShapeDtypeStruct((L, r, C), F32), grid_spec=grid_spec,
                          compiler_params=_cparams(("parallel",)))(chip, sums, land, land, land)


def _exchange(v, reduce, *, name):
    R, C = v.shape

    def body(v_ref, o_ref, *scratch):
        if reduce:
            buf, send_sems, recv_sems = scratch
        else:
            buf = o_ref
            send_sems, recv_sems = scratch
        x, y, c = _place()
        me = 4 * x + 2 * y + c
        buf[me] = v_ref[...]
        copies = []
        for k in range(1, N_DEV):
            kx, ky, kc = (k >> 2) & 1, (k >> 1) & 1, k & 1
            peer = (1 - x if kx else x, 1 - y if ky else y, 1 - c if kc else c)
            cp = pltpu.make_async_remote_copy(src_ref=v_ref, dst_ref=buf.at[me], send_sem=send_sems.at[k - 1],
                                              recv_sem=recv_sems.at[k - 1], device_id=peer, device_id_type=MESH)
            cp.start()
            copies.append(cp)
        for cp in copies:
            cp.wait_recv()
        for cp in copies:
            cp.wait_send()
        if reduce:
            acc = buf[0]
            for d in range(1, N_DEV):
                acc = acc + buf[d]
            o_ref[...] = acc

    sems = [pltpu.SemaphoreType.DMA((N_DEV - 1,)), pltpu.SemaphoreType.DMA((N_DEV - 1,))]
    vm = pl.BlockSpec(memory_space=pltpu.VMEM)
    if reduce:
        return pl.pallas_call(body, name=name, out_shape=jax.ShapeDtypeStruct((R, C), F32), in_specs=[vm], out_specs=vm,
                              scratch_shapes=[pltpu.VMEM((N_DEV, R, C), F32)] + sems)(v)
    return pl.pallas_call(body, name=name, out_shape=jax.ShapeDtypeStruct((N_DEV, R, C), F32), in_specs=[vm], out_specs=vm,
                          scratch_shapes=sems)(v)


def _local_step(x, mem, target, norms, conv_w, depth, n_a, get_w, next_ride, ride_done, put_g, done_g):
    S, D = x.shape
    main = D - MEM_WIDTH
    heads = main // HEAD_DIM
    row = lambda v: v.reshape(1, D)

    mem_n = _rmsnorm(mem, row(norms["mem_norm"]), name="mem_norm")
    saved = []
    kv = hk = x_kv = w_kv = None
    def carry_mm(*args, **kwargs):
        ride = next_ride()
        if ride is None:
            return _mm(*args, **kwargs)
        out, landed = _mm(*args, ride=ride, **kwargs)
        ride_done(landed)
        return out

    for i in range(depth):
        W = functools.partial(get_w, i)
        st = {"x": x}
        h = _rmsnorm(x, row(norms["mix_norm"][i]), name=f"mix_norm{i}")
        mkv = _mm(mem_n, W("mkv"), "nn", BF16, name=f"mkv{i}")
        if i < n_a:
            p = carry_mm(h, W("a"), "nt", BF16, name=f"a_in{i}")
            y_main = _conv_fwd(p, conv_w[i], main, name=f"conv{i}")
            qblk = 3 * main // MEM_WIDTH
        else:
            p = carry_mm(h, W("b"), "nn", BF16, name=f"b_in{i}")
            y_main, st["o32"] = _sb_fwd(p, kv, heads, name=f"sb{i}")
            qblk = main // MEM_WIDTH
        y_mem = _memattn_fwd(p, qblk, mkv, name=f"memattn{i}")
        y = jnp.concatenate([y_main, y_mem], axis=1)
        xm = carry_mm(y, W("o"), "nn", F32, residual=x, name=f"w_o{i}")
        h2 = _rmsnorm(xm, row(norms["ffn_norm"][i]), name=f"ffn_norm{i}")
        (gate, up, act), landed = _ffn_up(h2, W("g"), W("u"), name=f"ffn_up{i}", ride=next_ride())
        ride_done(landed)
        x = carry_mm(act, W("d"), "nn", F32, residual=xm, name=f"w_down{i}")
        st.update(h=h, mkv=mkv, p=p, qblk=qblk, y=y, xm=xm, h2=h2, gate=gate, up=up, act=act)
        saved.append(st)
        if i == n_a - 1:
            x_kv, w_kv = x, W("kv")
            hk = _rmsnorm(x, row(norms["kv_norm"]), name="kv_norm")
            kv = _mm(hk, w_kv, "nt", BF16, name="w_kv")

    dx, dxb, dg_final, loss = _loss_head(x, row(norms["final_norm"]), target, name="loss_head")

    dg_mix, dg_ffn, dconv = [None] * depth, [None] * depth, [None] * n_a
    dmem_n = dk = dv = dg_kv = g_kv = None
    for i in reversed(range(depth)):
        st = saved[i]
        W, g = functools.partial(get_w, i), {}
        dgate, dup = _ffn_down_bwd(dxb, W("d"), st["gate"], st["up"], name=f"ffn_down_bwd{i}")
        g["d"] = _mm(st["act"], dxb, "tn", BF16, name=f"g_w_down{i}")
        g["g"] = _mm(dgate, st["h2"], "tn", BF16, name=f"g_w_gate{i}")
        g["u"] = _mm(dup, st["h2"], "tn", BF16, name=f"g_w_up{i}")
        keys = dict(_layer_keys(i, n_a))
        ride = put_g([(k, keys[k]) for k in ("d", "g", "u")], g)
        (dx, dxb, dg_ffn[i]), landed = _mm_norm_bwd([(dgate, W("g")), (dup, W("u"))], "nn", st["xm"],
                                                    row(norms["ffn_norm"][i]), dx, name=f"d_h2_{i}", ride=ride)
        done_g(landed)
        g = {}
        dy = _mm(dxb, W("o"), "nt", BF16, name=f"d_y{i}")
        g["o"] = _mm(st["y"], dxb, "tn", BF16, name=f"g_w_o{i}")
        dqmem, dmk, dmv = _memattn_bwd(st["p"], st["qblk"], st["mkv"], dy, main // MEM_WIDTH, name=f"memattn_bwd{i}")
        dmkv = jnp.concatenate([dmk, dmv], axis=1)
        g["mkv"] = _mm(mem_n, dmkv, "tn", BF16, name=f"g_w_mem_kv{i}")
        dmem_n = _mm(dmkv, W("mkv"), "nt", F32, residual=dmem_n, name=f"d_mem_n{i}")
        if i < n_a:
            db, dc, du, dconv[i] = _conv_bwd(st["p"], conv_w[i], dy, main, name=f"conv_bwd{i}")
            dp = jnp.concatenate([db, dc, du, dqmem], axis=1)
            g["a"] = _mm(dp, st["h"], "tn", BF16, name=f"g_a_in{i}")
            w_in, form = W("a"), "nn"
        else:
            dq, dk, dv = _sb_bwd(st["p"], kv, st["o32"], dy, heads, dk, dv, name=f"sb_bwd{i}")
            dp = jnp.concatenate([dq, dqmem], axis=1)
            g["b"] = _mm(st["h"], dp, "tn", BF16, name=f"g_b_in{i}")
            w_in, form = W("b"), "nt"
        if i == n_a - 1:
            g["kv"] = g_kv
        ride = put_g([kl for kl in _layer_keys(i, n_a) if kl[0] in g], g)
        (dx, dxb, dg_mix[i]), landed = _mm_norm_bwd([(dp, w_in)], form, st["x"], row(norms["mix_norm"][i]), dx,
                                                    name=f"d_h{i}", ride=ride)
        done_g(landed)
        if i == n_a:
            dkv = jnp.concatenate([dk, dv], axis=1).astype(BF16)
            g_kv = _mm(dkv, hk, "tn", BF16, name="g_w_kv")
            (dx, dxb, dg_kv), _ = _mm_norm_bwd([(dkv, w_kv)], "nn", x_kv, row(norms["kv_norm"]), dx, name="d_hk")
    _, _, dg_mem = _rmsnorm_bwd(mem, row(norms["mem_norm"]), dmem_n, None, name="mem_norm_bwd")

    small = {"mix_norm": jnp.concatenate(dg_mix, axis=0), "ffn_norm": jnp.concatenate(dg_ffn, axis=0), "kv_norm": dg_kv[0],
             "mem_norm": dg_mem[0], "final_norm": dg_final[0], "conv_w": jnp.stack(dconv, axis=0)}
    return loss, dx, small


_COL_SHARDED = ("a", "kv", "g", "u")
_NAMES = {"a": "a_in", "kv": "w_kv_shared", "g": "w_gate", "u": "w_up", "b": "b_in", "o": "w_o", "d": "w_down", "mkv": "w_mem_kv"}
_ORDER = ("a", "kv", "g", "u", "d", "b", "o", "mkv")
_WEIGHTS = ("mix_norm", "a_in", "conv_w", "b_in", "kv_norm", "w_kv_shared", "w_mem_kv", "w_o", "ffn_norm", "w_gate", "w_up",
            "w_down", "mem_norm", "final_norm")


def _layer_keys(i, n_a):
    keys = [("a", i) if i < n_a else ("b", i - n_a), ("g", i), ("u", i), ("d", i), ("o", i), ("mkv", i)]
    return keys + [("kv", 0)] if i == n_a - 1 else keys


def _gather_groups(i, n_a):
    first, rest = _layer_keys(i, n_a)[0], dict(_layer_keys(i, n_a)[1:])
    small = [(k, rest[k]) for k in ("o", "mkv", "kv") if k in rest]
    return [[first], small, [("g", rest["g"]), ("u", rest["u"])], [("d", rest["d"])]]


def _canonical(key, w):
    w3 = w if w.ndim == 3 else w[None]
    if key in _COL_SHARDED:
        w3 = jnp.transpose(w3, (0, 2, 1))
    return w3


def _uncanonical(key, g3, like):
    if key in _COL_SHARDED:
        g3 = jnp.transpose(g3, (0, 2, 1))
    return g3.reshape(like.shape)


def _pad_rows(flat, C):
    n = flat.shape[0]
    rows = -(-n // C)
    return jnp.pad(flat, (0, rows * C - n)).reshape(rows, C)


def kernel(x, mem, mix_norm, a_in, conv_w, b_in, kv_norm, w_kv_shared, w_mem_kv, w_o, ffn_norm, w_gate, w_up, w_down, mem_norm, final_norm, loss_target, m_mix_norm, m_a_in, m_conv_w, m_b_in, m_kv_norm, m_w_kv_shared, m_w_mem_kv, m_w_o, m_ffn_norm, m_w_gate, m_w_up, m_w_down, m_mem_norm, m_final_norm, v_mix_norm, v_a_in, v_conv_w, v_b_in, v_kv_norm, v_w_kv_shared, v_w_mem_kv, v_w_o, v_ffn_norm, v_w_gate, v_w_up, v_w_down, v_mem_norm, v_final_norm):
    weights = dict(mix_norm=mix_norm, a_in=a_in, conv_w=conv_w, b_in=b_in, kv_norm=kv_norm, w_kv_shared=w_kv_shared,
                   w_mem_kv=w_mem_kv, w_o=w_o, ffn_norm=ffn_norm, w_gate=w_gate, w_up=w_up, w_down=w_down,
                   mem_norm=mem_norm, final_norm=final_norm)
    moments_m = dict(mix_norm=m_mix_norm, a_in=m_a_in, conv_w=m_conv_w, b_in=m_b_in, kv_norm=m_kv_norm,
                     w_kv_shared=m_w_kv_shared, w_mem_kv=m_w_mem_kv, w_o=m_w_o, ffn_norm=m_ffn_norm, w_gate=m_w_gate,
                     w_up=m_w_up, w_down=m_w_down, mem_norm=m_mem_norm, final_norm=m_final_norm)
    moments_v = dict(mix_norm=v_mix_norm, a_in=v_a_in, conv_w=v_conv_w, b_in=v_b_in, kv_norm=v_kv_norm,
                     w_kv_shared=v_w_kv_shared, w_mem_kv=v_w_mem_kv, w_o=v_w_o, ffn_norm=v_ffn_norm, w_gate=v_w_gate,
                     w_up=v_w_up, w_down=v_w_down, mem_norm=v_mem_norm, final_norm=v_final_norm)
    D = x.shape[-1]
    depth, n_a = w_o.shape[0], a_in.shape[0]
    xi, yi, ci = _place()
    me = 4 * xi + 2 * yi + ci
    core = ci.reshape(1).astype(jnp.int32)
    chip = (2 * xi + yi).reshape(1).astype(jnp.int32)

    cw_shape = conv_w.shape
    cw_rows = _pad_rows(conv_w.reshape(-1), D)
    cw_rows = jnp.pad(cw_rows, ((0, 8 - cw_rows.shape[0]), (0, 0)))
    cw_gathered = _exchange(cw_rows, False, name="gather_conv_w")
    n_cw = cw_shape[0] * cw_shape[1] * cw_shape[2]
    cw_all = cw_gathered.reshape(N_DEV, -1)[:, :n_cw].reshape((N_DEV,) + cw_shape)
    conv_full = jnp.transpose(cw_all, (1, 2, 0, 3)).reshape(cw_shape[0], cw_shape[1], N_DEV * cw_shape[2])

    shard3 = {k: _canonical(k, weights[_NAMES[k]]).astype(BF16) for k in _ORDER}
    keys0 = _layer_keys(0, n_a)
    fulls0 = _all_gather_weights([shard3[k][l][None] for k, l in keys0], name="all_gather_layer0")
    full = {kl: f[0] for kl, f in zip(keys0, fulls0)}

    groups = [grp for i in range(1, depth) for grp in _gather_groups(i, n_a)]
    carried, riding = [0], []

    def next_ride():
        n = carried[0]
        carried[0] += 1
        second = groups[n - 1] if 1 <= n <= len(groups) else []
        first = groups[n] if n < len(groups) else []
        if not second + first:
            return None
        riding.append(second + first)
        return _join_rides([_gather_ride_2([full[kl] for kl in second]) if second else None,
                            _gather_ride_1([shard3[k][l] for k, l in first]) if first else None])

    def ride_done(landed):
        if landed:
            full.update(zip(riding.pop(), landed))

    def get_w(i, key):
        return full[(key, dict(_layer_keys(i, n_a))[key])]

    batches = []

    def put_g(keys, g):
        g5 = [g[k].reshape(1, 4, 2, g[k].shape[0] // N_DEV, g[k].shape[1]) for k, _ in keys]
        from_sibling = _rs_sibling(g5, name=f"rs_sibling{len(batches)}")
        sums = [_chip_sum(a, s, core, name=f"chip_sum_{k}{l}") for (k, l), a, s in zip(keys, g5, from_sibling)]
        batches.append([keys, sums, None])
        return _chips_ride(sums)

    def done_g(landed):
        batches[-1][2] = landed

    norms = {k: weights[k] for k in ("mix_norm", "ffn_norm", "kv_norm", "mem_norm", "final_norm")}
    loss, grad_x, small = _local_step(x[0], mem[0], loss_target[0], norms, conv_full, depth, n_a, get_w, next_ride, ride_done,
                                      put_g, done_g)

    shard_grads = {(k, l): _final_sum(s, land, chip, name=f"final_sum_{k}{l}")[0]
                   for keys, sums, landed in batches for (k, l), s, land in zip(keys, sums, landed)}
    grads = {}
    for k in _ORDER:
        g3 = jnp.stack([shard_grads[(k, l)] for l in range(shard3[k].shape[0])])
        grads[_NAMES[k]] = _uncanonical(k, g3, weights[_NAMES[k]])

    order = ("mix_norm", "ffn_norm", "kv_norm", "mem_norm", "final_norm", "conv_w")
    flat = jnp.concatenate([small[k].reshape(-1) for k in order] + [loss[0, :1]])
    n_flat = flat.shape[0]
    rows = _pad_rows(flat, D)
    rows = jnp.pad(rows, ((0, (-rows.shape[0]) % 8), (0, 0)))
    total = _exchange(rows, True, name="all_reduce_small").reshape(-1)[:n_flat]
    off = 0
    for k in order:
        n = small[k].size
        grads[k] = total[off:off + n].reshape(small[k].shape)
        off += n
    loss_total = total[off]
    grads["conv_w"] = lax.dynamic_slice_in_dim(grads["conv_w"], me * cw_shape[2], cw_shape[2], axis=2)

    deltas, new_m, new_v = {}, {}, {}
    for k in _WEIGHTS:
        w = weights[k]
        two = (lambda a: a.reshape(-1, a.shape[-1])) if w.ndim > 1 else (lambda a: a.reshape(1, -1))
        d, nm, nv = _adamw(two(w), two(grads[k]), two(moments_m[k]), two(moments_v[k]), name=f"adamw_{k}")
        deltas[k], new_m[k], new_v[k] = d.reshape(w.shape), nm.reshape(w.shape), nv.reshape(w.shape)

    return (loss_total, grad_x[None], *[grads[k] for k in _WEIGHTS], *[deltas[k] for k in _WEIGHTS],
            *[new_m[k] for k in _WEIGHTS], *[new_v[k] for k in _WEIGHTS])
```

```python
import functools
import math

import jax
import jax.numpy as jnp
from jax import lax
from jax.experimental import pallas as pl
from jax.experimental.pallas import tpu as pltpu

F32 = jnp.float32
BF16 = jnp.bfloat16
MESH = pl.DeviceIdType.MESH

HEAD_DIM = 64
MEM_HEADS = 4
MEM_WIDTH = MEM_HEADS * HEAD_DIM
EPS = 1e-6
LANES = 128
BF16_ROWS = 16
VMEM_LIMIT = 56 * 1024 * 1024
N_DEV = 8

ADAM_LR = 0.001
ADAM_B1 = 0.9
ADAM_B2 = 0.999
ADAM_EPS = 1e-08
ADAM_WD = 0.01
ADAM_STEP = 10

ANY = pl.BlockSpec(memory_space=pl.ANY)


def _cparams(sem=None):
    return pltpu.CompilerParams(dimension_semantics=sem, vmem_limit_bytes=VMEM_LIMIT)


def _pick(n, cands):
    for c in cands:
        if n % c == 0:
            return c
    raise ValueError(f"no tile for {n} in {cands}")


def _mm(a, b, form, out_dtype, *, name, residual=None, ride=None):
    if form == "tn":
        K, M = a.shape
    else:
        M, K = a.shape
    if form == "nt":
        N, K2 = b.shape
    else:
        K2, N = b.shape
    assert K == K2, (name, a.shape, b.shape)
    wide = (1408, 1280, 1024, 768, 512, 256, 128)
    tm = _pick(M, wide if form == "tn" else (1024, 512, 256, 128))
    tn = _pick(N, wide)
    tk = _pick(K, (1024, 1408, 1280, 768, 512, 256))
    nk = K // tk
    dims = {"nn": (((1,), (0,)), ((), ())), "nt": (((1,), (1,)), ((), ())), "tn": (((0,), (0,)), ((), ()))}[form]
    a_bytes, b_bytes = M * K * a.dtype.itemsize, N * K * b.dtype.itemsize
    n_outer = nk == 1 and (N // tn) * a_bytes + b_bytes < a_bytes + (M // tm) * b_bytes
    ij = (lambda g0, g1: (g1, g0)) if n_outer else (lambda g0, g1: (g0, g1))

    def spec(block, f):
        return pl.BlockSpec(block, lambda g0, g1, k: f(*ij(g0, g1), k))

    a_spec = spec((tk, tm), lambda i, j, k: (k, i)) if form == "tn" else spec((tm, tk), lambda i, j, k: (i, k))
    b_spec = spec((tn, tk), lambda i, j, k: (j, k)) if form == "nt" else spec((tk, tn), lambda i, j, k: (k, j))
    out_spec = spec((tm, tn), lambda i, j, k: (i, j))
    operands, in_specs = [a, b], [a_spec, b_spec]
    has_res = residual is not None
    if has_res:
        operands.append(residual)
        in_specs.append(out_spec)
    grid = (N // tn, M // tm, nk) if n_outer else (M // tm, N // tn, nk)

    def body(*refs):
        a_ref, b_ref = refs[0], refs[1]
        r_ref = refs[2] if has_res else None
        o_ref = refs[2 + int(has_res)]
        acc_ref = refs[-1]
        part = lax.dot_general(a_ref[...].astype(BF16), b_ref[...].astype(BF16), dims, preferred_element_type=F32)

        def finish(total):
            if has_res:
                total = total + r_ref[...].astype(F32)
            o_ref[...] = total.astype(out_dtype)

        if nk == 1:
            finish(part)
        else:
            k = pl.program_id(2)

            @pl.when(k == 0)
            def _():
                acc_ref[...] = part

            @pl.when(jnp.logical_and(k > 0, k < nk - 1))
            def _():
                acc_ref[...] += part

            @pl.when(k == nk - 1)
            def _():
                finish(acc_ref[...] + part)

    def edges():
        ids = [pl.program_id(d) for d in range(3)]
        first = jnp.logical_and(jnp.logical_and(ids[0] == 0, ids[1] == 0), ids[2] == 0)
        last = jnp.logical_and(jnp.logical_and(ids[0] == grid[0] - 1, ids[1] == grid[1] - 1), ids[2] == grid[2] - 1)
        return first, last

    (out,), landed = _ride_call(
        body, ride, edges, name=name, out_shape=[jax.ShapeDtypeStruct((M, N), out_dtype)], grid=grid, in_specs=in_specs,
        out_specs=[out_spec], scratch_shapes=[pltpu.VMEM((tm, tn), F32)],
        compiler_params=_cparams(("parallel", "parallel", "arbitrary")), operands=operands)
    return out if ride is None else (out, landed)


def _rmsnorm(x, g, *, name):
    R, D = x.shape
    tr = _pick(R, (512, 256))

    def body(x_ref, g_ref, o_ref):
        xv = x_ref[...]
        r = lax.rsqrt(jnp.mean(xv * xv, axis=-1, keepdims=True) + EPS)
        o_ref[...] = (xv * r * g_ref[...]).astype(BF16)

    return pl.pallas_call(
        body, name=name, out_shape=jax.ShapeDtypeStruct((R, D), BF16), grid=(R // tr,),
        in_specs=[pl.BlockSpec((tr, D), lambda i: (i, 0)), pl.BlockSpec((1, D), lambda i: (0, 0))],
        out_specs=pl.BlockSpec((tr, D), lambda i: (i, 0)), compiler_params=_cparams(("parallel",)),
    )(x, g)


def _rmsnorm_bwd(x, g, dh, dx_in, *, name):
    R, D = x.shape
    tr = _pick(R, (512, 256))
    has_in = dx_in is not None

    def body(*refs):
        x_ref, g_ref, dh_ref = refs[:3]
        dxi_ref = refs[3] if has_in else None
        dx_ref, dxb_ref, dg_ref = refs[3 + int(has_in):]
        xv = x_ref[...]
        r = lax.rsqrt(jnp.mean(xv * xv, axis=-1, keepdims=True) + EPS)
        xhat = xv * r
        dhv = dh_ref[...].astype(F32)
        dxh = dhv * g_ref[...]
        dx = r * (dxh - xhat * jnp.mean(dxh * xhat, axis=-1, keepdims=True))
        if has_in:
            dx = dx + dxi_ref[...]
        dx_ref[...] = dx
        dxb_ref[...] = dx.astype(BF16)
        part = jnp.sum(dhv * xhat, axis=0, keepdims=True)

        @pl.when(pl.program_id(0) == 0)
        def _():
            dg_ref[...] = part

        @pl.when(pl.program_id(0) > 0)
        def _():
            dg_ref[...] += part

    row = pl.BlockSpec((tr, D), lambda i: (i, 0))
    vec = pl.BlockSpec((1, D), lambda i: (0, 0))
    ops = [x, g, dh] + ([dx_in] if has_in else [])
    return pl.pallas_call(
        body, name=name,
        out_shape=(jax.ShapeDtypeStruct((R, D), F32), jax.ShapeDtypeStruct((R, D), BF16), jax.ShapeDtypeStruct((1, D), F32)),
        grid=(R // tr,), in_specs=[row, vec, row] + ([row] if has_in else []), out_specs=(row, row, vec),
        compiler_params=_cparams(("arbitrary",)),
    )(*ops)


class _Ride:
    def __init__(self, srcs, dsts, start, wait, sems):
        self.srcs, self.dsts, self.start, self.wait, self.sems = list(srcs), list(dsts), start, wait, list(sems)


def _ride_call(body, ride, edges, *, name, out_shape, grid, in_specs, out_specs, scratch_shapes, compiler_params, operands):
    n_in, n_out, n_scr = len(in_specs), len(out_specs), len(scratch_shapes)
    if ride is None:
        outs = pl.pallas_call(body, name=name, out_shape=tuple(out_shape), grid=grid, in_specs=list(in_specs),
                              out_specs=tuple(out_specs), scratch_shapes=list(scratch_shapes),
                              compiler_params=compiler_params)(*operands)
        return tuple(outs), []
    ns, nd = len(ride.srcs), len(ride.dsts)

    def riding(*refs):
        ins, rin = refs[:n_in], refs[n_in:n_in + ns + nd]
        outs = refs[n_in + ns + nd:n_in + ns + nd + n_out]
        scratch = refs[n_in + ns + 2 * nd + n_out:]
        sems = scratch[n_scr:]
        first, last = edges()

        @pl.when(first)
        def _():
            ride.start(rin[:ns], rin[ns:], sems)

        body(*ins, *outs, *scratch[:n_scr])

        @pl.when(last)
        def _():
            ride.wait(rin[:ns], rin[ns:], sems)

    outs = pl.pallas_call(
        riding, name=name, out_shape=(*out_shape, *[jax.ShapeDtypeStruct(d.shape, d.dtype) for d in ride.dsts]), grid=grid,
        in_specs=[*in_specs, *[ANY] * (ns + nd)], out_specs=(*out_specs, *[ANY] * nd),
        scratch_shapes=[*scratch_shapes, *[pltpu.SemaphoreType.DMA(s) for s in ride.sems]],
        input_output_aliases={n_in + ns + d: n_out + d for d in range(nd)}, compiler_params=compiler_params,
    )(*operands, *ride.srcs, *ride.dsts)
    return tuple(outs[:n_out]), list(outs[n_out:])


def _mm_norm_bwd(parts, form, x, g, dx_in, *, name, ride=None):
    S, K = parts[0][0].shape
    D = x.shape[1]
    tm = _pick(S, (512, 256))
    tk = _pick(K, (1024, 1408, 1280, 768, 512, 256))
    nk, P = K // tk, len(parts)
    dims = _NN if form == "nn" else _NT

    ni, nsteps = S // tm, P * nk

    def body(*refs):
        ab = refs[:2 * P]
        x_ref, g_ref, dxi_ref, dx_ref, dxb_ref, dg_ref, acc_ref = refs[2 * P:]
        k, i = pl.program_id(0), pl.program_id(1)
        rows = pl.ds(pl.multiple_of(i * tm, tm), tm)

        @pl.when(k == 0)
        def _():
            acc_ref[rows, :] = jnp.zeros((tm, D), F32)

        for p in range(P):
            @pl.when(jnp.logical_and(k >= p * nk, k < (p + 1) * nk))
            def _():
                acc_ref[rows, :] += _dot(ab[2 * p][...], ab[2 * p + 1][...], dims)

        @pl.when(k == nsteps - 1)
        def _():
            xv = x_ref[...]
            r = lax.rsqrt(jnp.mean(xv * xv, axis=-1, keepdims=True) + EPS)
            xhat = xv * r
            dhv = acc_ref[rows, :]
            dxh = dhv * g_ref[...]
            dx = r * (dxh - xhat * jnp.mean(dxh * xhat, axis=-1, keepdims=True)) + dxi_ref[...]
            dx_ref[...] = dx
            dxb_ref[...] = dx.astype(BF16)
            part = jnp.sum(dhv * xhat, axis=0, keepdims=True)

            @pl.when(i == 0)
            def _():
                dg_ref[...] = part

            @pl.when(i > 0)
            def _():
                dg_ref[...] += part

    def kk(p):
        return lambda k: jnp.clip(k - p * nk, 0, nk - 1)

    def active_row(p):
        return lambda k, i: jnp.where(jnp.logical_and(k >= p * nk, k < (p + 1) * nk), i, 0)

    in_specs, operands = [], []
    for p, (a, b) in enumerate(parts):
        in_specs.append(pl.BlockSpec((tm, tk), lambda k, i, f=kk(p), r=active_row(p): (r(k, i), f(k))))
        in_specs.append(pl.BlockSpec((tk, D), lambda k, i, f=kk(p): (f(k), 0)) if form == "nn"
                        else pl.BlockSpec((D, tk), lambda k, i, f=kk(p): (0, f(k))))
        operands += [a, b]
    row = pl.BlockSpec((tm, D), lambda k, i: (jnp.where(k == nsteps - 1, i, 0), 0))
    vec = pl.BlockSpec((1, D), lambda k, i: (0, 0))

    def edges():
        k, i = pl.program_id(0), pl.program_id(1)
        return jnp.logical_and(i == 0, k == 0), jnp.logical_and(i == ni - 1, k == nsteps - 1)

    return _ride_call(
        body, ride, edges, name=name,
        out_shape=(jax.ShapeDtypeStruct((S, D), F32), jax.ShapeDtypeStruct((S, D), BF16), jax.ShapeDtypeStruct((1, D), F32)),
        grid=(nsteps, ni), in_specs=in_specs + [row, vec, row], out_specs=(row, row, vec),
        scratch_shapes=[pltpu.VMEM((S, D), F32)], compiler_params=_cparams(("arbitrary", "arbitrary")),
        operands=[*operands, x, g, dx_in])


def _loss_head(x, g, target, *, name):
    R, D = x.shape
    tr = _pick(R, (512, 256))

    def body(x_ref, g_ref, t_ref, dx_ref, dxb_ref, dg_ref, loss_ref):
        xv = x_ref[...]
        gv = g_ref[...]
        r = lax.rsqrt(jnp.mean(xv * xv, axis=-1, keepdims=True) + EPS)
        xhat = xv * r
        err = xhat * gv - t_ref[...]
        loss = 0.5 * jnp.sum(jnp.mean(err * err, axis=-1, keepdims=True), axis=0, keepdims=True)
        dy = err * (1.0 / D)
        dxh = dy * gv
        dx = r * (dxh - xhat * jnp.mean(dxh * xhat, axis=-1, keepdims=True))
        dx_ref[...] = dx
        dxb_ref[...] = dx.astype(BF16)
        dg = jnp.sum(dy * xhat, axis=0, keepdims=True)
        lossv = jnp.broadcast_to(loss, (1, LANES))

        @pl.when(pl.program_id(0) == 0)
        def _():
            dg_ref[...] = dg
            loss_ref[...] = lossv

        @pl.when(pl.program_id(0) > 0)
        def _():
            dg_ref[...] += dg
            loss_ref[...] += lossv

    row = pl.BlockSpec((tr, D), lambda i: (i, 0))
    vec = pl.BlockSpec((1, D), lambda i: (0, 0))
    return pl.pallas_call(
        body, name=name,
        out_shape=(jax.ShapeDtypeStruct((R, D), F32), jax.ShapeDtypeStruct((R, D), BF16), jax.ShapeDtypeStruct((1, D), F32),
                   jax.ShapeDtypeStruct((1, LANES), F32)),
        grid=(R // tr,), in_specs=[row, vec, row], out_specs=(row, row, vec, pl.BlockSpec((1, LANES), lambda i: (0, 0))),
        compiler_params=_cparams(("arbitrary",)),
    )(x, g, target)


def _conv_taps(gv, S):
    t = lax.broadcasted_iota(jnp.int32, gv.shape, 0)
    g1 = jnp.where(t >= 1, pltpu.roll(gv, 1, 0), 0.0)
    g2 = jnp.where(t >= 2, pltpu.roll(gv, 2, 0), 0.0)
    return g1, g2


def _conv_fwd(p, w, main, *, name):
    S = p.shape[0]
    tc = LANES
    nb = main // tc

    def body(b_ref, c_ref, u_ref, w_ref, y_ref):
        gv = c_ref[...].astype(F32) * u_ref[...].astype(F32)
        g1, g2 = _conv_taps(gv, S)
        cv = w_ref[0:1, :] * g2 + w_ref[1:2, :] * g1 + w_ref[2:3, :] * gv
        y_ref[...] = (b_ref[...].astype(F32) * cv).astype(BF16)

    col = lambda off: pl.BlockSpec((S, tc), lambda j: (0, off + j))
    return pl.pallas_call(
        body, name=name, out_shape=jax.ShapeDtypeStruct((S, p.shape[1] - 2 * main), BF16), grid=(nb,),
        in_specs=[col(0), col(nb), col(2 * nb), pl.BlockSpec((3, tc), lambda j: (0, j))],
        out_specs=pl.BlockSpec((S, tc), lambda j: (0, j)), compiler_params=_cparams(("parallel",)),
    )(p, p, p, w)


def _conv_bwd(p, w, dy, main, *, name):
    S = p.shape[0]
    tc = LANES
    nb = main // tc

    def body(b_ref, c_ref, u_ref, w_ref, dy_ref, db_ref, dc_ref, du_ref, dw_ref):
        cvv, uv = c_ref[...].astype(F32), u_ref[...].astype(F32)
        gv = cvv * uv
        g1, g2 = _conv_taps(gv, S)
        w0, w1, w2 = w_ref[0:1, :], w_ref[1:2, :], w_ref[2:3, :]
        dyv = dy_ref[...].astype(F32)
        db_ref[...] = (dyv * (w0 * g2 + w1 * g1 + w2 * gv)).astype(BF16)
        dcv = dyv * b_ref[...].astype(F32)
        t = lax.broadcasted_iota(jnp.int32, dcv.shape, 0)
        n1 = jnp.where(t <= S - 2, pltpu.roll(dcv, S - 1, 0), 0.0)
        n2 = jnp.where(t <= S - 3, pltpu.roll(dcv, S - 2, 0), 0.0)
        dg = w2 * dcv + w1 * n1 + w0 * n2
        dc_ref[...] = (dg * uv).astype(BF16)
        du_ref[...] = (dg * cvv).astype(BF16)
        dw_ref[0:1, :] = jnp.sum(dcv * g2, axis=0, keepdims=True)
        dw_ref[1:2, :] = jnp.sum(dcv * g1, axis=0, keepdims=True)
        dw_ref[2:3, :] = jnp.sum(dcv * gv, axis=0, keepdims=True)

    col = lambda off: pl.BlockSpec((S, tc), lambda j: (0, off + j))
    out = jax.ShapeDtypeStruct((S, main), BF16)
    return pl.pallas_call(
        body, name=name, out_shape=(out, out, out, jax.ShapeDtypeStruct((3, main), F32)), grid=(nb,),
        in_specs=[col(0), col(nb), col(2 * nb), pl.BlockSpec((3, tc), lambda j: (0, j)), col(0)],
        out_specs=(col(0), col(0), col(0), pl.BlockSpec((3, tc), lambda j: (0, j))),
        compiler_params=_cparams(("parallel",)),
    )(p, p, p, w, dy)


def _head_mask(width, h):
    lane = lax.broadcasted_iota(jnp.int32, (1, width), 1)
    return jnp.logical_and(lane >= h * HEAD_DIM, lane < (h + 1) * HEAD_DIM)


_NT = (((1,), (1,)), ((), ()))
_NN = (((1,), (0,)), ((), ()))
_TN = (((0,), (0,)), ((), ()))


def _dot(a, b, dims):
    return lax.dot_general(a, b, dims, preferred_element_type=F32)


def _mem_probs(qh, kv):
    s = _dot(qh, kv, _NT) * (1.0 / math.sqrt(HEAD_DIM))
    s = s - jnp.max(s, axis=-1, keepdims=True)
    e = jnp.exp(s)
    return e / jnp.sum(e, axis=-1, keepdims=True)


def _memattn_fwd(p, qblk, mkv, into, *, name):
    S = p.shape[0]
    M = mkv.shape[0]
    W = MEM_WIDTH
    tq = _pick(S, (512, 256))
    last = into.shape[1] // W - 1

    def body(q_ref, k_ref, v_ref, _, o_ref):
        q = q_ref[...].astype(BF16)
        kv, vv = k_ref[...], v_ref[...]
        out = jnp.zeros((tq, W), F32)
        for h in range(MEM_HEADS):
            m = _head_mask(W, h)
            pr = _mem_probs(jnp.where(m, q, jnp.zeros_like(q)), kv)
            out = jnp.where(m, _dot(pr.astype(BF16), vv, _NN), out)
        o_ref[...] = out.astype(BF16)

    return pl.pallas_call(
        body, name=name, out_shape=jax.ShapeDtypeStruct(into.shape, BF16), grid=(S // tq,),
        in_specs=[pl.BlockSpec((tq, W), lambda i: (i, qblk)), pl.BlockSpec((M, W), lambda i: (0, 0)),
                  pl.BlockSpec((M, W), lambda i: (0, 1)), ANY],
        out_specs=pl.BlockSpec((tq, W), lambda i: (i, last)), input_output_aliases={3: 0},
        compiler_params=_cparams(("parallel",)),
    )(p, mkv, mkv, into)


def _memattn_bwd(p, qblk, mkv, dy, dyblk, *, name):
    S = p.shape[0]
    M = mkv.shape[0]
    W = MEM_WIDTH
    tq = _pick(S, (512, 256))
    scale = 1.0 / math.sqrt(HEAD_DIM)

    def body(q_ref, k_ref, v_ref, do_ref, dq_ref, dk_ref, dv_ref, dk_acc, dv_acc):
        q = q_ref[...].astype(BF16)
        do = do_ref[...].astype(BF16)
        kv, vv = k_ref[...], v_ref[...]
        dq = jnp.zeros((tq, W), F32)
        dk = jnp.zeros((M, W), F32)
        dv = jnp.zeros((M, W), F32)
        for h in range(MEM_HEADS):
            m = _head_mask(W, h)
            qh = jnp.where(m, q, jnp.zeros_like(q))
            doh = jnp.where(m, do, jnp.zeros_like(do))
            pr = _mem_probs(qh, kv)
            dpr = _dot(doh, vv, _NT)
            ds = (pr * (dpr - jnp.sum(dpr * pr, axis=-1, keepdims=True)) * scale).astype(BF16)
            dq = jnp.where(m, _dot(ds, kv, _NN), dq)
            dk = dk + _dot(ds, qh, _TN)
            dv = dv + _dot(pr.astype(BF16), doh, _TN)
        dq_ref[...] = dq.astype(BF16)
        i = pl.program_id(0)

        @pl.when(i == 0)
        def _():
            dk_acc[...] = dk
            dv_acc[...] = dv

        @pl.when(i > 0)
        def _():
            dk_acc[...] += dk
            dv_acc[...] += dv

        @pl.when(i == S // tq - 1)
        def _():
            dk_ref[...] = dk_acc[...].astype(BF16)
            dv_ref[...] = dv_acc[...].astype(BF16)

    kspec = lambda c: pl.BlockSpec((M, W), lambda i: (0, c))
    return pl.pallas_call(
        body, name=name,
        out_shape=(jax.ShapeDtypeStruct((S, W), BF16), jax.ShapeDtypeStruct((M, W), BF16), jax.ShapeDtypeStruct((M, W), BF16)),
        grid=(S // tq,),
        in_specs=[pl.BlockSpec((tq, W), lambda i: (i, qblk)), kspec(0), kspec(1), pl.BlockSpec((tq, W), lambda i: (i, dyblk))],
        out_specs=(pl.BlockSpec((tq, W), lambda i: (i, 0)), kspec(0), kspec(0)),
        scratch_shapes=[pltpu.VMEM((M, W), F32), pltpu.VMEM((M, W), F32)],
        compiler_params=_cparams(("arbitrary",)),
    )(p, mkv, mkv, dy)


SB_TQ = 256
SB_CLAMP = 80.0
SB_DEAD = 110.0


SB_CHUNK = 64


def _by_rows(fn, *arrays):
    rows = next(a for a in arrays if a is not None).shape[0]
    outs = [fn(*[None if a is None else a[r0:r0 + SB_CHUNK] for a in arrays]) for r0 in range(0, rows, SB_CHUNK)]
    return tuple(jnp.concatenate(col, axis=0) for col in zip(*outs))


def _sb_scores(qh, kb, causal):
    def chain(z, mask):
        z = jnp.clip(z, -SB_CLAMP, SB_CLAMP)
        w = 1.0 + jnp.exp(z)
        sp = jnp.log(w)
        zs = z - sp
        if mask is not None:
            sp = jnp.where(mask, sp, 0.0)
            zs = jnp.where(mask, zs, -1e30)
            w = jnp.where(mask, w, 1.0)
        return zs, sp.astype(BF16), jnp.sum(sp, axis=1, keepdims=True), w

    return _by_rows(chain, _dot(qh, kb, _NT), causal)


def _sb_weights(zs, spb, tri, carry):
    return _by_rows(lambda zs_c, t_c, c_c: (jnp.exp(zs_c - (t_c + c_c)).astype(BF16),), zs, _dot(spb, tri, _NN), carry)[0]


def _sb_live(carry):
    return jnp.min(carry) <= SB_DEAD


def _stack_heads(v, m0):
    zero = jnp.zeros_like(v)
    return jnp.concatenate([jnp.where(m0, v, zero), jnp.where(m0, zero, v)], axis=0)


def _stacked_causal(tq):
    r = lax.broadcasted_iota(jnp.int32, (2 * tq, tq), 0)
    c = lax.broadcasted_iota(jnp.int32, (2 * tq, tq), 1)
    return c < jnp.where(r >= tq, r - tq, r)


def _sb_fwd(p, kv, heads, *, name):
    S = p.shape[0]
    tq = SB_TQ
    npair = heads // 2

    def body(q_ref, k_ref, v_ref, o_ref, o32_ref):
        qi = pl.program_id(1)
        r = lax.broadcasted_iota(jnp.int32, (tq, tq), 0)
        c = lax.broadcasted_iota(jnp.int32, (tq, tq), 1)
        tri = (r > c).astype(BF16)
        causal = _stacked_causal(tq)
        m0 = _head_mask(LANES, 0)
        qh = _stack_heads(q_ref[...] * jnp.asarray(1.0 / math.sqrt(HEAD_DIM), BF16), m0)

        def block(j, carry, acc, mask):
            off = pl.multiple_of(j * tq, tq)
            kb = k_ref[pl.ds(off, tq), :]
            vb = v_ref[pl.ds(off, tq), :]
            zs, spb, sp_sum, _ = _sb_scores(qh, kb, mask)
            acc = acc + _dot(_sb_weights(zs, spb, tri, carry), vb, _NN)
            return carry + sp_sum, acc

        st = (jnp.zeros((2 * tq, 1), F32), jnp.zeros((2 * tq, LANES), F32))
        st = lax.cond(qi >= 1, lambda s: block(qi - 1, *block(qi, *s, causal), None), lambda s: block(qi, *s, causal), st)
        left = jnp.maximum(qi - 1, 0)
        odd = left % 2
        st = lax.cond(jnp.logical_and(odd == 1, _sb_live(st[0])), lambda s: block(qi - 2, *s, None), lambda s: s, st)

        def pair(s):
            it, _, carry, acc = s
            j = qi - 2 - odd - 2 * it
            carry, acc = block(j, carry, acc, None)
            carry, acc = block(j - 1, carry, acc, None)
            return it + 1, _sb_live(carry), carry, acc

        _, _, carry, acc = lax.while_loop(lambda s: jnp.logical_and(s[0] < left // 2, s[1]), pair,
                                          (jnp.int32(0), _sb_live(st[0]), st[0], st[1]))
        out = jnp.where(m0, acc[:tq], acc[tq:])
        o_ref[...] = out.astype(BF16)
        o32_ref[...] = out

    W = heads * HEAD_DIM
    qspec = pl.BlockSpec((tq, LANES), lambda hp, i: (i, hp))
    return pl.pallas_call(
        body, name=name, out_shape=(jax.ShapeDtypeStruct(p.shape, BF16), jax.ShapeDtypeStruct((S, W), F32)), grid=(npair, S // tq),
        in_specs=[qspec, pl.BlockSpec((S, LANES), lambda hp, i: (0, hp)), pl.BlockSpec((S, LANES), lambda hp, i: (0, npair + hp))],
        out_specs=(qspec, qspec), compiler_params=_cparams(("parallel", "arbitrary")),
    )(p, kv, kv)


def _sb_bwd(p, kv, o32, dy, heads, dk_in, dv_in, *, name):
    S = p.shape[0]
    tq = SB_TQ
    npair = heads // 2
    has_in = dk_in is not None
    scale = 1.0 / math.sqrt(HEAD_DIM)

    def body(*refs):
        q_ref, k_ref, v_ref, o_ref, do_ref = refs[:5]
        dq_ref, dk_ref, dv_ref = refs[5 + 2 * int(has_in):]
        qi = pl.program_id(1)

        @pl.when(qi == 0)
        def _():
            if has_in:
                dk_ref[...] = refs[5][...]
                dv_ref[...] = refs[6][...]
            else:
                dk_ref[...] = jnp.zeros_like(dk_ref)
                dv_ref[...] = jnp.zeros_like(dv_ref)

        r = lax.broadcasted_iota(jnp.int32, (tq, tq), 0)
        c = lax.broadcasted_iota(jnp.int32, (tq, tq), 1)
        tri = (r > c).astype(BF16)
        tri_low = (r < c).astype(BF16)
        causal = _stacked_causal(tq)
        m0 = _head_mask(LANES, 0)
        qh = _stack_heads(q_ref[...] * jnp.asarray(scale, BF16), m0)
        do = do_ref[...]
        doh = _stack_heads(do, m0)
        dov = do.astype(F32) * o_ref[...]
        dsum = jnp.concatenate([jnp.sum(jnp.where(m0, dov, 0.0), axis=1, keepdims=True),
                                jnp.sum(jnp.where(m0, 0.0, dov), axis=1, keepdims=True)], axis=0)

        def block(j, carry, gcarry, acc, mask):
            off = pl.multiple_of(j * tq, tq)
            kb = k_ref[pl.ds(off, tq), :]
            vb = v_ref[pl.ds(off, tq), :]
            zs, spb, sp_sum, w = _sb_scores(qh, kb, mask)
            ab = _sb_weights(zs, spb, tri, carry)

            def grads(ab_c, da_c):
                g = ab_c.astype(F32) * da_c
                return g, g.astype(BF16), jnp.sum(g, axis=1, keepdims=True)

            g, gb, g_sum = _by_rows(grads, ab, _dot(doh, vb, _NT))
            gcarry = gcarry + g_sum

            def logit_grads(g_c, w_c, low_c, left_c):
                rinv = 1.0 / w_c
                return ((g_c * rinv - (left_c + low_c) * (1.0 - rinv)).astype(BF16),)

            dzs = _by_rows(logit_grads, g, w, _dot(gb, tri_low, _NN), dsum - gcarry)[0]
            acc = acc + _dot(dzs, kb, _NN)
            dk_ref[pl.ds(off, tq), :] += _dot(dzs, qh, _TN)
            dv_ref[pl.ds(off, tq), :] += _dot(ab, doh, _TN)
            return (carry + sp_sum, gcarry, acc)

        zero = jnp.zeros((2 * tq, 1), F32)
        st = (zero, zero, jnp.zeros((2 * tq, LANES), F32))
        st = lax.cond(qi >= 1, lambda s: block(qi - 1, *block(qi, *s, causal), None), lambda s: block(qi, *s, causal), st)
        left = jnp.maximum(qi - 1, 0)
        odd = left % 2
        st = lax.cond(jnp.logical_and(odd == 1, _sb_live(st[0])), lambda s: block(qi - 2, *s, None), lambda s: s, st)

        def pair(s):
            j = qi - 2 - odd - 2 * s[0]
            b = block(j, s[2], s[3], s[4], None)
            b = block(j - 1, b[0], b[1], b[2], None)
            return (s[0] + 1, _sb_live(b[0])) + b

        st = lax.while_loop(lambda s: jnp.logical_and(s[0] < left // 2, s[1]), pair, (jnp.int32(0), _sb_live(st[0])) + st)[2:]
        dq_ref[...] = (jnp.where(m0, st[2][:tq], st[2][tq:]) * scale).astype(BF16)

    W = heads * HEAD_DIM
    qspec = pl.BlockSpec((tq, LANES), lambda hp, i: (i, hp))
    seq = lambda off: pl.BlockSpec((S, LANES), lambda hp, i: (0, off + hp))
    ops = [p, kv, kv, o32, dy] + ([dk_in, dv_in] if has_in else [])
    return pl.pallas_call(
        body, name=name,
        out_shape=(jax.ShapeDtypeStruct((S, W), BF16), jax.ShapeDtypeStruct((S, W), F32), jax.ShapeDtypeStruct((S, W), F32)),
        grid=(npair, S // tq),
        in_specs=[qspec, seq(0), seq(npair), qspec, qspec] + ([seq(0), seq(0)] if has_in else []),
        out_specs=(qspec, seq(0), seq(0)),
        compiler_params=_cparams(("parallel", "arbitrary")),
    )(*ops)


def _ffn_up(h, wg, wu, *, name, ride=None):
    S, D = h.shape
    F = wg.shape[0]
    tm = _pick(S, (512, 256))
    tn = _pick(F, (1408, 1024, 512, 256, 128))

    def body(h_ref, g_ref, u_ref, act_ref, silu_ref, uds_ref):
        hv = h_ref[...]
        g = _dot(hv, g_ref[...], _NT)
        u = _dot(hv, u_ref[...], _NT)
        s = jax.nn.sigmoid(g)
        silu = g * s
        act_ref[...] = (silu * u).astype(BF16)
        silu_ref[...] = silu.astype(BF16)
        uds_ref[...] = (u * (s + silu * (1.0 - s))).astype(BF16)

    wspec = pl.BlockSpec((tn, D), lambda j, i: (j, 0))
    ospec = pl.BlockSpec((tm, tn), lambda j, i: (i, j))
    out = jax.ShapeDtypeStruct((S, F), BF16)
    grid = (F // tn, S // tm)

    def edges():
        j, i = pl.program_id(0), pl.program_id(1)
        return jnp.logical_and(j == 0, i == 0), jnp.logical_and(j == grid[0] - 1, i == grid[1] - 1)

    return _ride_call(
        body, ride, edges, name=name, out_shape=(out, out, out), grid=grid,
        in_specs=[pl.BlockSpec((tm, D), lambda j, i: (i, 0)), wspec, wspec], out_specs=(ospec, ospec, ospec),
        scratch_shapes=[], compiler_params=_cparams(("parallel", "parallel")), operands=[h, wg, wu])


def _ffn_down_bwd(dx, wd, silu, uds, *, name):
    S, D = dx.shape
    F = wd.shape[0]
    tm = _pick(S, (512, 256))
    tn = _pick(F, (1408, 1024, 512, 256, 128))

    def body(dx_ref, w_ref, silu_ref, uds_ref, dg_ref, du_ref):
        da = _dot(dx_ref[...], w_ref[...], _NT)
        dg_ref[...] = (da * uds_ref[...].astype(F32)).astype(BF16)
        du_ref[...] = (da * silu_ref[...].astype(F32)).astype(BF16)

    ospec = pl.BlockSpec((tm, tn), lambda j, i: (i, j))
    out = jax.ShapeDtypeStruct((S, F), BF16)
    return pl.pallas_call(
        body, name=name, out_shape=(out, out), grid=(F // tn, S // tm),
        in_specs=[pl.BlockSpec((tm, D), lambda j, i: (i, 0)), pl.BlockSpec((tn, D), lambda j, i: (j, 0)), ospec, ospec],
        out_specs=(ospec, ospec), compiler_params=_cparams(("parallel", "parallel")),
    )(dx, wd, silu, uds)


def _adamw(w, g, m, v, *, name):
    R, C = w.shape
    tr = R
    for cand in (1024, 512, 256, 128, 64, 32, 16, 8):
        if R % cand == 0 and cand * C * 4 <= (1 << 20):
            tr = cand
            break
    bc1 = 1.0 - ADAM_B1 ** ADAM_STEP
    bc2 = 1.0 - ADAM_B2 ** ADAM_STEP

    def body(w_ref, g_ref, m_ref, v_ref, d_ref, nm_ref, nv_ref):
        gv = g_ref[...]
        nm = ADAM_B1 * m_ref[...] + (1.0 - ADAM_B1) * gv
        nv = ADAM_B2 * v_ref[...] + (1.0 - ADAM_B2) * (gv * gv)
        nm_ref[...] = nm
        nv_ref[...] = nv
        d_ref[...] = -ADAM_LR * ((nm / bc1) / (jnp.sqrt(nv / bc2) + ADAM_EPS) + ADAM_WD * w_ref[...])

    blk = pl.BlockSpec((tr, C), lambda i: (i, 0))
    out = jax.ShapeDtypeStruct((R, C), F32)
    return pl.pallas_call(body, name=name, out_shape=(out, out, out), grid=(R // tr,), in_specs=[blk] * 4,
                          out_specs=(blk, blk, blk), compiler_params=_cparams(("parallel",)))(w, g, m, v)


def _place():
    x, y, c = lax.axis_index("x"), lax.axis_index("y"), lax.axis_index("c")
    return x, y, c


def _all_gather_weights(shards, *, name):
    n = len(shards)

    def body(*refs):
        sh, full = refs[:n], refs[n:2 * n]
        send_sems, recv_sems, local_sems = refs[2 * n:]
        x, y, c = _place()
        me, sibling = (x, y, c), (x, y, 1 - c)
        chips = [(1 - x, y), (x, 1 - y), (1 - x, 1 - y)]

        def rows(t, px, py, pc):
            r = sh[t].shape[1]
            return full[t].at[:, pl.ds(pl.multiple_of((4 * px + 2 * py + pc) * r, BF16_ROWS), r), :]

        def copy(t, k, block, to, src=None):
            return pltpu.make_async_remote_copy(
                src_ref=rows(t, *block) if src is None else src, dst_ref=rows(t, *block),
                send_sem=send_sems.at[7 * t + k], recv_sem=recv_sems.at[7 * t + k], device_id=to, device_id_type=MESH)

        started = []
        for t in range(n):
            mine = pltpu.make_async_copy(sh[t], rows(t, *me), local_sems.at[t])
            mine.start()
            started.append(mine)
        sends = []
        for t in range(n):
            first = [copy(t, 0, me, sibling, src=sh[t])]
            first += [copy(t, 1 + j, me, (*chip, c), src=sh[t]) for j, chip in enumerate(chips)]
            for cp in first:
                cp.start()
            sends += first
        for t in range(n):
            for j, chip in enumerate(chips):
                copy(t, 1 + j, (*chip, c), me).wait_recv()
                fwd = copy(t, 4 + j, (*chip, c), sibling)
                fwd.start()
                sends.append(fwd)
        for t in range(n):
            copy(t, 0, sibling, me).wait_recv()
            for j, chip in enumerate(chips):
                copy(t, 4 + j, (*chip, 1 - c), me).wait_recv()
        for cp in sends:
            cp.wait_send()
        for cp in started:
            cp.wait()

    out_shape = [jax.ShapeDtypeStruct((s.shape[0], N_DEV * s.shape[1], s.shape[2]), s.dtype) for s in shards]
    return pl.pallas_call(
        body, name=name, out_shape=out_shape, in_specs=[ANY] * n, out_specs=[ANY] * n,
        scratch_shapes=[pltpu.SemaphoreType.DMA((7 * n,)), pltpu.SemaphoreType.DMA((7 * n,)), pltpu.SemaphoreType.DMA((n,))],
    )(*shards)


def _whole(ref_a, ref_b, send_sem, recv_sem, me):
    return pltpu.make_async_remote_copy(src_ref=ref_a, dst_ref=ref_b, send_sem=send_sem, recv_sem=recv_sem,
                                        device_id=me, device_id_type=MESH)


def _rs_sibling(grads, *, name):
    n = len(grads)

    def body(*refs):
        g, land = refs[:n], refs[n:2 * n]
        send_sems, recv_sems = refs[2 * n:]
        x, y, c = _place()
        for t in range(n):
            for k in range(4):
                pltpu.make_async_remote_copy(
                    src_ref=g[t].at[:, k, 1 - c], dst_ref=land[t].at[k], send_sem=send_sems.at[t], recv_sem=recv_sems.at[t],
                    device_id=(x, y, 1 - c), device_id_type=MESH).start()
        for t in range(n):
            w = _whole(land[t], land[t], send_sems.at[t], recv_sems.at[t], (x, y, c))
            w.wait_send()
            w.wait_recv()

    out_shape = [jax.ShapeDtypeStruct((4, s.shape[0], s.shape[3], s.shape[4]), s.dtype) for s in grads]
    return pl.pallas_call(
        body, name=name, out_shape=out_shape, in_specs=[ANY] * n, out_specs=[ANY] * n,
        scratch_shapes=[pltpu.SemaphoreType.DMA((n,)), pltpu.SemaphoreType.DMA((n,))],
    )(*grads)


def _chips_ride(sums):
    n = len(sums)
    lands = [lax.empty((3,) + s.shape[1:], s.dtype) for s in sums]

    def start(s, land, sems):
        x, y, c = _place()
        for t in range(n):
            for j, (px, py) in enumerate([(1 - x, y), (x, 1 - y), (1 - x, 1 - y)]):
                pltpu.make_async_remote_copy(
                    src_ref=s[t].at[2 * px + py], dst_ref=land[t].at[j], send_sem=sems[0].at[t], recv_sem=sems[1].at[t],
                    device_id=(px, py, c), device_id_type=MESH).start()

    def wait(s, land, sems):
        x, y, c = _place()
        for t in range(n):
            w = _whole(land[t], land[t], sems[0].at[t], sems[1].at[t], (x, y, c))
            w.wait_send()
            w.wait_recv()

    return _Ride(sums, lands, start, wait, [(n,), (n,)])


def _gather_ride_1(shards):
    n = len(shards)
    fulls = [lax.empty((N_DEV * s.shape[0], s.shape[1]), s.dtype) for s in shards]

    def rows(full, r, px, py, pc, count=1):
        return full.at[pl.ds(pl.multiple_of((4 * px + 2 * py + pc) * r, BF16_ROWS), count * r), :]

    def start(sh, full, sems):
        x, y, c = _place()
        for t in range(n):
            r = sh[t].shape[0]
            mine = rows(full[t], r, x, y, c)
            pltpu.make_async_copy(sh[t], mine, sems[2].at[t]).start()
            for peer in [(x, y, 1 - c), (1 - x, y, c), (x, 1 - y, c), (1 - x, 1 - y, c)]:
                pltpu.make_async_remote_copy(src_ref=sh[t], dst_ref=mine, send_sem=sems[0].at[t], recv_sem=sems[1].at[t],
                                             device_id=peer, device_id_type=MESH).start()

    def wait(sh, full, sems):
        x, y, c = _place()
        for t in range(n):
            r = sh[t].shape[0]
            pltpu.make_async_copy(sh[t], rows(full[t], r, x, y, c), sems[2].at[t]).wait()
            four = full[t].at[pl.ds(0, 4 * r), :]
            w = _whole(four, four, sems[0].at[t], sems[1].at[t], (x, y, c))
            w.wait_send()
            w.wait_recv()

    return _Ride(shards, fulls, start, wait, [(n,), (n,), (n,)])


def _gather_ride_2(fulls):
    n = len(fulls)

    def start(_, full, sems):
        x, y, c = _place()
        for t in range(n):
            r = full[t].shape[0] // N_DEV
            for px, py in [(1 - x, y), (x, 1 - y), (1 - x, 1 - y)]:
                block = full[t].at[pl.ds(pl.multiple_of((4 * px + 2 * py + c) * r, BF16_ROWS), r), :]
                pltpu.make_async_remote_copy(src_ref=block, dst_ref=block, send_sem=sems[0].at[t], recv_sem=sems[1].at[t],
                                             device_id=(x, y, 1 - c), device_id_type=MESH).start()

    def wait(_, full, sems):
        x, y, c = _place()
        for t in range(n):
            three = full[t].at[pl.ds(0, 3 * (full[t].shape[0] // N_DEV)), :]
            w = _whole(three, three, sems[0].at[t], sems[1].at[t], (x, y, c))
            w.wait_send()
            w.wait_recv()

    return _Ride([], fulls, start, wait, [(n,), (n,)])


def _join_rides(rides):
    rides = [r for r in rides if r is not None]
    if len(rides) <= 1:
        return rides[0] if rides else None

    def parts(src, dst, sems):
        so = do = mo = 0
        for r in rides:
            yield r, src[so:so + len(r.srcs)], dst[do:do + len(r.dsts)], sems[mo:mo + len(r.sems)]
            so, do, mo = so + len(r.srcs), do + len(r.dsts), mo + len(r.sems)

    def start(src, dst, sems):
        for r, s, d, m in parts(src, dst, sems):
            r.start(s, d, m)

    def wait(src, dst, sems):
        for r, s, d, m in parts(src, dst, sems):
            r.wait(s, d, m)

    return _Ride([a for r in rides for a in r.srcs], [a for r in rides for a in r.dsts], start, wait,
                 [m for r in rides for m in r.sems])


def _chip_sum(g, land, core, *, name):
    L, _, _, r, C = g.shape

    def body(core_ref, g_ref, l_ref, o_ref):
        o_ref[...] = (g_ref[...].astype(F32) + l_ref[...].astype(F32)).astype(BF16)

    grid_spec = pltpu.PrefetchScalarGridSpec(
        num_scalar_prefetch=1, grid=(4, L),
        in_specs=[pl.BlockSpec((None, None, None, r, C), lambda k, l, core_ref: (l, k, core_ref[0], 0, 0)),
                  pl.BlockSpec((None, None, r, C), lambda k, l, core_ref: (k, l, 0, 0))],
        out_specs=pl.BlockSpec((None, None, r, C), lambda k, l, core_ref: (k, l, 0, 0)))
    return pl.pallas_call(body, name=name, out_shape=jax.ShapeDtypeStruct((4, L, r, C), BF16), grid_spec=grid_spec,
                          compiler_params=_cparams(("parallel", "parallel")))(core, g, land)


def _final_sum(sums, land, chip, *, name):
    _, L, r, C = sums.shape

    def body(chip_ref, s_ref, a_ref, b_ref, c_ref, o_ref):
        o_ref[...] = ((s_ref[...].astype(F32) + a_ref[...].astype(F32)) + b_ref[...].astype(F32)) + c_ref[...].astype(F32)

    slot = lambda j: pl.BlockSpec((None, None, r, C), lambda l, chip_ref: (j, l, 0, 0))
    grid_spec = pltpu.PrefetchScalarGridSpec(
        num_scalar_prefetch=1, grid=(L,),
        in_specs=[pl.BlockSpec((None, None, r, C), lambda l, chip_ref: (chip_ref[0], l, 0, 0)), slot(0), slot(1), slot(2)],
        out_specs=pl.BlockSpec((None, r, C), lambda l, chip_ref: (l, 0, 0)))
    return pl.pallas_call(body, name=name, out_shape=jax.ShapeDtypeStruct((L, r, C), F32), grid_spec=grid_spec,
                          compiler_params=_cparams(("parallel",)))(chip, sums, land, land, land)


def _exchange(v, reduce, *, name):
    R, C = v.shape

    def body(v_ref, o_ref, *scratch):
        if reduce:
            buf, send_sems, recv_sems = scratch
        else:
            buf = o_ref
            send_sems, recv_sems = scratch
        x, y, c = _place()
        me = 4 * x + 2 * y + c
        buf[me] = v_ref[...]
        copies = []
        for k in range(1, N_DEV):
            kx, ky, kc = (k >> 2) & 1, (k >> 1) & 1, k & 1
            peer = (1 - x if kx else x, 1 - y if ky else y, 1 - c if kc else c)
            cp = pltpu.make_async_remote_copy(src_ref=v_ref, dst_ref=buf.at[me], send_sem=send_sems.at[k - 1],
                                              recv_sem=recv_sems.at[k - 1], device_id=peer, device_id_type=MESH)
            cp.start()
            copies.append(cp)
        for cp in copies:
            cp.wait_recv()
        for cp in copies:
            cp.wait_send()
        if reduce:
            acc = buf[0]
            for d in range(1, N_DEV):
                acc = acc + buf[d]
            o_ref[...] = acc

    sems = [pltpu.SemaphoreType.DMA((N_DEV - 1,)), pltpu.SemaphoreType.DMA((N_DEV - 1,))]
    vm = pl.BlockSpec(memory_space=pltpu.VMEM)
    if reduce:
        return pl.pallas_call(body, name=name, out_shape=jax.ShapeDtypeStruct((R, C), F32), in_specs=[vm], out_specs=vm,
                              scratch_shapes=[pltpu.VMEM((N_DEV, R, C), F32)] + sems)(v)
    return pl.pallas_call(body, name=name, out_shape=jax.ShapeDtypeStruct((N_DEV, R, C), F32), in_specs=[vm], out_specs=vm,
                          scratch_shapes=sems)(v)


def _local_step(x, mem, target, norms, conv_w, depth, n_a, get_w, next_ride, ride_done, put_g, done_g):
    S, D = x.shape
    main = D - MEM_WIDTH
    heads = main // HEAD_DIM
    row = lambda v: v.reshape(1, D)

    mem_n = _rmsnorm(mem, row(norms["mem_norm"]), name="mem_norm")
    saved = []
    kv = hk = x_kv = w_kv = None
    def carry_mm(*args, **kwargs):
        ride = next_ride()
        if ride is None:
            return _mm(*args, **kwargs)
        out, landed = _mm(*args, ride=ride, **kwargs)
        ride_done(landed)
        return out

    for i in range(depth):
        W = functools.partial(get_w, i)
        st = {"x": x}
        h = _rmsnorm(x, row(norms["mix_norm"][i]), name=f"mix_norm{i}")
        mkv = _mm(mem_n, W("mkv"), "nn", BF16, name=f"mkv{i}")
        if i < n_a:
            p = carry_mm(h, W("a"), "nt", BF16, name=f"a_in{i}")
            y_main = _conv_fwd(p, conv_w[i], main, name=f"conv{i}")
            qblk = 3 * main // MEM_WIDTH
        else:
            p = carry_mm(h, W("b"), "nn", BF16, name=f"b_in{i}")
            y_main, st["o32"] = _sb_fwd(p, kv, heads, name=f"sb{i}")
            qblk = main // MEM_WIDTH
        y = _memattn_fwd(p, qblk, mkv, y_main, name=f"memattn{i}")
        xm = carry_mm(y, W("o"), "nn", F32, residual=x, name=f"w_o{i}")
        h2 = _rmsnorm(xm, row(norms["ffn_norm"][i]), name=f"ffn_norm{i}")
        (act, silu, uds), landed = _ffn_up(h2, W("g"), W("u"), name=f"ffn_up{i}", ride=next_ride())
        ride_done(landed)
        x = carry_mm(act, W("d"), "nn", F32, residual=xm, name=f"w_down{i}")
        st.update(h=h, mkv=mkv, p=p, qblk=qblk, y=y, xm=xm, h2=h2, silu=silu, uds=uds, act=act)
        saved.append(st)
        if i == n_a - 1:
            x_kv, w_kv = x, W("kv")
            hk = _rmsnorm(x, row(norms["kv_norm"]), name="kv_norm")
            kv = _mm(hk, w_kv, "nt", BF16, name="w_kv")

    dx, dxb, dg_final, loss = _loss_head(x, row(norms["final_norm"]), target, name="loss_head")

    dg_mix, dg_ffn, dconv = [None] * depth, [None] * depth, [None] * n_a
    dmem_n = dk = dv = dg_kv = g_kv = None
    for i in reversed(range(depth)):
        st = saved[i]
        W, g = functools.partial(get_w, i), {}
        dgate, dup = _ffn_down_bwd(dxb, W("d"), st["silu"], st["uds"], name=f"ffn_down_bwd{i}")
        g["d"] = _mm(st["act"], dxb, "tn", BF16, name=f"g_w_down{i}")
        g["g"] = _mm(dgate, st["h2"], "tn", BF16, name=f"g_w_gate{i}")
        g["u"] = _mm(dup, st["h2"], "tn", BF16, name=f"g_w_up{i}")
        keys = dict(_layer_keys(i, n_a))
        ride = put_g([(k, keys[k]) for k in ("d", "g", "u")], g)
        (dx, dxb, dg_ffn[i]), landed = _mm_norm_bwd([(dgate, W("g")), (dup, W("u"))], "nn", st["xm"],
                                                    row(norms["ffn_norm"][i]), dx, name=f"d_h2_{i}", ride=ride)
        done_g(landed)
        g = {}
        dy = _mm(dxb, W("o"), "nt", BF16, name=f"d_y{i}")
        g["o"] = _mm(st["y"], dxb, "tn", BF16, name=f"g_w_o{i}")
        dqmem, dmk, dmv = _memattn_bwd(st["p"], st["qblk"], st["mkv"], dy, main // MEM_WIDTH, name=f"memattn_bwd{i}")
        dmkv = jnp.concatenate([dmk, dmv], axis=1)
        g["mkv"] = _mm(mem_n, dmkv, "tn", BF16, name=f"g_w_mem_kv{i}")
        dmem_n = _mm(dmkv, W("mkv"), "nt", F32, residual=dmem_n, name=f"d_mem_n{i}")
        if i < n_a:
            db, dc, du, dconv[i] = _conv_bwd(st["p"], conv_w[i], dy, main, name=f"conv_bwd{i}")
            dp = jnp.concatenate([db, dc, du, dqmem], axis=1)
            g["a"] = _mm(dp, st["h"], "tn", BF16, name=f"g_a_in{i}")
            w_in, form = W("a"), "nn"
        else:
            dq, dk, dv = _sb_bwd(st["p"], kv, st["o32"], dy, heads, dk, dv, name=f"sb_bwd{i}")
            dp = jnp.concatenate([dq, dqmem], axis=1)
            g["b"] = _mm(st["h"], dp, "tn", BF16, name=f"g_b_in{i}")
            w_in, form = W("b"), "nt"
        if i == n_a - 1:
            g["kv"] = g_kv
        ride = put_g([kl for kl in _layer_keys(i, n_a) if kl[0] in g], g)
        (dx, dxb, dg_mix[i]), landed = _mm_norm_bwd([(dp, w_in)], form, st["x"], row(norms["mix_norm"][i]), dx,
                                                    name=f"d_h{i}", ride=ride)
        done_g(landed)
        if i == n_a:
            dkv = jnp.concatenate([dk, dv], axis=1).astype(BF16)
            g_kv = _mm(dkv, hk, "tn", BF16, name="g_w_kv")
            (dx, dxb, dg_kv), _ = _mm_norm_bwd([(dkv, w_kv)], "nn", x_kv, row(norms["kv_norm"]), dx, name="d_hk")
    _, _, dg_mem = _rmsnorm_bwd(mem, row(norms["mem_norm"]), dmem_n, None, name="mem_norm_bwd")

    small = {"mix_norm": jnp.concatenate(dg_mix, axis=0), "ffn_norm": jnp.concatenate(dg_ffn, axis=0), "kv_norm": dg_kv[0],
             "mem_norm": dg_mem[0], "final_norm": dg_final[0], "conv_w": jnp.stack(dconv, axis=0)}
    return loss, dx, small


_COL_SHARDED = ("a", "kv", "g", "u")
_NAMES = {"a": "a_in", "kv": "w_kv_shared", "g": "w_gate", "u": "w_up", "b": "b_in", "o": "w_o", "d": "w_down", "mkv": "w_mem_kv"}
_ORDER = ("a", "kv", "g", "u", "d", "b", "o", "mkv")
_WEIGHTS = ("mix_norm", "a_in", "conv_w", "b_in", "kv_norm", "w_kv_shared", "w_mem_kv", "w_o", "ffn_norm", "w_gate", "w_up",
            "w_down", "mem_norm", "final_norm")


def _layer_keys(i, n_a):
    keys = [("a", i) if i < n_a else ("b", i - n_a), ("g", i), ("u", i), ("d", i), ("o", i), ("mkv", i)]
    return keys + [("kv", 0)] if i == n_a - 1 else keys


def _gather_groups(i, n_a):
    first, rest = _layer_keys(i, n_a)[0], dict(_layer_keys(i, n_a)[1:])
    small = [(k, rest[k]) for k in ("o", "mkv", "kv") if k in rest]
    return [[first], small, [("g", rest["g"]), ("u", rest["u"])], [("d", rest["d"])]]


def _canonical(key, w):
    w3 = w if w.ndim == 3 else w[None]
    if key in _COL_SHARDED:
        w3 = jnp.transpose(w3, (0, 2, 1))
    return w3


def _uncanonical(key, g3, like):
    if key in _COL_SHARDED:
        g3 = jnp.transpose(g3, (0, 2, 1))
    return g3.reshape(like.shape)


def _pad_rows(flat, C):
    n = flat.shape[0]
    rows = -(-n // C)
    return jnp.pad(flat, (0, rows * C - n)).reshape(rows, C)


def kernel(x, mem, mix_norm, a_in, conv_w, b_in, kv_norm, w_kv_shared, w_mem_kv, w_o, ffn_norm, w_gate, w_up, w_down, mem_norm, final_norm, loss_target, m_mix_norm, m_a_in, m_conv_w, m_b_in, m_kv_norm, m_w_kv_shared, m_w_mem_kv, m_w_o, m_ffn_norm, m_w_gate, m_w_up, m_w_down, m_mem_norm, m_final_norm, v_mix_norm, v_a_in, v_conv_w, v_b_in, v_kv_norm, v_w_kv_shared, v_w_mem_kv, v_w_o, v_ffn_norm, v_w_gate, v_w_up, v_w_down, v_mem_norm, v_final_norm):
    weights = dict(mix_norm=mix_norm, a_in=a_in, conv_w=conv_w, b_in=b_in, kv_norm=kv_norm, w_kv_shared=w_kv_shared,
                   w_mem_kv=w_mem_kv, w_o=w_o, ffn_norm=ffn_norm, w_gate=w_gate, w_up=w_up, w_down=w_down,
                   mem_norm=mem_norm, final_norm=final_norm)
    moments_m = dict(mix_norm=m_mix_norm, a_in=m_a_in, conv_w=m_conv_w, b_in=m_b_in, kv_norm=m_kv_norm,
                     w_kv_shared=m_w_kv_shared, w_mem_kv=m_w_mem_kv, w_o=m_w_o, ffn_norm=m_ffn_norm, w_gate=m_w_gate,
                     w_up=m_w_up, w_down=m_w_down, mem_norm=m_mem_norm, final_norm=m_final_norm)
    moments_v = dict(mix_norm=v_mix_norm, a_in=v_a_in, conv_w=v_conv_w, b_in=v_b_in, kv_norm=v_kv_norm,
                     w_kv_shared=v_w_kv_shared, w_mem_kv=v_w_mem_kv, w_o=v_w_o, ffn_norm=v_ffn_norm, w_gate=v_w_gate,
                     w_up=v_w_up, w_down=v_w_down, mem_norm=v_mem_norm, final_norm=v_final_norm)
    D = x.shape[-1]
    depth, n_a = w_o.shape[0], a_in.shape[0]
    xi, yi, ci = _place()
    me = 4 * xi + 2 * yi + ci
    core = ci.reshape(1).astype(jnp.int32)
    chip = (2 * xi + yi).reshape(1).astype(jnp.int32)

    cw_shape = conv_w.shape
    cw_rows = _pad_rows(conv_w.reshape(-1), D)
    cw_rows = jnp.pad(cw_rows, ((0, 8 - cw_rows.shape[0]), (0, 0)))
    cw_gathered = _exchange(cw_rows, False, name="gather_conv_w")
    n_cw = cw_shape[0] * cw_shape[1] * cw_shape[2]
    cw_all = cw_gathered.reshape(N_DEV, -1)[:, :n_cw].reshape((N_DEV,) + cw_shape)
    conv_full = jnp.transpose(cw_all, (1, 2, 0, 3)).reshape(cw_shape[0], cw_shape[1], N_DEV * cw_shape[2])

    shard3 = {k: _canonical(k, weights[_NAMES[k]]).astype(BF16) for k in _ORDER}
    keys0 = _layer_keys(0, n_a)
    fulls0 = _all_gather_weights([shard3[k][l][None] for k, l in keys0], name="all_gather_layer0")
    full = {kl: f[0] for kl, f in zip(keys0, fulls0)}

    groups = [grp for i in range(1, depth) for grp in _gather_groups(i, n_a)]
    carried, riding = [0], []

    def next_ride():
        n = carried[0]
        carried[0] += 1
        second = groups[n - 1] if 1 <= n <= len(groups) else []
        first = groups[n] if n < len(groups) else []
        if not second + first:
            return None
        riding.append(second + first)
        return _join_rides([_gather_ride_2([full[kl] for kl in second]) if second else None,
                            _gather_ride_1([shard3[k][l] for k, l in first]) if first else None])

    def ride_done(landed):
        if landed:
            full.update(zip(riding.pop(), landed))

    def get_w(i, key):
        return full[(key, dict(_layer_keys(i, n_a))[key])]

    batches = []

    def put_g(keys, g):
        g5 = [g[k].reshape(1, 4, 2, g[k].shape[0] // N_DEV, g[k].shape[1]) for k, _ in keys]
        from_sibling = _rs_sibling(g5, name=f"rs_sibling{len(batches)}")
        sums = [_chip_sum(a, s, core, name=f"chip_sum_{k}{l}") for (k, l), a, s in zip(keys, g5, from_sibling)]
        batches.append([keys, sums, None])
        return _chips_ride(sums)

    def done_g(landed):
        batches[-1][2] = landed

    norms = {k: weights[k] for k in ("mix_norm", "ffn_norm", "kv_norm", "mem_norm", "final_norm")}
    loss, grad_x, small = _local_step(x[0], mem[0], loss_target[0], norms, conv_full, depth, n_a, get_w, next_ride, ride_done,
                                      put_g, done_g)

    shard_grads = {(k, l): _final_sum(s, land, chip, name=f"final_sum_{k}{l}")[0]
                   for keys, sums, landed in batches for (k, l), s, land in zip(keys, sums, landed)}
    grads = {}
    for k in _ORDER:
        g3 = jnp.stack([shard_grads[(k, l)] for l in range(shard3[k].shape[0])])
        grads[_NAMES[k]] = _uncanonical(k, g3, weights[_NAMES[k]])

    order = ("mix_norm", "ffn_norm", "kv_norm", "mem_norm", "final_norm", "conv_w")
    flat = jnp.concatenate([small[k].reshape(-1) for k in order] + [loss[0, :1]])
    n_flat = flat.shape[0]
    rows = _pad_rows(flat, D)
    rows = jnp.pad(rows, ((0, (-rows.shape[0]) % 8), (0, 0)))
    total = _exchange(rows, True, name="all_reduce_small").reshape(-1)[:n_flat]
    off = 0
    for k in order:
        n = small[k].size
        grads[k] = total[off:off + n].reshape(small[k].shape)
        off += n
    loss_total = total[off]
    grads["conv_w"] = lax.dynamic_slice_in_dim(grads["conv_w"], me * cw_shape[2], cw_shape[2], axis=2)

    deltas, new_m, new_v = {}, {}, {}
    for k in _WEIGHTS:
        w = weights[k]
        two = (lambda a: a.reshape(-1, a.shape[-1])) if w.ndim > 1 else (lambda a: a.reshape(1, -1))
        d, nm, nv = _adamw(two(w), two(grads[k]), two(moments_m[k]), two(moments_v[k]), name=f"adamw_{k}")
        deltas[k], new_m[k], new_v[k] = d.reshape(w.shape), nm.reshape(w.shape), nv.reshape(w.shape)

    return (loss_total, grad_x[None], *[grads[k] for k in _WEIGHTS], *[deltas[k] for k in _WEIGHTS],
            *[new_m[k] for k in _WEIGHTS], *[new_v[k] for k in _WEIGHTS])
```

```python
import functools
import math

import jax
import jax.numpy as jnp
from jax import lax
from jax.experimental import pallas as pl
from jax.experimental.pallas import tpu as pltpu

F32 = jnp.float32
BF16 = jnp.bfloat16
MESH = pl.DeviceIdType.MESH

HEAD_DIM = 64
MEM_HEADS = 4
MEM_WIDTH = MEM_HEADS * HEAD_DIM
EPS = 1e-6
LANES = 128
BF16_ROWS = 16
VMEM_LIMIT = 56 * 1024 * 1024
N_DEV = 8

ADAM_LR = 0.001
ADAM_B1 = 0.9
ADAM_B2 = 0.999
ADAM_EPS = 1e-08
ADAM_WD = 0.01
ADAM_STEP = 10

ANY = pl.BlockSpec(memory_space=pl.ANY)


def _cparams(sem=None):
    return pltpu.CompilerParams(dimension_semantics=sem, vmem_limit_bytes=VMEM_LIMIT)


def _pick(n, cands):
    for c in cands:
        if n % c == 0:
            return c
    raise ValueError(f"no tile for {n} in {cands}")


def _mm(a, b, form, out_dtype, *, name, residual=None, ride=None):
    if form == "tn":
        K, M = a.shape
    else:
        M, K = a.shape
    if form == "nt":
        N, K2 = b.shape
    else:
        K2, N = b.shape
    assert K == K2, (name, a.shape, b.shape)
    wide = (1408, 1280, 1024, 768, 512, 256, 128)
    tm = _pick(M, wide if form == "tn" else (1024, 512, 256, 128))
    tn = _pick(N, wide)
    tk = _pick(K, (1024, 1408, 1280, 768, 512, 256))
    nk = K // tk
    dims = {"nn": (((1,), (0,)), ((), ())), "nt": (((1,), (1,)), ((), ())), "tn": (((0,), (0,)), ((), ()))}[form]
    a_bytes, b_bytes = M * K * a.dtype.itemsize, N * K * b.dtype.itemsize
    n_outer = nk == 1 and (N // tn) * a_bytes + b_bytes < a_bytes + (M // tm) * b_bytes
    ij = (lambda g0, g1: (g1, g0)) if n_outer else (lambda g0, g1: (g0, g1))

    def spec(block, f):
        return pl.BlockSpec(block, lambda g0, g1, k: f(*ij(g0, g1), k))

    a_spec = spec((tk, tm), lambda i, j, k: (k, i)) if form == "tn" else spec((tm, tk), lambda i, j, k: (i, k))
    b_spec = spec((tn, tk), lambda i, j, k: (j, k)) if form == "nt" else spec((tk, tn), lambda i, j, k: (k, j))
    out_spec = spec((tm, tn), lambda i, j, k: (i, j))
    operands, in_specs = [a, b], [a_spec, b_spec]
    has_res = residual is not None
    if has_res:
        operands.append(residual)
        in_specs.append(out_spec)
    grid = (N // tn, M // tm, nk) if n_outer else (M // tm, N // tn, nk)

    def body(*refs):
        a_ref, b_ref = refs[0], refs[1]
        r_ref = refs[2] if has_res else None
        o_ref = refs[2 + int(has_res)]
        acc_ref = refs[-1]
        part = lax.dot_general(a_ref[...].astype(BF16), b_ref[...].astype(BF16), dims, preferred_element_type=F32)

        def finish(total):
            if has_res:
                total = total + r_ref[...].astype(F32)
            o_ref[...] = total.astype(out_dtype)

        if nk == 1:
            finish(part)
        else:
            k = pl.program_id(2)

            @pl.when(k == 0)
            def _():
                acc_ref[...] = part

            @pl.when(jnp.logical_and(k > 0, k < nk - 1))
            def _():
                acc_ref[...] += part

            @pl.when(k == nk - 1)
            def _():
                finish(acc_ref[...] + part)

    def edges():
        ids = [pl.program_id(d) for d in range(3)]
        first = jnp.logical_and(jnp.logical_and(ids[0] == 0, ids[1] == 0), ids[2] == 0)
        last = jnp.logical_and(jnp.logical_and(ids[0] == grid[0] - 1, ids[1] == grid[1] - 1), ids[2] == grid[2] - 1)
        return first, last

    (out,), landed = _ride_call(
        body, ride, edges, name=name, out_shape=[jax.ShapeDtypeStruct((M, N), out_dtype)], grid=grid, in_specs=in_specs,
        out_specs=[out_spec], scratch_shapes=[pltpu.VMEM((tm, tn), F32)],
        compiler_params=_cparams(("parallel", "parallel", "arbitrary")), operands=operands)
    return out if ride is None else (out, landed)


def _rmsnorm(x, g, *, name):
    R, D = x.shape
    tr = _pick(R, (512, 256))

    def body(x_ref, g_ref, o_ref):
        xv = x_ref[...]
        r = lax.rsqrt(jnp.mean(xv * xv, axis=-1, keepdims=True) + EPS)
        o_ref[...] = (xv * r * g_ref[...]).astype(BF16)

    return pl.pallas_call(
        body, name=name, out_shape=jax.ShapeDtypeStruct((R, D), BF16), grid=(R // tr,),
        in_specs=[pl.BlockSpec((tr, D), lambda i: (i, 0)), pl.BlockSpec((1, D), lambda i: (0, 0))],
        out_specs=pl.BlockSpec((tr, D), lambda i: (i, 0)), compiler_params=_cparams(("parallel",)),
    )(x, g)


def _rmsnorm_bwd(x, g, dh, dx_in, *, name):
    R, D = x.shape
    tr = _pick(R, (512, 256))
    has_in = dx_in is not None

    def body(*refs):
        x_ref, g_ref, dh_ref = refs[:3]
        dxi_ref = refs[3] if has_in else None
        dx_ref, dxb_ref, dg_ref = refs[3 + int(has_in):]
        xv = x_ref[...]
        r = lax.rsqrt(jnp.mean(xv * xv, axis=-1, keepdims=True) + EPS)
        xhat = xv * r
        dhv = dh_ref[...].astype(F32)
        dxh = dhv * g_ref[...]
        dx = r * (dxh - xhat * jnp.mean(dxh * xhat, axis=-1, keepdims=True))
        if has_in:
            dx = dx + dxi_ref[...]
        dx_ref[...] = dx
        dxb_ref[...] = dx.astype(BF16)
        part = jnp.sum(dhv * xhat, axis=0, keepdims=True)

        @pl.when(pl.program_id(0) == 0)
        def _():
            dg_ref[...] = part

        @pl.when(pl.program_id(0) > 0)
        def _():
            dg_ref[...] += part

    row = pl.BlockSpec((tr, D), lambda i: (i, 0))
    vec = pl.BlockSpec((1, D), lambda i: (0, 0))
    ops = [x, g, dh] + ([dx_in] if has_in else [])
    return pl.pallas_call(
        body, name=name,
        out_shape=(jax.ShapeDtypeStruct((R, D), F32), jax.ShapeDtypeStruct((R, D), BF16), jax.ShapeDtypeStruct((1, D), F32)),
        grid=(R // tr,), in_specs=[row, vec, row] + ([row] if has_in else []), out_specs=(row, row, vec),
        compiler_params=_cparams(("arbitrary",)),
    )(*ops)


class _Ride:
    def __init__(self, srcs, dsts, start, wait, sems):
        self.srcs, self.dsts, self.start, self.wait, self.sems = list(srcs), list(dsts), start, wait, list(sems)


def _ride_call(body, ride, edges, *, name, out_shape, grid, in_specs, out_specs, scratch_shapes, compiler_params, operands):
    n_in, n_out, n_scr = len(in_specs), len(out_specs), len(scratch_shapes)
    if ride is None:
        outs = pl.pallas_call(body, name=name, out_shape=tuple(out_shape), grid=grid, in_specs=list(in_specs),
                              out_specs=tuple(out_specs), scratch_shapes=list(scratch_shapes),
                              compiler_params=compiler_params)(*operands)
        return tuple(outs), []
    ns, nd = len(ride.srcs), len(ride.dsts)

    def riding(*refs):
        ins, rin = refs[:n_in], refs[n_in:n_in + ns + nd]
        outs = refs[n_in + ns + nd:n_in + ns + nd + n_out]
        scratch = refs[n_in + ns + 2 * nd + n_out:]
        sems = scratch[n_scr:]
        first, last = edges()

        @pl.when(first)
        def _():
            ride.start(rin[:ns], rin[ns:], sems)

        body(*ins, *outs, *scratch[:n_scr])

        @pl.when(last)
        def _():
            ride.wait(rin[:ns], rin[ns:], sems)

    outs = pl.pallas_call(
        riding, name=name, out_shape=(*out_shape, *[jax.ShapeDtypeStruct(d.shape, d.dtype) for d in ride.dsts]), grid=grid,
        in_specs=[*in_specs, *[ANY] * (ns + nd)], out_specs=(*out_specs, *[ANY] * nd),
        scratch_shapes=[*scratch_shapes, *[pltpu.SemaphoreType.DMA(s) for s in ride.sems]],
        input_output_aliases={n_in + ns + d: n_out + d for d in range(nd)}, compiler_params=compiler_params,
    )(*operands, *ride.srcs, *ride.dsts)
    return tuple(outs[:n_out]), list(outs[n_out:])


def _mm_norm_bwd(parts, form, x, g, dx_in, *, name, ride=None):
    S, K = parts[0][0].shape
    D = x.shape[1]
    tm = _pick(S, (512, 256))
    tk = _pick(K, (1024, 1408, 1280, 768, 512, 256))
    nk, P = K // tk, len(parts)
    dims = _NN if form == "nn" else _NT

    ni, nsteps = S // tm, P * nk

    def body(*refs):
        ab = refs[:2 * P]
        x_ref, g_ref, dxi_ref, dx_ref, dxb_ref, dg_ref, acc_ref = refs[2 * P:]
        k, i = pl.program_id(0), pl.program_id(1)
        rows = pl.ds(pl.multiple_of(i * tm, tm), tm)

        @pl.when(k == 0)
        def _():
            acc_ref[rows, :] = jnp.zeros((tm, D), F32)

        for p in range(P):
            @pl.when(jnp.logical_and(k >= p * nk, k < (p + 1) * nk))
            def _():
                acc_ref[rows, :] += _dot(ab[2 * p][...], ab[2 * p + 1][...], dims)

        @pl.when(k == nsteps - 1)
        def _():
            xv = x_ref[...]
            r = lax.rsqrt(jnp.mean(xv * xv, axis=-1, keepdims=True) + EPS)
            xhat = xv * r
            dhv = acc_ref[rows, :]
            dxh = dhv * g_ref[...]
            dx = r * (dxh - xhat * jnp.mean(dxh * xhat, axis=-1, keepdims=True)) + dxi_ref[...]
            dx_ref[...] = dx
            dxb_ref[...] = dx.astype(BF16)
            part = jnp.sum(dhv * xhat, axis=0, keepdims=True)

            @pl.when(i == 0)
            def _():
                dg_ref[...] = part

            @pl.when(i > 0)
            def _():
                dg_ref[...] += part

    def kk(p):
        return lambda k: jnp.clip(k - p * nk, 0, nk - 1)

    def active_row(p):
        return lambda k, i: jnp.where(jnp.logical_and(k >= p * nk, k < (p + 1) * nk), i, 0)

    in_specs, operands = [], []
    for p, (a, b) in enumerate(parts):
        in_specs.append(pl.BlockSpec((tm, tk), lambda k, i, f=kk(p), r=active_row(p): (r(k, i), f(k))))
        in_specs.append(pl.BlockSpec((tk, D), lambda k, i, f=kk(p): (f(k), 0)) if form == "nn"
                        else pl.BlockSpec((D, tk), lambda k, i, f=kk(p): (0, f(k))))
        operands += [a, b]
    row = pl.BlockSpec((tm, D), lambda k, i: (jnp.where(k == nsteps - 1, i, 0), 0))
    vec = pl.BlockSpec((1, D), lambda k, i: (0, 0))

    def edges():
        k, i = pl.program_id(0), pl.program_id(1)
        return jnp.logical_and(i == 0, k == 0), jnp.logical_and(i == ni - 1, k == nsteps - 1)

    return _ride_call(
        body, ride, edges, name=name,
        out_shape=(jax.ShapeDtypeStruct((S, D), F32), jax.ShapeDtypeStruct((S, D), BF16), jax.ShapeDtypeStruct((1, D), F32)),
        grid=(nsteps, ni), in_specs=in_specs + [row, vec, row], out_specs=(row, row, vec),
        scratch_shapes=[pltpu.VMEM((S, D), F32)], compiler_params=_cparams(("arbitrary", "arbitrary")),
        operands=[*operands, x, g, dx_in])


def _loss_head(x, g, target, *, name):
    R, D = x.shape
    tr = _pick(R, (512, 256))

    def body(x_ref, g_ref, t_ref, dx_ref, dxb_ref, dg_ref, loss_ref):
        xv = x_ref[...]
        gv = g_ref[...]
        r = lax.rsqrt(jnp.mean(xv * xv, axis=-1, keepdims=True) + EPS)
        xhat = xv * r
        err = xhat * gv - t_ref[...]
        loss = 0.5 * jnp.sum(jnp.mean(err * err, axis=-1, keepdims=True), axis=0, keepdims=True)
        dy = err * (1.0 / D)
        dxh = dy * gv
        dx = r * (dxh - xhat * jnp.mean(dxh * xhat, axis=-1, keepdims=True))
        dx_ref[...] = dx
        dxb_ref[...] = dx.astype(BF16)
        dg = jnp.sum(dy * xhat, axis=0, keepdims=True)
        lossv = jnp.broadcast_to(loss, (1, LANES))

        @pl.when(pl.program_id(0) == 0)
        def _():
            dg_ref[...] = dg
            loss_ref[...] = lossv

        @pl.when(pl.program_id(0) > 0)
        def _():
            dg_ref[...] += dg
            loss_ref[...] += lossv

    row = pl.BlockSpec((tr, D), lambda i: (i, 0))
    vec = pl.BlockSpec((1, D), lambda i: (0, 0))
    return pl.pallas_call(
        body, name=name,
        out_shape=(jax.ShapeDtypeStruct((R, D), F32), jax.ShapeDtypeStruct((R, D), BF16), jax.ShapeDtypeStruct((1, D), F32),
                   jax.ShapeDtypeStruct((1, LANES), F32)),
        grid=(R // tr,), in_specs=[row, vec, row], out_specs=(row, row, vec, pl.BlockSpec((1, LANES), lambda i: (0, 0))),
        compiler_params=_cparams(("arbitrary",)),
    )(x, g, target)


def _conv_taps(gv, S):
    t = lax.broadcasted_iota(jnp.int32, gv.shape, 0)
    g1 = jnp.where(t >= 1, pltpu.roll(gv, 1, 0), 0.0)
    g2 = jnp.where(t >= 2, pltpu.roll(gv, 2, 0), 0.0)
    return g1, g2


def _conv_fwd(p, w, main, *, name):
    S = p.shape[0]
    tc = LANES
    nb = main // tc

    def body(b_ref, c_ref, u_ref, w_ref, y_ref):
        gv = c_ref[...].astype(F32) * u_ref[...].astype(F32)
        g1, g2 = _conv_taps(gv, S)
        cv = w_ref[0:1, :] * g2 + w_ref[1:2, :] * g1 + w_ref[2:3, :] * gv
        y_ref[...] = (b_ref[...].astype(F32) * cv).astype(BF16)

    col = lambda off: pl.BlockSpec((S, tc), lambda j: (0, off + j))
    return pl.pallas_call(
        body, name=name, out_shape=jax.ShapeDtypeStruct((S, p.shape[1] - 2 * main), BF16), grid=(nb,),
        in_specs=[col(0), col(nb), col(2 * nb), pl.BlockSpec((3, tc), lambda j: (0, j))],
        out_specs=pl.BlockSpec((S, tc), lambda j: (0, j)), compiler_params=_cparams(("parallel",)),
    )(p, p, p, w)


def _conv_bwd(p, w, dy, main, *, name):
    S = p.shape[0]
    tc = LANES
    nb = main // tc

    def body(b_ref, c_ref, u_ref, w_ref, dy_ref, db_ref, dc_ref, du_ref, dw_ref):
        cvv, uv = c_ref[...].astype(F32), u_ref[...].astype(F32)
        gv = cvv * uv
        g1, g2 = _conv_taps(gv, S)
        w0, w1, w2 = w_ref[0:1, :], w_ref[1:2, :], w_ref[2:3, :]
        dyv = dy_ref[...].astype(F32)
        db_ref[...] = (dyv * (w0 * g2 + w1 * g1 + w2 * gv)).astype(BF16)
        dcv = dyv * b_ref[...].astype(F32)
        t = lax.broadcasted_iota(jnp.int32, dcv.shape, 0)
        n1 = jnp.where(t <= S - 2, pltpu.roll(dcv, S - 1, 0), 0.0)
        n2 = jnp.where(t <= S - 3, pltpu.roll(dcv, S - 2, 0), 0.0)
        dg = w2 * dcv + w1 * n1 + w0 * n2
        dc_ref[...] = (dg * uv).astype(BF16)
        du_ref[...] = (dg * cvv).astype(BF16)
        dw_ref[0:1, :] = jnp.sum(dcv * g2, axis=0, keepdims=True)
        dw_ref[1:2, :] = jnp.sum(dcv * g1, axis=0, keepdims=True)
        dw_ref[2:3, :] = jnp.sum(dcv * gv, axis=0, keepdims=True)

    col = lambda off: pl.BlockSpec((S, tc), lambda j: (0, off + j))
    out = jax.ShapeDtypeStruct((S, main), BF16)
    return pl.pallas_call(
        body, name=name, out_shape=(out, out, out, jax.ShapeDtypeStruct((3, main), F32)), grid=(nb,),
        in_specs=[col(0), col(nb), col(2 * nb), pl.BlockSpec((3, tc), lambda j: (0, j)), col(0)],
        out_specs=(col(0), col(0), col(0), pl.BlockSpec((3, tc), lambda j: (0, j))),
        compiler_params=_cparams(("parallel",)),
    )(p, p, p, w, dy)


def _head_mask(width, h):
    lane = lax.broadcasted_iota(jnp.int32, (1, width), 1)
    return jnp.logical_and(lane >= h * HEAD_DIM, lane < (h + 1) * HEAD_DIM)


_NT = (((1,), (1,)), ((), ()))
_NN = (((1,), (0,)), ((), ()))
_TN = (((0,), (0,)), ((), ()))


def _dot(a, b, dims):
    return lax.dot_general(a, b, dims, preferred_element_type=F32)


def _mem_probs(qh, kv):
    s = _dot(qh, kv, _NT) * (1.0 / math.sqrt(HEAD_DIM))
    s = s - jnp.max(s, axis=-1, keepdims=True)
    e = jnp.exp(s)
    return e / jnp.sum(e, axis=-1, keepdims=True)


def _memattn_fwd(p, qblk, mkv, into, *, name):
    S = p.shape[0]
    M = mkv.shape[0]
    W = MEM_WIDTH
    tq = _pick(S, (512, 256))
    last = into.shape[1] // W - 1

    def body(q_ref, k_ref, v_ref, _, o_ref):
        q = q_ref[...].astype(BF16)
        kv, vv = k_ref[...], v_ref[...]
        out = jnp.zeros((tq, W), F32)
        for h in range(MEM_HEADS):
            m = _head_mask(W, h)
            pr = _mem_probs(jnp.where(m, q, jnp.zeros_like(q)), kv)
            out = jnp.where(m, _dot(pr.astype(BF16), vv, _NN), out)
        o_ref[...] = out.astype(BF16)

    return pl.pallas_call(
        body, name=name, out_shape=jax.ShapeDtypeStruct(into.shape, BF16), grid=(S // tq,),
        in_specs=[pl.BlockSpec((tq, W), lambda i: (i, qblk)), pl.BlockSpec((M, W), lambda i: (0, 0)),
                  pl.BlockSpec((M, W), lambda i: (0, 1)), ANY],
        out_specs=pl.BlockSpec((tq, W), lambda i: (i, last)), input_output_aliases={3: 0},
        compiler_params=_cparams(("parallel",)),
    )(p, mkv, mkv, into)


def _memattn_bwd(p, qblk, mkv, dy, dyblk, *, name):
    S = p.shape[0]
    M = mkv.shape[0]
    W = MEM_WIDTH
    tq = _pick(S, (512, 256))
    scale = 1.0 / math.sqrt(HEAD_DIM)

    def body(q_ref, k_ref, v_ref, do_ref, dq_ref, dk_ref, dv_ref, dk_acc, dv_acc):
        q = q_ref[...].astype(BF16)
        do = do_ref[...].astype(BF16)
        kv, vv = k_ref[...], v_ref[...]
        dq = jnp.zeros((tq, W), F32)
        dk = jnp.zeros((M, W), F32)
        dv = jnp.zeros((M, W), F32)
        for h in range(MEM_HEADS):
            m = _head_mask(W, h)
            qh = jnp.where(m, q, jnp.zeros_like(q))
            doh = jnp.where(m, do, jnp.zeros_like(do))
            pr = _mem_probs(qh, kv)
            dpr = _dot(doh, vv, _NT)
            ds = (pr * (dpr - jnp.sum(dpr * pr, axis=-1, keepdims=True)) * scale).astype(BF16)
            dq = jnp.where(m, _dot(ds, kv, _NN), dq)
            dk = dk + _dot(ds, qh, _TN)
            dv = dv + _dot(pr.astype(BF16), doh, _TN)
        dq_ref[...] = dq.astype(BF16)
        i = pl.program_id(0)

        @pl.when(i == 0)
        def _():
            dk_acc[...] = dk
            dv_acc[...] = dv

        @pl.when(i > 0)
        def _():
            dk_acc[...] += dk
            dv_acc[...] += dv

        @pl.when(i == S // tq - 1)
        def _():
            dk_ref[...] = dk_acc[...].astype(BF16)
            dv_ref[...] = dv_acc[...].astype(BF16)

    kspec = lambda c: pl.BlockSpec((M, W), lambda i: (0, c))
    return pl.pallas_call(
        body, name=name,
        out_shape=(jax.ShapeDtypeStruct((S, W), BF16), jax.ShapeDtypeStruct((M, W), BF16), jax.ShapeDtypeStruct((M, W), BF16)),
        grid=(S // tq,),
        in_specs=[pl.BlockSpec((tq, W), lambda i: (i, qblk)), kspec(0), kspec(1), pl.BlockSpec((tq, W), lambda i: (i, dyblk))],
        out_specs=(pl.BlockSpec((tq, W), lambda i: (i, 0)), kspec(0), kspec(0)),
        scratch_shapes=[pltpu.VMEM((M, W), F32), pltpu.VMEM((M, W), F32)],
        compiler_params=_cparams(("arbitrary",)),
    )(p, mkv, mkv, dy)


SB_TQ = 256
SB_CLAMP = 80.0
SB_DEAD = 110.0


SB_CHUNK = 64


def _by_rows(fn, *arrays):
    rows = next(a for a in arrays if a is not None).shape[0]
    outs = [fn(*[None if a is None else a[r0:r0 + SB_CHUNK] for a in arrays]) for r0 in range(0, rows, SB_CHUNK)]
    return tuple(jnp.concatenate(col, axis=0) for col in zip(*outs))


def _sb_scores(qh, kb, causal):
    def chain(z, mask):
        z = jnp.clip(z, -SB_CLAMP, SB_CLAMP)
        w = 1.0 + jnp.exp(z)
        sp = jnp.log(w)
        zs = z - sp
        if mask is not None:
            sp = jnp.where(mask, sp, 0.0)
            zs = jnp.where(mask, zs, -1e30)
            w = jnp.where(mask, w, 1.0)
        return zs, sp.astype(BF16), jnp.sum(sp, axis=1, keepdims=True), w

    return _by_rows(chain, _dot(qh, kb, _NT), causal)


def _sb_weights(zs, spb, tri, carry):
    return _by_rows(lambda zs_c, t_c, c_c: (jnp.exp(zs_c - (t_c + c_c)).astype(BF16),), zs, _dot(spb, tri, _NN), carry)[0]


def _sb_live(carry):
    return jnp.min(carry) <= SB_DEAD


def _stack_heads(v, m0):
    zero = jnp.zeros_like(v)
    return jnp.concatenate([jnp.where(m0, v, zero), jnp.where(m0, zero, v)], axis=0)


def _stacked_causal(tq):
    r = lax.broadcasted_iota(jnp.int32, (2 * tq, tq), 0)
    c = lax.broadcasted_iota(jnp.int32, (2 * tq, tq), 1)
    return c < jnp.where(r >= tq, r - tq, r)


def _sb_fwd(p, kv, heads, *, name):
    S = p.shape[0]
    tq = SB_TQ
    npair = heads // 2

    def body(q_ref, k_ref, v_ref, o_ref, o32_ref):
        qi = pl.program_id(1)
        r = lax.broadcasted_iota(jnp.int32, (tq, tq), 0)
        c = lax.broadcasted_iota(jnp.int32, (tq, tq), 1)
        tri = (r > c).astype(BF16)
        causal = _stacked_causal(tq)
        m0 = _head_mask(LANES, 0)
        qh = _stack_heads(q_ref[...] * jnp.asarray(1.0 / math.sqrt(HEAD_DIM), BF16), m0)

        def block(j, carry, acc, mask):
            off = pl.multiple_of(j * tq, tq)
            kb = k_ref[pl.ds(off, tq), :]
            vb = v_ref[pl.ds(off, tq), :]
            zs, spb, sp_sum, _ = _sb_scores(qh, kb, mask)
            acc = acc + _dot(_sb_weights(zs, spb, tri, carry), vb, _NN)
            return carry + sp_sum, acc

        st = (jnp.zeros((2 * tq, 1), F32), jnp.zeros((2 * tq, LANES), F32))
        st = lax.cond(qi >= 1, lambda s: block(qi - 1, *block(qi, *s, causal), None), lambda s: block(qi, *s, causal), st)
        left = jnp.maximum(qi - 1, 0)
        odd = left % 2
        st = lax.cond(jnp.logical_and(odd == 1, _sb_live(st[0])), lambda s: block(qi - 2, *s, None), lambda s: s, st)

        def pair(s):
            it, _, carry, acc = s
            j = qi - 2 - odd - 2 * it
            carry, acc = block(j, carry, acc, None)
            carry, acc = block(j - 1, carry, acc, None)
            return it + 1, _sb_live(carry), carry, acc

        _, _, carry, acc = lax.while_loop(lambda s: jnp.logical_and(s[0] < left // 2, s[1]), pair,
                                          (jnp.int32(0), _sb_live(st[0]), st[0], st[1]))
        out = jnp.where(m0, acc[:tq], acc[tq:])
        o_ref[...] = out.astype(BF16)
        o32_ref[...] = out

    W = heads * HEAD_DIM
    qspec = pl.BlockSpec((tq, LANES), lambda hp, i: (i, hp))
    return pl.pallas_call(
        body, name=name, out_shape=(jax.ShapeDtypeStruct(p.shape, BF16), jax.ShapeDtypeStruct((S, W), F32)), grid=(npair, S // tq),
        in_specs=[qspec, pl.BlockSpec((S, LANES), lambda hp, i: (0, hp)), pl.BlockSpec((S, LANES), lambda hp, i: (0, npair + hp))],
        out_specs=(qspec, qspec), compiler_params=_cparams(("parallel", "arbitrary")),
    )(p, kv, kv)


def _sb_bwd(p, kv, o32, dy, heads, dk_in, dv_in, *, name):
    S = p.shape[0]
    tq = SB_TQ
    npair = heads // 2
    has_in = dk_in is not None
    scale = 1.0 / math.sqrt(HEAD_DIM)

    def body(*refs):
        q_ref, k_ref, v_ref, o_ref, do_ref = refs[:5]
        dq_ref, dk_ref, dv_ref = refs[5 + 2 * int(has_in):]
        qi = pl.program_id(1)

        @pl.when(qi == 0)
        def _():
            if has_in:
                dk_ref[...] = refs[5][...]
                dv_ref[...] = refs[6][...]
            else:
                dk_ref[...] = jnp.zeros_like(dk_ref)
                dv_ref[...] = jnp.zeros_like(dv_ref)

        r = lax.broadcasted_iota(jnp.int32, (tq, tq), 0)
        c = lax.broadcasted_iota(jnp.int32, (tq, tq), 1)
        tri = (r > c).astype(BF16)
        tri_low = (r < c).astype(BF16)
        causal = _stacked_causal(tq)
        m0 = _head_mask(LANES, 0)
        qh = _stack_heads(q_ref[...] * jnp.asarray(scale, BF16), m0)
        do = do_ref[...]
        doh = _stack_heads(do, m0)
        dov = do.astype(F32) * o_ref[...]
        dsum = jnp.concatenate([jnp.sum(jnp.where(m0, dov, 0.0), axis=1, keepdims=True),
                                jnp.sum(jnp.where(m0, 0.0, dov), axis=1, keepdims=True)], axis=0)

        def block(j, carry, gcarry, acc, mask):
            off = pl.multiple_of(j * tq, tq)
            kb = k_ref[pl.ds(off, tq), :]
            vb = v_ref[pl.ds(off, tq), :]
            zs, spb, sp_sum, w = _sb_scores(qh, kb, mask)
            ab = _sb_weights(zs, spb, tri, carry)

            def grads(ab_c, da_c):
                g = ab_c.astype(F32) * da_c
                return g, g.astype(BF16), jnp.sum(g, axis=1, keepdims=True)

            g, gb, g_sum = _by_rows(grads, ab, _dot(doh, vb, _NT))
            gcarry = gcarry + g_sum

            def logit_grads(g_c, w_c, low_c, left_c):
                rinv = 1.0 / w_c
                return ((g_c * rinv - (left_c + low_c) * (1.0 - rinv)).astype(BF16),)

            dzs = _by_rows(logit_grads, g, w, _dot(gb, tri_low, _NN), dsum - gcarry)[0]
            acc = acc + _dot(dzs, kb, _NN)
            dk_ref[pl.ds(off, tq), :] += _dot(dzs, qh, _TN)
            dv_ref[pl.ds(off, tq), :] += _dot(ab, doh, _TN)
            return (carry + sp_sum, gcarry, acc)

        zero = jnp.zeros((2 * tq, 1), F32)
        st = (zero, zero, jnp.zeros((2 * tq, LANES), F32))
        st = lax.cond(qi >= 1, lambda s: block(qi - 1, *block(qi, *s, causal), None), lambda s: block(qi, *s, causal), st)
        left = jnp.maximum(qi - 1, 0)
        odd = left % 2
        st = lax.cond(jnp.logical_and(odd == 1, _sb_live(st[0])), lambda s: block(qi - 2, *s, None), lambda s: s, st)

        def pair(s):
            j = qi - 2 - odd - 2 * s[0]
            b = block(j, s[2], s[3], s[4], None)
            b = block(j - 1, b[0], b[1], b[2], None)
            return (s[0] + 1, _sb_live(b[0])) + b

        st = lax.while_loop(lambda s: jnp.logical_and(s[0] < left // 2, s[1]), pair, (jnp.int32(0), _sb_live(st[0])) + st)[2:]
        dq_ref[...] = (jnp.where(m0, st[2][:tq], st[2][tq:]) * scale).astype(BF16)

    W = heads * HEAD_DIM
    qspec = pl.BlockSpec((tq, LANES), lambda hp, i: (i, hp))
    seq = lambda off: pl.BlockSpec((S, LANES), lambda hp, i: (0, off + hp))
    ops = [p, kv, kv, o32, dy] + ([dk_in, dv_in] if has_in else [])
    return pl.pallas_call(
        body, name=name,
        out_shape=(jax.ShapeDtypeStruct((S, W), BF16), jax.ShapeDtypeStruct((S, W), F32), jax.ShapeDtypeStruct((S, W), F32)),
        grid=(npair, S // tq),
        in_specs=[qspec, seq(0), seq(npair), qspec, qspec] + ([seq(0), seq(0)] if has_in else []),
        out_specs=(qspec, seq(0), seq(0)),
        compiler_params=_cparams(("parallel", "arbitrary")),
    )(*ops)


def _ffn_up(h, wg, wu, *, name, ride=None):
    S, D = h.shape
    F = wg.shape[0]
    tm = _pick(S, (512, 256))
    tn = _pick(F, (1408, 1024, 512, 256, 128))

    def body(h_ref, g_ref, u_ref, act_ref, silu_ref, uds_ref):
        hv = h_ref[...]
        g = _dot(hv, g_ref[...], _NT)
        u = _dot(hv, u_ref[...], _NT)
        s = jax.nn.sigmoid(g)
        silu = g * s
        act_ref[...] = (silu * u).astype(BF16)
        silu_ref[...] = silu.astype(BF16)
        uds_ref[...] = (u * (s + silu * (1.0 - s))).astype(BF16)

    wspec = pl.BlockSpec((tn, D), lambda j, i: (j, 0))
    ospec = pl.BlockSpec((tm, tn), lambda j, i: (i, j))
    out = jax.ShapeDtypeStruct((S, F), BF16)
    grid = (F // tn, S // tm)

    def edges():
        j, i = pl.program_id(0), pl.program_id(1)
        return jnp.logical_and(j == 0, i == 0), jnp.logical_and(j == grid[0] - 1, i == grid[1] - 1)

    return _ride_call(
        body, ride, edges, name=name, out_shape=(out, out, out), grid=grid,
        in_specs=[pl.BlockSpec((tm, D), lambda j, i: (i, 0)), wspec, wspec], out_specs=(ospec, ospec, ospec),
        scratch_shapes=[], compiler_params=_cparams(("parallel", "parallel")), operands=[h, wg, wu])


def _ffn_down_bwd(dx, wd, silu, uds, *, name):
    S, D = dx.shape
    F = wd.shape[0]
    tm = _pick(S, (512, 256))
    tn = _pick(F, (1408, 1024, 512, 256, 128))

    def body(dx_ref, w_ref, silu_ref, uds_ref, dg_ref, du_ref):
        da = _dot(dx_ref[...], w_ref[...], _NT)
        dg_ref[...] = (da * uds_ref[...].astype(F32)).astype(BF16)
        du_ref[...] = (da * silu_ref[...].astype(F32)).astype(BF16)

    ospec = pl.BlockSpec((tm, tn), lambda j, i: (i, j))
    out = jax.ShapeDtypeStruct((S, F), BF16)
    return pl.pallas_call(
        body, name=name, out_shape=(out, out), grid=(F // tn, S // tm),
        in_specs=[pl.BlockSpec((tm, D), lambda j, i: (i, 0)), pl.BlockSpec((tn, D), lambda j, i: (j, 0)), ospec, ospec],
        out_specs=(ospec, ospec), compiler_params=_cparams(("parallel", "parallel")),
    )(dx, wd, silu, uds)


def _adamw(w, g, m, v, *, name):
    R, C = w.shape
    tr = R
    for cand in (1024, 512, 256, 128, 64, 32, 16, 8):
        if R % cand == 0 and cand * C * 4 <= (1 << 20):
            tr = cand
            break
    bc1 = 1.0 - ADAM_B1 ** ADAM_STEP
    bc2 = 1.0 - ADAM_B2 ** ADAM_STEP

    def body(w_ref, g_ref, m_ref, v_ref, d_ref, nm_ref, nv_ref):
        gv = g_ref[...]
        nm = ADAM_B1 * m_ref[...] + (1.0 - ADAM_B1) * gv
        nv = ADAM_B2 * v_ref[...] + (1.0 - ADAM_B2) * (gv * gv)
        nm_ref[...] = nm
        nv_ref[...] = nv
        d_ref[...] = -ADAM_LR * ((nm / bc1) / (jnp.sqrt(nv / bc2) + ADAM_EPS) + ADAM_WD * w_ref[...])

    blk = pl.BlockSpec((tr, C), lambda i: (i, 0))
    out = jax.ShapeDtypeStruct((R, C), F32)
    return pl.pallas_call(body, name=name, out_shape=(out, out, out), grid=(R // tr,), in_specs=[blk] * 4,
                          out_specs=(blk, blk, blk), compiler_params=_cparams(("parallel",)))(w, g, m, v)


def _place():
    x, y, c = lax.axis_index("x"), lax.axis_index("y"), lax.axis_index("c")
    return x, y, c


def _all_gather_weights(shards, *, name):
    n = len(shards)

    def body(*refs):
        sh, full = refs[:n], refs[n:2 * n]
        send_sems, recv_sems, local_sems = refs[2 * n:]
        x, y, c = _place()
        me, sibling = (x, y, c), (x, y, 1 - c)
        chips = [(1 - x, y), (x, 1 - y), (1 - x, 1 - y)]

        def rows(t, px, py, pc):
            r = sh[t].shape[1]
            return full[t].at[:, pl.ds(pl.multiple_of((4 * px + 2 * py + pc) * r, BF16_ROWS), r), :]

        def copy(t, k, block, to, src=None):
            return pltpu.make_async_remote_copy(
                src_ref=rows(t, *block) if src is None else src, dst_ref=rows(t, *block),
                send_sem=send_sems.at[7 * t + k], recv_sem=recv_sems.at[7 * t + k], device_id=to, device_id_type=MESH)

        started = []
        for t in range(n):
            mine = pltpu.make_async_copy(sh[t], rows(t, *me), local_sems.at[t])
            mine.start()
            started.append(mine)
        sends = []
        for t in range(n):
            first = [copy(t, 0, me, sibling, src=sh[t])]
            first += [copy(t, 1 + j, me, (*chip, c), src=sh[t]) for j, chip in enumerate(chips)]
            for cp in first:
                cp.start()
            sends += first
        for t in range(n):
            for j, chip in enumerate(chips):
                copy(t, 1 + j, (*chip, c), me).wait_recv()
                fwd = copy(t, 4 + j, (*chip, c), sibling)
                fwd.start()
                sends.append(fwd)
        for t in range(n):
            copy(t, 0, sibling, me).wait_recv()
            for j, chip in enumerate(chips):
                copy(t, 4 + j, (*chip, 1 - c), me).wait_recv()
        for cp in sends:
            cp.wait_send()
        for cp in started:
            cp.wait()

    out_shape = [jax.ShapeDtypeStruct((s.shape[0], N_DEV * s.shape[1], s.shape[2]), s.dtype) for s in shards]
    return pl.pallas_call(
        body, name=name, out_shape=out_shape, in_specs=[ANY] * n, out_specs=[ANY] * n,
        scratch_shapes=[pltpu.SemaphoreType.DMA((7 * n,)), pltpu.SemaphoreType.DMA((7 * n,)), pltpu.SemaphoreType.DMA((n,))],
    )(*shards)


def _whole(ref_a, ref_b, send_sem, recv_sem, me):
    return pltpu.make_async_remote_copy(src_ref=ref_a, dst_ref=ref_b, send_sem=send_sem, recv_sem=recv_sem,
                                        device_id=me, device_id_type=MESH)


def _sibling_ride(grads):
    n = len(grads)
    lands = [lax.empty((4, s.shape[0], s.shape[3], s.shape[4]), s.dtype) for s in grads]

    def start(g, land, sems):
        x, y, c = _place()
        for t in range(n):
            for k in range(4):
                pltpu.make_async_remote_copy(
                    src_ref=g[t].at[:, k, 1 - c], dst_ref=land[t].at[k], send_sem=sems[0].at[t], recv_sem=sems[1].at[t],
                    device_id=(x, y, 1 - c), device_id_type=MESH).start()

    def wait(g, land, sems):
        x, y, c = _place()
        for t in range(n):
            w = _whole(land[t], land[t], sems[0].at[t], sems[1].at[t], (x, y, c))
            w.wait_send()
            w.wait_recv()

    return _Ride(grads, lands, start, wait, [(n,), (n,)])


def _carry_alone(ride, *, name):
    def body(o_ref):
        o_ref[...] = jnp.zeros_like(o_ref)

    one = lambda: (pl.program_id(0) == 0, pl.program_id(0) == 0)
    _, landed = _ride_call(body, ride, one, name=name, out_shape=[jax.ShapeDtypeStruct((8, LANES), F32)], grid=(1,),
                           in_specs=[], out_specs=[pl.BlockSpec((8, LANES), lambda i: (0, 0))], scratch_shapes=[],
                           compiler_params=_cparams(("arbitrary",)), operands=[])
    return landed


def _chips_ride(sums):
    n = len(sums)
    lands = [lax.empty((3,) + s.shape[1:], s.dtype) for s in sums]

    def start(s, land, sems):
        x, y, c = _place()
        for t in range(n):
            for j, (px, py) in enumerate([(1 - x, y), (x, 1 - y), (1 - x, 1 - y)]):
                pltpu.make_async_remote_copy(
                    src_ref=s[t].at[2 * px + py], dst_ref=land[t].at[j], send_sem=sems[0].at[t], recv_sem=sems[1].at[t],
                    device_id=(px, py, c), device_id_type=MESH).start()

    def wait(s, land, sems):
        x, y, c = _place()
        for t in range(n):
            w = _whole(land[t], land[t], sems[0].at[t], sems[1].at[t], (x, y, c))
            w.wait_send()
            w.wait_recv()

    return _Ride(sums, lands, start, wait, [(n,), (n,)])


def _gather_ride_1(shards):
    n = len(shards)
    fulls = [lax.empty((N_DEV * s.shape[0], s.shape[1]), s.dtype) for s in shards]

    def rows(full, r, px, py, pc, count=1):
        return full.at[pl.ds(pl.multiple_of((4 * px + 2 * py + pc) * r, BF16_ROWS), count * r), :]

    def start(sh, full, sems):
        x, y, c = _place()
        for t in range(n):
            r = sh[t].shape[0]
            mine = rows(full[t], r, x, y, c)
            pltpu.make_async_copy(sh[t], mine, sems[2].at[t]).start()
            for peer in [(x, y, 1 - c), (1 - x, y, c), (x, 1 - y, c), (1 - x, 1 - y, c)]:
                pltpu.make_async_remote_copy(src_ref=sh[t], dst_ref=mine, send_sem=sems[0].at[t], recv_sem=sems[1].at[t],
                                             device_id=peer, device_id_type=MESH).start()

    def wait(sh, full, sems):
        x, y, c = _place()
        for t in range(n):
            r = sh[t].shape[0]
            pltpu.make_async_copy(sh[t], rows(full[t], r, x, y, c), sems[2].at[t]).wait()
            four = full[t].at[pl.ds(0, 4 * r), :]
            w = _whole(four, four, sems[0].at[t], sems[1].at[t], (x, y, c))
            w.wait_send()
            w.wait_recv()

    return _Ride(shards, fulls, start, wait, [(n,), (n,), (n,)])


def _gather_ride_2(fulls):
    n = len(fulls)

    def start(_, full, sems):
        x, y, c = _place()
        for t in range(n):
            r = full[t].shape[0] // N_DEV
            for px, py in [(1 - x, y), (x, 1 - y), (1 - x, 1 - y)]:
                block = full[t].at[pl.ds(pl.multiple_of((4 * px + 2 * py + c) * r, BF16_ROWS), r), :]
                pltpu.make_async_remote_copy(src_ref=block, dst_ref=block, send_sem=sems[0].at[t], recv_sem=sems[1].at[t],
                                             device_id=(x, y, 1 - c), device_id_type=MESH).start()

    def wait(_, full, sems):
        x, y, c = _place()
        for t in range(n):
            three = full[t].at[pl.ds(0, 3 * (full[t].shape[0] // N_DEV)), :]
            w = _whole(three, three, sems[0].at[t], sems[1].at[t], (x, y, c))
            w.wait_send()
            w.wait_recv()

    return _Ride([], fulls, start, wait, [(n,), (n,)])


def _join_rides(rides):
    rides = [r for r in rides if r is not None]
    if len(rides) <= 1:
        return rides[0] if rides else None

    def parts(src, dst, sems):
        so = do = mo = 0
        for r in rides:
            yield r, src[so:so + len(r.srcs)], dst[do:do + len(r.dsts)], sems[mo:mo + len(r.sems)]
            so, do, mo = so + len(r.srcs), do + len(r.dsts), mo + len(r.sems)

    def start(src, dst, sems):
        for r, s, d, m in parts(src, dst, sems):
            r.start(s, d, m)

    def wait(src, dst, sems):
        for r, s, d, m in parts(src, dst, sems):
            r.wait(s, d, m)

    return _Ride([a for r in rides for a in r.srcs], [a for r in rides for a in r.dsts], start, wait,
                 [m for r in rides for m in r.sems])


def _chip_sum(g, land, core, *, name):
    L, _, _, r, C = g.shape

    def body(core_ref, g_ref, l_ref, o_ref):
        o_ref[...] = (g_ref[...].astype(F32) + l_ref[...].astype(F32)).astype(BF16)

    grid_spec = pltpu.PrefetchScalarGridSpec(
        num_scalar_prefetch=1, grid=(4, L),
        in_specs=[pl.BlockSpec((None, None, None, r, C), lambda k, l, core_ref: (l, k, core_ref[0], 0, 0)),
                  pl.BlockSpec((None, None, r, C), lambda k, l, core_ref: (k, l, 0, 0))],
        out_specs=pl.BlockSpec((None, None, r, C), lambda k, l, core_ref: (k, l, 0, 0)))
    return pl.pallas_call(body, name=name, out_shape=jax.ShapeDtypeStruct((4, L, r, C), BF16), grid_spec=grid_spec,
                          compiler_params=_cparams(("parallel", "parallel")))(core, g, land)


def _final_sum(sums, land, chip, *, name):
    _, L, r, C = sums.shape

    def body(chip_ref, s_ref, a_ref, b_ref, c_ref, o_ref):
        o_ref[...] = ((s_ref[...].astype(F32) + a_ref[...].astype(F32)) + b_ref[...].astype(F32)) + c_ref[...].astype(F32)

    slot = lambda j: pl.BlockSpec((None, None, r, C), lambda l, chip_ref: (j, l, 0, 0))
    grid_spec = pltpu.PrefetchScalarGridSpec(
        num_scalar_prefetch=1, grid=(L,),
        in_specs=[pl.BlockSpec((None, None, r, C), lambda l, chip_ref: (chip_ref[0], l, 0, 0)), slot(0), slot(1), slot(2)],
        out_specs=pl.BlockSpec((None, r, C), lambda l, chip_ref: (l, 0, 0)))
    return pl.pallas_call(body, name=name, out_shape=jax.ShapeDtypeStruct((L, r, C), F32), grid_spec=grid_spec,
                          compiler_params=_cparams(("parallel",)))(chip, sums, land, land, land)


def _exchange(v, reduce, *, name):
    R, C = v.shape

    def body(v_ref, o_ref, *scratch):
        if reduce:
            buf, send_sems, recv_sems = scratch
        else:
            buf = o_ref
            send_sems, recv_sems = scratch
        x, y, c = _place()
        me = 4 * x + 2 * y + c
        buf[me] = v_ref[...]
        copies = []
        for k in range(1, N_DEV):
            kx, ky, kc = (k >> 2) & 1, (k >> 1) & 1, k & 1
            peer = (1 - x if kx else x, 1 - y if ky else y, 1 - c if kc else c)
            cp = pltpu.make_async_remote_copy(src_ref=v_ref, dst_ref=buf.at[me], send_sem=send_sems.at[k - 1],
                                              recv_sem=recv_sems.at[k - 1], device_id=peer, device_id_type=MESH)
            cp.start()
            copies.append(cp)
        for cp in copies:
            cp.wait_recv()
        for cp in copies:
            cp.wait_send()
        if reduce:
            acc = buf[0]
            for d in range(1, N_DEV):
                acc = acc + buf[d]
            o_ref[...] = acc

    sems = [pltpu.SemaphoreType.DMA((N_DEV - 1,)), pltpu.SemaphoreType.DMA((N_DEV - 1,))]
    vm = pl.BlockSpec(memory_space=pltpu.VMEM)
    if reduce:
        return pl.pallas_call(body, name=name, out_shape=jax.ShapeDtypeStruct((R, C), F32), in_specs=[vm], out_specs=vm,
                              scratch_shapes=[pltpu.VMEM((N_DEV, R, C), F32)] + sems)(v)
    return pl.pallas_call(body, name=name, out_shape=jax.ShapeDtypeStruct((N_DEV, R, C), F32), in_specs=[vm], out_specs=vm,
                          scratch_shapes=sems)(v)


def _local_step(x, mem, target, norms, conv_w, depth, n_a, get_w, next_ride, ride_done, grad_ready, grad_ride, grad_landed):
    S, D = x.shape
    main = D - MEM_WIDTH
    heads = main // HEAD_DIM
    row = lambda v: v.reshape(1, D)

    mem_n = _rmsnorm(mem, row(norms["mem_norm"]), name="mem_norm")
    saved = []
    kv = hk = x_kv = w_kv = None
    def carry_mm(*args, **kwargs):
        ride = next_ride()
        if ride is None:
            return _mm(*args, **kwargs)
        out, landed = _mm(*args, ride=ride, **kwargs)
        ride_done(landed)
        return out

    for i in range(depth):
        W = functools.partial(get_w, i)
        st = {"x": x}
        h = _rmsnorm(x, row(norms["mix_norm"][i]), name=f"mix_norm{i}")
        mkv = _mm(mem_n, W("mkv"), "nn", BF16, name=f"mkv{i}")
        if i < n_a:
            p = carry_mm(h, W("a"), "nt", BF16, name=f"a_in{i}")
            y_main = _conv_fwd(p, conv_w[i], main, name=f"conv{i}")
            qblk = 3 * main // MEM_WIDTH
        else:
            p = carry_mm(h, W("b"), "nn", BF16, name=f"b_in{i}")
            y_main, st["o32"] = _sb_fwd(p, kv, heads, name=f"sb{i}")
            qblk = main // MEM_WIDTH
        y = _memattn_fwd(p, qblk, mkv, y_main, name=f"memattn{i}")
        xm = carry_mm(y, W("o"), "nn", F32, residual=x, name=f"w_o{i}")
        h2 = _rmsnorm(xm, row(norms["ffn_norm"][i]), name=f"ffn_norm{i}")
        (act, silu, uds), landed = _ffn_up(h2, W("g"), W("u"), name=f"ffn_up{i}", ride=next_ride())
        ride_done(landed)
        x = carry_mm(act, W("d"), "nn", F32, residual=xm, name=f"w_down{i}")
        st.update(h=h, mkv=mkv, p=p, qblk=qblk, y=y, xm=xm, h2=h2, silu=silu, uds=uds, act=act)
        saved.append(st)
        if i == n_a - 1:
            x_kv, w_kv = x, W("kv")
            hk = _rmsnorm(x, row(norms["kv_norm"]), name="kv_norm")
            kv = _mm(hk, w_kv, "nt", BF16, name="w_kv")

    dx, dxb, dg_final, loss = _loss_head(x, row(norms["final_norm"]), target, name="loss_head")

    dg_mix, dg_ffn, dconv = [None] * depth, [None] * depth, [None] * n_a
    dmem_n = dk = dv = dg_kv = g_kv = None
    def carry_back(*args, **kwargs):
        ride = grad_ride()
        if ride is None:
            return _mm(*args, **kwargs)
        out, landed = _mm(*args, ride=ride, **kwargs)
        grad_landed(landed)
        return out

    def carry_back_norm(*args, **kwargs):
        outs, landed = _mm_norm_bwd(*args, ride=grad_ride(), **kwargs)
        grad_landed(landed)
        return outs

    for i in reversed(range(depth)):
        st = saved[i]
        W, key = functools.partial(get_w, i), dict(_layer_keys(i, n_a))
        dgate, dup = _ffn_down_bwd(dxb, W("d"), st["silu"], st["uds"], name=f"ffn_down_bwd{i}")
        grad_ready(("d", key["d"]), carry_back(st["act"], dxb, "tn", BF16, name=f"g_w_down{i}"))
        grad_ready(("g", key["g"]), carry_back(dgate, st["h2"], "tn", BF16, name=f"g_w_gate{i}"))
        grad_ready(("u", key["u"]), carry_back(dup, st["h2"], "tn", BF16, name=f"g_w_up{i}"))
        dx, dxb, dg_ffn[i] = carry_back_norm([(dgate, W("g")), (dup, W("u"))], "nn", st["xm"], row(norms["ffn_norm"][i]), dx,
                                             name=f"d_h2_{i}")
        dy = carry_back(dxb, W("o"), "nt", BF16, name=f"d_y{i}")
        grad_ready(("o", key["o"]), carry_back(st["y"], dxb, "tn", BF16, name=f"g_w_o{i}"))
        dqmem, dmk, dmv = _memattn_bwd(st["p"], st["qblk"], st["mkv"], dy, main // MEM_WIDTH, name=f"memattn_bwd{i}")
        dmkv = jnp.concatenate([dmk, dmv], axis=1)
        grad_ready(("mkv", key["mkv"]), _mm(mem_n, dmkv, "tn", BF16, name=f"g_w_mem_kv{i}"))
        dmem_n = _mm(dmkv, W("mkv"), "nt", F32, residual=dmem_n, name=f"d_mem_n{i}")
        if i < n_a:
            db, dc, du, dconv[i] = _conv_bwd(st["p"], conv_w[i], dy, main, name=f"conv_bwd{i}")
            dp = jnp.concatenate([db, dc, du, dqmem], axis=1)
            grad_ready(("a", key["a"]), carry_back(dp, st["h"], "tn", BF16, name=f"g_a_in{i}"))
            w_in, form = W("a"), "nn"
        else:
            dq, dk, dv = _sb_bwd(st["p"], kv, st["o32"], dy, heads, dk, dv, name=f"sb_bwd{i}")
            dp = jnp.concatenate([dq, dqmem], axis=1)
            grad_ready(("b", key["b"]), carry_back(st["h"], dp, "tn", BF16, name=f"g_b_in{i}"))
            w_in, form = W("b"), "nt"
        dx, dxb, dg_mix[i] = carry_back_norm([(dp, w_in)], form, st["x"], row(norms["mix_norm"][i]), dx, name=f"d_h{i}")
        if i == n_a:
            dkv = jnp.concatenate([dk, dv], axis=1).astype(BF16)
            grad_ready(("kv", 0), carry_back(dkv, hk, "tn", BF16, name="g_w_kv"))
            dx, dxb, dg_kv = carry_back_norm([(dkv, w_kv)], "nn", x_kv, row(norms["kv_norm"]), dx, name="d_hk")
    _, _, dg_mem = _rmsnorm_bwd(mem, row(norms["mem_norm"]), dmem_n, None, name="mem_norm_bwd")

    small = {"mix_norm": jnp.concatenate(dg_mix, axis=0), "ffn_norm": jnp.concatenate(dg_ffn, axis=0), "kv_norm": dg_kv[0],
             "mem_norm": dg_mem[0], "final_norm": dg_final[0], "conv_w": jnp.stack(dconv, axis=0)}
    return loss, dx, small


_COL_SHARDED = ("a", "kv", "g", "u")
_NAMES = {"a": "a_in", "kv": "w_kv_shared", "g": "w_gate", "u": "w_up", "b": "b_in", "o": "w_o", "d": "w_down", "mkv": "w_mem_kv"}
_ORDER = ("a", "kv", "g", "u", "d", "b", "o", "mkv")
_WEIGHTS = ("mix_norm", "a_in", "conv_w", "b_in", "kv_norm", "w_kv_shared", "w_mem_kv", "w_o", "ffn_norm", "w_gate", "w_up",
            "w_down", "mem_norm", "final_norm")


def _layer_keys(i, n_a):
    keys = [("a", i) if i < n_a else ("b", i - n_a), ("g", i), ("u", i), ("d", i), ("o", i), ("mkv", i)]
    return keys + [("kv", 0)] if i == n_a - 1 else keys


def _gather_groups(i, n_a):
    first, rest = _layer_keys(i, n_a)[0], dict(_layer_keys(i, n_a)[1:])
    small = [(k, rest[k]) for k in ("o", "mkv", "kv") if k in rest]
    return [[first], small, [("g", rest["g"]), ("u", rest["u"])], [("d", rest["d"])]]


def _canonical(key, w):
    w3 = w if w.ndim == 3 else w[None]
    if key in _COL_SHARDED:
        w3 = jnp.transpose(w3, (0, 2, 1))
    return w3


def _uncanonical(key, g3, like):
    if key in _COL_SHARDED:
        g3 = jnp.transpose(g3, (0, 2, 1))
    return g3.reshape(like.shape)


def _pad_rows(flat, C):
    n = flat.shape[0]
    rows = -(-n // C)
    return jnp.pad(flat, (0, rows * C - n)).reshape(rows, C)


def kernel(x, mem, mix_norm, a_in, conv_w, b_in, kv_norm, w_kv_shared, w_mem_kv, w_o, ffn_norm, w_gate, w_up, w_down, mem_norm, final_norm, loss_target, m_mix_norm, m_a_in, m_conv_w, m_b_in, m_kv_norm, m_w_kv_shared, m_w_mem_kv, m_w_o, m_ffn_norm, m_w_gate, m_w_up, m_w_down, m_mem_norm, m_final_norm, v_mix_norm, v_a_in, v_conv_w, v_b_in, v_kv_norm, v_w_kv_shared, v_w_mem_kv, v_w_o, v_ffn_norm, v_w_gate, v_w_up, v_w_down, v_mem_norm, v_final_norm):
    weights = dict(mix_norm=mix_norm, a_in=a_in, conv_w=conv_w, b_in=b_in, kv_norm=kv_norm, w_kv_shared=w_kv_shared,
                   w_mem_kv=w_mem_kv, w_o=w_o, ffn_norm=ffn_norm, w_gate=w_gate, w_up=w_up, w_down=w_down,
                   mem_norm=mem_norm, final_norm=final_norm)
    moments_m = dict(mix_norm=m_mix_norm, a_in=m_a_in, conv_w=m_conv_w, b_in=m_b_in, kv_norm=m_kv_norm,
                     w_kv_shared=m_w_kv_shared, w_mem_kv=m_w_mem_kv, w_o=m_w_o, ffn_norm=m_ffn_norm, w_gate=m_w_gate,
                     w_up=m_w_up, w_down=m_w_down, mem_norm=m_mem_norm, final_norm=m_final_norm)
    moments_v = dict(mix_norm=v_mix_norm, a_in=v_a_in, conv_w=v_conv_w, b_in=v_b_in, kv_norm=v_kv_norm,
                     w_kv_shared=v_w_kv_shared, w_mem_kv=v_w_mem_kv, w_o=v_w_o, ffn_norm=v_ffn_norm, w_gate=v_w_gate,
                     w_up=v_w_up, w_down=v_w_down, mem_norm=v_mem_norm, final_norm=v_final_norm)
    D = x.shape[-1]
    depth, n_a = w_o.shape[0], a_in.shape[0]
    xi, yi, ci = _place()
    me = 4 * xi + 2 * yi + ci
    core = ci.reshape(1).astype(jnp.int32)
    chip = (2 * xi + yi).reshape(1).astype(jnp.int32)

    cw_shape = conv_w.shape
    cw_rows = _pad_rows(conv_w.reshape(-1), D)
    cw_rows = jnp.pad(cw_rows, ((0, 8 - cw_rows.shape[0]), (0, 0)))
    cw_gathered = _exchange(cw_rows, False, name="gather_conv_w")
    n_cw = cw_shape[0] * cw_shape[1] * cw_shape[2]
    cw_all = cw_gathered.reshape(N_DEV, -1)[:, :n_cw].reshape((N_DEV,) + cw_shape)
    conv_full = jnp.transpose(cw_all, (1, 2, 0, 3)).reshape(cw_shape[0], cw_shape[1], N_DEV * cw_shape[2])

    shard3 = {k: _canonical(k, weights[_NAMES[k]]).astype(BF16) for k in _ORDER}
    keys0 = _layer_keys(0, n_a)
    fulls0 = _all_gather_weights([shard3[k][l][None] for k, l in keys0], name="all_gather_layer0")
    full = {kl: f[0] for kl, f in zip(keys0, fulls0)}

    groups = [grp for i in range(1, depth) for grp in _gather_groups(i, n_a)]
    carried, riding = [0], []

    def next_ride():
        n = carried[0]
        carried[0] += 1
        second = groups[n - 1] if 1 <= n <= len(groups) else []
        first = groups[n] if n < len(groups) else []
        if not second + first:
            return None
        riding.append(second + first)
        return _join_rides([_gather_ride_2([full[kl] for kl in second]) if second else None,
                            _gather_ride_1([shard3[k][l] for k, l in first]) if first else None])

    def ride_done(landed):
        if landed:
            full.update(zip(riding.pop(), landed))

    def get_w(i, key):
        return full[(key, dict(_layer_keys(i, n_a))[key])]

    fresh, summed, reduced, travelling = [], [], {}, []

    def grad_ready(kl, g):
        fresh.append((kl, g.reshape(1, 4, 2, g.shape[0] // N_DEV, g.shape[1])))

    def grad_ride():
        if not fresh + summed:
            return None
        travelling.append((list(summed), list(fresh)))
        ride = _join_rides([_chips_ride([s for _, s in summed]) if summed else None,
                            _sibling_ride([g for _, g in fresh]) if fresh else None])
        summed.clear()
        fresh.clear()
        return ride

    def grad_landed(landed):
        if not landed:
            return
        between_chips, to_sibling = travelling.pop()
        for (kl, s), land in zip(between_chips, landed):
            reduced[kl] = (s, land)
        for (kl, g), land in zip(to_sibling, landed[len(between_chips):]):
            summed.append((kl, _chip_sum(g, land, core, name=f"chip_sum_{kl[0]}{kl[1]}")))

    norms = {k: weights[k] for k in ("mix_norm", "ffn_norm", "kv_norm", "mem_norm", "final_norm")}
    loss, grad_x, small = _local_step(x[0], mem[0], loss_target[0], norms, conv_full, depth, n_a, get_w, next_ride, ride_done,
                                      grad_ready, grad_ride, grad_landed)
    for tail in range(2):
        ride = grad_ride()
        if ride is not None:
            grad_landed(_carry_alone(ride, name=f"reduce_scatter_tail{tail}"))

    shard_grads = {kl: _final_sum(s, land, chip, name=f"final_sum_{kl[0]}{kl[1]}")[0] for kl, (s, land) in reduced.items()}
    grads = {}
    for k in _ORDER:
        g3 = jnp.stack([shard_grads[(k, l)] for l in range(shard3[k].shape[0])])
        grads[_NAMES[k]] = _uncanonical(k, g3, weights[_NAMES[k]])

    order = ("mix_norm", "ffn_norm", "kv_norm", "mem_norm", "final_norm", "conv_w")
    flat = jnp.concatenate([small[k].reshape(-1) for k in order] + [loss[0, :1]])
    n_flat = flat.shape[0]
    rows = _pad_rows(flat, D)
    rows = jnp.pad(rows, ((0, (-rows.shape[0]) % 8), (0, 0)))
    total = _exchange(rows, True, name="all_reduce_small").reshape(-1)[:n_flat]
    off = 0
    for k in order:
        n = small[k].size
        grads[k] = total[off:off + n].reshape(small[k].shape)
        off += n
    loss_total = total[off]
    grads["conv_w"] = lax.dynamic_slice_in_dim(grads["conv_w"], me * cw_shape[2], cw_shape[2], axis=2)

    deltas, new_m, new_v = {}, {}, {}
    for k in _WEIGHTS:
        w = weights[k]
        two = (lambda a: a.reshape(-1, a.shape[-1])) if w.ndim > 1 else (lambda a: a.reshape(1, -1))
        d, nm, nv = _adamw(two(w), two(grads[k]), two(moments_m[k]), two(moments_v[k]), name=f"adamw_{k}")
        deltas[k], new_m[k], new_v[k] = d.reshape(w.shape), nm.reshape(w.shape), nv.reshape(w.shape)

    return (loss_total, grad_x[None], *[grads[k] for k in _WEIGHTS], *[deltas[k] for k in _WEIGHTS],
            *[new_m[k] for k in _WEIGHTS], *[new_v[k] for k in _WEIGHTS])
```

```python
import functools
import math

import jax
import jax.numpy as jnp
from jax import lax
from jax.experimental import pallas as pl
from jax.experimental.pallas import tpu as pltpu

F32 = jnp.float32
BF16 = jnp.bfloat16
MESH = pl.DeviceIdType.MESH

HEAD_DIM = 64
MEM_HEADS = 4
MEM_WIDTH = MEM_HEADS * HEAD_DIM
EPS = 1e-6
LANES = 128
BF16_ROWS = 16
VMEM_LIMIT = 56 * 1024 * 1024
N_DEV = 8

ADAM_LR = 0.001
ADAM_B1 = 0.9
ADAM_B2 = 0.999
ADAM_EPS = 1e-08
ADAM_WD = 0.01
ADAM_STEP = 10

ANY = pl.BlockSpec(memory_space=pl.ANY)


def _cparams(sem=None):
    return pltpu.CompilerParams(dimension_semantics=sem, vmem_limit_bytes=VMEM_LIMIT)


def _pick(n, cands):
    for c in cands:
        if n % c == 0:
            return c
    raise ValueError(f"no tile for {n} in {cands}")


def _mm(a, b, form, out_dtype, *, name, residual=None, ride=None):
    if form == "tn":
        K, M = a.shape
    else:
        M, K = a.shape
    if form == "nt":
        N, K2 = b.shape
    else:
        K2, N = b.shape
    assert K == K2, (name, a.shape, b.shape)
    wide = (1408, 1280, 1024, 768, 512, 256, 128)
    tm = _pick(M, wide if form == "tn" else (1024, 512, 256, 128))
    tn = _pick(N, wide)
    tk = _pick(K, (1024, 1408, 1280, 768, 512, 256))
    nk = K // tk
    dims = {"nn": (((1,), (0,)), ((), ())), "nt": (((1,), (1,)), ((), ())), "tn": (((0,), (0,)), ((), ()))}[form]
    a_bytes, b_bytes = M * K * a.dtype.itemsize, N * K * b.dtype.itemsize
    n_outer = nk == 1 and (N // tn) * a_bytes + b_bytes < a_bytes + (M // tm) * b_bytes
    ij = (lambda g0, g1: (g1, g0)) if n_outer else (lambda g0, g1: (g0, g1))

    def spec(block, f):
        return pl.BlockSpec(block, lambda g0, g1, k: f(*ij(g0, g1), k))

    a_spec = spec((tk, tm), lambda i, j, k: (k, i)) if form == "tn" else spec((tm, tk), lambda i, j, k: (i, k))
    b_spec = spec((tn, tk), lambda i, j, k: (j, k)) if form == "nt" else spec((tk, tn), lambda i, j, k: (k, j))
    out_spec = spec((tm, tn), lambda i, j, k: (i, j))
    operands, in_specs = [a, b], [a_spec, b_spec]
    has_res = residual is not None
    if has_res:
        operands.append(residual)
        in_specs.append(out_spec)
    grid = (N // tn, M // tm, nk) if n_outer else (M // tm, N // tn, nk)

    def body(*refs):
        a_ref, b_ref = refs[0], refs[1]
        r_ref = refs[2] if has_res else None
        o_ref = refs[2 + int(has_res)]
        acc_ref = refs[-1]
        part = lax.dot_general(a_ref[...].astype(BF16), b_ref[...].astype(BF16), dims, preferred_element_type=F32)

        def finish(total):
            if has_res:
                total = total + r_ref[...].astype(F32)
            o_ref[...] = total.astype(out_dtype)

        if nk == 1:
            finish(part)
        else:
            k = pl.program_id(2)

            @pl.when(k == 0)
            def _():
                acc_ref[...] = part

            @pl.when(jnp.logical_and(k > 0, k < nk - 1))
            def _():
                acc_ref[...] += part

            @pl.when(k == nk - 1)
            def _():
                finish(acc_ref[...] + part)

    def edges():
        ids = [pl.program_id(d) for d in range(3)]
        first = jnp.logical_and(jnp.logical_and(ids[0] == 0, ids[1] == 0), ids[2] == 0)
        last = jnp.logical_and(jnp.logical_and(ids[0] == grid[0] - 1, ids[1] == grid[1] - 1), ids[2] == grid[2] - 1)
        return first, last

    (out,), landed = _ride_call(
        body, ride, edges, name=name, out_shape=[jax.ShapeDtypeStruct((M, N), out_dtype)], grid=grid, in_specs=in_specs,
        out_specs=[out_spec], scratch_shapes=[pltpu.VMEM((tm, tn), F32)],
        compiler_params=_cparams(("parallel", "parallel", "arbitrary")), operands=operands)
    return out if ride is None else (out, landed)


def _rmsnorm(x, g, *, name):
    R, D = x.shape
    tr = _pick(R, (512, 256))

    def body(x_ref, g_ref, o_ref):
        xv = x_ref[...]
        r = lax.rsqrt(jnp.mean(xv * xv, axis=-1, keepdims=True) + EPS)
        o_ref[...] = (xv * r * g_ref[...]).astype(BF16)

    return pl.pallas_call(
        body, name=name, out_shape=jax.ShapeDtypeStruct((R, D), BF16), grid=(R // tr,),
        in_specs=[pl.BlockSpec((tr, D), lambda i: (i, 0)), pl.BlockSpec((1, D), lambda i: (0, 0))],
        out_specs=pl.BlockSpec((tr, D), lambda i: (i, 0)), compiler_params=_cparams(("parallel",)),
    )(x, g)


def _rmsnorm_bwd(x, g, dh, dx_in, *, name):
    R, D = x.shape
    tr = _pick(R, (512, 256))
    has_in = dx_in is not None

    def body(*refs):
        x_ref, g_ref, dh_ref = refs[:3]
        dxi_ref = refs[3] if has_in else None
        dx_ref, dxb_ref, dg_ref = refs[3 + int(has_in):]
        xv = x_ref[...]
        r = lax.rsqrt(jnp.mean(xv * xv, axis=-1, keepdims=True) + EPS)
        xhat = xv * r
        dhv = dh_ref[...].astype(F32)
        dxh = dhv * g_ref[...]
        dx = r * (dxh - xhat * jnp.mean(dxh * xhat, axis=-1, keepdims=True))
        if has_in:
            dx = dx + dxi_ref[...]
        dx_ref[...] = dx
        dxb_ref[...] = dx.astype(BF16)
        part = jnp.sum(dhv * xhat, axis=0, keepdims=True)

        @pl.when(pl.program_id(0) == 0)
        def _():
            dg_ref[...] = part

        @pl.when(pl.program_id(0) > 0)
        def _():
            dg_ref[...] += part

    row = pl.BlockSpec((tr, D), lambda i: (i, 0))
    vec = pl.BlockSpec((1, D), lambda i: (0, 0))
    ops = [x, g, dh] + ([dx_in] if has_in else [])
    return pl.pallas_call(
        body, name=name,
        out_shape=(jax.ShapeDtypeStruct((R, D), F32), jax.ShapeDtypeStruct((R, D), BF16), jax.ShapeDtypeStruct((1, D), F32)),
        grid=(R // tr,), in_specs=[row, vec, row] + ([row] if has_in else []), out_specs=(row, row, vec),
        compiler_params=_cparams(("arbitrary",)),
    )(*ops)


class _Ride:
    def __init__(self, srcs, dsts, start, wait, sems):
        self.srcs, self.dsts, self.start, self.wait, self.sems = list(srcs), list(dsts), start, wait, list(sems)


def _ride_call(body, ride, edges, *, name, out_shape, grid, in_specs, out_specs, scratch_shapes, compiler_params, operands):
    n_in, n_out, n_scr = len(in_specs), len(out_specs), len(scratch_shapes)
    if ride is None:
        outs = pl.pallas_call(body, name=name, out_shape=tuple(out_shape), grid=grid, in_specs=list(in_specs),
                              out_specs=tuple(out_specs), scratch_shapes=list(scratch_shapes),
                              compiler_params=compiler_params)(*operands)
        return tuple(outs), []
    ns, nd = len(ride.srcs), len(ride.dsts)

    def riding(*refs):
        ins, rin = refs[:n_in], refs[n_in:n_in + ns + nd]
        outs = refs[n_in + ns + nd:n_in + ns + nd + n_out]
        scratch = refs[n_in + ns + 2 * nd + n_out:]
        sems = scratch[n_scr:]
        first, last = edges()

        @pl.when(first)
        def _():
            ride.start(rin[:ns], rin[ns:], sems)

        body(*ins, *outs, *scratch[:n_scr])

        @pl.when(last)
        def _():
            ride.wait(rin[:ns], rin[ns:], sems)

    outs = pl.pallas_call(
        riding, name=name, out_shape=(*out_shape, *[jax.ShapeDtypeStruct(d.shape, d.dtype) for d in ride.dsts]), grid=grid,
        in_specs=[*in_specs, *[ANY] * (ns + nd)], out_specs=(*out_specs, *[ANY] * nd),
        scratch_shapes=[*scratch_shapes, *[pltpu.SemaphoreType.DMA(s) for s in ride.sems]],
        input_output_aliases={n_in + ns + d: n_out + d for d in range(nd)}, compiler_params=compiler_params,
    )(*operands, *ride.srcs, *ride.dsts)
    return tuple(outs[:n_out]), list(outs[n_out:])


def _mm_norm_bwd(parts, form, x, g, dx_in, *, name, ride=None):
    S, K = parts[0][0].shape
    D = x.shape[1]
    tm = _pick(S, (512, 256))
    tk = _pick(K, (1024, 1408, 1280, 768, 512, 256))
    nk, P = K // tk, len(parts)
    dims = _NN if form == "nn" else _NT

    ni, nsteps = S // tm, P * nk

    def body(*refs):
        ab = refs[:2 * P]
        x_ref, g_ref, dxi_ref, dx_ref, dxb_ref, dg_ref, acc_ref = refs[2 * P:]
        k, i = pl.program_id(0), pl.program_id(1)
        rows = pl.ds(pl.multiple_of(i * tm, tm), tm)

        @pl.when(k == 0)
        def _():
            acc_ref[rows, :] = jnp.zeros((tm, D), F32)

        for p in range(P):
            @pl.when(jnp.logical_and(k >= p * nk, k < (p + 1) * nk))
            def _():
                acc_ref[rows, :] += _dot(ab[2 * p][...], ab[2 * p + 1][...], dims)

        @pl.when(k == nsteps - 1)
        def _():
            xv = x_ref[...]
            r = lax.rsqrt(jnp.mean(xv * xv, axis=-1, keepdims=True) + EPS)
            xhat = xv * r
            dhv = acc_ref[rows, :]
            dxh = dhv * g_ref[...]
            dx = r * (dxh - xhat * jnp.mean(dxh * xhat, axis=-1, keepdims=True)) + dxi_ref[...]
            dx_ref[...] = dx
            dxb_ref[...] = dx.astype(BF16)
            part = jnp.sum(dhv * xhat, axis=0, keepdims=True)

            @pl.when(i == 0)
            def _():
                dg_ref[...] = part

            @pl.when(i > 0)
            def _():
                dg_ref[...] += part

    def kk(p):
        return lambda k: jnp.clip(k - p * nk, 0, nk - 1)

    def active_row(p):
        return lambda k, i: jnp.where(jnp.logical_and(k >= p * nk, k < (p + 1) * nk), i, 0)

    in_specs, operands = [], []
    for p, (a, b) in enumerate(parts):
        in_specs.append(pl.BlockSpec((tm, tk), lambda k, i, f=kk(p), r=active_row(p): (r(k, i), f(k))))
        in_specs.append(pl.BlockSpec((tk, D), lambda k, i, f=kk(p): (f(k), 0)) if form == "nn"
                        else pl.BlockSpec((D, tk), lambda k, i, f=kk(p): (0, f(k))))
        operands += [a, b]
    row = pl.BlockSpec((tm, D), lambda k, i: (jnp.where(k == nsteps - 1, i, 0), 0))
    vec = pl.BlockSpec((1, D), lambda k, i: (0, 0))

    def edges():
        k, i = pl.program_id(0), pl.program_id(1)
        return jnp.logical_and(i == 0, k == 0), jnp.logical_and(i == ni - 1, k == nsteps - 1)

    return _ride_call(
        body, ride, edges, name=name,
        out_shape=(jax.ShapeDtypeStruct((S, D), F32), jax.ShapeDtypeStruct((S, D), BF16), jax.ShapeDtypeStruct((1, D), F32)),
        grid=(nsteps, ni), in_specs=in_specs + [row, vec, row], out_specs=(row, row, vec),
        scratch_shapes=[pltpu.VMEM((S, D), F32)], compiler_params=_cparams(("arbitrary", "arbitrary")),
        operands=[*operands, x, g, dx_in])


def _loss_head(x, g, target, *, name):
    R, D = x.shape
    tr = _pick(R, (512, 256))

    def body(x_ref, g_ref, t_ref, dx_ref, dxb_ref, dg_ref, loss_ref):
        xv = x_ref[...]
        gv = g_ref[...]
        r = lax.rsqrt(jnp.mean(xv * xv, axis=-1, keepdims=True) + EPS)
        xhat = xv * r
        err = xhat * gv - t_ref[...]
        loss = 0.5 * jnp.sum(jnp.mean(err * err, axis=-1, keepdims=True), axis=0, keepdims=True)
        dy = err * (1.0 / D)
        dxh = dy * gv
        dx = r * (dxh - xhat * jnp.mean(dxh * xhat, axis=-1, keepdims=True))
        dx_ref[...] = dx
        dxb_ref[...] = dx.astype(BF16)
        dg = jnp.sum(dy * xhat, axis=0, keepdims=True)
        lossv = jnp.broadcast_to(loss, (1, LANES))

        @pl.when(pl.program_id(0) == 0)
        def _():
            dg_ref[...] = dg
            loss_ref[...] = lossv

        @pl.when(pl.program_id(0) > 0)
        def _():
            dg_ref[...] += dg
            loss_ref[...] += lossv

    row = pl.BlockSpec((tr, D), lambda i: (i, 0))
    vec = pl.BlockSpec((1, D), lambda i: (0, 0))
    return pl.pallas_call(
        body, name=name,
        out_shape=(jax.ShapeDtypeStruct((R, D), F32), jax.ShapeDtypeStruct((R, D), BF16), jax.ShapeDtypeStruct((1, D), F32),
                   jax.ShapeDtypeStruct((1, LANES), F32)),
        grid=(R // tr,), in_specs=[row, vec, row], out_specs=(row, row, vec, pl.BlockSpec((1, LANES), lambda i: (0, 0))),
        compiler_params=_cparams(("arbitrary",)),
    )(x, g, target)


def _conv_taps(gv, S):
    t = lax.broadcasted_iota(jnp.int32, gv.shape, 0)
    g1 = jnp.where(t >= 1, pltpu.roll(gv, 1, 0), 0.0)
    g2 = jnp.where(t >= 2, pltpu.roll(gv, 2, 0), 0.0)
    return g1, g2


def _conv_fwd(p, w, main, *, name):
    S = p.shape[0]
    tc = LANES
    nb = main // tc

    def body(b_ref, c_ref, u_ref, w_ref, y_ref):
        gv = c_ref[...].astype(F32) * u_ref[...].astype(F32)
        g1, g2 = _conv_taps(gv, S)
        cv = w_ref[0:1, :] * g2 + w_ref[1:2, :] * g1 + w_ref[2:3, :] * gv
        y_ref[...] = (b_ref[...].astype(F32) * cv).astype(BF16)

    col = lambda off: pl.BlockSpec((S, tc), lambda j: (0, off + j))
    return pl.pallas_call(
        body, name=name, out_shape=jax.ShapeDtypeStruct((S, p.shape[1] - 2 * main), BF16), grid=(nb,),
        in_specs=[col(0), col(nb), col(2 * nb), pl.BlockSpec((3, tc), lambda j: (0, j))],
        out_specs=pl.BlockSpec((S, tc), lambda j: (0, j)), compiler_params=_cparams(("parallel",)),
    )(p, p, p, w)


def _conv_bwd(p, w, dy, main, *, name):
    S = p.shape[0]
    tc = LANES
    nb = main // tc

    def body(b_ref, c_ref, u_ref, w_ref, dy_ref, db_ref, dc_ref, du_ref, dw_ref):
        cvv, uv = c_ref[...].astype(F32), u_ref[...].astype(F32)
        gv = cvv * uv
        g1, g2 = _conv_taps(gv, S)
        w0, w1, w2 = w_ref[0:1, :], w_ref[1:2, :], w_ref[2:3, :]
        dyv = dy_ref[...].astype(F32)
        db_ref[...] = (dyv * (w0 * g2 + w1 * g1 + w2 * gv)).astype(BF16)
        dcv = dyv * b_ref[...].astype(F32)
        t = lax.broadcasted_iota(jnp.int32, dcv.shape, 0)
        n1 = jnp.where(t <= S - 2, pltpu.roll(dcv, S - 1, 0), 0.0)
        n2 = jnp.where(t <= S - 3, pltpu.roll(dcv, S - 2, 0), 0.0)
        dg = w2 * dcv + w1 * n1 + w0 * n2
        dc_ref[...] = (dg * uv).astype(BF16)
        du_ref[...] = (dg * cvv).astype(BF16)
        dw_ref[0:1, :] = jnp.sum(dcv * g2, axis=0, keepdims=True)
        dw_ref[1:2, :] = jnp.sum(dcv * g1, axis=0, keepdims=True)
        dw_ref[2:3, :] = jnp.sum(dcv * gv, axis=0, keepdims=True)

    col = lambda off: pl.BlockSpec((S, tc), lambda j: (0, off + j))
    out = jax.ShapeDtypeStruct((S, main), BF16)
    return pl.pallas_call(
        body, name=name, out_shape=(out, out, out, jax.ShapeDtypeStruct((3, main), F32)), grid=(nb,),
        in_specs=[col(0), col(nb), col(2 * nb), pl.BlockSpec((3, tc), lambda j: (0, j)), col(0)],
        out_specs=(col(0), col(0), col(0), pl.BlockSpec((3, tc), lambda j: (0, j))),
        compiler_params=_cparams(("parallel",)),
    )(p, p, p, w, dy)


def _head_mask(width, h):
    lane = lax.broadcasted_iota(jnp.int32, (1, width), 1)
    return jnp.logical_and(lane >= h * HEAD_DIM, lane < (h + 1) * HEAD_DIM)


_NT = (((1,), (1,)), ((), ()))
_NN = (((1,), (0,)), ((), ()))
_TN = (((0,), (0,)), ((), ()))


def _dot(a, b, dims):
    return lax.dot_general(a, b, dims, preferred_element_type=F32)


def _mem_probs(qh, kv):
    s = _dot(qh, kv, _NT) * (1.0 / math.sqrt(HEAD_DIM))
    s = s - jnp.max(s, axis=-1, keepdims=True)
    e = jnp.exp(s)
    return e / jnp.sum(e, axis=-1, keepdims=True)


def _memattn_fwd(p, qblk, mkv, into, *, name):
    S = p.shape[0]
    M = mkv.shape[0]
    W = MEM_WIDTH
    tq = _pick(S, (512, 256))
    last = into.shape[1] // W - 1

    def body(q_ref, k_ref, v_ref, _, o_ref):
        q = q_ref[...].astype(BF16)
        kv, vv = k_ref[...], v_ref[...]
        out = jnp.zeros((tq, W), F32)
        for h in range(MEM_HEADS):
            m = _head_mask(W, h)
            pr = _mem_probs(jnp.where(m, q, jnp.zeros_like(q)), kv)
            out = jnp.where(m, _dot(pr.astype(BF16), vv, _NN), out)
        o_ref[...] = out.astype(BF16)

    return pl.pallas_call(
        body, name=name, out_shape=jax.ShapeDtypeStruct(into.shape, BF16), grid=(S // tq,),
        in_specs=[pl.BlockSpec((tq, W), lambda i: (i, qblk)), pl.BlockSpec((M, W), lambda i: (0, 0)),
                  pl.BlockSpec((M, W), lambda i: (0, 1)), ANY],
        out_specs=pl.BlockSpec((tq, W), lambda i: (i, last)), input_output_aliases={3: 0},
        compiler_params=_cparams(("parallel",)),
    )(p, mkv, mkv, into)


def _memattn_bwd(p, qblk, mkv, dy, dyblk, *, name):
    S = p.shape[0]
    M = mkv.shape[0]
    W = MEM_WIDTH
    tq = _pick(S, (512, 256))
    scale = 1.0 / math.sqrt(HEAD_DIM)

    def body(q_ref, k_ref, v_ref, do_ref, dq_ref, dk_ref, dv_ref, dk_acc, dv_acc):
        q = q_ref[...].astype(BF16)
        do = do_ref[...].astype(BF16)
        kv, vv = k_ref[...], v_ref[...]
        dq = jnp.zeros((tq, W), F32)
        dk = jnp.zeros((M, W), F32)
        dv = jnp.zeros((M, W), F32)
        for h in range(MEM_HEADS):
            m = _head_mask(W, h)
            qh = jnp.where(m, q, jnp.zeros_like(q))
            doh = jnp.where(m, do, jnp.zeros_like(do))
            pr = _mem_probs(qh, kv)
            dpr = _dot(doh, vv, _NT)
            ds = (pr * (dpr - jnp.sum(dpr * pr, axis=-1, keepdims=True)) * scale).astype(BF16)
            dq = jnp.where(m, _dot(ds, kv, _NN), dq)
            dk = dk + _dot(ds, qh, _TN)
            dv = dv + _dot(pr.astype(BF16), doh, _TN)
        dq_ref[...] = dq.astype(BF16)
        i = pl.program_id(0)

        @pl.when(i == 0)
        def _():
            dk_acc[...] = dk
            dv_acc[...] = dv

        @pl.when(i > 0)
        def _():
            dk_acc[...] += dk
            dv_acc[...] += dv

        @pl.when(i == S // tq - 1)
        def _():
            dk_ref[...] = dk_acc[...].astype(BF16)
            dv_ref[...] = dv_acc[...].astype(BF16)

    kspec = lambda c: pl.BlockSpec((M, W), lambda i: (0, c))
    return pl.pallas_call(
        body, name=name,
        out_shape=(jax.ShapeDtypeStruct((S, W), BF16), jax.ShapeDtypeStruct((M, W), BF16), jax.ShapeDtypeStruct((M, W), BF16)),
        grid=(S // tq,),
        in_specs=[pl.BlockSpec((tq, W), lambda i: (i, qblk)), kspec(0), kspec(1), pl.BlockSpec((tq, W), lambda i: (i, dyblk))],
        out_specs=(pl.BlockSpec((tq, W), lambda i: (i, 0)), kspec(0), kspec(0)),
        scratch_shapes=[pltpu.VMEM((M, W), F32), pltpu.VMEM((M, W), F32)],
        compiler_params=_cparams(("arbitrary",)),
    )(p, mkv, mkv, dy)


SB_TQ = 256
SB_CLAMP = 80.0
SB_DEAD = 110.0


SB_CHUNK = 64


def _by_rows(fn, *arrays):
    rows = next(a for a in arrays if a is not None).shape[0]
    outs = [fn(*[None if a is None else a[r0:r0 + SB_CHUNK] for a in arrays]) for r0 in range(0, rows, SB_CHUNK)]
    return tuple(jnp.concatenate(col, axis=0) for col in zip(*outs))


def _sb_scores(qh, kb, causal):
    def chain(z, mask):
        z = jnp.clip(z, -SB_CLAMP, SB_CLAMP)
        w = 1.0 + jnp.exp(z)
        sp = jnp.log(w)
        zs = z - sp
        if mask is not None:
            sp = jnp.where(mask, sp, 0.0)
            zs = jnp.where(mask, zs, -1e30)
            w = jnp.where(mask, w, 1.0)
        return zs, sp.astype(BF16), jnp.sum(sp, axis=1, keepdims=True), w

    return _by_rows(chain, _dot(qh, kb, _NT), causal)


def _sb_weights(zs, spb, tri, carry):
    return _by_rows(lambda zs_c, t_c, c_c: (jnp.exp(zs_c - (t_c + c_c)).astype(BF16),), zs, _dot(spb, tri, _NN), carry)[0]


def _sb_live(carry):
    return jnp.min(carry) <= SB_DEAD


def _stack_heads(v, m0):
    zero = jnp.zeros_like(v)
    return jnp.concatenate([jnp.where(m0, v, zero), jnp.where(m0, zero, v)], axis=0)


def _stacked_causal(tq):
    r = lax.broadcasted_iota(jnp.int32, (2 * tq, tq), 0)
    c = lax.broadcasted_iota(jnp.int32, (2 * tq, tq), 1)
    return c < jnp.where(r >= tq, r - tq, r)


def _sb_fwd(p, kv, heads, *, name):
    S = p.shape[0]
    tq = SB_TQ
    npair = heads // 2

    def body(q_ref, k_ref, v_ref, o_ref, o32_ref):
        qi = pl.program_id(1)
        r = lax.broadcasted_iota(jnp.int32, (tq, tq), 0)
        c = lax.broadcasted_iota(jnp.int32, (tq, tq), 1)
        tri = (r > c).astype(BF16)
        causal = _stacked_causal(tq)
        m0 = _head_mask(LANES, 0)
        qh = _stack_heads(q_ref[...] * jnp.asarray(1.0 / math.sqrt(HEAD_DIM), BF16), m0)

        def block(j, carry, acc, mask):
            off = pl.multiple_of(j * tq, tq)
            kb = k_ref[pl.ds(off, tq), :]
            vb = v_ref[pl.ds(off, tq), :]
            zs, spb, sp_sum, _ = _sb_scores(qh, kb, mask)
            acc = acc + _dot(_sb_weights(zs, spb, tri, carry), vb, _NN)
            return carry + sp_sum, acc

        st = (jnp.zeros((2 * tq, 1), F32), jnp.zeros((2 * tq, LANES), F32))
        st = lax.cond(qi >= 1, lambda s: block(qi - 1, *block(qi, *s, causal), None), lambda s: block(qi, *s, causal), st)
        left = jnp.maximum(qi - 1, 0)
        odd = left % 2
        st = lax.cond(jnp.logical_and(odd == 1, _sb_live(st[0])), lambda s: block(qi - 2, *s, None), lambda s: s, st)

        def pair(s):
            it, _, carry, acc = s
            j = qi - 2 - odd - 2 * it
            carry, acc = block(j, carry, acc, None)
            carry, acc = block(j - 1, carry, acc, None)
            return it + 1, _sb_live(carry), carry, acc

        _, _, carry, acc = lax.while_loop(lambda s: jnp.logical_and(s[0] < left // 2, s[1]), pair,
                                          (jnp.int32(0), _sb_live(st[0]), st[0], st[1]))
        out = jnp.where(m0, acc[:tq], acc[tq:])
        o_ref[...] = out.astype(BF16)
        o32_ref[...] = out

    W = heads * HEAD_DIM
    qspec = pl.BlockSpec((tq, LANES), lambda hp, i: (i, hp))
    return pl.pallas_call(
        body, name=name, out_shape=(jax.ShapeDtypeStruct(p.shape, BF16), jax.ShapeDtypeStruct((S, W), F32)), grid=(npair, S // tq),
        in_specs=[qspec, pl.BlockSpec((S, LANES), lambda hp, i: (0, hp)), pl.BlockSpec((S, LANES), lambda hp, i: (0, npair + hp))],
        out_specs=(qspec, qspec), compiler_params=_cparams(("parallel", "arbitrary")),
    )(p, kv, kv)


def _sb_bwd(p, kv, o32, dy, heads, dk_in, dv_in, *, name):
    S = p.shape[0]
    tq = SB_TQ
    npair = heads // 2
    has_in = dk_in is not None
    scale = 1.0 / math.sqrt(HEAD_DIM)

    def body(*refs):
        q_ref, k_ref, v_ref, o_ref, do_ref = refs[:5]
        dq_ref, dk_ref, dv_ref = refs[5 + 2 * int(has_in):]
        qi = pl.program_id(1)

        @pl.when(qi == 0)
        def _():
            if has_in:
                dk_ref[...] = refs[5][...]
                dv_ref[...] = refs[6][...]
            else:
                dk_ref[...] = jnp.zeros_like(dk_ref)
                dv_ref[...] = jnp.zeros_like(dv_ref)

        r = lax.broadcasted_iota(jnp.int32, (tq, tq), 0)
        c = lax.broadcasted_iota(jnp.int32, (tq, tq), 1)
        tri = (r > c).astype(BF16)
        tri_low = (r < c).astype(BF16)
        causal = _stacked_causal(tq)
        m0 = _head_mask(LANES, 0)
        qh = _stack_heads(q_ref[...] * jnp.asarray(scale, BF16), m0)
        do = do_ref[...]
        doh = _stack_heads(do, m0)
        dov = do.astype(F32) * o_ref[...]
        dsum = jnp.concatenate([jnp.sum(jnp.where(m0, dov, 0.0), axis=1, keepdims=True),
                                jnp.sum(jnp.where(m0, 0.0, dov), axis=1, keepdims=True)], axis=0)

        def block(j, carry, gcarry, acc, mask):
            off = pl.multiple_of(j * tq, tq)
            kb = k_ref[pl.ds(off, tq), :]
            vb = v_ref[pl.ds(off, tq), :]
            zs, spb, sp_sum, w = _sb_scores(qh, kb, mask)
            ab = _sb_weights(zs, spb, tri, carry)

            def grads(ab_c, da_c):
                g = ab_c.astype(F32) * da_c
                return g, g.astype(BF16), jnp.sum(g, axis=1, keepdims=True)

            g, gb, g_sum = _by_rows(grads, ab, _dot(doh, vb, _NT))
            gcarry = gcarry + g_sum

            def logit_grads(g_c, w_c, low_c, left_c):
                rinv = 1.0 / w_c
                return ((g_c * rinv - (left_c + low_c) * (1.0 - rinv)).astype(BF16),)

            dzs = _by_rows(logit_grads, g, w, _dot(gb, tri_low, _NN), dsum - gcarry)[0]
            acc = acc + _dot(dzs, kb, _NN)
            dk_ref[pl.ds(off, tq), :] += _dot(dzs, qh, _TN)
            dv_ref[pl.ds(off, tq), :] += _dot(ab, doh, _TN)
            return (carry + sp_sum, gcarry, acc)

        zero = jnp.zeros((2 * tq, 1), F32)
        st = (zero, zero, jnp.zeros((2 * tq, LANES), F32))
        st = lax.cond(qi >= 1, lambda s: block(qi - 1, *block(qi, *s, causal), None), lambda s: block(qi, *s, causal), st)
        left = jnp.maximum(qi - 1, 0)
        odd = left % 2
        st = lax.cond(jnp.logical_and(odd == 1, _sb_live(st[0])), lambda s: block(qi - 2, *s, None), lambda s: s, st)

        def pair(s):
            j = qi - 2 - odd - 2 * s[0]
            b = block(j, s[2], s[3], s[4], None)
            b = block(j - 1, b[0], b[1], b[2], None)
            return (s[0] + 1, _sb_live(b[0])) + b

        st = lax.while_loop(lambda s: jnp.logical_and(s[0] < left // 2, s[1]), pair, (jnp.int32(0), _sb_live(st[0])) + st)[2:]
        dq_ref[...] = (jnp.where(m0, st[2][:tq], st[2][tq:]) * scale).astype(BF16)

    W = heads * HEAD_DIM
    qspec = pl.BlockSpec((tq, LANES), lambda hp, i: (i, hp))
    seq = lambda off: pl.BlockSpec((S, LANES), lambda hp, i: (0, off + hp))
    ops = [p, kv, kv, o32, dy] + ([dk_in, dv_in] if has_in else [])
    return pl.pallas_call(
        body, name=name,
        out_shape=(jax.ShapeDtypeStruct((S, W), BF16), jax.ShapeDtypeStruct((S, W), F32), jax.ShapeDtypeStruct((S, W), F32)),
        grid=(npair, S // tq),
        in_specs=[qspec, seq(0), seq(npair), qspec, qspec] + ([seq(0), seq(0)] if has_in else []),
        out_specs=(qspec, seq(0), seq(0)),
        compiler_params=_cparams(("parallel", "arbitrary")),
    )(*ops)


def _ffn_up(h, wg, wu, *, name, ride=None):
    S, D = h.shape
    F = wg.shape[0]
    tm = _pick(S, (512, 256))
    tn = _pick(F, (1408, 1024, 512, 256, 128))

    def body(h_ref, g_ref, u_ref, act_ref, silu_ref, uds_ref):
        hv = h_ref[...]
        g = _dot(hv, g_ref[...], _NT)
        u = _dot(hv, u_ref[...], _NT)
        s = jax.nn.sigmoid(g)
        silu = g * s
        act_ref[...] = (silu * u).astype(BF16)
        silu_ref[...] = silu.astype(BF16)
        uds_ref[...] = (u * (s + silu * (1.0 - s))).astype(BF16)

    wspec = pl.BlockSpec((tn, D), lambda j, i: (j, 0))
    ospec = pl.BlockSpec((tm, tn), lambda j, i: (i, j))
    out = jax.ShapeDtypeStruct((S, F), BF16)
    grid = (F // tn, S // tm)

    def edges():
        j, i = pl.program_id(0), pl.program_id(1)
        return jnp.logical_and(j == 0, i == 0), jnp.logical_and(j == grid[0] - 1, i == grid[1] - 1)

    return _ride_call(
        body, ride, edges, name=name, out_shape=(out, out, out), grid=grid,
        in_specs=[pl.BlockSpec((tm, D), lambda j, i: (i, 0)), wspec, wspec], out_specs=(ospec, ospec, ospec),
        scratch_shapes=[], compiler_params=_cparams(("parallel", "parallel")), operands=[h, wg, wu])


def _ffn_down_bwd(dx, wd, silu, uds, *, name):
    S, D = dx.shape
    F = wd.shape[0]
    tm = _pick(S, (512, 256))
    tn = _pick(F, (1408, 1024, 512, 256, 128))

    def body(dx_ref, w_ref, silu_ref, uds_ref, dg_ref, du_ref):
        da = _dot(dx_ref[...], w_ref[...], _NT)
        dg_ref[...] = (da * uds_ref[...].astype(F32)).astype(BF16)
        du_ref[...] = (da * silu_ref[...].astype(F32)).astype(BF16)

    ospec = pl.BlockSpec((tm, tn), lambda j, i: (i, j))
    out = jax.ShapeDtypeStruct((S, F), BF16)
    return pl.pallas_call(
        body, name=name, out_shape=(out, out), grid=(F // tn, S // tm),
        in_specs=[pl.BlockSpec((tm, D), lambda j, i: (i, 0)), pl.BlockSpec((tn, D), lambda j, i: (j, 0)), ospec, ospec],
        out_specs=(ospec, ospec), compiler_params=_cparams(("parallel", "parallel")),
    )(dx, wd, silu, uds)


def _adamw(w, g, m, v, *, name):
    R, C = w.shape
    tr = R
    for cand in (1024, 512, 256, 128, 64, 32, 16, 8):
        if R % cand == 0 and cand * C * 4 <= (1 << 20):
            tr = cand
            break
    bc1 = 1.0 - ADAM_B1 ** ADAM_STEP
    bc2 = 1.0 - ADAM_B2 ** ADAM_STEP

    def body(w_ref, g_ref, m_ref, v_ref, d_ref, nm_ref, nv_ref):
        gv = g_ref[...]
        nm = ADAM_B1 * m_ref[...] + (1.0 - ADAM_B1) * gv
        nv = ADAM_B2 * v_ref[...] + (1.0 - ADAM_B2) * (gv * gv)
        nm_ref[...] = nm
        nv_ref[...] = nv
        d_ref[...] = -ADAM_LR * ((nm / bc1) / (jnp.sqrt(nv / bc2) + ADAM_EPS) + ADAM_WD * w_ref[...])

    blk = pl.BlockSpec((tr, C), lambda i: (i, 0))
    out = jax.ShapeDtypeStruct((R, C), F32)
    return pl.pallas_call(body, name=name, out_shape=(out, out, out), grid=(R // tr,), in_specs=[blk] * 4,
                          out_specs=(blk, blk, blk), compiler_params=_cparams(("parallel",)))(w, g, m, v)


def _place():
    x, y, c = lax.axis_index("x"), lax.axis_index("y"), lax.axis_index("c")
    return x, y, c


def _all_gather_weights(shards, *, name):
    n = len(shards)

    def body(*refs):
        sh, full = refs[:n], refs[n:2 * n]
        send_sems, recv_sems, local_sems = refs[2 * n:]
        x, y, c = _place()
        me, sibling = (x, y, c), (x, y, 1 - c)
        chips = [(1 - x, y), (x, 1 - y), (1 - x, 1 - y)]

        def rows(t, px, py, pc):
            r = sh[t].shape[1]
            return full[t].at[:, pl.ds(pl.multiple_of((4 * px + 2 * py + pc) * r, BF16_ROWS), r), :]

        def copy(t, k, block, to, src=None):
            return pltpu.make_async_remote_copy(
                src_ref=rows(t, *block) if src is None else src, dst_ref=rows(t, *block),
                send_sem=send_sems.at[7 * t + k], recv_sem=recv_sems.at[7 * t + k], device_id=to, device_id_type=MESH)

        started = []
        for t in range(n):
            mine = pltpu.make_async_copy(sh[t], rows(t, *me), local_sems.at[t])
            mine.start()
            started.append(mine)
        sends = []
        for t in range(n):
            first = [copy(t, 0, me, sibling, src=sh[t])]
            first += [copy(t, 1 + j, me, (*chip, c), src=sh[t]) for j, chip in enumerate(chips)]
            for cp in first:
                cp.start()
            sends += first
        for t in range(n):
            for j, chip in enumerate(chips):
                copy(t, 1 + j, (*chip, c), me).wait_recv()
                fwd = copy(t, 4 + j, (*chip, c), sibling)
                fwd.start()
                sends.append(fwd)
        for t in range(n):
            copy(t, 0, sibling, me).wait_recv()
            for j, chip in enumerate(chips):
                copy(t, 4 + j, (*chip, 1 - c), me).wait_recv()
        for cp in sends:
            cp.wait_send()
        for cp in started:
            cp.wait()

    out_shape = [jax.ShapeDtypeStruct((s.shape[0], N_DEV * s.shape[1], s.shape[2]), s.dtype) for s in shards]
    return pl.pallas_call(
        body, name=name, out_shape=out_shape, in_specs=[ANY] * n, out_specs=[ANY] * n,
        scratch_shapes=[pltpu.SemaphoreType.DMA((7 * n,)), pltpu.SemaphoreType.DMA((7 * n,)), pltpu.SemaphoreType.DMA((n,))],
    )(*shards)


def _whole(ref_a, ref_b, send_sem, recv_sem, me):
    return pltpu.make_async_remote_copy(src_ref=ref_a, dst_ref=ref_b, send_sem=send_sem, recv_sem=recv_sem,
                                        device_id=me, device_id_type=MESH)


def _sibling_ride(grads):
    n = len(grads)
    lands = [lax.empty((4, s.shape[0], s.shape[3], s.shape[4]), s.dtype) for s in grads]

    def start(g, land, sems):
        x, y, c = _place()
        for t in range(n):
            for k in range(4):
                pltpu.make_async_remote_copy(
                    src_ref=g[t].at[:, k, 1 - c], dst_ref=land[t].at[k], send_sem=sems[0].at[t], recv_sem=sems[1].at[t],
                    device_id=(x, y, 1 - c), device_id_type=MESH).start()

    def wait(g, land, sems):
        x, y, c = _place()
        for t in range(n):
            w = _whole(land[t], land[t], sems[0].at[t], sems[1].at[t], (x, y, c))
            w.wait_send()
            w.wait_recv()

    return _Ride(grads, lands, start, wait, [(n,), (n,)])


def _carry_alone(ride, *, name):
    def body(o_ref):
        o_ref[...] = jnp.zeros_like(o_ref)

    one = lambda: (pl.program_id(0) == 0, pl.program_id(0) == 0)
    _, landed = _ride_call(body, ride, one, name=name, out_shape=[jax.ShapeDtypeStruct((8, LANES), F32)], grid=(1,),
                           in_specs=[], out_specs=[pl.BlockSpec((8, LANES), lambda i: (0, 0))], scratch_shapes=[],
                           compiler_params=_cparams(("arbitrary",)), operands=[])
    return landed


def _chips_ride(sums):
    n = len(sums)
    lands = [lax.empty((3,) + s.shape[1:], s.dtype) for s in sums]

    def start(s, land, sems):
        x, y, c = _place()
        for t in range(n):
            for j, (px, py) in enumerate([(1 - x, y), (x, 1 - y), (1 - x, 1 - y)]):
                pltpu.make_async_remote_copy(
                    src_ref=s[t].at[2 * px + py], dst_ref=land[t].at[j], send_sem=sems[0].at[t], recv_sem=sems[1].at[t],
                    device_id=(px, py, c), device_id_type=MESH).start()

    def wait(s, land, sems):
        x, y, c = _place()
        for t in range(n):
            w = _whole(land[t], land[t], sems[0].at[t], sems[1].at[t], (x, y, c))
            w.wait_send()
            w.wait_recv()

    return _Ride(sums, lands, start, wait, [(n,), (n,)])


def _gather_ride_1(shards):
    n = len(shards)
    layer = [l for _, l in shards]
    fulls = [lax.empty((N_DEV * s.shape[1], s.shape[2]), s.dtype) for s, _ in shards]

    def rows(full, r, px, py, pc):
        return full.at[pl.ds(pl.multiple_of((4 * px + 2 * py + pc) * r, BF16_ROWS), r), :]

    def start(sh, full, sems):
        x, y, c = _place()
        for t in range(n):
            shard = sh[t].at[layer[t]]
            mine = rows(full[t], shard.shape[0], x, y, c)
            pltpu.make_async_copy(shard, mine, sems[2].at[t]).start()
            for peer in [(x, y, 1 - c), (1 - x, y, c), (x, 1 - y, c), (1 - x, 1 - y, c)]:
                pltpu.make_async_remote_copy(src_ref=shard, dst_ref=mine, send_sem=sems[0].at[t], recv_sem=sems[1].at[t],
                                             device_id=peer, device_id_type=MESH).start()

    def wait(sh, full, sems):
        x, y, c = _place()
        for t in range(n):
            shard = sh[t].at[layer[t]]
            r = shard.shape[0]
            pltpu.make_async_copy(shard, rows(full[t], r, x, y, c), sems[2].at[t]).wait()
            four = full[t].at[pl.ds(0, 4 * r), :]
            w = _whole(four, four, sems[0].at[t], sems[1].at[t], (x, y, c))
            w.wait_send()
            w.wait_recv()

    return _Ride([s for s, _ in shards], fulls, start, wait, [(n,), (n,), (n,)])


def _gather_ride_2(fulls):
    n = len(fulls)

    def start(_, full, sems):
        x, y, c = _place()
        for t in range(n):
            r = full[t].shape[0] // N_DEV
            for px, py in [(1 - x, y), (x, 1 - y), (1 - x, 1 - y)]:
                block = full[t].at[pl.ds(pl.multiple_of((4 * px + 2 * py + c) * r, BF16_ROWS), r), :]
                pltpu.make_async_remote_copy(src_ref=block, dst_ref=block, send_sem=sems[0].at[t], recv_sem=sems[1].at[t],
                                             device_id=(x, y, 1 - c), device_id_type=MESH).start()

    def wait(_, full, sems):
        x, y, c = _place()
        for t in range(n):
            three = full[t].at[pl.ds(0, 3 * (full[t].shape[0] // N_DEV)), :]
            w = _whole(three, three, sems[0].at[t], sems[1].at[t], (x, y, c))
            w.wait_send()
            w.wait_recv()

    return _Ride([], fulls, start, wait, [(n,), (n,)])


def _join_rides(rides):
    rides = [r for r in rides if r is not None]
    if len(rides) <= 1:
        return rides[0] if rides else None

    def parts(src, dst, sems):
        so = do = mo = 0
        for r in rides:
            yield r, src[so:so + len(r.srcs)], dst[do:do + len(r.dsts)], sems[mo:mo + len(r.sems)]
            so, do, mo = so + len(r.srcs), do + len(r.dsts), mo + len(r.sems)

    def start(src, dst, sems):
        for r, s, d, m in parts(src, dst, sems):
            r.start(s, d, m)

    def wait(src, dst, sems):
        for r, s, d, m in parts(src, dst, sems):
            r.wait(s, d, m)

    return _Ride([a for r in rides for a in r.srcs], [a for r in rides for a in r.dsts], start, wait,
                 [m for r in rides for m in r.sems])


def _chip_sum(g, land, core, *, name):
    L, _, _, r, C = g.shape

    def body(core_ref, g_ref, l_ref, o_ref):
        o_ref[...] = (g_ref[...].astype(F32) + l_ref[...].astype(F32)).astype(BF16)

    grid_spec = pltpu.PrefetchScalarGridSpec(
        num_scalar_prefetch=1, grid=(4, L),
        in_specs=[pl.BlockSpec((None, None, None, r, C), lambda k, l, core_ref: (l, k, core_ref[0], 0, 0)),
                  pl.BlockSpec((None, None, r, C), lambda k, l, core_ref: (k, l, 0, 0))],
        out_specs=pl.BlockSpec((None, None, r, C), lambda k, l, core_ref: (k, l, 0, 0)))
    return pl.pallas_call(body, name=name, out_shape=jax.ShapeDtypeStruct((4, L, r, C), BF16), grid_spec=grid_spec,
                          compiler_params=_cparams(("parallel", "parallel")))(core, g, land)


def _final_sum(sums, land, chip, into, layer, *, name):
    _, _, r, C = sums.shape

    def body(chip_ref, s_ref, a_ref, b_ref, c_ref, _, o_ref):
        o_ref[...] = ((s_ref[...].astype(F32) + a_ref[...].astype(F32)) + b_ref[...].astype(F32)) + c_ref[...].astype(F32)

    slot = lambda j: pl.BlockSpec((None, None, r, C), lambda i, chip_ref: (j, 0, 0, 0))
    grid_spec = pltpu.PrefetchScalarGridSpec(
        num_scalar_prefetch=1, grid=(1,),
        in_specs=[pl.BlockSpec((None, None, r, C), lambda i, chip_ref: (chip_ref[0], 0, 0, 0)), slot(0), slot(1), slot(2), ANY],
        out_specs=pl.BlockSpec((None, r, C), lambda i, chip_ref: (layer, 0, 0)))
    return pl.pallas_call(body, name=name, out_shape=jax.ShapeDtypeStruct(into.shape, F32), grid_spec=grid_spec,
                          input_output_aliases={5: 0}, compiler_params=_cparams(("arbitrary",)))(chip, sums, land, land, land, into)


def _exchange(v, reduce, *, name):
    R, C = v.shape

    def body(v_ref, o_ref, *scratch):
        if reduce:
            buf, send_sems, recv_sems = scratch
        else:
            buf = o_ref
            send_sems, recv_sems = scratch
        x, y, c = _place()
        me = 4 * x + 2 * y + c
        buf[me] = v_ref[...]
        copies = []
        for k in range(1, N_DEV):
            kx, ky, kc = (k >> 2) & 1, (k >> 1) & 1, k & 1
            peer = (1 - x if kx else x, 1 - y if ky else y, 1 - c if kc else c)
            cp = pltpu.make_async_remote_copy(src_ref=v_ref, dst_ref=buf.at[me], send_sem=send_sems.at[k - 1],
                                              recv_sem=recv_sems.at[k - 1], device_id=peer, device_id_type=MESH)
            cp.start()
            copies.append(cp)
        for cp in copies:
            cp.wait_recv()
        for cp in copies:
            cp.wait_send()
        if reduce:
            acc = buf[0]
            for d in range(1, N_DEV):
                acc = acc + buf[d]
            o_ref[...] = acc

    sems = [pltpu.SemaphoreType.DMA((N_DEV - 1,)), pltpu.SemaphoreType.DMA((N_DEV - 1,))]
    vm = pl.BlockSpec(memory_space=pltpu.VMEM)
    if reduce:
        return pl.pallas_call(body, name=name, out_shape=jax.ShapeDtypeStruct((R, C), F32), in_specs=[vm], out_specs=vm,
                              scratch_shapes=[pltpu.VMEM((N_DEV, R, C), F32)] + sems)(v)
    return pl.pallas_call(body, name=name, out_shape=jax.ShapeDtypeStruct((N_DEV, R, C), F32), in_specs=[vm], out_specs=vm,
                          scratch_shapes=sems)(v)


def _local_step(x, mem, target, norms, conv_w, depth, n_a, get_w, next_ride, ride_done, grad_ready, grad_ride, grad_landed):
    S, D = x.shape
    main = D - MEM_WIDTH
    heads = main // HEAD_DIM
    row = lambda v: v.reshape(1, D)

    mem_n = _rmsnorm(mem, row(norms["mem_norm"]), name="mem_norm")
    saved = []
    kv = hk = x_kv = w_kv = None
    def carry_mm(*args, **kwargs):
        ride = next_ride()
        if ride is None:
            return _mm(*args, **kwargs)
        out, landed = _mm(*args, ride=ride, **kwargs)
        ride_done(landed)
        return out

    for i in range(depth):
        W = functools.partial(get_w, i)
        st = {"x": x}
        h = _rmsnorm(x, row(norms["mix_norm"][i]), name=f"mix_norm{i}")
        mkv = _mm(mem_n, W("mkv"), "nn", BF16, name=f"mkv{i}")
        if i < n_a:
            p = carry_mm(h, W("a"), "nt", BF16, name=f"a_in{i}")
            y_main = _conv_fwd(p, conv_w[i], main, name=f"conv{i}")
            qblk = 3 * main // MEM_WIDTH
        else:
            p = carry_mm(h, W("b"), "nn", BF16, name=f"b_in{i}")
            y_main, st["o32"] = _sb_fwd(p, kv, heads, name=f"sb{i}")
            qblk = main // MEM_WIDTH
        y = _memattn_fwd(p, qblk, mkv, y_main, name=f"memattn{i}")
        xm = carry_mm(y, W("o"), "nn", F32, residual=x, name=f"w_o{i}")
        h2 = _rmsnorm(xm, row(norms["ffn_norm"][i]), name=f"ffn_norm{i}")
        (act, silu, uds), landed = _ffn_up(h2, W("g"), W("u"), name=f"ffn_up{i}", ride=next_ride())
        ride_done(landed)
        x = carry_mm(act, W("d"), "nn", F32, residual=xm, name=f"w_down{i}")
        st.update(h=h, mkv=mkv, p=p, qblk=qblk, y=y, xm=xm, h2=h2, silu=silu, uds=uds, act=act)
        saved.append(st)
        if i == n_a - 1:
            x_kv, w_kv = x, W("kv")
            hk = _rmsnorm(x, row(norms["kv_norm"]), name="kv_norm")
            kv = _mm(hk, w_kv, "nt", BF16, name="w_kv")

    dx, dxb, dg_final, loss = _loss_head(x, row(norms["final_norm"]), target, name="loss_head")

    dg_mix, dg_ffn, dconv = [None] * depth, [None] * depth, [None] * n_a
    dmem_n = dk = dv = dg_kv = g_kv = None
    def carry_back(*args, **kwargs):
        ride = grad_ride()
        if ride is None:
            return _mm(*args, **kwargs)
        out, landed = _mm(*args, ride=ride, **kwargs)
        grad_landed(landed)
        return out

    def carry_back_norm(*args, **kwargs):
        outs, landed = _mm_norm_bwd(*args, ride=grad_ride(), **kwargs)
        grad_landed(landed)
        return outs

    for i in reversed(range(depth)):
        st = saved[i]
        W, key = functools.partial(get_w, i), dict(_layer_keys(i, n_a))
        dgate, dup = _ffn_down_bwd(dxb, W("d"), st["silu"], st["uds"], name=f"ffn_down_bwd{i}")
        grad_ready(("d", key["d"]), carry_back(st["act"], dxb, "tn", BF16, name=f"g_w_down{i}"))
        grad_ready(("g", key["g"]), carry_back(dgate, st["h2"], "tn", BF16, name=f"g_w_gate{i}"))
        grad_ready(("u", key["u"]), carry_back(dup, st["h2"], "tn", BF16, name=f"g_w_up{i}"))
        dx, dxb, dg_ffn[i] = carry_back_norm([(dgate, W("g")), (dup, W("u"))], "nn", st["xm"], row(norms["ffn_norm"][i]), dx,
                                             name=f"d_h2_{i}")
        dy = _mm(dxb, W("o"), "nt", BF16, name=f"d_y{i}")
        grad_ready(("o", key["o"]), _mm(st["y"], dxb, "tn", BF16, name=f"g_w_o{i}"))
        dqmem, dmk, dmv = _memattn_bwd(st["p"], st["qblk"], st["mkv"], dy, main // MEM_WIDTH, name=f"memattn_bwd{i}")
        dmkv = jnp.concatenate([dmk, dmv], axis=1)
        grad_ready(("mkv", key["mkv"]), _mm(mem_n, dmkv, "tn", BF16, name=f"g_w_mem_kv{i}"))
        dmem_n = _mm(dmkv, W("mkv"), "nt", F32, residual=dmem_n, name=f"d_mem_n{i}")
        if i < n_a:
            db, dc, du, dconv[i] = _conv_bwd(st["p"], conv_w[i], dy, main, name=f"conv_bwd{i}")
            dp = jnp.concatenate([db, dc, du, dqmem], axis=1)
            grad_ready(("a", key["a"]), carry_back(dp, st["h"], "tn", BF16, name=f"g_a_in{i}"))
            w_in, form = W("a"), "nn"
        else:
            dq, dk, dv = _sb_bwd(st["p"], kv, st["o32"], dy, heads, dk, dv, name=f"sb_bwd{i}")
            dp = jnp.concatenate([dq, dqmem], axis=1)
            grad_ready(("b", key["b"]), carry_back(st["h"], dp, "tn", BF16, name=f"g_b_in{i}"))
            w_in, form = W("b"), "nt"
        dx, dxb, dg_mix[i] = carry_back_norm([(dp, w_in)], form, st["x"], row(norms["mix_norm"][i]), dx, name=f"d_h{i}")
        if i == n_a:
            dkv = jnp.concatenate([dk, dv], axis=1).astype(BF16)
            grad_ready(("kv", 0), carry_back(dkv, hk, "tn", BF16, name="g_w_kv"))
            dx, dxb, dg_kv = carry_back_norm([(dkv, w_kv)], "nn", x_kv, row(norms["kv_norm"]), dx, name="d_hk")
    _, _, dg_mem = _rmsnorm_bwd(mem, row(norms["mem_norm"]), dmem_n, None, name="mem_norm_bwd")

    small = {"mix_norm": jnp.concatenate(dg_mix, axis=0), "ffn_norm": jnp.concatenate(dg_ffn, axis=0), "kv_norm": dg_kv[0],
             "mem_norm": dg_mem[0], "final_norm": dg_final[0], "conv_w": jnp.stack(dconv, axis=0)}
    return loss, dx, small


_COL_SHARDED = ("a", "kv", "g", "u")
_NAMES = {"a": "a_in", "kv": "w_kv_shared", "g": "w_gate", "u": "w_up", "b": "b_in", "o": "w_o", "d": "w_down", "mkv": "w_mem_kv"}
_ORDER = ("a", "kv", "g", "u", "d", "b", "o", "mkv")
_WEIGHTS = ("mix_norm", "a_in", "conv_w", "b_in", "kv_norm", "w_kv_shared", "w_mem_kv", "w_o", "ffn_norm", "w_gate", "w_up",
            "w_down", "mem_norm", "final_norm")


def _layer_keys(i, n_a):
    keys = [("a", i) if i < n_a else ("b", i - n_a), ("g", i), ("u", i), ("d", i), ("o", i), ("mkv", i)]
    return keys + [("kv", 0)] if i == n_a - 1 else keys


def _gather_groups(i, n_a):
    first, rest = _layer_keys(i, n_a)[0], dict(_layer_keys(i, n_a)[1:])
    small = [(k, rest[k]) for k in ("o", "mkv", "kv") if k in rest]
    return [[first], small, [("g", rest["g"]), ("u", rest["u"])], [("d", rest["d"])]]


def _canonical(key, w):
    w3 = w if w.ndim == 3 else w[None]
    if key in _COL_SHARDED:
        w3 = jnp.transpose(w3, (0, 2, 1))
    return w3


def _uncanonical(key, g3, like):
    if key in _COL_SHARDED:
        g3 = jnp.transpose(g3, (0, 2, 1))
    return g3.reshape(like.shape)


def _pad_rows(flat, C):
    n = flat.shape[0]
    rows = -(-n // C)
    return jnp.pad(flat, (0, rows * C - n)).reshape(rows, C)


def kernel(x, mem, mix_norm, a_in, conv_w, b_in, kv_norm, w_kv_shared, w_mem_kv, w_o, ffn_norm, w_gate, w_up, w_down, mem_norm, final_norm, loss_target, m_mix_norm, m_a_in, m_conv_w, m_b_in, m_kv_norm, m_w_kv_shared, m_w_mem_kv, m_w_o, m_ffn_norm, m_w_gate, m_w_up, m_w_down, m_mem_norm, m_final_norm, v_mix_norm, v_a_in, v_conv_w, v_b_in, v_kv_norm, v_w_kv_shared, v_w_mem_kv, v_w_o, v_ffn_norm, v_w_gate, v_w_up, v_w_down, v_mem_norm, v_final_norm):
    weights = dict(mix_norm=mix_norm, a_in=a_in, conv_w=conv_w, b_in=b_in, kv_norm=kv_norm, w_kv_shared=w_kv_shared,
                   w_mem_kv=w_mem_kv, w_o=w_o, ffn_norm=ffn_norm, w_gate=w_gate, w_up=w_up, w_down=w_down,
                   mem_norm=mem_norm, final_norm=final_norm)
    moments_m = dict(mix_norm=m_mix_norm, a_in=m_a_in, conv_w=m_conv_w, b_in=m_b_in, kv_norm=m_kv_norm,
                     w_kv_shared=m_w_kv_shared, w_mem_kv=m_w_mem_kv, w_o=m_w_o, ffn_norm=m_ffn_norm, w_gate=m_w_gate,
                     w_up=m_w_up, w_down=m_w_down, mem_norm=m_mem_norm, final_norm=m_final_norm)
    moments_v = dict(mix_norm=v_mix_norm, a_in=v_a_in, conv_w=v_conv_w, b_in=v_b_in, kv_norm=v_kv_norm,
                     w_kv_shared=v_w_kv_shared, w_mem_kv=v_w_mem_kv, w_o=v_w_o, ffn_norm=v_ffn_norm, w_gate=v_w_gate,
                     w_up=v_w_up, w_down=v_w_down, mem_norm=v_mem_norm, final_norm=v_final_norm)
    D = x.shape[-1]
    depth, n_a = w_o.shape[0], a_in.shape[0]
    xi, yi, ci = _place()
    me = 4 * xi + 2 * yi + ci
    core = ci.reshape(1).astype(jnp.int32)
    chip = (2 * xi + yi).reshape(1).astype(jnp.int32)

    cw_shape = conv_w.shape
    cw_rows = _pad_rows(conv_w.reshape(-1), D)
    cw_rows = jnp.pad(cw_rows, ((0, 8 - cw_rows.shape[0]), (0, 0)))
    cw_gathered = _exchange(cw_rows, False, name="gather_conv_w")
    n_cw = cw_shape[0] * cw_shape[1] * cw_shape[2]
    cw_all = cw_gathered.reshape(N_DEV, -1)[:, :n_cw].reshape((N_DEV,) + cw_shape)
    conv_full = jnp.transpose(cw_all, (1, 2, 0, 3)).reshape(cw_shape[0], cw_shape[1], N_DEV * cw_shape[2])

    shard3 = {k: _canonical(k, weights[_NAMES[k]]).astype(BF16) for k in _ORDER}
    keys0 = _layer_keys(0, n_a)
    fulls0 = _all_gather_weights([shard3[k][l][None] for k, l in keys0], name="all_gather_layer0")
    full = {kl: f[0] for kl, f in zip(keys0, fulls0)}

    groups = [grp for i in range(1, depth) for grp in _gather_groups(i, n_a)]
    carried, riding = [0], []

    def next_ride():
        n = carried[0]
        carried[0] += 1
        second = groups[n - 1] if 1 <= n <= len(groups) else []
        first = groups[n] if n < len(groups) else []
        if not second + first:
            return None
        riding.append(second + first)
        return _join_rides([_gather_ride_2([full[kl] for kl in second]) if second else None,
                            _gather_ride_1([(shard3[k], l) for k, l in first]) if first else None])

    def ride_done(landed):
        if landed:
            full.update(zip(riding.pop(), landed))

    def get_w(i, key):
        return full[(key, dict(_layer_keys(i, n_a))[key])]

    fresh, summed, reduced, travelling = [], [], {}, []

    def grad_ready(kl, g):
        fresh.append((kl, g.reshape(1, 4, 2, g.shape[0] // N_DEV, g.shape[1])))

    def grad_ride():
        if not fresh + summed:
            return None
        travelling.append((list(summed), list(fresh)))
        ride = _join_rides([_chips_ride([s for _, s in summed]) if summed else None,
                            _sibling_ride([g for _, g in fresh]) if fresh else None])
        summed.clear()
        fresh.clear()
        return ride

    def grad_landed(landed):
        if not landed:
            return
        between_chips, to_sibling = travelling.pop()
        for (kl, s), land in zip(between_chips, landed):
            reduced[kl] = (s, land)
        for (kl, g), land in zip(to_sibling, landed[len(between_chips):]):
            summed.append((kl, _chip_sum(g, land, core, name=f"chip_sum_{kl[0]}{kl[1]}")))

    norms = {k: weights[k] for k in ("mix_norm", "ffn_norm", "kv_norm", "mem_norm", "final_norm")}
    loss, grad_x, small = _local_step(x[0], mem[0], loss_target[0], norms, conv_full, depth, n_a, get_w, next_ride, ride_done,
                                      grad_ready, grad_ride, grad_landed)
    for tail in range(2):
        ride = grad_ride()
        if ride is not None:
            grad_landed(_carry_alone(ride, name=f"reduce_scatter_tail{tail}"))

    stacks = {k: lax.empty(shard3[k].shape, F32) for k in _ORDER}
    for (k, l), (s, land) in reduced.items():
        stacks[k] = _final_sum(s, land, chip, stacks[k], l, name=f"final_sum_{k}{l}")
    grads = {_NAMES[k]: _uncanonical(k, stacks[k], weights[_NAMES[k]]) for k in _ORDER}

    order = ("mix_norm", "ffn_norm", "kv_norm", "mem_norm", "final_norm", "conv_w")
    flat = jnp.concatenate([small[k].reshape(-1) for k in order] + [loss[0, :1]])
    n_flat = flat.shape[0]
    rows = _pad_rows(flat, D)
    rows = jnp.pad(rows, ((0, (-rows.shape[0]) % 8), (0, 0)))
    total = _exchange(rows, True, name="all_reduce_small").reshape(-1)[:n_flat]
    off = 0
    for k in order:
        n = small[k].size
        grads[k] = total[off:off + n].reshape(small[k].shape)
        off += n
    loss_total = total[off]
    grads["conv_w"] = lax.dynamic_slice_in_dim(grads["conv_w"], me * cw_shape[2], cw_shape[2], axis=2)

    deltas, new_m, new_v = {}, {}, {}
    for k in _WEIGHTS:
        w = weights[k]
        two = (lambda a: a.reshape(-1, a.shape[-1])) if w.ndim > 1 else (lambda a: a.reshape(1, -1))
        d, nm, nv = _adamw(two(w), two(grads[k]), two(moments_m[k]), two(moments_v[k]), name=f"adamw_{k}")
        deltas[k], new_m[k], new_v[k] = d.reshape(w.shape), nm.reshape(w.shape), nv.reshape(w.shape)

    return (loss_total, grad_x[None], *[grads[k] for k in _WEIGHTS], *[deltas[k] for k in _WEIGHTS],
            *[new_m[k] for k in _WEIGHTS], *[new_v[k] for k in _WEIGHTS])
```

```python
import functools
import math

import jax
import jax.numpy as jnp
from jax import lax
from jax.experimental import pallas as pl
from jax.experimental.pallas import tpu as pltpu

F32 = jnp.float32
BF16 = jnp.bfloat16
MESH = pl.DeviceIdType.MESH

HEAD_DIM = 64
MEM_HEADS = 4
MEM_WIDTH = MEM_HEADS * HEAD_DIM
EPS = 1e-6
LANES = 128
BF16_ROWS = 16
VMEM_LIMIT = 56 * 1024 * 1024
N_DEV = 8

ADAM_LR = 0.001
ADAM_B1 = 0.9
ADAM_B2 = 0.999
ADAM_EPS = 1e-08
ADAM_WD = 0.01
ADAM_STEP = 10

ANY = pl.BlockSpec(memory_space=pl.ANY)


def _cparams(sem=None):
    return pltpu.CompilerParams(dimension_semantics=sem, vmem_limit_bytes=VMEM_LIMIT)


def _pick(n, cands):
    for c in cands:
        if n % c == 0:
            return c
    raise ValueError(f"no tile for {n} in {cands}")


def _mm(a, b, form, out_dtype, *, name, residual=None, ride=None):
    if form == "tn":
        K, M = a.shape
    else:
        M, K = a.shape
    if form == "nt":
        N, K2 = b.shape
    else:
        K2, N = b.shape
    assert K == K2, (name, a.shape, b.shape)
    wide = (1408, 1280, 1024, 768, 512, 256, 128)
    tm = _pick(M, wide if form == "tn" else (1024, 512, 256, 128))
    tn = _pick(N, wide)
    tk = _pick(K, (1024, 1408, 1280, 768, 512, 256))
    nk = K // tk
    dims = {"nn": (((1,), (0,)), ((), ())), "nt": (((1,), (1,)), ((), ())), "tn": (((0,), (0,)), ((), ()))}[form]
    a_bytes, b_bytes = M * K * a.dtype.itemsize, N * K * b.dtype.itemsize
    n_outer = nk == 1 and (N // tn) * a_bytes + b_bytes < a_bytes + (M // tm) * b_bytes
    ij = (lambda g0, g1: (g1, g0)) if n_outer else (lambda g0, g1: (g0, g1))

    def spec(block, f):
        return pl.BlockSpec(block, lambda g0, g1, k: f(*ij(g0, g1), k))

    a_spec = spec((tk, tm), lambda i, j, k: (k, i)) if form == "tn" else spec((tm, tk), lambda i, j, k: (i, k))
    b_spec = spec((tn, tk), lambda i, j, k: (j, k)) if form == "nt" else spec((tk, tn), lambda i, j, k: (k, j))
    out_spec = spec((tm, tn), lambda i, j, k: (i, j))
    operands, in_specs = [a, b], [a_spec, b_spec]
    has_res = residual is not None
    if has_res:
        operands.append(residual)
        in_specs.append(out_spec)
    grid = (N // tn, M // tm, nk) if n_outer else (M // tm, N // tn, nk)

    def body(*refs):
        a_ref, b_ref = refs[0], refs[1]
        r_ref = refs[2] if has_res else None
        o_ref = refs[2 + int(has_res)]
        acc_ref = refs[-1]
        part = lax.dot_general(a_ref[...].astype(BF16), b_ref[...].astype(BF16), dims, preferred_element_type=F32)

        def finish(total):
            if has_res:
                total = total + r_ref[...].astype(F32)
            o_ref[...] = total.astype(out_dtype)

        if nk == 1:
            finish(part)
        else:
            k = pl.program_id(2)

            @pl.when(k == 0)
            def _():
                acc_ref[...] = part

            @pl.when(jnp.logical_and(k > 0, k < nk - 1))
            def _():
                acc_ref[...] += part

            @pl.when(k == nk - 1)
            def _():
                finish(acc_ref[...] + part)

    def edges():
        ids = [pl.program_id(d) for d in range(3)]
        first = jnp.logical_and(jnp.logical_and(ids[0] == 0, ids[1] == 0), ids[2] == 0)
        last = jnp.logical_and(jnp.logical_and(ids[0] == grid[0] - 1, ids[1] == grid[1] - 1), ids[2] == grid[2] - 1)
        return first, last

    (out,), landed = _ride_call(
        body, ride, edges, name=name, out_shape=[jax.ShapeDtypeStruct((M, N), out_dtype)], grid=grid, in_specs=in_specs,
        out_specs=[out_spec], scratch_shapes=[pltpu.VMEM((tm, tn), F32)],
        compiler_params=_cparams(("parallel", "parallel", "arbitrary")), operands=operands)
    return out if ride is None else (out, landed)


def _rmsnorm(x, g, *, name):
    R, D = x.shape
    tr = _pick(R, (512, 256))

    def body(x_ref, g_ref, o_ref):
        xv = x_ref[...]
        r = lax.rsqrt(jnp.mean(xv * xv, axis=-1, keepdims=True) + EPS)
        o_ref[...] = (xv * r * g_ref[...]).astype(BF16)

    return pl.pallas_call(
        body, name=name, out_shape=jax.ShapeDtypeStruct((R, D), BF16), grid=(R // tr,),
        in_specs=[pl.BlockSpec((tr, D), lambda i: (i, 0)), pl.BlockSpec((1, D), lambda i: (0, 0))],
        out_specs=pl.BlockSpec((tr, D), lambda i: (i, 0)), compiler_params=_cparams(("parallel",)),
    )(x, g)


def _rmsnorm_bwd(x, g, dh, dx_in, *, name):
    R, D = x.shape
    tr = _pick(R, (512, 256))
    has_in = dx_in is not None

    def body(*refs):
        x_ref, g_ref, dh_ref = refs[:3]
        dxi_ref = refs[3] if has_in else None
        dx_ref, dxb_ref, dg_ref = refs[3 + int(has_in):]
        xv = x_ref[...]
        r = lax.rsqrt(jnp.mean(xv * xv, axis=-1, keepdims=True) + EPS)
        xhat = xv * r
        dhv = dh_ref[...].astype(F32)
        dxh = dhv * g_ref[...]
        dx = r * (dxh - xhat * jnp.mean(dxh * xhat, axis=-1, keepdims=True))
        if has_in:
            dx = dx + dxi_ref[...]
        dx_ref[...] = dx
        dxb_ref[...] = dx.astype(BF16)
        part = jnp.sum(dhv * xhat, axis=0, keepdims=True)

        @pl.when(pl.program_id(0) == 0)
        def _():
            dg_ref[...] = part

        @pl.when(pl.program_id(0) > 0)
        def _():
            dg_ref[...] += part

    row = pl.BlockSpec((tr, D), lambda i: (i, 0))
    vec = pl.BlockSpec((1, D), lambda i: (0, 0))
    ops = [x, g, dh] + ([dx_in] if has_in else [])
    return pl.pallas_call(
        body, name=name,
        out_shape=(jax.ShapeDtypeStruct((R, D), F32), jax.ShapeDtypeStruct((R, D), BF16), jax.ShapeDtypeStruct((1, D), F32)),
        grid=(R // tr,), in_specs=[row, vec, row] + ([row] if has_in else []), out_specs=(row, row, vec),
        compiler_params=_cparams(("arbitrary",)),
    )(*ops)


class _Ride:
    def __init__(self, srcs, dsts, start, wait, sems):
        self.srcs, self.dsts, self.start, self.wait, self.sems = list(srcs), list(dsts), start, wait, list(sems)


def _ride_call(body, ride, edges, *, name, out_shape, grid, in_specs, out_specs, scratch_shapes, compiler_params, operands):
    n_in, n_out, n_scr = len(in_specs), len(out_specs), len(scratch_shapes)
    if ride is None:
        outs = pl.pallas_call(body, name=name, out_shape=tuple(out_shape), grid=grid, in_specs=list(in_specs),
                              out_specs=tuple(out_specs), scratch_shapes=list(scratch_shapes),
                              compiler_params=compiler_params)(*operands)
        return tuple(outs), []
    ns, nd = len(ride.srcs), len(ride.dsts)

    def riding(*refs):
        ins, rin = refs[:n_in], refs[n_in:n_in + ns + nd]
        outs = refs[n_in + ns + nd:n_in + ns + nd + n_out]
        scratch = refs[n_in + ns + 2 * nd + n_out:]
        sems = scratch[n_scr:]
        first, last = edges()

        @pl.when(first)
        def _():
            ride.start(rin[:ns], rin[ns:], sems)

        body(*ins, *outs, *scratch[:n_scr])

        @pl.when(last)
        def _():
            ride.wait(rin[:ns], rin[ns:], sems)

    outs = pl.pallas_call(
        riding, name=name, out_shape=(*out_shape, *[jax.ShapeDtypeStruct(d.shape, d.dtype) for d in ride.dsts]), grid=grid,
        in_specs=[*in_specs, *[ANY] * (ns + nd)], out_specs=(*out_specs, *[ANY] * nd),
        scratch_shapes=[*scratch_shapes, *[pltpu.SemaphoreType.DMA(s) for s in ride.sems]],
        input_output_aliases={n_in + ns + d: n_out + d for d in range(nd)}, compiler_params=compiler_params,
    )(*operands, *ride.srcs, *ride.dsts)
    return tuple(outs[:n_out]), list(outs[n_out:])


def _mm_norm_bwd(parts, form, x, g, dx_in, *, name, ride=None):
    S, K = parts[0][0].shape
    D = x.shape[1]
    tm = _pick(S, (512, 256))
    tk = _pick(K, (1024, 1408, 1280, 768, 512, 256))
    nk, P = K // tk, len(parts)
    dims = _NN if form == "nn" else _NT

    ni, nsteps = S // tm, P * nk

    def body(*refs):
        ab = refs[:2 * P]
        x_ref, g_ref, dxi_ref, dx_ref, dxb_ref, dg_ref, acc_ref = refs[2 * P:]
        k, i = pl.program_id(0), pl.program_id(1)
        rows = pl.ds(pl.multiple_of(i * tm, tm), tm)

        @pl.when(k == 0)
        def _():
            acc_ref[rows, :] = jnp.zeros((tm, D), F32)

        for p in range(P):
            @pl.when(jnp.logical_and(k >= p * nk, k < (p + 1) * nk))
            def _():
                acc_ref[rows, :] += _dot(ab[2 * p][...], ab[2 * p + 1][...], dims)

        @pl.when(k == nsteps - 1)
        def _():
            xv = x_ref[...]
            r = lax.rsqrt(jnp.mean(xv * xv, axis=-1, keepdims=True) + EPS)
            xhat = xv * r
            dhv = acc_ref[rows, :]
            dxh = dhv * g_ref[...]
            dx = r * (dxh - xhat * jnp.mean(dxh * xhat, axis=-1, keepdims=True)) + dxi_ref[...]
            dx_ref[...] = dx
            dxb_ref[...] = dx.astype(BF16)
            part = jnp.sum(dhv * xhat, axis=0, keepdims=True)

            @pl.when(i == 0)
            def _():
                dg_ref[...] = part

            @pl.when(i > 0)
            def _():
                dg_ref[...] += part

    def kk(p):
        return lambda k: jnp.clip(k - p * nk, 0, nk - 1)

    def active_row(p):
        return lambda k, i: jnp.where(jnp.logical_and(k >= p * nk, k < (p + 1) * nk), i, 0)

    in_specs, operands = [], []
    for p, (a, b) in enumerate(parts):
        in_specs.append(pl.BlockSpec((tm, tk), lambda k, i, f=kk(p), r=active_row(p): (r(k, i), f(k))))
        in_specs.append(pl.BlockSpec((tk, D), lambda k, i, f=kk(p): (f(k), 0)) if form == "nn"
                        else pl.BlockSpec((D, tk), lambda k, i, f=kk(p): (0, f(k))))
        operands += [a, b]
    row = pl.BlockSpec((tm, D), lambda k, i: (jnp.where(k == nsteps - 1, i, 0), 0))
    vec = pl.BlockSpec((1, D), lambda k, i: (0, 0))

    def edges():
        k, i = pl.program_id(0), pl.program_id(1)
        return jnp.logical_and(i == 0, k == 0), jnp.logical_and(i == ni - 1, k == nsteps - 1)

    return _ride_call(
        body, ride, edges, name=name,
        out_shape=(jax.ShapeDtypeStruct((S, D), F32), jax.ShapeDtypeStruct((S, D), BF16), jax.ShapeDtypeStruct((1, D), F32)),
        grid=(nsteps, ni), in_specs=in_specs + [row, vec, row], out_specs=(row, row, vec),
        scratch_shapes=[pltpu.VMEM((S, D), F32)], compiler_params=_cparams(("arbitrary", "arbitrary")),
        operands=[*operands, x, g, dx_in])


def _loss_head(x, g, target, *, name):
    R, D = x.shape
    tr = _pick(R, (512, 256))

    def body(x_ref, g_ref, t_ref, dx_ref, dxb_ref, dg_ref, loss_ref):
        xv = x_ref[...]
        gv = g_ref[...]
        r = lax.rsqrt(jnp.mean(xv * xv, axis=-1, keepdims=True) + EPS)
        xhat = xv * r
        err = xhat * gv - t_ref[...]
        loss = 0.5 * jnp.sum(jnp.mean(err * err, axis=-1, keepdims=True), axis=0, keepdims=True)
        dy = err * (1.0 / D)
        dxh = dy * gv
        dx = r * (dxh - xhat * jnp.mean(dxh * xhat, axis=-1, keepdims=True))
        dx_ref[...] = dx
        dxb_ref[...] = dx.astype(BF16)
        dg = jnp.sum(dy * xhat, axis=0, keepdims=True)
        lossv = jnp.broadcast_to(loss, (1, LANES))

        @pl.when(pl.program_id(0) == 0)
        def _():
            dg_ref[...] = dg
            loss_ref[...] = lossv

        @pl.when(pl.program_id(0) > 0)
        def _():
            dg_ref[...] += dg
            loss_ref[...] += lossv

    row = pl.BlockSpec((tr, D), lambda i: (i, 0))
    vec = pl.BlockSpec((1, D), lambda i: (0, 0))
    return pl.pallas_call(
        body, name=name,
        out_shape=(jax.ShapeDtypeStruct((R, D), F32), jax.ShapeDtypeStruct((R, D), BF16), jax.ShapeDtypeStruct((1, D), F32),
                   jax.ShapeDtypeStruct((1, LANES), F32)),
        grid=(R // tr,), in_specs=[row, vec, row], out_specs=(row, row, vec, pl.BlockSpec((1, LANES), lambda i: (0, 0))),
        compiler_params=_cparams(("arbitrary",)),
    )(x, g, target)


def _conv_taps(gv, S):
    t = lax.broadcasted_iota(jnp.int32, gv.shape, 0)
    g1 = jnp.where(t >= 1, pltpu.roll(gv, 1, 0), 0.0)
    g2 = jnp.where(t >= 2, pltpu.roll(gv, 2, 0), 0.0)
    return g1, g2


def _conv_fwd(p, w, main, *, name):
    S = p.shape[0]
    tc = LANES
    nb = main // tc

    def body(b_ref, c_ref, u_ref, w_ref, y_ref):
        gv = c_ref[...].astype(F32) * u_ref[...].astype(F32)
        g1, g2 = _conv_taps(gv, S)
        cv = w_ref[0:1, :] * g2 + w_ref[1:2, :] * g1 + w_ref[2:3, :] * gv
        y_ref[...] = (b_ref[...].astype(F32) * cv).astype(BF16)

    col = lambda off: pl.BlockSpec((S, tc), lambda j: (0, off + j))
    return pl.pallas_call(
        body, name=name, out_shape=jax.ShapeDtypeStruct((S, p.shape[1] - 2 * main), BF16), grid=(nb,),
        in_specs=[col(0), col(nb), col(2 * nb), pl.BlockSpec((3, tc), lambda j: (0, j))],
        out_specs=pl.BlockSpec((S, tc), lambda j: (0, j)), compiler_params=_cparams(("parallel",)),
    )(p, p, p, w)


def _conv_bwd(p, w, dy, main, *, name):
    S = p.shape[0]
    tc = LANES
    nb = main // tc

    def body(b_ref, c_ref, u_ref, w_ref, dy_ref, db_ref, dc_ref, du_ref, dw_ref):
        cvv, uv = c_ref[...].astype(F32), u_ref[...].astype(F32)
        gv = cvv * uv
        g1, g2 = _conv_taps(gv, S)
        w0, w1, w2 = w_ref[0:1, :], w_ref[1:2, :], w_ref[2:3, :]
        dyv = dy_ref[...].astype(F32)
        db_ref[...] = (dyv * (w0 * g2 + w1 * g1 + w2 * gv)).astype(BF16)
        dcv = dyv * b_ref[...].astype(F32)
        t = lax.broadcasted_iota(jnp.int32, dcv.shape, 0)
        n1 = jnp.where(t <= S - 2, pltpu.roll(dcv, S - 1, 0), 0.0)
        n2 = jnp.where(t <= S - 3, pltpu.roll(dcv, S - 2, 0), 0.0)
        dg = w2 * dcv + w1 * n1 + w0 * n2
        dc_ref[...] = (dg * uv).astype(BF16)
        du_ref[...] = (dg * cvv).astype(BF16)
        dw_ref[0:1, :] = jnp.sum(dcv * g2, axis=0, keepdims=True)
        dw_ref[1:2, :] = jnp.sum(dcv * g1, axis=0, keepdims=True)
        dw_ref[2:3, :] = jnp.sum(dcv * gv, axis=0, keepdims=True)

    col = lambda off: pl.BlockSpec((S, tc), lambda j: (0, off + j))
    out = jax.ShapeDtypeStruct((S, main), BF16)
    return pl.pallas_call(
        body, name=name, out_shape=(out, out, out, jax.ShapeDtypeStruct((3, main), F32)), grid=(nb,),
        in_specs=[col(0), col(nb), col(2 * nb), pl.BlockSpec((3, tc), lambda j: (0, j)), col(0)],
        out_specs=(col(0), col(0), col(0), pl.BlockSpec((3, tc), lambda j: (0, j))),
        compiler_params=_cparams(("parallel",)),
    )(p, p, p, w, dy)


def _head_mask(width, h):
    lane = lax.broadcasted_iota(jnp.int32, (1, width), 1)
    return jnp.logical_and(lane >= h * HEAD_DIM, lane < (h + 1) * HEAD_DIM)


_NT = (((1,), (1,)), ((), ()))
_NN = (((1,), (0,)), ((), ()))
_TN = (((0,), (0,)), ((), ()))


def _dot(a, b, dims):
    return lax.dot_general(a, b, dims, preferred_element_type=F32)


def _mem_probs(qh, kv):
    s = _dot(qh, kv, _NT) * (1.0 / math.sqrt(HEAD_DIM))
    s = s - jnp.max(s, axis=-1, keepdims=True)
    e = jnp.exp(s)
    return e / jnp.sum(e, axis=-1, keepdims=True)


def _memattn_fwd(p, qblk, mkv, into, *, name):
    S = p.shape[0]
    M = mkv.shape[0]
    W = MEM_WIDTH
    tq = _pick(S, (512, 256))
    last = into.shape[1] // W - 1

    def body(q_ref, k_ref, v_ref, _, o_ref):
        q = q_ref[...].astype(BF16)
        kv, vv = k_ref[...], v_ref[...]
        out = jnp.zeros((tq, W), F32)
        for h in range(MEM_HEADS):
            m = _head_mask(W, h)
            pr = _mem_probs(jnp.where(m, q, jnp.zeros_like(q)), kv)
            out = jnp.where(m, _dot(pr.astype(BF16), vv, _NN), out)
        o_ref[...] = out.astype(BF16)

    return pl.pallas_call(
        body, name=name, out_shape=jax.ShapeDtypeStruct(into.shape, BF16), grid=(S // tq,),
        in_specs=[pl.BlockSpec((tq, W), lambda i: (i, qblk)), pl.BlockSpec((M, W), lambda i: (0, 0)),
                  pl.BlockSpec((M, W), lambda i: (0, 1)), ANY],
        out_specs=pl.BlockSpec((tq, W), lambda i: (i, last)), input_output_aliases={3: 0},
        compiler_params=_cparams(("parallel",)),
    )(p, mkv, mkv, into)


def _memattn_bwd(p, qblk, mkv, dy, dyblk, *, name):
    S = p.shape[0]
    M = mkv.shape[0]
    W = MEM_WIDTH
    tq = _pick(S, (512, 256))
    scale = 1.0 / math.sqrt(HEAD_DIM)

    def body(q_ref, k_ref, v_ref, do_ref, dq_ref, dkv_ref, dk_acc, dv_acc):
        q = q_ref[...].astype(BF16)
        do = do_ref[...].astype(BF16)
        kv, vv = k_ref[...], v_ref[...]
        dq = jnp.zeros((tq, W), F32)
        dk = jnp.zeros((M, W), F32)
        dv = jnp.zeros((M, W), F32)
        for h in range(MEM_HEADS):
            m = _head_mask(W, h)
            qh = jnp.where(m, q, jnp.zeros_like(q))
            doh = jnp.where(m, do, jnp.zeros_like(do))
            pr = _mem_probs(qh, kv)
            dpr = _dot(doh, vv, _NT)
            ds = (pr * (dpr - jnp.sum(dpr * pr, axis=-1, keepdims=True)) * scale).astype(BF16)
            dq = jnp.where(m, _dot(ds, kv, _NN), dq)
            dk = dk + _dot(ds, qh, _TN)
            dv = dv + _dot(pr.astype(BF16), doh, _TN)
        dq_ref[...] = dq.astype(BF16)
        i = pl.program_id(0)

        @pl.when(i == 0)
        def _():
            dk_acc[...] = dk
            dv_acc[...] = dv

        @pl.when(i > 0)
        def _():
            dk_acc[...] += dk
            dv_acc[...] += dv

        @pl.when(i == S // tq - 1)
        def _():
            dkv_ref[:, :W] = dk_acc[...].astype(BF16)
            dkv_ref[:, W:] = dv_acc[...].astype(BF16)

    kspec = lambda c: pl.BlockSpec((M, W), lambda i: (0, c))
    return pl.pallas_call(
        body, name=name,
        out_shape=(jax.ShapeDtypeStruct((S, W), BF16), jax.ShapeDtypeStruct((M, 2 * W), BF16)),
        grid=(S // tq,),
        in_specs=[pl.BlockSpec((tq, W), lambda i: (i, qblk)), kspec(0), kspec(1), pl.BlockSpec((tq, W), lambda i: (i, dyblk))],
        out_specs=(pl.BlockSpec((tq, W), lambda i: (i, 0)), pl.BlockSpec((M, 2 * W), lambda i: (0, 0))),
        scratch_shapes=[pltpu.VMEM((M, W), F32), pltpu.VMEM((M, W), F32)],
        compiler_params=_cparams(("arbitrary",)),
    )(p, mkv, mkv, dy)


SB_TQ = 256
SB_CLAMP = 80.0
SB_DEAD = 110.0


SB_CHUNK = 64


def _by_rows(fn, *arrays):
    rows = next(a for a in arrays if a is not None).shape[0]
    outs = [fn(*[None if a is None else a[r0:r0 + SB_CHUNK] for a in arrays]) for r0 in range(0, rows, SB_CHUNK)]
    return tuple(jnp.concatenate(col, axis=0) for col in zip(*outs))


def _sb_scores(qh, kb, causal):
    def chain(z, mask):
        z = jnp.clip(z, -SB_CLAMP, SB_CLAMP)
        w = 1.0 + jnp.exp(z)
        sp = jnp.log(w)
        zs = z - sp
        if mask is not None:
            sp = jnp.where(mask, sp, 0.0)
            zs = jnp.where(mask, zs, -1e30)
            w = jnp.where(mask, w, 1.0)
        return zs, sp.astype(BF16), jnp.sum(sp, axis=1, keepdims=True), w

    return _by_rows(chain, _dot(qh, kb, _NT), causal)


def _sb_weights(zs, spb, tri, carry):
    return _by_rows(lambda zs_c, t_c, c_c: (jnp.exp(zs_c - (t_c + c_c)).astype(BF16),), zs, _dot(spb, tri, _NN), carry)[0]


def _sb_live(carry):
    return jnp.min(carry) <= SB_DEAD


def _stack_heads(v, m0):
    zero = jnp.zeros_like(v)
    return jnp.concatenate([jnp.where(m0, v, zero), jnp.where(m0, zero, v)], axis=0)


def _stacked_causal(tq):
    r = lax.broadcasted_iota(jnp.int32, (2 * tq, tq), 0)
    c = lax.broadcasted_iota(jnp.int32, (2 * tq, tq), 1)
    return c < jnp.where(r >= tq, r - tq, r)


def _sb_fwd(p, kv, heads, *, name):
    S = p.shape[0]
    tq = SB_TQ
    npair = heads // 2

    def body(q_ref, k_ref, v_ref, o_ref, o32_ref):
        qi = pl.program_id(1)
        r = lax.broadcasted_iota(jnp.int32, (tq, tq), 0)
        c = lax.broadcasted_iota(jnp.int32, (tq, tq), 1)
        tri = (r > c).astype(BF16)
        causal = _stacked_causal(tq)
        m0 = _head_mask(LANES, 0)
        qh = _stack_heads(q_ref[...] * jnp.asarray(1.0 / math.sqrt(HEAD_DIM), BF16), m0)

        def block(j, carry, acc, mask):
            off = pl.multiple_of(j * tq, tq)
            kb = k_ref[pl.ds(off, tq), :]
            vb = v_ref[pl.ds(off, tq), :]
            zs, spb, sp_sum, _ = _sb_scores(qh, kb, mask)
            acc = acc + _dot(_sb_weights(zs, spb, tri, carry), vb, _NN)
            return carry + sp_sum, acc

        st = (jnp.zeros((2 * tq, 1), F32), jnp.zeros((2 * tq, LANES), F32))
        st = lax.cond(qi >= 1, lambda s: block(qi - 1, *block(qi, *s, causal), None), lambda s: block(qi, *s, causal), st)
        left = jnp.maximum(qi - 1, 0)
        odd = left % 2
        st = lax.cond(jnp.logical_and(odd == 1, _sb_live(st[0])), lambda s: block(qi - 2, *s, None), lambda s: s, st)

        def pair(s):
            it, _, carry, acc = s
            j = qi - 2 - odd - 2 * it
            carry, acc = block(j, carry, acc, None)
            carry, acc = block(j - 1, carry, acc, None)
            return it + 1, _sb_live(carry), carry, acc

        _, _, carry, acc = lax.while_loop(lambda s: jnp.logical_and(s[0] < left // 2, s[1]), pair,
                                          (jnp.int32(0), _sb_live(st[0]), st[0], st[1]))
        out = jnp.where(m0, acc[:tq], acc[tq:])
        o_ref[...] = out.astype(BF16)
        o32_ref[...] = out

    W = heads * HEAD_DIM
    qspec = pl.BlockSpec((tq, LANES), lambda hp, i: (i, hp))
    return pl.pallas_call(
        body, name=name, out_shape=(jax.ShapeDtypeStruct(p.shape, BF16), jax.ShapeDtypeStruct((S, W), F32)), grid=(npair, S // tq),
        in_specs=[qspec, pl.BlockSpec((S, LANES), lambda hp, i: (0, hp)), pl.BlockSpec((S, LANES), lambda hp, i: (0, npair + hp))],
        out_specs=(qspec, qspec), compiler_params=_cparams(("parallel", "arbitrary")),
    )(p, kv, kv)


def _sb_bwd(p, kv, o32, dy, heads, dk_in, dv_in, *, name):
    S = p.shape[0]
    tq = SB_TQ
    npair = heads // 2
    has_in = dk_in is not None
    scale = 1.0 / math.sqrt(HEAD_DIM)

    def body(*refs):
        q_ref, k_ref, v_ref, o_ref, do_ref = refs[:5]
        dq_ref, dk_ref, dv_ref = refs[5 + 2 * int(has_in):]
        qi = pl.program_id(1)

        @pl.when(qi == 0)
        def _():
            if has_in:
                dk_ref[...] = refs[5][...]
                dv_ref[...] = refs[6][...]
            else:
                dk_ref[...] = jnp.zeros_like(dk_ref)
                dv_ref[...] = jnp.zeros_like(dv_ref)

        r = lax.broadcasted_iota(jnp.int32, (tq, tq), 0)
        c = lax.broadcasted_iota(jnp.int32, (tq, tq), 1)
        tri = (r > c).astype(BF16)
        tri_low = (r < c).astype(BF16)
        causal = _stacked_causal(tq)
        m0 = _head_mask(LANES, 0)
        qh = _stack_heads(q_ref[...] * jnp.asarray(scale, BF16), m0)
        do = do_ref[...]
        doh = _stack_heads(do, m0)
        dov = do.astype(F32) * o_ref[...]
        dsum = jnp.concatenate([jnp.sum(jnp.where(m0, dov, 0.0), axis=1, keepdims=True),
                                jnp.sum(jnp.where(m0, 0.0, dov), axis=1, keepdims=True)], axis=0)

        def block(j, carry, gcarry, acc, mask):
            off = pl.multiple_of(j * tq, tq)
            kb = k_ref[pl.ds(off, tq), :]
            vb = v_ref[pl.ds(off, tq), :]
            zs, spb, sp_sum, w = _sb_scores(qh, kb, mask)
            ab = _sb_weights(zs, spb, tri, carry)

            def grads(ab_c, da_c):
                g = ab_c.astype(F32) * da_c
                return g, g.astype(BF16), jnp.sum(g, axis=1, keepdims=True)

            g, gb, g_sum = _by_rows(grads, ab, _dot(doh, vb, _NT))
            gcarry = gcarry + g_sum

            def logit_grads(g_c, w_c, low_c, left_c):
                rinv = 1.0 / w_c
                return ((g_c * rinv - (left_c + low_c) * (1.0 - rinv)).astype(BF16),)

            dzs = _by_rows(logit_grads, g, w, _dot(gb, tri_low, _NN), dsum - gcarry)[0]
            acc = acc + _dot(dzs, kb, _NN)
            dk_ref[pl.ds(off, tq), :] += _dot(dzs, qh, _TN)
            dv_ref[pl.ds(off, tq), :] += _dot(ab, doh, _TN)
            return (carry + sp_sum, gcarry, acc)

        zero = jnp.zeros((2 * tq, 1), F32)
        st = (zero, zero, jnp.zeros((2 * tq, LANES), F32))
        st = lax.cond(qi >= 1, lambda s: block(qi - 1, *block(qi, *s, causal), None), lambda s: block(qi, *s, causal), st)
        left = jnp.maximum(qi - 1, 0)
        odd = left % 2
        st = lax.cond(jnp.logical_and(odd == 1, _sb_live(st[0])), lambda s: block(qi - 2, *s, None), lambda s: s, st)

        def pair(s):
            j = qi - 2 - odd - 2 * s[0]
            b = block(j, s[2], s[3], s[4], None)
            b = block(j - 1, b[0], b[1], b[2], None)
            return (s[0] + 1, _sb_live(b[0])) + b

        st = lax.while_loop(lambda s: jnp.logical_and(s[0] < left // 2, s[1]), pair, (jnp.int32(0), _sb_live(st[0])) + st)[2:]
        dq_ref[...] = (jnp.where(m0, st[2][:tq], st[2][tq:]) * scale).astype(BF16)

    W = heads * HEAD_DIM
    qspec = pl.BlockSpec((tq, LANES), lambda hp, i: (i, hp))
    seq = lambda off: pl.BlockSpec((S, LANES), lambda hp, i: (0, off + hp))
    ops = [p, kv, kv, o32, dy] + ([dk_in, dv_in] if has_in else [])
    return pl.pallas_call(
        body, name=name,
        out_shape=(jax.ShapeDtypeStruct((S, W), BF16), jax.ShapeDtypeStruct((S, W), F32), jax.ShapeDtypeStruct((S, W), F32)),
        grid=(npair, S // tq),
        in_specs=[qspec, seq(0), seq(npair), qspec, qspec] + ([seq(0), seq(0)] if has_in else []),
        out_specs=(qspec, seq(0), seq(0)),
        compiler_params=_cparams(("parallel", "arbitrary")),
    )(*ops)


def _ffn_up(h, wg, wu, *, name, ride=None):
    S, D = h.shape
    F = wg.shape[0]
    tm = _pick(S, (512, 256))
    tn = _pick(F, (1408, 1024, 512, 256, 128))

    def body(h_ref, g_ref, u_ref, act_ref, silu_ref, uds_ref):
        hv = h_ref[...]
        g = _dot(hv, g_ref[...], _NT)
        u = _dot(hv, u_ref[...], _NT)
        s = jax.nn.sigmoid(g)
        silu = g * s
        act_ref[...] = (silu * u).astype(BF16)
        silu_ref[...] = silu.astype(BF16)
        uds_ref[...] = (u * (s + silu * (1.0 - s))).astype(BF16)

    wspec = pl.BlockSpec((tn, D), lambda j, i: (j, 0))
    ospec = pl.BlockSpec((tm, tn), lambda j, i: (i, j))
    out = jax.ShapeDtypeStruct((S, F), BF16)
    grid = (F // tn, S // tm)

    def edges():
        j, i = pl.program_id(0), pl.program_id(1)
        return jnp.logical_and(j == 0, i == 0), jnp.logical_and(j == grid[0] - 1, i == grid[1] - 1)

    return _ride_call(
        body, ride, edges, name=name, out_shape=(out, out, out), grid=grid,
        in_specs=[pl.BlockSpec((tm, D), lambda j, i: (i, 0)), wspec, wspec], out_specs=(ospec, ospec, ospec),
        scratch_shapes=[], compiler_params=_cparams(("parallel", "parallel")), operands=[h, wg, wu])


def _ffn_down_bwd(dx, wd, silu, uds, *, name):
    S, D = dx.shape
    F = wd.shape[0]
    tm = _pick(S, (512, 256))
    tn = _pick(F, (1408, 1024, 512, 256, 128))

    def body(dx_ref, w_ref, silu_ref, uds_ref, dg_ref, du_ref):
        da = _dot(dx_ref[...], w_ref[...], _NT)
        dg_ref[...] = (da * uds_ref[...].astype(F32)).astype(BF16)
        du_ref[...] = (da * silu_ref[...].astype(F32)).astype(BF16)

    ospec = pl.BlockSpec((tm, tn), lambda j, i: (i, j))
    out = jax.ShapeDtypeStruct((S, F), BF16)
    return pl.pallas_call(
        body, name=name, out_shape=(out, out), grid=(F // tn, S // tm),
        in_specs=[pl.BlockSpec((tm, D), lambda j, i: (i, 0)), pl.BlockSpec((tn, D), lambda j, i: (j, 0)), ospec, ospec],
        out_specs=(ospec, ospec), compiler_params=_cparams(("parallel", "parallel")),
    )(dx, wd, silu, uds)


def _adamw(w, g, m, v, *, name):
    L, R, C = w.shape
    tr = R
    for cand in (1024, 512, 256, 128, 64, 32, 16, 8):
        if R % cand == 0 and cand * C * 4 <= (1 << 20):
            tr = cand
            break
    bc1 = 1.0 - ADAM_B1 ** ADAM_STEP
    bc2 = 1.0 - ADAM_B2 ** ADAM_STEP

    def body(w_ref, g_ref, m_ref, v_ref, d_ref, nm_ref, nv_ref):
        gv = g_ref[...]
        nm = ADAM_B1 * m_ref[...] + (1.0 - ADAM_B1) * gv
        nv = ADAM_B2 * v_ref[...] + (1.0 - ADAM_B2) * (gv * gv)
        nm_ref[...] = nm
        nv_ref[...] = nv
        d_ref[...] = -ADAM_LR * ((nm / bc1) / (jnp.sqrt(nv / bc2) + ADAM_EPS) + ADAM_WD * w_ref[...])

    blk = pl.BlockSpec((None, tr, C), lambda l, i: (l, i, 0))
    out = jax.ShapeDtypeStruct((L, R, C), F32)
    return pl.pallas_call(body, name=name, out_shape=(out, out, out), grid=(L, R // tr), in_specs=[blk] * 4,
                          out_specs=(blk, blk, blk), compiler_params=_cparams(("parallel", "parallel")))(w, g, m, v)


def _place():
    x, y, c = lax.axis_index("x"), lax.axis_index("y"), lax.axis_index("c")
    return x, y, c


def _all_gather_weights(shards, *, name):
    n = len(shards)

    def body(*refs):
        sh, full = refs[:n], refs[n:2 * n]
        send_sems, recv_sems, local_sems = refs[2 * n:]
        x, y, c = _place()
        me, sibling = (x, y, c), (x, y, 1 - c)
        chips = [(1 - x, y), (x, 1 - y), (1 - x, 1 - y)]

        def rows(t, px, py, pc):
            r = sh[t].shape[1]
            return full[t].at[:, pl.ds(pl.multiple_of((4 * px + 2 * py + pc) * r, BF16_ROWS), r), :]

        def copy(t, k, block, to, src=None):
            return pltpu.make_async_remote_copy(
                src_ref=rows(t, *block) if src is None else src, dst_ref=rows(t, *block),
                send_sem=send_sems.at[7 * t + k], recv_sem=recv_sems.at[7 * t + k], device_id=to, device_id_type=MESH)

        started = []
        for t in range(n):
            mine = pltpu.make_async_copy(sh[t], rows(t, *me), local_sems.at[t])
            mine.start()
            started.append(mine)
        sends = []
        for t in range(n):
            first = [copy(t, 0, me, sibling, src=sh[t])]
            first += [copy(t, 1 + j, me, (*chip, c), src=sh[t]) for j, chip in enumerate(chips)]
            for cp in first:
                cp.start()
            sends += first
        for t in range(n):
            for j, chip in enumerate(chips):
                copy(t, 1 + j, (*chip, c), me).wait_recv()
                fwd = copy(t, 4 + j, (*chip, c), sibling)
                fwd.start()
                sends.append(fwd)
        for t in range(n):
            copy(t, 0, sibling, me).wait_recv()
            for j, chip in enumerate(chips):
                copy(t, 4 + j, (*chip, 1 - c), me).wait_recv()
        for cp in sends:
            cp.wait_send()
        for cp in started:
            cp.wait()

    out_shape = [jax.ShapeDtypeStruct((s.shape[0], N_DEV * s.shape[1], s.shape[2]), s.dtype) for s in shards]
    return pl.pallas_call(
        body, name=name, out_shape=out_shape, in_specs=[ANY] * n, out_specs=[ANY] * n,
        scratch_shapes=[pltpu.SemaphoreType.DMA((7 * n,)), pltpu.SemaphoreType.DMA((7 * n,)), pltpu.SemaphoreType.DMA((n,))],
    )(*shards)


def _whole(ref_a, ref_b, send_sem, recv_sem, me):
    return pltpu.make_async_remote_copy(src_ref=ref_a, dst_ref=ref_b, send_sem=send_sem, recv_sem=recv_sem,
                                        device_id=me, device_id_type=MESH)


def _sibling_ride(grads):
    n = len(grads)
    lands = [lax.empty((4, s.shape[0], s.shape[3], s.shape[4]), s.dtype) for s in grads]

    def start(g, land, sems):
        x, y, c = _place()
        for t in range(n):
            for k in range(4):
                pltpu.make_async_remote_copy(
                    src_ref=g[t].at[:, k, 1 - c], dst_ref=land[t].at[k], send_sem=sems[0].at[t], recv_sem=sems[1].at[t],
                    device_id=(x, y, 1 - c), device_id_type=MESH).start()

    def wait(g, land, sems):
        x, y, c = _place()
        for t in range(n):
            w = _whole(land[t], land[t], sems[0].at[t], sems[1].at[t], (x, y, c))
            w.wait_send()
            w.wait_recv()

    return _Ride(grads, lands, start, wait, [(n,), (n,)])


def _carry_alone(ride, *, name):
    def body(o_ref):
        o_ref[...] = jnp.zeros_like(o_ref)

    one = lambda: (pl.program_id(0) == 0, pl.program_id(0) == 0)
    _, landed = _ride_call(body, ride, one, name=name, out_shape=[jax.ShapeDtypeStruct((8, LANES), F32)], grid=(1,),
                           in_specs=[], out_specs=[pl.BlockSpec((8, LANES), lambda i: (0, 0))], scratch_shapes=[],
                           compiler_params=_cparams(("arbitrary",)), operands=[])
    return landed


def _chips_ride(sums):
    n = len(sums)
    lands = [lax.empty((3,) + s.shape[1:], s.dtype) for s in sums]

    def start(s, land, sems):
        x, y, c = _place()
        for t in range(n):
            for j, (px, py) in enumerate([(1 - x, y), (x, 1 - y), (1 - x, 1 - y)]):
                pltpu.make_async_remote_copy(
                    src_ref=s[t].at[2 * px + py], dst_ref=land[t].at[j], send_sem=sems[0].at[t], recv_sem=sems[1].at[t],
                    device_id=(px, py, c), device_id_type=MESH).start()

    def wait(s, land, sems):
        x, y, c = _place()
        for t in range(n):
            w = _whole(land[t], land[t], sems[0].at[t], sems[1].at[t], (x, y, c))
            w.wait_send()
            w.wait_recv()

    return _Ride(sums, lands, start, wait, [(n,), (n,)])


def _gather_ride_1(shards):
    n = len(shards)
    layer = [l for _, l in shards]
    fulls = [lax.empty((N_DEV * s.shape[1], s.shape[2]), s.dtype) for s, _ in shards]

    def rows(full, r, px, py, pc):
        return full.at[pl.ds(pl.multiple_of((4 * px + 2 * py + pc) * r, BF16_ROWS), r), :]

    def start(sh, full, sems):
        x, y, c = _place()
        for t in range(n):
            shard = sh[t].at[layer[t]]
            mine = rows(full[t], shard.shape[0], x, y, c)
            pltpu.make_async_copy(shard, mine, sems[2].at[t]).start()
            for peer in [(x, y, 1 - c), (1 - x, y, c), (x, 1 - y, c), (1 - x, 1 - y, c)]:
                pltpu.make_async_remote_copy(src_ref=shard, dst_ref=mine, send_sem=sems[0].at[t], recv_sem=sems[1].at[t],
                                             device_id=peer, device_id_type=MESH).start()

    def wait(sh, full, sems):
        x, y, c = _place()
        for t in range(n):
            shard = sh[t].at[layer[t]]
            r = shard.shape[0]
            pltpu.make_async_copy(shard, rows(full[t], r, x, y, c), sems[2].at[t]).wait()
            four = full[t].at[pl.ds(0, 4 * r), :]
            w = _whole(four, four, sems[0].at[t], sems[1].at[t], (x, y, c))
            w.wait_send()
            w.wait_recv()

    return _Ride([s for s, _ in shards], fulls, start, wait, [(n,), (n,), (n,)])


def _gather_ride_2(fulls):
    n = len(fulls)

    def start(_, full, sems):
        x, y, c = _place()
        for t in range(n):
            r = full[t].shape[0] // N_DEV
            for px, py in [(1 - x, y), (x, 1 - y), (1 - x, 1 - y)]:
                block = full[t].at[pl.ds(pl.multiple_of((4 * px + 2 * py + c) * r, BF16_ROWS), r), :]
                pltpu.make_async_remote_copy(src_ref=block, dst_ref=block, send_sem=sems[0].at[t], recv_sem=sems[1].at[t],
                                             device_id=(x, y, 1 - c), device_id_type=MESH).start()

    def wait(_, full, sems):
        x, y, c = _place()
        for t in range(n):
            three = full[t].at[pl.ds(0, 3 * (full[t].shape[0] // N_DEV)), :]
            w = _whole(three, three, sems[0].at[t], sems[1].at[t], (x, y, c))
            w.wait_send()
            w.wait_recv()

    return _Ride([], fulls, start, wait, [(n,), (n,)])


def _join_rides(rides):
    rides = [r for r in rides if r is not None]
    if len(rides) <= 1:
        return rides[0] if rides else None

    def parts(src, dst, sems):
        so = do = mo = 0
        for r in rides:
            yield r, src[so:so + len(r.srcs)], dst[do:do + len(r.dsts)], sems[mo:mo + len(r.sems)]
            so, do, mo = so + len(r.srcs), do + len(r.dsts), mo + len(r.sems)

    def start(src, dst, sems):
        for r, s, d, m in parts(src, dst, sems):
            r.start(s, d, m)

    def wait(src, dst, sems):
        for r, s, d, m in parts(src, dst, sems):
            r.wait(s, d, m)

    return _Ride([a for r in rides for a in r.srcs], [a for r in rides for a in r.dsts], start, wait,
                 [m for r in rides for m in r.sems])


def _chip_sum(g, land, core, *, name):
    L, _, _, r, C = g.shape

    def body(core_ref, g_ref, l_ref, o_ref):
        o_ref[...] = (g_ref[...].astype(F32) + l_ref[...].astype(F32)).astype(BF16)

    grid_spec = pltpu.PrefetchScalarGridSpec(
        num_scalar_prefetch=1, grid=(4, L),
        in_specs=[pl.BlockSpec((None, None, None, r, C), lambda k, l, core_ref: (l, k, core_ref[0], 0, 0)),
                  pl.BlockSpec((None, None, r, C), lambda k, l, core_ref: (k, l, 0, 0))],
        out_specs=pl.BlockSpec((None, None, r, C), lambda k, l, core_ref: (k, l, 0, 0)))
    return pl.pallas_call(body, name=name, out_shape=jax.ShapeDtypeStruct((4, L, r, C), BF16), grid_spec=grid_spec,
                          compiler_params=_cparams(("parallel", "parallel")))(core, g, land)


def _final_sum(sums, land, chip, into, layer, *, name):
    _, _, r, C = sums.shape

    def body(chip_ref, s_ref, a_ref, b_ref, c_ref, _, o_ref):
        o_ref[...] = ((s_ref[...].astype(F32) + a_ref[...].astype(F32)) + b_ref[...].astype(F32)) + c_ref[...].astype(F32)

    slot = lambda j: pl.BlockSpec((None, None, r, C), lambda i, chip_ref: (j, 0, 0, 0))
    grid_spec = pltpu.PrefetchScalarGridSpec(
        num_scalar_prefetch=1, grid=(1,),
        in_specs=[pl.BlockSpec((None, None, r, C), lambda i, chip_ref: (chip_ref[0], 0, 0, 0)), slot(0), slot(1), slot(2), ANY],
        out_specs=pl.BlockSpec((None, r, C), lambda i, chip_ref: (layer, 0, 0)))
    return pl.pallas_call(body, name=name, out_shape=jax.ShapeDtypeStruct(into.shape, F32), grid_spec=grid_spec,
                          input_output_aliases={5: 0}, compiler_params=_cparams(("arbitrary",)))(chip, sums, land, land, land, into)


def _exchange(v, reduce, *, name):
    R, C = v.shape

    def body(v_ref, o_ref, *scratch):
        if reduce:
            buf, send_sems, recv_sems = scratch
        else:
            buf = o_ref
            send_sems, recv_sems = scratch
        x, y, c = _place()
        me = 4 * x + 2 * y + c
        buf[me] = v_ref[...]
        copies = []
        for k in range(1, N_DEV):
            kx, ky, kc = (k >> 2) & 1, (k >> 1) & 1, k & 1
            peer = (1 - x if kx else x, 1 - y if ky else y, 1 - c if kc else c)
            cp = pltpu.make_async_remote_copy(src_ref=v_ref, dst_ref=buf.at[me], send_sem=send_sems.at[k - 1],
                                              recv_sem=recv_sems.at[k - 1], device_id=peer, device_id_type=MESH)
            cp.start()
            copies.append(cp)
        for cp in copies:
            cp.wait_recv()
        for cp in copies:
            cp.wait_send()
        if reduce:
            acc = buf[0]
            for d in range(1, N_DEV):
                acc = acc + buf[d]
            o_ref[...] = acc

    sems = [pltpu.SemaphoreType.DMA((N_DEV - 1,)), pltpu.SemaphoreType.DMA((N_DEV - 1,))]
    vm = pl.BlockSpec(memory_space=pltpu.VMEM)
    if reduce:
        return pl.pallas_call(body, name=name, out_shape=jax.ShapeDtypeStruct((R, C), F32), in_specs=[vm], out_specs=vm,
                              scratch_shapes=[pltpu.VMEM((N_DEV, R, C), F32)] + sems)(v)
    return pl.pallas_call(body, name=name, out_shape=jax.ShapeDtypeStruct((N_DEV, R, C), F32), in_specs=[vm], out_specs=vm,
                          scratch_shapes=sems)(v)


def _local_step(x, mem, target, norms, conv_w, depth, n_a, get_w, next_ride, ride_done, grad_ready, grad_ride, grad_landed):
    S, D = x.shape
    main = D - MEM_WIDTH
    heads = main // HEAD_DIM
    row = lambda v: v.reshape(1, D)

    mem_n = _rmsnorm(mem, row(norms["mem_norm"]), name="mem_norm")
    saved = []
    kv = hk = x_kv = w_kv = None
    def carry_mm(*args, **kwargs):
        ride = next_ride()
        if ride is None:
            return _mm(*args, **kwargs)
        out, landed = _mm(*args, ride=ride, **kwargs)
        ride_done(landed)
        return out

    for i in range(depth):
        W = functools.partial(get_w, i)
        st = {"x": x}
        h = _rmsnorm(x, row(norms["mix_norm"][i]), name=f"mix_norm{i}")
        mkv = _mm(mem_n, W("mkv"), "nn", BF16, name=f"mkv{i}")
        if i < n_a:
            p = carry_mm(h, W("a"), "nt", BF16, name=f"a_in{i}")
            y_main = _conv_fwd(p, conv_w[i], main, name=f"conv{i}")
            qblk = 3 * main // MEM_WIDTH
        else:
            p = carry_mm(h, W("b"), "nn", BF16, name=f"b_in{i}")
            y_main, st["o32"] = _sb_fwd(p, kv, heads, name=f"sb{i}")
            qblk = main // MEM_WIDTH
        y = _memattn_fwd(p, qblk, mkv, y_main, name=f"memattn{i}")
        xm = carry_mm(y, W("o"), "nn", F32, residual=x, name=f"w_o{i}")
        h2 = _rmsnorm(xm, row(norms["ffn_norm"][i]), name=f"ffn_norm{i}")
        (act, silu, uds), landed = _ffn_up(h2, W("g"), W("u"), name=f"ffn_up{i}", ride=next_ride())
        ride_done(landed)
        x = carry_mm(act, W("d"), "nn", F32, residual=xm, name=f"w_down{i}")
        st.update(h=h, mkv=mkv, p=p, qblk=qblk, y=y, xm=xm, h2=h2, silu=silu, uds=uds, act=act)
        saved.append(st)
        if i == n_a - 1:
            x_kv, w_kv = x, W("kv")
            hk = _rmsnorm(x, row(norms["kv_norm"]), name="kv_norm")
            kv = _mm(hk, w_kv, "nt", BF16, name="w_kv")

    dx, dxb, dg_final, loss = _loss_head(x, row(norms["final_norm"]), target, name="loss_head")

    dg_mix, dg_ffn, dconv = [None] * depth, [None] * depth, [None] * n_a
    dmem_n = dk = dv = dg_kv = g_kv = None
    def carry_back(*args, **kwargs):
        ride = grad_ride()
        if ride is None:
            return _mm(*args, **kwargs)
        out, landed = _mm(*args, ride=ride, **kwargs)
        grad_landed(landed)
        return out

    def carry_back_norm(*args, **kwargs):
        outs, landed = _mm_norm_bwd(*args, ride=grad_ride(), **kwargs)
        grad_landed(landed)
        return outs

    for i in reversed(range(depth)):
        st = saved[i]
        W, key = functools.partial(get_w, i), dict(_layer_keys(i, n_a))
        dgate, dup = _ffn_down_bwd(dxb, W("d"), st["silu"], st["uds"], name=f"ffn_down_bwd{i}")
        grad_ready(("d", key["d"]), carry_back(st["act"], dxb, "tn", BF16, name=f"g_w_down{i}"))
        grad_ready(("g", key["g"]), carry_back(dgate, st["h2"], "tn", BF16, name=f"g_w_gate{i}"))
        grad_ready(("u", key["u"]), carry_back(dup, st["h2"], "tn", BF16, name=f"g_w_up{i}"))
        dx, dxb, dg_ffn[i] = carry_back_norm([(dgate, W("g")), (dup, W("u"))], "nn", st["xm"], row(norms["ffn_norm"][i]), dx,
                                             name=f"d_h2_{i}")
        dy = _mm(dxb, W("o"), "nt", BF16, name=f"d_y{i}")
        grad_ready(("o", key["o"]), _mm(st["y"], dxb, "tn", BF16, name=f"g_w_o{i}"))
        dqmem, dmkv = _memattn_bwd(st["p"], st["qblk"], st["mkv"], dy, main // MEM_WIDTH, name=f"memattn_bwd{i}")
        grad_ready(("mkv", key["mkv"]), _mm(mem_n, dmkv, "tn", BF16, name=f"g_w_mem_kv{i}"))
        dmem_n = _mm(dmkv, W("mkv"), "nt", F32, residual=dmem_n, name=f"d_mem_n{i}")
        if i < n_a:
            db, dc, du, dconv[i] = _conv_bwd(st["p"], conv_w[i], dy, main, name=f"conv_bwd{i}")
            dp = jnp.concatenate([db, dc, du, dqmem], axis=1)
            grad_ready(("a", key["a"]), carry_back(dp, st["h"], "tn", BF16, name=f"g_a_in{i}"))
            w_in, form = W("a"), "nn"
        else:
            dq, dk, dv = _sb_bwd(st["p"], kv, st["o32"], dy, heads, dk, dv, name=f"sb_bwd{i}")
            dp = jnp.concatenate([dq, dqmem], axis=1)
            grad_ready(("b", key["b"]), carry_back(st["h"], dp, "tn", BF16, name=f"g_b_in{i}"))
            w_in, form = W("b"), "nt"
        dx, dxb, dg_mix[i] = carry_back_norm([(dp, w_in)], form, st["x"], row(norms["mix_norm"][i]), dx, name=f"d_h{i}")
        if i == n_a:
            dkv = jnp.concatenate([dk, dv], axis=1).astype(BF16)
            grad_ready(("kv", 0), carry_back(dkv, hk, "tn", BF16, name="g_w_kv"))
            dx, dxb, dg_kv = carry_back_norm([(dkv, w_kv)], "nn", x_kv, row(norms["kv_norm"]), dx, name="d_hk")
    _, _, dg_mem = _rmsnorm_bwd(mem, row(norms["mem_norm"]), dmem_n, None, name="mem_norm_bwd")

    small = {"mix_norm": jnp.concatenate(dg_mix, axis=0), "ffn_norm": jnp.concatenate(dg_ffn, axis=0), "kv_norm": dg_kv[0],
             "mem_norm": dg_mem[0], "final_norm": dg_final[0], "conv_w": jnp.stack(dconv, axis=0)}
    return loss, dx, small


_COL_SHARDED = ("a", "kv", "g", "u")
_NAMES = {"a": "a_in", "kv": "w_kv_shared", "g": "w_gate", "u": "w_up", "b": "b_in", "o": "w_o", "d": "w_down", "mkv": "w_mem_kv"}
_ORDER = ("a", "kv", "g", "u", "d", "b", "o", "mkv")
_WEIGHTS = ("mix_norm", "a_in", "conv_w", "b_in", "kv_norm", "w_kv_shared", "w_mem_kv", "w_o", "ffn_norm", "w_gate", "w_up",
            "w_down", "mem_norm", "final_norm")


def _layer_keys(i, n_a):
    keys = [("a", i) if i < n_a else ("b", i - n_a), ("g", i), ("u", i), ("d", i), ("o", i), ("mkv", i)]
    return keys + [("kv", 0)] if i == n_a - 1 else keys


def _gather_groups(i, n_a):
    first, rest = _layer_keys(i, n_a)[0], dict(_layer_keys(i, n_a)[1:])
    small = [(k, rest[k]) for k in ("o", "mkv", "kv") if k in rest]
    return [[first], small, [("g", rest["g"]), ("u", rest["u"])], [("d", rest["d"])]]


def _canonical(key, w):
    w3 = w if w.ndim == 3 else w[None]
    if key in _COL_SHARDED:
        w3 = jnp.transpose(w3, (0, 2, 1))
    return w3


def _uncanonical(key, g3, like):
    if key in _COL_SHARDED:
        g3 = jnp.transpose(g3, (0, 2, 1))
    return g3.reshape(like.shape)


def _pad_rows(flat, C):
    n = flat.shape[0]
    rows = -(-n // C)
    return jnp.pad(flat, (0, rows * C - n)).reshape(rows, C)


def kernel(x, mem, mix_norm, a_in, conv_w, b_in, kv_norm, w_kv_shared, w_mem_kv, w_o, ffn_norm, w_gate, w_up, w_down, mem_norm, final_norm, loss_target, m_mix_norm, m_a_in, m_conv_w, m_b_in, m_kv_norm, m_w_kv_shared, m_w_mem_kv, m_w_o, m_ffn_norm, m_w_gate, m_w_up, m_w_down, m_mem_norm, m_final_norm, v_mix_norm, v_a_in, v_conv_w, v_b_in, v_kv_norm, v_w_kv_shared, v_w_mem_kv, v_w_o, v_ffn_norm, v_w_gate, v_w_up, v_w_down, v_mem_norm, v_final_norm):
    weights = dict(mix_norm=mix_norm, a_in=a_in, conv_w=conv_w, b_in=b_in, kv_norm=kv_norm, w_kv_shared=w_kv_shared,
                   w_mem_kv=w_mem_kv, w_o=w_o, ffn_norm=ffn_norm, w_gate=w_gate, w_up=w_up, w_down=w_down,
                   mem_norm=mem_norm, final_norm=final_norm)
    moments_m = dict(mix_norm=m_mix_norm, a_in=m_a_in, conv_w=m_conv_w, b_in=m_b_in, kv_norm=m_kv_norm,
                     w_kv_shared=m_w_kv_shared, w_mem_kv=m_w_mem_kv, w_o=m_w_o, ffn_norm=m_ffn_norm, w_gate=m_w_gate,
                     w_up=m_w_up, w_down=m_w_down, mem_norm=m_mem_norm, final_norm=m_final_norm)
    moments_v = dict(mix_norm=v_mix_norm, a_in=v_a_in, conv_w=v_conv_w, b_in=v_b_in, kv_norm=v_kv_norm,
                     w_kv_shared=v_w_kv_shared, w_mem_kv=v_w_mem_kv, w_o=v_w_o, ffn_norm=v_ffn_norm, w_gate=v_w_gate,
                     w_up=v_w_up, w_down=v_w_down, mem_norm=v_mem_norm, final_norm=v_final_norm)
    D = x.shape[-1]
    depth, n_a = w_o.shape[0], a_in.shape[0]
    xi, yi, ci = _place()
    me = 4 * xi + 2 * yi + ci
    core = ci.reshape(1).astype(jnp.int32)
    chip = (2 * xi + yi).reshape(1).astype(jnp.int32)

    cw_shape = conv_w.shape
    cw_rows = _pad_rows(conv_w.reshape(-1), D)
    cw_rows = jnp.pad(cw_rows, ((0, 8 - cw_rows.shape[0]), (0, 0)))
    cw_gathered = _exchange(cw_rows, False, name="gather_conv_w")
    n_cw = cw_shape[0] * cw_shape[1] * cw_shape[2]
    cw_all = cw_gathered.reshape(N_DEV, -1)[:, :n_cw].reshape((N_DEV,) + cw_shape)
    conv_full = jnp.transpose(cw_all, (1, 2, 0, 3)).reshape(cw_shape[0], cw_shape[1], N_DEV * cw_shape[2])

    shard3 = {k: _canonical(k, weights[_NAMES[k]]).astype(BF16) for k in _ORDER}
    keys0 = _layer_keys(0, n_a)
    fulls0 = _all_gather_weights([shard3[k][l][None] for k, l in keys0], name="all_gather_layer0")
    full = {kl: f[0] for kl, f in zip(keys0, fulls0)}

    groups = [grp for i in range(1, depth) for grp in _gather_groups(i, n_a)]
    carried, riding = [0], []

    def next_ride():
        n = carried[0]
        carried[0] += 1
        second = groups[n - 1] if 1 <= n <= len(groups) else []
        first = groups[n] if n < len(groups) else []
        if not second + first:
            return None
        riding.append(second + first)
        return _join_rides([_gather_ride_2([full[kl] for kl in second]) if second else None,
                            _gather_ride_1([(shard3[k], l) for k, l in first]) if first else None])

    def ride_done(landed):
        if landed:
            full.update(zip(riding.pop(), landed))

    def get_w(i, key):
        return full[(key, dict(_layer_keys(i, n_a))[key])]

    fresh, summed, reduced, travelling = [], [], {}, []

    def grad_ready(kl, g):
        fresh.append((kl, g.reshape(1, 4, 2, g.shape[0] // N_DEV, g.shape[1])))

    def grad_ride():
        if not fresh + summed:
            return None
        travelling.append((list(summed), list(fresh)))
        ride = _join_rides([_chips_ride([s for _, s in summed]) if summed else None,
                            _sibling_ride([g for _, g in fresh]) if fresh else None])
        summed.clear()
        fresh.clear()
        return ride

    def grad_landed(landed):
        if not landed:
            return
        between_chips, to_sibling = travelling.pop()
        for (kl, s), land in zip(between_chips, landed):
            reduced[kl] = (s, land)
        for (kl, g), land in zip(to_sibling, landed[len(between_chips):]):
            summed.append((kl, _chip_sum(g, land, core, name=f"chip_sum_{kl[0]}{kl[1]}")))

    norms = {k: weights[k] for k in ("mix_norm", "ffn_norm", "kv_norm", "mem_norm", "final_norm")}
    loss, grad_x, small = _local_step(x[0], mem[0], loss_target[0], norms, conv_full, depth, n_a, get_w, next_ride, ride_done,
                                      grad_ready, grad_ride, grad_landed)
    for tail in range(2):
        ride = grad_ride()
        if ride is not None:
            grad_landed(_carry_alone(ride, name=f"reduce_scatter_tail{tail}"))

    stacks = {k: lax.empty(shard3[k].shape, F32) for k in _ORDER}
    for (k, l), (s, land) in reduced.items():
        stacks[k] = _final_sum(s, land, chip, stacks[k], l, name=f"final_sum_{k}{l}")
    grads = {_NAMES[k]: _uncanonical(k, stacks[k], weights[_NAMES[k]]) for k in _ORDER}

    order = ("mix_norm", "ffn_norm", "kv_norm", "mem_norm", "final_norm", "conv_w")
    flat = jnp.concatenate([small[k].reshape(-1) for k in order] + [loss[0, :1]])
    n_flat = flat.shape[0]
    rows = _pad_rows(flat, D)
    rows = jnp.pad(rows, ((0, (-rows.shape[0]) % 8), (0, 0)))
    total = _exchange(rows, True, name="all_reduce_small").reshape(-1)[:n_flat]
    off = 0
    for k in order:
        n = small[k].size
        grads[k] = total[off:off + n].reshape(small[k].shape)
        off += n
    loss_total = total[off]
    grads["conv_w"] = lax.dynamic_slice_in_dim(grads["conv_w"], me * cw_shape[2], cw_shape[2], axis=2)

    deltas, new_m, new_v = {}, {}, {}
    for k in _WEIGHTS:
        w = weights[k]
        three = lambda a: a.reshape((1,) * (3 - a.ndim) + a.shape)
        d, nm, nv = _adamw(three(w), three(grads[k]), three(moments_m[k]), three(moments_v[k]), name=f"adamw_{k}")
        deltas[k], new_m[k], new_v[k] = d.reshape(w.shape), nm.reshape(w.shape), nv.reshape(w.shape)

    return (loss_total, grad_x[None], *[grads[k] for k in _WEIGHTS], *[deltas[k] for k in _WEIGHTS],
            *[new_m[k] for k in _WEIGHTS], *[new_v[k] for k in _WEIGHTS])
```

```python
import functools
import math

import jax
import jax.numpy as jnp
from jax import lax
from jax.experimental import pallas as pl
from jax.experimental.pallas import tpu as pltpu

F32 = jnp.float32
BF16 = jnp.bfloat16
MESH = pl.DeviceIdType.MESH

HEAD_DIM = 64
MEM_HEADS = 4
MEM_WIDTH = MEM_HEADS * HEAD_DIM
EPS = 1e-6
LANES = 128
BF16_ROWS = 16
VMEM_LIMIT = 56 * 1024 * 1024
N_DEV = 8

ADAM_LR = 0.001
ADAM_B1 = 0.9
ADAM_B2 = 0.999
ADAM_EPS = 1e-08
ADAM_WD = 0.01
ADAM_STEP = 10

ANY = pl.BlockSpec(memory_space=pl.ANY)


def _cparams(sem=None):
    return pltpu.CompilerParams(dimension_semantics=sem, vmem_limit_bytes=VMEM_LIMIT)


def _pick(n, cands):
    for c in cands:
        if n % c == 0:
            return c
    raise ValueError(f"no tile for {n} in {cands}")


def _mm(a, b, form, out_dtype, *, name, residual=None, ride=None):
    if form == "tn":
        K, M = a.shape
    else:
        M, K = a.shape
    if form == "nt":
        N, K2 = b.shape
    else:
        K2, N = b.shape
    assert K == K2, (name, a.shape, b.shape)
    wide = (1408, 1280, 1024, 768, 512, 256, 128)
    tm = _pick(M, wide if form == "tn" else (1024, 512, 256, 128))
    tn = _pick(N, wide)
    tk = _pick(K, (1024, 1408, 1280, 768, 512, 256))
    nk = K // tk
    dims = {"nn": (((1,), (0,)), ((), ())), "nt": (((1,), (1,)), ((), ())), "tn": (((0,), (0,)), ((), ()))}[form]
    a_bytes, b_bytes = M * K * a.dtype.itemsize, N * K * b.dtype.itemsize
    n_outer = nk == 1 and (N // tn) * a_bytes + b_bytes < a_bytes + (M // tm) * b_bytes
    ij = (lambda g0, g1: (g1, g0)) if n_outer else (lambda g0, g1: (g0, g1))

    def spec(block, f):
        return pl.BlockSpec(block, lambda g0, g1, k: f(*ij(g0, g1), k))

    a_spec = spec((tk, tm), lambda i, j, k: (k, i)) if form == "tn" else spec((tm, tk), lambda i, j, k: (i, k))
    b_spec = spec((tn, tk), lambda i, j, k: (j, k)) if form == "nt" else spec((tk, tn), lambda i, j, k: (k, j))
    out_spec = spec((tm, tn), lambda i, j, k: (i, j))
    operands, in_specs = [a, b], [a_spec, b_spec]
    has_res = residual is not None
    if has_res:
        operands.append(residual)
        in_specs.append(out_spec)
    grid = (N // tn, M // tm, nk) if n_outer else (M // tm, N // tn, nk)

    def body(*refs):
        a_ref, b_ref = refs[0], refs[1]
        r_ref = refs[2] if has_res else None
        o_ref = refs[2 + int(has_res)]
        acc_ref = refs[-1]
        part = lax.dot_general(a_ref[...].astype(BF16), b_ref[...].astype(BF16), dims, preferred_element_type=F32)

        def finish(total):
            if has_res:
                total = total + r_ref[...].astype(F32)
            o_ref[...] = total.astype(out_dtype)

        if nk == 1:
            finish(part)
        else:
            k = pl.program_id(2)

            @pl.when(k == 0)
            def _():
                acc_ref[...] = part

            @pl.when(jnp.logical_and(k > 0, k < nk - 1))
            def _():
                acc_ref[...] += part

            @pl.when(k == nk - 1)
            def _():
                finish(acc_ref[...] + part)

    def edges():
        ids = [pl.program_id(d) for d in range(3)]
        first = jnp.logical_and(jnp.logical_and(ids[0] == 0, ids[1] == 0), ids[2] == 0)
        last = jnp.logical_and(jnp.logical_and(ids[0] == grid[0] - 1, ids[1] == grid[1] - 1), ids[2] == grid[2] - 1)
        return first, last

    (out,), landed = _ride_call(
        body, ride, edges, name=name, out_shape=[jax.ShapeDtypeStruct((M, N), out_dtype)], grid=grid, in_specs=in_specs,
        out_specs=[out_spec], scratch_shapes=[pltpu.VMEM((tm, tn), F32)],
        compiler_params=_cparams(("parallel", "parallel", "arbitrary")), operands=operands)
    return out if ride is None else (out, landed)


def _rmsnorm(x, g, *, name):
    R, D = x.shape
    tr = _pick(R, (512, 256))

    def body(x_ref, g_ref, o_ref):
        xv = x_ref[...]
        r = lax.rsqrt(jnp.mean(xv * xv, axis=-1, keepdims=True) + EPS)
        o_ref[...] = (xv * r * g_ref[...]).astype(BF16)

    return pl.pallas_call(
        body, name=name, out_shape=jax.ShapeDtypeStruct((R, D), BF16), grid=(R // tr,),
        in_specs=[pl.BlockSpec((tr, D), lambda i: (i, 0)), pl.BlockSpec((1, D), lambda i: (0, 0))],
        out_specs=pl.BlockSpec((tr, D), lambda i: (i, 0)), compiler_params=_cparams(("parallel",)),
    )(x, g)


def _rmsnorm_bwd(x, g, dh, dx_in, *, name):
    R, D = x.shape
    tr = _pick(R, (512, 256))
    has_in = dx_in is not None

    def body(*refs):
        x_ref, g_ref, dh_ref = refs[:3]
        dxi_ref = refs[3] if has_in else None
        dx_ref, dxb_ref, dg_ref = refs[3 + int(has_in):]
        xv = x_ref[...]
        r = lax.rsqrt(jnp.mean(xv * xv, axis=-1, keepdims=True) + EPS)
        xhat = xv * r
        dhv = dh_ref[...].astype(F32)
        dxh = dhv * g_ref[...]
        dx = r * (dxh - xhat * jnp.mean(dxh * xhat, axis=-1, keepdims=True))
        if has_in:
            dx = dx + dxi_ref[...]
        dx_ref[...] = dx
        dxb_ref[...] = dx.astype(BF16)
        part = jnp.sum(dhv * xhat, axis=0, keepdims=True)

        @pl.when(pl.program_id(0) == 0)
        def _():
            dg_ref[...] = part

        @pl.when(pl.program_id(0) > 0)
        def _():
            dg_ref[...] += part

    row = pl.BlockSpec((tr, D), lambda i: (i, 0))
    vec = pl.BlockSpec((1, D), lambda i: (0, 0))
    ops = [x, g, dh] + ([dx_in] if has_in else [])
    return pl.pallas_call(
        body, name=name,
        out_shape=(jax.ShapeDtypeStruct((R, D), F32), jax.ShapeDtypeStruct((R, D), BF16), jax.ShapeDtypeStruct((1, D), F32)),
        grid=(R // tr,), in_specs=[row, vec, row] + ([row] if has_in else []), out_specs=(row, row, vec),
        compiler_params=_cparams(("arbitrary",)),
    )(*ops)


class _Ride:
    def __init__(self, srcs, dsts, start, wait, sems):
        self.srcs, self.dsts, self.start, self.wait, self.sems = list(srcs), list(dsts), start, wait, list(sems)


def _ride_call(body, ride, edges, *, name, out_shape, grid, in_specs, out_specs, scratch_shapes, compiler_params, operands):
    n_in, n_out, n_scr = len(in_specs), len(out_specs), len(scratch_shapes)
    if ride is None:
        outs = pl.pallas_call(body, name=name, out_shape=tuple(out_shape), grid=grid, in_specs=list(in_specs),
                              out_specs=tuple(out_specs), scratch_shapes=list(scratch_shapes),
                              compiler_params=compiler_params)(*operands)
        return tuple(outs), []
    ns, nd = len(ride.srcs), len(ride.dsts)

    def riding(*refs):
        ins, rin = refs[:n_in], refs[n_in:n_in + ns + nd]
        outs = refs[n_in + ns + nd:n_in + ns + nd + n_out]
        scratch = refs[n_in + ns + 2 * nd + n_out:]
        sems = scratch[n_scr:]
        first, last = edges()

        @pl.when(first)
        def _():
            ride.start(rin[:ns], rin[ns:], sems)

        body(*ins, *outs, *scratch[:n_scr])

        @pl.when(last)
        def _():
            ride.wait(rin[:ns], rin[ns:], sems)

    outs = pl.pallas_call(
        riding, name=name, out_shape=(*out_shape, *[jax.ShapeDtypeStruct(d.shape, d.dtype) for d in ride.dsts]), grid=grid,
        in_specs=[*in_specs, *[ANY] * (ns + nd)], out_specs=(*out_specs, *[ANY] * nd),
        scratch_shapes=[*scratch_shapes, *[pltpu.SemaphoreType.DMA(s) for s in ride.sems]],
        input_output_aliases={n_in + ns + d: n_out + d for d in range(nd)}, compiler_params=compiler_params,
    )(*operands, *ride.srcs, *ride.dsts)
    return tuple(outs[:n_out]), list(outs[n_out:])


def _mm_norm_bwd(parts, form, x, g, dx_in, *, name, ride=None):
    S, K = parts[0][0].shape
    D = x.shape[1]
    tm = _pick(S, (512, 256))
    tk = _pick(K, (1024, 1408, 1280, 768, 512, 256))
    nk, P = K // tk, len(parts)
    dims = _NN if form == "nn" else _NT

    ni, nsteps = S // tm, P * nk

    def body(*refs):
        ab = refs[:2 * P]
        x_ref, g_ref, dxi_ref, dx_ref, dxb_ref, dg_ref, acc_ref = refs[2 * P:]
        k, i = pl.program_id(0), pl.program_id(1)
        rows = pl.ds(pl.multiple_of(i * tm, tm), tm)

        @pl.when(k == 0)
        def _():
            acc_ref[rows, :] = jnp.zeros((tm, D), F32)

        for p in range(P):
            @pl.when(jnp.logical_and(k >= p * nk, k < (p + 1) * nk))
            def _():
                acc_ref[rows, :] += _dot(ab[2 * p][...], ab[2 * p + 1][...], dims)

        @pl.when(k == nsteps - 1)
        def _():
            xv = x_ref[...]
            r = lax.rsqrt(jnp.mean(xv * xv, axis=-1, keepdims=True) + EPS)
            xhat = xv * r
            dhv = acc_ref[rows, :]
            dxh = dhv * g_ref[...]
            dx = r * (dxh - xhat * jnp.mean(dxh * xhat, axis=-1, keepdims=True)) + dxi_ref[...]
            dx_ref[...] = dx
            dxb_ref[...] = dx.astype(BF16)
            part = jnp.sum(dhv * xhat, axis=0, keepdims=True)

            @pl.when(i == 0)
            def _():
                dg_ref[...] = part

            @pl.when(i > 0)
            def _():
                dg_ref[...] += part

    def kk(p):
        return lambda k: jnp.clip(k - p * nk, 0, nk - 1)

    def active_row(p):
        return lambda k, i: jnp.where(jnp.logical_and(k >= p * nk, k < (p + 1) * nk), i, 0)

    in_specs, operands = [], []
    for p, (a, b) in enumerate(parts):
        in_specs.append(pl.BlockSpec((tm, tk), lambda k, i, f=kk(p), r=active_row(p): (r(k, i), f(k))))
        in_specs.append(pl.BlockSpec((tk, D), lambda k, i, f=kk(p): (f(k), 0)) if form == "nn"
                        else pl.BlockSpec((D, tk), lambda k, i, f=kk(p): (0, f(k))))
        operands += [a, b]
    row = pl.BlockSpec((tm, D), lambda k, i: (jnp.where(k == nsteps - 1, i, 0), 0))
    vec = pl.BlockSpec((1, D), lambda k, i: (0, 0))

    def edges():
        k, i = pl.program_id(0), pl.program_id(1)
        return jnp.logical_and(i == 0, k == 0), jnp.logical_and(i == ni - 1, k == nsteps - 1)

    return _ride_call(
        body, ride, edges, name=name,
        out_shape=(jax.ShapeDtypeStruct((S, D), F32), jax.ShapeDtypeStruct((S, D), BF16), jax.ShapeDtypeStruct((1, D), F32)),
        grid=(nsteps, ni), in_specs=in_specs + [row, vec, row], out_specs=(row, row, vec),
        scratch_shapes=[pltpu.VMEM((S, D), F32)], compiler_params=_cparams(("arbitrary", "arbitrary")),
        operands=[*operands, x, g, dx_in])


def _loss_head(x, g, target, *, name):
    R, D = x.shape
    tr = _pick(R, (512, 256))

    def body(x_ref, g_ref, t_ref, dx_ref, dxb_ref, dg_ref, loss_ref):
        xv = x_ref[...]
        gv = g_ref[...]
        r = lax.rsqrt(jnp.mean(xv * xv, axis=-1, keepdims=True) + EPS)
        xhat = xv * r
        err = xhat * gv - t_ref[...]
        loss = 0.5 * jnp.sum(jnp.mean(err * err, axis=-1, keepdims=True), axis=0, keepdims=True)
        dy = err * (1.0 / D)
        dxh = dy * gv
        dx = r * (dxh - xhat * jnp.mean(dxh * xhat, axis=-1, keepdims=True))
        dx_ref[...] = dx
        dxb_ref[...] = dx.astype(BF16)
        dg = jnp.sum(dy * xhat, axis=0, keepdims=True)
        lossv = jnp.broadcast_to(loss, (1, LANES))

        @pl.when(pl.program_id(0) == 0)
        def _():
            dg_ref[...] = dg
            loss_ref[...] = lossv

        @pl.when(pl.program_id(0) > 0)
        def _():
            dg_ref[...] += dg
            loss_ref[...] += lossv

    row = pl.BlockSpec((tr, D), lambda i: (i, 0))
    vec = pl.BlockSpec((1, D), lambda i: (0, 0))
    return pl.pallas_call(
        body, name=name,
        out_shape=(jax.ShapeDtypeStruct((R, D), F32), jax.ShapeDtypeStruct((R, D), BF16), jax.ShapeDtypeStruct((1, D), F32),
                   jax.ShapeDtypeStruct((1, LANES), F32)),
        grid=(R // tr,), in_specs=[row, vec, row], out_specs=(row, row, vec, pl.BlockSpec((1, LANES), lambda i: (0, 0))),
        compiler_params=_cparams(("arbitrary",)),
    )(x, g, target)


def _conv_taps(gv, S):
    t = lax.broadcasted_iota(jnp.int32, gv.shape, 0)
    g1 = jnp.where(t >= 1, pltpu.roll(gv, 1, 0), 0.0)
    g2 = jnp.where(t >= 2, pltpu.roll(gv, 2, 0), 0.0)
    return g1, g2


def _conv_fwd(p, w, main, *, name):
    S = p.shape[0]
    tc = LANES
    nb = main // tc

    def body(b_ref, c_ref, u_ref, w_ref, y_ref):
        gv = c_ref[...].astype(F32) * u_ref[...].astype(F32)
        g1, g2 = _conv_taps(gv, S)
        cv = w_ref[0:1, :] * g2 + w_ref[1:2, :] * g1 + w_ref[2:3, :] * gv
        y_ref[...] = (b_ref[...].astype(F32) * cv).astype(BF16)

    col = lambda off: pl.BlockSpec((S, tc), lambda j: (0, off + j))
    return pl.pallas_call(
        body, name=name, out_shape=jax.ShapeDtypeStruct((S, p.shape[1] - 2 * main), BF16), grid=(nb,),
        in_specs=[col(0), col(nb), col(2 * nb), pl.BlockSpec((3, tc), lambda j: (0, j))],
        out_specs=pl.BlockSpec((S, tc), lambda j: (0, j)), compiler_params=_cparams(("parallel",)),
    )(p, p, p, w)


def _conv_bwd(p, w, dy, main, *, name):
    S = p.shape[0]
    tc = LANES
    nb = main // tc

    def body(b_ref, c_ref, u_ref, w_ref, dy_ref, db_ref, dc_ref, du_ref, dw_ref):
        cvv, uv = c_ref[...].astype(F32), u_ref[...].astype(F32)
        gv = cvv * uv
        g1, g2 = _conv_taps(gv, S)
        w0, w1, w2 = w_ref[0:1, :], w_ref[1:2, :], w_ref[2:3, :]
        dyv = dy_ref[...].astype(F32)
        db_ref[...] = (dyv * (w0 * g2 + w1 * g1 + w2 * gv)).astype(BF16)
        dcv = dyv * b_ref[...].astype(F32)
        t = lax.broadcasted_iota(jnp.int32, dcv.shape, 0)
        n1 = jnp.where(t <= S - 2, pltpu.roll(dcv, S - 1, 0), 0.0)
        n2 = jnp.where(t <= S - 3, pltpu.roll(dcv, S - 2, 0), 0.0)
        dg = w2 * dcv + w1 * n1 + w0 * n2
        dc_ref[...] = (dg * uv).astype(BF16)
        du_ref[...] = (dg * cvv).astype(BF16)
        dw_ref[0:1, :] = jnp.sum(dcv * g2, axis=0, keepdims=True)
        dw_ref[1:2, :] = jnp.sum(dcv * g1, axis=0, keepdims=True)
        dw_ref[2:3, :] = jnp.sum(dcv * gv, axis=0, keepdims=True)

    col = lambda off: pl.BlockSpec((S, tc), lambda j: (0, off + j))
    out = jax.ShapeDtypeStruct((S, main), BF16)
    return pl.pallas_call(
        body, name=name, out_shape=(out, out, out, jax.ShapeDtypeStruct((3, main), F32)), grid=(nb,),
        in_specs=[col(0), col(nb), col(2 * nb), pl.BlockSpec((3, tc), lambda j: (0, j)), col(0)],
        out_specs=(col(0), col(0), col(0), pl.BlockSpec((3, tc), lambda j: (0, j))),
        compiler_params=_cparams(("parallel",)),
    )(p, p, p, w, dy)


def _head_mask(width, h):
    lane = lax.broadcasted_iota(jnp.int32, (1, width), 1)
    return jnp.logical_and(lane >= h * HEAD_DIM, lane < (h + 1) * HEAD_DIM)


_NT = (((1,), (1,)), ((), ()))
_NN = (((1,), (0,)), ((), ()))
_TN = (((0,), (0,)), ((), ()))


def _dot(a, b, dims):
    return lax.dot_general(a, b, dims, preferred_element_type=F32)


def _mem_probs(qh, kv):
    s = _dot(qh, kv, _NT) * (1.0 / math.sqrt(HEAD_DIM))
    s = s - jnp.max(s, axis=-1, keepdims=True)
    e = jnp.exp(s)
    return e / jnp.sum(e, axis=-1, keepdims=True)


def _memattn_fwd(p, qblk, mkv, into, *, name):
    S = p.shape[0]
    M = mkv.shape[0]
    W = MEM_WIDTH
    tq = _pick(S, (512, 256))
    last = into.shape[1] // W - 1

    def body(q_ref, k_ref, v_ref, _, o_ref):
        q = q_ref[...].astype(BF16)
        kv, vv = k_ref[...], v_ref[...]
        out = jnp.zeros((tq, W), F32)
        for h in range(MEM_HEADS):
            m = _head_mask(W, h)
            pr = _mem_probs(jnp.where(m, q, jnp.zeros_like(q)), kv)
            out = jnp.where(m, _dot(pr.astype(BF16), vv, _NN), out)
        o_ref[...] = out.astype(BF16)

    return pl.pallas_call(
        body, name=name, out_shape=jax.ShapeDtypeStruct(into.shape, BF16), grid=(S // tq,),
        in_specs=[pl.BlockSpec((tq, W), lambda i: (i, qblk)), pl.BlockSpec((M, W), lambda i: (0, 0)),
                  pl.BlockSpec((M, W), lambda i: (0, 1)), ANY],
        out_specs=pl.BlockSpec((tq, W), lambda i: (i, last)), input_output_aliases={3: 0},
        compiler_params=_cparams(("parallel",)),
    )(p, mkv, mkv, into)


def _memattn_bwd(p, qblk, mkv, dy, dyblk, *, name):
    S = p.shape[0]
    M = mkv.shape[0]
    W = MEM_WIDTH
    tq = _pick(S, (512, 256))
    scale = 1.0 / math.sqrt(HEAD_DIM)

    def body(q_ref, k_ref, v_ref, do_ref, dq_ref, dkv_ref, dk_acc, dv_acc):
        q = q_ref[...].astype(BF16)
        do = do_ref[...].astype(BF16)
        kv, vv = k_ref[...], v_ref[...]
        dq = jnp.zeros((tq, W), F32)
        dk = jnp.zeros((M, W), F32)
        dv = jnp.zeros((M, W), F32)
        for h in range(MEM_HEADS):
            m = _head_mask(W, h)
            qh = jnp.where(m, q, jnp.zeros_like(q))
            doh = jnp.where(m, do, jnp.zeros_like(do))
            pr = _mem_probs(qh, kv)
            dpr = _dot(doh, vv, _NT)
            ds = (pr * (dpr - jnp.sum(dpr * pr, axis=-1, keepdims=True)) * scale).astype(BF16)
            dq = jnp.where(m, _dot(ds, kv, _NN), dq)
            dk = dk + _dot(ds, qh, _TN)
            dv = dv + _dot(pr.astype(BF16), doh, _TN)
        dq_ref[...] = dq.astype(BF16)
        i = pl.program_id(0)

        @pl.when(i == 0)
        def _():
            dk_acc[...] = dk
            dv_acc[...] = dv

        @pl.when(i > 0)
        def _():
            dk_acc[...] += dk
            dv_acc[...] += dv

        @pl.when(i == S // tq - 1)
        def _():
            dkv_ref[:, :W] = dk_acc[...].astype(BF16)
            dkv_ref[:, W:] = dv_acc[...].astype(BF16)

    kspec = lambda c: pl.BlockSpec((M, W), lambda i: (0, c))
    return pl.pallas_call(
        body, name=name,
        out_shape=(jax.ShapeDtypeStruct((S, W), BF16), jax.ShapeDtypeStruct((M, 2 * W), BF16)),
        grid=(S // tq,),
        in_specs=[pl.BlockSpec((tq, W), lambda i: (i, qblk)), kspec(0), kspec(1), pl.BlockSpec((tq, W), lambda i: (i, dyblk))],
        out_specs=(pl.BlockSpec((tq, W), lambda i: (i, 0)), pl.BlockSpec((M, 2 * W), lambda i: (0, 0))),
        scratch_shapes=[pltpu.VMEM((M, W), F32), pltpu.VMEM((M, W), F32)],
        compiler_params=_cparams(("arbitrary",)),
    )(p, mkv, mkv, dy)


SB_TQ = 256
SB_CLAMP = 80.0
SB_DEAD = 110.0


SB_CHUNK = 64


def _by_rows(fn, *arrays):
    rows = next(a for a in arrays if a is not None).shape[0]
    outs = [fn(*[None if a is None else a[r0:r0 + SB_CHUNK] for a in arrays]) for r0 in range(0, rows, SB_CHUNK)]
    return tuple(jnp.concatenate(col, axis=0) for col in zip(*outs))


def _sb_scores(qh, kb, causal):
    def chain(z, mask):
        z = jnp.clip(z, -SB_CLAMP, SB_CLAMP)
        w = 1.0 + jnp.exp(z)
        sp = jnp.log(w)
        zs = z - sp
        if mask is not None:
            sp = jnp.where(mask, sp, 0.0)
            zs = jnp.where(mask, zs, -1e30)
            w = jnp.where(mask, w, 1.0)
        return zs, sp.astype(BF16), jnp.sum(sp, axis=1, keepdims=True), w

    return _by_rows(chain, _dot(qh, kb, _NT), causal)


def _sb_weights(zs, spb, tri, carry):
    return _by_rows(lambda zs_c, t_c, c_c: (jnp.exp(zs_c - (t_c + c_c)).astype(BF16),), zs, _dot(spb, tri, _NN), carry)[0]


def _sb_live(carry):
    return jnp.min(carry) <= SB_DEAD


def _stack_heads(v, m0):
    zero = jnp.zeros_like(v)
    return jnp.concatenate([jnp.where(m0, v, zero), jnp.where(m0, zero, v)], axis=0)


def _stacked_causal(tq):
    r = lax.broadcasted_iota(jnp.int32, (2 * tq, tq), 0)
    c = lax.broadcasted_iota(jnp.int32, (2 * tq, tq), 1)
    return c < jnp.where(r >= tq, r - tq, r)


def _sb_fwd(p, kv, heads, *, name):
    S = p.shape[0]
    tq = SB_TQ
    npair = heads // 2

    def body(q_ref, k_ref, v_ref, o_ref, o32_ref, carry_ref, acc_ref):
        qi = pl.program_id(1)
        r = lax.broadcasted_iota(jnp.int32, (tq, tq), 0)
        c = lax.broadcasted_iota(jnp.int32, (tq, tq), 1)
        tri = (r > c).astype(BF16)
        causal = _stacked_causal(tq)
        m0 = _head_mask(LANES, 0)
        qh = _stack_heads(q_ref[...] * jnp.asarray(1.0 / math.sqrt(HEAD_DIM), BF16), m0)

        def block(j, carry, mask):
            off = pl.multiple_of(j * tq, tq)
            kb = k_ref[pl.ds(off, tq), :]
            vb = v_ref[pl.ds(off, tq), :]
            zs, spb, sp_sum, _ = _sb_scores(qh, kb, mask)
            return carry + sp_sum, _dot(_sb_weights(zs, spb, tri, carry), vb, _NN)

        def two_blocks(j, carry, mask):
            carry, first = block(j, carry, mask)
            carry, second = block(j - 1, carry, None)
            return carry, first + second

        def keep(carry, added, fresh=False):
            carry_ref[...] = carry
            acc_ref[...] = added if fresh else acc_ref[...] + added

        zero = jnp.zeros((2 * tq, 1), F32)

        @pl.when(qi >= 1)
        def _():
            keep(*two_blocks(qi, zero, causal), fresh=True)

        @pl.when(qi == 0)
        def _():
            keep(*block(qi, zero, causal), fresh=True)

        left = jnp.maximum(qi - 1, 0)
        odd = left % 2

        @pl.when(jnp.logical_and(odd == 1, _sb_live(carry_ref[...])))
        def _():
            keep(*block(qi - 2, carry_ref[...], None))

        def pair(s):
            carry, added = two_blocks(qi - 2 - odd - 2 * s[0], carry_ref[...], None)
            keep(carry, added)
            return s[0] + 1, _sb_live(carry)

        lax.while_loop(lambda s: jnp.logical_and(s[0] < left // 2, s[1]), pair, (jnp.int32(0), _sb_live(carry_ref[...])))
        acc = acc_ref[...]
        out = jnp.where(m0, acc[:tq], acc[tq:])
        o_ref[...] = out.astype(BF16)
        o32_ref[...] = out

    W = heads * HEAD_DIM
    qspec = pl.BlockSpec((tq, LANES), lambda hp, i: (i, hp))
    return pl.pallas_call(
        body, name=name, out_shape=(jax.ShapeDtypeStruct(p.shape, BF16), jax.ShapeDtypeStruct((S, W), F32)), grid=(npair, S // tq),
        in_specs=[qspec, pl.BlockSpec((S, LANES), lambda hp, i: (0, hp)), pl.BlockSpec((S, LANES), lambda hp, i: (0, npair + hp))],
        out_specs=(qspec, qspec), scratch_shapes=[pltpu.VMEM((2 * tq, 1), F32), pltpu.VMEM((2 * tq, LANES), F32)],
        compiler_params=_cparams(("parallel", "arbitrary")),
    )(p, kv, kv)


def _sb_bwd(p, kv, o32, dy, heads, dk_in, dv_in, *, name):
    S = p.shape[0]
    tq = SB_TQ
    npair = heads // 2
    has_in = dk_in is not None
    scale = 1.0 / math.sqrt(HEAD_DIM)

    def body(*refs):
        q_ref, k_ref, v_ref, o_ref, do_ref = refs[:5]
        dq_ref, dk_ref, dv_ref, carry_ref, gcarry_ref, acc_ref = refs[5 + 2 * int(has_in):]
        qi = pl.program_id(1)

        @pl.when(qi == 0)
        def _():
            if has_in:
                dk_ref[...] = refs[5][...]
                dv_ref[...] = refs[6][...]
            else:
                dk_ref[...] = jnp.zeros_like(dk_ref)
                dv_ref[...] = jnp.zeros_like(dv_ref)

        r = lax.broadcasted_iota(jnp.int32, (tq, tq), 0)
        c = lax.broadcasted_iota(jnp.int32, (tq, tq), 1)
        tri = (r > c).astype(BF16)
        tri_low = (r < c).astype(BF16)
        causal = _stacked_causal(tq)
        m0 = _head_mask(LANES, 0)
        qh = _stack_heads(q_ref[...] * jnp.asarray(scale, BF16), m0)
        do = do_ref[...]
        doh = _stack_heads(do, m0)
        dov = do.astype(F32) * o_ref[...]
        dsum = jnp.concatenate([jnp.sum(jnp.where(m0, dov, 0.0), axis=1, keepdims=True),
                                jnp.sum(jnp.where(m0, 0.0, dov), axis=1, keepdims=True)], axis=0)

        def block(j, carry, gcarry, mask):
            off = pl.multiple_of(j * tq, tq)
            kb = k_ref[pl.ds(off, tq), :]
            vb = v_ref[pl.ds(off, tq), :]
            zs, spb, sp_sum, w = _sb_scores(qh, kb, mask)
            ab = _sb_weights(zs, spb, tri, carry)

            def grads(ab_c, da_c):
                g = ab_c.astype(F32) * da_c
                return g, g.astype(BF16), jnp.sum(g, axis=1, keepdims=True)

            g, gb, g_sum = _by_rows(grads, ab, _dot(doh, vb, _NT))
            gcarry = gcarry + g_sum

            def logit_grads(g_c, w_c, low_c, left_c):
                rinv = 1.0 / w_c
                return ((g_c * rinv - (left_c + low_c) * (1.0 - rinv)).astype(BF16),)

            dzs = _by_rows(logit_grads, g, w, _dot(gb, tri_low, _NN), dsum - gcarry)[0]
            dk_ref[pl.ds(off, tq), :] += _dot(dzs, qh, _TN)
            dv_ref[pl.ds(off, tq), :] += _dot(ab, doh, _TN)
            return carry + sp_sum, gcarry, _dot(dzs, kb, _NN)

        def two_blocks(j, carry, gcarry, mask):
            carry, gcarry, first = block(j, carry, gcarry, mask)
            carry, gcarry, second = block(j - 1, carry, gcarry, None)
            return carry, gcarry, first + second

        def keep(carry, gcarry, added, fresh=False):
            carry_ref[...] = carry
            gcarry_ref[...] = gcarry
            acc_ref[...] = added if fresh else acc_ref[...] + added

        zero = jnp.zeros((2 * tq, 1), F32)

        @pl.when(qi >= 1)
        def _():
            keep(*two_blocks(qi, zero, zero, causal), fresh=True)

        @pl.when(qi == 0)
        def _():
            keep(*block(qi, zero, zero, causal), fresh=True)

        left = jnp.maximum(qi - 1, 0)
        odd = left % 2

        @pl.when(jnp.logical_and(odd == 1, _sb_live(carry_ref[...])))
        def _():
            keep(*block(qi - 2, carry_ref[...], gcarry_ref[...], None))

        def pair(s):
            carry, gcarry, added = two_blocks(qi - 2 - odd - 2 * s[0], carry_ref[...], gcarry_ref[...], None)
            keep(carry, gcarry, added)
            return s[0] + 1, _sb_live(carry)

        lax.while_loop(lambda s: jnp.logical_and(s[0] < left // 2, s[1]), pair, (jnp.int32(0), _sb_live(carry_ref[...])))
        acc = acc_ref[...]
        dq_ref[...] = (jnp.where(m0, acc[:tq], acc[tq:]) * scale).astype(BF16)

    W = heads * HEAD_DIM
    qspec = pl.BlockSpec((tq, LANES), lambda hp, i: (i, hp))
    seq = lambda off: pl.BlockSpec((S, LANES), lambda hp, i: (0, off + hp))
    ops = [p, kv, kv, o32, dy] + ([dk_in, dv_in] if has_in else [])
    return pl.pallas_call(
        body, name=name,
        out_shape=(jax.ShapeDtypeStruct((S, W), BF16), jax.ShapeDtypeStruct((S, W), F32), jax.ShapeDtypeStruct((S, W), F32)),
        grid=(npair, S // tq),
        in_specs=[qspec, seq(0), seq(npair), qspec, qspec] + ([seq(0), seq(0)] if has_in else []),
        out_specs=(qspec, seq(0), seq(0)),
        scratch_shapes=[pltpu.VMEM((2 * tq, 1), F32), pltpu.VMEM((2 * tq, 1), F32), pltpu.VMEM((2 * tq, LANES), F32)],
        compiler_params=_cparams(("parallel", "arbitrary")),
    )(*ops)


def _ffn_up(h, wg, wu, *, name, ride=None):
    S, D = h.shape
    F = wg.shape[0]
    tm = _pick(S, (512, 256))
    tn = _pick(F, (1408, 1024, 512, 256, 128))

    def body(h_ref, g_ref, u_ref, act_ref, silu_ref, uds_ref):
        hv = h_ref[...]
        g = _dot(hv, g_ref[...], _NT)
        u = _dot(hv, u_ref[...], _NT)
        s = jax.nn.sigmoid(g)
        silu = g * s
        act_ref[...] = (silu * u).astype(BF16)
        silu_ref[...] = silu.astype(BF16)
        uds_ref[...] = (u * (s + silu * (1.0 - s))).astype(BF16)

    wspec = pl.BlockSpec((tn, D), lambda j, i: (j, 0))
    ospec = pl.BlockSpec((tm, tn), lambda j, i: (i, j))
    out = jax.ShapeDtypeStruct((S, F), BF16)
    grid = (F // tn, S // tm)

    def edges():
        j, i = pl.program_id(0), pl.program_id(1)
        return jnp.logical_and(j == 0, i == 0), jnp.logical_and(j == grid[0] - 1, i == grid[1] - 1)

    return _ride_call(
        body, ride, edges, name=name, out_shape=(out, out, out), grid=grid,
        in_specs=[pl.BlockSpec((tm, D), lambda j, i: (i, 0)), wspec, wspec], out_specs=(ospec, ospec, ospec),
        scratch_shapes=[], compiler_params=_cparams(("parallel", "parallel")), operands=[h, wg, wu])


def _ffn_down_bwd(dx, wd, silu, uds, *, name):
    S, D = dx.shape
    F = wd.shape[0]
    tm = _pick(S, (512, 256))
    tn = _pick(F, (1408, 1024, 512, 256, 128))

    def body(dx_ref, w_ref, silu_ref, uds_ref, dg_ref, du_ref):
        da = _dot(dx_ref[...], w_ref[...], _NT)
        dg_ref[...] = (da * uds_ref[...].astype(F32)).astype(BF16)
        du_ref[...] = (da * silu_ref[...].astype(F32)).astype(BF16)

    ospec = pl.BlockSpec((tm, tn), lambda j, i: (i, j))
    out = jax.ShapeDtypeStruct((S, F), BF16)
    return pl.pallas_call(
        body, name=name, out_shape=(out, out), grid=(F // tn, S // tm),
        in_specs=[pl.BlockSpec((tm, D), lambda j, i: (i, 0)), pl.BlockSpec((tn, D), lambda j, i: (j, 0)), ospec, ospec],
        out_specs=(ospec, ospec), compiler_params=_cparams(("parallel", "parallel")),
    )(dx, wd, silu, uds)


def _adamw(w, g, m, v, *, name):
    R, C = w.shape
    tr = R
    for cand in (1024, 512, 256, 128, 64, 32, 16, 8):
        if R % cand == 0 and cand * C * 4 <= (1 << 20):
            tr = cand
            break
    bc1 = 1.0 - ADAM_B1 ** ADAM_STEP
    bc2 = 1.0 - ADAM_B2 ** ADAM_STEP

    def body(w_ref, g_ref, m_ref, v_ref, d_ref, nm_ref, nv_ref):
        gv = g_ref[...]
        nm = ADAM_B1 * m_ref[...] + (1.0 - ADAM_B1) * gv
        nv = ADAM_B2 * v_ref[...] + (1.0 - ADAM_B2) * (gv * gv)
        nm_ref[...] = nm
        nv_ref[...] = nv
        d_ref[...] = -ADAM_LR * ((nm / bc1) / (jnp.sqrt(nv / bc2) + ADAM_EPS) + ADAM_WD * w_ref[...])

    blk = pl.BlockSpec((tr, C), lambda i: (i, 0))
    out = jax.ShapeDtypeStruct((R, C), F32)
    return pl.pallas_call(body, name=name, out_shape=(out, out, out), grid=(R // tr,), in_specs=[blk] * 4,
                          out_specs=(blk, blk, blk), compiler_params=_cparams(("parallel",)))(w, g, m, v)


def _place():
    x, y, c = lax.axis_index("x"), lax.axis_index("y"), lax.axis_index("c")
    return x, y, c


def _all_gather_weights(shards, *, name):
    n = len(shards)

    def body(*refs):
        sh, full = refs[:n], refs[n:2 * n]
        send_sems, recv_sems, local_sems = refs[2 * n:]
        x, y, c = _place()
        me, sibling = (x, y, c), (x, y, 1 - c)
        chips = [(1 - x, y), (x, 1 - y), (1 - x, 1 - y)]

        def rows(t, px, py, pc):
            r = sh[t].shape[1]
            return full[t].at[:, pl.ds(pl.multiple_of((4 * px + 2 * py + pc) * r, BF16_ROWS), r), :]

        def copy(t, k, block, to, src=None):
            return pltpu.make_async_remote_copy(
                src_ref=rows(t, *block) if src is None else src, dst_ref=rows(t, *block),
                send_sem=send_sems.at[7 * t + k], recv_sem=recv_sems.at[7 * t + k], device_id=to, device_id_type=MESH)

        started = []
        for t in range(n):
            mine = pltpu.make_async_copy(sh[t], rows(t, *me), local_sems.at[t])
            mine.start()
            started.append(mine)
        sends = []
        for t in range(n):
            first = [copy(t, 0, me, sibling, src=sh[t])]
            first += [copy(t, 1 + j, me, (*chip, c), src=sh[t]) for j, chip in enumerate(chips)]
            for cp in first:
                cp.start()
            sends += first
        for t in range(n):
            for j, chip in enumerate(chips):
                copy(t, 1 + j, (*chip, c), me).wait_recv()
                fwd = copy(t, 4 + j, (*chip, c), sibling)
                fwd.start()
                sends.append(fwd)
        for t in range(n):
            copy(t, 0, sibling, me).wait_recv()
            for j, chip in enumerate(chips):
                copy(t, 4 + j, (*chip, 1 - c), me).wait_recv()
        for cp in sends:
            cp.wait_send()
        for cp in started:
            cp.wait()

    out_shape = [jax.ShapeDtypeStruct((s.shape[0], N_DEV * s.shape[1], s.shape[2]), s.dtype) for s in shards]
    return pl.pallas_call(
        body, name=name, out_shape=out_shape, in_specs=[ANY] * n, out_specs=[ANY] * n,
        scratch_shapes=[pltpu.SemaphoreType.DMA((7 * n,)), pltpu.SemaphoreType.DMA((7 * n,)), pltpu.SemaphoreType.DMA((n,))],
    )(*shards)


def _whole(ref_a, ref_b, send_sem, recv_sem, me):
    return pltpu.make_async_remote_copy(src_ref=ref_a, dst_ref=ref_b, send_sem=send_sem, recv_sem=recv_sem,
                                        device_id=me, device_id_type=MESH)


def _sibling_ride(grads):
    n = len(grads)
    lands = [lax.empty((4, s.shape[0], s.shape[3], s.shape[4]), s.dtype) for s in grads]

    def start(g, land, sems):
        x, y, c = _place()
        for t in range(n):
            for k in range(4):
                pltpu.make_async_remote_copy(
                    src_ref=g[t].at[:, k, 1 - c], dst_ref=land[t].at[k], send_sem=sems[0].at[t], recv_sem=sems[1].at[t],
                    device_id=(x, y, 1 - c), device_id_type=MESH).start()

    def wait(g, land, sems):
        x, y, c = _place()
        for t in range(n):
            w = _whole(land[t], land[t], sems[0].at[t], sems[1].at[t], (x, y, c))
            w.wait_send()
            w.wait_recv()

    return _Ride(grads, lands, start, wait, [(n,), (n,)])


def _carry_alone(ride, *, name):
    def body(o_ref):
        o_ref[...] = jnp.zeros_like(o_ref)

    one = lambda: (pl.program_id(0) == 0, pl.program_id(0) == 0)
    _, landed = _ride_call(body, ride, one, name=name, out_shape=[jax.ShapeDtypeStruct((8, LANES), F32)], grid=(1,),
                           in_specs=[], out_specs=[pl.BlockSpec((8, LANES), lambda i: (0, 0))], scratch_shapes=[],
                           compiler_params=_cparams(("arbitrary",)), operands=[])
    return landed


def _chips_ride(sums):
    n = len(sums)
    lands = [lax.empty((3,) + s.shape[1:], s.dtype) for s in sums]

    def start(s, land, sems):
        x, y, c = _place()
        for t in range(n):
            for j, (px, py) in enumerate([(1 - x, y), (x, 1 - y), (1 - x, 1 - y)]):
                pltpu.make_async_remote_copy(
                    src_ref=s[t].at[2 * px + py], dst_ref=land[t].at[j], send_sem=sems[0].at[t], recv_sem=sems[1].at[t],
                    device_id=(px, py, c), device_id_type=MESH).start()

    def wait(s, land, sems):
        x, y, c = _place()
        for t in range(n):
            w = _whole(land[t], land[t], sems[0].at[t], sems[1].at[t], (x, y, c))
            w.wait_send()
            w.wait_recv()

    return _Ride(sums, lands, start, wait, [(n,), (n,)])


def _gather_ride_1(shards):
    n = len(shards)
    layer = [l for _, l in shards]
    fulls = [lax.empty((N_DEV * s.shape[1], s.shape[2]), s.dtype) for s, _ in shards]

    def rows(full, r, px, py, pc):
        return full.at[pl.ds(pl.multiple_of((4 * px + 2 * py + pc) * r, BF16_ROWS), r), :]

    def start(sh, full, sems):
        x, y, c = _place()
        for t in range(n):
            shard = sh[t].at[layer[t]]
            mine = rows(full[t], shard.shape[0], x, y, c)
            pltpu.make_async_copy(shard, mine, sems[2].at[t]).start()
            for peer in [(x, y, 1 - c), (1 - x, y, c), (x, 1 - y, c), (1 - x, 1 - y, c)]:
                pltpu.make_async_remote_copy(src_ref=shard, dst_ref=mine, send_sem=sems[0].at[t], recv_sem=sems[1].at[t],
                                             device_id=peer, device_id_type=MESH).start()

    def wait(sh, full, sems):
        x, y, c = _place()
        for t in range(n):
            shard = sh[t].at[layer[t]]
            r = shard.shape[0]
            pltpu.make_async_copy(shard, rows(full[t], r, x, y, c), sems[2].at[t]).wait()
            four = full[t].at[pl.ds(0, 4 * r), :]
            w = _whole(four, four, sems[0].at[t], sems[1].at[t], (x, y, c))
            w.wait_send()
            w.wait_recv()

    return _Ride([s for s, _ in shards], fulls, start, wait, [(n,), (n,), (n,)])


def _gather_ride_2(fulls):
    n = len(fulls)

    def start(_, full, sems):
        x, y, c = _place()
        for t in range(n):
            r = full[t].shape[0] // N_DEV
            for px, py in [(1 - x, y), (x, 1 - y), (1 - x, 1 - y)]:
                block = full[t].at[pl.ds(pl.multiple_of((4 * px + 2 * py + c) * r, BF16_ROWS), r), :]
                pltpu.make_async_remote_copy(src_ref=block, dst_ref=block, send_sem=sems[0].at[t], recv_sem=sems[1].at[t],
                                             device_id=(x, y, 1 - c), device_id_type=MESH).start()

    def wait(_, full, sems):
        x, y, c = _place()
        for t in range(n):
            three = full[t].at[pl.ds(0, 3 * (full[t].shape[0] // N_DEV)), :]
            w = _whole(three, three, sems[0].at[t], sems[1].at[t], (x, y, c))
            w.wait_send()
            w.wait_recv()

    return _Ride([], fulls, start, wait, [(n,), (n,)])


def _join_rides(rides):
    rides = [r for r in rides if r is not None]
    if len(rides) <= 1:
        return rides[0] if rides else None

    def parts(src, dst, sems):
        so = do = mo = 0
        for r in rides:
            yield r, src[so:so + len(r.srcs)], dst[do:do + len(r.dsts)], sems[mo:mo + len(r.sems)]
            so, do, mo = so + len(r.srcs), do + len(r.dsts), mo + len(r.sems)

    def start(src, dst, sems):
        for r, s, d, m in parts(src, dst, sems):
            r.start(s, d, m)

    def wait(src, dst, sems):
        for r, s, d, m in parts(src, dst, sems):
            r.wait(s, d, m)

    return _Ride([a for r in rides for a in r.srcs], [a for r in rides for a in r.dsts], start, wait,
                 [m for r in rides for m in r.sems])


def _chip_sum(g, land, core, *, name):
    L, _, _, r, C = g.shape

    def body(core_ref, g_ref, l_ref, o_ref):
        o_ref[...] = (g_ref[...].astype(F32) + l_ref[...].astype(F32)).astype(BF16)

    grid_spec = pltpu.PrefetchScalarGridSpec(
        num_scalar_prefetch=1, grid=(4, L),
        in_specs=[pl.BlockSpec((None, None, None, r, C), lambda k, l, core_ref: (l, k, core_ref[0], 0, 0)),
                  pl.BlockSpec((None, None, r, C), lambda k, l, core_ref: (k, l, 0, 0))],
        out_specs=pl.BlockSpec((None, None, r, C), lambda k, l, core_ref: (k, l, 0, 0)))
    return pl.pallas_call(body, name=name, out_shape=jax.ShapeDtypeStruct((4, L, r, C), BF16), grid_spec=grid_spec,
                          compiler_params=_cparams(("parallel", "parallel")))(core, g, land)


def _final_sum(sums, land, chip, into, layer, *, name):
    _, _, r, C = sums.shape

    def body(chip_ref, s_ref, a_ref, b_ref, c_ref, _, o_ref):
        o_ref[...] = ((s_ref[...].astype(F32) + a_ref[...].astype(F32)) + b_ref[...].astype(F32)) + c_ref[...].astype(F32)

    slot = lambda j: pl.BlockSpec((None, None, r, C), lambda i, chip_ref: (j, 0, 0, 0))
    grid_spec = pltpu.PrefetchScalarGridSpec(
        num_scalar_prefetch=1, grid=(1,),
        in_specs=[pl.BlockSpec((None, None, r, C), lambda i, chip_ref: (chip_ref[0], 0, 0, 0)), slot(0), slot(1), slot(2), ANY],
        out_specs=pl.BlockSpec((None, r, C), lambda i, chip_ref: (layer, 0, 0)))
    return pl.pallas_call(body, name=name, out_shape=jax.ShapeDtypeStruct(into.shape, F32), grid_spec=grid_spec,
                          input_output_aliases={5: 0}, compiler_params=_cparams(("arbitrary",)))(chip, sums, land, land, land, into)


def _exchange(v, reduce, *, name):
    R, C = v.shape

    def body(v_ref, o_ref, *scratch):
        if reduce:
            buf, send_sems, recv_sems = scratch
        else:
            buf = o_ref
            send_sems, recv_sems = scratch
        x, y, c = _place()
        me = 4 * x + 2 * y + c
        buf[me] = v_ref[...]
        copies = []
        for k in range(1, N_DEV):
            kx, ky, kc = (k >> 2) & 1, (k >> 1) & 1, k & 1
            peer = (1 - x if kx else x, 1 - y if ky else y, 1 - c if kc else c)
            cp = pltpu.make_async_remote_copy(src_ref=v_ref, dst_ref=buf.at[me], send_sem=send_sems.at[k - 1],
                                              recv_sem=recv_sems.at[k - 1], device_id=peer, device_id_type=MESH)
            cp.start()
            copies.append(cp)
        for cp in copies:
            cp.wait_recv()
        for cp in copies:
            cp.wait_send()
        if reduce:
            acc = buf[0]
            for d in range(1, N_DEV):
                acc = acc + buf[d]
            o_ref[...] = acc

    sems = [pltpu.SemaphoreType.DMA((N_DEV - 1,)), pltpu.SemaphoreType.DMA((N_DEV - 1,))]
    vm = pl.BlockSpec(memory_space=pltpu.VMEM)
    if reduce:
        return pl.pallas_call(body, name=name, out_shape=jax.ShapeDtypeStruct((R, C), F32), in_specs=[vm], out_specs=vm,
                              scratch_shapes=[pltpu.VMEM((N_DEV, R, C), F32)] + sems)(v)
    return pl.pallas_call(body, name=name, out_shape=jax.ShapeDtypeStruct((N_DEV, R, C), F32), in_specs=[vm], out_specs=vm,
                          scratch_shapes=sems)(v)


def _local_step(x, mem, target, norms, conv_w, depth, n_a, get_w, next_ride, ride_done, grad_ready, grad_ride, grad_landed):
    S, D = x.shape
    main = D - MEM_WIDTH
    heads = main // HEAD_DIM
    row = lambda v: v.reshape(1, D)

    mem_n = _rmsnorm(mem, row(norms["mem_norm"]), name="mem_norm")
    saved = []
    kv = hk = x_kv = w_kv = None
    def carry_mm(*args, **kwargs):
        ride = next_ride()
        if ride is None:
            return _mm(*args, **kwargs)
        out, landed = _mm(*args, ride=ride, **kwargs)
        ride_done(landed)
        return out

    for i in range(depth):
        W = functools.partial(get_w, i)
        st = {"x": x}
        h = _rmsnorm(x, row(norms["mix_norm"][i]), name=f"mix_norm{i}")
        mkv = _mm(mem_n, W("mkv"), "nn", BF16, name=f"mkv{i}")
        if i < n_a:
            p = carry_mm(h, W("a"), "nt", BF16, name=f"a_in{i}")
            y_main = _conv_fwd(p, conv_w[i], main, name=f"conv{i}")
            qblk = 3 * main // MEM_WIDTH
        else:
            p = carry_mm(h, W("b"), "nn", BF16, name=f"b_in{i}")
            y_main, st["o32"] = _sb_fwd(p, kv, heads, name=f"sb{i}")
            qblk = main // MEM_WIDTH
        y = _memattn_fwd(p, qblk, mkv, y_main, name=f"memattn{i}")
        xm = carry_mm(y, W("o"), "nn", F32, residual=x, name=f"w_o{i}")
        h2 = _rmsnorm(xm, row(norms["ffn_norm"][i]), name=f"ffn_norm{i}")
        (act, silu, uds), landed = _ffn_up(h2, W("g"), W("u"), name=f"ffn_up{i}", ride=next_ride())
        ride_done(landed)
        x = carry_mm(act, W("d"), "nn", F32, residual=xm, name=f"w_down{i}")
        st.update(h=h, mkv=mkv, p=p, qblk=qblk, y=y, xm=xm, h2=h2, silu=silu, uds=uds, act=act)
        saved.append(st)
        if i == n_a - 1:
            x_kv, w_kv = x, W("kv")
            hk = _rmsnorm(x, row(norms["kv_norm"]), name="kv_norm")
            kv = _mm(hk, w_kv, "nt", BF16, name="w_kv")

    dx, dxb, dg_final, loss = _loss_head(x, row(norms["final_norm"]), target, name="loss_head")

    dg_mix, dg_ffn, dconv = [None] * depth, [None] * depth, [None] * n_a
    dmem_n = dk = dv = dg_kv = g_kv = None
    def carry_back(*args, **kwargs):
        ride = grad_ride()
        if ride is None:
            return _mm(*args, **kwargs)
        out, landed = _mm(*args, ride=ride, **kwargs)
        grad_landed(landed)
        return out

    def carry_back_norm(*args, **kwargs):
        outs, landed = _mm_norm_bwd(*args, ride=grad_ride(), **kwargs)
        grad_landed(landed)
        return outs

    for i in reversed(range(depth)):
        st = saved[i]
        W, key = functools.partial(get_w, i), dict(_layer_keys(i, n_a))
        dgate, dup = _ffn_down_bwd(dxb, W("d"), st["silu"], st["uds"], name=f"ffn_down_bwd{i}")
        grad_ready(("d", key["d"]), carry_back(st["act"], dxb, "tn", BF16, name=f"g_w_down{i}"))
        grad_ready(("g", key["g"]), carry_back(dgate, st["h2"], "tn", BF16, name=f"g_w_gate{i}"))
        grad_ready(("u", key["u"]), carry_back(dup, st["h2"], "tn", BF16, name=f"g_w_up{i}"))
        dx, dxb, dg_ffn[i] = carry_back_norm([(dgate, W("g")), (dup, W("u"))], "nn", st["xm"], row(norms["ffn_norm"][i]), dx,
                                             name=f"d_h2_{i}")
        dy = _mm(dxb, W("o"), "nt", BF16, name=f"d_y{i}")
        grad_ready(("o", key["o"]), _mm(st["y"], dxb, "tn", BF16, name=f"g_w_o{i}"))
        dqmem, dmkv = _memattn_bwd(st["p"], st["qblk"], st["mkv"], dy, main // MEM_WIDTH, name=f"memattn_bwd{i}")
        grad_ready(("mkv", key["mkv"]), _mm(mem_n, dmkv, "tn", BF16, name=f"g_w_mem_kv{i}"))
        dmem_n = _mm(dmkv, W("mkv"), "nt", F32, residual=dmem_n, name=f"d_mem_n{i}")
        if i < n_a:
            db, dc, du, dconv[i] = _conv_bwd(st["p"], conv_w[i], dy, main, name=f"conv_bwd{i}")
            dp = jnp.concatenate([db, dc, du, dqmem], axis=1)
            grad_ready(("a", key["a"]), carry_back(dp, st["h"], "tn", BF16, name=f"g_a_in{i}"))
            w_in, form = W("a"), "nn"
        else:
            dq, dk, dv = _sb_bwd(st["p"], kv, st["o32"], dy, heads, dk, dv, name=f"sb_bwd{i}")
            dp = jnp.concatenate([dq, dqmem], axis=1)
            grad_ready(("b", key["b"]), carry_back(st["h"], dp, "tn", BF16, name=f"g_b_in{i}"))
            w_in, form = W("b"), "nt"
        dx, dxb, dg_mix[i] = carry_back_norm([(dp, w_in)], form, st["x"], row(norms["mix_norm"][i]), dx, name=f"d_h{i}")
        if i == n_a:
            dkv = jnp.concatenate([dk, dv], axis=1).astype(BF16)
            grad_ready(("kv", 0), carry_back(dkv, hk, "tn", BF16, name="g_w_kv"))
            dx, dxb, dg_kv = carry_back_norm([(dkv, w_kv)], "nn", x_kv, row(norms["kv_norm"]), dx, name="d_hk")
    _, _, dg_mem = _rmsnorm_bwd(mem, row(norms["mem_norm"]), dmem_n, None, name="mem_norm_bwd")

    small = {"mix_norm": jnp.concatenate(dg_mix, axis=0), "ffn_norm": jnp.concatenate(dg_ffn, axis=0), "kv_norm": dg_kv[0],
             "mem_norm": dg_mem[0], "final_norm": dg_final[0], "conv_w": jnp.stack(dconv, axis=0)}
    return loss, dx, small


_COL_SHARDED = ("a", "kv", "g", "u")
_NAMES = {"a": "a_in", "kv": "w_kv_shared", "g": "w_gate", "u": "w_up", "b": "b_in", "o": "w_o", "d": "w_down", "mkv": "w_mem_kv"}
_ORDER = ("a", "kv", "g", "u", "d", "b", "o", "mkv")
_WEIGHTS = ("mix_norm", "a_in", "conv_w", "b_in", "kv_norm", "w_kv_shared", "w_mem_kv", "w_o", "ffn_norm", "w_gate", "w_up",
            "w_down", "mem_norm", "final_norm")


def _layer_keys(i, n_a):
    keys = [("a", i) if i < n_a else ("b", i - n_a), ("g", i), ("u", i), ("d", i), ("o", i), ("mkv", i)]
    return keys + [("kv", 0)] if i == n_a - 1 else keys


def _gather_groups(i, n_a):
    first, rest = _layer_keys(i, n_a)[0], dict(_layer_keys(i, n_a)[1:])
    small = [(k, rest[k]) for k in ("o", "mkv", "kv") if k in rest]
    return [[first], small, [("g", rest["g"]), ("u", rest["u"])], [("d", rest["d"])]]


def _canonical(key, w):
    w3 = w if w.ndim == 3 else w[None]
    if key in _COL_SHARDED:
        w3 = jnp.transpose(w3, (0, 2, 1))
    return w3


def _uncanonical(key, g3, like):
    if key in _COL_SHARDED:
        g3 = jnp.transpose(g3, (0, 2, 1))
    return g3.reshape(like.shape)


def _pad_rows(flat, C):
    n = flat.shape[0]
    rows = -(-n // C)
    return jnp.pad(flat, (0, rows * C - n)).reshape(rows, C)


def kernel(x, mem, mix_norm, a_in, conv_w, b_in, kv_norm, w_kv_shared, w_mem_kv, w_o, ffn_norm, w_gate, w_up, w_down, mem_norm, final_norm, loss_target, m_mix_norm, m_a_in, m_conv_w, m_b_in, m_kv_norm, m_w_kv_shared, m_w_mem_kv, m_w_o, m_ffn_norm, m_w_gate, m_w_up, m_w_down, m_mem_norm, m_final_norm, v_mix_norm, v_a_in, v_conv_w, v_b_in, v_kv_norm, v_w_kv_shared, v_w_mem_kv, v_w_o, v_ffn_norm, v_w_gate, v_w_up, v_w_down, v_mem_norm, v_final_norm):
    weights = dict(mix_norm=mix_norm, a_in=a_in, conv_w=conv_w, b_in=b_in, kv_norm=kv_norm, w_kv_shared=w_kv_shared,
                   w_mem_kv=w_mem_kv, w_o=w_o, ffn_norm=ffn_norm, w_gate=w_gate, w_up=w_up, w_down=w_down,
                   mem_norm=mem_norm, final_norm=final_norm)
    moments_m = dict(mix_norm=m_mix_norm, a_in=m_a_in, conv_w=m_conv_w, b_in=m_b_in, kv_norm=m_kv_norm,
                     w_kv_shared=m_w_kv_shared, w_mem_kv=m_w_mem_kv, w_o=m_w_o, ffn_norm=m_ffn_norm, w_gate=m_w_gate,
                     w_up=m_w_up, w_down=m_w_down, mem_norm=m_mem_norm, final_norm=m_final_norm)
    moments_v = dict(mix_norm=v_mix_norm, a_in=v_a_in, conv_w=v_conv_w, b_in=v_b_in, kv_norm=v_kv_norm,
                     w_kv_shared=v_w_kv_shared, w_mem_kv=v_w_mem_kv, w_o=v_w_o, ffn_norm=v_ffn_norm, w_gate=v_w_gate,
                     w_up=v_w_up, w_down=v_w_down, mem_norm=v_mem_norm, final_norm=v_final_norm)
    D = x.shape[-1]
    depth, n_a = w_o.shape[0], a_in.shape[0]
    xi, yi, ci = _place()
    me = 4 * xi + 2 * yi + ci
    core = ci.reshape(1).astype(jnp.int32)
    chip = (2 * xi + yi).reshape(1).astype(jnp.int32)

    cw_shape = conv_w.shape
    cw_rows = _pad_rows(conv_w.reshape(-1), D)
    cw_rows = jnp.pad(cw_rows, ((0, 8 - cw_rows.shape[0]), (0, 0)))
    cw_gathered = _exchange(cw_rows, False, name="gather_conv_w")
    n_cw = cw_shape[0] * cw_shape[1] * cw_shape[2]
    cw_all = cw_gathered.reshape(N_DEV, -1)[:, :n_cw].reshape((N_DEV,) + cw_shape)
    conv_full = jnp.transpose(cw_all, (1, 2, 0, 3)).reshape(cw_shape[0], cw_shape[1], N_DEV * cw_shape[2])

    shard3 = {k: _canonical(k, weights[_NAMES[k]]).astype(BF16) for k in _ORDER}
    keys0 = _layer_keys(0, n_a)
    fulls0 = _all_gather_weights([shard3[k][l][None] for k, l in keys0], name="all_gather_layer0")
    full = {kl: f[0] for kl, f in zip(keys0, fulls0)}

    groups = [grp for i in range(1, depth) for grp in _gather_groups(i, n_a)]
    carried, riding = [0], []

    def next_ride():
        n = carried[0]
        carried[0] += 1
        second = groups[n - 1] if 1 <= n <= len(groups) else []
        first = groups[n] if n < len(groups) else []
        if not second + first:
            return None
        riding.append(second + first)
        return _join_rides([_gather_ride_2([full[kl] for kl in second]) if second else None,
                            _gather_ride_1([(shard3[k], l) for k, l in first]) if first else None])

    def ride_done(landed):
        if landed:
            full.update(zip(riding.pop(), landed))

    def get_w(i, key):
        return full[(key, dict(_layer_keys(i, n_a))[key])]

    fresh, summed, reduced, travelling = [], [], {}, []

    def grad_ready(kl, g):
        fresh.append((kl, g.reshape(1, 4, 2, g.shape[0] // N_DEV, g.shape[1])))

    def grad_ride():
        if not fresh + summed:
            return None
        travelling.append((list(summed), list(fresh)))
        ride = _join_rides([_chips_ride([s for _, s in summed]) if summed else None,
                            _sibling_ride([g for _, g in fresh]) if fresh else None])
        summed.clear()
        fresh.clear()
        return ride

    def grad_landed(landed):
        if not landed:
            return
        between_chips, to_sibling = travelling.pop()
        for (kl, s), land in zip(between_chips, landed):
            reduced[kl] = (s, land)
        for (kl, g), land in zip(to_sibling, landed[len(between_chips):]):
            summed.append((kl, _chip_sum(g, land, core, name=f"chip_sum_{kl[0]}{kl[1]}")))

    norms = {k: weights[k] for k in ("mix_norm", "ffn_norm", "kv_norm", "mem_norm", "final_norm")}
    loss, grad_x, small = _local_step(x[0], mem[0], loss_target[0], norms, conv_full, depth, n_a, get_w, next_ride, ride_done,
                                      grad_ready, grad_ride, grad_landed)
    for tail in range(2):
        ride = grad_ride()
        if ride is not None:
            grad_landed(_carry_alone(ride, name=f"reduce_scatter_tail{tail}"))

    stacks = {k: lax.empty(shard3[k].shape, F32) for k in _ORDER}
    for (k, l), (s, land) in reduced.items():
        stacks[k] = _final_sum(s, land, chip, stacks[k], l, name=f"final_sum_{k}{l}")
    grads = {_NAMES[k]: _uncanonical(k, stacks[k], weights[_NAMES[k]]) for k in _ORDER}

    order = ("mix_norm", "ffn_norm", "kv_norm", "mem_norm", "final_norm", "conv_w")
    flat = jnp.concatenate([small[k].reshape(-1) for k in order] + [loss[0, :1]])
    n_flat = flat.shape[0]
    rows = _pad_rows(flat, D)
    rows = jnp.pad(rows, ((0, (-rows.shape[0]) % 8), (0, 0)))
    total = _exchange(rows, True, name="all_reduce_small").reshape(-1)[:n_flat]
    off = 0
    for k in order:
        n = small[k].size
        grads[k] = total[off:off + n].reshape(small[k].shape)
        off += n
    loss_total = total[off]
    grads["conv_w"] = lax.dynamic_slice_in_dim(grads["conv_w"], me * cw_shape[2], cw_shape[2], axis=2)

    deltas, new_m, new_v = {}, {}, {}
    for k in _WEIGHTS:
        w = weights[k]
        two = (lambda a: a.reshape(-1, a.shape[-1])) if w.ndim > 1 else (lambda a: a.reshape(1, -1))
        d, nm, nv = _adamw(two(w), two(grads[k]), two(moments_m[k]), two(moments_v[k]), name=f"adamw_{k}")
        deltas[k], new_m[k], new_v[k] = d.reshape(w.shape), nm.reshape(w.shape), nv.reshape(w.shape)

    return (loss_total, grad_x[None], *[grads[k] for k in _WEIGHTS], *[deltas[k] for k in _WEIGHTS],
            *[new_m[k] for k in _WEIGHTS], *[new_v[k] for k in _WEIGHTS])
```

```python
import functools
import math

import jax
import jax.numpy as jnp
from jax import lax
from jax.experimental import pallas as pl
from jax.experimental.pallas import tpu as pltpu

F32 = jnp.float32
BF16 = jnp.bfloat16
MESH = pl.DeviceIdType.MESH

HEAD_DIM = 64
MEM_HEADS = 4
MEM_WIDTH = MEM_HEADS * HEAD_DIM
EPS = 1e-6
LANES = 128
BF16_ROWS = 16
VMEM_LIMIT = 56 * 1024 * 1024
N_DEV = 8

ADAM_LR = 0.001
ADAM_B1 = 0.9
ADAM_B2 = 0.999
ADAM_EPS = 1e-08
ADAM_WD = 0.01
ADAM_STEP = 10

ANY = pl.BlockSpec(memory_space=pl.ANY)


def _cparams(sem=None):
    return pltpu.CompilerParams(dimension_semantics=sem, vmem_limit_bytes=VMEM_LIMIT)


def _pick(n, cands):
    for c in cands:
        if n % c == 0:
            return c
    raise ValueError(f"no tile for {n} in {cands}")


def _mm(a, b, form, out_dtype, *, name, residual=None, ride=None):
    if form == "tn":
        K, M = a.shape
    else:
        M, K = a.shape
    if form == "nt":
        N, K2 = b.shape
    else:
        K2, N = b.shape
    assert K == K2, (name, a.shape, b.shape)
    wide = (1408, 1280, 1024, 768, 512, 256, 128)
    tm = _pick(M, wide if form == "tn" else (1024, 512, 256, 128))
    tn = _pick(N, wide)
    tk = _pick(K, (1024, 1408, 1280, 768, 512, 256))
    nk = K // tk
    dims = {"nn": (((1,), (0,)), ((), ())), "nt": (((1,), (1,)), ((), ())), "tn": (((0,), (0,)), ((), ()))}[form]
    a_bytes, b_bytes = M * K * a.dtype.itemsize, N * K * b.dtype.itemsize
    n_outer = nk == 1 and (N // tn) * a_bytes + b_bytes < a_bytes + (M // tm) * b_bytes
    ij = (lambda g0, g1: (g1, g0)) if n_outer else (lambda g0, g1: (g0, g1))

    def spec(block, f):
        return pl.BlockSpec(block, lambda g0, g1, k: f(*ij(g0, g1), k))

    a_spec = spec((tk, tm), lambda i, j, k: (k, i)) if form == "tn" else spec((tm, tk), lambda i, j, k: (i, k))
    b_spec = spec((tn, tk), lambda i, j, k: (j, k)) if form == "nt" else spec((tk, tn), lambda i, j, k: (k, j))
    out_spec = spec((tm, tn), lambda i, j, k: (i, j))
    operands, in_specs = [a, b], [a_spec, b_spec]
    has_res = residual is not None
    if has_res:
        operands.append(residual)
        in_specs.append(out_spec)
    grid = (N // tn, M // tm, nk) if n_outer else (M // tm, N // tn, nk)

    def body(*refs):
        a_ref, b_ref = refs[0], refs[1]
        r_ref = refs[2] if has_res else None
        o_ref = refs[2 + int(has_res)]
        acc_ref = refs[-1]
        part = lax.dot_general(a_ref[...].astype(BF16), b_ref[...].astype(BF16), dims, preferred_element_type=F32)

        def finish(total):
            if has_res:
                total = total + r_ref[...].astype(F32)
            o_ref[...] = total.astype(out_dtype)

        if nk == 1:
            finish(part)
        else:
            k = pl.program_id(2)

            @pl.when(k == 0)
            def _():
                acc_ref[...] = part

            @pl.when(jnp.logical_and(k > 0, k < nk - 1))
            def _():
                acc_ref[...] += part

            @pl.when(k == nk - 1)
            def _():
                finish(acc_ref[...] + part)

    def edges():
        ids = [pl.program_id(d) for d in range(3)]
        first = jnp.logical_and(jnp.logical_and(ids[0] == 0, ids[1] == 0), ids[2] == 0)
        last = jnp.logical_and(jnp.logical_and(ids[0] == grid[0] - 1, ids[1] == grid[1] - 1), ids[2] == grid[2] - 1)
        return first, last

    (out,), landed = _ride_call(
        body, ride, edges, name=name, out_shape=[jax.ShapeDtypeStruct((M, N), out_dtype)], grid=grid, in_specs=in_specs,
        out_specs=[out_spec], scratch_shapes=[pltpu.VMEM((tm, tn), F32)],
        compiler_params=_cparams(("parallel", "parallel", "arbitrary")), operands=operands)
    return out if ride is None else (out, landed)


def _rmsnorm(x, g, *, name):
    R, D = x.shape
    tr = _pick(R, (512, 256))

    def body(x_ref, g_ref, o_ref):
        xv = x_ref[...]
        r = lax.rsqrt(jnp.mean(xv * xv, axis=-1, keepdims=True) + EPS)
        o_ref[...] = (xv * r * g_ref[...]).astype(BF16)

    return pl.pallas_call(
        body, name=name, out_shape=jax.ShapeDtypeStruct((R, D), BF16), grid=(R // tr,),
        in_specs=[pl.BlockSpec((tr, D), lambda i: (i, 0)), pl.BlockSpec((1, D), lambda i: (0, 0))],
        out_specs=pl.BlockSpec((tr, D), lambda i: (i, 0)), compiler_params=_cparams(("parallel",)),
    )(x, g)


def _rmsnorm_bwd(x, g, dh, dx_in, *, name):
    R, D = x.shape
    tr = _pick(R, (512, 256))
    has_in = dx_in is not None

    def body(*refs):
        x_ref, g_ref, dh_ref = refs[:3]
        dxi_ref = refs[3] if has_in else None
        dx_ref, dxb_ref, dg_ref = refs[3 + int(has_in):]
        xv = x_ref[...]
        r = lax.rsqrt(jnp.mean(xv * xv, axis=-1, keepdims=True) + EPS)
        xhat = xv * r
        dhv = dh_ref[...].astype(F32)
        dxh = dhv * g_ref[...]
        dx = r * (dxh - xhat * jnp.mean(dxh * xhat, axis=-1, keepdims=True))
        if has_in:
            dx = dx + dxi_ref[...]
        dx_ref[...] = dx
        dxb_ref[...] = dx.astype(BF16)
        part = jnp.sum(dhv * xhat, axis=0, keepdims=True)

        @pl.when(pl.program_id(0) == 0)
        def _():
            dg_ref[...] = part

        @pl.when(pl.program_id(0) > 0)
        def _():
            dg_ref[...] += part

    row = pl.BlockSpec((tr, D), lambda i: (i, 0))
    vec = pl.BlockSpec((1, D), lambda i: (0, 0))
    ops = [x, g, dh] + ([dx_in] if has_in else [])
    return pl.pallas_call(
        body, name=name,
        out_shape=(jax.ShapeDtypeStruct((R, D), F32), jax.ShapeDtypeStruct((R, D), BF16), jax.ShapeDtypeStruct((1, D), F32)),
        grid=(R // tr,), in_specs=[row, vec, row] + ([row] if has_in else []), out_specs=(row, row, vec),
        compiler_params=_cparams(("arbitrary",)),
    )(*ops)


class _Ride:
    def __init__(self, srcs, dsts, start, wait, sems):
        self.srcs, self.dsts, self.start, self.wait, self.sems = list(srcs), list(dsts), start, wait, list(sems)


def _ride_call(body, ride, edges, *, name, out_shape, grid, in_specs, out_specs, scratch_shapes, compiler_params, operands):
    n_in, n_out, n_scr = len(in_specs), len(out_specs), len(scratch_shapes)
    if ride is None:
        outs = pl.pallas_call(body, name=name, out_shape=tuple(out_shape), grid=grid, in_specs=list(in_specs),
                              out_specs=tuple(out_specs), scratch_shapes=list(scratch_shapes),
                              compiler_params=compiler_params)(*operands)
        return tuple(outs), []
    ns, nd = len(ride.srcs), len(ride.dsts)

    def riding(*refs):
        ins, rin = refs[:n_in], refs[n_in:n_in + ns + nd]
        outs = refs[n_in + ns + nd:n_in + ns + nd + n_out]
        scratch = refs[n_in + ns + 2 * nd + n_out:]
        sems = scratch[n_scr:]
        first, last = edges()

        @pl.when(first)
        def _():
            ride.start(rin[:ns], rin[ns:], sems)

        body(*ins, *outs, *scratch[:n_scr])

        @pl.when(last)
        def _():
            ride.wait(rin[:ns], rin[ns:], sems)

    outs = pl.pallas_call(
        riding, name=name, out_shape=(*out_shape, *[jax.ShapeDtypeStruct(d.shape, d.dtype) for d in ride.dsts]), grid=grid,
        in_specs=[*in_specs, *[ANY] * (ns + nd)], out_specs=(*out_specs, *[ANY] * nd),
        scratch_shapes=[*scratch_shapes, *[pltpu.SemaphoreType.DMA(s) for s in ride.sems]],
        input_output_aliases={n_in + ns + d: n_out + d for d in range(nd)}, compiler_params=compiler_params,
    )(*operands, *ride.srcs, *ride.dsts)
    return tuple(outs[:n_out]), list(outs[n_out:])


def _mm_norm_bwd(parts, form, x, g, dx_in, *, name, ride=None):
    S, K = parts[0][0].shape
    D = x.shape[1]
    tm = _pick(S, (512, 256))
    tk = _pick(K, (1024, 1408, 1280, 768, 512, 256))
    nk, P = K // tk, len(parts)
    dims = _NN if form == "nn" else _NT

    ni, nsteps = S // tm, P * nk

    def body(*refs):
        ab = refs[:2 * P]
        x_ref, g_ref, dxi_ref, dx_ref, dxb_ref, dg_ref, acc_ref = refs[2 * P:]
        k, i = pl.program_id(0), pl.program_id(1)
        rows = pl.ds(pl.multiple_of(i * tm, tm), tm)

        @pl.when(k == 0)
        def _():
            acc_ref[rows, :] = jnp.zeros((tm, D), F32)

        for p in range(P):
            @pl.when(jnp.logical_and(k >= p * nk, k < (p + 1) * nk))
            def _():
                acc_ref[rows, :] += _dot(ab[2 * p][...], ab[2 * p + 1][...], dims)

        @pl.when(k == nsteps - 1)
        def _():
            xv = x_ref[...]
            r = lax.rsqrt(jnp.mean(xv * xv, axis=-1, keepdims=True) + EPS)
            xhat = xv * r
            dhv = acc_ref[rows, :]
            dxh = dhv * g_ref[...]
            dx = r * (dxh - xhat * jnp.mean(dxh * xhat, axis=-1, keepdims=True)) + dxi_ref[...]
            dx_ref[...] = dx
            dxb_ref[...] = dx.astype(BF16)
            part = jnp.sum(dhv * xhat, axis=0, keepdims=True)

            @pl.when(i == 0)
            def _():
                dg_ref[...] = part

            @pl.when(i > 0)
            def _():
                dg_ref[...] += part

    def kk(p):
        return lambda k: jnp.clip(k - p * nk, 0, nk - 1)

    def active_row(p):
        return lambda k, i: jnp.where(jnp.logical_and(k >= p * nk, k < (p + 1) * nk), i, 0)

    in_specs, operands = [], []
    for p, (a, b) in enumerate(parts):
        in_specs.append(pl.BlockSpec((tm, tk), lambda k, i, f=kk(p), r=active_row(p): (r(k, i), f(k))))
        in_specs.append(pl.BlockSpec((tk, D), lambda k, i, f=kk(p): (f(k), 0)) if form == "nn"
                        else pl.BlockSpec((D, tk), lambda k, i, f=kk(p): (0, f(k))))
        operands += [a, b]
    row = pl.BlockSpec((tm, D), lambda k, i: (jnp.where(k == nsteps - 1, i, 0), 0))
    vec = pl.BlockSpec((1, D), lambda k, i: (0, 0))

    def edges():
        k, i = pl.program_id(0), pl.program_id(1)
        return jnp.logical_and(i == 0, k == 0), jnp.logical_and(i == ni - 1, k == nsteps - 1)

    return _ride_call(
        body, ride, edges, name=name,
        out_shape=(jax.ShapeDtypeStruct((S, D), F32), jax.ShapeDtypeStruct((S, D), BF16), jax.ShapeDtypeStruct((1, D), F32)),
        grid=(nsteps, ni), in_specs=in_specs + [row, vec, row], out_specs=(row, row, vec),
        scratch_shapes=[pltpu.VMEM((S, D), F32)], compiler_params=_cparams(("arbitrary", "arbitrary")),
        operands=[*operands, x, g, dx_in])


def _loss_head(x, g, target, *, name):
    R, D = x.shape
    tr = _pick(R, (512, 256))

    def body(x_ref, g_ref, t_ref, dx_ref, dxb_ref, dg_ref, loss_ref):
        xv = x_ref[...]
        gv = g_ref[...]
        r = lax.rsqrt(jnp.mean(xv * xv, axis=-1, keepdims=True) + EPS)
        xhat = xv * r
        err = xhat * gv - t_ref[...]
        loss = 0.5 * jnp.sum(jnp.mean(err * err, axis=-1, keepdims=True), axis=0, keepdims=True)
        dy = err * (1.0 / D)
        dxh = dy * gv
        dx = r * (dxh - xhat * jnp.mean(dxh * xhat, axis=-1, keepdims=True))
        dx_ref[...] = dx
        dxb_ref[...] = dx.astype(BF16)
        dg = jnp.sum(dy * xhat, axis=0, keepdims=True)
        lossv = jnp.broadcast_to(loss, (1, LANES))

        @pl.when(pl.program_id(0) == 0)
        def _():
            dg_ref[...] = dg
            loss_ref[...] = lossv

        @pl.when(pl.program_id(0) > 0)
        def _():
            dg_ref[...] += dg
            loss_ref[...] += lossv

    row = pl.BlockSpec((tr, D), lambda i: (i, 0))
    vec = pl.BlockSpec((1, D), lambda i: (0, 0))
    return pl.pallas_call(
        body, name=name,
        out_shape=(jax.ShapeDtypeStruct((R, D), F32), jax.ShapeDtypeStruct((R, D), BF16), jax.ShapeDtypeStruct((1, D), F32),
                   jax.ShapeDtypeStruct((1, LANES), F32)),
        grid=(R // tr,), in_specs=[row, vec, row], out_specs=(row, row, vec, pl.BlockSpec((1, LANES), lambda i: (0, 0))),
        compiler_params=_cparams(("arbitrary",)),
    )(x, g, target)


def _conv_taps(gv, S):
    t = lax.broadcasted_iota(jnp.int32, gv.shape, 0)
    g1 = jnp.where(t >= 1, pltpu.roll(gv, 1, 0), 0.0)
    g2 = jnp.where(t >= 2, pltpu.roll(gv, 2, 0), 0.0)
    return g1, g2


def _conv_fwd(p, w, main, *, name):
    S = p.shape[0]
    tc = LANES
    nb = main // tc

    def body(b_ref, c_ref, u_ref, w_ref, y_ref):
        gv = c_ref[...].astype(F32) * u_ref[...].astype(F32)
        g1, g2 = _conv_taps(gv, S)
        cv = w_ref[0:1, :] * g2 + w_ref[1:2, :] * g1 + w_ref[2:3, :] * gv
        y_ref[...] = (b_ref[...].astype(F32) * cv).astype(BF16)

    col = lambda off: pl.BlockSpec((S, tc), lambda j: (0, off + j))
    return pl.pallas_call(
        body, name=name, out_shape=jax.ShapeDtypeStruct((S, p.shape[1] - 2 * main), BF16), grid=(nb,),
        in_specs=[col(0), col(nb), col(2 * nb), pl.BlockSpec((3, tc), lambda j: (0, j))],
        out_specs=pl.BlockSpec((S, tc), lambda j: (0, j)), compiler_params=_cparams(("parallel",)),
    )(p, p, p, w)


def _conv_bwd(p, w, dy, main, *, name):
    S = p.shape[0]
    tc = LANES
    nb = main // tc

    def body(b_ref, c_ref, u_ref, w_ref, dy_ref, db_ref, dc_ref, du_ref, dw_ref):
        cvv, uv = c_ref[...].astype(F32), u_ref[...].astype(F32)
        gv = cvv * uv
        g1, g2 = _conv_taps(gv, S)
        w0, w1, w2 = w_ref[0:1, :], w_ref[1:2, :], w_ref[2:3, :]
        dyv = dy_ref[...].astype(F32)
        db_ref[...] = (dyv * (w0 * g2 + w1 * g1 + w2 * gv)).astype(BF16)
        dcv = dyv * b_ref[...].astype(F32)
        t = lax.broadcasted_iota(jnp.int32, dcv.shape, 0)
        n1 = jnp.where(t <= S - 2, pltpu.roll(dcv, S - 1, 0), 0.0)
        n2 = jnp.where(t <= S - 3, pltpu.roll(dcv, S - 2, 0), 0.0)
        dg = w2 * dcv + w1 * n1 + w0 * n2
        dc_ref[...] = (dg * uv).astype(BF16)
        du_ref[...] = (dg * cvv).astype(BF16)
        dw_ref[0:1, :] = jnp.sum(dcv * g2, axis=0, keepdims=True)
        dw_ref[1:2, :] = jnp.sum(dcv * g1, axis=0, keepdims=True)
        dw_ref[2:3, :] = jnp.sum(dcv * gv, axis=0, keepdims=True)

    col = lambda off: pl.BlockSpec((S, tc), lambda j: (0, off + j))
    out = jax.ShapeDtypeStruct((S, main), BF16)
    return pl.pallas_call(
        body, name=name, out_shape=(out, out, out, jax.ShapeDtypeStruct((3, main), F32)), grid=(nb,),
        in_specs=[col(0), col(nb), col(2 * nb), pl.BlockSpec((3, tc), lambda j: (0, j)), col(0)],
        out_specs=(col(0), col(0), col(0), pl.BlockSpec((3, tc), lambda j: (0, j))),
        compiler_params=_cparams(("parallel",)),
    )(p, p, p, w, dy)


def _head_mask(width, h):
    lane = lax.broadcasted_iota(jnp.int32, (1, width), 1)
    return jnp.logical_and(lane >= h * HEAD_DIM, lane < (h + 1) * HEAD_DIM)


_NT = (((1,), (1,)), ((), ()))
_NN = (((1,), (0,)), ((), ()))
_TN = (((0,), (0,)), ((), ()))


def _dot(a, b, dims):
    return lax.dot_general(a, b, dims, preferred_element_type=F32)


def _mem_probs(qh, kv):
    s = _dot(qh, kv, _NT) * (1.0 / math.sqrt(HEAD_DIM))
    s = s - jnp.max(s, axis=-1, keepdims=True)
    e = jnp.exp(s)
    return e / jnp.sum(e, axis=-1, keepdims=True)


def _memattn_fwd(p, qblk, mkv, into, *, name):
    S = p.shape[0]
    M = mkv.shape[0]
    W = MEM_WIDTH
    tq = _pick(S, (512, 256))
    last = into.shape[1] // W - 1

    def body(q_ref, k_ref, v_ref, _, o_ref):
        q = q_ref[...].astype(BF16)
        kv, vv = k_ref[...], v_ref[...]
        out = jnp.zeros((tq, W), F32)
        for h in range(MEM_HEADS):
            m = _head_mask(W, h)
            pr = _mem_probs(jnp.where(m, q, jnp.zeros_like(q)), kv)
            out = jnp.where(m, _dot(pr.astype(BF16), vv, _NN), out)
        o_ref[...] = out.astype(BF16)

    return pl.pallas_call(
        body, name=name, out_shape=jax.ShapeDtypeStruct(into.shape, BF16), grid=(S // tq,),
        in_specs=[pl.BlockSpec((tq, W), lambda i: (i, qblk)), pl.BlockSpec((M, W), lambda i: (0, 0)),
                  pl.BlockSpec((M, W), lambda i: (0, 1)), ANY],
        out_specs=pl.BlockSpec((tq, W), lambda i: (i, last)), input_output_aliases={3: 0},
        compiler_params=_cparams(("parallel",)),
    )(p, mkv, mkv, into)


def _memattn_bwd(p, qblk, mkv, dy, dyblk, *, name):
    S = p.shape[0]
    M = mkv.shape[0]
    W = MEM_WIDTH
    tq = _pick(S, (512, 256))
    scale = 1.0 / math.sqrt(HEAD_DIM)

    def body(q_ref, k_ref, v_ref, do_ref, dq_ref, dkv_ref, dk_acc, dv_acc):
        q = q_ref[...].astype(BF16)
        do = do_ref[...].astype(BF16)
        kv, vv = k_ref[...], v_ref[...]
        dq = jnp.zeros((tq, W), F32)
        dk = jnp.zeros((M, W), F32)
        dv = jnp.zeros((M, W), F32)
        for h in range(MEM_HEADS):
            m = _head_mask(W, h)
            qh = jnp.where(m, q, jnp.zeros_like(q))
            doh = jnp.where(m, do, jnp.zeros_like(do))
            pr = _mem_probs(qh, kv)
            dpr = _dot(doh, vv, _NT)
            ds = (pr * (dpr - jnp.sum(dpr * pr, axis=-1, keepdims=True)) * scale).astype(BF16)
            dq = jnp.where(m, _dot(ds, kv, _NN), dq)
            dk = dk + _dot(ds, qh, _TN)
            dv = dv + _dot(pr.astype(BF16), doh, _TN)
        dq_ref[...] = dq.astype(BF16)
        i = pl.program_id(0)

        @pl.when(i == 0)
        def _():
            dk_acc[...] = dk
            dv_acc[...] = dv

        @pl.when(i > 0)
        def _():
            dk_acc[...] += dk
            dv_acc[...] += dv

        @pl.when(i == S // tq - 1)
        def _():
            dkv_ref[:, :W] = dk_acc[...].astype(BF16)
            dkv_ref[:, W:] = dv_acc[...].astype(BF16)

    kspec = lambda c: pl.BlockSpec((M, W), lambda i: (0, c))
    return pl.pallas_call(
        body, name=name,
        out_shape=(jax.ShapeDtypeStruct((S, W), BF16), jax.ShapeDtypeStruct((M, 2 * W), BF16)),
        grid=(S // tq,),
        in_specs=[pl.BlockSpec((tq, W), lambda i: (i, qblk)), kspec(0), kspec(1), pl.BlockSpec((tq, W), lambda i: (i, dyblk))],
        out_specs=(pl.BlockSpec((tq, W), lambda i: (i, 0)), pl.BlockSpec((M, 2 * W), lambda i: (0, 0))),
        scratch_shapes=[pltpu.VMEM((M, W), F32), pltpu.VMEM((M, W), F32)],
        compiler_params=_cparams(("arbitrary",)),
    )(p, mkv, mkv, dy)


SB_TQ = 256
SB_CLAMP = 80.0
SB_DEAD = 110.0


SB_CHUNK = 64


def _by_rows(fn, *arrays):
    rows = next(a for a in arrays if a is not None).shape[0]
    outs = [fn(*[None if a is None else a[r0:r0 + SB_CHUNK] for a in arrays]) for r0 in range(0, rows, SB_CHUNK)]
    return tuple(jnp.concatenate(col, axis=0) for col in zip(*outs))


def _sb_scores(qh, kb, causal):
    def chain(z, mask):
        z = jnp.clip(z, -SB_CLAMP, SB_CLAMP)
        w = 1.0 + jnp.exp(z)
        sp = jnp.log(w)
        zs = z - sp
        if mask is not None:
            sp = jnp.where(mask, sp, 0.0)
            zs = jnp.where(mask, zs, -1e30)
            w = jnp.where(mask, w, 1.0)
        return zs, sp.astype(BF16), jnp.sum(sp, axis=1, keepdims=True), w

    return _by_rows(chain, _dot(qh, kb, _NT), causal)


def _sb_weights(zs, spb, tri, carry):
    return _by_rows(lambda zs_c, t_c, c_c: (jnp.exp(zs_c - (t_c + c_c)).astype(BF16),), zs, _dot(spb, tri, _NN), carry)[0]


def _sb_live(carry):
    return jnp.min(carry) <= SB_DEAD


def _stack_heads(v, m0):
    zero = jnp.zeros_like(v)
    return jnp.concatenate([jnp.where(m0, v, zero), jnp.where(m0, zero, v)], axis=0)


def _sb_masks(tq):
    r = lax.broadcasted_iota(jnp.int32, (tq, tq), 0)
    c = lax.broadcasted_iota(jnp.int32, (tq, tq), 1)
    r2 = lax.broadcasted_iota(jnp.int32, (2 * tq, tq), 0)
    c2 = lax.broadcasted_iota(jnp.int32, (2 * tq, tq), 1)
    return (r > c).astype(BF16), (r < c).astype(BF16), (c2 < jnp.where(r2 >= tq, r2 - tq, r2)).astype(F32)


def _sb_fwd(p, kv, heads, *, name):
    S = p.shape[0]
    tq = SB_TQ
    npair = heads // 2
    later, _, causal = _sb_masks(tq)

    def body(q_ref, k_ref, v_ref, tri_ref, causal_ref, o_ref, o32_ref, carry_ref, acc_ref):
        qi = pl.program_id(1)
        tri = tri_ref[...]
        causal = causal_ref[...] > 0.5
        m0 = _head_mask(LANES, 0)
        qh = _stack_heads(q_ref[...] * jnp.asarray(1.0 / math.sqrt(HEAD_DIM), BF16), m0)

        def block(j, carry, mask):
            off = pl.multiple_of(j * tq, tq)
            kb = k_ref[pl.ds(off, tq), :]
            vb = v_ref[pl.ds(off, tq), :]
            zs, spb, sp_sum, _ = _sb_scores(qh, kb, mask)
            return carry + sp_sum, _dot(_sb_weights(zs, spb, tri, carry), vb, _NN)

        def two_blocks(j, carry, mask):
            carry, first = block(j, carry, mask)
            carry, second = block(j - 1, carry, None)
            return carry, first + second

        def keep(carry, added, fresh=False):
            carry_ref[...] = carry
            acc_ref[...] = added if fresh else acc_ref[...] + added

        zero = jnp.zeros((2 * tq, 1), F32)

        @pl.when(qi >= 1)
        def _():
            keep(*two_blocks(qi, zero, causal), fresh=True)

        @pl.when(qi == 0)
        def _():
            keep(*block(qi, zero, causal), fresh=True)

        left = jnp.maximum(qi - 1, 0)
        odd = left % 2

        @pl.when(jnp.logical_and(odd == 1, _sb_live(carry_ref[...])))
        def _():
            keep(*block(qi - 2, carry_ref[...], None))

        def pair(s):
            carry, added = two_blocks(qi - 2 - odd - 2 * s[0], carry_ref[...], None)
            keep(carry, added)
            return s[0] + 1, _sb_live(carry)

        lax.while_loop(lambda s: jnp.logical_and(s[0] < left // 2, s[1]), pair, (jnp.int32(0), _sb_live(carry_ref[...])))
        acc = acc_ref[...]
        out = jnp.where(m0, acc[:tq], acc[tq:])
        o_ref[...] = out.astype(BF16)
        o32_ref[...] = out

    W = heads * HEAD_DIM
    qspec = pl.BlockSpec((tq, LANES), lambda hp, i: (i, hp))
    return pl.pallas_call(
        body, name=name, out_shape=(jax.ShapeDtypeStruct(p.shape, BF16), jax.ShapeDtypeStruct((S, W), F32)), grid=(npair, S // tq),
        in_specs=[qspec, pl.BlockSpec((S, LANES), lambda hp, i: (0, hp)), pl.BlockSpec((S, LANES), lambda hp, i: (0, npair + hp)),
                  pl.BlockSpec((tq, tq), lambda hp, i: (0, 0)), pl.BlockSpec((2 * tq, tq), lambda hp, i: (0, 0))],
        out_specs=(qspec, qspec), scratch_shapes=[pltpu.VMEM((2 * tq, 1), F32), pltpu.VMEM((2 * tq, LANES), F32)],
        compiler_params=_cparams(("parallel", "arbitrary")),
    )(p, kv, kv, later, causal)


def _sb_bwd(p, kv, o32, dy, heads, dk_in, dv_in, *, name):
    S = p.shape[0]
    tq = SB_TQ
    npair = heads // 2
    has_in = dk_in is not None
    scale = 1.0 / math.sqrt(HEAD_DIM)

    def body(*refs):
        q_ref, k_ref, v_ref, o_ref, do_ref, tri_ref, tri_low_ref, causal_ref = refs[:8]
        dq_ref, dk_ref, dv_ref, carry_ref, gcarry_ref, acc_ref = refs[8 + 2 * int(has_in):]
        qi = pl.program_id(1)

        @pl.when(qi == 0)
        def _():
            if has_in:
                dk_ref[...] = refs[8][...]
                dv_ref[...] = refs[9][...]
            else:
                dk_ref[...] = jnp.zeros_like(dk_ref)
                dv_ref[...] = jnp.zeros_like(dv_ref)

        tri = tri_ref[...]
        tri_low = tri_low_ref[...]
        causal = causal_ref[...] > 0.5
        m0 = _head_mask(LANES, 0)
        qh = _stack_heads(q_ref[...] * jnp.asarray(scale, BF16), m0)
        do = do_ref[...]
        doh = _stack_heads(do, m0)
        dov = do.astype(F32) * o_ref[...]
        dsum = jnp.concatenate([jnp.sum(jnp.where(m0, dov, 0.0), axis=1, keepdims=True),
                                jnp.sum(jnp.where(m0, 0.0, dov), axis=1, keepdims=True)], axis=0)

        def block(j, carry, gcarry, mask):
            off = pl.multiple_of(j * tq, tq)
            kb = k_ref[pl.ds(off, tq), :]
            vb = v_ref[pl.ds(off, tq), :]
            zs, spb, sp_sum, w = _sb_scores(qh, kb, mask)
            ab = _sb_weights(zs, spb, tri, carry)

            def grads(ab_c, da_c):
                g = ab_c.astype(F32) * da_c
                return g, g.astype(BF16), jnp.sum(g, axis=1, keepdims=True)

            g, gb, g_sum = _by_rows(grads, ab, _dot(doh, vb, _NT))
            gcarry = gcarry + g_sum

            def logit_grads(g_c, w_c, low_c, left_c):
                rinv = 1.0 / w_c
                return ((g_c * rinv - (left_c + low_c) * (1.0 - rinv)).astype(BF16),)

            dzs = _by_rows(logit_grads, g, w, _dot(gb, tri_low, _NN), dsum - gcarry)[0]
            dk_ref[pl.ds(off, tq), :] += _dot(dzs, qh, _TN)
            dv_ref[pl.ds(off, tq), :] += _dot(ab, doh, _TN)
            return carry + sp_sum, gcarry, _dot(dzs, kb, _NN)

        def two_blocks(j, carry, gcarry, mask):
            carry, gcarry, first = block(j, carry, gcarry, mask)
            carry, gcarry, second = block(j - 1, carry, gcarry, None)
            return carry, gcarry, first + second

        def keep(carry, gcarry, added, fresh=False):
            carry_ref[...] = carry
            gcarry_ref[...] = gcarry
            acc_ref[...] = added if fresh else acc_ref[...] + added

        zero = jnp.zeros((2 * tq, 1), F32)

        @pl.when(qi >= 1)
        def _():
            keep(*two_blocks(qi, zero, zero, causal), fresh=True)

        @pl.when(qi == 0)
        def _():
            keep(*block(qi, zero, zero, causal), fresh=True)

        left = jnp.maximum(qi - 1, 0)
        odd = left % 2

        @pl.when(jnp.logical_and(odd == 1, _sb_live(carry_ref[...])))
        def _():
            keep(*block(qi - 2, carry_ref[...], gcarry_ref[...], None))

        def pair(s):
            carry, gcarry, added = two_blocks(qi - 2 - odd - 2 * s[0], carry_ref[...], gcarry_ref[...], None)
            keep(carry, gcarry, added)
            return s[0] + 1, _sb_live(carry)

        lax.while_loop(lambda s: jnp.logical_and(s[0] < left // 2, s[1]), pair, (jnp.int32(0), _sb_live(carry_ref[...])))
        acc = acc_ref[...]
        dq_ref[...] = (jnp.where(m0, acc[:tq], acc[tq:]) * scale).astype(BF16)

    W = heads * HEAD_DIM
    qspec = pl.BlockSpec((tq, LANES), lambda hp, i: (i, hp))
    seq = lambda off: pl.BlockSpec((S, LANES), lambda hp, i: (0, off + hp))
    square = pl.BlockSpec((tq, tq), lambda hp, i: (0, 0))
    ops = [p, kv, kv, o32, dy, *_sb_masks(tq)] + ([dk_in, dv_in] if has_in else [])
    return pl.pallas_call(
        body, name=name,
        out_shape=(jax.ShapeDtypeStruct((S, W), BF16), jax.ShapeDtypeStruct((S, W), F32), jax.ShapeDtypeStruct((S, W), F32)),
        grid=(npair, S // tq),
        in_specs=[qspec, seq(0), seq(npair), qspec, qspec, square, square, pl.BlockSpec((2 * tq, tq), lambda hp, i: (0, 0))]
        + ([seq(0), seq(0)] if has_in else []),
        out_specs=(qspec, seq(0), seq(0)),
        scratch_shapes=[pltpu.VMEM((2 * tq, 1), F32), pltpu.VMEM((2 * tq, 1), F32), pltpu.VMEM((2 * tq, LANES), F32)],
        compiler_params=_cparams(("parallel", "arbitrary")),
    )(*ops)


def _ffn_up(h, wg, wu, *, name, ride=None):
    S, D = h.shape
    F = wg.shape[0]
    tm = _pick(S, (512, 256))
    tn = _pick(F, (1408, 1024, 512, 256, 128))

    def body(h_ref, g_ref, u_ref, act_ref, silu_ref, uds_ref):
        hv = h_ref[...]
        g = _dot(hv, g_ref[...], _NT)
        u = _dot(hv, u_ref[...], _NT)
        s = jax.nn.sigmoid(g)
        silu = g * s
        act_ref[...] = (silu * u).astype(BF16)
        silu_ref[...] = silu.astype(BF16)
        uds_ref[...] = (u * (s + silu * (1.0 - s))).astype(BF16)

    wspec = pl.BlockSpec((tn, D), lambda j, i: (j, 0))
    ospec = pl.BlockSpec((tm, tn), lambda j, i: (i, j))
    out = jax.ShapeDtypeStruct((S, F), BF16)
    grid = (F // tn, S // tm)

    def edges():
        j, i = pl.program_id(0), pl.program_id(1)
        return jnp.logical_and(j == 0, i == 0), jnp.logical_and(j == grid[0] - 1, i == grid[1] - 1)

    return _ride_call(
        body, ride, edges, name=name, out_shape=(out, out, out), grid=grid,
        in_specs=[pl.BlockSpec((tm, D), lambda j, i: (i, 0)), wspec, wspec], out_specs=(ospec, ospec, ospec),
        scratch_shapes=[], compiler_params=_cparams(("parallel", "parallel")), operands=[h, wg, wu])


def _ffn_down_bwd(dx, wd, silu, uds, *, name):
    S, D = dx.shape
    F = wd.shape[0]
    tm = _pick(S, (512, 256))
    tn = _pick(F, (1408, 1024, 512, 256, 128))

    def body(dx_ref, w_ref, silu_ref, uds_ref, dg_ref, du_ref):
        da = _dot(dx_ref[...], w_ref[...], _NT)
        dg_ref[...] = (da * uds_ref[...].astype(F32)).astype(BF16)
        du_ref[...] = (da * silu_ref[...].astype(F32)).astype(BF16)

    ospec = pl.BlockSpec((tm, tn), lambda j, i: (i, j))
    out = jax.ShapeDtypeStruct((S, F), BF16)
    return pl.pallas_call(
        body, name=name, out_shape=(out, out), grid=(F // tn, S // tm),
        in_specs=[pl.BlockSpec((tm, D), lambda j, i: (i, 0)), pl.BlockSpec((tn, D), lambda j, i: (j, 0)), ospec, ospec],
        out_specs=(ospec, ospec), compiler_params=_cparams(("parallel", "parallel")),
    )(dx, wd, silu, uds)


def _adamw(w, g, m, v, *, name):
    R, C = w.shape
    tr = R
    for cand in (1024, 512, 256, 128, 64, 32, 16, 8):
        if R % cand == 0 and cand * C * 4 <= (1 << 20):
            tr = cand
            break
    bc1 = 1.0 - ADAM_B1 ** ADAM_STEP
    bc2 = 1.0 - ADAM_B2 ** ADAM_STEP

    def body(w_ref, g_ref, m_ref, v_ref, d_ref, nm_ref, nv_ref):
        gv = g_ref[...]
        nm = ADAM_B1 * m_ref[...] + (1.0 - ADAM_B1) * gv
        nv = ADAM_B2 * v_ref[...] + (1.0 - ADAM_B2) * (gv * gv)
        nm_ref[...] = nm
        nv_ref[...] = nv
        d_ref[...] = -ADAM_LR * ((nm / bc1) / (jnp.sqrt(nv / bc2) + ADAM_EPS) + ADAM_WD * w_ref[...])

    blk = pl.BlockSpec((tr, C), lambda i: (i, 0))
    out = jax.ShapeDtypeStruct((R, C), F32)
    return pl.pallas_call(body, name=name, out_shape=(out, out, out), grid=(R // tr,), in_specs=[blk] * 4,
                          out_specs=(blk, blk, blk), compiler_params=_cparams(("parallel",)))(w, g, m, v)


def _place():
    x, y, c = lax.axis_index("x"), lax.axis_index("y"), lax.axis_index("c")
    return x, y, c


def _all_gather_weights(shards, *, name):
    n = len(shards)

    def body(*refs):
        sh, full = refs[:n], refs[n:2 * n]
        send_sems, recv_sems, local_sems = refs[2 * n:]
        x, y, c = _place()
        me, sibling = (x, y, c), (x, y, 1 - c)
        chips = [(1 - x, y), (x, 1 - y), (1 - x, 1 - y)]

        def rows(t, px, py, pc):
            r = sh[t].shape[1]
            return full[t].at[:, pl.ds(pl.multiple_of((4 * px + 2 * py + pc) * r, BF16_ROWS), r), :]

        def copy(t, k, block, to, src=None):
            return pltpu.make_async_remote_copy(
                src_ref=rows(t, *block) if src is None else src, dst_ref=rows(t, *block),
                send_sem=send_sems.at[7 * t + k], recv_sem=recv_sems.at[7 * t + k], device_id=to, device_id_type=MESH)

        started = []
        for t in range(n):
            mine = pltpu.make_async_copy(sh[t], rows(t, *me), local_sems.at[t])
            mine.start()
            started.append(mine)
        sends = []
        for t in range(n):
            first = [copy(t, 0, me, sibling, src=sh[t])]
            first += [copy(t, 1 + j, me, (*chip, c), src=sh[t]) for j, chip in enumerate(chips)]
            for cp in first:
                cp.start()
            sends += first
        for t in range(n):
            for j, chip in enumerate(chips):
                copy(t, 1 + j, (*chip, c), me).wait_recv()
                fwd = copy(t, 4 + j, (*chip, c), sibling)
                fwd.start()
                sends.append(fwd)
        for t in range(n):
            copy(t, 0, sibling, me).wait_recv()
            for j, chip in enumerate(chips):
                copy(t, 4 + j, (*chip, 1 - c), me).wait_recv()
        for cp in sends:
            cp.wait_send()
        for cp in started:
            cp.wait()

    out_shape = [jax.ShapeDtypeStruct((s.shape[0], N_DEV * s.shape[1], s.shape[2]), s.dtype) for s in shards]
    return pl.pallas_call(
        body, name=name, out_shape=out_shape, in_specs=[ANY] * n, out_specs=[ANY] * n,
        scratch_shapes=[pltpu.SemaphoreType.DMA((7 * n,)), pltpu.SemaphoreType.DMA((7 * n,)), pltpu.SemaphoreType.DMA((n,))],
    )(*shards)


def _whole(ref_a, ref_b, send_sem, recv_sem, me):
    return pltpu.make_async_remote_copy(src_ref=ref_a, dst_ref=ref_b, send_sem=send_sem, recv_sem=recv_sem,
                                        device_id=me, device_id_type=MESH)


def _sibling_ride(grads):
    n = len(grads)
    lands = [lax.empty((4, s.shape[0], s.shape[3], s.shape[4]), s.dtype) for s in grads]

    def start(g, land, sems):
        x, y, c = _place()
        for t in range(n):
            for k in range(4):
                pltpu.make_async_remote_copy(
                    src_ref=g[t].at[:, k, 1 - c], dst_ref=land[t].at[k], send_sem=sems[0].at[t], recv_sem=sems[1].at[t],
                    device_id=(x, y, 1 - c), device_id_type=MESH).start()

    def wait(g, land, sems):
        x, y, c = _place()
        for t in range(n):
            w = _whole(land[t], land[t], sems[0].at[t], sems[1].at[t], (x, y, c))
            w.wait_send()
            w.wait_recv()

    return _Ride(grads, lands, start, wait, [(n,), (n,)])


def _carry_alone(ride, *, name):
    def body(o_ref):
        o_ref[...] = jnp.zeros_like(o_ref)

    one = lambda: (pl.program_id(0) == 0, pl.program_id(0) == 0)
    _, landed = _ride_call(body, ride, one, name=name, out_shape=[jax.ShapeDtypeStruct((8, LANES), F32)], grid=(1,),
                           in_specs=[], out_specs=[pl.BlockSpec((8, LANES), lambda i: (0, 0))], scratch_shapes=[],
                           compiler_params=_cparams(("arbitrary",)), operands=[])
    return landed


def _chips_ride(sums):
    n = len(sums)
    lands = [lax.empty((3,) + s.shape[1:], s.dtype) for s in sums]

    def start(s, land, sems):
        x, y, c = _place()
        for t in range(n):
            for j, (px, py) in enumerate([(1 - x, y), (x, 1 - y), (1 - x, 1 - y)]):
                pltpu.make_async_remote_copy(
                    src_ref=s[t].at[2 * px + py], dst_ref=land[t].at[j], send_sem=sems[0].at[t], recv_sem=sems[1].at[t],
                    device_id=(px, py, c), device_id_type=MESH).start()

    def wait(s, land, sems):
        x, y, c = _place()
        for t in range(n):
            w = _whole(land[t], land[t], sems[0].at[t], sems[1].at[t], (x, y, c))
            w.wait_send()
            w.wait_recv()

    return _Ride(sums, lands, start, wait, [(n,), (n,)])


def _gather_ride_1(shards):
    n = len(shards)
    layer = [l for _, l in shards]
    fulls = [lax.empty((N_DEV * s.shape[1], s.shape[2]), s.dtype) for s, _ in shards]

    def rows(full, r, px, py, pc):
        return full.at[pl.ds(pl.multiple_of((4 * px + 2 * py + pc) * r, BF16_ROWS), r), :]

    def start(sh, full, sems):
        x, y, c = _place()
        for t in range(n):
            shard = sh[t].at[layer[t]]
            mine = rows(full[t], shard.shape[0], x, y, c)
            pltpu.make_async_copy(shard, mine, sems[2].at[t]).start()
            for peer in [(x, y, 1 - c), (1 - x, y, c), (x, 1 - y, c), (1 - x, 1 - y, c)]:
                pltpu.make_async_remote_copy(src_ref=shard, dst_ref=mine, send_sem=sems[0].at[t], recv_sem=sems[1].at[t],
                                             device_id=peer, device_id_type=MESH).start()

    def wait(sh, full, sems):
        x, y, c = _place()
        for t in range(n):
            shard = sh[t].at[layer[t]]
            r = shard.shape[0]
            pltpu.make_async_copy(shard, rows(full[t], r, x, y, c), sems[2].at[t]).wait()
            four = full[t].at[pl.ds(0, 4 * r), :]
            w = _whole(four, four, sems[0].at[t], sems[1].at[t], (x, y, c))
            w.wait_send()
            w.wait_recv()

    return _Ride([s for s, _ in shards], fulls, start, wait, [(n,), (n,), (n,)])


def _gather_ride_2(fulls):
    n = len(fulls)

    def start(_, full, sems):
        x, y, c = _place()
        for t in range(n):
            r = full[t].shape[0] // N_DEV
            for px, py in [(1 - x, y), (x, 1 - y), (1 - x, 1 - y)]:
                block = full[t].at[pl.ds(pl.multiple_of((4 * px + 2 * py + c) * r, BF16_ROWS), r), :]
                pltpu.make_async_remote_copy(src_ref=block, dst_ref=block, send_sem=sems[0].at[t], recv_sem=sems[1].at[t],
                                             device_id=(x, y, 1 - c), device_id_type=MESH).start()

    def wait(_, full, sems):
        x, y, c = _place()
        for t in range(n):
            three = full[t].at[pl.ds(0, 3 * (full[t].shape[0] // N_DEV)), :]
            w = _whole(three, three, sems[0].at[t], sems[1].at[t], (x, y, c))
            w.wait_send()
            w.wait_recv()

    return _Ride([], fulls, start, wait, [(n,), (n,)])


def _join_rides(rides):
    rides = [r for r in rides if r is not None]
    if len(rides) <= 1:
        return rides[0] if rides else None

    def parts(src, dst, sems):
        so = do = mo = 0
        for r in rides:
            yield r, src[so:so + len(r.srcs)], dst[do:do + len(r.dsts)], sems[mo:mo + len(r.sems)]
            so, do, mo = so + len(r.srcs), do + len(r.dsts), mo + len(r.sems)

    def start(src, dst, sems):
        for r, s, d, m in parts(src, dst, sems):
            r.start(s, d, m)

    def wait(src, dst, sems):
        for r, s, d, m in parts(src, dst, sems):
            r.wait(s, d, m)

    return _Ride([a for r in rides for a in r.srcs], [a for r in rides for a in r.dsts], start, wait,
                 [m for r in rides for m in r.sems])


def _chip_sum(g, land, core, *, name):
    L, _, _, r, C = g.shape

    def body(core_ref, g_ref, l_ref, o_ref):
        o_ref[...] = (g_ref[...].astype(F32) + l_ref[...].astype(F32)).astype(BF16)

    grid_spec = pltpu.PrefetchScalarGridSpec(
        num_scalar_prefetch=1, grid=(4, L),
        in_specs=[pl.BlockSpec((None, None, None, r, C), lambda k, l, core_ref: (l, k, core_ref[0], 0, 0)),
                  pl.BlockSpec((None, None, r, C), lambda k, l, core_ref: (k, l, 0, 0))],
        out_specs=pl.BlockSpec((None, None, r, C), lambda k, l, core_ref: (k, l, 0, 0)))
    return pl.pallas_call(body, name=name, out_shape=jax.ShapeDtypeStruct((4, L, r, C), BF16), grid_spec=grid_spec,
                          compiler_params=_cparams(("parallel", "parallel")))(core, g, land)


def _final_sum(sums, land, chip, into, layer, *, name):
    _, _, r, C = sums.shape

    def body(chip_ref, s_ref, a_ref, b_ref, c_ref, _, o_ref):
        o_ref[...] = ((s_ref[...].astype(F32) + a_ref[...].astype(F32)) + b_ref[...].astype(F32)) + c_ref[...].astype(F32)

    slot = lambda j: pl.BlockSpec((None, None, r, C), lambda i, chip_ref: (j, 0, 0, 0))
    grid_spec = pltpu.PrefetchScalarGridSpec(
        num_scalar_prefetch=1, grid=(1,),
        in_specs=[pl.BlockSpec((None, None, r, C), lambda i, chip_ref: (chip_ref[0], 0, 0, 0)), slot(0), slot(1), slot(2), ANY],
        out_specs=pl.BlockSpec((None, r, C), lambda i, chip_ref: (layer, 0, 0)))
    return pl.pallas_call(body, name=name, out_shape=jax.ShapeDtypeStruct(into.shape, F32), grid_spec=grid_spec,
                          input_output_aliases={5: 0}, compiler_params=_cparams(("arbitrary",)))(chip, sums, land, land, land, into)


def _exchange(v, reduce, *, name):
    R, C = v.shape

    def body(v_ref, o_ref, *scratch):
        if reduce:
            buf, send_sems, recv_sems = scratch
        else:
            buf = o_ref
            send_sems, recv_sems = scratch
        x, y, c = _place()
        me = 4 * x + 2 * y + c
        buf[me] = v_ref[...]
        copies = []
        for k in range(1, N_DEV):
            kx, ky, kc = (k >> 2) & 1, (k >> 1) & 1, k & 1
            peer = (1 - x if kx else x, 1 - y if ky else y, 1 - c if kc else c)
            cp = pltpu.make_async_remote_copy(src_ref=v_ref, dst_ref=buf.at[me], send_sem=send_sems.at[k - 1],
                                              recv_sem=recv_sems.at[k - 1], device_id=peer, device_id_type=MESH)
            cp.start()
            copies.append(cp)
        for cp in copies:
            cp.wait_recv()
        for cp in copies:
            cp.wait_send()
        if reduce:
            acc = buf[0]
            for d in range(1, N_DEV):
                acc = acc + buf[d]
            o_ref[...] = acc

    sems = [pltpu.SemaphoreType.DMA((N_DEV - 1,)), pltpu.SemaphoreType.DMA((N_DEV - 1,))]
    vm = pl.BlockSpec(memory_space=pltpu.VMEM)
    if reduce:
        return pl.pallas_call(body, name=name, out_shape=jax.ShapeDtypeStruct((R, C), F32), in_specs=[vm], out_specs=vm,
                              scratch_shapes=[pltpu.VMEM((N_DEV, R, C), F32)] + sems)(v)
    return pl.pallas_call(body, name=name, out_shape=jax.ShapeDtypeStruct((N_DEV, R, C), F32), in_specs=[vm], out_specs=vm,
                          scratch_shapes=sems)(v)


def _local_step(x, mem, target, norms, conv_w, depth, n_a, get_w, next_ride, ride_done, grad_ready, grad_ride, grad_landed):
    S, D = x.shape
    main = D - MEM_WIDTH
    heads = main // HEAD_DIM
    row = lambda v: v.reshape(1, D)

    mem_n = _rmsnorm(mem, row(norms["mem_norm"]), name="mem_norm")
    saved = []
    kv = hk = x_kv = w_kv = None
    def carry_mm(*args, **kwargs):
        ride = next_ride()
        if ride is None:
            return _mm(*args, **kwargs)
        out, landed = _mm(*args, ride=ride, **kwargs)
        ride_done(landed)
        return out

    for i in range(depth):
        W = functools.partial(get_w, i)
        st = {"x": x}
        h = _rmsnorm(x, row(norms["mix_norm"][i]), name=f"mix_norm{i}")
        mkv = _mm(mem_n, W("mkv"), "nn", BF16, name=f"mkv{i}")
        if i < n_a:
            p = carry_mm(h, W("a"), "nt", BF16, name=f"a_in{i}")
            y_main = _conv_fwd(p, conv_w[i], main, name=f"conv{i}")
            qblk = 3 * main // MEM_WIDTH
        else:
            p = carry_mm(h, W("b"), "nn", BF16, name=f"b_in{i}")
            y_main, st["o32"] = _sb_fwd(p, kv, heads, name=f"sb{i}")
            qblk = main // MEM_WIDTH
        y = _memattn_fwd(p, qblk, mkv, y_main, name=f"memattn{i}")
        xm = carry_mm(y, W("o"), "nn", F32, residual=x, name=f"w_o{i}")
        h2 = _rmsnorm(xm, row(norms["ffn_norm"][i]), name=f"ffn_norm{i}")
        (act, silu, uds), landed = _ffn_up(h2, W("g"), W("u"), name=f"ffn_up{i}", ride=next_ride())
        ride_done(landed)
        x = carry_mm(act, W("d"), "nn", F32, residual=xm, name=f"w_down{i}")
        st.update(h=h, mkv=mkv, p=p, qblk=qblk, y=y, xm=xm, h2=h2, silu=silu, uds=uds, act=act)
        saved.append(st)
        if i == n_a - 1:
            x_kv, w_kv = x, W("kv")
            hk = _rmsnorm(x, row(norms["kv_norm"]), name="kv_norm")
            kv = _mm(hk, w_kv, "nt", BF16, name="w_kv")

    dx, dxb, dg_final, loss = _loss_head(x, row(norms["final_norm"]), target, name="loss_head")

    dg_mix, dg_ffn, dconv = [None] * depth, [None] * depth, [None] * n_a
    dmem_n = dk = dv = dg_kv = g_kv = None
    def carry_back(*args, **kwargs):
        ride = grad_ride()
        if ride is None:
            return _mm(*args, **kwargs)
        out, landed = _mm(*args, ride=ride, **kwargs)
        grad_landed(landed)
        return out

    def carry_back_norm(*args, **kwargs):
        outs, landed = _mm_norm_bwd(*args, ride=grad_ride(), **kwargs)
        grad_landed(landed)
        return outs

    for i in reversed(range(depth)):
        st = saved[i]
        W, key = functools.partial(get_w, i), dict(_layer_keys(i, n_a))
        dgate, dup = _ffn_down_bwd(dxb, W("d"), st["silu"], st["uds"], name=f"ffn_down_bwd{i}")
        grad_ready(("d", key["d"]), carry_back(st["act"], dxb, "tn", BF16, name=f"g_w_down{i}"))
        grad_ready(("g", key["g"]), carry_back(dgate, st["h2"], "tn", BF16, name=f"g_w_gate{i}"))
        grad_ready(("u", key["u"]), carry_back(dup, st["h2"], "tn", BF16, name=f"g_w_up{i}"))
        dx, dxb, dg_ffn[i] = carry_back_norm([(dgate, W("g")), (dup, W("u"))], "nn", st["xm"], row(norms["ffn_norm"][i]), dx,
                                             name=f"d_h2_{i}")
        dy = _mm(dxb, W("o"), "nt", BF16, name=f"d_y{i}")
        grad_ready(("o", key["o"]), _mm(st["y"], dxb, "tn", BF16, name=f"g_w_o{i}"))
        dqmem, dmkv = _memattn_bwd(st["p"], st["qblk"], st["mkv"], dy, main // MEM_WIDTH, name=f"memattn_bwd{i}")
        grad_ready(("mkv", key["mkv"]), _mm(mem_n, dmkv, "tn", BF16, name=f"g_w_mem_kv{i}"))
        dmem_n = _mm(dmkv, W("mkv"), "nt", F32, residual=dmem_n, name=f"d_mem_n{i}")
        if i < n_a:
            db, dc, du, dconv[i] = _conv_bwd(st["p"], conv_w[i], dy, main, name=f"conv_bwd{i}")
            dp = jnp.concatenate([db, dc, du, dqmem], axis=1)
            grad_ready(("a", key["a"]), carry_back(dp, st["h"], "tn", BF16, name=f"g_a_in{i}"))
            w_in, form = W("a"), "nn"
        else:
            dq, dk, dv = _sb_bwd(st["p"], kv, st["o32"], dy, heads, dk, dv, name=f"sb_bwd{i}")
            dp = jnp.concatenate([dq, dqmem], axis=1)
            grad_ready(("b", key["b"]), carry_back(st["h"], dp, "tn", BF16, name=f"g_b_in{i}"))
            w_in, form = W("b"), "nt"
        dx, dxb, dg_mix[i] = carry_back_norm([(dp, w_in)], form, st["x"], row(norms["mix_norm"][i]), dx, name=f"d_h{i}")
        if i == n_a:
            dkv = jnp.concatenate([dk, dv], axis=1).astype(BF16)
            grad_ready(("kv", 0), carry_back(dkv, hk, "tn", BF16, name="g_w_kv"))
            dx, dxb, dg_kv = carry_back_norm([(dkv, w_kv)], "nn", x_kv, row(norms["kv_norm"]), dx, name="d_hk")
    _, _, dg_mem = _rmsnorm_bwd(mem, row(norms["mem_norm"]), dmem_n, None, name="mem_norm_bwd")

    small = {"mix_norm": jnp.concatenate(dg_mix, axis=0), "ffn_norm": jnp.concatenate(dg_ffn, axis=0), "kv_norm": dg_kv[0],
             "mem_norm": dg_mem[0], "final_norm": dg_final[0], "conv_w": jnp.stack(dconv, axis=0)}
    return loss, dx, small


_COL_SHARDED = ("a", "kv", "g", "u")
_NAMES = {"a": "a_in", "kv": "w_kv_shared", "g": "w_gate", "u": "w_up", "b": "b_in", "o": "w_o", "d": "w_down", "mkv": "w_mem_kv"}
_ORDER = ("a", "kv", "g", "u", "d", "b", "o", "mkv")
_WEIGHTS = ("mix_norm", "a_in", "conv_w", "b_in", "kv_norm", "w_kv_shared", "w_mem_kv", "w_o", "ffn_norm", "w_gate", "w_up",
            "w_down", "mem_norm", "final_norm")


def _layer_keys(i, n_a):
    keys = [("a", i) if i < n_a else ("b", i - n_a), ("g", i), ("u", i), ("d", i), ("o", i), ("mkv", i)]
    return keys + [("kv", 0)] if i == n_a - 1 else keys


def _gather_groups(i, n_a):
    first, rest = _layer_keys(i, n_a)[0], dict(_layer_keys(i, n_a)[1:])
    small = [(k, rest[k]) for k in ("o", "mkv", "kv") if k in rest]
    return [[first], small, [("g", rest["g"]), ("u", rest["u"])], [("d", rest["d"])]]


def _canonical(key, w):
    w3 = w if w.ndim == 3 else w[None]
    if key in _COL_SHARDED:
        w3 = jnp.transpose(w3, (0, 2, 1))
    return w3


def _uncanonical(key, g3, like):
    if key in _COL_SHARDED:
        g3 = jnp.transpose(g3, (0, 2, 1))
    return g3.reshape(like.shape)


def _pad_rows(flat, C):
    n = flat.shape[0]
    rows = -(-n // C)
    return jnp.pad(flat, (0, rows * C - n)).reshape(rows, C)


def kernel(x, mem, mix_norm, a_in, conv_w, b_in, kv_norm, w_kv_shared, w_mem_kv, w_o, ffn_norm, w_gate, w_up, w_down, mem_norm, final_norm, loss_target, m_mix_norm, m_a_in, m_conv_w, m_b_in, m_kv_norm, m_w_kv_shared, m_w_mem_kv, m_w_o, m_ffn_norm, m_w_gate, m_w_up, m_w_down, m_mem_norm, m_final_norm, v_mix_norm, v_a_in, v_conv_w, v_b_in, v_kv_norm, v_w_kv_shared, v_w_mem_kv, v_w_o, v_ffn_norm, v_w_gate, v_w_up, v_w_down, v_mem_norm, v_final_norm):
    weights = dict(mix_norm=mix_norm, a_in=a_in, conv_w=conv_w, b_in=b_in, kv_norm=kv_norm, w_kv_shared=w_kv_shared,
                   w_mem_kv=w_mem_kv, w_o=w_o, ffn_norm=ffn_norm, w_gate=w_gate, w_up=w_up, w_down=w_down,
                   mem_norm=mem_norm, final_norm=final_norm)
    moments_m = dict(mix_norm=m_mix_norm, a_in=m_a_in, conv_w=m_conv_w, b_in=m_b_in, kv_norm=m_kv_norm,
                     w_kv_shared=m_w_kv_shared, w_mem_kv=m_w_mem_kv, w_o=m_w_o, ffn_norm=m_ffn_norm, w_gate=m_w_gate,
                     w_up=m_w_up, w_down=m_w_down, mem_norm=m_mem_norm, final_norm=m_final_norm)
    moments_v = dict(mix_norm=v_mix_norm, a_in=v_a_in, conv_w=v_conv_w, b_in=v_b_in, kv_norm=v_kv_norm,
                     w_kv_shared=v_w_kv_shared, w_mem_kv=v_w_mem_kv, w_o=v_w_o, ffn_norm=v_ffn_norm, w_gate=v_w_gate,
                     w_up=v_w_up, w_down=v_w_down, mem_norm=v_mem_norm, final_norm=v_final_norm)
    D = x.shape[-1]
    depth, n_a = w_o.shape[0], a_in.shape[0]
    xi, yi, ci = _place()
    me = 4 * xi + 2 * yi + ci
    core = ci.reshape(1).astype(jnp.int32)
    chip = (2 * xi + yi).reshape(1).astype(jnp.int32)

    cw_shape = conv_w.shape
    cw_rows = _pad_rows(conv_w.reshape(-1), D)
    cw_rows = jnp.pad(cw_rows, ((0, 8 - cw_rows.shape[0]), (0, 0)))
    cw_gathered = _exchange(cw_rows, False, name="gather_conv_w")
    n_cw = cw_shape[0] * cw_shape[1] * cw_shape[2]
    cw_all = cw_gathered.reshape(N_DEV, -1)[:, :n_cw].reshape((N_DEV,) + cw_shape)
    conv_full = jnp.transpose(cw_all, (1, 2, 0, 3)).reshape(cw_shape[0], cw_shape[1], N_DEV * cw_shape[2])

    shard3 = {k: _canonical(k, weights[_NAMES[k]]).astype(BF16) for k in _ORDER}
    keys0 = _layer_keys(0, n_a)
    fulls0 = _all_gather_weights([shard3[k][l][None] for k, l in keys0], name="all_gather_layer0")
    full = {kl: f[0] for kl, f in zip(keys0, fulls0)}

    groups = [grp for i in range(1, depth) for grp in _gather_groups(i, n_a)]
    carried, riding = [0], []

    def next_ride():
        n = carried[0]
        carried[0] += 1
        second = groups[n - 1] if 1 <= n <= len(groups) else []
        first = groups[n] if n < len(groups) else []
        if not second + first:
            return None
        riding.append(second + first)
        return _join_rides([_gather_ride_2([full[kl] for kl in second]) if second else None,
                            _gather_ride_1([(shard3[k], l) for k, l in first]) if first else None])

    def ride_done(landed):
        if landed:
            full.update(zip(riding.pop(), landed))

    def get_w(i, key):
        return full[(key, dict(_layer_keys(i, n_a))[key])]

    fresh, summed, reduced, travelling = [], [], {}, []

    def grad_ready(kl, g):
        fresh.append((kl, g.reshape(1, 4, 2, g.shape[0] // N_DEV, g.shape[1])))

    def grad_ride():
        if not fresh + summed:
            return None
        travelling.append((list(summed), list(fresh)))
        ride = _join_rides([_chips_ride([s for _, s in summed]) if summed else None,
                            _sibling_ride([g for _, g in fresh]) if fresh else None])
        summed.clear()
        fresh.clear()
        return ride

    def grad_landed(landed):
        if not landed:
            return
        between_chips, to_sibling = travelling.pop()
        for (kl, s), land in zip(between_chips, landed):
            reduced[kl] = (s, land)
        for (kl, g), land in zip(to_sibling, landed[len(between_chips):]):
            summed.append((kl, _chip_sum(g, land, core, name=f"chip_sum_{kl[0]}{kl[1]}")))

    norms = {k: weights[k] for k in ("mix_norm", "ffn_norm", "kv_norm", "mem_norm", "final_norm")}
    loss, grad_x, small = _local_step(x[0], mem[0], loss_target[0], norms, conv_full, depth, n_a, get_w, next_ride, ride_done,
                                      grad_ready, grad_ride, grad_landed)
    for tail in range(2):
        ride = grad_ride()
        if ride is not None:
            grad_landed(_carry_alone(ride, name=f"reduce_scatter_tail{tail}"))

    stacks = {k: lax.empty(shard3[k].shape, F32) for k in _ORDER}
    for (k, l), (s, land) in reduced.items():
        stacks[k] = _final_sum(s, land, chip, stacks[k], l, name=f"final_sum_{k}{l}")
    grads = {_NAMES[k]: _uncanonical(k, stacks[k], weights[_NAMES[k]]) for k in _ORDER}

    order = ("mix_norm", "ffn_norm", "kv_norm", "mem_norm", "final_norm", "conv_w")
    flat = jnp.concatenate([small[k].reshape(-1) for k in order] + [loss[0, :1]])
    n_flat = flat.shape[0]
    rows = _pad_rows(flat, D)
    rows = jnp.pad(rows, ((0, (-rows.shape[0]) % 8), (0, 0)))
    total = _exchange(rows, True, name="all_reduce_small").reshape(-1)[:n_flat]
    off = 0
    for k in order:
        n = small[k].size
        grads[k] = total[off:off + n].reshape(small[k].shape)
        off += n
    loss_total = total[off]
    grads["conv_w"] = lax.dynamic_slice_in_dim(grads["conv_w"], me * cw_shape[2], cw_shape[2], axis=2)

    deltas, new_m, new_v = {}, {}, {}
    for k in _WEIGHTS:
        w = weights[k]
        two = (lambda a: a.reshape(-1, a.shape[-1])) if w.ndim > 1 else (lambda a: a.reshape(1, -1))
        d, nm, nv = _adamw(two(w), two(grads[k]), two(moments_m[k]), two(moments_v[k]), name=f"adamw_{k}")
        deltas[k], new_m[k], new_v[k] = d.reshape(w.shape), nm.reshape(w.shape), nv.reshape(w.shape)

    return (loss_total, grad_x[None], *[grads[k] for k in _WEIGHTS], *[deltas[k] for k in _WEIGHTS],
            *[new_m[k] for k in _WEIGHTS], *[new_v[k] for k in _WEIGHTS])
```

```python
import functools
import math

import jax
import jax.numpy as jnp
from jax import lax
from jax.experimental import pallas as pl
from jax.experimental.pallas import tpu as pltpu

F32 = jnp.float32
BF16 = jnp.bfloat16
MESH = pl.DeviceIdType.MESH

HEAD_DIM = 64
MEM_HEADS = 4
MEM_WIDTH = MEM_HEADS * HEAD_DIM
EPS = 1e-6
LANES = 128
BF16_ROWS = 16
VMEM_LIMIT = 56 * 1024 * 1024
N_DEV = 8

ADAM_LR = 0.001
ADAM_B1 = 0.9
ADAM_B2 = 0.999
ADAM_EPS = 1e-08
ADAM_WD = 0.01
ADAM_STEP = 10

ANY = pl.BlockSpec(memory_space=pl.ANY)


def _cparams(sem=None):
    return pltpu.CompilerParams(dimension_semantics=sem, vmem_limit_bytes=VMEM_LIMIT)


def _pick(n, cands):
    for c in cands:
        if n % c == 0:
            return c
    raise ValueError(f"no tile for {n} in {cands}")


def _mm(a, b, form, out_dtype, *, name, residual=None, ride=None):
    if form == "tn":
        K, M = a.shape
    else:
        M, K = a.shape
    if form == "nt":
        N, K2 = b.shape
    else:
        K2, N = b.shape
    assert K == K2, (name, a.shape, b.shape)
    wide = (1408, 1280, 1024, 768, 512, 256, 128)
    tm = _pick(M, wide if form == "tn" else (1024, 512, 256, 128))
    tn = _pick(N, wide)
    tk = _pick(K, (1024, 1408, 1280, 768, 512, 256))
    nk = K // tk
    dims = {"nn": (((1,), (0,)), ((), ())), "nt": (((1,), (1,)), ((), ())), "tn": (((0,), (0,)), ((), ()))}[form]
    a_bytes, b_bytes = M * K * a.dtype.itemsize, N * K * b.dtype.itemsize
    n_outer = nk == 1 and (N // tn) * a_bytes + b_bytes < a_bytes + (M // tm) * b_bytes
    ij = (lambda g0, g1: (g1, g0)) if n_outer else (lambda g0, g1: (g0, g1))

    def spec(block, f):
        return pl.BlockSpec(block, lambda g0, g1, k: f(*ij(g0, g1), k))

    a_spec = spec((tk, tm), lambda i, j, k: (k, i)) if form == "tn" else spec((tm, tk), lambda i, j, k: (i, k))
    b_spec = spec((tn, tk), lambda i, j, k: (j, k)) if form == "nt" else spec((tk, tn), lambda i, j, k: (k, j))
    out_spec = spec((tm, tn), lambda i, j, k: (i, j))
    operands, in_specs = [a, b], [a_spec, b_spec]
    has_res = residual is not None
    if has_res:
        operands.append(residual)
        in_specs.append(out_spec)
    grid = (N // tn, M // tm, nk) if n_outer else (M // tm, N // tn, nk)

    def body(*refs):
        a_ref, b_ref = refs[0], refs[1]
        r_ref = refs[2] if has_res else None
        o_ref = refs[2 + int(has_res)]
        acc_ref = refs[-1]
        part = lax.dot_general(a_ref[...].astype(BF16), b_ref[...].astype(BF16), dims, preferred_element_type=F32)

        def finish(total):
            if has_res:
                total = total + r_ref[...].astype(F32)
            o_ref[...] = total.astype(out_dtype)

        if nk == 1:
            finish(part)
        else:
            k = pl.program_id(2)

            @pl.when(k == 0)
            def _():
                acc_ref[...] = part

            @pl.when(jnp.logical_and(k > 0, k < nk - 1))
            def _():
                acc_ref[...] += part

            @pl.when(k == nk - 1)
            def _():
                finish(acc_ref[...] + part)

    def edges():
        ids = [pl.program_id(d) for d in range(3)]
        first = jnp.logical_and(jnp.logical_and(ids[0] == 0, ids[1] == 0), ids[2] == 0)
        last = jnp.logical_and(jnp.logical_and(ids[0] == grid[0] - 1, ids[1] == grid[1] - 1), ids[2] == grid[2] - 1)
        return first, last

    (out,), landed = _ride_call(
        body, ride, edges, name=name, out_shape=[jax.ShapeDtypeStruct((M, N), out_dtype)], grid=grid, in_specs=in_specs,
        out_specs=[out_spec], scratch_shapes=[pltpu.VMEM((tm, tn), F32)],
        compiler_params=_cparams(("parallel", "parallel", "arbitrary")), operands=operands)
    return out if ride is None else (out, landed)


def _rmsnorm(x, g, *, name):
    R, D = x.shape
    tr = _pick(R, (512, 256))

    def body(x_ref, g_ref, o_ref):
        xv = x_ref[...]
        r = lax.rsqrt(jnp.mean(xv * xv, axis=-1, keepdims=True) + EPS)
        o_ref[...] = (xv * r * g_ref[...]).astype(BF16)

    return pl.pallas_call(
        body, name=name, out_shape=jax.ShapeDtypeStruct((R, D), BF16), grid=(R // tr,),
        in_specs=[pl.BlockSpec((tr, D), lambda i: (i, 0)), pl.BlockSpec((1, D), lambda i: (0, 0))],
        out_specs=pl.BlockSpec((tr, D), lambda i: (i, 0)), compiler_params=_cparams(("parallel",)),
    )(x, g)


def _rmsnorm_bwd(x, g, dh, dx_in, *, name):
    R, D = x.shape
    tr = _pick(R, (512, 256))
    has_in = dx_in is not None

    def body(*refs):
        x_ref, g_ref, dh_ref = refs[:3]
        dxi_ref = refs[3] if has_in else None
        dx_ref, dxb_ref, dg_ref = refs[3 + int(has_in):]
        xv = x_ref[...]
        r = lax.rsqrt(jnp.mean(xv * xv, axis=-1, keepdims=True) + EPS)
        xhat = xv * r
        dhv = dh_ref[...].astype(F32)
        dxh = dhv * g_ref[...]
        dx = r * (dxh - xhat * jnp.mean(dxh * xhat, axis=-1, keepdims=True))
        if has_in:
            dx = dx + dxi_ref[...]
        dx_ref[...] = dx
        dxb_ref[...] = dx.astype(BF16)
        part = jnp.sum(dhv * xhat, axis=0, keepdims=True)

        @pl.when(pl.program_id(0) == 0)
        def _():
            dg_ref[...] = part

        @pl.when(pl.program_id(0) > 0)
        def _():
            dg_ref[...] += part

    row = pl.BlockSpec((tr, D), lambda i: (i, 0))
    vec = pl.BlockSpec((1, D), lambda i: (0, 0))
    ops = [x, g, dh] + ([dx_in] if has_in else [])
    return pl.pallas_call(
        body, name=name,
        out_shape=(jax.ShapeDtypeStruct((R, D), F32), jax.ShapeDtypeStruct((R, D), BF16), jax.ShapeDtypeStruct((1, D), F32)),
        grid=(R // tr,), in_specs=[row, vec, row] + ([row] if has_in else []), out_specs=(row, row, vec),
        compiler_params=_cparams(("arbitrary",)),
    )(*ops)


class _Ride:
    def __init__(self, srcs, dsts, start, wait, sems):
        self.srcs, self.dsts, self.start, self.wait, self.sems = list(srcs), list(dsts), start, wait, list(sems)


def _ride_call(body, ride, edges, *, name, out_shape, grid, in_specs, out_specs, scratch_shapes, compiler_params, operands):
    n_in, n_out, n_scr = len(in_specs), len(out_specs), len(scratch_shapes)
    if ride is None:
        outs = pl.pallas_call(body, name=name, out_shape=tuple(out_shape), grid=grid, in_specs=list(in_specs),
                              out_specs=tuple(out_specs), scratch_shapes=list(scratch_shapes),
                              compiler_params=compiler_params)(*operands)
        return tuple(outs), []
    ns, nd = len(ride.srcs), len(ride.dsts)

    def riding(*refs):
        ins, rin = refs[:n_in], refs[n_in:n_in + ns + nd]
        outs = refs[n_in + ns + nd:n_in + ns + nd + n_out]
        scratch = refs[n_in + ns + 2 * nd + n_out:]
        sems = scratch[n_scr:]
        first, last = edges()

        @pl.when(first)
        def _():
            ride.start(rin[:ns], rin[ns:], sems)

        body(*ins, *outs, *scratch[:n_scr])

        @pl.when(last)
        def _():
            ride.wait(rin[:ns], rin[ns:], sems)

    outs = pl.pallas_call(
        riding, name=name, out_shape=(*out_shape, *[jax.ShapeDtypeStruct(d.shape, d.dtype) for d in ride.dsts]), grid=grid,
        in_specs=[*in_specs, *[ANY] * (ns + nd)], out_specs=(*out_specs, *[ANY] * nd),
        scratch_shapes=[*scratch_shapes, *[pltpu.SemaphoreType.DMA(s) for s in ride.sems]],
        input_output_aliases={n_in + ns + d: n_out + d for d in range(nd)}, compiler_params=compiler_params,
    )(*operands, *ride.srcs, *ride.dsts)
    return tuple(outs[:n_out]), list(outs[n_out:])


def _mm_norm_bwd(parts, form, x, g, dx_in, *, name, ride=None):
    S, K = parts[0][0].shape
    D = x.shape[1]
    tm = _pick(S, (512, 256))
    tk = _pick(K, (1024, 1408, 1280, 768, 512, 256))
    nk, P = K // tk, len(parts)
    dims = _NN if form == "nn" else _NT

    ni, nsteps = S // tm, P * nk

    def body(*refs):
        ab = refs[:2 * P]
        x_ref, g_ref, dxi_ref, dx_ref, dxb_ref, dg_ref, acc_ref = refs[2 * P:]
        k, i = pl.program_id(0), pl.program_id(1)
        rows = pl.ds(pl.multiple_of(i * tm, tm), tm)

        @pl.when(k == 0)
        def _():
            acc_ref[rows, :] = jnp.zeros((tm, D), F32)

        for p in range(P):
            @pl.when(jnp.logical_and(k >= p * nk, k < (p + 1) * nk))
            def _():
                acc_ref[rows, :] += _dot(ab[2 * p][...], ab[2 * p + 1][...], dims)

        @pl.when(k == nsteps - 1)
        def _():
            xv = x_ref[...]
            r = lax.rsqrt(jnp.mean(xv * xv, axis=-1, keepdims=True) + EPS)
            xhat = xv * r
            dhv = acc_ref[rows, :]
            dxh = dhv * g_ref[...]
            dx = r * (dxh - xhat * jnp.mean(dxh * xhat, axis=-1, keepdims=True)) + dxi_ref[...]
            dx_ref[...] = dx
            dxb_ref[...] = dx.astype(BF16)
            part = jnp.sum(dhv * xhat, axis=0, keepdims=True)

            @pl.when(i == 0)
            def _():
                dg_ref[...] = part

            @pl.when(i > 0)
            def _():
                dg_ref[...] += part

    def kk(p):
        return lambda k: jnp.clip(k - p * nk, 0, nk - 1)

    def active_row(p):
        return lambda k, i: jnp.where(jnp.logical_and(k >= p * nk, k < (p + 1) * nk), i, 0)

    in_specs, operands = [], []
    for p, (a, b) in enumerate(parts):
        in_specs.append(pl.BlockSpec((tm, tk), lambda k, i, f=kk(p), r=active_row(p): (r(k, i), f(k))))
        in_specs.append(pl.BlockSpec((tk, D), lambda k, i, f=kk(p): (f(k), 0)) if form == "nn"
                        else pl.BlockSpec((D, tk), lambda k, i, f=kk(p): (0, f(k))))
        operands += [a, b]
    row = pl.BlockSpec((tm, D), lambda k, i: (jnp.where(k == nsteps - 1, i, 0), 0))
    vec = pl.BlockSpec((1, D), lambda k, i: (0, 0))

    def edges():
        k, i = pl.program_id(0), pl.program_id(1)
        return jnp.logical_and(i == 0, k == 0), jnp.logical_and(i == ni - 1, k == nsteps - 1)

    return _ride_call(
        body, ride, edges, name=name,
        out_shape=(jax.ShapeDtypeStruct((S, D), F32), jax.ShapeDtypeStruct((S, D), BF16), jax.ShapeDtypeStruct((1, D), F32)),
        grid=(nsteps, ni), in_specs=in_specs + [row, vec, row], out_specs=(row, row, vec),
        scratch_shapes=[pltpu.VMEM((S, D), F32)], compiler_params=_cparams(("arbitrary", "arbitrary")),
        operands=[*operands, x, g, dx_in])


def _loss_head(x, g, target, *, name):
    R, D = x.shape
    tr = _pick(R, (512, 256))

    def body(x_ref, g_ref, t_ref, dx_ref, dxb_ref, dg_ref, loss_ref):
        xv = x_ref[...]
        gv = g_ref[...]
        r = lax.rsqrt(jnp.mean(xv * xv, axis=-1, keepdims=True) + EPS)
        xhat = xv * r
        err = xhat * gv - t_ref[...]
        loss = 0.5 * jnp.sum(jnp.mean(err * err, axis=-1, keepdims=True), axis=0, keepdims=True)
        dy = err * (1.0 / D)
        dxh = dy * gv
        dx = r * (dxh - xhat * jnp.mean(dxh * xhat, axis=-1, keepdims=True))
        dx_ref[...] = dx
        dxb_ref[...] = dx.astype(BF16)
        dg = jnp.sum(dy * xhat, axis=0, keepdims=True)
        lossv = jnp.broadcast_to(loss, (1, LANES))

        @pl.when(pl.program_id(0) == 0)
        def _():
            dg_ref[...] = dg
            loss_ref[...] = lossv

        @pl.when(pl.program_id(0) > 0)
        def _():
            dg_ref[...] += dg
            loss_ref[...] += lossv

    row = pl.BlockSpec((tr, D), lambda i: (i, 0))
    vec = pl.BlockSpec((1, D), lambda i: (0, 0))
    return pl.pallas_call(
        body, name=name,
        out_shape=(jax.ShapeDtypeStruct((R, D), F32), jax.ShapeDtypeStruct((R, D), BF16), jax.ShapeDtypeStruct((1, D), F32),
                   jax.ShapeDtypeStruct((1, LANES), F32)),
        grid=(R // tr,), in_specs=[row, vec, row], out_specs=(row, row, vec, pl.BlockSpec((1, LANES), lambda i: (0, 0))),
        compiler_params=_cparams(("arbitrary",)),
    )(x, g, target)


def _conv_taps(gv, S):
    t = lax.broadcasted_iota(jnp.int32, gv.shape, 0)
    g1 = jnp.where(t >= 1, pltpu.roll(gv, 1, 0), 0.0)
    g2 = jnp.where(t >= 2, pltpu.roll(gv, 2, 0), 0.0)
    return g1, g2


def _conv_fwd(p, w, main, *, name):
    S = p.shape[0]
    tc = LANES
    nb = main // tc

    def body(b_ref, c_ref, u_ref, w_ref, y_ref):
        gv = c_ref[...].astype(F32) * u_ref[...].astype(F32)
        g1, g2 = _conv_taps(gv, S)
        cv = w_ref[0:1, :] * g2 + w_ref[1:2, :] * g1 + w_ref[2:3, :] * gv
        y_ref[...] = (b_ref[...].astype(F32) * cv).astype(BF16)

    col = lambda off: pl.BlockSpec((S, tc), lambda j: (0, off + j))
    return pl.pallas_call(
        body, name=name, out_shape=jax.ShapeDtypeStruct((S, p.shape[1] - 2 * main), BF16), grid=(nb,),
        in_specs=[col(0), col(nb), col(2 * nb), pl.BlockSpec((3, tc), lambda j: (0, j))],
        out_specs=pl.BlockSpec((S, tc), lambda j: (0, j)), compiler_params=_cparams(("parallel",)),
    )(p, p, p, w)


def _conv_bwd(p, w, dy, main, *, name):
    S = p.shape[0]
    tc = LANES
    nb = main // tc

    def body(b_ref, c_ref, u_ref, w_ref, dy_ref, db_ref, dc_ref, du_ref, dw_ref):
        cvv, uv = c_ref[...].astype(F32), u_ref[...].astype(F32)
        gv = cvv * uv
        g1, g2 = _conv_taps(gv, S)
        w0, w1, w2 = w_ref[0:1, :], w_ref[1:2, :], w_ref[2:3, :]
        dyv = dy_ref[...].astype(F32)
        db_ref[...] = (dyv * (w0 * g2 + w1 * g1 + w2 * gv)).astype(BF16)
        dcv = dyv * b_ref[...].astype(F32)
        t = lax.broadcasted_iota(jnp.int32, dcv.shape, 0)
        n1 = jnp.where(t <= S - 2, pltpu.roll(dcv, S - 1, 0), 0.0)
        n2 = jnp.where(t <= S - 3, pltpu.roll(dcv, S - 2, 0), 0.0)
        dg = w2 * dcv + w1 * n1 + w0 * n2
        dc_ref[...] = (dg * uv).astype(BF16)
        du_ref[...] = (dg * cvv).astype(BF16)
        dw_ref[0:1, :] = jnp.sum(dcv * g2, axis=0, keepdims=True)
        dw_ref[1:2, :] = jnp.sum(dcv * g1, axis=0, keepdims=True)
        dw_ref[2:3, :] = jnp.sum(dcv * gv, axis=0, keepdims=True)

    col = lambda off: pl.BlockSpec((S, tc), lambda j: (0, off + j))
    out = jax.ShapeDtypeStruct((S, main), BF16)
    return pl.pallas_call(
        body, name=name, out_shape=(out, out, out, jax.ShapeDtypeStruct((3, main), F32)), grid=(nb,),
        in_specs=[col(0), col(nb), col(2 * nb), pl.BlockSpec((3, tc), lambda j: (0, j)), col(0)],
        out_specs=(col(0), col(0), col(0), pl.BlockSpec((3, tc), lambda j: (0, j))),
        compiler_params=_cparams(("parallel",)),
    )(p, p, p, w, dy)


def _head_mask(width, h):
    lane = lax.broadcasted_iota(jnp.int32, (1, width), 1)
    return jnp.logical_and(lane >= h * HEAD_DIM, lane < (h + 1) * HEAD_DIM)


_NT = (((1,), (1,)), ((), ()))
_NN = (((1,), (0,)), ((), ()))
_TN = (((0,), (0,)), ((), ()))


def _dot(a, b, dims):
    return lax.dot_general(a, b, dims, preferred_element_type=F32)


def _mem_probs(qh, kv):
    s = _dot(qh, kv, _NT) * (1.0 / math.sqrt(HEAD_DIM))
    s = s - jnp.max(s, axis=-1, keepdims=True)
    e = jnp.exp(s)
    return e / jnp.sum(e, axis=-1, keepdims=True)


def _memattn_fwd(p, qblk, mkv, into, *, name):
    S = p.shape[0]
    M = mkv.shape[0]
    W = MEM_WIDTH
    tq = _pick(S, (512, 256))
    last = into.shape[1] // W - 1

    def body(q_ref, k_ref, v_ref, _, o_ref):
        q = q_ref[...].astype(BF16)
        kv, vv = k_ref[...], v_ref[...]
        out = jnp.zeros((tq, W), F32)
        for h in range(MEM_HEADS):
            m = _head_mask(W, h)
            pr = _mem_probs(jnp.where(m, q, jnp.zeros_like(q)), kv)
            out = jnp.where(m, _dot(pr.astype(BF16), vv, _NN), out)
        o_ref[...] = out.astype(BF16)

    return pl.pallas_call(
        body, name=name, out_shape=jax.ShapeDtypeStruct(into.shape, BF16), grid=(S // tq,),
        in_specs=[pl.BlockSpec((tq, W), lambda i: (i, qblk)), pl.BlockSpec((M, W), lambda i: (0, 0)),
                  pl.BlockSpec((M, W), lambda i: (0, 1)), ANY],
        out_specs=pl.BlockSpec((tq, W), lambda i: (i, last)), input_output_aliases={3: 0},
        compiler_params=_cparams(("parallel",)),
    )(p, mkv, mkv, into)


def _memattn_bwd(p, qblk, mkv, dy, dyblk, *, name):
    S = p.shape[0]
    M = mkv.shape[0]
    W = MEM_WIDTH
    tq = _pick(S, (512, 256))
    scale = 1.0 / math.sqrt(HEAD_DIM)

    def body(q_ref, k_ref, v_ref, do_ref, dq_ref, dkv_ref, dk_acc, dv_acc):
        q = q_ref[...].astype(BF16)
        do = do_ref[...].astype(BF16)
        kv, vv = k_ref[...], v_ref[...]
        dq = jnp.zeros((tq, W), F32)
        dk = jnp.zeros((M, W), F32)
        dv = jnp.zeros((M, W), F32)
        for h in range(MEM_HEADS):
            m = _head_mask(W, h)
            qh = jnp.where(m, q, jnp.zeros_like(q))
            doh = jnp.where(m, do, jnp.zeros_like(do))
            pr = _mem_probs(qh, kv)
            dpr = _dot(doh, vv, _NT)
            ds = (pr * (dpr - jnp.sum(dpr * pr, axis=-1, keepdims=True)) * scale).astype(BF16)
            dq = jnp.where(m, _dot(ds, kv, _NN), dq)
            dk = dk + _dot(ds, qh, _TN)
            dv = dv + _dot(pr.astype(BF16), doh, _TN)
        dq_ref[...] = dq.astype(BF16)
        i = pl.program_id(0)

        @pl.when(i == 0)
        def _():
            dk_acc[...] = dk
            dv_acc[...] = dv

        @pl.when(i > 0)
        def _():
            dk_acc[...] += dk
            dv_acc[...] += dv

        @pl.when(i == S // tq - 1)
        def _():
            dkv_ref[:, :W] = dk_acc[...].astype(BF16)
            dkv_ref[:, W:] = dv_acc[...].astype(BF16)

    kspec = lambda c: pl.BlockSpec((M, W), lambda i: (0, c))
    return pl.pallas_call(
        body, name=name,
        out_shape=(jax.ShapeDtypeStruct((S, W), BF16), jax.ShapeDtypeStruct((M, 2 * W), BF16)),
        grid=(S // tq,),
        in_specs=[pl.BlockSpec((tq, W), lambda i: (i, qblk)), kspec(0), kspec(1), pl.BlockSpec((tq, W), lambda i: (i, dyblk))],
        out_specs=(pl.BlockSpec((tq, W), lambda i: (i, 0)), pl.BlockSpec((M, 2 * W), lambda i: (0, 0))),
        scratch_shapes=[pltpu.VMEM((M, W), F32), pltpu.VMEM((M, W), F32)],
        compiler_params=_cparams(("arbitrary",)),
    )(p, mkv, mkv, dy)


SB_TQ = 256
SB_CLAMP = 80.0
SB_DEAD = 110.0


SB_CHUNK = 64


def _by_rows(fn, *arrays):
    rows = next(a for a in arrays if a is not None).shape[0]
    outs = [fn(*[None if a is None else a[r0:r0 + SB_CHUNK] for a in arrays]) for r0 in range(0, rows, SB_CHUNK)]
    return tuple(jnp.concatenate(col, axis=0) for col in zip(*outs))


def _sb_scores(qh, kb, causal):
    def chain(z, mask):
        z = jnp.clip(z, -SB_CLAMP, SB_CLAMP)
        w = 1.0 + jnp.exp(z)
        sp = jnp.log(w)
        zs = z - sp
        if mask is not None:
            sp = jnp.where(mask, sp, 0.0)
            zs = jnp.where(mask, zs, -1e30)
            w = jnp.where(mask, w, 1.0)
        return zs, sp.astype(BF16), jnp.sum(sp, axis=1, keepdims=True), w

    return _by_rows(chain, _dot(qh, kb, _NT), causal)


def _sb_weights(zs, spb, tri, carry):
    return _by_rows(lambda zs_c, t_c, c_c: (jnp.exp(zs_c - (t_c + c_c)).astype(BF16),), zs, _dot(spb, tri, _NN), carry)[0]


def _sb_live(carry):
    return jnp.min(carry) <= SB_DEAD


def _stack_heads(v, m0):
    zero = jnp.zeros_like(v)
    return jnp.concatenate([jnp.where(m0, v, zero), jnp.where(m0, zero, v)], axis=0)


def _stacked_causal(tq):
    r = lax.broadcasted_iota(jnp.int32, (2 * tq, tq), 0)
    c = lax.broadcasted_iota(jnp.int32, (2 * tq, tq), 1)
    return c < jnp.where(r >= tq, r - tq, r)


def _sb_fwd(p, kv, heads, *, name):
    S = p.shape[0]
    tq = SB_TQ
    npair = heads // 2

    def body(q_ref, k_ref, v_ref, o_ref, o32_ref, carry_ref, acc_ref):
        qi = pl.program_id(1)
        r = lax.broadcasted_iota(jnp.int32, (tq, tq), 0)
        c = lax.broadcasted_iota(jnp.int32, (tq, tq), 1)
        tri = (r > c).astype(BF16)
        causal = _stacked_causal(tq)
        m0 = _head_mask(LANES, 0)
        qh = _stack_heads(q_ref[...] * jnp.asarray(1.0 / math.sqrt(HEAD_DIM), BF16), m0)

        def block(j, carry, mask):
            off = pl.multiple_of(j * tq, tq)
            kb = k_ref[pl.ds(off, tq), :]
            vb = v_ref[pl.ds(off, tq), :]
            zs, spb, sp_sum, _ = _sb_scores(qh, kb, mask)
            return carry + sp_sum, _dot(_sb_weights(zs, spb, tri, carry), vb, _NN)

        def two_blocks(j, carry, mask):
            carry, first = block(j, carry, mask)
            carry, second = block(j - 1, carry, None)
            return carry, first + second

        def keep(carry, added, fresh=False):
            carry_ref[...] = carry
            acc_ref[...] = added if fresh else acc_ref[...] + added

        zero = jnp.zeros((2 * tq, 1), F32)

        @pl.when(qi >= 1)
        def _():
            keep(*two_blocks(qi, zero, causal), fresh=True)

        @pl.when(qi == 0)
        def _():
            keep(*block(qi, zero, causal), fresh=True)

        left = jnp.maximum(qi - 1, 0)
        odd = left % 2

        @pl.when(jnp.logical_and(odd == 1, _sb_live(carry_ref[...])))
        def _():
            keep(*block(qi - 2, carry_ref[...], None))

        def pair(s):
            carry, added = two_blocks(qi - 2 - odd - 2 * s[0], carry_ref[...], None)
            keep(carry, added)
            return s[0] + 1, _sb_live(carry)

        lax.while_loop(lambda s: jnp.logical_and(s[0] < left // 2, s[1]), pair, (jnp.int32(0), _sb_live(carry_ref[...])))
        acc = acc_ref[...]
        out = jnp.where(m0, acc[:tq], acc[tq:])
        o_ref[...] = out.astype(BF16)
        o32_ref[...] = out

    W = heads * HEAD_DIM
    qspec = pl.BlockSpec((tq, LANES), lambda hp, i: (i, hp))
    return pl.pallas_call(
        body, name=name, out_shape=(jax.ShapeDtypeStruct(p.shape, BF16), jax.ShapeDtypeStruct((S, W), F32)), grid=(npair, S // tq),
        in_specs=[qspec, pl.BlockSpec((S, LANES), lambda hp, i: (0, hp)), pl.BlockSpec((S, LANES), lambda hp, i: (0, npair + hp))],
        out_specs=(qspec, qspec), scratch_shapes=[pltpu.VMEM((2 * tq, 1), F32), pltpu.VMEM((2 * tq, LANES), F32)],
        compiler_params=_cparams(("parallel", "arbitrary")),
    )(p, kv, kv)


def _sb_bwd(p, kv, o32, dy, heads, dk_in, dv_in, *, name):
    S = p.shape[0]
    tq = SB_TQ
    npair = heads // 2
    has_in = dk_in is not None
    scale = 1.0 / math.sqrt(HEAD_DIM)

    def body(*refs):
        q_ref, k_ref, v_ref, o_ref, do_ref = refs[:5]
        dq_ref, dk_ref, dv_ref, carry_ref, gcarry_ref, acc_ref = refs[5 + 2 * int(has_in):]
        qi = pl.program_id(1)

        @pl.when(qi == 0)
        def _():
            if has_in:
                dk_ref[...] = refs[5][...]
                dv_ref[...] = refs[6][...]
            else:
                dk_ref[...] = jnp.zeros_like(dk_ref)
                dv_ref[...] = jnp.zeros_like(dv_ref)

        r = lax.broadcasted_iota(jnp.int32, (tq, tq), 0)
        c = lax.broadcasted_iota(jnp.int32, (tq, tq), 1)
        tri = (r > c).astype(BF16)
        tri_low = (r < c).astype(BF16)
        causal = _stacked_causal(tq)
        m0 = _head_mask(LANES, 0)
        qh = _stack_heads(q_ref[...] * jnp.asarray(scale, BF16), m0)
        do = do_ref[...]
        doh = _stack_heads(do, m0)
        dov = do.astype(F32) * o_ref[...]
        dsum = jnp.concatenate([jnp.sum(jnp.where(m0, dov, 0.0), axis=1, keepdims=True),
                                jnp.sum(jnp.where(m0, 0.0, dov), axis=1, keepdims=True)], axis=0)

        def block(j, carry, gcarry, mask):
            off = pl.multiple_of(j * tq, tq)
            kb = k_ref[pl.ds(off, tq), :]
            vb = v_ref[pl.ds(off, tq), :]
            zs, spb, sp_sum, w = _sb_scores(qh, kb, mask)
            ab = _sb_weights(zs, spb, tri, carry)

            def grads(ab_c, da_c):
                g = ab_c.astype(F32) * da_c
                return g, g.astype(BF16), jnp.sum(g, axis=1, keepdims=True)

            g, gb, g_sum = _by_rows(grads, ab, _dot(doh, vb, _NT))
            gcarry = gcarry + g_sum

            def logit_grads(g_c, w_c, low_c, left_c):
                rinv = 1.0 / w_c
                return ((g_c * rinv - (left_c + low_c) * (1.0 - rinv)).astype(BF16),)

            dzs = _by_rows(logit_grads, g, w, _dot(gb, tri_low, _NN), dsum - gcarry)[0]
            dk_ref[pl.ds(off, tq), :] += _dot(dzs, qh, _TN)
            dv_ref[pl.ds(off, tq), :] += _dot(ab, doh, _TN)
            return carry + sp_sum, gcarry, _dot(dzs, kb, _NN)

        def two_blocks(j, carry, gcarry, mask):
            carry, gcarry, first = block(j, carry, gcarry, mask)
            carry, gcarry, second = block(j - 1, carry, gcarry, None)
            return carry, gcarry, first + second

        def keep(carry, gcarry, added, fresh=False):
            carry_ref[...] = carry
            gcarry_ref[...] = gcarry
            acc_ref[...] = added if fresh else acc_ref[...] + added

        zero = jnp.zeros((2 * tq, 1), F32)

        @pl.when(qi >= 1)
        def _():
            keep(*two_blocks(qi, zero, zero, causal), fresh=True)

        @pl.when(qi == 0)
        def _():
            keep(*block(qi, zero, zero, causal), fresh=True)

        left = jnp.maximum(qi - 1, 0)
        odd = left % 2

        @pl.when(jnp.logical_and(odd == 1, _sb_live(carry_ref[...])))
        def _():
            keep(*block(qi - 2, carry_ref[...], gcarry_ref[...], None))

        def pair(s):
            carry, gcarry, added = two_blocks(qi - 2 - odd - 2 * s[0], carry_ref[...], gcarry_ref[...], None)
            keep(carry, gcarry, added)
            return s[0] + 1, _sb_live(carry)

        lax.while_loop(lambda s: jnp.logical_and(s[0] < left // 2, s[1]), pair, (jnp.int32(0), _sb_live(carry_ref[...])))
        acc = acc_ref[...]
        dq_ref[...] = (jnp.where(m0, acc[:tq], acc[tq:]) * scale).astype(BF16)

    W = heads * HEAD_DIM
    qspec = pl.BlockSpec((tq, LANES), lambda hp, i: (i, hp))
    seq = lambda off: pl.BlockSpec((S, LANES), lambda hp, i: (0, off + hp))
    ops = [p, kv, kv, o32, dy] + ([dk_in, dv_in] if has_in else [])
    return pl.pallas_call(
        body, name=name,
        out_shape=(jax.ShapeDtypeStruct((S, W), BF16), jax.ShapeDtypeStruct((S, W), F32), jax.ShapeDtypeStruct((S, W), F32)),
        grid=(npair, S // tq),
        in_specs=[qspec, seq(0), seq(npair), qspec, qspec] + ([seq(0), seq(0)] if has_in else []),
        out_specs=(qspec, seq(0), seq(0)),
        scratch_shapes=[pltpu.VMEM((2 * tq, 1), F32), pltpu.VMEM((2 * tq, 1), F32), pltpu.VMEM((2 * tq, LANES), F32)],
        compiler_params=_cparams(("parallel", "arbitrary")),
    )(*ops)


def _ffn_up(h, wg, wu, *, name, ride=None):
    S, D = h.shape
    F = wg.shape[0]
    tm = _pick(S, (512, 256))
    tn = _pick(F, (1408, 1024, 512, 256, 128))

    def body(h_ref, g_ref, u_ref, act_ref, silu_ref, uds_ref):
        hv = h_ref[...]
        g = _dot(hv, g_ref[...], _NT)
        u = _dot(hv, u_ref[...], _NT)
        s = jax.nn.sigmoid(g)
        silu = g * s
        act_ref[...] = (silu * u).astype(BF16)
        silu_ref[...] = silu.astype(BF16)
        uds_ref[...] = (u * (s + silu * (1.0 - s))).astype(BF16)

    wspec = pl.BlockSpec((tn, D), lambda j, i: (j, 0))
    ospec = pl.BlockSpec((tm, tn), lambda j, i: (i, j))
    out = jax.ShapeDtypeStruct((S, F), BF16)
    grid = (F // tn, S // tm)

    def edges():
        j, i = pl.program_id(0), pl.program_id(1)
        return jnp.logical_and(j == 0, i == 0), jnp.logical_and(j == grid[0] - 1, i == grid[1] - 1)

    return _ride_call(
        body, ride, edges, name=name, out_shape=(out, out, out), grid=grid,
        in_specs=[pl.BlockSpec((tm, D), lambda j, i: (i, 0)), wspec, wspec], out_specs=(ospec, ospec, ospec),
        scratch_shapes=[], compiler_params=_cparams(("parallel", "parallel")), operands=[h, wg, wu])


def _ffn_down_bwd(dx, wd, silu, uds, *, name, ride=None):
    S, D = dx.shape
    F = wd.shape[0]
    tm = _pick(S, (512, 256))
    tn = _pick(F, (1408, 1024, 512, 256, 128))

    def body(dx_ref, w_ref, silu_ref, uds_ref, dg_ref, du_ref):
        da = _dot(dx_ref[...], w_ref[...], _NT)
        dg_ref[...] = (da * uds_ref[...].astype(F32)).astype(BF16)
        du_ref[...] = (da * silu_ref[...].astype(F32)).astype(BF16)

    ospec = pl.BlockSpec((tm, tn), lambda j, i: (i, j))
    out = jax.ShapeDtypeStruct((S, F), BF16)
    grid = (F // tn, S // tm)

    def edges():
        j, i = pl.program_id(0), pl.program_id(1)
        return jnp.logical_and(j == 0, i == 0), jnp.logical_and(j == grid[0] - 1, i == grid[1] - 1)

    return _ride_call(
        body, ride, edges, name=name, out_shape=(out, out), grid=grid,
        in_specs=[pl.BlockSpec((tm, D), lambda j, i: (i, 0)), pl.BlockSpec((tn, D), lambda j, i: (j, 0)), ospec, ospec],
        out_specs=(ospec, ospec), scratch_shapes=[], compiler_params=_cparams(("parallel", "parallel")),
        operands=[dx, wd, silu, uds])


def _adamw(w, g, m, v, *, name):
    R, C = w.shape
    tr = R
    for cand in (1024, 512, 256, 128, 64, 32, 16, 8):
        if R % cand == 0 and cand * C * 4 <= (1 << 20):
            tr = cand
            break
    bc1 = 1.0 - ADAM_B1 ** ADAM_STEP
    bc2 = 1.0 - ADAM_B2 ** ADAM_STEP

    def body(w_ref, g_ref, m_ref, v_ref, d_ref, nm_ref, nv_ref):
        gv = g_ref[...]
        nm = ADAM_B1 * m_ref[...] + (1.0 - ADAM_B1) * gv
        nv = ADAM_B2 * v_ref[...] + (1.0 - ADAM_B2) * (gv * gv)
        nm_ref[...] = nm
        nv_ref[...] = nv
        d_ref[...] = -ADAM_LR * ((nm / bc1) / (jnp.sqrt(nv / bc2) + ADAM_EPS) + ADAM_WD * w_ref[...])

    blk = pl.BlockSpec((tr, C), lambda i: (i, 0))
    out = jax.ShapeDtypeStruct((R, C), F32)
    return pl.pallas_call(body, name=name, out_shape=(out, out, out), grid=(R // tr,), in_specs=[blk] * 4,
                          out_specs=(blk, blk, blk), compiler_params=_cparams(("parallel",)))(w, g, m, v)


def _place():
    x, y, c = lax.axis_index("x"), lax.axis_index("y"), lax.axis_index("c")
    return x, y, c


def _all_gather_weights(shards, *, name):
    n = len(shards)

    def body(*refs):
        sh, full = refs[:n], refs[n:2 * n]
        send_sems, recv_sems, local_sems = refs[2 * n:]
        x, y, c = _place()
        me, sibling = (x, y, c), (x, y, 1 - c)
        chips = [(1 - x, y), (x, 1 - y), (1 - x, 1 - y)]

        def rows(t, px, py, pc):
            r = sh[t].shape[1]
            return full[t].at[:, pl.ds(pl.multiple_of((4 * px + 2 * py + pc) * r, BF16_ROWS), r), :]

        def copy(t, k, block, to, src=None):
            return pltpu.make_async_remote_copy(
                src_ref=rows(t, *block) if src is None else src, dst_ref=rows(t, *block),
                send_sem=send_sems.at[7 * t + k], recv_sem=recv_sems.at[7 * t + k], device_id=to, device_id_type=MESH)

        started = []
        for t in range(n):
            mine = pltpu.make_async_copy(sh[t], rows(t, *me), local_sems.at[t])
            mine.start()
            started.append(mine)
        sends = []
        for t in range(n):
            first = [copy(t, 0, me, sibling, src=sh[t])]
            first += [copy(t, 1 + j, me, (*chip, c), src=sh[t]) for j, chip in enumerate(chips)]
            for cp in first:
                cp.start()
            sends += first
        for t in range(n):
            for j, chip in enumerate(chips):
                copy(t, 1 + j, (*chip, c), me).wait_recv()
                fwd = copy(t, 4 + j, (*chip, c), sibling)
                fwd.start()
                sends.append(fwd)
        for t in range(n):
            copy(t, 0, sibling, me).wait_recv()
            for j, chip in enumerate(chips):
                copy(t, 4 + j, (*chip, 1 - c), me).wait_recv()
        for cp in sends:
            cp.wait_send()
        for cp in started:
            cp.wait()

    out_shape = [jax.ShapeDtypeStruct((s.shape[0], N_DEV * s.shape[1], s.shape[2]), s.dtype) for s in shards]
    return pl.pallas_call(
        body, name=name, out_shape=out_shape, in_specs=[ANY] * n, out_specs=[ANY] * n,
        scratch_shapes=[pltpu.SemaphoreType.DMA((7 * n,)), pltpu.SemaphoreType.DMA((7 * n,)), pltpu.SemaphoreType.DMA((n,))],
    )(*shards)


def _whole(ref_a, ref_b, send_sem, recv_sem, me):
    return pltpu.make_async_remote_copy(src_ref=ref_a, dst_ref=ref_b, send_sem=send_sem, recv_sem=recv_sem,
                                        device_id=me, device_id_type=MESH)


def _sibling_ride(grads):
    n = len(grads)
    lands = [lax.empty((4, s.shape[0], s.shape[3], s.shape[4]), s.dtype) for s in grads]

    def start(g, land, sems):
        x, y, c = _place()
        for t in range(n):
            for k in range(4):
                pltpu.make_async_remote_copy(
                    src_ref=g[t].at[:, k, 1 - c], dst_ref=land[t].at[k], send_sem=sems[0].at[t], recv_sem=sems[1].at[t],
                    device_id=(x, y, 1 - c), device_id_type=MESH).start()

    def wait(g, land, sems):
        x, y, c = _place()
        for t in range(n):
            w = _whole(land[t], land[t], sems[0].at[t], sems[1].at[t], (x, y, c))
            w.wait_send()
            w.wait_recv()

    return _Ride(grads, lands, start, wait, [(n,), (n,)])


def _carry_alone(ride, *, name):
    def body(o_ref):
        o_ref[...] = jnp.zeros_like(o_ref)

    one = lambda: (pl.program_id(0) == 0, pl.program_id(0) == 0)
    _, landed = _ride_call(body, ride, one, name=name, out_shape=[jax.ShapeDtypeStruct((8, LANES), F32)], grid=(1,),
                           in_specs=[], out_specs=[pl.BlockSpec((8, LANES), lambda i: (0, 0))], scratch_shapes=[],
                           compiler_params=_cparams(("arbitrary",)), operands=[])
    return landed


def _chips_ride(sums):
    n = len(sums)
    lands = [lax.empty((3,) + s.shape[1:], s.dtype) for s in sums]

    def start(s, land, sems):
        x, y, c = _place()
        for t in range(n):
            for j, (px, py) in enumerate([(1 - x, y), (x, 1 - y), (1 - x, 1 - y)]):
                pltpu.make_async_remote_copy(
                    src_ref=s[t].at[2 * px + py], dst_ref=land[t].at[j], send_sem=sems[0].at[t], recv_sem=sems[1].at[t],
                    device_id=(px, py, c), device_id_type=MESH).start()

    def wait(s, land, sems):
        x, y, c = _place()
        for t in range(n):
            w = _whole(land[t], land[t], sems[0].at[t], sems[1].at[t], (x, y, c))
            w.wait_send()
            w.wait_recv()

    return _Ride(sums, lands, start, wait, [(n,), (n,)])


def _gather_ride_1(shards):
    n = len(shards)
    layer = [l for _, l in shards]
    fulls = [lax.empty((N_DEV * s.shape[1], s.shape[2]), s.dtype) for s, _ in shards]

    def rows(full, r, px, py, pc):
        return full.at[pl.ds(pl.multiple_of((4 * px + 2 * py + pc) * r, BF16_ROWS), r), :]

    def start(sh, full, sems):
        x, y, c = _place()
        for t in range(n):
            shard = sh[t].at[layer[t]]
            mine = rows(full[t], shard.shape[0], x, y, c)
            pltpu.make_async_copy(shard, mine, sems[2].at[t]).start()
            for peer in [(x, y, 1 - c), (1 - x, y, c), (x, 1 - y, c), (1 - x, 1 - y, c)]:
                pltpu.make_async_remote_copy(src_ref=shard, dst_ref=mine, send_sem=sems[0].at[t], recv_sem=sems[1].at[t],
                                             device_id=peer, device_id_type=MESH).start()

    def wait(sh, full, sems):
        x, y, c = _place()
        for t in range(n):
            shard = sh[t].at[layer[t]]
            r = shard.shape[0]
            pltpu.make_async_copy(shard, rows(full[t], r, x, y, c), sems[2].at[t]).wait()
            four = full[t].at[pl.ds(0, 4 * r), :]
            w = _whole(four, four, sems[0].at[t], sems[1].at[t], (x, y, c))
            w.wait_send()
            w.wait_recv()

    return _Ride([s for s, _ in shards], fulls, start, wait, [(n,), (n,), (n,)])


def _gather_ride_2(fulls):
    n = len(fulls)

    def start(_, full, sems):
        x, y, c = _place()
        for t in range(n):
            r = full[t].shape[0] // N_DEV
            for px, py in [(1 - x, y), (x, 1 - y), (1 - x, 1 - y)]:
                block = full[t].at[pl.ds(pl.multiple_of((4 * px + 2 * py + c) * r, BF16_ROWS), r), :]
                pltpu.make_async_remote_copy(src_ref=block, dst_ref=block, send_sem=sems[0].at[t], recv_sem=sems[1].at[t],
                                             device_id=(x, y, 1 - c), device_id_type=MESH).start()

    def wait(_, full, sems):
        x, y, c = _place()
        for t in range(n):
            three = full[t].at[pl.ds(0, 3 * (full[t].shape[0] // N_DEV)), :]
            w = _whole(three, three, sems[0].at[t], sems[1].at[t], (x, y, c))
            w.wait_send()
            w.wait_recv()

    return _Ride([], fulls, start, wait, [(n,), (n,)])


def _join_rides(rides):
    rides = [r for r in rides if r is not None]
    if len(rides) <= 1:
        return rides[0] if rides else None

    def parts(src, dst, sems):
        so = do = mo = 0
        for r in rides:
            yield r, src[so:so + len(r.srcs)], dst[do:do + len(r.dsts)], sems[mo:mo + len(r.sems)]
            so, do, mo = so + len(r.srcs), do + len(r.dsts), mo + len(r.sems)

    def start(src, dst, sems):
        for r, s, d, m in parts(src, dst, sems):
            r.start(s, d, m)

    def wait(src, dst, sems):
        for r, s, d, m in parts(src, dst, sems):
            r.wait(s, d, m)

    return _Ride([a for r in rides for a in r.srcs], [a for r in rides for a in r.dsts], start, wait,
                 [m for r in rides for m in r.sems])


def _chip_sum(g, land, core, *, name):
    L, _, _, r, C = g.shape

    def body(core_ref, g_ref, l_ref, o_ref):
        o_ref[...] = (g_ref[...].astype(F32) + l_ref[...].astype(F32)).astype(BF16)

    grid_spec = pltpu.PrefetchScalarGridSpec(
        num_scalar_prefetch=1, grid=(4, L),
        in_specs=[pl.BlockSpec((None, None, None, r, C), lambda k, l, core_ref: (l, k, core_ref[0], 0, 0)),
                  pl.BlockSpec((None, None, r, C), lambda k, l, core_ref: (k, l, 0, 0))],
        out_specs=pl.BlockSpec((None, None, r, C), lambda k, l, core_ref: (k, l, 0, 0)))
    return pl.pallas_call(body, name=name, out_shape=jax.ShapeDtypeStruct((4, L, r, C), BF16), grid_spec=grid_spec,
                          compiler_params=_cparams(("parallel", "parallel")))(core, g, land)


def _final_sum(sums, land, chip, into, layer, *, name):
    _, _, r, C = sums.shape

    def body(chip_ref, s_ref, a_ref, b_ref, c_ref, _, o_ref):
        o_ref[...] = ((s_ref[...].astype(F32) + a_ref[...].astype(F32)) + b_ref[...].astype(F32)) + c_ref[...].astype(F32)

    slot = lambda j: pl.BlockSpec((None, None, r, C), lambda i, chip_ref: (j, 0, 0, 0))
    grid_spec = pltpu.PrefetchScalarGridSpec(
        num_scalar_prefetch=1, grid=(1,),
        in_specs=[pl.BlockSpec((None, None, r, C), lambda i, chip_ref: (chip_ref[0], 0, 0, 0)), slot(0), slot(1), slot(2), ANY],
        out_specs=pl.BlockSpec((None, r, C), lambda i, chip_ref: (layer, 0, 0)))
    return pl.pallas_call(body, name=name, out_shape=jax.ShapeDtypeStruct(into.shape, F32), grid_spec=grid_spec,
                          input_output_aliases={5: 0}, compiler_params=_cparams(("arbitrary",)))(chip, sums, land, land, land, into)


def _exchange(v, reduce, *, name):
    R, C = v.shape

    def body(v_ref, o_ref, *scratch):
        if reduce:
            buf, send_sems, recv_sems = scratch
        else:
            buf = o_ref
            send_sems, recv_sems = scratch
        x, y, c = _place()
        me = 4 * x + 2 * y + c
        buf[me] = v_ref[...]
        copies = []
        for k in range(1, N_DEV):
            kx, ky, kc = (k >> 2) & 1, (k >> 1) & 1, k & 1
            peer = (1 - x if kx else x, 1 - y if ky else y, 1 - c if kc else c)
            cp = pltpu.make_async_remote_copy(src_ref=v_ref, dst_ref=buf.at[me], send_sem=send_sems.at[k - 1],
                                              recv_sem=recv_sems.at[k - 1], device_id=peer, device_id_type=MESH)
            cp.start()
            copies.append(cp)
        for cp in copies:
            cp.wait_recv()
        for cp in copies:
            cp.wait_send()
        if reduce:
            acc = buf[0]
            for d in range(1, N_DEV):
                acc = acc + buf[d]
            o_ref[...] = acc

    sems = [pltpu.SemaphoreType.DMA((N_DEV - 1,)), pltpu.SemaphoreType.DMA((N_DEV - 1,))]
    vm = pl.BlockSpec(memory_space=pltpu.VMEM)
    if reduce:
        return pl.pallas_call(body, name=name, out_shape=jax.ShapeDtypeStruct((R, C), F32), in_specs=[vm], out_specs=vm,
                              scratch_shapes=[pltpu.VMEM((N_DEV, R, C), F32)] + sems)(v)
    return pl.pallas_call(body, name=name, out_shape=jax.ShapeDtypeStruct((N_DEV, R, C), F32), in_specs=[vm], out_specs=vm,
                          scratch_shapes=sems)(v)


def _local_step(x, mem, target, norms, conv_w, depth, n_a, get_w, next_ride, ride_done, grad_ready, grad_ride, grad_landed):
    S, D = x.shape
    main = D - MEM_WIDTH
    heads = main // HEAD_DIM
    row = lambda v: v.reshape(1, D)

    mem_n = _rmsnorm(mem, row(norms["mem_norm"]), name="mem_norm")
    saved = []
    kv = hk = x_kv = w_kv = None
    def carry_mm(*args, **kwargs):
        ride = next_ride()
        if ride is None:
            return _mm(*args, **kwargs)
        out, landed = _mm(*args, ride=ride, **kwargs)
        ride_done(landed)
        return out

    for i in range(depth):
        W = functools.partial(get_w, i)
        st = {"x": x}
        h = _rmsnorm(x, row(norms["mix_norm"][i]), name=f"mix_norm{i}")
        mkv = _mm(mem_n, W("mkv"), "nn", BF16, name=f"mkv{i}")
        if i < n_a:
            p = carry_mm(h, W("a"), "nt", BF16, name=f"a_in{i}")
            y_main = _conv_fwd(p, conv_w[i], main, name=f"conv{i}")
            qblk = 3 * main // MEM_WIDTH
        else:
            p = carry_mm(h, W("b"), "nn", BF16, name=f"b_in{i}")
            y_main, st["o32"] = _sb_fwd(p, kv, heads, name=f"sb{i}")
            qblk = main // MEM_WIDTH
        y = _memattn_fwd(p, qblk, mkv, y_main, name=f"memattn{i}")
        xm = carry_mm(y, W("o"), "nn", F32, residual=x, name=f"w_o{i}")
        h2 = _rmsnorm(xm, row(norms["ffn_norm"][i]), name=f"ffn_norm{i}")
        (act, silu, uds), landed = _ffn_up(h2, W("g"), W("u"), name=f"ffn_up{i}", ride=next_ride())
        ride_done(landed)
        x = carry_mm(act, W("d"), "nn", F32, residual=xm, name=f"w_down{i}")
        st.update(h=h, mkv=mkv, p=p, qblk=qblk, y=y, xm=xm, h2=h2, silu=silu, uds=uds, act=act)
        saved.append(st)
        if i == n_a - 1:
            x_kv, w_kv = x, W("kv")
            hk = _rmsnorm(x, row(norms["kv_norm"]), name="kv_norm")
            kv = _mm(hk, w_kv, "nt", BF16, name="w_kv")

    dx, dxb, dg_final, loss = _loss_head(x, row(norms["final_norm"]), target, name="loss_head")

    dg_mix, dg_ffn, dconv = [None] * depth, [None] * depth, [None] * n_a
    dmem_n = dk = dv = dg_kv = g_kv = None
    def carry_back(*args, **kwargs):
        ride = grad_ride()
        if ride is None:
            return _mm(*args, **kwargs)
        out, landed = _mm(*args, ride=ride, **kwargs)
        grad_landed(landed)
        return out

    def carry_back_norm(*args, **kwargs):
        outs, landed = _mm_norm_bwd(*args, ride=grad_ride(), **kwargs)
        grad_landed(landed)
        return outs

    for i in reversed(range(depth)):
        st = saved[i]
        W, key = functools.partial(get_w, i), dict(_layer_keys(i, n_a))
        (dgate, dup), landed = _ffn_down_bwd(dxb, W("d"), st["silu"], st["uds"], name=f"ffn_down_bwd{i}", ride=grad_ride())
        grad_landed(landed)
        grad_ready(("d", key["d"]), carry_back(st["act"], dxb, "tn", BF16, name=f"g_w_down{i}"))
        grad_ready(("g", key["g"]), carry_back(dgate, st["h2"], "tn", BF16, name=f"g_w_gate{i}"))
        grad_ready(("u", key["u"]), carry_back(dup, st["h2"], "tn", BF16, name=f"g_w_up{i}"))
        dx, dxb, dg_ffn[i] = carry_back_norm([(dgate, W("g")), (dup, W("u"))], "nn", st["xm"], row(norms["ffn_norm"][i]), dx,
                                             name=f"d_h2_{i}")
        dy = _mm(dxb, W("o"), "nt", BF16, name=f"d_y{i}")
        grad_ready(("o", key["o"]), _mm(st["y"], dxb, "tn", BF16, name=f"g_w_o{i}"))
        dqmem, dmkv = _memattn_bwd(st["p"], st["qblk"], st["mkv"], dy, main // MEM_WIDTH, name=f"memattn_bwd{i}")
        grad_ready(("mkv", key["mkv"]), _mm(mem_n, dmkv, "tn", BF16, name=f"g_w_mem_kv{i}"))
        dmem_n = _mm(dmkv, W("mkv"), "nt", F32, residual=dmem_n, name=f"d_mem_n{i}")
        if i < n_a:
            db, dc, du, dconv[i] = _conv_bwd(st["p"], conv_w[i], dy, main, name=f"conv_bwd{i}")
            dp = jnp.concatenate([db, dc, du, dqmem], axis=1)
            grad_ready(("a", key["a"]), carry_back(dp, st["h"], "tn", BF16, name=f"g_a_in{i}"))
            w_in, form = W("a"), "nn"
        else:
            dq, dk, dv = _sb_bwd(st["p"], kv, st["o32"], dy, heads, dk, dv, name=f"sb_bwd{i}")
            dp = jnp.concatenate([dq, dqmem], axis=1)
            grad_ready(("b", key["b"]), carry_back(st["h"], dp, "tn", BF16, name=f"g_b_in{i}"))
            w_in, form = W("b"), "nt"
        dx, dxb, dg_mix[i] = carry_back_norm([(dp, w_in)], form, st["x"], row(norms["mix_norm"][i]), dx, name=f"d_h{i}")
        if i == n_a:
            dkv = jnp.concatenate([dk, dv], axis=1).astype(BF16)
            grad_ready(("kv", 0), carry_back(dkv, hk, "tn", BF16, name="g_w_kv"))
            dx, dxb, dg_kv = carry_back_norm([(dkv, w_kv)], "nn", x_kv, row(norms["kv_norm"]), dx, name="d_hk")
    _, _, dg_mem = _rmsnorm_bwd(mem, row(norms["mem_norm"]), dmem_n, None, name="mem_norm_bwd")

    small = {"mix_norm": jnp.concatenate(dg_mix, axis=0), "ffn_norm": jnp.concatenate(dg_ffn, axis=0), "kv_norm": dg_kv[0],
             "mem_norm": dg_mem[0], "final_norm": dg_final[0], "conv_w": jnp.stack(dconv, axis=0)}
    return loss, dx, small


_COL_SHARDED = ("a", "kv", "g", "u")
_NAMES = {"a": "a_in", "kv": "w_kv_shared", "g": "w_gate", "u": "w_up", "b": "b_in", "o": "w_o", "d": "w_down", "mkv": "w_mem_kv"}
_ORDER = ("a", "kv", "g", "u", "d", "b", "o", "mkv")
_WEIGHTS = ("mix_norm", "a_in", "conv_w", "b_in", "kv_norm", "w_kv_shared", "w_mem_kv", "w_o", "ffn_norm", "w_gate", "w_up",
            "w_down", "mem_norm", "final_norm")


def _layer_keys(i, n_a):
    keys = [("a", i) if i < n_a else ("b", i - n_a), ("g", i), ("u", i), ("d", i), ("o", i), ("mkv", i)]
    return keys + [("kv", 0)] if i == n_a - 1 else keys


def _gather_groups(i, n_a):
    first, rest = _layer_keys(i, n_a)[0], dict(_layer_keys(i, n_a)[1:])
    small = [(k, rest[k]) for k in ("o", "mkv", "kv") if k in rest]
    return [[first], small, [("g", rest["g"]), ("u", rest["u"])], [("d", rest["d"])]]


def _canonical(key, w):
    w3 = w if w.ndim == 3 else w[None]
    if key in _COL_SHARDED:
        w3 = jnp.transpose(w3, (0, 2, 1))
    return w3


def _uncanonical(key, g3, like):
    if key in _COL_SHARDED:
        g3 = jnp.transpose(g3, (0, 2, 1))
    return g3.reshape(like.shape)


def _pad_rows(flat, C):
    n = flat.shape[0]
    rows = -(-n // C)
    return jnp.pad(flat, (0, rows * C - n)).reshape(rows, C)


def kernel(x, mem, mix_norm, a_in, conv_w, b_in, kv_norm, w_kv_shared, w_mem_kv, w_o, ffn_norm, w_gate, w_up, w_down, mem_norm, final_norm, loss_target, m_mix_norm, m_a_in, m_conv_w, m_b_in, m_kv_norm, m_w_kv_shared, m_w_mem_kv, m_w_o, m_ffn_norm, m_w_gate, m_w_up, m_w_down, m_mem_norm, m_final_norm, v_mix_norm, v_a_in, v_conv_w, v_b_in, v_kv_norm, v_w_kv_shared, v_w_mem_kv, v_w_o, v_ffn_norm, v_w_gate, v_w_up, v_w_down, v_mem_norm, v_final_norm):
    weights = dict(mix_norm=mix_norm, a_in=a_in, conv_w=conv_w, b_in=b_in, kv_norm=kv_norm, w_kv_shared=w_kv_shared,
                   w_mem_kv=w_mem_kv, w_o=w_o, ffn_norm=ffn_norm, w_gate=w_gate, w_up=w_up, w_down=w_down,
                   mem_norm=mem_norm, final_norm=final_norm)
    moments_m = dict(mix_norm=m_mix_norm, a_in=m_a_in, conv_w=m_conv_w, b_in=m_b_in, kv_norm=m_kv_norm,
                     w_kv_shared=m_w_kv_shared, w_mem_kv=m_w_mem_kv, w_o=m_w_o, ffn_norm=m_ffn_norm, w_gate=m_w_gate,
                     w_up=m_w_up, w_down=m_w_down, mem_norm=m_mem_norm, final_norm=m_final_norm)
    moments_v = dict(mix_norm=v_mix_norm, a_in=v_a_in, conv_w=v_conv_w, b_in=v_b_in, kv_norm=v_kv_norm,
                     w_kv_shared=v_w_kv_shared, w_mem_kv=v_w_mem_kv, w_o=v_w_o, ffn_norm=v_ffn_norm, w_gate=v_w_gate,
                     w_up=v_w_up, w_down=v_w_down, mem_norm=v_mem_norm, final_norm=v_final_norm)
    D = x.shape[-1]
    depth, n_a = w_o.shape[0], a_in.shape[0]
    xi, yi, ci = _place()
    me = 4 * xi + 2 * yi + ci
    core = ci.reshape(1).astype(jnp.int32)
    chip = (2 * xi + yi).reshape(1).astype(jnp.int32)

    cw_shape = conv_w.shape
    cw_rows = _pad_rows(conv_w.reshape(-1), D)
    cw_rows = jnp.pad(cw_rows, ((0, 8 - cw_rows.shape[0]), (0, 0)))
    cw_gathered = _exchange(cw_rows, False, name="gather_conv_w")
    n_cw = cw_shape[0] * cw_shape[1] * cw_shape[2]
    cw_all = cw_gathered.reshape(N_DEV, -1)[:, :n_cw].reshape((N_DEV,) + cw_shape)
    conv_full = jnp.transpose(cw_all, (1, 2, 0, 3)).reshape(cw_shape[0], cw_shape[1], N_DEV * cw_shape[2])

    shard3 = {k: _canonical(k, weights[_NAMES[k]]).astype(BF16) for k in _ORDER}
    keys0 = _layer_keys(0, n_a)
    fulls0 = _all_gather_weights([shard3[k][l][None] for k, l in keys0], name="all_gather_layer0")
    full = {kl: f[0] for kl, f in zip(keys0, fulls0)}

    groups = [grp for i in range(1, depth) for grp in _gather_groups(i, n_a)]
    carried, riding = [0], []

    def next_ride():
        n = carried[0]
        carried[0] += 1
        second = groups[n - 1] if 1 <= n <= len(groups) else []
        first = groups[n] if n < len(groups) else []
        if not second + first:
            return None
        riding.append(second + first)
        return _join_rides([_gather_ride_2([full[kl] for kl in second]) if second else None,
                            _gather_ride_1([(shard3[k], l) for k, l in first]) if first else None])

    def ride_done(landed):
        if landed:
            full.update(zip(riding.pop(), landed))

    def get_w(i, key):
        return full[(key, dict(_layer_keys(i, n_a))[key])]

    fresh, summed, reduced, travelling = [], [], {}, []

    def grad_ready(kl, g):
        fresh.append((kl, g.reshape(1, 4, 2, g.shape[0] // N_DEV, g.shape[1])))

    def grad_ride():
        if not fresh + summed:
            return None
        travelling.append((list(summed), list(fresh)))
        ride = _join_rides([_chips_ride([s for _, s in summed]) if summed else None,
                            _sibling_ride([g for _, g in fresh]) if fresh else None])
        summed.clear()
        fresh.clear()
        return ride

    def grad_landed(landed):
        if not landed:
            return
        between_chips, to_sibling = travelling.pop()
        for (kl, s), land in zip(between_chips, landed):
            reduced[kl] = (s, land)
        for (kl, g), land in zip(to_sibling, landed[len(between_chips):]):
            summed.append((kl, _chip_sum(g, land, core, name=f"chip_sum_{kl[0]}{kl[1]}")))

    norms = {k: weights[k] for k in ("mix_norm", "ffn_norm", "kv_norm", "mem_norm", "final_norm")}
    loss, grad_x, small = _local_step(x[0], mem[0], loss_target[0], norms, conv_full, depth, n_a, get_w, next_ride, ride_done,
                                      grad_ready, grad_ride, grad_landed)
    for tail in range(2):
        ride = grad_ride()
        if ride is not None:
            grad_landed(_carry_alone(ride, name=f"reduce_scatter_tail{tail}"))

    stacks = {k: lax.empty(shard3[k].shape, F32) for k in _ORDER}
    for (k, l), (s, land) in reduced.items():
        stacks[k] = _final_sum(s, land, chip, stacks[k], l, name=f"final_sum_{k}{l}")
    grads = {_NAMES[k]: _uncanonical(k, stacks[k], weights[_NAMES[k]]) for k in _ORDER}

    order = ("mix_norm", "ffn_norm", "kv_norm", "mem_norm", "final_norm", "conv_w")
    flat = jnp.concatenate([small[k].reshape(-1) for k in order] + [loss[0, :1]])
    n_flat = flat.shape[0]
    rows = _pad_rows(flat, D)
    rows = jnp.pad(rows, ((0, (-rows.shape[0]) % 8), (0, 0)))
    total = _exchange(rows, True, name="all_reduce_small").reshape(-1)[:n_flat]
    off = 0
    for k in order:
        n = small[k].size
        grads[k] = total[off:off + n].reshape(small[k].shape)
        off += n
    loss_total = total[off]
    grads["conv_w"] = lax.dynamic_slice_in_dim(grads["conv_w"], me * cw_shape[2], cw_shape[2], axis=2)

    deltas, new_m, new_v = {}, {}, {}
    for k in _WEIGHTS:
        w = weights[k]
        two = (lambda a: a.reshape(-1, a.shape[-1])) if w.ndim > 1 else (lambda a: a.reshape(1, -1))
        d, nm, nv = _adamw(two(w), two(grads[k]), two(moments_m[k]), two(moments_v[k]), name=f"adamw_{k}")
        deltas[k], new_m[k], new_v[k] = d.reshape(w.shape), nm.reshape(w.shape), nv.reshape(w.shape)

    return (loss_total, grad_x[None], *[grads[k] for k in _WEIGHTS], *[deltas[k] for k in _WEIGHTS],
            *[new_m[k] for k in _WEIGHTS], *[new_v[k] for k in _WEIGHTS])
```
